```python
import math
import jax, jax.numpy as jnp
from jax import lax
import numpy as np

D_MODEL = 1024
BATCH = 8
SEQ = 8192
DEPTH = 1

HEAD_DIM = 64
ROT_DIM = HEAD_DIM // 4
ROPE_THETA = 500000.0
NSA_HEADS = 8
NSA_KV_GROUPS = 2
NSA_HPG = NSA_HEADS // NSA_KV_GROUPS
CMP_BLOCK = 32
CMP_STRIDE = 16
CMP_HIDDEN = 256
SLC_BLOCK = 64
SLC_TOPK = 16
WINDOW = 512
NSA_QBLOCK = 64
FORCED_SCORE = 1e4
HGRN_HEADS = 4
HGRN_DK = 64
HGRN_DV = 64
HGRN_CHUNK = 64
MEM_HEADS = 4
MEM_TOKENS = 256
NSA_WIDTH = NSA_HEADS * HEAD_DIM
NSA_KV_WIDTH = NSA_KV_GROUPS * HEAD_DIM
HGRN_WIDTH = HGRN_HEADS * HGRN_DK
MEM_WIDTH = MEM_HEADS * HEAD_DIM
MIX_WIDTH = NSA_WIDTH + HGRN_WIDTH + MEM_WIDTH
IN_SIZES = (NSA_WIDTH, NSA_KV_WIDTH, NSA_KV_WIDTH, NSA_KV_WIDTH, NSA_KV_WIDTH, NSA_KV_WIDTH, NSA_KV_WIDTH,
            3 * NSA_HEADS, HGRN_WIDTH, HGRN_WIDTH, HGRN_WIDTH, HGRN_WIDTH, MEM_WIDTH)
IN_WIDTH = sum(IN_SIZES)
D_FF = 2816
EPS = 1e-6
MASK_VALUE = -1e30

kernel_name = "hybrid_nsa_hgrn2_memory_macaron"


def split_points():
    pts, acc = [], 0
    for s in IN_SIZES[:-1]:
        acc += s
        pts.append(acc)
    return pts


def rms_norm(x, gain):
    xf = x.astype(jnp.float32)
    y = xf * lax.rsqrt(jnp.mean(xf * xf, axis=-1, keepdims=True) + EPS)
    return (y * gain.astype(jnp.float32)).astype(x.dtype)


def swiglu(x, w_gate, w_up, w_down):
    return (jax.nn.silu(x @ w_gate) * (x @ w_up)) @ w_down


def masked_softmax(s, mask):
    s = jnp.where(mask, s.astype(jnp.float32), MASK_VALUE)
    p = jax.nn.softmax(s, axis=-1)
    return jnp.where(mask, p, 0.0)


def rope_tables(seq):
    pos = jnp.arange(seq, dtype=jnp.float32)
    inv = ROPE_THETA ** (-(jnp.arange(0, ROT_DIM, 2, dtype=jnp.float32) / ROT_DIM))
    ang = pos[:, None] * inv[None, :]
    return jnp.cos(ang), jnp.sin(ang)


def partial_rope(x, cos, sin):
    half = ROT_DIM // 2
    x1 = x[..., :half].astype(jnp.float32)
    x2 = x[..., half:ROT_DIM].astype(jnp.float32)
    c = cos[:, None, :]
    s = sin[:, None, :]
    r = jnp.concatenate([x1 * c - x2 * s, x2 * c + x1 * s], axis=-1).astype(x.dtype)
    return jnp.concatenate([r, x[..., ROT_DIM:]], axis=-1)


def compress_blocks(t, pos_emb, w1, w2):
    B, S, G, dh = t.shape
    n_cmp = (S - CMP_BLOCK) // CMP_STRIDE + 1
    idx = jnp.arange(n_cmp)[:, None] * CMP_STRIDE + jnp.arange(CMP_BLOCK)[None, :]
    blocks = t[:, idx] + pos_emb[:, None, :]
    blocks = blocks.transpose(0, 1, 3, 2, 4).reshape(B, n_cmp, G, CMP_BLOCK * dh)
    return jax.nn.silu(blocks @ w1) @ w2


def nsa_mixer(q, k_cmp, v_cmp, k_slc, v_slc, k_win, v_win, gates,
              cmp_pos_k, cmp_w1_k, cmp_w2_k, cmp_pos_v, cmp_w1_v, cmp_w2_v,
              q_gain, k_gain, cos, sin):
    B, S = q.shape[:2]
    G, HPG, dh, QB = NSA_KV_GROUPS, NSA_HPG, HEAD_DIM, NSA_QBLOCK
    scale = HEAD_DIM ** -0.5
    q = partial_rope(rms_norm(q, q_gain), cos, sin)
    k_slc = partial_rope(rms_norm(k_slc, k_gain), cos, sin)
    k_win = partial_rope(rms_norm(k_win, k_gain), cos, sin)
    kc = rms_norm(compress_blocks(partial_rope(k_cmp, cos, sin), cmp_pos_k, cmp_w1_k, cmp_w2_k), k_gain)
    vc = compress_blocks(v_cmp, cmp_pos_v, cmp_w1_v, cmp_w2_v)
    n_cmp = kc.shape[1]
    kc = kc.transpose(0, 2, 1, 3)
    vc = vc.transpose(0, 2, 1, 3)
    cmp_start = jnp.arange(n_cmp) * CMP_STRIDE
    cmp_end = cmp_start + CMP_BLOCK - 1
    n_slc = S // SLC_BLOCK
    top_k = min(SLC_TOPK, n_slc)
    slc_start = jnp.arange(n_slc) * SLC_BLOCK
    overlap = ((cmp_start[:, None] < slc_start[None, :] + SLC_BLOCK)
               & (cmp_start[:, None] + CMP_BLOCK > slc_start[None, :])).astype(jnp.float32)
    ks_blocks = k_slc.reshape(B, n_slc, SLC_BLOCK, G, dh).transpose(0, 3, 1, 2, 4)
    vs_blocks = v_slc.reshape(B, n_slc, SLC_BLOCK, G, dh).transpose(0, 3, 1, 2, 4)
    kw_pad = jnp.pad(k_win, ((0, 0), (WINDOW, 0), (0, 0), (0, 0)))
    vw_pad = jnp.pad(v_win, ((0, 0), (WINDOW, 0), (0, 0), (0, 0)))
    qg = q.reshape(B, S, G, HPG, dh)
    b_ix = jnp.arange(B)[:, None, None, None]
    g_ix = jnp.arange(G)[None, :, None, None]
    blk_j = jnp.arange(n_slc)

    def block(i):
        s0 = i * QB
        t = s0 + jnp.arange(QB)
        qb = lax.dynamic_slice_in_dim(qg, s0, QB, axis=1).transpose(0, 2, 3, 1, 4)
        sc = jnp.einsum('bghqd,bgnd->bghqn', qb, kc) * scale
        pc = masked_softmax(sc, cmp_end[None, :] <= t[:, None])
        o_cmp = jnp.einsum('bghqn,bgnd->bghqd', pc.astype(vc.dtype), vc)
        imp = jnp.einsum('bghqn,nj->bgqj', pc, overlap)
        cur = t // SLC_BLOCK
        valid = blk_j[None, :] <= cur[:, None]
        forced = (blk_j[None, :] == 0) | (blk_j[None, :] == cur[:, None]) | (blk_j[None, :] == cur[:, None] - 1)
        imp = jnp.where(valid, jnp.where(forced, FORCED_SCORE, imp), -1.0)
        top_val, top_idx = lax.top_k(imp, top_k)
        ks = ks_blocks[b_ix, g_ix, top_idx]
        vs = vs_blocks[b_ix, g_ix, top_idx]
        key_pos = top_idx[..., None] * SLC_BLOCK + jnp.arange(SLC_BLOCK)
        mask_s = (key_pos <= t[None, None, :, None, None]) & (top_val >= 0.0)[..., None]
        ss = jnp.einsum('bghqd,bgqksd->bghqks', qb, ks) * scale
        ss = ss.reshape(B, G, HPG, QB, top_k * SLC_BLOCK)
        ps = masked_softmax(ss, mask_s.reshape(B, G, 1, QB, top_k * SLC_BLOCK))
        o_slc = jnp.einsum('bghqn,bgqnd->bghqd', ps.astype(vs.dtype),
                           vs.reshape(B, G, QB, top_k * SLC_BLOCK, dh))
        kw = lax.dynamic_slice_in_dim(kw_pad, s0, WINDOW + QB, axis=1).transpose(0, 2, 1, 3)
        vw = lax.dynamic_slice_in_dim(vw_pad, s0, WINDOW + QB, axis=1).transpose(0, 2, 1, 3)
        wpos = s0 - WINDOW + jnp.arange(WINDOW + QB)
        diff = t[:, None] - wpos[None, :]
        mask_w = (diff >= 0) & (diff < WINDOW) & (wpos[None, :] >= 0)
        sw = jnp.einsum('bghqd,bgkd->bghqk', qb, kw) * scale
        pw = masked_softmax(sw, mask_w)
        o_win = jnp.einsum('bghqk,bgkd->bghqd', pw.astype(vw.dtype), vw)
        gb = lax.dynamic_slice_in_dim(gates, s0, QB, axis=1)
        gb = gb.reshape(B, QB, G, HPG, 3).transpose(0, 2, 3, 1, 4)
        out = gb[..., 0:1] * o_cmp + gb[..., 1:2] * o_slc + gb[..., 2:3] * o_win
        return out.transpose(0, 3, 1, 2, 4).reshape(B, QB, NSA_WIDTH)

    outs = lax.map(block, jnp.arange(S // QB))
    return outs.transpose(1, 0, 2, 3).reshape(B, S, NSA_WIDTH)


def hgrn2_mixer(q, f, i, g, lower_bound, out_gain):
    B, S, _ = q.shape
    H, dk, dv, C = HGRN_HEADS, HGRN_DK, HGRN_DV, HGRN_CHUNK
    nC = S // C
    scale = dk ** -0.5
    lb = lower_bound.astype(jnp.float32)
    fgate = lb + (1.0 - lb) * jax.nn.sigmoid(f.astype(jnp.float32))
    log_f = jnp.log(fgate)
    k = 1.0 - fgate
    qa = jax.nn.silu(q.astype(jnp.float32)) * scale

    def to_chunks(a, d):
        return a.reshape(B, nC, C, H, d).transpose(1, 0, 3, 2, 4)

    xs = (to_chunks(qa, dk), to_chunks(k, dk), to_chunks(i.astype(jnp.float32), dv), to_chunks(log_f, dk))
    causal = jnp.tril(jnp.ones((C, C), dtype=bool))

    def step(state, inp):
        qc, kc, vc, lc = inp
        b = jnp.cumsum(lc, axis=2)
        inter = jnp.einsum('bhtk,bhkv->bhtv', qc * jnp.exp(b), state)
        diff = b[:, :, :, None, :] - b[:, :, None, :, :]
        decay = jnp.exp(jnp.where(causal[:, :, None], diff, MASK_VALUE))
        attn = jnp.einsum('bhtk,bhsk,bhtsk->bhts', qc, kc, decay)
        intra = jnp.einsum('bhts,bhsv->bhtv', attn, vc)
        b_last = b[:, :, -1:, :]
        new_state = jnp.exp(b_last[:, :, 0, :])[..., None] * state + \
            jnp.einsum('bhsk,bhsv->bhkv', kc * jnp.exp(b_last - b), vc)
        return new_state, inter + intra

    state0 = jnp.zeros((B, H, dk, dv), jnp.float32)
    _, o = lax.scan(step, state0, xs)
    o = o.transpose(1, 0, 3, 2, 4).reshape(B, S, H, dv)
    o = rms_norm(o, out_gain.reshape(H, dv)) * jax.nn.silu(g.astype(jnp.float32).reshape(B, S, H, dv))
    return o.reshape(B, S, HGRN_WIDTH).astype(q.dtype)


def memory_mixer(q, mem_n, w_k, w_v, q_gain, k_gain):
    B, S, _ = q.shape
    M = mem_n.shape[1]
    qh = rms_norm(q.reshape(B, S, MEM_HEADS, HEAD_DIM), q_gain)
    kh = rms_norm((mem_n @ w_k).reshape(B, M, MEM_HEADS, HEAD_DIM), k_gain)
    vh = (mem_n @ w_v).reshape(B, M, MEM_HEADS, HEAD_DIM)
    s = jnp.einsum('bshd,bmhd->bhsm', qh, kh).astype(jnp.float32) * (HEAD_DIM ** -0.5)
    p = jax.nn.softmax(s, axis=-1)
    o = jnp.einsum('bhsm,bmhd->bshd', p.astype(vh.dtype), vh)
    return o.reshape(B, S, MEM_WIDTH)


def setup_inputs(seed: int = 0) -> dict:
    key = jax.random.key(seed)
    ks = iter(jax.random.split(key, 40))

    def nrm(shape, scale):
        return jax.random.normal(next(ks), shape, jnp.float32) * scale

    def gain(shape):
        return 1.0 + 0.02 * jax.random.normal(next(ks), shape, jnp.float32)

    L = DEPTH
    return {
        "x": nrm((BATCH, SEQ, D_MODEL), 1.0),
        "mem": nrm((BATCH, MEM_TOKENS, D_MODEL), 1.0),
        "ffn1_norm": gain((L, D_MODEL)),
        "ffn1_w_gate": nrm((L, D_MODEL, D_FF), D_MODEL ** -0.5),
        "ffn1_w_up": nrm((L, D_MODEL, D_FF), D_MODEL ** -0.5),
        "ffn1_w_down": nrm((L, D_FF, D_MODEL), D_FF ** -0.5),
        "mix_norm": gain((L, D_MODEL)),
        "w_in": nrm((L, D_MODEL, IN_WIDTH), D_MODEL ** -0.5),
        "w_out": nrm((L, MIX_WIDTH, D_MODEL), MIX_WIDTH ** -0.5),
        "nsa_q_norm": gain((L, HEAD_DIM)),
        "nsa_k_norm": gain((L, HEAD_DIM)),
        "cmp_pos_k": nrm((L, CMP_BLOCK, HEAD_DIM), 0.02),
        "cmp_w1_k": nrm((L, CMP_BLOCK * HEAD_DIM, CMP_HIDDEN), (CMP_BLOCK * HEAD_DIM) ** -0.5),
        "cmp_w2_k": nrm((L, CMP_HIDDEN, HEAD_DIM), CMP_HIDDEN ** -0.5),
        "cmp_pos_v": nrm((L, CMP_BLOCK, HEAD_DIM), 0.02),
        "cmp_w1_v": nrm((L, CMP_BLOCK * HEAD_DIM, CMP_HIDDEN), (CMP_BLOCK * HEAD_DIM) ** -0.5),
        "cmp_w2_v": nrm((L, CMP_HIDDEN, HEAD_DIM), CMP_HIDDEN ** -0.5),
        "nsa_out_norm": gain((L, NSA_WIDTH)),
        "hgrn_lb_logits": nrm((L + 1, HGRN_WIDTH), 0.1),
        "hgrn_out_norm": gain((L, HGRN_WIDTH)),
        "mem_norm": gain((L, D_MODEL)),
        "mem_w_k": nrm((L, D_MODEL, MEM_WIDTH), D_MODEL ** -0.5),
        "mem_w_v": nrm((L, D_MODEL, MEM_WIDTH), D_MODEL ** -0.5),
        "mem_q_norm": gain((L, HEAD_DIM)),
        "mem_k_norm": gain((L, HEAD_DIM)),
        "mem_out_norm": gain((L, MEM_WIDTH)),
        "ffn2_norm": gain((L, D_MODEL)),
        "ffn2_w_gate": nrm((L, D_MODEL, D_FF), D_MODEL ** -0.5),
        "ffn2_w_up": nrm((L, D_MODEL, D_FF), D_MODEL ** -0.5),
        "ffn2_w_down": nrm((L, D_FF, D_MODEL), D_FF ** -0.5),
    }


def reference(x, mem, ffn1_norm, ffn1_w_gate, ffn1_w_up, ffn1_w_down, mix_norm, w_in, w_out,
              nsa_q_norm, nsa_k_norm, cmp_pos_k, cmp_w1_k, cmp_w2_k, cmp_pos_v, cmp_w1_v, cmp_w2_v,
              nsa_out_norm, hgrn_lb_logits, hgrn_out_norm,
              mem_norm, mem_w_k, mem_w_v, mem_q_norm, mem_k_norm, mem_out_norm,
              ffn2_norm, ffn2_w_gate, ffn2_w_up, ffn2_w_down):
    B, S, _ = x.shape
    cos, sin = rope_tables(S)
    lower_bounds = jnp.cumsum(jax.nn.softmax(hgrn_lb_logits.astype(jnp.float32), axis=0), axis=0)
    pts = split_points()
    for l in range(DEPTH):
        x = x + 0.5 * swiglu(rms_norm(x, ffn1_norm[l]), ffn1_w_gate[l], ffn1_w_up[l], ffn1_w_down[l])
        h = rms_norm(x, mix_norm[l])
        proj = h @ w_in[l]
        (q_a, k_c, v_c, k_s, v_s, k_w, v_w, g_a,
         q_h, f_h, i_h, g_h, q_m) = jnp.split(proj, pts, axis=-1)
        kv = lambda t: t.reshape(B, S, NSA_KV_GROUPS, HEAD_DIM)
        y_nsa = nsa_mixer(q_a.reshape(B, S, NSA_HEADS, HEAD_DIM), kv(k_c), kv(v_c), kv(k_s), kv(v_s),
                          kv(k_w), kv(v_w), jax.nn.sigmoid(g_a).reshape(B, S, NSA_HEADS, 3),
                          cmp_pos_k[l], cmp_w1_k[l], cmp_w2_k[l], cmp_pos_v[l], cmp_w1_v[l], cmp_w2_v[l],
                          nsa_q_norm[l], nsa_k_norm[l], cos, sin)
        y_hgrn = hgrn2_mixer(q_h, f_h, i_h, g_h, lower_bounds[l], hgrn_out_norm[l])
        mem_n = rms_norm(mem, mem_norm[l])
        y_mem = memory_mixer(q_m, mem_n, mem_w_k[l], mem_w_v[l], mem_q_norm[l], mem_k_norm[l])
        mixed = jnp.concatenate([rms_norm(y_nsa, nsa_out_norm[l]), y_hgrn.astype(x.dtype),
                                 rms_norm(y_mem, mem_out_norm[l])], axis=-1)
        x = x + mixed @ w_out[l]
        x = x + 0.5 * swiglu(rms_norm(x, ffn2_norm[l]), ffn2_w_gate[l], ffn2_w_up[l], ffn2_w_down[l])
    return x
```

```python
import functools

import numpy as np
import jax
import jax.numpy as jnp
from jax import lax
from jax.experimental import pallas as pl
from jax.experimental.pallas import tpu as pltpu

F32 = jnp.float32
BF16 = jnp.bfloat16

HEAD_DIM = 64
ROT_DIM = 16
ROT_HALF = 8
ROPE_THETA = 500000.0
NSA_HEADS = 8
NSA_GROUPS = 2
NSA_HPG = 4
CMP_BLOCK = 32
CMP_STRIDE = 16
SLC_BLOCK = 64
SLC_SHIFT = 6
SLC_TOPK = 16
WINDOW = 512
FORCED_SCORE = 1e4
HGRN_HEADS = 4
HGRN_CHUNK = 64
HGRN_WIDTH = 256
MEM_HEADS = 4
MEM_WIDTH = 256
NSA_WIDTH = 512
EPS = 1e-6
NEG = -1e30
QK_SCALE = HEAD_DIM ** -0.5

VMEM_LIMIT = 56 * 1024 * 1024
MAX_BLOCKS = 128
GATE_ROWS = 16
V_ROWS = 80
TOK_TILE = 512
Q_TILE = 128
HGRN_CB = 4
HGRN_LEVELS = (32, 16, 8, 4, 2, 1)

NT_DIMS = (((1,), (1,)), ((), ()))
TN_DIMS = (((0,), (0,)), ((), ()))


def _cparams(sem):
    return pltpu.CompilerParams(dimension_semantics=sem, vmem_limit_bytes=VMEM_LIMIT)


def _dot(a, b):
    return jnp.dot(a, b, preferred_element_type=F32)


def _dot_nt(a, b):
    return lax.dot_general(a, b, NT_DIMS, preferred_element_type=F32)


def _dot_tn(a, b):
    return lax.dot_general(a, b, TN_DIMS, preferred_element_type=F32)


def _sigmoid(x):
    return 1.0 / (1.0 + jnp.exp(-x))


def _silu(x):
    return x * _sigmoid(x)


def _split2(x):
    hi = x.astype(BF16)
    lo = (x - hi.astype(F32)).astype(BF16)
    return hi, lo


def _split3(x):
    hi = x.astype(BF16)
    r1 = x - hi.astype(F32)
    mid = r1.astype(BF16)
    lo = (r1 - mid.astype(F32)).astype(BF16)
    return hi, mid, lo


def _rms_rows(x, gain_row):
    ms = jnp.mean(x * x, axis=-1, keepdims=True)
    return x * lax.rsqrt(ms + EPS) * gain_row


def _rms_cols(x, gain_col):
    ms = jnp.mean(x * x, axis=0, keepdims=True)
    return x * lax.rsqrt(ms + EPS) * gain_col


def _seg_mean_sq(x, bd):
    hi, lo = _split2(x * x)
    return _dot(hi, bd) + _dot(lo, bd)


def _ffn_body(x_ref, g_ref, wg_ref, wu_ref, wd_ref, o_ref, a_scr, *, fc):
    x = x_ref[...]
    xn = _rms_rows(x, g_ref[...]).astype(BF16)
    d_ff = wg_ref.shape[1]
    for c in range(d_ff // fc):
        sl = slice(c * fc, (c + 1) * fc)
        g = _dot(xn, wg_ref[:, sl])
        u = _dot(xn, wu_ref[:, sl])
        a_scr[:, sl] = (_silu(g) * u).astype(BF16)
    o_ref[...] = x + 0.5 * _dot(a_scr[...], wd_ref[...])


def _ffn(x2d, gain, wg, wu, wd, *, tm=512, fc=256):
    t, d = x2d.shape
    d_ff = wg.shape[1]
    full = lambda shape: pl.BlockSpec(shape, lambda i: (0,) * len(shape))
    return pl.pallas_call(
        functools.partial(_ffn_body, fc=fc),
        grid=(t // tm,),
        in_specs=[pl.BlockSpec((tm, d), lambda i: (i, 0)), full((1, d)),
                  full((d, d_ff)), full((d, d_ff)), full((d_ff, d))],
        out_specs=pl.BlockSpec((tm, d), lambda i: (i, 0)),
        out_shape=jax.ShapeDtypeStruct((t, d), F32),
        scratch_shapes=[pltpu.VMEM((tm, d_ff), BF16)],
        compiler_params=_cparams(("parallel",)),
        name="ffn_half_step",
    )(x2d, gain.reshape(1, d), wg, wu, wd)


def _rope_cols(xn, cos, sin):
    x0, x1 = xn[0:ROT_HALF], xn[ROT_HALF:ROT_DIM]
    return jnp.concatenate([x0 * cos - x1 * sin, x1 * cos + x0 * sin, xn[ROT_DIM:]], axis=0)


def _rope_rows(x, cn, sa, sb):
    return x * cn + pltpu.roll(x, 128 - ROT_HALF, 1) * sa + pltpu.roll(x, ROT_HALF, 1) * sb


def _proj_body(x_ref, mg_ref, wt_ref, wn_ref, qg_ref, kg_ref, mqg_ref, cos_ref, sin_ref,
               cn_ref, sa_ref, sb_ref, bd_ref,
               qt_ref, vst_ref, vwt_ref, gt_ref, qmt_ref, kaug_ref, kvc_ref, hg_ref):
    tm = x_ref.shape[0]
    h = _rms_rows(x_ref[...], mg_ref[...]).astype(BF16)

    pt = _dot_nt(wt_ref[...], h)
    cos, sin = cos_ref[...], sin_ref[...]
    qg = qg_ref[...]
    for hh in range(NSA_HEADS):
        xq = _rms_cols(pt[hh * 64:(hh + 1) * 64], qg)
        qt_ref[hh * 64:(hh + 1) * 64, :] = (_rope_cols(xq, cos, sin) * QK_SCALE).astype(BF16)
    ones_rows = (lax.broadcasted_iota(jnp.int32, (V_ROWS - 64, tm), 0) == 0).astype(BF16)
    for g in range(NSA_GROUPS):
        vst_ref[g, 0:64, :] = pt[512 + g * 64:576 + g * 64].astype(BF16)
        vst_ref[g, 64:V_ROWS, :] = ones_rows
        vwt_ref[g, 0:64, :] = pt[640 + g * 64:704 + g * 64].astype(BF16)
        vwt_ref[g, 64:V_ROWS, :] = ones_rows
    gt_ref[...] = _sigmoid(pt[768:800])
    mqg = mqg_ref[...]
    for hh in range(MEM_HEADS):
        xm = _rms_cols(pt[800 + hh * 64:864 + hh * 64], mqg)
        qmt_ref[hh * 64:(hh + 1) * 64, :] = (xm * QK_SCALE).astype(BF16)

    pn = _dot(h, wn_ref[...])
    cn, sa, sb = cn_ref[...], sa_ref[...], sb_ref[...]
    bd, kg = bd_ref[...], kg_ref[...]
    kvc_ref[:, 0:128] = _rope_rows(pn[:, 0:128], cn, sa, sb)
    kvc_ref[:, 128:256] = pn[:, 128:256]
    ks = pn[:, 256:384]
    kw = pn[:, 384:512]
    ks = _rope_rows(ks * lax.rsqrt(_seg_mean_sq(ks, bd) + EPS) * kg, cn, sa, sb)
    kw = _rope_rows(kw * lax.rsqrt(_seg_mean_sq(kw, bd) + EPS) * kg, cn, sa, sb)
    lane = lax.broadcasted_iota(jnp.int32, (tm, 128), 1)
    row = lax.broadcasted_iota(jnp.int32, (tm, 128), 0)
    blk = (pl.program_id(1) * tm + row) >> SLC_SHIFT
    onehot = (lane == blk).astype(BF16)
    lo_half = lane < 64
    kaug_ref[0, :, 0:128] = jnp.where(lo_half, ks, pltpu.roll(kw, 64, 1)).astype(BF16)
    kaug_ref[0, :, 128:256] = onehot
    kaug_ref[1, :, 0:128] = jnp.where(lo_half, pltpu.roll(ks, 64, 1), kw).astype(BF16)
    kaug_ref[1, :, 128:256] = onehot
    hg_ref[...] = pn[:, 512:1536]


def _proj(x3d, mix_gain, wt, wn, q_gain, k_gain, mq_gain, rope):
    b, s, d = x3d.shape
    tm = TOK_TILE
    ns = s // tm
    cos_t, sin_t, cn, sa, sb = rope
    bd = jnp.asarray(np.kron(np.eye(2), np.full((64, 64), 1.0 / 64)), BF16)
    full = lambda shape: pl.BlockSpec(shape, lambda bi, i: (0,) * len(shape))
    out_shape = (
        jax.ShapeDtypeStruct((b, 512, s), BF16),
        jax.ShapeDtypeStruct((b, 2, ns, V_ROWS, tm), BF16),
        jax.ShapeDtypeStruct((b, 2, ns, V_ROWS, tm), BF16),
        jax.ShapeDtypeStruct((b, 32, s), F32),
        jax.ShapeDtypeStruct((b, 256, s), BF16),
        jax.ShapeDtypeStruct((b, 2, s, 256), BF16),
        jax.ShapeDtypeStruct((b, s, 256), F32),
        jax.ShapeDtypeStruct((b, s, 1024), F32),
    )
    out_specs = (
        pl.BlockSpec((None, 512, tm), lambda bi, i: (bi, 0, i)),
        pl.BlockSpec((None, 2, None, V_ROWS, tm), lambda bi, i: (bi, 0, i, 0, 0)),
        pl.BlockSpec((None, 2, None, V_ROWS, tm), lambda bi, i: (bi, 0, i, 0, 0)),
        pl.BlockSpec((None, 32, tm), lambda bi, i: (bi, 0, i)),
        pl.BlockSpec((None, 256, tm), lambda bi, i: (bi, 0, i)),
        pl.BlockSpec((None, 2, tm, 256), lambda bi, i: (bi, 0, i, 0)),
        pl.BlockSpec((None, tm, 256), lambda bi, i: (bi, i, 0)),
        pl.BlockSpec((None, tm, 1024), lambda bi, i: (bi, i, 0)),
    )
    in_specs = [
        pl.BlockSpec((None, tm, d), lambda bi, i: (bi, i, 0)),
        full((1, d)), full(wt.shape), full(wn.shape),
        full((64, 1)), full((1, 128)), full((64, 1)),
        pl.BlockSpec((ROT_HALF, tm), lambda bi, i: (0, i)),
        pl.BlockSpec((ROT_HALF, tm), lambda bi, i: (0, i)),
        pl.BlockSpec((tm, 128), lambda bi, i: (i, 0)),
        pl.BlockSpec((tm, 128), lambda bi, i: (i, 0)),
        pl.BlockSpec((tm, 128), lambda bi, i: (i, 0)),
        full((128, 128)),
    ]
    return pl.pallas_call(
        _proj_body, grid=(b, ns), in_specs=in_specs, out_specs=out_specs, out_shape=out_shape,
        compiler_params=_cparams(("parallel", "parallel")), name="mix_projection",
    )(x3d, mix_gain.reshape(1, d), wt, wn, q_gain.reshape(64, 1),
      jnp.tile(k_gain.reshape(1, 64), (1, 2)), mq_gain.reshape(64, 1), cos_t, sin_t, cn, sa, sb, bd)


def _rope_tables(s):
    pos = jnp.arange(s, dtype=F32)
    inv = ROPE_THETA ** (-(jnp.arange(0, ROT_DIM, 2, dtype=F32) / ROT_DIM))
    ang = pos[:, None] * inv[None, :]
    cos, sin = jnp.cos(ang), jnp.sin(ang)
    zeros = jnp.zeros((s, 64 - ROT_DIM), F32)
    cn = jnp.concatenate([cos, cos, jnp.ones((s, 64 - ROT_DIM), F32)], axis=1)
    sa = jnp.concatenate([-sin, jnp.zeros((s, ROT_HALF), F32), zeros], axis=1)
    sb = jnp.concatenate([jnp.zeros((s, ROT_HALF), F32), sin, zeros], axis=1)
    tile2 = lambda a: jnp.concatenate([a, a], axis=1)
    return cos.T, sin.T, tile2(cn), tile2(sa), tile2(sb)


def _cmp_body(yk_ref, ykn_ref, yv_ref, yvn_ref, pk_ref, pv_ref, w1k_ref, w1v_ref, w2k_ref, w2vt_ref,
              kg_ref, kc_ref, vct_ref):
    half = w1k_ref.shape[0] // 2

    def hidden(y_ref, yn_ref, p_ref, w1_ref):
        top = (y_ref[...] + p_ref[0:1, :]).astype(BF16)
        bot = (yn_ref[...] + p_ref[1:2, :]).astype(BF16)
        return _silu(_dot(top, w1_ref[0:half, :]) + _dot(bot, w1_ref[half:, :])).astype(BF16)

    kc = _dot(hidden(yk_ref, ykn_ref, pk_ref, w1k_ref), w2k_ref[...])
    kc_ref[...] = _rms_rows(kc, kg_ref[...]).astype(BF16)
    vct_ref[...] = _dot_nt(w2vt_ref[...], hidden(yv_ref, yvn_ref, pv_ref, w1v_ref)).astype(BF16)


def _compress(y, ynext, pos_k, pos_v, w1k, w1v, w2k, w2vt, k_gain):
    b, _, nc, width = y.shape
    ysp = lambda off: pl.BlockSpec((None, None, nc, width), lambda bi, g: (bi, g + off, 0, 0))
    full = lambda shape: pl.BlockSpec(shape, lambda bi, g: (0,) * len(shape))
    return pl.pallas_call(
        _cmp_body, grid=(b, 2),
        in_specs=[ysp(0), ysp(0), ysp(2), ysp(2), full(pos_k.shape), full(pos_v.shape),
                  full(w1k.shape), full(w1v.shape), full(w2k.shape), full(w2vt.shape), full((1, 64))],
        out_specs=(pl.BlockSpec((None, None, nc, 64), lambda bi, g: (bi, g, 0, 0)),
                   pl.BlockSpec((None, None, 64, nc), lambda bi, g: (bi, g, 0, 0))),
        out_shape=(jax.ShapeDtypeStruct((b, 2, nc, 64), BF16), jax.ShapeDtypeStruct((b, 2, 64, nc), BF16)),
        compiler_params=_cparams(("parallel", "parallel")), name="nsa_compress",
    )(y, ynext, y, ynext, pos_k, pos_v, w1k, w1v, w2k, w2vt, k_gain.reshape(1, 64))


def _heads_on_lanes(qt_ref):
    return jnp.concatenate([qt_ref[hh * 64:(hh + 1) * 64, :] for hh in range(NSA_HPG)], axis=1)


def _cmpsel_body(qt_ref, kc_ref, vct_ref, oc_ref, bias_ref):
    tq = qt_ref.shape[1]
    nc = kc_ref.shape[0]
    nblk = bias_ref.shape[0]
    mq = NSA_HPG * tq
    t0 = pl.program_id(2) * tq
    q4 = _heads_on_lanes(qt_ref)
    s = _dot(kc_ref[...], q4)
    n_idx = lax.broadcasted_iota(jnp.int32, (nc, mq), 0)
    t_idx = t0 + (lax.broadcasted_iota(jnp.int32, (nc, mq), 1) & (tq - 1))
    mask = n_idx * CMP_STRIDE + (CMP_BLOCK - 1) <= t_idx
    s = jnp.where(mask, s, NEG)
    m = jnp.max(s, axis=0, keepdims=True)
    p = jnp.where(mask, jnp.exp(s - m), 0.0)
    l = jnp.sum(p, axis=0, keepdims=True)
    pc = p * (1.0 / jnp.maximum(l, 1e-30))
    oc = _dot(vct_ref[...], pc.astype(BF16))
    for hh in range(NSA_HPG):
        oc_ref[hh * 64:(hh + 1) * 64, :] = oc[:, hh * tq:(hh + 1) * tq]

    pcs = pc[:, 0:tq]
    for hh in range(1, NSA_HPG):
        pcs = pcs + pc[:, hh * tq:(hh + 1) * tq]
    jj = lax.broadcasted_iota(jnp.int32, (nblk, nc), 0)
    nn = lax.broadcasted_iota(jnp.int32, (nblk, nc), 1)
    ov = jnp.where((nn * CMP_STRIDE < jj * SLC_BLOCK + SLC_BLOCK)
                   & (nn * CMP_STRIDE + CMP_BLOCK > jj * SLC_BLOCK), 1.0, 0.0).astype(BF16)
    hi, lo = _split2(pcs)
    imp = _dot(ov, hi) + _dot(ov, lo)

    j = lax.broadcasted_iota(jnp.int32, (nblk, tq), 0)
    cur = (t0 + lax.broadcasted_iota(jnp.int32, (nblk, tq), 1)) >> SLC_SHIFT
    forced = (j == 0) | (j == cur) | (j == cur - 1)
    imp = jnp.where(j <= cur, jnp.where(forced, FORCED_SCORE, imp), -1.0)
    jf = j.astype(F32)
    bias = jnp.full((nblk, tq), NEG, F32)
    for _ in range(min(SLC_TOPK, nblk)):
        v = jnp.max(imp, axis=0, keepdims=True)
        first = jnp.min(jnp.where(imp == v, jf, float(nblk)), axis=0, keepdims=True)
        pick = jf == first
        bias = jnp.where(pick, jnp.where(v >= 0.0, 0.0, NEG), bias)
        imp = jnp.where(pick, -3e38, imp)
    bias_ref[...] = bias.astype(BF16)


def _cmp_select(qt, kc, vct, *, tq=Q_TILE):
    b, _, s = qt.shape
    nc = kc.shape[2]
    nblk = MAX_BLOCKS
    assert s // SLC_BLOCK <= MAX_BLOCKS and s // SLC_BLOCK >= SLC_TOPK
    return pl.pallas_call(
        _cmpsel_body, grid=(b, NSA_GROUPS, s // tq),
        in_specs=[pl.BlockSpec((None, 256, tq), lambda bi, g, i: (bi, g, i)),
                  pl.BlockSpec((None, None, nc, 64), lambda bi, g, i: (bi, g, 0, 0)),
                  pl.BlockSpec((None, None, 64, nc), lambda bi, g, i: (bi, g, 0, 0))],
        out_specs=(pl.BlockSpec((None, 256, tq), lambda bi, g, i: (bi, g, i)),
                   pl.BlockSpec((None, None, nblk, tq), lambda bi, g, i: (bi, g, 0, i))),
        out_shape=(jax.ShapeDtypeStruct((b, 512, s), F32),
                   jax.ShapeDtypeStruct((b, NSA_GROUPS, nblk, s), BF16)),
        compiler_params=_cparams(("parallel", "parallel", "parallel")), name="nsa_compressed_select",
    )(qt, kc, vct)


def _flash_step(s, vt, m_ref, acc_ref, mask):
    if mask is not None:
        s = jnp.where(mask, s, NEG)
    m_old = m_ref[...]
    m_new = jnp.maximum(m_old, jnp.max(s, axis=0, keepdims=True))
    p = jnp.exp(s - m_new)
    if mask is not None:
        p = jnp.where(mask, p, 0.0)
    acc_ref[...] = jnp.exp(m_old - m_new) * acc_ref[...] + _dot(vt, p.astype(BF16))
    m_ref[...] = m_new


def _slcwin_body(qt_ref, bias_ref, kaug_ref, vst_ref, vwt_ref, oc_ref, gt_ref, y_ref,
                 qs_scr, qw_scr, ms_scr, as_scr, mw_scr, aw_scr):
    tq = qt_ref.shape[1]
    tk = vst_ref.shape[2]
    mq = NSA_HPG * tq
    t0 = pl.program_id(2) * tq
    n_full = t0 // tk

    q4 = _heads_on_lanes(qt_ref)
    bias = bias_ref[...]
    zeros = jnp.zeros((64, mq), BF16)
    qs_scr[0:64, :] = q4
    qs_scr[64:128, :] = zeros
    qs_scr[128:256, :] = jnp.concatenate([bias] * NSA_HPG, axis=1)
    qw_scr[0:64, :] = zeros
    qw_scr[64:128, :] = q4
    ms_scr[...] = jnp.full(ms_scr.shape, NEG, F32)
    mw_scr[...] = jnp.full(mw_scr.shape, NEG, F32)
    as_scr[...] = jnp.zeros(as_scr.shape, F32)
    aw_scr[...] = jnp.zeros(aw_scr.shape, F32)

    def sel_scores(kt):
        k = kaug_ref[pl.ds(pl.multiple_of(kt * tk, tk), tk), :]
        return _dot(k, qs_scr[...])

    def win_scores(kt):
        k = kaug_ref[pl.ds(pl.multiple_of(kt * tk, tk), tk), 0:128]
        return _dot(k, qw_scr[...])

    def full_tile(kt, carry):
        _flash_step(sel_scores(kt), vst_ref[kt], ms_scr, as_scr, None)
        return carry

    lax.fori_loop(0, n_full, full_tile, 0)

    key0 = lax.broadcasted_iota(jnp.int32, (tk, mq), 0)
    t_idx = t0 + (lax.broadcasted_iota(jnp.int32, (tk, mq), 1) & (tq - 1))

    @pl.when(n_full > 0)
    def _():
        key = (n_full - 1) * tk + key0
        _flash_step(win_scores(n_full - 1), vwt_ref[n_full - 1], mw_scr, aw_scr, t_idx - key < WINDOW)

    key = n_full * tk + key0
    causal = key <= t_idx
    _flash_step(sel_scores(n_full), vst_ref[n_full], ms_scr, as_scr, causal)
    _flash_step(win_scores(n_full), vwt_ref[n_full], mw_scr, aw_scr, causal & (t_idx - key < WINDOW))

    o_s = as_scr[0:64, :] * (1.0 / as_scr[64:65, :])
    o_w = aw_scr[0:64, :] * (1.0 / aw_scr[64:65, :])
    gt = gt_ref[...]
    for hh in range(NSA_HPG):
        sl = slice(hh * tq, (hh + 1) * tq)
        y_ref[hh * 64:(hh + 1) * 64, :] = (gt[3 * hh:3 * hh + 1] * oc_ref[hh * 64:(hh + 1) * 64, :]
                                          + gt[3 * hh + 1:3 * hh + 2] * o_s[:, sl]
                                          + gt[3 * hh + 2:3 * hh + 3] * o_w[:, sl])


def _slc_win(qt, bias, kaug, vst, vwt, oc, gt, *, tq=Q_TILE):
    b, _, s = qt.shape
    nblk = bias.shape[2]
    ns, tk = vst.shape[2], vst.shape[4]
    mq = NSA_HPG * tq
    qblk = pl.BlockSpec((None, 256, tq), lambda bi, g, i: (bi, g, i))
    vblk = pl.BlockSpec((None, None, ns, V_ROWS, tk), lambda bi, g, i: (bi, g, 0, 0, 0))
    return pl.pallas_call(
        _slcwin_body, grid=(b, NSA_GROUPS, s // tq),
        in_specs=[qblk,
                  pl.BlockSpec((None, None, nblk, tq), lambda bi, g, i: (bi, g, 0, i)),
                  pl.BlockSpec((None, None, s, 256), lambda bi, g, i: (bi, g, 0, 0)),
                  vblk, vblk, qblk,
                  pl.BlockSpec((None, None, GATE_ROWS, tq), lambda bi, g, i: (bi, g, 0, i))],
        out_specs=qblk,
        out_shape=jax.ShapeDtypeStruct((b, 512, s), F32),
        scratch_shapes=[pltpu.VMEM((256, mq), BF16), pltpu.VMEM((128, mq), BF16),
                        pltpu.VMEM((1, mq), F32), pltpu.VMEM((V_ROWS, mq), F32),
                        pltpu.VMEM((1, mq), F32), pltpu.VMEM((V_ROWS, mq), F32)],
        compiler_params=_cparams(("parallel", "parallel", "arbitrary")), name="nsa_selected_window",
    )(qt, bias, kaug, vst, vwt, oc, gt.reshape(b, NSA_GROUPS, GATE_ROWS, s))


def _hgrn_consts():
    c = HGRN_CHUNK
    t = np.arange(c)
    lower = (t[None, :] <= t[:, None]).astype(np.float32)
    rows = [lower]
    masks = []
    for half in HGRN_LEVELS:
        mid = (t // (2 * half)) * (2 * half) + half - 1
        rows.append(lower[mid])
        same = (t[:, None] // (2 * half)) == (t[None, :] // (2 * half))
        right = (t[:, None] & half) != 0
        left = (t[None, :] & half) == 0
        masks.append((same & right & left).astype(np.float32))
    masks.append(np.eye(c, dtype=np.float32))
    mall = np.concatenate(rows, axis=0)
    lvl = np.stack([np.tile(mk, (HGRN_HEADS, 1)) for mk in masks])
    bdm = np.kron(np.eye(HGRN_HEADS), np.ones((64, 64), np.float32))
    return jnp.asarray(mall, BF16), jnp.asarray(lvl, F32), jnp.asarray(bdm, F32), jnp.asarray(bdm / 64, BF16)


def _hgrn_body(hg_ref, lb_ref, og_ref, mall_ref, lvl_ref, bdm_ref, bdn_ref, y_ref, st_scr):
    c = HGRN_CHUNK
    w = HGRN_WIDTH

    @pl.when(pl.program_id(1) == 0)
    def _():
        st_scr[...] = jnp.zeros(st_scr.shape, F32)

    lb = lb_ref[...]
    lane = lax.broadcasted_iota(jnp.int32, (c, w), 1)
    trow = lax.broadcasted_iota(jnp.int32, (c, w), 0)
    head_masks = [(lane >> 6) == hh for hh in range(HGRN_HEADS)]
    mall = mall_ref[...]

    def stack_heads(x):
        return jnp.concatenate([jnp.where(hm, x, 0.0) for hm in head_masks], axis=0).astype(BF16)

    for ci in range(hg_ref.shape[0] // c):
        rows = slice(ci * c, (ci + 1) * c)
        qa = _silu(hg_ref[rows, 0:w]) * (HEAD_DIM ** -0.5)
        fg = lb + (1.0 - lb) * _sigmoid(hg_ref[rows, w:2 * w])
        kk = 1.0 - fg
        v16 = hg_ref[rows, 2 * w:3 * w].astype(BF16)
        l_hi, l_mid, l_lo = _split3(jnp.log(fg))
        r_all = _dot(mall, l_hi) + _dot(mall, l_mid) + _dot(mall, l_lo)
        bcum = r_all[0:c]
        b_last = bcum[c - 1:c, :]

        attn = lvl_ref[len(HGRN_LEVELS)] * _dot_nt(stack_heads(qa), kk.astype(BF16))
        for li, half in enumerate(HGRN_LEVELS):
            ref_pt = r_all[(li + 1) * c:(li + 2) * c]
            right = (trow & half) != 0
            e = jnp.exp(jnp.where(right, bcum - ref_pt, ref_pt - bcum))
            qt = jnp.where(right, qa * e, 0.0)
            kt = jnp.where(right, 0.0, kk * e).astype(BF16)
            attn = attn + lvl_ref[li] * _dot_nt(stack_heads(qt), kt)
        intra = jnp.zeros((c, w), F32)
        attn16 = attn.astype(BF16)
        for hh in range(HGRN_HEADS):
            intra = intra + _dot(attn16[hh * c:(hh + 1) * c], jnp.where(head_masks[hh], v16, 0))

        st = st_scr[...]
        inter = _dot_nt((qa * jnp.exp(bcum)).astype(BF16), st.astype(BF16))
        kl = (kk * jnp.exp(b_last - bcum)).astype(BF16)
        st_scr[...] = st * jnp.exp(b_last) + bdm_ref[...] * _dot_tn(v16, kl)

        o = inter + intra
        hi, lo = _split2(o * o)
        ms = _dot(hi, bdn_ref[...]) + _dot(lo, bdn_ref[...])
        y_ref[rows, :] = o * lax.rsqrt(ms + EPS) * og_ref[...] * _silu(hg_ref[rows, 3 * w:4 * w])


def _hgrn(hg, lower_bound, out_gain):
    b, s, _ = hg.shape
    rows = HGRN_CB * HGRN_CHUNK
    mall, lvl, bdm, bdn = _hgrn_consts()
    full = lambda shape: pl.BlockSpec(shape, lambda bi, i: (0,) * len(shape))
    return pl.pallas_call(
        _hgrn_body, grid=(b, s // rows),
        in_specs=[pl.BlockSpec((None, rows, 4 * HGRN_WIDTH), lambda bi, i: (bi, i, 0)),
                  full((1, HGRN_WIDTH)), full((1, HGRN_WIDTH)),
                  full(mall.shape), full(lvl.shape), full(bdm.shape), full(bdn.shape)],
        out_specs=pl.BlockSpec((None, rows, HGRN_WIDTH), lambda bi, i: (bi, i, 0)),
        out_shape=jax.ShapeDtypeStruct((b, s, HGRN_WIDTH), F32),
        scratch_shapes=[pltpu.VMEM((HGRN_WIDTH, HGRN_WIDTH), F32)],
        compiler_params=_cparams(("parallel", "arbitrary")), name="hgrn2_chunks",
    )(hg, lower_bound.reshape(1, -1), out_gain.reshape(1, -1), mall, lvl, bdm, bdn)


def _memkv_body(mem_ref, mg_ref, wk_ref, wvt_ref, kg_ref, kh_ref, vht_ref):
    m = mem_ref.shape[0]
    mn = _rms_rows(mem_ref[...], mg_ref[...]).astype(BF16)
    k = _dot(mn, wk_ref[...])
    vt = _dot_nt(wvt_ref[...], mn)
    ones_rows = (lax.broadcasted_iota(jnp.int32, (V_ROWS - 64, m), 0) == 0).astype(BF16)
    for hh in range(MEM_HEADS):
        kh_ref[hh] = _rms_rows(k[:, hh * 64:(hh + 1) * 64], kg_ref[...]).astype(BF16)
        vht_ref[hh, 0:64, :] = vt[hh * 64:(hh + 1) * 64].astype(BF16)
        vht_ref[hh, 64:V_ROWS, :] = ones_rows


def _mem_kv(mem, mem_gain, wk, wvt, k_gain):
    b, m, d = mem.shape
    full = lambda shape: pl.BlockSpec(shape, lambda bi: (0,) * len(shape))
    return pl.pallas_call(
        _memkv_body, grid=(b,),
        in_specs=[pl.BlockSpec((None, m, d), lambda bi: (bi, 0, 0)), full((1, d)),
                  full(wk.shape), full(wvt.shape), full((1, 64))],
        out_specs=(pl.BlockSpec((None, MEM_HEADS, m, 64), lambda bi: (bi, 0, 0, 0)),
                   pl.BlockSpec((None, MEM_HEADS, V_ROWS, m), lambda bi: (bi, 0, 0, 0))),
        out_shape=(jax.ShapeDtypeStruct((b, MEM_HEADS, m, 64), BF16),
                   jax.ShapeDtypeStruct((b, MEM_HEADS, V_ROWS, m), BF16)),
        compiler_params=_cparams(("parallel",)), name="memory_kv",
    )(mem, mem_gain.reshape(1, d), wk, wvt, k_gain.reshape(1, 64))


def _memattn_body(qmt_ref, kh_ref, vht_ref, y_ref):
    for hh in range(MEM_HEADS):
        s = _dot(kh_ref[hh], qmt_ref[hh * 64:(hh + 1) * 64, :])
        p = jnp.exp(s - jnp.max(s, axis=0, keepdims=True))
        o = _dot(vht_ref[hh], p.astype(BF16))
        y_ref[hh * 64:(hh + 1) * 64, :] = o[0:64] * (1.0 / o[64:65])


def _mem_attn(qmt, kh, vht, *, tq=TOK_TILE):
    b, _, s = qmt.shape
    m = kh.shape[2]
    return pl.pallas_call(
        _memattn_body, grid=(b, s // tq),
        in_specs=[pl.BlockSpec((None, MEM_WIDTH, tq), lambda bi, i: (bi, 0, i)),
                  pl.BlockSpec((None, MEM_HEADS, m, 64), lambda bi, i: (bi, 0, 0, 0)),
                  pl.BlockSpec((None, MEM_HEADS, V_ROWS, m), lambda bi, i: (bi, 0, 0, 0))],
        out_specs=pl.BlockSpec((None, MEM_WIDTH, tq), lambda bi, i: (bi, 0, i)),
        out_shape=jax.ShapeDtypeStruct((b, MEM_WIDTH, s), F32),
        compiler_params=_cparams(("parallel", "parallel")), name="memory_attention",
    )(qmt, kh, vht)


def _out_body(x_ref, ynt_ref, yh_ref, ymt_ref, ng_ref, mg_ref, wo_ref, o_ref):
    nsa = _rms_cols(ynt_ref[...], ng_ref[...]).astype(BF16)
    mem = _rms_cols(ymt_ref[...], mg_ref[...]).astype(BF16)
    acc = _dot_tn(nsa, wo_ref[0:NSA_WIDTH, :])
    acc = acc + _dot(yh_ref[...].astype(BF16), wo_ref[NSA_WIDTH:NSA_WIDTH + HGRN_WIDTH, :])
    acc = acc + _dot_tn(mem, wo_ref[NSA_WIDTH + HGRN_WIDTH:, :])
    o_ref[...] = x_ref[...] + acc


def _out_proj(x3d, ynt, yh, ymt, nsa_gain, mem_gain, wo, *, tm=TOK_TILE):
    b, s, d = x3d.shape
    full = lambda shape: pl.BlockSpec(shape, lambda bi, i: (0,) * len(shape))
    return pl.pallas_call(
        _out_body, grid=(b, s // tm),
        in_specs=[pl.BlockSpec((None, tm, d), lambda bi, i: (bi, i, 0)),
                  pl.BlockSpec((None, NSA_WIDTH, tm), lambda bi, i: (bi, 0, i)),
                  pl.BlockSpec((None, tm, HGRN_WIDTH), lambda bi, i: (bi, i, 0)),
                  pl.BlockSpec((None, MEM_WIDTH, tm), lambda bi, i: (bi, 0, i)),
                  full((NSA_WIDTH, 1)), full((MEM_WIDTH, 1)), full(wo.shape)],
        out_specs=pl.BlockSpec((None, tm, d), lambda bi, i: (bi, i, 0)),
        out_shape=jax.ShapeDtypeStruct((b, s, d), F32),
        compiler_params=_cparams(("parallel", "parallel")), name="mix_out_projection",
    )(x3d, ynt, yh, ymt, nsa_gain.reshape(-1, 1), mem_gain.reshape(-1, 1), wo)


def _mix(x, mem, mix_norm, w_in, w_out, nsa_q_norm, nsa_k_norm, cmp_pos_k, cmp_w1_k, cmp_w2_k,
         cmp_pos_v, cmp_w1_v, cmp_w2_v, nsa_out_norm, lower_bound, hgrn_out_norm,
         mem_norm, mem_w_k, mem_w_v, mem_q_norm, mem_k_norm, mem_out_norm):
    b, s, d = x.shape
    sizes = (512, 128, 128, 128, 128, 128, 128, 24, 256, 256, 256, 256, 256)
    offs = np.concatenate([[0], np.cumsum(sizes)])
    col = lambda i: w_in[:, offs[i]:offs[i + 1]]
    (q_a, k_c, v_c, k_s, v_s, k_w, v_w, g_a, q_h, f_h, i_h, g_h, q_m) = [col(i) for i in range(13)]
    gpad = jnp.zeros((d, GATE_ROWS - 3 * NSA_HPG), w_in.dtype)
    wt = jnp.concatenate([q_a, v_s, v_w, g_a[:, :3 * NSA_HPG], gpad, g_a[:, 3 * NSA_HPG:], gpad, q_m],
                         axis=1).T.astype(BF16)
    wn = jnp.concatenate([k_c, v_c, k_s, k_w, q_h, f_h, i_h, g_h], axis=1).astype(BF16)

    qt, vst, vwt, gt, qmt, kaug, kvc, hg = _proj(
        x, mix_norm, wt, wn, nsa_q_norm, nsa_k_norm, mem_q_norm, _rope_tables(s))

    y = kvc.reshape(b, s, 4, 64).transpose(0, 2, 1, 3).reshape(b, 4, s // CMP_STRIDE, CMP_STRIDE * 64)
    ynext = jnp.concatenate([y[:, :, 1:], jnp.zeros_like(y[:, :, :1])], axis=2)
    kc, vct = _compress(y, ynext, cmp_pos_k.reshape(2, -1), cmp_pos_v.reshape(2, -1),
                        cmp_w1_k.astype(BF16), cmp_w1_v.astype(BF16), cmp_w2_k.astype(BF16),
                        cmp_w2_v.T.astype(BF16), nsa_k_norm)
    oc, bias = _cmp_select(qt, kc, vct)
    y_nsa = _slc_win(qt, bias, kaug, vst, vwt, oc, gt)

    y_hgrn = _hgrn(hg, lower_bound, hgrn_out_norm)

    kh, vht = _mem_kv(mem, mem_norm, mem_w_k.astype(BF16), mem_w_v.T.astype(BF16), mem_k_norm)
    y_mem = _mem_attn(qmt, kh, vht)

    return _out_proj(x, y_nsa, y_hgrn, y_mem, nsa_out_norm, mem_out_norm, w_out.astype(BF16))


def kernel(x, mem, ffn1_norm, ffn1_w_gate, ffn1_w_up, ffn1_w_down, mix_norm, w_in, w_out, nsa_q_norm, nsa_k_norm, cmp_pos_k, cmp_w1_k, cmp_w2_k, cmp_pos_v, cmp_w1_v, cmp_w2_v, nsa_out_norm, hgrn_lb_logits, hgrn_out_norm, mem_norm, mem_w_k, mem_w_v, mem_q_norm, mem_k_norm, mem_out_norm, ffn2_norm, ffn2_w_gate, ffn2_w_up, ffn2_w_down):
    b, s, d = x.shape
    depth = ffn1_norm.shape[0]
    lower_bounds = jnp.cumsum(jax.nn.softmax(hgrn_lb_logits.astype(F32), axis=0), axis=0)
    for l in range(depth):
        x = _ffn(x.reshape(b * s, d), ffn1_norm[l], ffn1_w_gate[l].astype(BF16),
                 ffn1_w_up[l].astype(BF16), ffn1_w_down[l].astype(BF16)).reshape(b, s, d)
        x = _mix(x, mem, mix_norm[l], w_in[l], w_out[l], nsa_q_norm[l], nsa_k_norm[l],
                 cmp_pos_k[l], cmp_w1_k[l], cmp_w2_k[l], cmp_pos_v[l], cmp_w1_v[l], cmp_w2_v[l],
                 nsa_out_norm[l], lower_bounds[l], hgrn_out_norm[l],
                 mem_norm[l], mem_w_k[l], mem_w_v[l], mem_q_norm[l], mem_k_norm[l], mem_out_norm[l])
        x = _ffn(x.reshape(b * s, d), ffn2_norm[l], ffn2_w_gate[l].astype(BF16),
                 ffn2_w_up[l].astype(BF16), ffn2_w_down[l].astype(BF16)).reshape(b, s, d)
    return x
```

```python
import functools

import numpy as np
import jax
import jax.numpy as jnp
from jax import lax
from jax.experimental import pallas as pl
from jax.experimental.pallas import tpu as pltpu

F32 = jnp.float32
BF16 = jnp.bfloat16

HEAD_DIM = 64
ROT_DIM = 16
ROT_HALF = 8
ROPE_THETA = 500000.0
NSA_HEADS = 8
NSA_GROUPS = 2
NSA_HPG = 4
CMP_BLOCK = 32
CMP_STRIDE = 16
SLC_BLOCK = 64
SLC_SHIFT = 6
SLC_TOPK = 16
WINDOW = 512
FORCED_SCORE = 1e4
HGRN_HEADS = 4
HGRN_CHUNK = 64
HGRN_WIDTH = 256
MEM_HEADS = 4
MEM_WIDTH = 256
NSA_WIDTH = 512
EPS = 1e-6
NEG = -1e30
QK_SCALE = HEAD_DIM ** -0.5

VMEM_LIMIT = 56 * 1024 * 1024
MAX_BLOCKS = 128
GATE_ROWS = 16
V_ROWS = 80
TOK_TILE = 512
HGRN_CB = 4
HGRN_LEVELS = (32, 16, 8, 4, 2, 1)

NT_DIMS = (((1,), (1,)), ((), ()))
TN_DIMS = (((0,), (0,)), ((), ()))


def _cparams(sem):
    return pltpu.CompilerParams(dimension_semantics=sem, vmem_limit_bytes=VMEM_LIMIT)


def _dot(a, b):
    return jnp.dot(a, b, preferred_element_type=F32)


def _dot_nt(a, b):
    return lax.dot_general(a, b, NT_DIMS, preferred_element_type=F32)


def _dot_tn(a, b):
    return lax.dot_general(a, b, TN_DIMS, preferred_element_type=F32)


def _sigmoid(x):
    return 1.0 / (1.0 + jnp.exp(-x))


def _silu(x):
    return x * _sigmoid(x)


def _split2(x):
    hi = x.astype(BF16)
    lo = (x - hi.astype(F32)).astype(BF16)
    return hi, lo


def _split3(x):
    hi = x.astype(BF16)
    r1 = x - hi.astype(F32)
    mid = r1.astype(BF16)
    lo = (r1 - mid.astype(F32)).astype(BF16)
    return hi, mid, lo


def _rms_rows(x, gain_row):
    ms = jnp.mean(x * x, axis=-1, keepdims=True)
    return x * lax.rsqrt(ms + EPS) * gain_row


def _rms_cols(x, gain_col):
    ms = jnp.mean(x * x, axis=0, keepdims=True)
    return x * lax.rsqrt(ms + EPS) * gain_col


def _seg_mean_sq(x, bd):
    hi, lo = _split2(x * x)
    return _dot(hi, bd) + _dot(lo, bd)


def _ffn_body(x_ref, g_ref, wg_ref, wu_ref, wd_ref, o_ref, a_scr, *, fc):
    x = x_ref[...]
    xn = _rms_rows(x, g_ref[...]).astype(BF16)
    d_ff = wg_ref.shape[1]
    for c in range(d_ff // fc):
        sl = slice(c * fc, (c + 1) * fc)
        g = _dot(xn, wg_ref[:, sl])
        u = _dot(xn, wu_ref[:, sl])
        a_scr[:, sl] = (_silu(g) * u).astype(BF16)
    o_ref[...] = x + 0.5 * _dot(a_scr[...], wd_ref[...])


def _ffn(x2d, gain, wg, wu, wd, *, tm=512, fc=256):
    t, d = x2d.shape
    d_ff = wg.shape[1]
    full = lambda shape: pl.BlockSpec(shape, lambda i: (0,) * len(shape))
    return pl.pallas_call(
        functools.partial(_ffn_body, fc=fc),
        grid=(t // tm,),
        in_specs=[pl.BlockSpec((tm, d), lambda i: (i, 0)), full((1, d)),
                  full((d, d_ff)), full((d, d_ff)), full((d_ff, d))],
        out_specs=pl.BlockSpec((tm, d), lambda i: (i, 0)),
        out_shape=jax.ShapeDtypeStruct((t, d), F32),
        scratch_shapes=[pltpu.VMEM((tm, d_ff), BF16)],
        compiler_params=_cparams(("parallel",)),
        name="ffn_half_step",
    )(x2d, gain.reshape(1, d), wg, wu, wd)


def _rope_cols(xn, cos, sin):
    x0, x1 = xn[0:ROT_HALF], xn[ROT_HALF:ROT_DIM]
    return jnp.concatenate([x0 * cos - x1 * sin, x1 * cos + x0 * sin, xn[ROT_DIM:]], axis=0)


def _rope_rows(x, cn, sa, sb):
    return x * cn + pltpu.roll(x, 128 - ROT_HALF, 1) * sa + pltpu.roll(x, ROT_HALF, 1) * sb


def _proj_body(x_ref, mg_ref, wt_ref, wn_ref, qg_ref, kg_ref, mqg_ref, cos_ref, sin_ref,
               cn_ref, sa_ref, sb_ref, bd_ref,
               qt_ref, vst_ref, vwt_ref, gt_ref, qmt_ref, kaug_ref, kvc_ref, hg_ref):
    tm = x_ref.shape[0]
    h = _rms_rows(x_ref[...], mg_ref[...]).astype(BF16)

    pt = _dot_nt(wt_ref[...], h)
    cos, sin = cos_ref[...], sin_ref[...]
    qg = qg_ref[...]
    for hh in range(NSA_HEADS):
        xq = _rms_cols(pt[hh * 64:(hh + 1) * 64], qg)
        qt_ref[hh * 64:(hh + 1) * 64, :] = (_rope_cols(xq, cos, sin) * QK_SCALE).astype(BF16)
    ones_rows = (lax.broadcasted_iota(jnp.int32, (V_ROWS - 64, tm), 0) == 0).astype(BF16)
    for g in range(NSA_GROUPS):
        vst_ref[g, 0:64, :] = pt[512 + g * 64:576 + g * 64].astype(BF16)
        vst_ref[g, 64:V_ROWS, :] = ones_rows
        vwt_ref[g, 0:64, :] = pt[640 + g * 64:704 + g * 64].astype(BF16)
        vwt_ref[g, 64:V_ROWS, :] = ones_rows
    gt_ref[...] = _sigmoid(pt[768:800])
    mqg = mqg_ref[...]
    for hh in range(MEM_HEADS):
        xm = _rms_cols(pt[800 + hh * 64:864 + hh * 64], mqg)
        qmt_ref[hh * 64:(hh + 1) * 64, :] = (xm * QK_SCALE).astype(BF16)

    pn = _dot(h, wn_ref[...])
    cn, sa, sb = cn_ref[...], sa_ref[...], sb_ref[...]
    bd, kg = bd_ref[...], kg_ref[...]
    kvc_ref[:, 0:128] = _rope_rows(pn[:, 0:128], cn, sa, sb)
    kvc_ref[:, 128:256] = pn[:, 128:256]
    ks = pn[:, 256:384]
    kw = pn[:, 384:512]
    ks = _rope_rows(ks * lax.rsqrt(_seg_mean_sq(ks, bd) + EPS) * kg, cn, sa, sb)
    kw = _rope_rows(kw * lax.rsqrt(_seg_mean_sq(kw, bd) + EPS) * kg, cn, sa, sb)
    lane = lax.broadcasted_iota(jnp.int32, (tm, 128), 1)
    row = lax.broadcasted_iota(jnp.int32, (tm, 128), 0)
    blk = (pl.program_id(1) * tm + row) >> SLC_SHIFT
    onehot = (lane == blk).astype(BF16)
    lo_half = lane < 64
    kaug_ref[0, :, 0:128] = jnp.where(lo_half, ks, pltpu.roll(kw, 64, 1)).astype(BF16)
    kaug_ref[0, :, 128:256] = onehot
    kaug_ref[1, :, 0:128] = jnp.where(lo_half, pltpu.roll(ks, 64, 1), kw).astype(BF16)
    kaug_ref[1, :, 128:256] = onehot
    hg_ref[...] = pn[:, 512:1536]


def _proj(x3d, mix_gain, wt, wn, q_gain, k_gain, mq_gain, rope):
    b, s, d = x3d.shape
    tm = TOK_TILE
    ns = s // tm
    cos_t, sin_t, cn, sa, sb = rope
    bd = jnp.asarray(np.kron(np.eye(2), np.full((64, 64), 1.0 / 64)), BF16)
    full = lambda shape: pl.BlockSpec(shape, lambda bi, i: (0,) * len(shape))
    out_shape = (
        jax.ShapeDtypeStruct((b, 512, s), BF16),
        jax.ShapeDtypeStruct((b, 2, ns, V_ROWS, tm), BF16),
        jax.ShapeDtypeStruct((b, 2, ns, V_ROWS, tm), BF16),
        jax.ShapeDtypeStruct((b, 32, s), F32),
        jax.ShapeDtypeStruct((b, 256, s), BF16),
        jax.ShapeDtypeStruct((b, 2, s, 256), BF16),
        jax.ShapeDtypeStruct((b, s, 256), F32),
        jax.ShapeDtypeStruct((b, s, 1024), F32),
    )
    out_specs = (
        pl.BlockSpec((None, 512, tm), lambda bi, i: (bi, 0, i)),
        pl.BlockSpec((None, 2, None, V_ROWS, tm), lambda bi, i: (bi, 0, i, 0, 0)),
        pl.BlockSpec((None, 2, None, V_ROWS, tm), lambda bi, i: (bi, 0, i, 0, 0)),
        pl.BlockSpec((None, 32, tm), lambda bi, i: (bi, 0, i)),
        pl.BlockSpec((None, 256, tm), lambda bi, i: (bi, 0, i)),
        pl.BlockSpec((None, 2, tm, 256), lambda bi, i: (bi, 0, i, 0)),
        pl.BlockSpec((None, tm, 256), lambda bi, i: (bi, i, 0)),
        pl.BlockSpec((None, tm, 1024), lambda bi, i: (bi, i, 0)),
    )
    in_specs = [
        pl.BlockSpec((None, tm, d), lambda bi, i: (bi, i, 0)),
        full((1, d)), full(wt.shape), full(wn.shape),
        full((64, 1)), full((1, 128)), full((64, 1)),
        pl.BlockSpec((ROT_HALF, tm), lambda bi, i: (0, i)),
        pl.BlockSpec((ROT_HALF, tm), lambda bi, i: (0, i)),
        pl.BlockSpec((tm, 128), lambda bi, i: (i, 0)),
        pl.BlockSpec((tm, 128), lambda bi, i: (i, 0)),
        pl.BlockSpec((tm, 128), lambda bi, i: (i, 0)),
        full((128, 128)),
    ]
    return pl.pallas_call(
        _proj_body, grid=(b, ns), in_specs=in_specs, out_specs=out_specs, out_shape=out_shape,
        compiler_params=_cparams(("parallel", "parallel")), name="mix_projection",
    )(x3d, mix_gain.reshape(1, d), wt, wn, q_gain.reshape(64, 1),
      jnp.tile(k_gain.reshape(1, 64), (1, 2)), mq_gain.reshape(64, 1), cos_t, sin_t, cn, sa, sb, bd)


def _rope_tables(s):
    pos = jnp.arange(s, dtype=F32)
    inv = ROPE_THETA ** (-(jnp.arange(0, ROT_DIM, 2, dtype=F32) / ROT_DIM))
    ang = pos[:, None] * inv[None, :]
    cos, sin = jnp.cos(ang), jnp.sin(ang)
    zeros = jnp.zeros((s, 64 - ROT_DIM), F32)
    cn = jnp.concatenate([cos, cos, jnp.ones((s, 64 - ROT_DIM), F32)], axis=1)
    sa = jnp.concatenate([-sin, jnp.zeros((s, ROT_HALF), F32), zeros], axis=1)
    sb = jnp.concatenate([jnp.zeros((s, ROT_HALF), F32), sin, zeros], axis=1)
    tile2 = lambda a: jnp.concatenate([a, a], axis=1)
    return cos.T, sin.T, tile2(cn), tile2(sa), tile2(sb)


def _cmp_body(yk_ref, ykn_ref, yv_ref, yvn_ref, pk_ref, pv_ref, w1k_ref, w1v_ref, w2k_ref, w2vt_ref,
              kg_ref, kc_ref, vct_ref):
    half = w1k_ref.shape[0] // 2

    def hidden(y_ref, yn_ref, p_ref, w1_ref):
        top = (y_ref[...] + p_ref[0:1, :]).astype(BF16)
        bot = (yn_ref[...] + p_ref[1:2, :]).astype(BF16)
        return _silu(_dot(top, w1_ref[0:half, :]) + _dot(bot, w1_ref[half:, :])).astype(BF16)

    kc = _dot(hidden(yk_ref, ykn_ref, pk_ref, w1k_ref), w2k_ref[...])
    kc_ref[...] = _rms_rows(kc, kg_ref[...]).astype(BF16)
    vct_ref[...] = _dot_nt(w2vt_ref[...], hidden(yv_ref, yvn_ref, pv_ref, w1v_ref)).astype(BF16)


def _compress(y, ynext, pos_k, pos_v, w1k, w1v, w2k, w2vt, k_gain):
    b, _, nc, width = y.shape
    ysp = lambda off: pl.BlockSpec((None, None, nc, width), lambda bi, g: (bi, g + off, 0, 0))
    full = lambda shape: pl.BlockSpec(shape, lambda bi, g: (0,) * len(shape))
    return pl.pallas_call(
        _cmp_body, grid=(b, 2),
        in_specs=[ysp(0), ysp(0), ysp(2), ysp(2), full(pos_k.shape), full(pos_v.shape),
                  full(w1k.shape), full(w1v.shape), full(w2k.shape), full(w2vt.shape), full((1, 64))],
        out_specs=(pl.BlockSpec((None, None, nc, 64), lambda bi, g: (bi, g, 0, 0)),
                   pl.BlockSpec((None, None, 64, nc), lambda bi, g: (bi, g, 0, 0))),
        out_shape=(jax.ShapeDtypeStruct((b, 2, nc, 64), BF16), jax.ShapeDtypeStruct((b, 2, 64, nc), BF16)),
        compiler_params=_cparams(("parallel", "parallel")), name="nsa_compress",
    )(y, ynext, y, ynext, pos_k, pos_v, w1k, w1v, w2k, w2vt, k_gain.reshape(1, 64))


def _cmpsel_body(qt_ref, kc_ref, vct_ref, oc_ref, bias_ref):
    tq = qt_ref.shape[1]
    nc = kc_ref.shape[0]
    nblk = bias_ref.shape[0]
    t0 = pl.program_id(2) * tq
    n_idx = lax.broadcasted_iota(jnp.int32, (nc, tq), 0)
    t_idx = t0 + lax.broadcasted_iota(jnp.int32, (nc, tq), 1)
    mask = n_idx * CMP_STRIDE + (CMP_BLOCK - 1) <= t_idx
    kc = kc_ref[...]
    vct = vct_ref[...]
    pcs = jnp.zeros((nc, tq), F32)
    for hh in range(NSA_HPG):
        s = jnp.where(mask, _dot(kc, qt_ref[hh * 64:(hh + 1) * 64, :]), NEG)
        m = jnp.max(s, axis=0, keepdims=True)
        p = jnp.where(mask, jnp.exp(s - m), 0.0)
        l = jnp.sum(p, axis=0, keepdims=True)
        pc = p * (1.0 / jnp.maximum(l, 1e-30))
        oc_ref[hh * 64:(hh + 1) * 64, :] = _dot(vct, pc.astype(BF16))
        pcs = pcs + pc

    jj = lax.broadcasted_iota(jnp.int32, (nblk, nc), 0)
    nn = lax.broadcasted_iota(jnp.int32, (nblk, nc), 1)
    ov = jnp.where((nn * CMP_STRIDE < jj * SLC_BLOCK + SLC_BLOCK)
                   & (nn * CMP_STRIDE + CMP_BLOCK > jj * SLC_BLOCK), 1.0, 0.0).astype(BF16)
    hi, lo = _split2(pcs)
    imp = _dot(ov, hi) + _dot(ov, lo)

    j = lax.broadcasted_iota(jnp.int32, (nblk, tq), 0)
    cur = (t0 + lax.broadcasted_iota(jnp.int32, (nblk, tq), 1)) >> SLC_SHIFT
    forced = (j == 0) | (j == cur) | (j == cur - 1)
    imp = jnp.where(j <= cur, jnp.where(forced, FORCED_SCORE, imp), -1.0)
    jf = j.astype(F32)
    bias = jnp.full((nblk, tq), NEG, F32)
    for _ in range(min(SLC_TOPK, nblk)):
        v = jnp.max(imp, axis=0, keepdims=True)
        first = jnp.min(jnp.where(imp == v, jf, float(nblk)), axis=0, keepdims=True)
        pick = jf == first
        bias = jnp.where(pick, jnp.where(v >= 0.0, 0.0, NEG), bias)
        imp = jnp.where(pick, -3e38, imp)
    bias_ref[...] = bias.astype(BF16)


def _cmp_select(qt, kc, vct, *, tq=TOK_TILE):
    b, _, s = qt.shape
    nc = kc.shape[2]
    nblk = MAX_BLOCKS
    assert s // SLC_BLOCK <= MAX_BLOCKS and s // SLC_BLOCK >= SLC_TOPK
    return pl.pallas_call(
        _cmpsel_body, grid=(b, NSA_GROUPS, s // tq),
        in_specs=[pl.BlockSpec((None, 256, tq), lambda bi, g, i: (bi, g, i)),
                  pl.BlockSpec((None, None, nc, 64), lambda bi, g, i: (bi, g, 0, 0)),
                  pl.BlockSpec((None, None, 64, nc), lambda bi, g, i: (bi, g, 0, 0))],
        out_specs=(pl.BlockSpec((None, 256, tq), lambda bi, g, i: (bi, g, i)),
                   pl.BlockSpec((None, None, nblk, tq), lambda bi, g, i: (bi, g, 0, i))),
        out_shape=(jax.ShapeDtypeStruct((b, 512, s), F32),
                   jax.ShapeDtypeStruct((b, NSA_GROUPS, nblk, s), BF16)),
        compiler_params=_cparams(("parallel", "parallel", "parallel")), name="nsa_compressed_select",
    )(qt, kc, vct)


def _flash_step(s_ref, vt, m_ref, acc_ref, mask):
    def scores():
        return s_ref[...] if mask is None else jnp.where(mask, s_ref[...], NEG)

    m_old = m_ref[...]
    m_new = jnp.maximum(m_old, jnp.max(scores(), axis=0, keepdims=True))
    p = jnp.exp(scores() - m_new)
    if mask is not None:
        p = jnp.where(mask, p, 0.0)
    acc_ref[...] = jnp.exp(m_old - m_new) * acc_ref[...] + _dot(vt, p.astype(BF16))
    m_ref[...] = m_new


def _slcwin_body(qt_ref, bias_ref, kaug_ref, vst_ref, vwt_ref, oc_ref, gt_ref, y_ref,
                 qs_scr, qw_scr, ms_scr, as_scr, mw_scr, aw_scr, sa_scr, sb_scr):
    tq = qt_ref.shape[1]
    tk = vst_ref.shape[2]
    assert tq == tk and WINDOW <= tk
    diag = pl.program_id(2)

    bias = bias_ref[...]
    zeros = jnp.zeros((64, tq), BF16)
    for hh in range(NSA_HPG):
        qh = qt_ref[hh * 64:(hh + 1) * 64, :]
        qs_scr[hh, 0:64, :] = qh
        qs_scr[hh, 64:128, :] = zeros
        qs_scr[hh, 128:256, :] = bias
        qw_scr[hh, 0:64, :] = zeros
        qw_scr[hh, 64:128, :] = qh
    ms_scr[...] = jnp.full(ms_scr.shape, NEG, F32)
    mw_scr[...] = jnp.full(mw_scr.shape, NEG, F32)
    as_scr[...] = jnp.zeros(as_scr.shape, F32)
    aw_scr[...] = jnp.zeros(aw_scr.shape, F32)

    def sel_scores(kt, buf):
        k = kaug_ref[pl.ds(pl.multiple_of(kt * tk, tk), tk), :]
        for hh in range(NSA_HPG):
            buf[hh] = _dot(k, qs_scr[hh])

    def win_scores(kt, buf):
        k = kaug_ref[pl.ds(pl.multiple_of(kt * tk, tk), tk), 0:128]
        for hh in range(NSA_HPG):
            buf[hh] = _dot(k, qw_scr[hh])

    def sel_softmax(kt, buf, mask):
        v = vst_ref[kt]
        for hh in range(NSA_HPG):
            _flash_step(buf.at[hh], v, ms_scr.at[hh], as_scr.at[hh], mask)

    def win_softmax(kt, buf, mask):
        v = vwt_ref[kt]
        for hh in range(NSA_HPG):
            _flash_step(buf.at[hh], v, mw_scr.at[hh], aw_scr.at[hh], mask)

    odd = (diag & 1) == 1

    @pl.when(jnp.logical_not(odd))
    def _():
        sel_scores(0, sa_scr)

    @pl.when(odd)
    def _():
        sel_scores(0, sb_scr)
        sel_scores(1, sa_scr)
        sel_softmax(0, sb_scr, None)

    first = diag & 1

    def tile_pair(j, carry):
        kt = first + 2 * j
        sel_scores(kt + 1, sb_scr)
        sel_softmax(kt, sa_scr, None)
        sel_scores(kt + 2, sa_scr)
        sel_softmax(kt + 1, sb_scr, None)
        return carry

    lax.fori_loop(0, (diag - first) >> 1, tile_pair, 0)

    key_rel = lax.broadcasted_iota(jnp.int32, (tk, tq), 0)
    t_rel = lax.broadcasted_iota(jnp.int32, (tk, tq), 1)

    @pl.when(diag > 0)
    def _():
        win_scores(diag - 1, sb_scr)
        win_softmax(diag - 1, sb_scr, t_rel + tk - key_rel < WINDOW)

    causal = key_rel <= t_rel
    win_scores(diag, sb_scr)
    sel_softmax(diag, sa_scr, causal)
    win_softmax(diag, sb_scr, causal)

    gt = gt_ref[...]
    for hh in range(NSA_HPG):
        o_s = as_scr[hh, 0:64, :] * (1.0 / as_scr[hh, 64:65, :])
        o_w = aw_scr[hh, 0:64, :] * (1.0 / aw_scr[hh, 64:65, :])
        y_ref[hh * 64:(hh + 1) * 64, :] = (gt[3 * hh:3 * hh + 1] * oc_ref[hh * 64:(hh + 1) * 64, :]
                                          + gt[3 * hh + 1:3 * hh + 2] * o_s
                                          + gt[3 * hh + 2:3 * hh + 3] * o_w)


def _slc_win(qt, bias, kaug, vst, vwt, oc, gt):
    b, _, s = qt.shape
    nblk = bias.shape[2]
    ns, tk = vst.shape[2], vst.shape[4]
    tq = tk
    qblk = pl.BlockSpec((None, 256, tq), lambda bi, g, i: (bi, g, i))
    vblk = pl.BlockSpec((None, None, ns, V_ROWS, tk), lambda bi, g, i: (bi, g, 0, 0, 0))
    return pl.pallas_call(
        _slcwin_body, grid=(b, NSA_GROUPS, s // tq),
        in_specs=[qblk,
                  pl.BlockSpec((None, None, nblk, tq), lambda bi, g, i: (bi, g, 0, i)),
                  pl.BlockSpec((None, None, s, 256), lambda bi, g, i: (bi, g, 0, 0)),
                  vblk, vblk, qblk,
                  pl.BlockSpec((None, None, GATE_ROWS, tq), lambda bi, g, i: (bi, g, 0, i))],
        out_specs=qblk,
        out_shape=jax.ShapeDtypeStruct((b, 512, s), F32),
        scratch_shapes=[pltpu.VMEM((NSA_HPG, 256, tq), BF16), pltpu.VMEM((NSA_HPG, 128, tq), BF16),
                        pltpu.VMEM((NSA_HPG, 1, tq), F32), pltpu.VMEM((NSA_HPG, V_ROWS, tq), F32),
                        pltpu.VMEM((NSA_HPG, 1, tq), F32), pltpu.VMEM((NSA_HPG, V_ROWS, tq), F32),
                        pltpu.VMEM((NSA_HPG, tk, tq), F32), pltpu.VMEM((NSA_HPG, tk, tq), F32)],
        compiler_params=_cparams(("parallel", "parallel", "arbitrary")), name="nsa_selected_window",
    )(qt, bias, kaug, vst, vwt, oc, gt.reshape(b, NSA_GROUPS, GATE_ROWS, s))


def _hgrn_consts():
    c = HGRN_CHUNK
    t = np.arange(c)
    lower = (t[None, :] <= t[:, None]).astype(np.float32)
    rows = [lower]
    masks = []
    for half in HGRN_LEVELS:
        mid = (t // (2 * half)) * (2 * half) + half - 1
        rows.append(lower[mid])
        same = (t[:, None] // (2 * half)) == (t[None, :] // (2 * half))
        right = (t[:, None] & half) != 0
        left = (t[None, :] & half) == 0
        masks.append((same & right & left).astype(np.float32))
    masks.append(np.eye(c, dtype=np.float32))
    mall = np.concatenate(rows, axis=0)
    lvl = np.stack([np.tile(mk, (HGRN_HEADS, 1)) for mk in masks])
    bdm = np.kron(np.eye(HGRN_HEADS), np.ones((64, 64), np.float32))
    return jnp.asarray(mall, BF16), jnp.asarray(lvl, F32), jnp.asarray(bdm, F32), jnp.asarray(bdm / 64, BF16)


def _hgrn_body(hg_ref, lb_ref, og_ref, mall_ref, lvl_ref, bdm_ref, bdn_ref, y_ref, st_scr):
    c = HGRN_CHUNK
    w = HGRN_WIDTH

    @pl.when(pl.program_id(1) == 0)
    def _():
        st_scr[...] = jnp.zeros(st_scr.shape, F32)

    lb = lb_ref[...]
    lane = lax.broadcasted_iota(jnp.int32, (c, w), 1)
    trow = lax.broadcasted_iota(jnp.int32, (c, w), 0)
    head_masks = [(lane >> 6) == hh for hh in range(HGRN_HEADS)]
    mall = mall_ref[...]

    def stack_heads(x):
        return jnp.concatenate([jnp.where(hm, x, 0.0) for hm in head_masks], axis=0).astype(BF16)

    for ci in range(hg_ref.shape[0] // c):
        rows = slice(ci * c, (ci + 1) * c)
        qa = _silu(hg_ref[rows, 0:w]) * (HEAD_DIM ** -0.5)
        fg = lb + (1.0 - lb) * _sigmoid(hg_ref[rows, w:2 * w])
        kk = 1.0 - fg
        v16 = hg_ref[rows, 2 * w:3 * w].astype(BF16)
        l_hi, l_mid, l_lo = _split3(jnp.log(fg))
        r_all = _dot(mall, l_hi) + _dot(mall, l_mid) + _dot(mall, l_lo)
        bcum = r_all[0:c]
        b_last = bcum[c - 1:c, :]

        attn = lvl_ref[len(HGRN_LEVELS)] * _dot_nt(stack_heads(qa), kk.astype(BF16))
        for li, half in enumerate(HGRN_LEVELS):
            ref_pt = r_all[(li + 1) * c:(li + 2) * c]
            right = (trow & half) != 0
            e = jnp.exp(jnp.where(right, bcum - ref_pt, ref_pt - bcum))
            qt = jnp.where(right, qa * e, 0.0)
            kt = jnp.where(right, 0.0, kk * e).astype(BF16)
            attn = attn + lvl_ref[li] * _dot_nt(stack_heads(qt), kt)
        intra = jnp.zeros((c, w), F32)
        attn16 = attn.astype(BF16)
        for hh in range(HGRN_HEADS):
            intra = intra + _dot(attn16[hh * c:(hh + 1) * c], jnp.where(head_masks[hh], v16, 0))

        st = st_scr[...]
        inter = _dot_nt((qa * jnp.exp(bcum)).astype(BF16), st.astype(BF16))
        kl = (kk * jnp.exp(b_last - bcum)).astype(BF16)
        st_scr[...] = st * jnp.exp(b_last) + bdm_ref[...] * _dot_tn(v16, kl)

        o = inter + intra
        hi, lo = _split2(o * o)
        ms = _dot(hi, bdn_ref[...]) + _dot(lo, bdn_ref[...])
        y_ref[rows, :] = o * lax.rsqrt(ms + EPS) * og_ref[...] * _silu(hg_ref[rows, 3 * w:4 * w])


def _hgrn(hg, lower_bound, out_gain):
    b, s, _ = hg.shape
    rows = HGRN_CB * HGRN_CHUNK
    mall, lvl, bdm, bdn = _hgrn_consts()
    full = lambda shape: pl.BlockSpec(shape, lambda bi, i: (0,) * len(shape))
    return pl.pallas_call(
        _hgrn_body, grid=(b, s // rows),
        in_specs=[pl.BlockSpec((None, rows, 4 * HGRN_WIDTH), lambda bi, i: (bi, i, 0)),
                  full((1, HGRN_WIDTH)), full((1, HGRN_WIDTH)),
                  full(mall.shape), full(lvl.shape), full(bdm.shape), full(bdn.shape)],
        out_specs=pl.BlockSpec((None, rows, HGRN_WIDTH), lambda bi, i: (bi, i, 0)),
        out_shape=jax.ShapeDtypeStruct((b, s, HGRN_WIDTH), F32),
        scratch_shapes=[pltpu.VMEM((HGRN_WIDTH, HGRN_WIDTH), F32)],
        compiler_params=_cparams(("parallel", "arbitrary")), name="hgrn2_chunks",
    )(hg, lower_bound.reshape(1, -1), out_gain.reshape(1, -1), mall, lvl, bdm, bdn)


def _memkv_body(mem_ref, mg_ref, wk_ref, wvt_ref, kg_ref, kh_ref, vht_ref):
    m = mem_ref.shape[0]
    mn = _rms_rows(mem_ref[...], mg_ref[...]).astype(BF16)
    k = _dot(mn, wk_ref[...])
    vt = _dot_nt(wvt_ref[...], mn)
    ones_rows = (lax.broadcasted_iota(jnp.int32, (V_ROWS - 64, m), 0) == 0).astype(BF16)
    for hh in range(MEM_HEADS):
        kh_ref[hh] = _rms_rows(k[:, hh * 64:(hh + 1) * 64], kg_ref[...]).astype(BF16)
        vht_ref[hh, 0:64, :] = vt[hh * 64:(hh + 1) * 64].astype(BF16)
        vht_ref[hh, 64:V_ROWS, :] = ones_rows


def _mem_kv(mem, mem_gain, wk, wvt, k_gain):
    b, m, d = mem.shape
    full = lambda shape: pl.BlockSpec(shape, lambda bi: (0,) * len(shape))
    return pl.pallas_call(
        _memkv_body, grid=(b,),
        in_specs=[pl.BlockSpec((None, m, d), lambda bi: (bi, 0, 0)), full((1, d)),
                  full(wk.shape), full(wvt.shape), full((1, 64))],
        out_specs=(pl.BlockSpec((None, MEM_HEADS, m, 64), lambda bi: (bi, 0, 0, 0)),
                   pl.BlockSpec((None, MEM_HEADS, V_ROWS, m), lambda bi: (bi, 0, 0, 0))),
        out_shape=(jax.ShapeDtypeStruct((b, MEM_HEADS, m, 64), BF16),
                   jax.ShapeDtypeStruct((b, MEM_HEADS, V_ROWS, m), BF16)),
        compiler_params=_cparams(("parallel",)), name="memory_kv",
    )(mem, mem_gain.reshape(1, d), wk, wvt, k_gain.reshape(1, 64))


def _memattn_body(qmt_ref, kh_ref, vht_ref, y_ref):
    for hh in range(MEM_HEADS):
        s = _dot(kh_ref[hh], qmt_ref[hh * 64:(hh + 1) * 64, :])
        p = jnp.exp(s - jnp.max(s, axis=0, keepdims=True))
        o = _dot(vht_ref[hh], p.astype(BF16))
        y_ref[hh * 64:(hh + 1) * 64, :] = o[0:64] * (1.0 / o[64:65])


def _mem_attn(qmt, kh, vht, *, tq=TOK_TILE):
    b, _, s = qmt.shape
    m = kh.shape[2]
    return pl.pallas_call(
        _memattn_body, grid=(b, s // tq),
        in_specs=[pl.BlockSpec((None, MEM_WIDTH, tq), lambda bi, i: (bi, 0, i)),
                  pl.BlockSpec((None, MEM_HEADS, m, 64), lambda bi, i: (bi, 0, 0, 0)),
                  pl.BlockSpec((None, MEM_HEADS, V_ROWS, m), lambda bi, i: (bi, 0, 0, 0))],
        out_specs=pl.BlockSpec((None, MEM_WIDTH, tq), lambda bi, i: (bi, 0, i)),
        out_shape=jax.ShapeDtypeStruct((b, MEM_WIDTH, s), F32),
        compiler_params=_cparams(("parallel", "parallel")), name="memory_attention",
    )(qmt, kh, vht)


def _out_body(x_ref, ynt_ref, yh_ref, ymt_ref, ng_ref, mg_ref, wo_ref, o_ref):
    nsa = _rms_cols(ynt_ref[...], ng_ref[...]).astype(BF16)
    mem = _rms_cols(ymt_ref[...], mg_ref[...]).astype(BF16)
    acc = _dot_tn(nsa, wo_ref[0:NSA_WIDTH, :])
    acc = acc + _dot(yh_ref[...].astype(BF16), wo_ref[NSA_WIDTH:NSA_WIDTH + HGRN_WIDTH, :])
    acc = acc + _dot_tn(mem, wo_ref[NSA_WIDTH + HGRN_WIDTH:, :])
    o_ref[...] = x_ref[...] + acc


def _out_proj(x3d, ynt, yh, ymt, nsa_gain, mem_gain, wo, *, tm=TOK_TILE):
    b, s, d = x3d.shape
    full = lambda shape: pl.BlockSpec(shape, lambda bi, i: (0,) * len(shape))
    return pl.pallas_call(
        _out_body, grid=(b, s // tm),
        in_specs=[pl.BlockSpec((None, tm, d), lambda bi, i: (bi, i, 0)),
                  pl.BlockSpec((None, NSA_WIDTH, tm), lambda bi, i: (bi, 0, i)),
                  pl.BlockSpec((None, tm, HGRN_WIDTH), lambda bi, i: (bi, i, 0)),
                  pl.BlockSpec((None, MEM_WIDTH, tm), lambda bi, i: (bi, 0, i)),
                  full((NSA_WIDTH, 1)), full((MEM_WIDTH, 1)), full(wo.shape)],
        out_specs=pl.BlockSpec((None, tm, d), lambda bi, i: (bi, i, 0)),
        out_shape=jax.ShapeDtypeStruct((b, s, d), F32),
        compiler_params=_cparams(("parallel", "parallel")), name="mix_out_projection",
    )(x3d, ynt, yh, ymt, nsa_gain.reshape(-1, 1), mem_gain.reshape(-1, 1), wo)


def _mix(x, mem, mix_norm, w_in, w_out, nsa_q_norm, nsa_k_norm, cmp_pos_k, cmp_w1_k, cmp_w2_k,
         cmp_pos_v, cmp_w1_v, cmp_w2_v, nsa_out_norm, lower_bound, hgrn_out_norm,
         mem_norm, mem_w_k, mem_w_v, mem_q_norm, mem_k_norm, mem_out_norm):
    b, s, d = x.shape
    sizes = (512, 128, 128, 128, 128, 128, 128, 24, 256, 256, 256, 256, 256)
    offs = np.concatenate([[0], np.cumsum(sizes)])
    col = lambda i: w_in[:, offs[i]:offs[i + 1]]
    (q_a, k_c, v_c, k_s, v_s, k_w, v_w, g_a, q_h, f_h, i_h, g_h, q_m) = [col(i) for i in range(13)]
    gpad = jnp.zeros((d, GATE_ROWS - 3 * NSA_HPG), w_in.dtype)
    wt = jnp.concatenate([q_a, v_s, v_w, g_a[:, :3 * NSA_HPG], gpad, g_a[:, 3 * NSA_HPG:], gpad, q_m],
                         axis=1).T.astype(BF16)
    wn = jnp.concatenate([k_c, v_c, k_s, k_w, q_h, f_h, i_h, g_h], axis=1).astype(BF16)

    qt, vst, vwt, gt, qmt, kaug, kvc, hg = _proj(
        x, mix_norm, wt, wn, nsa_q_norm, nsa_k_norm, mem_q_norm, _rope_tables(s))

    y = kvc.reshape(b, s, 4, 64).transpose(0, 2, 1, 3).reshape(b, 4, s // CMP_STRIDE, CMP_STRIDE * 64)
    ynext = jnp.concatenate([y[:, :, 1:], jnp.zeros_like(y[:, :, :1])], axis=2)
    kc, vct = _compress(y, ynext, cmp_pos_k.reshape(2, -1), cmp_pos_v.reshape(2, -1),
                        cmp_w1_k.astype(BF16), cmp_w1_v.astype(BF16), cmp_w2_k.astype(BF16),
                        cmp_w2_v.T.astype(BF16), nsa_k_norm)
    oc, bias = _cmp_select(qt, kc, vct)
    y_nsa = _slc_win(qt, bias, kaug, vst, vwt, oc, gt)

    y_hgrn = _hgrn(hg, lower_bound, hgrn_out_norm)

    kh, vht = _mem_kv(mem, mem_norm, mem_w_k.astype(BF16), mem_w_v.T.astype(BF16), mem_k_norm)
    y_mem = _mem_attn(qmt, kh, vht)

    return _out_proj(x, y_nsa, y_hgrn, y_mem, nsa_out_norm, mem_out_norm, w_out.astype(BF16))


def kernel(x, mem, ffn1_norm, ffn1_w_gate, ffn1_w_up, ffn1_w_down, mix_norm, w_in, w_out, nsa_q_norm, nsa_k_norm, cmp_pos_k, cmp_w1_k, cmp_w2_k, cmp_pos_v, cmp_w1_v, cmp_w2_v, nsa_out_norm, hgrn_lb_logits, hgrn_out_norm, mem_norm, mem_w_k, mem_w_v, mem_q_norm, mem_k_norm, mem_out_norm, ffn2_norm, ffn2_w_gate, ffn2_w_up, ffn2_w_down):
    b, s, d = x.shape
    depth = ffn1_norm.shape[0]
    lower_bounds = jnp.cumsum(jax.nn.softmax(hgrn_lb_logits.astype(F32), axis=0), axis=0)
    for l in range(depth):
        x = _ffn(x.reshape(b * s, d), ffn1_norm[l], ffn1_w_gate[l].astype(BF16),
                 ffn1_w_up[l].astype(BF16), ffn1_w_down[l].astype(BF16)).reshape(b, s, d)
        x = _mix(x, mem, mix_norm[l], w_in[l], w_out[l], nsa_q_norm[l], nsa_k_norm[l],
                 cmp_pos_k[l], cmp_w1_k[l], cmp_w2_k[l], cmp_pos_v[l], cmp_w1_v[l], cmp_w2_v[l],
                 nsa_out_norm[l], lower_bounds[l], hgrn_out_norm[l],
                 mem_norm[l], mem_w_k[l], mem_w_v[l], mem_q_norm[l], mem_k_norm[l], mem_out_norm[l])
        x = _ffn(x.reshape(b * s, d), ffn2_norm[l], ffn2_w_gate[l].astype(BF16),
                 ffn2_w_up[l].astype(BF16), ffn2_w_down[l].astype(BF16)).reshape(b, s, d)
    return x
```

```python
import functools

import numpy as np
import jax
import jax.numpy as jnp
from jax import lax
from jax.experimental import pallas as pl
from jax.experimental.pallas import tpu as pltpu

F32 = jnp.float32
BF16 = jnp.bfloat16

HEAD_DIM = 64
ROT_DIM = 16
ROT_HALF = 8
ROPE_THETA = 500000.0
NSA_HEADS = 8
NSA_GROUPS = 2
NSA_HPG = 4
CMP_BLOCK = 32
CMP_STRIDE = 16
SLC_BLOCK = 64
SLC_SHIFT = 6
SLC_TOPK = 16
WINDOW = 512
FORCED_SCORE = 1e4
HGRN_HEADS = 4
HGRN_CHUNK = 64
HGRN_WIDTH = 256
MEM_HEADS = 4
MEM_WIDTH = 256
NSA_WIDTH = 512
EPS = 1e-6
NEG = -1e30
QK_SCALE_LOG2 = HEAD_DIM ** -0.5 * 1.4426950408889634

VMEM_LIMIT = 56 * 1024 * 1024
MAX_BLOCKS = 128
GATE_ROWS = 16
V_ROWS = 80
TOK_TILE = 512
HGRN_CB = 4
HGRN_LEVELS = (32, 16, 8, 4, 2, 1)

NT_DIMS = (((1,), (1,)), ((), ()))
TN_DIMS = (((0,), (0,)), ((), ()))


def _cparams(sem):
    return pltpu.CompilerParams(dimension_semantics=sem, vmem_limit_bytes=VMEM_LIMIT)


def _dot(a, b):
    return jnp.dot(a, b, preferred_element_type=F32)


def _dot_nt(a, b):
    return lax.dot_general(a, b, NT_DIMS, preferred_element_type=F32)


def _dot_tn(a, b):
    return lax.dot_general(a, b, TN_DIMS, preferred_element_type=F32)


def _sigmoid(x):
    return 1.0 / (1.0 + jnp.exp(-x))


def _silu(x):
    return x * _sigmoid(x)


def _split2(x):
    hi = x.astype(BF16)
    lo = (x - hi.astype(F32)).astype(BF16)
    return hi, lo


def _split3(x):
    hi = x.astype(BF16)
    r1 = x - hi.astype(F32)
    mid = r1.astype(BF16)
    lo = (r1 - mid.astype(F32)).astype(BF16)
    return hi, mid, lo


def _rms_rows(x, gain_row):
    ms = jnp.mean(x * x, axis=-1, keepdims=True)
    return x * lax.rsqrt(ms + EPS) * gain_row


def _rms_cols(x, gain_col):
    ms = jnp.mean(x * x, axis=0, keepdims=True)
    return x * lax.rsqrt(ms + EPS) * gain_col


def _seg_mean_sq(x, bd):
    hi, lo = _split2(x * x)
    return _dot(hi, bd) + _dot(lo, bd)


def _ffn_body(x_ref, g_ref, wg_ref, wu_ref, wd_ref, o_ref, a_scr, *, fc):
    x = x_ref[...]
    xn = _rms_rows(x, g_ref[...]).astype(BF16)
    d_ff = wg_ref.shape[1]
    for c in range(d_ff // fc):
        sl = slice(c * fc, (c + 1) * fc)
        g = _dot(xn, wg_ref[:, sl])
        u = _dot(xn, wu_ref[:, sl])
        a_scr[:, sl] = (_silu(g) * u).astype(BF16)
    o_ref[...] = x + 0.5 * _dot(a_scr[...], wd_ref[...])


def _ffn(x2d, gain, wg, wu, wd, *, tm=512, fc=256):
    t, d = x2d.shape
    d_ff = wg.shape[1]
    full = lambda shape: pl.BlockSpec(shape, lambda i: (0,) * len(shape))
    return pl.pallas_call(
        functools.partial(_ffn_body, fc=fc),
        grid=(t // tm,),
        in_specs=[pl.BlockSpec((tm, d), lambda i: (i, 0)), full((1, d)),
                  full((d, d_ff)), full((d, d_ff)), full((d_ff, d))],
        out_specs=pl.BlockSpec((tm, d), lambda i: (i, 0)),
        out_shape=jax.ShapeDtypeStruct((t, d), F32),
        scratch_shapes=[pltpu.VMEM((tm, d_ff), BF16)],
        compiler_params=_cparams(("parallel",)),
        name="ffn_half_step",
    )(x2d, gain.reshape(1, d), wg, wu, wd)


def _rope_cols(xn, cos, sin):
    x0, x1 = xn[0:ROT_HALF], xn[ROT_HALF:ROT_DIM]
    return jnp.concatenate([x0 * cos - x1 * sin, x1 * cos + x0 * sin, xn[ROT_DIM:]], axis=0)


def _rope_rows(x, cn, sa, sb):
    return x * cn + pltpu.roll(x, 128 - ROT_HALF, 1) * sa + pltpu.roll(x, ROT_HALF, 1) * sb


def _proj_body(x_ref, mg_ref, wt_ref, wn_ref, qg_ref, kg_ref, mqg_ref, cos_ref, sin_ref,
               cn_ref, sa_ref, sb_ref, bd_ref,
               qt_ref, vt_ref, gt_ref, qmt_ref, kaug_ref, kvc_ref, hg_ref):
    tm = x_ref.shape[0]
    h = _rms_rows(x_ref[...], mg_ref[...]).astype(BF16)

    pt = _dot_nt(wt_ref[...], h)
    cos, sin = cos_ref[...], sin_ref[...]
    qg = qg_ref[...]
    for hh in range(NSA_HEADS):
        xq = _rms_cols(pt[hh * 64:(hh + 1) * 64], qg)
        qt_ref[hh * 64:(hh + 1) * 64, :] = (_rope_cols(xq, cos, sin) * QK_SCALE_LOG2).astype(BF16)
    ones_rows = (lax.broadcasted_iota(jnp.int32, (V_ROWS - 64, tm), 0) == 0).astype(BF16)
    for g in range(NSA_GROUPS):
        for br in range(2):
            rows = 512 + br * 128 + g * 64
            vt_ref[g, br, 0:64, :] = pt[rows:rows + 64].astype(BF16)
            vt_ref[g, br, 64:V_ROWS, :] = ones_rows
    gt_ref[...] = _sigmoid(pt[768:800])
    mqg = mqg_ref[...]
    for hh in range(MEM_HEADS):
        xm = _rms_cols(pt[800 + hh * 64:864 + hh * 64], mqg)
        qmt_ref[hh * 64:(hh + 1) * 64, :] = (xm * QK_SCALE_LOG2).astype(BF16)

    pn = _dot(h, wn_ref[...])
    cn, sa, sb = cn_ref[...], sa_ref[...], sb_ref[...]
    bd, kg = bd_ref[...], kg_ref[...]
    kvc_ref[:, 0:128] = _rope_rows(pn[:, 0:128], cn, sa, sb)
    kvc_ref[:, 128:256] = pn[:, 128:256]
    ks = pn[:, 256:384]
    kw = pn[:, 384:512]
    ks = _rope_rows(ks * lax.rsqrt(_seg_mean_sq(ks, bd) + EPS) * kg, cn, sa, sb)
    kw = _rope_rows(kw * lax.rsqrt(_seg_mean_sq(kw, bd) + EPS) * kg, cn, sa, sb)
    lane = lax.broadcasted_iota(jnp.int32, (tm, 128), 1)
    row = lax.broadcasted_iota(jnp.int32, (tm, 128), 0)
    onehot = jnp.where(lane - 64 == (row >> SLC_SHIFT), 1.0, 0.0)
    lo_half = lane < 64
    kaug_ref[0, 0] = jnp.where(lo_half, ks, onehot).astype(BF16)
    kaug_ref[0, 1] = jnp.where(lo_half, kw, 0.0).astype(BF16)
    kaug_ref[1, 0] = jnp.where(lo_half, pltpu.roll(ks, 64, 1), onehot).astype(BF16)
    kaug_ref[1, 1] = jnp.where(lo_half, pltpu.roll(kw, 64, 1), 0.0).astype(BF16)
    hg_ref[...] = pn[:, 512:1536]


def _proj(x3d, mix_gain, wt, wn, q_gain, k_gain, mq_gain, rope):
    b, s, d = x3d.shape
    tm = TOK_TILE
    ns = s // tm
    cos_t, sin_t, cn, sa, sb = rope
    bd = jnp.asarray(np.kron(np.eye(2), np.full((64, 64), 1.0 / 64)), BF16)
    full = lambda shape: pl.BlockSpec(shape, lambda bi, i: (0,) * len(shape))
    out_shape = (
        jax.ShapeDtypeStruct((b, 512, s), BF16),
        jax.ShapeDtypeStruct((b, 2, 2, ns, V_ROWS, tm), BF16),
        jax.ShapeDtypeStruct((b, 32, s), F32),
        jax.ShapeDtypeStruct((b, 256, s), BF16),
        jax.ShapeDtypeStruct((b, 2, 2, s, 128), BF16),
        jax.ShapeDtypeStruct((b, s, 256), F32),
        jax.ShapeDtypeStruct((b, s, 1024), F32),
    )
    out_specs = (
        pl.BlockSpec((None, 512, tm), lambda bi, i: (bi, 0, i)),
        pl.BlockSpec((None, 2, 2, None, V_ROWS, tm), lambda bi, i: (bi, 0, 0, i, 0, 0)),
        pl.BlockSpec((None, 32, tm), lambda bi, i: (bi, 0, i)),
        pl.BlockSpec((None, 256, tm), lambda bi, i: (bi, 0, i)),
        pl.BlockSpec((None, 2, 2, tm, 128), lambda bi, i: (bi, 0, 0, i, 0)),
        pl.BlockSpec((None, tm, 256), lambda bi, i: (bi, i, 0)),
        pl.BlockSpec((None, tm, 1024), lambda bi, i: (bi, i, 0)),
    )
    in_specs = [
        pl.BlockSpec((None, tm, d), lambda bi, i: (bi, i, 0)),
        full((1, d)), full(wt.shape), full(wn.shape),
        full((64, 1)), full((1, 128)), full((64, 1)),
        pl.BlockSpec((ROT_HALF, tm), lambda bi, i: (0, i)),
        pl.BlockSpec((ROT_HALF, tm), lambda bi, i: (0, i)),
        pl.BlockSpec((tm, 128), lambda bi, i: (i, 0)),
        pl.BlockSpec((tm, 128), lambda bi, i: (i, 0)),
        pl.BlockSpec((tm, 128), lambda bi, i: (i, 0)),
        full((128, 128)),
    ]
    return pl.pallas_call(
        _proj_body, grid=(b, ns), in_specs=in_specs, out_specs=out_specs, out_shape=out_shape,
        compiler_params=_cparams(("parallel", "parallel")), name="mix_projection",
    )(x3d, mix_gain.reshape(1, d), wt, wn, q_gain.reshape(64, 1),
      jnp.tile(k_gain.reshape(1, 64), (1, 2)), mq_gain.reshape(64, 1), cos_t, sin_t, cn, sa, sb, bd)


def _rope_tables(s):
    pos = jnp.arange(s, dtype=F32)
    inv = ROPE_THETA ** (-(jnp.arange(0, ROT_DIM, 2, dtype=F32) / ROT_DIM))
    ang = pos[:, None] * inv[None, :]
    cos, sin = jnp.cos(ang), jnp.sin(ang)
    zeros = jnp.zeros((s, 64 - ROT_DIM), F32)
    cn = jnp.concatenate([cos, cos, jnp.ones((s, 64 - ROT_DIM), F32)], axis=1)
    sa = jnp.concatenate([-sin, jnp.zeros((s, ROT_HALF), F32), zeros], axis=1)
    sb = jnp.concatenate([jnp.zeros((s, ROT_HALF), F32), sin, zeros], axis=1)
    tile2 = lambda a: jnp.concatenate([a, a], axis=1)
    return cos.T, sin.T, tile2(cn), tile2(sa), tile2(sb)


def _cmp_body(yk_ref, ykn_ref, yv_ref, yvn_ref, pk_ref, pv_ref, w1k_ref, w1v_ref, w2k_ref, w2vt_ref,
              kg_ref, kc_ref, vct_ref):
    half = w1k_ref.shape[0] // 2

    def hidden(y_ref, yn_ref, p_ref, w1_ref):
        top = (y_ref[...] + p_ref[0:1, :]).astype(BF16)
        bot = (yn_ref[...] + p_ref[1:2, :]).astype(BF16)
        return _silu(_dot(top, w1_ref[0:half, :]) + _dot(bot, w1_ref[half:, :])).astype(BF16)

    kc = _dot(hidden(yk_ref, ykn_ref, pk_ref, w1k_ref), w2k_ref[...])
    kc_ref[...] = _rms_rows(kc, kg_ref[...]).astype(BF16)
    vct_ref[...] = _dot_nt(w2vt_ref[...], hidden(yv_ref, yvn_ref, pv_ref, w1v_ref)).astype(BF16)


def _compress(y, ynext, pos_k, pos_v, w1k, w1v, w2k, w2vt, k_gain):
    b, _, nc, width = y.shape
    ysp = lambda off: pl.BlockSpec((None, None, nc, width), lambda bi, g: (bi, g + off, 0, 0))
    full = lambda shape: pl.BlockSpec(shape, lambda bi, g: (0,) * len(shape))
    return pl.pallas_call(
        _cmp_body, grid=(b, 2),
        in_specs=[ysp(0), ysp(0), ysp(2), ysp(2), full(pos_k.shape), full(pos_v.shape),
                  full(w1k.shape), full(w1v.shape), full(w2k.shape), full(w2vt.shape), full((1, 64))],
        out_specs=(pl.BlockSpec((None, None, nc, 64), lambda bi, g: (bi, g, 0, 0)),
                   pl.BlockSpec((None, None, 64, nc), lambda bi, g: (bi, g, 0, 0))),
        out_shape=(jax.ShapeDtypeStruct((b, 2, nc, 64), BF16), jax.ShapeDtypeStruct((b, 2, 64, nc), BF16)),
        compiler_params=_cparams(("parallel", "parallel")), name="nsa_compress",
    )(y, ynext, y, ynext, pos_k, pos_v, w1k, w1v, w2k, w2vt, k_gain.reshape(1, 64))


CMP_CLASS_ROWS = 128


def _cmpsel_variant(nc, nblk, qt_ref, kc_ref, vct_ref, oc_ref, bias_ref, s_scr):
    tq = qt_ref.shape[1]
    t0 = pl.program_id(2) * tq
    n_idx = lax.broadcasted_iota(jnp.int32, (nc, tq), 0)
    t_idx = t0 + lax.broadcasted_iota(jnp.int32, (nc, tq), 1)
    mask_bias = jnp.where(n_idx * CMP_STRIDE + (CMP_BLOCK - 1) <= t_idx, 0.0, NEG)
    t_row = t0 + lax.broadcasted_iota(jnp.int32, (1, tq), 1)
    kc = kc_ref[0:nc, :]
    vct = vct_ref[:, 0:nc]
    for hh in range(NSA_HPG):
        s_scr[hh, 0:nc, :] = _dot(kc, qt_ref[hh * 64:(hh + 1) * 64, :]) + mask_bias
    pcs = jnp.zeros((nc, tq), F32)
    for hh in range(NSA_HPG):
        m = jnp.max(s_scr[hh, 0:nc, :], axis=0, keepdims=True)
        p = jnp.exp2(s_scr[hh, 0:nc, :] - m)
        l = jnp.sum(p, axis=0, keepdims=True)
        pc = p * jnp.where(t_row >= CMP_BLOCK - 1, 1.0 / l, 0.0)
        oc_ref[hh * 64:(hh + 1) * 64, :] = _dot(vct, pc.astype(BF16))
        pcs = pcs + pc

    jj = lax.broadcasted_iota(jnp.int32, (nblk, nc), 0)
    nn = lax.broadcasted_iota(jnp.int32, (nblk, nc), 1)
    ov = jnp.where((nn * CMP_STRIDE < jj * SLC_BLOCK + SLC_BLOCK)
                   & (nn * CMP_STRIDE + CMP_BLOCK > jj * SLC_BLOCK), 1.0, 0.0).astype(BF16)
    hi, lo = _split2(pcs)
    imp = _dot(ov, hi) + _dot(ov, lo)

    j = lax.broadcasted_iota(jnp.int32, (nblk, tq), 0)
    cur = (t0 + lax.broadcasted_iota(jnp.int32, (nblk, tq), 1)) >> SLC_SHIFT
    forced = (j == 0) | (j == cur) | (j == cur - 1)
    bias = jnp.where(forced & (j <= cur), 0.0, NEG)
    imp = jnp.where((j <= cur) & jnp.logical_not(forced), imp, -1.0)
    jf = j.astype(F32)
    for _ in range(SLC_TOPK - 3):
        v = jnp.max(imp, axis=0, keepdims=True)
        first = jnp.min(jnp.where(imp == v, jf, float(nblk)), axis=0, keepdims=True)
        pick = jf == first
        bias = jnp.where(pick & (v >= 0.0), 0.0, bias)
        imp = jnp.where(pick, -3e38, imp)
    bias_ref[0:nblk, :] = bias
    if nblk < bias_ref.shape[0]:
        bias_ref[nblk:, :] = jnp.full((bias_ref.shape[0] - nblk, tq), NEG, F32)


def _cmpsel_body(qt_ref, kc_ref, vct_ref, oc_ref, bias_ref, s_scr):
    tq = qt_ref.shape[1]
    nc_total = kc_ref.shape[0]
    tiles_per_class = CMP_CLASS_ROWS // (tq // CMP_STRIDE)
    cls = pl.program_id(2) // tiles_per_class
    for c in range(nc_total // CMP_CLASS_ROWS):
        nc = (c + 1) * CMP_CLASS_ROWS
        nblk = min(nc * CMP_STRIDE // SLC_BLOCK, bias_ref.shape[0])
        pl.when(cls == c)(functools.partial(_cmpsel_variant, nc, nblk, qt_ref, kc_ref, vct_ref,
                                            oc_ref, bias_ref, s_scr))


def _cmp_select(qt, kc, vct, *, tq=TOK_TILE):
    b, _, s = qt.shape
    nc = kc.shape[2]
    nblk = MAX_BLOCKS
    assert s // SLC_BLOCK <= MAX_BLOCKS and s // SLC_BLOCK >= SLC_TOPK and nc % CMP_CLASS_ROWS == 0
    return pl.pallas_call(
        _cmpsel_body, grid=(b, NSA_GROUPS, s // tq),
        in_specs=[pl.BlockSpec((None, 256, tq), lambda bi, g, i: (bi, g, i)),
                  pl.BlockSpec((None, None, nc, 64), lambda bi, g, i: (bi, g, 0, 0)),
                  pl.BlockSpec((None, None, 64, nc), lambda bi, g, i: (bi, g, 0, 0))],
        out_specs=(pl.BlockSpec((None, 256, tq), lambda bi, g, i: (bi, g, i)),
                   pl.BlockSpec((None, None, nblk, tq), lambda bi, g, i: (bi, g, 0, i))),
        out_shape=(jax.ShapeDtypeStruct((b, 512, s), F32),
                   jax.ShapeDtypeStruct((b, NSA_GROUPS, nblk, s), F32)),
        scratch_shapes=[pltpu.VMEM((NSA_HPG, nc, tq), F32)],
        compiler_params=_cparams(("parallel", "parallel", "parallel")), name="nsa_compressed_select",
    )(qt, kc, vct)


def _flash_step(s_ref, vt, m_ref, acc_ref):
    m_old = m_ref[...]
    m_new = jnp.maximum(m_old, jnp.max(s_ref[...], axis=0, keepdims=True))
    p = jnp.exp2(s_ref[...] - m_new)
    acc_ref[...] = jnp.exp2(m_old - m_new) * acc_ref[...] + _dot(vt, p.astype(BF16))
    m_ref[...] = m_new


SEL, WIN = 0, 1
BIAS_ROWS = 16


def _slcwin_body(qt_ref, bias_ref, kaug_ref, vt_ref, oc_ref, gt_ref, mb_ref, y_ref,
                 q_scr, m_scr, acc_scr, s_scr):
    tq = qt_ref.shape[1]
    tk = vt_ref.shape[3]
    assert tq == tk and WINDOW <= tk and tk == 8 * SLC_BLOCK
    diag = pl.program_id(2)

    zeros = jnp.zeros((64, tq), BF16)
    for br in (SEL, WIN):
        for hh in range(NSA_HPG):
            q_scr[br, hh, 0:64, :] = qt_ref[hh * 64:(hh + 1) * 64, :]
            q_scr[br, hh, 64:128, :] = zeros
    m_scr[...] = jnp.full(m_scr.shape, NEG, F32)
    acc_scr[...] = jnp.zeros(acc_scr.shape, F32)

    def set_selection_bias(kt):
        rows = bias_ref[pl.ds(pl.multiple_of(kt * 8, 8), 8), :]
        b16 = jnp.concatenate([rows, jnp.zeros_like(rows)], axis=0).astype(BF16)
        for hh in range(NSA_HPG):
            q_scr[SEL, hh, 64:64 + BIAS_ROWS, :] = b16

    def tile_step(br, kt, mask_bias):
        k = kaug_ref[br, pl.ds(pl.multiple_of(kt * tk, tk), tk), :]
        v = vt_ref[br, kt]
        for hh in range(NSA_HPG):
            s = _dot(k, q_scr[br, hh])
            s_scr[hh] = s if mask_bias is None else s + mask_bias
        for hh in range(NSA_HPG):
            _flash_step(s_scr.at[hh], v, m_scr.at[br, hh], acc_scr.at[br, hh])

    def full_tile(kt, carry):
        set_selection_bias(kt)
        tile_step(SEL, kt, None)
        return carry

    lax.fori_loop(0, diag, full_tile, 0)
    set_selection_bias(diag)

    def masked_tile(j, carry):
        first = j == 0
        br = jnp.where(j == 1, SEL, WIN)
        kt = jnp.where(first, jnp.maximum(diag - 1, 0), diag)
        extra = jnp.where(first & (diag == 0), NEG, 0.0)
        tile_step(br, kt, mb_ref[jnp.where(first, 1, 0)] + extra)
        return carry

    lax.fori_loop(0, 3, masked_tile, 0)

    gt = gt_ref[...]
    for hh in range(NSA_HPG):
        o_s = acc_scr[SEL, hh, 0:64, :] * (1.0 / acc_scr[SEL, hh, 64:65, :])
        o_w = acc_scr[WIN, hh, 0:64, :] * (1.0 / acc_scr[WIN, hh, 64:65, :])
        y_ref[hh * 64:(hh + 1) * 64, :] = (gt[3 * hh:3 * hh + 1] * oc_ref[hh * 64:(hh + 1) * 64, :]
                                          + gt[3 * hh + 1:3 * hh + 2] * o_s
                                          + gt[3 * hh + 2:3 * hh + 3] * o_w)


def _slc_win(qt, bias, kaug, vt, oc, gt):
    b, _, s = qt.shape
    nblk = bias.shape[2]
    ns, tk = vt.shape[3], vt.shape[5]
    tq = tk
    key_rel, t_rel = np.arange(tk)[:, None], np.arange(tq)[None, :]
    mask_bias = jnp.asarray(np.stack([np.where(key_rel <= t_rel, 0.0, NEG),
                                      np.where(t_rel + tk - key_rel < WINDOW, 0.0, NEG)]), F32)
    qblk = pl.BlockSpec((None, 256, tq), lambda bi, g, i: (bi, g, i))
    return pl.pallas_call(
        _slcwin_body, grid=(b, NSA_GROUPS, s // tq),
        in_specs=[qblk,
                  pl.BlockSpec((None, None, nblk, tq), lambda bi, g, i: (bi, g, 0, i)),
                  pl.BlockSpec((None, None, 2, s, 128), lambda bi, g, i: (bi, g, 0, 0, 0)),
                  pl.BlockSpec((None, None, 2, ns, V_ROWS, tk), lambda bi, g, i: (bi, g, 0, 0, 0, 0)),
                  qblk,
                  pl.BlockSpec((None, None, GATE_ROWS, tq), lambda bi, g, i: (bi, g, 0, i)),
                  pl.BlockSpec((2, tk, tq), lambda bi, g, i: (0, 0, 0))],
        out_specs=qblk,
        out_shape=jax.ShapeDtypeStruct((b, 512, s), F32),
        scratch_shapes=[pltpu.VMEM((2, NSA_HPG, 128, tq), BF16),
                        pltpu.VMEM((2, NSA_HPG, 1, tq), F32), pltpu.VMEM((2, NSA_HPG, V_ROWS, tq), F32),
                        pltpu.VMEM((NSA_HPG, tk, tq), F32)],
        compiler_params=_cparams(("parallel", "parallel", "arbitrary")), name="nsa_selected_window",
    )(qt, bias, kaug, vt, oc, gt.reshape(b, NSA_GROUPS, GATE_ROWS, s), mask_bias)


def _hgrn_consts():
    c = HGRN_CHUNK
    t = np.arange(c)
    lower = (t[None, :] <= t[:, None]).astype(np.float32)
    rows = [lower]
    masks = []
    for half in HGRN_LEVELS:
        mid = (t // (2 * half)) * (2 * half) + half - 1
        rows.append(lower[mid])
        same = (t[:, None] // (2 * half)) == (t[None, :] // (2 * half))
        right = (t[:, None] & half) != 0
        left = (t[None, :] & half) == 0
        masks.append((same & right & left).astype(np.float32))
    masks.append(np.eye(c, dtype=np.float32))
    mall = np.concatenate(rows, axis=0)
    lvl = np.stack([np.tile(mk, (HGRN_HEADS, 1)) for mk in masks])
    bdm = np.kron(np.eye(HGRN_HEADS), np.ones((64, 64), np.float32))
    return jnp.asarray(mall, BF16), jnp.asarray(lvl, F32), jnp.asarray(bdm, F32), jnp.asarray(bdm / 64, BF16)


def _hgrn_body(hg_ref, lb_ref, og_ref, mall_ref, lvl_ref, bdm_ref, bdn_ref, y_ref, st_scr):
    c = HGRN_CHUNK
    w = HGRN_WIDTH

    @pl.when(pl.program_id(1) == 0)
    def _():
        st_scr[...] = jnp.zeros(st_scr.shape, F32)

    lb = lb_ref[...]
    lane = lax.broadcasted_iota(jnp.int32, (c, w), 1)
    trow = lax.broadcasted_iota(jnp.int32, (c, w), 0)
    head_masks = [(lane >> 6) == hh for hh in range(HGRN_HEADS)]
    mall = mall_ref[...]

    def stack_heads(x):
        return jnp.concatenate([jnp.where(hm, x, 0.0) for hm in head_masks], axis=0).astype(BF16)

    for ci in range(hg_ref.shape[0] // c):
        rows = slice(ci * c, (ci + 1) * c)
        qa = _silu(hg_ref[rows, 0:w]) * (HEAD_DIM ** -0.5)
        fg = lb + (1.0 - lb) * _sigmoid(hg_ref[rows, w:2 * w])
        kk = 1.0 - fg
        v16 = hg_ref[rows, 2 * w:3 * w].astype(BF16)
        l_hi, l_mid, l_lo = _split3(jnp.log(fg))
        r_all = _dot(mall, l_hi) + _dot(mall, l_mid) + _dot(mall, l_lo)
        bcum = r_all[0:c]
        b_last = bcum[c - 1:c, :]

        attn = lvl_ref[len(HGRN_LEVELS)] * _dot_nt(stack_heads(qa), kk.astype(BF16))
        for li, half in enumerate(HGRN_LEVELS):
            ref_pt = r_all[(li + 1) * c:(li + 2) * c]
            right = (trow & half) != 0
            e = jnp.exp(jnp.where(right, bcum - ref_pt, ref_pt - bcum))
            qt = jnp.where(right, qa * e, 0.0)
            kt = jnp.where(right, 0.0, kk * e).astype(BF16)
            attn = attn + lvl_ref[li] * _dot_nt(stack_heads(qt), kt)
        intra = jnp.zeros((c, w), F32)
        attn16 = attn.astype(BF16)
        for hh in range(HGRN_HEADS):
            intra = intra + _dot(attn16[hh * c:(hh + 1) * c], jnp.where(head_masks[hh], v16, 0))

        st = st_scr[...]
        inter = _dot_nt((qa * jnp.exp(bcum)).astype(BF16), st.astype(BF16))
        kl = (kk * jnp.exp(b_last - bcum)).astype(BF16)
        st_scr[...] = st * jnp.exp(b_last) + bdm_ref[...] * _dot_tn(v16, kl)

        o = inter + intra
        hi, lo = _split2(o * o)
        ms = _dot(hi, bdn_ref[...]) + _dot(lo, bdn_ref[...])
        y_ref[rows, :] = o * lax.rsqrt(ms + EPS) * og_ref[...] * _silu(hg_ref[rows, 3 * w:4 * w])


def _hgrn(hg, lower_bound, out_gain):
    b, s, _ = hg.shape
    rows = HGRN_CB * HGRN_CHUNK
    mall, lvl, bdm, bdn = _hgrn_consts()
    full = lambda shape: pl.BlockSpec(shape, lambda bi, i: (0,) * len(shape))
    return pl.pallas_call(
        _hgrn_body, grid=(b, s // rows),
        in_specs=[pl.BlockSpec((None, rows, 4 * HGRN_WIDTH), lambda bi, i: (bi, i, 0)),
                  full((1, HGRN_WIDTH)), full((1, HGRN_WIDTH)),
                  full(mall.shape), full(lvl.shape), full(bdm.shape), full(bdn.shape)],
        out_specs=pl.BlockSpec((None, rows, HGRN_WIDTH), lambda bi, i: (bi, i, 0)),
        out_shape=jax.ShapeDtypeStruct((b, s, HGRN_WIDTH), F32),
        scratch_shapes=[pltpu.VMEM((HGRN_WIDTH, HGRN_WIDTH), F32)],
        compiler_params=_cparams(("parallel", "arbitrary")), name="hgrn2_chunks",
    )(hg, lower_bound.reshape(1, -1), out_gain.reshape(1, -1), mall, lvl, bdm, bdn)


def _memkv_body(mem_ref, mg_ref, wk_ref, wvt_ref, kg_ref, kh_ref, vht_ref):
    m = mem_ref.shape[0]
    mn = _rms_rows(mem_ref[...], mg_ref[...]).astype(BF16)
    k = _dot(mn, wk_ref[...])
    vt = _dot_nt(wvt_ref[...], mn)
    ones_rows = (lax.broadcasted_iota(jnp.int32, (V_ROWS - 64, m), 0) == 0).astype(BF16)
    for hh in range(MEM_HEADS):
        kh_ref[hh] = _rms_rows(k[:, hh * 64:(hh + 1) * 64], kg_ref[...]).astype(BF16)
        vht_ref[hh, 0:64, :] = vt[hh * 64:(hh + 1) * 64].astype(BF16)
        vht_ref[hh, 64:V_ROWS, :] = ones_rows


def _mem_kv(mem, mem_gain, wk, wvt, k_gain):
    b, m, d = mem.shape
    full = lambda shape: pl.BlockSpec(shape, lambda bi: (0,) * len(shape))
    return pl.pallas_call(
        _memkv_body, grid=(b,),
        in_specs=[pl.BlockSpec((None, m, d), lambda bi: (bi, 0, 0)), full((1, d)),
                  full(wk.shape), full(wvt.shape), full((1, 64))],
        out_specs=(pl.BlockSpec((None, MEM_HEADS, m, 64), lambda bi: (bi, 0, 0, 0)),
                   pl.BlockSpec((None, MEM_HEADS, V_ROWS, m), lambda bi: (bi, 0, 0, 0))),
        out_shape=(jax.ShapeDtypeStruct((b, MEM_HEADS, m, 64), BF16),
                   jax.ShapeDtypeStruct((b, MEM_HEADS, V_ROWS, m), BF16)),
        compiler_params=_cparams(("parallel",)), name="memory_kv",
    )(mem, mem_gain.reshape(1, d), wk, wvt, k_gain.reshape(1, 64))


def _memattn_body(qmt_ref, kh_ref, vht_ref, y_ref):
    for hh in range(MEM_HEADS):
        s = _dot(kh_ref[hh], qmt_ref[hh * 64:(hh + 1) * 64, :])
        p = jnp.exp2(s - jnp.max(s, axis=0, keepdims=True))
        o = _dot(vht_ref[hh], p.astype(BF16))
        y_ref[hh * 64:(hh + 1) * 64, :] = o[0:64] * (1.0 / o[64:65])


def _mem_attn(qmt, kh, vht, *, tq=TOK_TILE):
    b, _, s = qmt.shape
    m = kh.shape[2]
    return pl.pallas_call(
        _memattn_body, grid=(b, s // tq),
        in_specs=[pl.BlockSpec((None, MEM_WIDTH, tq), lambda bi, i: (bi, 0, i)),
                  pl.BlockSpec((None, MEM_HEADS, m, 64), lambda bi, i: (bi, 0, 0, 0)),
                  pl.BlockSpec((None, MEM_HEADS, V_ROWS, m), lambda bi, i: (bi, 0, 0, 0))],
        out_specs=pl.BlockSpec((None, MEM_WIDTH, tq), lambda bi, i: (bi, 0, i)),
        out_shape=jax.ShapeDtypeStruct((b, MEM_WIDTH, s), F32),
        compiler_params=_cparams(("parallel", "parallel")), name="memory_attention",
    )(qmt, kh, vht)


def _out_body(x_ref, ynt_ref, yh_ref, ymt_ref, ng_ref, mg_ref, wo_ref, o_ref):
    nsa = _rms_cols(ynt_ref[...], ng_ref[...]).astype(BF16)
    mem = _rms_cols(ymt_ref[...], mg_ref[...]).astype(BF16)
    acc = _dot_tn(nsa, wo_ref[0:NSA_WIDTH, :])
    acc = acc + _dot(yh_ref[...].astype(BF16), wo_ref[NSA_WIDTH:NSA_WIDTH + HGRN_WIDTH, :])
    acc = acc + _dot_tn(mem, wo_ref[NSA_WIDTH + HGRN_WIDTH:, :])
    o_ref[...] = x_ref[...] + acc


def _out_proj(x3d, ynt, yh, ymt, nsa_gain, mem_gain, wo, *, tm=TOK_TILE):
    b, s, d = x3d.shape
    full = lambda shape: pl.BlockSpec(shape, lambda bi, i: (0,) * len(shape))
    return pl.pallas_call(
        _out_body, grid=(b, s // tm),
        in_specs=[pl.BlockSpec((None, tm, d), lambda bi, i: (bi, i, 0)),
                  pl.BlockSpec((None, NSA_WIDTH, tm), lambda bi, i: (bi, 0, i)),
                  pl.BlockSpec((None, tm, HGRN_WIDTH), lambda bi, i: (bi, i, 0)),
                  pl.BlockSpec((None, MEM_WIDTH, tm), lambda bi, i: (bi, 0, i)),
                  full((NSA_WIDTH, 1)), full((MEM_WIDTH, 1)), full(wo.shape)],
        out_specs=pl.BlockSpec((None, tm, d), lambda bi, i: (bi, i, 0)),
        out_shape=jax.ShapeDtypeStruct((b, s, d), F32),
        compiler_params=_cparams(("parallel", "parallel")), name="mix_out_projection",
    )(x3d, ynt, yh, ymt, nsa_gain.reshape(-1, 1), mem_gain.reshape(-1, 1), wo)


def _mix(x, mem, mix_norm, w_in, w_out, nsa_q_norm, nsa_k_norm, cmp_pos_k, cmp_w1_k, cmp_w2_k,
         cmp_pos_v, cmp_w1_v, cmp_w2_v, nsa_out_norm, lower_bound, hgrn_out_norm,
         mem_norm, mem_w_k, mem_w_v, mem_q_norm, mem_k_norm, mem_out_norm):
    b, s, d = x.shape
    sizes = (512, 128, 128, 128, 128, 128, 128, 24, 256, 256, 256, 256, 256)
    offs = np.concatenate([[0], np.cumsum(sizes)])
    col = lambda i: w_in[:, offs[i]:offs[i + 1]]
    (q_a, k_c, v_c, k_s, v_s, k_w, v_w, g_a, q_h, f_h, i_h, g_h, q_m) = [col(i) for i in range(13)]
    gpad = jnp.zeros((d, GATE_ROWS - 3 * NSA_HPG), w_in.dtype)
    wt = jnp.concatenate([q_a, v_s, v_w, g_a[:, :3 * NSA_HPG], gpad, g_a[:, 3 * NSA_HPG:], gpad, q_m],
                         axis=1).T.astype(BF16)
    wn = jnp.concatenate([k_c, v_c, k_s, k_w, q_h, f_h, i_h, g_h], axis=1).astype(BF16)

    qt, vt, gt, qmt, kaug, kvc, hg = _proj(
        x, mix_norm, wt, wn, nsa_q_norm, nsa_k_norm, mem_q_norm, _rope_tables(s))

    y = kvc.reshape(b, s, 4, 64).transpose(0, 2, 1, 3).reshape(b, 4, s // CMP_STRIDE, CMP_STRIDE * 64)
    ynext = jnp.concatenate([y[:, :, 1:], jnp.zeros_like(y[:, :, :1])], axis=2)
    kc, vct = _compress(y, ynext, cmp_pos_k.reshape(2, -1), cmp_pos_v.reshape(2, -1),
                        cmp_w1_k.astype(BF16), cmp_w1_v.astype(BF16), cmp_w2_k.astype(BF16),
                        cmp_w2_v.T.astype(BF16), nsa_k_norm)
    oc, bias = _cmp_select(qt, kc, vct)
    y_nsa = _slc_win(qt, bias, kaug, vt, oc, gt)

    y_hgrn = _hgrn(hg, lower_bound, hgrn_out_norm)

    kh, vht = _mem_kv(mem, mem_norm, mem_w_k.astype(BF16), mem_w_v.T.astype(BF16), mem_k_norm)
    y_mem = _mem_attn(qmt, kh, vht)

    return _out_proj(x, y_nsa, y_hgrn, y_mem, nsa_out_norm, mem_out_norm, w_out.astype(BF16))


def kernel(x, mem, ffn1_norm, ffn1_w_gate, ffn1_w_up, ffn1_w_down, mix_norm, w_in, w_out, nsa_q_norm, nsa_k_norm, cmp_pos_k, cmp_w1_k, cmp_w2_k, cmp_pos_v, cmp_w1_v, cmp_w2_v, nsa_out_norm, hgrn_lb_logits, hgrn_out_norm, mem_norm, mem_w_k, mem_w_v, mem_q_norm, mem_k_norm, mem_out_norm, ffn2_norm, ffn2_w_gate, ffn2_w_up, ffn2_w_down):
    b, s, d = x.shape
    depth = ffn1_norm.shape[0]
    lower_bounds = jnp.cumsum(jax.nn.softmax(hgrn_lb_logits.astype(F32), axis=0), axis=0)
    for l in range(depth):
        x = _ffn(x.reshape(b * s, d), ffn1_norm[l], ffn1_w_gate[l].astype(BF16),
                 ffn1_w_up[l].astype(BF16), ffn1_w_down[l].astype(BF16)).reshape(b, s, d)
        x = _mix(x, mem, mix_norm[l], w_in[l], w_out[l], nsa_q_norm[l], nsa_k_norm[l],
                 cmp_pos_k[l], cmp_w1_k[l], cmp_w2_k[l], cmp_pos_v[l], cmp_w1_v[l], cmp_w2_v[l],
                 nsa_out_norm[l], lower_bounds[l], hgrn_out_norm[l],
                 mem_norm[l], mem_w_k[l], mem_w_v[l], mem_q_norm[l], mem_k_norm[l], mem_out_norm[l])
        x = _ffn(x.reshape(b * s, d), ffn2_norm[l], ffn2_w_gate[l].astype(BF16),
                 ffn2_w_up[l].astype(BF16), ffn2_w_down[l].astype(BF16)).reshape(b, s, d)
    return x
```

```python
import functools

import numpy as np
import jax
import jax.numpy as jnp
from jax import lax
from jax.experimental import pallas as pl
from jax.experimental.pallas import tpu as pltpu

F32 = jnp.float32
BF16 = jnp.bfloat16

HEAD_DIM = 64
ROT_DIM = 16
ROT_HALF = 8
ROPE_THETA = 500000.0
NSA_HEADS = 8
NSA_GROUPS = 2
NSA_HPG = 4
CMP_BLOCK = 32
CMP_STRIDE = 16
SLC_BLOCK = 64
SLC_SHIFT = 6
SLC_TOPK = 16
WINDOW = 512
FORCED_SCORE = 1e4
HGRN_HEADS = 4
HGRN_CHUNK = 64
HGRN_WIDTH = 256
MEM_HEADS = 4
MEM_WIDTH = 256
NSA_WIDTH = 512
EPS = 1e-6
NEG = -1e30
QK_SCALE_LOG2 = HEAD_DIM ** -0.5 * 1.4426950408889634

VMEM_LIMIT = 56 * 1024 * 1024
MAX_BLOCKS = 128
GATE_ROWS = 16
V_ROWS = 80
TOK_TILE = 512
HGRN_CB = 4
HGRN_LEVELS = (32, 16, 8, 4, 2, 1)

NT_DIMS = (((1,), (1,)), ((), ()))
TN_DIMS = (((0,), (0,)), ((), ()))


def _cparams(sem):
    return pltpu.CompilerParams(dimension_semantics=sem, vmem_limit_bytes=VMEM_LIMIT)


def _dot(a, b):
    return jnp.dot(a, b, preferred_element_type=F32)


def _dot_nt(a, b):
    return lax.dot_general(a, b, NT_DIMS, preferred_element_type=F32)


def _dot_tn(a, b):
    return lax.dot_general(a, b, TN_DIMS, preferred_element_type=F32)


def _sigmoid(x):
    return 1.0 / (1.0 + jnp.exp(-x))


def _silu(x):
    return x * _sigmoid(x)


def _split2(x):
    hi = x.astype(BF16)
    lo = (x - hi.astype(F32)).astype(BF16)
    return hi, lo


def _split3(x):
    hi = x.astype(BF16)
    r1 = x - hi.astype(F32)
    mid = r1.astype(BF16)
    lo = (r1 - mid.astype(F32)).astype(BF16)
    return hi, mid, lo


def _rms_rows(x, gain_row):
    ms = jnp.mean(x * x, axis=-1, keepdims=True)
    return x * lax.rsqrt(ms + EPS) * gain_row


def _rms_cols(x, gain_col):
    ms = jnp.mean(x * x, axis=0, keepdims=True)
    return x * lax.rsqrt(ms + EPS) * gain_col


def _seg_mean_sq(x, bd):
    hi, lo = _split2(x * x)
    return _dot(hi, bd) + _dot(lo, bd)


FFN_CHUNK = 256


def _ffn_half_step(x, g_ref, wg_ref, wu_ref, wd_ref, a_scr):
    xn = _rms_rows(x, g_ref[...]).astype(BF16)
    d_ff = wg_ref.shape[1]
    for c in range(d_ff // FFN_CHUNK):
        sl = slice(c * FFN_CHUNK, (c + 1) * FFN_CHUNK)
        g = _dot(xn, wg_ref[:, sl])
        u = _dot(xn, wu_ref[:, sl])
        a_scr[:, sl] = (_silu(g) * u).astype(BF16)
    return x + 0.5 * _dot(a_scr[...], wd_ref[...])


def _resident(shape):
    return pl.BlockSpec(shape, lambda *_: (0,) * len(shape), pipeline_mode=pl.Buffered(1))


def _rope_cols(xn, cos, sin):
    x0, x1 = xn[0:ROT_HALF], xn[ROT_HALF:ROT_DIM]
    return jnp.concatenate([x0 * cos - x1 * sin, x1 * cos + x0 * sin, xn[ROT_DIM:]], axis=0)


def _rope_rows(x, cn, sa, sb):
    return x * cn + pltpu.roll(x, 128 - ROT_HALF, 1) * sa + pltpu.roll(x, ROT_HALF, 1) * sb


def _proj_body(x_ref, fg_ref, wg_ref, wu_ref, wd_ref, mg_ref, wt_ref, wn_ref, qg_ref, kg_ref, mqg_ref,
               cos_ref, sin_ref, cn_ref, sa_ref, sb_ref, bd_ref,
               x1_ref, qt_ref, vt_ref, gt_ref, qmt_ref, kaug_ref, kvc_ref, hg_ref, a_scr):
    tm = x_ref.shape[0]
    assert tm == 8 * SLC_BLOCK
    x1 = _ffn_half_step(x_ref[...], fg_ref, wg_ref, wu_ref, wd_ref, a_scr)
    x1_ref[...] = x1
    h = _rms_rows(x1, mg_ref[...]).astype(BF16)

    pt = _dot_nt(wt_ref[...], h)
    cos, sin = cos_ref[...], sin_ref[...]
    qg = qg_ref[...]
    for hh in range(NSA_HEADS):
        xq = _rms_cols(pt[hh * 64:(hh + 1) * 64], qg)
        qt_ref[hh * 64:(hh + 1) * 64, :] = (_rope_cols(xq, cos, sin) * QK_SCALE_LOG2).astype(BF16)
    ones_rows = (lax.broadcasted_iota(jnp.int32, (V_ROWS - 64, tm), 0) == 0).astype(BF16)
    for g in range(NSA_GROUPS):
        for br in range(2):
            rows = 512 + br * 128 + g * 64
            vt_ref[g, br, 0:64, :] = pt[rows:rows + 64].astype(BF16)
            vt_ref[g, br, 64:V_ROWS, :] = ones_rows
    gt_ref[...] = _sigmoid(pt[768:800])
    mqg = mqg_ref[...]
    for hh in range(MEM_HEADS):
        xm = _rms_cols(pt[800 + hh * 64:864 + hh * 64], mqg)
        qmt_ref[hh * 64:(hh + 1) * 64, :] = (xm * QK_SCALE_LOG2).astype(BF16)

    pn = _dot(h, wn_ref[...])
    cn, sa, sb = cn_ref[...], sa_ref[...], sb_ref[...]
    bd, kg = bd_ref[...], kg_ref[...]
    kvc_ref[:, 0:128] = _rope_rows(pn[:, 0:128], cn, sa, sb)
    kvc_ref[:, 128:256] = pn[:, 128:256]
    ks = pn[:, 256:384]
    kw = pn[:, 384:512]
    ks = _rope_rows(ks * lax.rsqrt(_seg_mean_sq(ks, bd) + EPS) * kg, cn, sa, sb)
    kw = _rope_rows(kw * lax.rsqrt(_seg_mean_sq(kw, bd) + EPS) * kg, cn, sa, sb)
    lane = lax.broadcasted_iota(jnp.int32, (tm, 128), 1)
    row = lax.broadcasted_iota(jnp.int32, (tm, 128), 0)
    onehot = jnp.where(lane - 64 == (row >> SLC_SHIFT), 1.0, 0.0)
    lo_half = lane < 64
    kaug_ref[0, 0] = jnp.where(lo_half, ks, onehot).astype(BF16)
    kaug_ref[0, 1] = jnp.where(lo_half, kw, 0.0).astype(BF16)
    kaug_ref[1, 0] = jnp.where(lo_half, pltpu.roll(ks, 64, 1), onehot).astype(BF16)
    kaug_ref[1, 1] = jnp.where(lo_half, pltpu.roll(kw, 64, 1), 0.0).astype(BF16)
    hg_ref[...] = pn[:, 512:1536]


def _ffn_proj(x3d, ffn_gain, wg, wu, wd, mix_gain, wt, wn, q_gain, k_gain, mq_gain, rope):
    b, s, d = x3d.shape
    tm = TOK_TILE
    ns = s // tm
    cos_t, sin_t, cn, sa, sb = rope
    bd = jnp.asarray(np.kron(np.eye(2), np.full((64, 64), 1.0 / 64)), BF16)
    full = _resident
    out_shape = (
        jax.ShapeDtypeStruct((b, s, d), F32),
        jax.ShapeDtypeStruct((b, 512, s), BF16),
        jax.ShapeDtypeStruct((b, 2, 2, ns, V_ROWS, tm), BF16),
        jax.ShapeDtypeStruct((b, 32, s), F32),
        jax.ShapeDtypeStruct((b, 256, s), BF16),
        jax.ShapeDtypeStruct((b, 2, 2, s, 128), BF16),
        jax.ShapeDtypeStruct((b, s, 256), F32),
        jax.ShapeDtypeStruct((b, s, 1024), F32),
    )
    out_specs = (
        pl.BlockSpec((None, tm, d), lambda bi, i: (bi, i, 0)),
        pl.BlockSpec((None, 512, tm), lambda bi, i: (bi, 0, i)),
        pl.BlockSpec((None, 2, 2, None, V_ROWS, tm), lambda bi, i: (bi, 0, 0, i, 0, 0)),
        pl.BlockSpec((None, 32, tm), lambda bi, i: (bi, 0, i)),
        pl.BlockSpec((None, 256, tm), lambda bi, i: (bi, 0, i)),
        pl.BlockSpec((None, 2, 2, tm, 128), lambda bi, i: (bi, 0, 0, i, 0)),
        pl.BlockSpec((None, tm, 256), lambda bi, i: (bi, i, 0)),
        pl.BlockSpec((None, tm, 1024), lambda bi, i: (bi, i, 0)),
    )
    in_specs = [
        pl.BlockSpec((None, tm, d), lambda bi, i: (bi, i, 0)),
        full((1, d)), full(wg.shape), full(wu.shape), full(wd.shape),
        full((1, d)), full(wt.shape), full(wn.shape),
        full((64, 1)), full((1, 128)), full((64, 1)),
        pl.BlockSpec((ROT_HALF, tm), lambda bi, i: (0, i)),
        pl.BlockSpec((ROT_HALF, tm), lambda bi, i: (0, i)),
        pl.BlockSpec((tm, 128), lambda bi, i: (i, 0)),
        pl.BlockSpec((tm, 128), lambda bi, i: (i, 0)),
        pl.BlockSpec((tm, 128), lambda bi, i: (i, 0)),
        full((128, 128)),
    ]
    return pl.pallas_call(
        _proj_body, grid=(b, ns), in_specs=in_specs, out_specs=out_specs, out_shape=out_shape,
        scratch_shapes=[pltpu.VMEM((tm, wg.shape[1]), BF16)],
        compiler_params=_cparams(("parallel", "parallel")), name="ffn1_mix_projection",
    )(x3d, ffn_gain.reshape(1, d), wg, wu, wd, mix_gain.reshape(1, d), wt, wn, q_gain.reshape(64, 1),
      jnp.tile(k_gain.reshape(1, 64), (1, 2)), mq_gain.reshape(64, 1), cos_t, sin_t, cn, sa, sb, bd)


def _rope_tables(s):
    pos = jnp.arange(s, dtype=F32)
    inv = ROPE_THETA ** (-(jnp.arange(0, ROT_DIM, 2, dtype=F32) / ROT_DIM))
    ang = pos[:, None] * inv[None, :]
    cos, sin = jnp.cos(ang), jnp.sin(ang)
    zeros = jnp.zeros((s, 64 - ROT_DIM), F32)
    cn = jnp.concatenate([cos, cos, jnp.ones((s, 64 - ROT_DIM), F32)], axis=1)
    sa = jnp.concatenate([-sin, jnp.zeros((s, ROT_HALF), F32), zeros], axis=1)
    sb = jnp.concatenate([jnp.zeros((s, ROT_HALF), F32), sin, zeros], axis=1)
    tile2 = lambda a: jnp.concatenate([a, a], axis=1)
    return cos.T, sin.T, tile2(cn), tile2(sa), tile2(sb)


def _cmp_body(yk_ref, ykn_ref, yv_ref, yvn_ref, pk_ref, pv_ref, w1k_ref, w1v_ref, w2k_ref, w2vt_ref,
              kg_ref, kc_ref, vct_ref):
    half = w1k_ref.shape[0] // 2

    def hidden(y_ref, yn_ref, p_ref, w1_ref):
        top = (y_ref[...] + p_ref[0:1, :]).astype(BF16)
        bot = (yn_ref[...] + p_ref[1:2, :]).astype(BF16)
        return _silu(_dot(top, w1_ref[0:half, :]) + _dot(bot, w1_ref[half:, :])).astype(BF16)

    kc = _dot(hidden(yk_ref, ykn_ref, pk_ref, w1k_ref), w2k_ref[...])
    kc_ref[...] = _rms_rows(kc, kg_ref[...]).astype(BF16)
    vct_ref[...] = _dot_nt(w2vt_ref[...], hidden(yv_ref, yvn_ref, pv_ref, w1v_ref)).astype(BF16)


def _compress(y, ynext, pos_k, pos_v, w1k, w1v, w2k, w2vt, k_gain):
    b, _, nc, width = y.shape
    ysp = lambda off: pl.BlockSpec((None, None, nc, width), lambda bi, g: (bi, g + off, 0, 0))
    full = lambda shape: pl.BlockSpec(shape, lambda bi, g: (0,) * len(shape))
    return pl.pallas_call(
        _cmp_body, grid=(b, 2),
        in_specs=[ysp(0), ysp(0), ysp(2), ysp(2), full(pos_k.shape), full(pos_v.shape),
                  full(w1k.shape), full(w1v.shape), full(w2k.shape), full(w2vt.shape), full((1, 64))],
        out_specs=(pl.BlockSpec((None, None, nc, 64), lambda bi, g: (bi, g, 0, 0)),
                   pl.BlockSpec((None, None, 64, nc), lambda bi, g: (bi, g, 0, 0))),
        out_shape=(jax.ShapeDtypeStruct((b, 2, nc, 64), BF16), jax.ShapeDtypeStruct((b, 2, 64, nc), BF16)),
        compiler_params=_cparams(("parallel", "parallel")), name="nsa_compress",
    )(y, ynext, y, ynext, pos_k, pos_v, w1k, w1v, w2k, w2vt, k_gain.reshape(1, 64))


CMP_CLASS_ROWS = 128


def _cmpsel_variant(nc, nblk, qt_ref, kc_ref, vct_ref, oc_ref, bias_ref, s_scr):
    tq = qt_ref.shape[1]
    t0 = pl.program_id(2) * tq
    n_idx = lax.broadcasted_iota(jnp.int32, (nc, tq), 0)
    t_idx = t0 + lax.broadcasted_iota(jnp.int32, (nc, tq), 1)
    mask_bias = jnp.where(n_idx * CMP_STRIDE + (CMP_BLOCK - 1) <= t_idx, 0.0, NEG)
    t_row = t0 + lax.broadcasted_iota(jnp.int32, (1, tq), 1)
    kc = kc_ref[0:nc, :]
    vct = vct_ref[:, 0:nc]
    for hh in range(NSA_HPG):
        s_scr[hh, 0:nc, :] = _dot(kc, qt_ref[hh * 64:(hh + 1) * 64, :]) + mask_bias
    pcs = jnp.zeros((nc, tq), F32)
    for hh in range(NSA_HPG):
        m = jnp.max(s_scr[hh, 0:nc, :], axis=0, keepdims=True)
        p = jnp.exp2(s_scr[hh, 0:nc, :] - m)
        l = jnp.sum(p, axis=0, keepdims=True)
        pc = p * jnp.where(t_row >= CMP_BLOCK - 1, 1.0 / l, 0.0)
        oc_ref[hh * 64:(hh + 1) * 64, :] = _dot(vct, pc.astype(BF16))
        pcs = pcs + pc

    jj = lax.broadcasted_iota(jnp.int32, (nblk, nc), 0)
    nn = lax.broadcasted_iota(jnp.int32, (nblk, nc), 1)
    ov = jnp.where((nn * CMP_STRIDE < jj * SLC_BLOCK + SLC_BLOCK)
                   & (nn * CMP_STRIDE + CMP_BLOCK > jj * SLC_BLOCK), 1.0, 0.0).astype(BF16)
    hi, lo = _split2(pcs)
    imp = _dot(ov, hi) + _dot(ov, lo)

    j = lax.broadcasted_iota(jnp.int32, (nblk, tq), 0)
    cur = (t0 + lax.broadcasted_iota(jnp.int32, (nblk, tq), 1)) >> SLC_SHIFT
    forced = (j == 0) | (j == cur) | (j == cur - 1)
    bias = jnp.where(forced & (j <= cur), 0.0, NEG)
    imp = jnp.where((j <= cur) & jnp.logical_not(forced), imp, -1.0)
    jf = j.astype(F32)
    for _ in range(SLC_TOPK - 3):
        v = jnp.max(imp, axis=0, keepdims=True)
        first = jnp.min(jnp.where(imp == v, jf, float(nblk)), axis=0, keepdims=True)
        pick = jf == first
        bias = jnp.where(pick & (v >= 0.0), 0.0, bias)
        imp = jnp.where(pick, -3e38, imp)
    bias_ref[0:nblk, :] = bias
    if nblk < bias_ref.shape[0]:
        bias_ref[nblk:, :] = jnp.full((bias_ref.shape[0] - nblk, tq), NEG, F32)


def _cmpsel_body(qt_ref, kc_ref, vct_ref, oc_ref, bias_ref, s_scr):
    tq = qt_ref.shape[1]
    nc_total = kc_ref.shape[0]
    tiles_per_class = CMP_CLASS_ROWS // (tq // CMP_STRIDE)
    cls = pl.program_id(2) // tiles_per_class
    for c in range(nc_total // CMP_CLASS_ROWS):
        nc = (c + 1) * CMP_CLASS_ROWS
        nblk = min(nc * CMP_STRIDE // SLC_BLOCK, bias_ref.shape[0])
        pl.when(cls == c)(functools.partial(_cmpsel_variant, nc, nblk, qt_ref, kc_ref, vct_ref,
                                            oc_ref, bias_ref, s_scr))


def _cmp_select(qt, kc, vct, *, tq=TOK_TILE):
    b, _, s = qt.shape
    nc = kc.shape[2]
    nblk = MAX_BLOCKS
    assert s // SLC_BLOCK <= MAX_BLOCKS and s // SLC_BLOCK >= SLC_TOPK and nc % CMP_CLASS_ROWS == 0
    return pl.pallas_call(
        _cmpsel_body, grid=(b, NSA_GROUPS, s // tq),
        in_specs=[pl.BlockSpec((None, 256, tq), lambda bi, g, i: (bi, g, i)),
                  pl.BlockSpec((None, None, nc, 64), lambda bi, g, i: (bi, g, 0, 0)),
                  pl.BlockSpec((None, None, 64, nc), lambda bi, g, i: (bi, g, 0, 0))],
        out_specs=(pl.BlockSpec((None, 256, tq), lambda bi, g, i: (bi, g, i)),
                   pl.BlockSpec((None, None, nblk, tq), lambda bi, g, i: (bi, g, 0, i))),
        out_shape=(jax.ShapeDtypeStruct((b, 512, s), F32),
                   jax.ShapeDtypeStruct((b, NSA_GROUPS, nblk, s), F32)),
        scratch_shapes=[pltpu.VMEM((NSA_HPG, nc, tq), F32)],
        compiler_params=_cparams(("parallel", "parallel", "parallel")), name="nsa_compressed_select",
    )(qt, kc, vct)


def _flash_step(s_ref, vt, m_ref, acc_ref):
    m_old = m_ref[...]
    m_new = jnp.maximum(m_old, jnp.max(s_ref[...], axis=0, keepdims=True))
    p = jnp.exp2(s_ref[...] - m_new)
    acc_ref[...] = jnp.exp2(m_old - m_new) * acc_ref[...] + _dot(vt, p.astype(BF16))
    m_ref[...] = m_new


SEL, WIN = 0, 1
BIAS_ROWS = 16


def _slcwin_body(qt_ref, bias_ref, kaug_ref, vt_ref, oc_ref, gt_ref, mb_ref, y_ref,
                 q_scr, m_scr, acc_scr, s_scr):
    tq = qt_ref.shape[1]
    tk = vt_ref.shape[3]
    assert tq == tk and WINDOW <= tk and tk == 8 * SLC_BLOCK
    diag = pl.program_id(2)

    zeros = jnp.zeros((64, tq), BF16)
    for br in (SEL, WIN):
        for hh in range(NSA_HPG):
            q_scr[br, hh, 0:64, :] = qt_ref[hh * 64:(hh + 1) * 64, :]
            q_scr[br, hh, 64:128, :] = zeros
    m_scr[...] = jnp.full(m_scr.shape, NEG, F32)
    acc_scr[...] = jnp.zeros(acc_scr.shape, F32)

    def set_selection_bias(kt):
        rows = bias_ref[pl.ds(pl.multiple_of(kt * 8, 8), 8), :]
        b16 = jnp.concatenate([rows, jnp.zeros_like(rows)], axis=0).astype(BF16)
        for hh in range(NSA_HPG):
            q_scr[SEL, hh, 64:64 + BIAS_ROWS, :] = b16

    def tile_step(br, kt, mask_bias):
        k = kaug_ref[br, pl.ds(pl.multiple_of(kt * tk, tk), tk), :]
        v = vt_ref[br, kt]
        for hh in range(NSA_HPG):
            s = _dot(k, q_scr[br, hh])
            s_scr[hh] = s if mask_bias is None else s + mask_bias
        for hh in range(NSA_HPG):
            _flash_step(s_scr.at[hh], v, m_scr.at[br, hh], acc_scr.at[br, hh])

    def full_tile(kt, carry):
        set_selection_bias(kt)
        tile_step(SEL, kt, None)
        return carry

    lax.fori_loop(0, diag, full_tile, 0)
    set_selection_bias(diag)

    def masked_tile(j, carry):
        first = j == 0
        br = jnp.where(j == 1, SEL, WIN)
        kt = jnp.where(first, jnp.maximum(diag - 1, 0), diag)
        extra = jnp.where(first & (diag == 0), NEG, 0.0)
        tile_step(br, kt, mb_ref[jnp.where(first, 1, 0)] + extra)
        return carry

    lax.fori_loop(0, 3, masked_tile, 0)

    gt = gt_ref[...]
    for hh in range(NSA_HPG):
        o_s = acc_scr[SEL, hh, 0:64, :] * (1.0 / acc_scr[SEL, hh, 64:65, :])
        o_w = acc_scr[WIN, hh, 0:64, :] * (1.0 / acc_scr[WIN, hh, 64:65, :])
        y_ref[hh * 64:(hh + 1) * 64, :] = (gt[3 * hh:3 * hh + 1] * oc_ref[hh * 64:(hh + 1) * 64, :]
                                          + gt[3 * hh + 1:3 * hh + 2] * o_s
                                          + gt[3 * hh + 2:3 * hh + 3] * o_w)


def _slc_win(qt, bias, kaug, vt, oc, gt):
    b, _, s = qt.shape
    nblk = bias.shape[2]
    ns, tk = vt.shape[3], vt.shape[5]
    tq = tk
    key_rel, t_rel = np.arange(tk)[:, None], np.arange(tq)[None, :]
    mask_bias = jnp.asarray(np.stack([np.where(key_rel <= t_rel, 0.0, NEG),
                                      np.where(t_rel + tk - key_rel < WINDOW, 0.0, NEG)]), F32)
    qblk = pl.BlockSpec((None, 256, tq), lambda bi, g, i: (bi, g, i))
    return pl.pallas_call(
        _slcwin_body, grid=(b, NSA_GROUPS, s // tq),
        in_specs=[qblk,
                  pl.BlockSpec((None, None, nblk, tq), lambda bi, g, i: (bi, g, 0, i)),
                  pl.BlockSpec((None, None, 2, s, 128), lambda bi, g, i: (bi, g, 0, 0, 0)),
                  pl.BlockSpec((None, None, 2, ns, V_ROWS, tk), lambda bi, g, i: (bi, g, 0, 0, 0, 0)),
                  qblk,
                  pl.BlockSpec((None, None, GATE_ROWS, tq), lambda bi, g, i: (bi, g, 0, i)),
                  pl.BlockSpec((2, tk, tq), lambda bi, g, i: (0, 0, 0))],
        out_specs=qblk,
        out_shape=jax.ShapeDtypeStruct((b, 512, s), F32),
        scratch_shapes=[pltpu.VMEM((2, NSA_HPG, 128, tq), BF16),
                        pltpu.VMEM((2, NSA_HPG, 1, tq), F32), pltpu.VMEM((2, NSA_HPG, V_ROWS, tq), F32),
                        pltpu.VMEM((NSA_HPG, tk, tq), F32)],
        compiler_params=_cparams(("parallel", "parallel", "arbitrary")), name="nsa_selected_window",
    )(qt, bias, kaug, vt, oc, gt.reshape(b, NSA_GROUPS, GATE_ROWS, s), mask_bias)


def _hgrn_consts():
    c = HGRN_CHUNK
    t = np.arange(c)
    lower = (t[None, :] <= t[:, None]).astype(np.float32)
    rows = [lower]
    masks = []
    for half in HGRN_LEVELS:
        mid = (t // (2 * half)) * (2 * half) + half - 1
        if half < 8:
            rows.append(lower[mid])
        same = (t[:, None] // (2 * half)) == (t[None, :] // (2 * half))
        right = (t[:, None] & half) != 0
        left = (t[None, :] & half) == 0
        masks.append((same & right & left).astype(np.float32))
    masks.append(np.eye(c, dtype=np.float32))
    mall = np.concatenate(rows, axis=0)
    lvl = np.stack([np.tile(mk.T, (1, HGRN_HEADS)) for mk in masks])
    bdm = np.kron(np.eye(HGRN_HEADS), np.ones((64, 64), np.float32))
    return jnp.asarray(mall, BF16), jnp.asarray(lvl, F32), jnp.asarray(bdm, F32), jnp.asarray(bdm / 64, BF16)


def _hgrn_body(hg_ref, lb_ref, og_ref, mall_ref, lvl_ref, bdm_ref, bdn_ref, y_ref, st_scr):
    c = HGRN_CHUNK
    w = HGRN_WIDTH

    @pl.when(pl.program_id(1) == 0)
    def _():
        st_scr[...] = jnp.zeros(st_scr.shape, F32)

    chunks = range(hg_ref.shape[0] // c)
    lb = lb_ref[...]
    lane = lax.broadcasted_iota(jnp.int32, (c, w), 1)
    trow = lax.broadcasted_iota(jnp.int32, (c, w), 0)
    head_masks = [(lane >> 6) == hh for hh in range(HGRN_HEADS)]
    nlev = len(HGRN_LEVELS)

    def stack_heads(x):
        x16 = x.astype(BF16)
        return jnp.concatenate([jnp.where(hm, x16, 0) for hm in head_masks], axis=0)

    def row_bcast(x, half):
        return jnp.concatenate([jnp.broadcast_to(x[p + half - 1:p + half, :], (2 * half, w))
                                for p in range(0, c, 2 * half)], axis=0)

    qa, kk, v16, logf = [], [], [], []
    for ci in chunks:
        rows = slice(ci * c, (ci + 1) * c)
        qa.append(_silu(hg_ref[rows, 0:w]) * (HEAD_DIM ** -0.5))
        fg = lb + (1.0 - lb) * _sigmoid(hg_ref[rows, w:2 * w])
        kk.append(1.0 - fg)
        logf.append(jnp.log(fg))
        v16.append(hg_ref[rows, 2 * w:3 * w].astype(BF16))

    mall = mall_ref[...]
    r_all = sum(_dot(mall, part) for part in _split3(jnp.concatenate(logf, axis=1)))
    bcum = [r_all[0:c, ci * w:(ci + 1) * w] for ci in chunks]

    attn = [lvl_ref[nlev] * _dot_nt(kk[ci].astype(BF16), stack_heads(qa[ci])) for ci in chunks]
    fine = 0
    for li, half in enumerate(HGRN_LEVELS):
        right = (trow & half) != 0
        for ci in chunks:
            if half >= 8:
                ref_pt = row_bcast(bcum[ci], half)
            else:
                ref_pt = r_all[(fine + 1) * c:(fine + 2) * c, ci * w:(ci + 1) * w]
            e = jnp.exp(jnp.where(right, bcum[ci] - ref_pt, ref_pt - bcum[ci]))
            qt = jnp.where(right, qa[ci] * e, 0.0)
            kt = jnp.where(right, 0.0, kk[ci] * e).astype(BF16)
            attn[ci] = attn[ci] + lvl_ref[li] * _dot_nt(kt, stack_heads(qt))
        if half < 8:
            fine += 1

    intra, upd, decay, qb = [], [], [], []
    for ci in chunks:
        x = _dot_tn(attn[ci].astype(BF16), v16[ci])
        intra.append(sum(jnp.where(head_masks[hh], x[hh * c:(hh + 1) * c], 0.0) for hh in range(HGRN_HEADS)))
        b_last = bcum[ci][c - 1:c, :]
        kl = (kk[ci] * jnp.exp(b_last - bcum[ci])).astype(BF16)
        upd.append(bdm_ref[...] * _dot_tn(v16[ci], kl))
        decay.append(jnp.exp(b_last))
        qb.append((qa[ci] * jnp.exp(bcum[ci])).astype(BF16))

    st = st_scr[...]
    inter = []
    for ci in chunks:
        inter.append(_dot_nt(qb[ci], st.astype(BF16)))
        st = st * decay[ci] + upd[ci]
    st_scr[...] = st

    for ci in chunks:
        rows = slice(ci * c, (ci + 1) * c)
        o = inter[ci] + intra[ci]
        hi, lo = _split2(o * o)
        ms = _dot(hi, bdn_ref[...]) + _dot(lo, bdn_ref[...])
        y_ref[rows, :] = o * lax.rsqrt(ms + EPS) * og_ref[...] * _silu(hg_ref[rows, 3 * w:4 * w])


def _hgrn(hg, lower_bound, out_gain):
    b, s, _ = hg.shape
    rows = HGRN_CB * HGRN_CHUNK
    mall, lvl, bdm, bdn = _hgrn_consts()
    full = lambda shape: pl.BlockSpec(shape, lambda bi, i: (0,) * len(shape))
    return pl.pallas_call(
        _hgrn_body, grid=(b, s // rows),
        in_specs=[pl.BlockSpec((None, rows, 4 * HGRN_WIDTH), lambda bi, i: (bi, i, 0)),
                  full((1, HGRN_WIDTH)), full((1, HGRN_WIDTH)),
                  full(mall.shape), full(lvl.shape), full(bdm.shape), full(bdn.shape)],
        out_specs=pl.BlockSpec((None, rows, HGRN_WIDTH), lambda bi, i: (bi, i, 0)),
        out_shape=jax.ShapeDtypeStruct((b, s, HGRN_WIDTH), F32),
        scratch_shapes=[pltpu.VMEM((HGRN_WIDTH, HGRN_WIDTH), F32)],
        compiler_params=_cparams(("parallel", "arbitrary")), name="hgrn2_chunks",
    )(hg, lower_bound.reshape(1, -1), out_gain.reshape(1, -1), mall, lvl, bdm, bdn)


def _memkv_body(mem_ref, mg_ref, wk_ref, wvt_ref, kg_ref, kh_ref, vht_ref):
    m = mem_ref.shape[0]
    mn = _rms_rows(mem_ref[...], mg_ref[...]).astype(BF16)
    k = _dot(mn, wk_ref[...])
    vt = _dot_nt(wvt_ref[...], mn)
    ones_rows = (lax.broadcasted_iota(jnp.int32, (V_ROWS - 64, m), 0) == 0).astype(BF16)
    for hh in range(MEM_HEADS):
        kh_ref[hh] = _rms_rows(k[:, hh * 64:(hh + 1) * 64], kg_ref[...]).astype(BF16)
        vht_ref[hh, 0:64, :] = vt[hh * 64:(hh + 1) * 64].astype(BF16)
        vht_ref[hh, 64:V_ROWS, :] = ones_rows


def _mem_kv(mem, mem_gain, wk, wvt, k_gain):
    b, m, d = mem.shape
    full = lambda shape: pl.BlockSpec(shape, lambda bi: (0,) * len(shape))
    return pl.pallas_call(
        _memkv_body, grid=(b,),
        in_specs=[pl.BlockSpec((None, m, d), lambda bi: (bi, 0, 0)), full((1, d)),
                  full(wk.shape), full(wvt.shape), full((1, 64))],
        out_specs=(pl.BlockSpec((None, MEM_HEADS, m, 64), lambda bi: (bi, 0, 0, 0)),
                   pl.BlockSpec((None, MEM_HEADS, V_ROWS, m), lambda bi: (bi, 0, 0, 0))),
        out_shape=(jax.ShapeDtypeStruct((b, MEM_HEADS, m, 64), BF16),
                   jax.ShapeDtypeStruct((b, MEM_HEADS, V_ROWS, m), BF16)),
        compiler_params=_cparams(("parallel",)), name="memory_kv",
    )(mem, mem_gain.reshape(1, d), wk, wvt, k_gain.reshape(1, 64))


def _out_body(x_ref, ynt_ref, yh_ref, qmt_ref, kh_ref, vht_ref, ng_ref, mg_ref, wo_ref,
              fg_ref, wg_ref, wu_ref, wd_ref, o_ref, a_scr):
    y_mem = []
    for hh in range(MEM_HEADS):
        s = _dot(kh_ref[hh], qmt_ref[hh * 64:(hh + 1) * 64, :])
        p = jnp.exp2(s - jnp.max(s, axis=0, keepdims=True))
        o = _dot(vht_ref[hh], p.astype(BF16))
        y_mem.append(o[0:64] * (1.0 / o[64:65]))
    mem = _rms_cols(jnp.concatenate(y_mem, axis=0), mg_ref[...]).astype(BF16)
    nsa = _rms_cols(ynt_ref[...], ng_ref[...]).astype(BF16)
    acc = _dot_tn(nsa, wo_ref[0:NSA_WIDTH, :])
    acc = acc + _dot(yh_ref[...].astype(BF16), wo_ref[NSA_WIDTH:NSA_WIDTH + HGRN_WIDTH, :])
    acc = acc + _dot_tn(mem, wo_ref[NSA_WIDTH + HGRN_WIDTH:, :])
    o_ref[...] = _ffn_half_step(x_ref[...] + acc, fg_ref, wg_ref, wu_ref, wd_ref, a_scr)


def _out_ffn(x3d, ynt, yh, qmt, kh, vht, nsa_gain, mem_gain, wo, ffn_gain, wg, wu, wd, *, tm=TOK_TILE):
    b, s, d = x3d.shape
    m = kh.shape[2]
    return pl.pallas_call(
        _out_body, grid=(b, s // tm),
        in_specs=[pl.BlockSpec((None, tm, d), lambda bi, i: (bi, i, 0)),
                  pl.BlockSpec((None, NSA_WIDTH, tm), lambda bi, i: (bi, 0, i)),
                  pl.BlockSpec((None, tm, HGRN_WIDTH), lambda bi, i: (bi, i, 0)),
                  pl.BlockSpec((None, MEM_WIDTH, tm), lambda bi, i: (bi, 0, i)),
                  pl.BlockSpec((None, MEM_HEADS, m, 64), lambda bi, i: (bi, 0, 0, 0)),
                  pl.BlockSpec((None, MEM_HEADS, V_ROWS, m), lambda bi, i: (bi, 0, 0, 0)),
                  _resident((NSA_WIDTH, 1)), _resident((MEM_WIDTH, 1)), _resident(wo.shape),
                  _resident((1, d)), _resident(wg.shape), _resident(wu.shape), _resident(wd.shape)],
        out_specs=pl.BlockSpec((None, tm, d), lambda bi, i: (bi, i, 0)),
        out_shape=jax.ShapeDtypeStruct((b, s, d), F32),
        scratch_shapes=[pltpu.VMEM((tm, wg.shape[1]), BF16)],
        compiler_params=_cparams(("parallel", "parallel")), name="mix_out_ffn2",
    )(x3d, ynt, yh, qmt, kh, vht, nsa_gain.reshape(-1, 1), mem_gain.reshape(-1, 1), wo,
      ffn_gain.reshape(1, d), wg, wu, wd)


def _layer(x, mem, ffn1, ffn2, mix_norm, w_in, w_out, nsa_q_norm, nsa_k_norm, cmp_pos_k, cmp_w1_k, cmp_w2_k,
           cmp_pos_v, cmp_w1_v, cmp_w2_v, nsa_out_norm, lower_bound, hgrn_out_norm,
           mem_norm, mem_w_k, mem_w_v, mem_q_norm, mem_k_norm, mem_out_norm):
    b, s, d = x.shape
    sizes = (512, 128, 128, 128, 128, 128, 128, 24, 256, 256, 256, 256, 256)
    offs = np.concatenate([[0], np.cumsum(sizes)])
    col = lambda i: w_in[:, offs[i]:offs[i + 1]]
    (q_a, k_c, v_c, k_s, v_s, k_w, v_w, g_a, q_h, f_h, i_h, g_h, q_m) = [col(i) for i in range(13)]
    gpad = jnp.zeros((d, GATE_ROWS - 3 * NSA_HPG), w_in.dtype)
    wt = jnp.concatenate([q_a, v_s, v_w, g_a[:, :3 * NSA_HPG], gpad, g_a[:, 3 * NSA_HPG:], gpad, q_m],
                         axis=1).T.astype(BF16)
    wn = jnp.concatenate([k_c, v_c, k_s, k_w, q_h, f_h, i_h, g_h], axis=1).astype(BF16)

    x1, qt, vt, gt, qmt, kaug, kvc, hg = _ffn_proj(
        x, *ffn1, mix_norm, wt, wn, nsa_q_norm, nsa_k_norm, mem_q_norm, _rope_tables(s))

    y = kvc.reshape(b, s, 4, 64).transpose(0, 2, 1, 3).reshape(b, 4, s // CMP_STRIDE, CMP_STRIDE * 64)
    ynext = jnp.concatenate([y[:, :, 1:], jnp.zeros_like(y[:, :, :1])], axis=2)
    kc, vct = _compress(y, ynext, cmp_pos_k.reshape(2, -1), cmp_pos_v.reshape(2, -1),
                        cmp_w1_k.astype(BF16), cmp_w1_v.astype(BF16), cmp_w2_k.astype(BF16),
                        cmp_w2_v.T.astype(BF16), nsa_k_norm)
    oc, bias = _cmp_select(qt, kc, vct)
    y_nsa = _slc_win(qt, bias, kaug, vt, oc, gt)

    y_hgrn = _hgrn(hg, lower_bound, hgrn_out_norm)

    kh, vht = _mem_kv(mem, mem_norm, mem_w_k.astype(BF16), mem_w_v.T.astype(BF16), mem_k_norm)
    return _out_ffn(x1, y_nsa, y_hgrn, qmt, kh, vht, nsa_out_norm, mem_out_norm, w_out.astype(BF16), *ffn2)


def kernel(x, mem, ffn1_norm, ffn1_w_gate, ffn1_w_up, ffn1_w_down, mix_norm, w_in, w_out, nsa_q_norm, nsa_k_norm, cmp_pos_k, cmp_w1_k, cmp_w2_k, cmp_pos_v, cmp_w1_v, cmp_w2_v, nsa_out_norm, hgrn_lb_logits, hgrn_out_norm, mem_norm, mem_w_k, mem_w_v, mem_q_norm, mem_k_norm, mem_out_norm, ffn2_norm, ffn2_w_gate, ffn2_w_up, ffn2_w_down):
    b, s, d = x.shape
    depth = ffn1_norm.shape[0]
    lower_bounds = jnp.cumsum(jax.nn.softmax(hgrn_lb_logits.astype(F32), axis=0), axis=0)
    bf = lambda a: a.astype(BF16)
    for l in range(depth):
        x = _layer(x, mem, (ffn1_norm[l], bf(ffn1_w_gate[l]), bf(ffn1_w_up[l]), bf(ffn1_w_down[l])),
                   (ffn2_norm[l], bf(ffn2_w_gate[l]), bf(ffn2_w_up[l]), bf(ffn2_w_down[l])),
                   mix_norm[l], w_in[l], w_out[l], nsa_q_norm[l], nsa_k_norm[l],
                   cmp_pos_k[l], cmp_w1_k[l], cmp_w2_k[l], cmp_pos_v[l], cmp_w1_v[l], cmp_w2_v[l],
                   nsa_out_norm[l], lower_bounds[l], hgrn_out_norm[l],
                   mem_norm[l], mem_w_k[l], mem_w_v[l], mem_q_norm[l], mem_k_norm[l], mem_out_norm[l])
    return x
```

```python
import functools

import numpy as np
import jax
import jax.numpy as jnp
from jax import lax
from jax.experimental import pallas as pl
from jax.experimental.pallas import tpu as pltpu

F32 = jnp.float32
BF16 = jnp.bfloat16

HEAD_DIM = 64
ROT_DIM = 16
ROT_HALF = 8
ROPE_THETA = 500000.0
NSA_HEADS = 8
NSA_GROUPS = 2
NSA_HPG = 4
CMP_BLOCK = 32
CMP_STRIDE = 16
SLC_BLOCK = 64
SLC_SHIFT = 6
SLC_TOPK = 16
WINDOW = 512
FORCED_SCORE = 1e4
HGRN_HEADS = 4
HGRN_CHUNK = 64
HGRN_WIDTH = 256
MEM_HEADS = 4
MEM_WIDTH = 256
NSA_WIDTH = 512
EPS = 1e-6
NEG = -1e30
QK_SCALE_LOG2 = HEAD_DIM ** -0.5 * 1.4426950408889634

VMEM_LIMIT = 56 * 1024 * 1024
MAX_BLOCKS = 128
GATE_ROWS = 16
V_ROWS = 80
TOK_TILE = 512
HGRN_CB = 4
HGRN_LEVELS = (32, 16, 8, 4, 2, 1)

NT_DIMS = (((1,), (1,)), ((), ()))
TN_DIMS = (((0,), (0,)), ((), ()))


def _cparams(sem):
    return pltpu.CompilerParams(dimension_semantics=sem, vmem_limit_bytes=VMEM_LIMIT)


def _dot(a, b):
    return jnp.dot(a, b, preferred_element_type=F32)


def _dot_nt(a, b):
    return lax.dot_general(a, b, NT_DIMS, preferred_element_type=F32)


def _dot_tn(a, b):
    return lax.dot_general(a, b, TN_DIMS, preferred_element_type=F32)


def _sigmoid(x):
    return 1.0 / (1.0 + jnp.exp(-x))


def _silu(x):
    return x * _sigmoid(x)


def _split2(x):
    hi = x.astype(BF16)
    lo = (x - hi.astype(F32)).astype(BF16)
    return hi, lo


def _split3(x):
    hi = x.astype(BF16)
    r1 = x - hi.astype(F32)
    mid = r1.astype(BF16)
    lo = (r1 - mid.astype(F32)).astype(BF16)
    return hi, mid, lo


def _rms_rows(x, gain_row):
    ms = jnp.mean(x * x, axis=-1, keepdims=True)
    return x * lax.rsqrt(ms + EPS) * gain_row


def _rms_cols(x, gain_col):
    ms = jnp.mean(x * x, axis=0, keepdims=True)
    return x * lax.rsqrt(ms + EPS) * gain_col


def _seg_mean_sq(x, bd):
    hi, lo = _split2(x * x)
    return _dot(hi, bd) + _dot(lo, bd)


FFN_CHUNK = 256


def _ffn_half_step(x, g_ref, wg_ref, wu_ref, wd_ref, a_scr):
    xn = _rms_rows(x, g_ref[...]).astype(BF16)
    d_ff = wg_ref.shape[1]
    for c in range(d_ff // FFN_CHUNK):
        sl = slice(c * FFN_CHUNK, (c + 1) * FFN_CHUNK)
        g = _dot(xn, wg_ref[:, sl])
        u = _dot(xn, wu_ref[:, sl])
        a_scr[:, sl] = (_silu(g) * u).astype(BF16)
    return x + 0.5 * _dot(a_scr[...], wd_ref[...])


def _resident(shape):
    return pl.BlockSpec(shape, lambda *_: (0,) * len(shape), pipeline_mode=pl.Buffered(1))


def _rope_cols(xn, cos, sin):
    x0, x1 = xn[0:ROT_HALF], xn[ROT_HALF:ROT_DIM]
    return jnp.concatenate([x0 * cos - x1 * sin, x1 * cos + x0 * sin, xn[ROT_DIM:]], axis=0)


def _rope_rows(x, cn, sa, sb):
    return x * cn + pltpu.roll(x, 128 - ROT_HALF, 1) * sa + pltpu.roll(x, ROT_HALF, 1) * sb


def _proj_body(x_ref, fg_ref, wg_ref, wu_ref, wd_ref, mg_ref, wt_ref, wn_ref, qg_ref, kg_ref, mqg_ref,
               cos_ref, sin_ref, cn_ref, sa_ref, sb_ref, bd_ref,
               x1_ref, qt_ref, vt_ref, gt_ref, qmt_ref, kaug_ref, kvc_ref, hg_ref, a_scr):
    tm = x_ref.shape[0]
    assert tm == 8 * SLC_BLOCK
    x1 = _ffn_half_step(x_ref[...], fg_ref, wg_ref, wu_ref, wd_ref, a_scr)
    x1_ref[...] = x1
    h = _rms_rows(x1, mg_ref[...]).astype(BF16)

    pt = _dot_nt(wt_ref[...], h)
    cos, sin = cos_ref[...], sin_ref[...]
    qg = qg_ref[...]
    for hh in range(NSA_HEADS):
        xq = _rms_cols(pt[hh * 64:(hh + 1) * 64], qg)
        qt_ref[hh * 64:(hh + 1) * 64, :] = (_rope_cols(xq, cos, sin) * QK_SCALE_LOG2).astype(BF16)
    ones_rows = (lax.broadcasted_iota(jnp.int32, (V_ROWS - 64, tm), 0) == 0).astype(BF16)
    for g in range(NSA_GROUPS):
        for br in range(2):
            rows = 512 + br * 128 + g * 64
            vt_ref[g, br, 0:64, :] = pt[rows:rows + 64].astype(BF16)
            vt_ref[g, br, 64:V_ROWS, :] = ones_rows
    gt_ref[...] = _sigmoid(pt[768:800])
    mqg = mqg_ref[...]
    for hh in range(MEM_HEADS):
        xm = _rms_cols(pt[800 + hh * 64:864 + hh * 64], mqg)
        qmt_ref[hh * 64:(hh + 1) * 64, :] = (xm * QK_SCALE_LOG2).astype(BF16)

    pn = _dot(h, wn_ref[...])
    cn, sa, sb = cn_ref[...], sa_ref[...], sb_ref[...]
    bd, kg = bd_ref[...], kg_ref[...]
    kvc_ref[0] = _rope_rows(pn[:, 0:128], cn, sa, sb)
    kvc_ref[1] = pn[:, 128:256]
    ks = pn[:, 256:384]
    kw = pn[:, 384:512]
    ks = _rope_rows(ks * lax.rsqrt(_seg_mean_sq(ks, bd) + EPS) * kg, cn, sa, sb)
    kw = _rope_rows(kw * lax.rsqrt(_seg_mean_sq(kw, bd) + EPS) * kg, cn, sa, sb)
    lane = lax.broadcasted_iota(jnp.int32, (tm, 128), 1)
    row = lax.broadcasted_iota(jnp.int32, (tm, 128), 0)
    onehot = jnp.where(lane - 64 == (row >> SLC_SHIFT), 1.0, 0.0)
    lo_half = lane < 64
    kaug_ref[0, 0] = jnp.where(lo_half, ks, onehot).astype(BF16)
    kaug_ref[0, 1] = jnp.where(lo_half, kw, 0.0).astype(BF16)
    kaug_ref[1, 0] = jnp.where(lo_half, pltpu.roll(ks, 64, 1), onehot).astype(BF16)
    kaug_ref[1, 1] = jnp.where(lo_half, pltpu.roll(kw, 64, 1), 0.0).astype(BF16)
    hg_ref[...] = pn[:, 512:1536]


def _ffn_proj(x3d, ffn_gain, wg, wu, wd, mix_gain, wt, wn, q_gain, k_gain, mq_gain, rope):
    b, s, d = x3d.shape
    tm = TOK_TILE
    ns = s // tm
    cos_t, sin_t, cn, sa, sb = rope
    bd = jnp.asarray(np.kron(np.eye(2), np.full((64, 64), 1.0 / 64)), BF16)
    full = _resident
    out_shape = (
        jax.ShapeDtypeStruct((b, s, d), F32),
        jax.ShapeDtypeStruct((b, 512, s), BF16),
        jax.ShapeDtypeStruct((b, 2, 2, ns, V_ROWS, tm), BF16),
        jax.ShapeDtypeStruct((b, 32, s), F32),
        jax.ShapeDtypeStruct((b, 256, s), BF16),
        jax.ShapeDtypeStruct((b, 2, 2, s, 128), BF16),
        jax.ShapeDtypeStruct((b, 2, s, 128), F32),
        jax.ShapeDtypeStruct((b, s, 1024), F32),
    )
    out_specs = (
        pl.BlockSpec((None, tm, d), lambda bi, i: (bi, i, 0)),
        pl.BlockSpec((None, 512, tm), lambda bi, i: (bi, 0, i)),
        pl.BlockSpec((None, 2, 2, None, V_ROWS, tm), lambda bi, i: (bi, 0, 0, i, 0, 0)),
        pl.BlockSpec((None, 32, tm), lambda bi, i: (bi, 0, i)),
        pl.BlockSpec((None, 256, tm), lambda bi, i: (bi, 0, i)),
        pl.BlockSpec((None, 2, 2, tm, 128), lambda bi, i: (bi, 0, 0, i, 0)),
        pl.BlockSpec((None, 2, tm, 128), lambda bi, i: (bi, 0, i, 0)),
        pl.BlockSpec((None, tm, 1024), lambda bi, i: (bi, i, 0)),
    )
    in_specs = [
        pl.BlockSpec((None, tm, d), lambda bi, i: (bi, i, 0)),
        full((1, d)), full(wg.shape), full(wu.shape), full(wd.shape),
        full((1, d)), full(wt.shape), full(wn.shape),
        full((64, 1)), full((1, 128)), full((64, 1)),
        pl.BlockSpec((ROT_HALF, tm), lambda bi, i: (0, i)),
        pl.BlockSpec((ROT_HALF, tm), lambda bi, i: (0, i)),
        pl.BlockSpec((tm, 128), lambda bi, i: (i, 0)),
        pl.BlockSpec((tm, 128), lambda bi, i: (i, 0)),
        pl.BlockSpec((tm, 128), lambda bi, i: (i, 0)),
        full((128, 128)),
    ]
    return pl.pallas_call(
        _proj_body, grid=(b, ns), in_specs=in_specs, out_specs=out_specs, out_shape=out_shape,
        scratch_shapes=[pltpu.VMEM((tm, wg.shape[1]), BF16)],
        compiler_params=_cparams(("parallel", "parallel")), name="ffn1_mix_projection",
    )(x3d, ffn_gain.reshape(1, d), wg, wu, wd, mix_gain.reshape(1, d), wt, wn, q_gain.reshape(64, 1),
      jnp.tile(k_gain.reshape(1, 64), (1, 2)), mq_gain.reshape(64, 1), cos_t, sin_t, cn, sa, sb, bd)


def _rope_tables(s):
    pos = jnp.arange(s, dtype=F32)
    inv = ROPE_THETA ** (-(jnp.arange(0, ROT_DIM, 2, dtype=F32) / ROT_DIM))
    ang = pos[:, None] * inv[None, :]
    cos, sin = jnp.cos(ang), jnp.sin(ang)
    zeros = jnp.zeros((s, 64 - ROT_DIM), F32)
    cn = jnp.concatenate([cos, cos, jnp.ones((s, 64 - ROT_DIM), F32)], axis=1)
    sa = jnp.concatenate([-sin, jnp.zeros((s, ROT_HALF), F32), zeros], axis=1)
    sb = jnp.concatenate([jnp.zeros((s, ROT_HALF), F32), sin, zeros], axis=1)
    tile2 = lambda a: jnp.concatenate([a, a], axis=1)
    return cos.T, sin.T, tile2(cn), tile2(sa), tile2(sb)


def _cmp_body(kvc_ref, pos_ref, w1_ref, w2k_ref, w2vt_ref, kg_ref, kc_ref, vct_ref):
    nc = kvc_ref.shape[1] // CMP_STRIDE
    for kind in range(2):
        halves = []
        for part in range(2):
            x = jnp.concatenate(
                [(kvc_ref[kind, pl.ds(r, nc, stride=CMP_STRIDE), :]
                  + pos_ref[kind, part, :, r * 128:(r + 1) * 128]).astype(BF16) for r in range(CMP_STRIDE)],
                axis=1)
            halves.append(x)
        for g in range(NSA_GROUPS):
            second = _dot(halves[1], w1_ref[kind, g, 1])
            hid = _silu(_dot(halves[0], w1_ref[kind, g, 0]) + pltpu.roll(second, nc - 1, 0)).astype(BF16)
            if kind == 0:
                kc_ref[g] = _rms_rows(_dot(hid, w2k_ref[...]), kg_ref[...]).astype(BF16)
            else:
                vct_ref[g] = _dot_nt(w2vt_ref[...], hid).astype(BF16)


def _compress(kvc, pos, w1, w2k, w2vt, k_gain):
    b, _, s, _ = kvc.shape
    nc = s // CMP_STRIDE
    return pl.pallas_call(
        _cmp_body, grid=(b,),
        in_specs=[pl.BlockSpec((None, 2, s, 128), lambda bi: (bi, 0, 0, 0)),
                  _resident(pos.shape), _resident(w1.shape), _resident(w2k.shape), _resident(w2vt.shape),
                  _resident((1, 64))],
        out_specs=(pl.BlockSpec((None, 2, nc, 64), lambda bi: (bi, 0, 0, 0)),
                   pl.BlockSpec((None, 2, 64, nc), lambda bi: (bi, 0, 0, 0))),
        out_shape=(jax.ShapeDtypeStruct((b, 2, nc, 64), BF16), jax.ShapeDtypeStruct((b, 2, 64, nc), BF16)),
        compiler_params=_cparams(("parallel",)), name="nsa_compress",
    )(kvc, pos, w1, w2k, w2vt, k_gain.reshape(1, 64))


def _compress_weights(pos_k, pos_v, w1_k, w1_v):
    def pos_part(p):
        p = p.reshape(2, CMP_STRIDE, 1, 64)
        return jnp.broadcast_to(p, (2, CMP_STRIDE, NSA_GROUPS, 64)).reshape(2, 1, CMP_STRIDE * 128)

    def w1_part(w):
        hdim = w.shape[1]
        w = w.reshape(2, CMP_STRIDE, 1, 64, hdim)
        per_group = []
        for g in range(NSA_GROUPS):
            pads = [w if gg == g else jnp.zeros_like(w) for gg in range(NSA_GROUPS)]
            per_group.append(jnp.concatenate(pads, axis=2).reshape(2, CMP_STRIDE * 128, hdim))
        return jnp.stack(per_group)

    pos = jnp.stack([pos_part(pos_k), pos_part(pos_v)])
    w1 = jnp.stack([w1_part(w1_k), w1_part(w1_v)]).astype(BF16)
    return pos, w1


CMP_CLASS_ROWS = 128


def _cmpsel_variant(nc, nblk, qt_ref, kc_ref, vct_ref, oc_ref, bias_ref, s_scr):
    tq = qt_ref.shape[1]
    t0 = pl.program_id(2) * tq
    n_idx = lax.broadcasted_iota(jnp.int32, (nc, tq), 0)
    t_idx = t0 + lax.broadcasted_iota(jnp.int32, (nc, tq), 1)
    mask_bias = jnp.where(n_idx * CMP_STRIDE + (CMP_BLOCK - 1) <= t_idx, 0.0, NEG)
    t_row = t0 + lax.broadcasted_iota(jnp.int32, (1, tq), 1)
    kc = kc_ref[0:nc, :]
    vct = vct_ref[:, 0:nc]
    for hh in range(NSA_HPG):
        s_scr[hh, 0:nc, :] = _dot(kc, qt_ref[hh * 64:(hh + 1) * 64, :]) + mask_bias
    pcs = jnp.zeros((nc, tq), F32)
    for hh in range(NSA_HPG):
        m = jnp.max(s_scr[hh, 0:nc, :], axis=0, keepdims=True)
        p = jnp.exp2(s_scr[hh, 0:nc, :] - m)
        l = jnp.sum(p, axis=0, keepdims=True)
        pc = p * jnp.where(t_row >= CMP_BLOCK - 1, 1.0 / l, 0.0)
        oc_ref[hh * 64:(hh + 1) * 64, :] = _dot(vct, pc.astype(BF16))
        pcs = pcs + pc

    jj = lax.broadcasted_iota(jnp.int32, (nblk, nc), 0)
    nn = lax.broadcasted_iota(jnp.int32, (nblk, nc), 1)
    ov = jnp.where((nn * CMP_STRIDE < jj * SLC_BLOCK + SLC_BLOCK)
                   & (nn * CMP_STRIDE + CMP_BLOCK > jj * SLC_BLOCK), 1.0, 0.0).astype(BF16)
    hi, lo = _split2(pcs)
    imp = _dot(ov, hi) + _dot(ov, lo)

    j = lax.broadcasted_iota(jnp.int32, (nblk, tq), 0)
    cur = (t0 + lax.broadcasted_iota(jnp.int32, (nblk, tq), 1)) >> SLC_SHIFT
    forced = (j == 0) | (j == cur) | (j == cur - 1)
    bias = jnp.where(forced & (j <= cur), 0.0, NEG)
    imp = jnp.where((j <= cur) & jnp.logical_not(forced), imp, -1.0)
    jf = j.astype(F32)
    for _ in range(SLC_TOPK - 3):
        v = jnp.max(imp, axis=0, keepdims=True)
        first = jnp.min(jnp.where(imp == v, jf, float(nblk)), axis=0, keepdims=True)
        pick = jf == first
        bias = jnp.where(pick & (v >= 0.0), 0.0, bias)
        imp = jnp.where(pick, -3e38, imp)
    bias_ref[0:nblk, :] = bias
    if nblk < bias_ref.shape[0]:
        bias_ref[nblk:, :] = jnp.full((bias_ref.shape[0] - nblk, tq), NEG, F32)


def _cmpsel_body(qt_ref, kc_ref, vct_ref, oc_ref, bias_ref, s_scr):
    tq = qt_ref.shape[1]
    nc_total = kc_ref.shape[0]
    tiles_per_class = CMP_CLASS_ROWS // (tq // CMP_STRIDE)
    cls = pl.program_id(2) // tiles_per_class
    for c in range(nc_total // CMP_CLASS_ROWS):
        nc = (c + 1) * CMP_CLASS_ROWS
        nblk = min(nc * CMP_STRIDE // SLC_BLOCK, bias_ref.shape[0])
        pl.when(cls == c)(functools.partial(_cmpsel_variant, nc, nblk, qt_ref, kc_ref, vct_ref,
                                            oc_ref, bias_ref, s_scr))


def _cmp_select(qt, kc, vct, *, tq=TOK_TILE):
    b, _, s = qt.shape
    nc = kc.shape[2]
    nblk = MAX_BLOCKS
    assert s // SLC_BLOCK <= MAX_BLOCKS and s // SLC_BLOCK >= SLC_TOPK and nc % CMP_CLASS_ROWS == 0
    return pl.pallas_call(
        _cmpsel_body, grid=(b, NSA_GROUPS, s // tq),
        in_specs=[pl.BlockSpec((None, 256, tq), lambda bi, g, i: (bi, g, i)),
                  pl.BlockSpec((None, None, nc, 64), lambda bi, g, i: (bi, g, 0, 0)),
                  pl.BlockSpec((None, None, 64, nc), lambda bi, g, i: (bi, g, 0, 0))],
        out_specs=(pl.BlockSpec((None, 256, tq), lambda bi, g, i: (bi, g, i)),
                   pl.BlockSpec((None, None, nblk, tq), lambda bi, g, i: (bi, g, 0, i))),
        out_shape=(jax.ShapeDtypeStruct((b, 512, s), F32),
                   jax.ShapeDtypeStruct((b, NSA_GROUPS, nblk, s), F32)),
        scratch_shapes=[pltpu.VMEM((NSA_HPG, nc, tq), F32)],
        compiler_params=_cparams(("parallel", "parallel", "parallel")), name="nsa_compressed_select",
    )(qt, kc, vct)


def _flash_step(s_ref, vt, m_ref, acc_ref):
    m_old = m_ref[...]
    m_new = jnp.maximum(m_old, jnp.max(s_ref[...], axis=0, keepdims=True))
    p = jnp.exp2(s_ref[...] - m_new)
    acc_ref[...] = jnp.exp2(m_old - m_new) * acc_ref[...] + _dot(vt, p.astype(BF16))
    m_ref[...] = m_new


SEL, WIN = 0, 1
BIAS_ROWS = 16


def _slcwin_body(qt_ref, bias_ref, kaug_ref, vt_ref, oc_ref, gt_ref, mb_ref, y_ref,
                 q_scr, m_scr, acc_scr, s_scr):
    tq = qt_ref.shape[1]
    tk = vt_ref.shape[3]
    assert tq == tk and WINDOW <= tk and tk == 8 * SLC_BLOCK
    diag = pl.program_id(2)

    qa, qb, qw = 0, 1, 2
    zeros = jnp.zeros((64, tq), BF16)
    for slot in (qa, qb, qw):
        for hh in range(NSA_HPG):
            q_scr[slot, hh, 0:64, :] = qt_ref[hh * 64:(hh + 1) * 64, :]
            q_scr[slot, hh, 64:128, :] = zeros
    m_scr[...] = jnp.full(m_scr.shape, NEG, F32)
    acc_scr[...] = jnp.zeros(acc_scr.shape, F32)

    def set_selection_bias(kt, slot):
        rows = bias_ref[pl.ds(pl.multiple_of(kt * 8, 8), 8), :]
        b16 = jnp.concatenate([rows, jnp.zeros_like(rows)], axis=0).astype(BF16)
        for hh in range(NSA_HPG):
            q_scr[slot, hh, 64:64 + BIAS_ROWS, :] = b16

    def tile_step(br, qslot, sslot, kt, mask_bias):
        k = kaug_ref[br, pl.ds(pl.multiple_of(kt * tk, tk), tk), :]
        v = vt_ref[br, kt]
        for hh in range(NSA_HPG):
            s = _dot(k, q_scr[qslot, hh])
            s_scr[sslot, hh] = s if mask_bias is None else s + mask_bias
        for hh in range(NSA_HPG):
            _flash_step(s_scr.at[sslot, hh], v, m_scr.at[br, hh], acc_scr.at[br, hh])

    def tile_pair(j, carry):
        kt = 2 * j
        set_selection_bias(kt, qa)
        set_selection_bias(kt + 1, qb)
        tile_step(SEL, qa, 0, kt, None)
        tile_step(SEL, qb, 1, kt + 1, None)
        return carry

    lax.fori_loop(0, diag >> 1, tile_pair, 0)

    @pl.when((diag & 1) == 1)
    def _():
        set_selection_bias(diag - 1, qa)
        tile_step(SEL, qa, 0, diag - 1, None)

    set_selection_bias(diag, qa)

    def masked_tile(j, carry):
        first = j == 0
        sel = j == 1
        kt = jnp.where(first, jnp.maximum(diag - 1, 0), diag)
        extra = jnp.where(first & (diag == 0), NEG, 0.0)
        tile_step(jnp.where(sel, SEL, WIN), jnp.where(sel, qa, qw), 0, kt, mb_ref[jnp.where(first, 1, 0)] + extra)
        return carry

    lax.fori_loop(0, 3, masked_tile, 0)

    gt = gt_ref[...]
    for hh in range(NSA_HPG):
        o_s = acc_scr[SEL, hh, 0:64, :] * (1.0 / acc_scr[SEL, hh, 64:65, :])
        o_w = acc_scr[WIN, hh, 0:64, :] * (1.0 / acc_scr[WIN, hh, 64:65, :])
        y_ref[hh * 64:(hh + 1) * 64, :] = (gt[3 * hh:3 * hh + 1] * oc_ref[hh * 64:(hh + 1) * 64, :]
                                          + gt[3 * hh + 1:3 * hh + 2] * o_s
                                          + gt[3 * hh + 2:3 * hh + 3] * o_w)


def _slc_win(qt, bias, kaug, vt, oc, gt):
    b, _, s = qt.shape
    nblk = bias.shape[2]
    ns, tk = vt.shape[3], vt.shape[5]
    tq = tk
    key_rel, t_rel = np.arange(tk)[:, None], np.arange(tq)[None, :]
    mask_bias = jnp.asarray(np.stack([np.where(key_rel <= t_rel, 0.0, NEG),
                                      np.where(t_rel + tk - key_rel < WINDOW, 0.0, NEG)]), F32)
    qblk = pl.BlockSpec((None, 256, tq), lambda bi, g, i: (bi, g, i))
    return pl.pallas_call(
        _slcwin_body, grid=(b, NSA_GROUPS, s // tq),
        in_specs=[qblk,
                  pl.BlockSpec((None, None, nblk, tq), lambda bi, g, i: (bi, g, 0, i)),
                  pl.BlockSpec((None, None, 2, s, 128), lambda bi, g, i: (bi, g, 0, 0, 0)),
                  pl.BlockSpec((None, None, 2, ns, V_ROWS, tk), lambda bi, g, i: (bi, g, 0, 0, 0, 0)),
                  qblk,
                  pl.BlockSpec((None, None, GATE_ROWS, tq), lambda bi, g, i: (bi, g, 0, i)),
                  pl.BlockSpec((2, tk, tq), lambda bi, g, i: (0, 0, 0))],
        out_specs=qblk,
        out_shape=jax.ShapeDtypeStruct((b, 512, s), F32),
        scratch_shapes=[pltpu.VMEM((3, NSA_HPG, 128, tq), BF16),
                        pltpu.VMEM((2, NSA_HPG, 1, tq), F32), pltpu.VMEM((2, NSA_HPG, V_ROWS, tq), F32),
                        pltpu.VMEM((2, NSA_HPG, tk, tq), F32)],
        compiler_params=_cparams(("parallel", "parallel", "arbitrary")), name="nsa_selected_window",
    )(qt, bias, kaug, vt, oc, gt.reshape(b, NSA_GROUPS, GATE_ROWS, s), mask_bias)


def _hgrn_consts():
    c = HGRN_CHUNK
    t = np.arange(c)
    lower = (t[None, :] <= t[:, None]).astype(np.float32)
    rows = [lower]
    masks = []
    for half in HGRN_LEVELS:
        mid = (t // (2 * half)) * (2 * half) + half - 1
        if half < 8:
            rows.append(lower[mid])
        same = (t[:, None] // (2 * half)) == (t[None, :] // (2 * half))
        right = (t[:, None] & half) != 0
        left = (t[None, :] & half) == 0
        masks.append((same & right & left).astype(np.float32))
    masks.append(np.eye(c, dtype=np.float32))
    mall = np.concatenate(rows, axis=0)
    lvl = np.stack([np.tile(mk.T, (1, HGRN_HEADS)) for mk in masks])
    bdm = np.kron(np.eye(HGRN_HEADS), np.ones((64, 64), np.float32))
    return jnp.asarray(mall, BF16), jnp.asarray(lvl, F32), jnp.asarray(bdm, F32), jnp.asarray(bdm / 64, BF16)


def _hgrn_body(hg_ref, lb_ref, og_ref, mall_ref, lvl_ref, bdm_ref, bdn_ref, y_ref, st_scr):
    c = HGRN_CHUNK
    w = HGRN_WIDTH

    @pl.when(pl.program_id(1) == 0)
    def _():
        st_scr[...] = jnp.zeros(st_scr.shape, F32)

    chunks = range(hg_ref.shape[0] // c)
    lb = lb_ref[...]
    lane = lax.broadcasted_iota(jnp.int32, (c, w), 1)
    trow = lax.broadcasted_iota(jnp.int32, (c, w), 0)
    head_masks = [(lane >> 6) == hh for hh in range(HGRN_HEADS)]
    nlev = len(HGRN_LEVELS)

    def stack_heads(x):
        x16 = x.astype(BF16)
        return jnp.concatenate([jnp.where(hm, x16, 0) for hm in head_masks], axis=0)

    def row_bcast(x, half):
        return jnp.concatenate([jnp.broadcast_to(x[p + half - 1:p + half, :], (2 * half, w))
                                for p in range(0, c, 2 * half)], axis=0)

    qa, kk, v16, logf = [], [], [], []
    for ci in chunks:
        rows = slice(ci * c, (ci + 1) * c)
        qa.append(_silu(hg_ref[rows, 0:w]) * (HEAD_DIM ** -0.5))
        fg = lb + (1.0 - lb) * _sigmoid(hg_ref[rows, w:2 * w])
        kk.append(1.0 - fg)
        logf.append(jnp.log(fg))
        v16.append(hg_ref[rows, 2 * w:3 * w].astype(BF16))

    mall = mall_ref[...]
    r_all = sum(_dot(mall, part) for part in _split3(jnp.concatenate(logf, axis=1)))
    bcum = [r_all[0:c, ci * w:(ci + 1) * w] for ci in chunks]

    attn = [lvl_ref[nlev] * _dot_nt(kk[ci].astype(BF16), stack_heads(qa[ci])) for ci in chunks]
    fine = 0
    for li, half in enumerate(HGRN_LEVELS):
        right = (trow & half) != 0
        for ci in chunks:
            if half >= 8:
                ref_pt = row_bcast(bcum[ci], half)
            else:
                ref_pt = r_all[(fine + 1) * c:(fine + 2) * c, ci * w:(ci + 1) * w]
            e = jnp.exp(jnp.where(right, bcum[ci] - ref_pt, ref_pt - bcum[ci]))
            qt = jnp.where(right, qa[ci] * e, 0.0)
            kt = jnp.where(right, 0.0, kk[ci] * e).astype(BF16)
            attn[ci] = attn[ci] + lvl_ref[li] * _dot_nt(kt, stack_heads(qt))
        if half < 8:
            fine += 1

    intra, upd, decay, qb = [], [], [], []
    for ci in chunks:
        x = _dot_tn(attn[ci].astype(BF16), v16[ci])
        intra.append(sum(jnp.where(head_masks[hh], x[hh * c:(hh + 1) * c], 0.0) for hh in range(HGRN_HEADS)))
        b_last = bcum[ci][c - 1:c, :]
        kl = (kk[ci] * jnp.exp(b_last - bcum[ci])).astype(BF16)
        upd.append(bdm_ref[...] * _dot_tn(v16[ci], kl))
        decay.append(jnp.exp(b_last))
        qb.append((qa[ci] * jnp.exp(bcum[ci])).astype(BF16))

    st = st_scr[...]
    inter = []
    for ci in chunks:
        inter.append(_dot_nt(qb[ci], st.astype(BF16)))
        st = st * decay[ci] + upd[ci]
    st_scr[...] = st

    for ci in chunks:
        rows = slice(ci * c, (ci + 1) * c)
        o = inter[ci] + intra[ci]
        hi, lo = _split2(o * o)
        ms = _dot(hi, bdn_ref[...]) + _dot(lo, bdn_ref[...])
        y_ref[rows, :] = o * lax.rsqrt(ms + EPS) * og_ref[...] * _silu(hg_ref[rows, 3 * w:4 * w])


def _hgrn(hg, lower_bound, out_gain):
    b, s, _ = hg.shape
    rows = HGRN_CB * HGRN_CHUNK
    mall, lvl, bdm, bdn = _hgrn_consts()
    full = lambda shape: pl.BlockSpec(shape, lambda bi, i: (0,) * len(shape))
    return pl.pallas_call(
        _hgrn_body, grid=(b, s // rows),
        in_specs=[pl.BlockSpec((None, rows, 4 * HGRN_WIDTH), lambda bi, i: (bi, i, 0)),
                  full((1, HGRN_WIDTH)), full((1, HGRN_WIDTH)),
                  full(mall.shape), full(lvl.shape), full(bdm.shape), full(bdn.shape)],
        out_specs=pl.BlockSpec((None, rows, HGRN_WIDTH), lambda bi, i: (bi, i, 0)),
        out_shape=jax.ShapeDtypeStruct((b, s, HGRN_WIDTH), F32),
        scratch_shapes=[pltpu.VMEM((HGRN_WIDTH, HGRN_WIDTH), F32)],
        compiler_params=_cparams(("parallel", "arbitrary")), name="hgrn2_chunks",
    )(hg, lower_bound.reshape(1, -1), out_gain.reshape(1, -1), mall, lvl, bdm, bdn)


def _memkv_body(mem_ref, mg_ref, wk_ref, wvt_ref, kg_ref, kh_ref, vht_ref):
    m = mem_ref.shape[0]
    mn = _rms_rows(mem_ref[...], mg_ref[...]).astype(BF16)
    k = _dot(mn, wk_ref[...])
    vt = _dot_nt(wvt_ref[...], mn)
    ones_rows = (lax.broadcasted_iota(jnp.int32, (V_ROWS - 64, m), 0) == 0).astype(BF16)
    for hh in range(MEM_HEADS):
        kh_ref[hh] = _rms_rows(k[:, hh * 64:(hh + 1) * 64], kg_ref[...]).astype(BF16)
        vht_ref[hh, 0:64, :] = vt[hh * 64:(hh + 1) * 64].astype(BF16)
        vht_ref[hh, 64:V_ROWS, :] = ones_rows


def _mem_kv(mem, mem_gain, wk, wvt, k_gain):
    b, m, d = mem.shape
    full = lambda shape: pl.BlockSpec(shape, lambda bi: (0,) * len(shape))
    return pl.pallas_call(
        _memkv_body, grid=(b,),
        in_specs=[pl.BlockSpec((None, m, d), lambda bi: (bi, 0, 0)), full((1, d)),
                  full(wk.shape), full(wvt.shape), full((1, 64))],
        out_specs=(pl.BlockSpec((None, MEM_HEADS, m, 64), lambda bi: (bi, 0, 0, 0)),
                   pl.BlockSpec((None, MEM_HEADS, V_ROWS, m), lambda bi: (bi, 0, 0, 0))),
        out_shape=(jax.ShapeDtypeStruct((b, MEM_HEADS, m, 64), BF16),
                   jax.ShapeDtypeStruct((b, MEM_HEADS, V_ROWS, m), BF16)),
        compiler_params=_cparams(("parallel",)), name="memory_kv",
    )(mem, mem_gain.reshape(1, d), wk, wvt, k_gain.reshape(1, 64))


def _out_body(x_ref, ynt_ref, yh_ref, qmt_ref, kh_ref, vht_ref, ng_ref, mg_ref, wo_ref,
              fg_ref, wg_ref, wu_ref, wd_ref, o_ref, a_scr):
    y_mem = []
    for hh in range(MEM_HEADS):
        s = _dot(kh_ref[hh], qmt_ref[hh * 64:(hh + 1) * 64, :])
        p = jnp.exp2(s - jnp.max(s, axis=0, keepdims=True))
        o = _dot(vht_ref[hh], p.astype(BF16))
        y_mem.append(o[0:64] * (1.0 / o[64:65]))
    mem = _rms_cols(jnp.concatenate(y_mem, axis=0), mg_ref[...]).astype(BF16)
    nsa = _rms_cols(ynt_ref[...], ng_ref[...]).astype(BF16)
    acc = _dot_tn(nsa, wo_ref[0:NSA_WIDTH, :])
    acc = acc + _dot(yh_ref[...].astype(BF16), wo_ref[NSA_WIDTH:NSA_WIDTH + HGRN_WIDTH, :])
    acc = acc + _dot_tn(mem, wo_ref[NSA_WIDTH + HGRN_WIDTH:, :])
    o_ref[...] = _ffn_half_step(x_ref[...] + acc, fg_ref, wg_ref, wu_ref, wd_ref, a_scr)


def _out_ffn(x3d, ynt, yh, qmt, kh, vht, nsa_gain, mem_gain, wo, ffn_gain, wg, wu, wd, *, tm=TOK_TILE):
    b, s, d = x3d.shape
    m = kh.shape[2]
    return pl.pallas_call(
        _out_body, grid=(b, s // tm),
        in_specs=[pl.BlockSpec((None, tm, d), lambda bi, i: (bi, i, 0)),
                  pl.BlockSpec((None, NSA_WIDTH, tm), lambda bi, i: (bi, 0, i)),
                  pl.BlockSpec((None, tm, HGRN_WIDTH), lambda bi, i: (bi, i, 0)),
                  pl.BlockSpec((None, MEM_WIDTH, tm), lambda bi, i: (bi, 0, i)),
                  pl.BlockSpec((None, MEM_HEADS, m, 64), lambda bi, i: (bi, 0, 0, 0)),
                  pl.BlockSpec((None, MEM_HEADS, V_ROWS, m), lambda bi, i: (bi, 0, 0, 0)),
                  _resident((NSA_WIDTH, 1)), _resident((MEM_WIDTH, 1)), _resident(wo.shape),
                  _resident((1, d)), _resident(wg.shape), _resident(wu.shape), _resident(wd.shape)],
        out_specs=pl.BlockSpec((None, tm, d), lambda bi, i: (bi, i, 0)),
        out_shape=jax.ShapeDtypeStruct((b, s, d), F32),
        scratch_shapes=[pltpu.VMEM((tm, wg.shape[1]), BF16)],
        compiler_params=_cparams(("parallel", "parallel")), name="mix_out_ffn2",
    )(x3d, ynt, yh, qmt, kh, vht, nsa_gain.reshape(-1, 1), mem_gain.reshape(-1, 1), wo,
      ffn_gain.reshape(1, d), wg, wu, wd)


def _layer(x, mem, ffn1, ffn2, mix_norm, w_in, w_out, nsa_q_norm, nsa_k_norm, cmp_pos_k, cmp_w1_k, cmp_w2_k,
           cmp_pos_v, cmp_w1_v, cmp_w2_v, nsa_out_norm, lower_bound, hgrn_out_norm,
           mem_norm, mem_w_k, mem_w_v, mem_q_norm, mem_k_norm, mem_out_norm):
    b, s, d = x.shape
    sizes = (512, 128, 128, 128, 128, 128, 128, 24, 256, 256, 256, 256, 256)
    offs = np.concatenate([[0], np.cumsum(sizes)])
    col = lambda i: w_in[:, offs[i]:offs[i + 1]]
    (q_a, k_c, v_c, k_s, v_s, k_w, v_w, g_a, q_h, f_h, i_h, g_h, q_m) = [col(i) for i in range(13)]
    gpad = jnp.zeros((d, GATE_ROWS - 3 * NSA_HPG), w_in.dtype)
    wt = jnp.concatenate([q_a, v_s, v_w, g_a[:, :3 * NSA_HPG], gpad, g_a[:, 3 * NSA_HPG:], gpad, q_m],
                         axis=1).T.astype(BF16)
    wn = jnp.concatenate([k_c, v_c, k_s, k_w, q_h, f_h, i_h, g_h], axis=1).astype(BF16)

    x1, qt, vt, gt, qmt, kaug, kvc, hg = _ffn_proj(
        x, *ffn1, mix_norm, wt, wn, nsa_q_norm, nsa_k_norm, mem_q_norm, _rope_tables(s))

    cmp_pos, cmp_w1 = _compress_weights(cmp_pos_k, cmp_pos_v, cmp_w1_k, cmp_w1_v)
    kc, vct = _compress(kvc, cmp_pos, cmp_w1, cmp_w2_k.astype(BF16), cmp_w2_v.T.astype(BF16), nsa_k_norm)
    oc, bias = _cmp_select(qt, kc, vct)
    y_nsa = _slc_win(qt, bias, kaug, vt, oc, gt)

    y_hgrn = _hgrn(hg, lower_bound, hgrn_out_norm)

    kh, vht = _mem_kv(mem, mem_norm, mem_w_k.astype(BF16), mem_w_v.T.astype(BF16), mem_k_norm)
    return _out_ffn(x1, y_nsa, y_hgrn, qmt, kh, vht, nsa_out_norm, mem_out_norm, w_out.astype(BF16), *ffn2)


def kernel(x, mem, ffn1_norm, ffn1_w_gate, ffn1_w_up, ffn1_w_down, mix_norm, w_in, w_out, nsa_q_norm, nsa_k_norm, cmp_pos_k, cmp_w1_k, cmp_w2_k, cmp_pos_v, cmp_w1_v, cmp_w2_v, nsa_out_norm, hgrn_lb_logits, hgrn_out_norm, mem_norm, mem_w_k, mem_w_v, mem_q_norm, mem_k_norm, mem_out_norm, ffn2_norm, ffn2_w_gate, ffn2_w_up, ffn2_w_down):
    b, s, d = x.shape
    depth = ffn1_norm.shape[0]
    lower_bounds = jnp.cumsum(jax.nn.softmax(hgrn_lb_logits.astype(F32), axis=0), axis=0)
    bf = lambda a: a.astype(BF16)
    for l in range(depth):
        x = _layer(x, mem, (ffn1_norm[l], bf(ffn1_w_gate[l]), bf(ffn1_w_up[l]), bf(ffn1_w_down[l])),
                   (ffn2_norm[l], bf(ffn2_w_gate[l]), bf(ffn2_w_up[l]), bf(ffn2_w_down[l])),
                   mix_norm[l], w_in[l], w_out[l], nsa_q_norm[l], nsa_k_norm[l],
                   cmp_pos_k[l], cmp_w1_k[l], cmp_w2_k[l], cmp_pos_v[l], cmp_w1_v[l], cmp_w2_v[l],
                   nsa_out_norm[l], lower_bounds[l], hgrn_out_norm[l],
                   mem_norm[l], mem_w_k[l], mem_w_v[l], mem_q_norm[l], mem_k_norm[l], mem_out_norm[l])
    return x
```

```python
import functools

import numpy as np
import jax
import jax.numpy as jnp
from jax import lax
from jax.experimental import pallas as pl
from jax.experimental.pallas import tpu as pltpu

F32 = jnp.float32
BF16 = jnp.bfloat16

HEAD_DIM = 64
ROT_DIM = 16
ROT_HALF = 8
ROPE_THETA = 500000.0
NSA_HEADS = 8
NSA_GROUPS = 2
NSA_HPG = 4
CMP_BLOCK = 32
CMP_STRIDE = 16
SLC_BLOCK = 64
SLC_SHIFT = 6
SLC_TOPK = 16
WINDOW = 512
FORCED_SCORE = 1e4
HGRN_HEADS = 4
HGRN_CHUNK = 64
HGRN_WIDTH = 256
MEM_HEADS = 4
MEM_WIDTH = 256
NSA_WIDTH = 512
EPS = 1e-6
NEG = -1e30
QK_SCALE_LOG2 = HEAD_DIM ** -0.5 * 1.4426950408889634

VMEM_LIMIT = 56 * 1024 * 1024
MAX_BLOCKS = 128
GATE_ROWS = 16
V_ROWS = 80
TOK_TILE = 512
HGRN_CB = 8
HGRN_LEVELS = (32, 16, 8, 4, 2, 1)

NT_DIMS = (((1,), (1,)), ((), ()))
TN_DIMS = (((0,), (0,)), ((), ()))


def _cparams(sem):
    return pltpu.CompilerParams(dimension_semantics=sem, vmem_limit_bytes=VMEM_LIMIT)


def _dot(a, b):
    return jnp.dot(a, b, preferred_element_type=F32)


def _dot_nt(a, b):
    return lax.dot_general(a, b, NT_DIMS, preferred_element_type=F32)


def _dot_tn(a, b):
    return lax.dot_general(a, b, TN_DIMS, preferred_element_type=F32)


def _sigmoid(x):
    return 1.0 / (1.0 + jnp.exp(-x))


def _silu(x):
    return x * _sigmoid(x)


def _split2(x):
    hi = x.astype(BF16)
    lo = (x - hi.astype(F32)).astype(BF16)
    return hi, lo


def _split3(x):
    hi = x.astype(BF16)
    r1 = x - hi.astype(F32)
    mid = r1.astype(BF16)
    lo = (r1 - mid.astype(F32)).astype(BF16)
    return hi, mid, lo


def _rms_rows(x, gain_row):
    ms = jnp.mean(x * x, axis=-1, keepdims=True)
    return x * lax.rsqrt(ms + EPS) * gain_row


def _rms_cols(x, gain_col):
    ms = jnp.mean(x * x, axis=0, keepdims=True)
    return x * lax.rsqrt(ms + EPS) * gain_col


def _seg_mean_sq(x, bd):
    hi, lo = _split2(x * x)
    return _dot(hi, bd) + _dot(lo, bd)


FFN_CHUNK = 256


def _ffn_half_step(x, g_ref, wg_ref, wu_ref, wd_ref, a_scr):
    xn = _rms_rows(x, g_ref[...]).astype(BF16)
    d_ff = wg_ref.shape[1]
    for c in range(d_ff // FFN_CHUNK):
        sl = slice(c * FFN_CHUNK, (c + 1) * FFN_CHUNK)
        g = _dot(xn, wg_ref[:, sl])
        u = _dot(xn, wu_ref[:, sl])
        a_scr[:, sl] = (_silu(g) * u).astype(BF16)
    return x + 0.5 * _dot(a_scr[...], wd_ref[...])


def _resident(shape):
    return pl.BlockSpec(shape, lambda *_: (0,) * len(shape), pipeline_mode=pl.Buffered(1))


def _rope_cols(xn, cos, sin):
    x0, x1 = xn[0:ROT_HALF], xn[ROT_HALF:ROT_DIM]
    return jnp.concatenate([x0 * cos - x1 * sin, x1 * cos + x0 * sin, xn[ROT_DIM:]], axis=0)


def _rope_rows(x, cn, sa, sb):
    return x * cn + pltpu.roll(x, 128 - ROT_HALF, 1) * sa + pltpu.roll(x, ROT_HALF, 1) * sb


def _proj_body(x_ref, fg_ref, wg_ref, wu_ref, wd_ref, mg_ref, wt_ref, wn_ref, qg_ref, kg_ref, mqg_ref,
               cos_ref, sin_ref, cn_ref, sa_ref, sb_ref, bd_ref,
               x1_ref, qt_ref, vt_ref, gt_ref, qmt_ref, kaug_ref, kvc_ref, hg_ref, a_scr):
    tm = x_ref.shape[0]
    assert tm == 8 * SLC_BLOCK
    x1 = _ffn_half_step(x_ref[...], fg_ref, wg_ref, wu_ref, wd_ref, a_scr)
    x1_ref[...] = x1
    h = _rms_rows(x1, mg_ref[...]).astype(BF16)

    pt = _dot_nt(wt_ref[...], h)
    cos, sin = cos_ref[...], sin_ref[...]
    qg = qg_ref[...]
    for hh in range(NSA_HEADS):
        xq = _rms_cols(pt[hh * 64:(hh + 1) * 64], qg)
        qt_ref[hh * 64:(hh + 1) * 64, :] = (_rope_cols(xq, cos, sin) * QK_SCALE_LOG2).astype(BF16)
    ones_rows = (lax.broadcasted_iota(jnp.int32, (V_ROWS - 64, tm), 0) == 0).astype(BF16)
    for g in range(NSA_GROUPS):
        for br in range(2):
            rows = 512 + br * 128 + g * 64
            vt_ref[g, br, 0:64, :] = pt[rows:rows + 64].astype(BF16)
            vt_ref[g, br, 64:V_ROWS, :] = ones_rows
    gt_ref[...] = _sigmoid(pt[768:800])
    mqg = mqg_ref[...]
    for hh in range(MEM_HEADS):
        xm = _rms_cols(pt[800 + hh * 64:864 + hh * 64], mqg)
        qmt_ref[hh * 64:(hh + 1) * 64, :] = (xm * QK_SCALE_LOG2).astype(BF16)

    pn = _dot(h, wn_ref[...])
    cn, sa, sb = cn_ref[...], sa_ref[...], sb_ref[...]
    bd, kg = bd_ref[...], kg_ref[...]
    kvc_ref[0] = _rope_rows(pn[:, 0:128], cn, sa, sb)
    kvc_ref[1] = pn[:, 128:256]
    ks = pn[:, 256:384]
    kw = pn[:, 384:512]
    ks = _rope_rows(ks * lax.rsqrt(_seg_mean_sq(ks, bd) + EPS) * kg, cn, sa, sb)
    kw = _rope_rows(kw * lax.rsqrt(_seg_mean_sq(kw, bd) + EPS) * kg, cn, sa, sb)
    lane = lax.broadcasted_iota(jnp.int32, (tm, 128), 1)
    row = lax.broadcasted_iota(jnp.int32, (tm, 128), 0)
    onehot = jnp.where(lane - 64 == (row >> SLC_SHIFT), 1.0, 0.0)
    lo_half = lane < 64
    kaug_ref[0, 0] = jnp.where(lo_half, ks, onehot).astype(BF16)
    kaug_ref[0, 1] = jnp.where(lo_half, kw, 0.0).astype(BF16)
    kaug_ref[1, 0] = jnp.where(lo_half, pltpu.roll(ks, 64, 1), onehot).astype(BF16)
    kaug_ref[1, 1] = jnp.where(lo_half, pltpu.roll(kw, 64, 1), 0.0).astype(BF16)
    hg_ref[...] = pn[:, 512:1536]


def _ffn_proj(x3d, ffn_gain, wg, wu, wd, mix_gain, wt, wn, q_gain, k_gain, mq_gain, rope):
    b, s, d = x3d.shape
    tm = TOK_TILE
    ns = s // tm
    cos_t, sin_t, cn, sa, sb = rope
    bd = jnp.asarray(np.kron(np.eye(2), np.full((64, 64), 1.0 / 64)), BF16)
    full = _resident
    out_shape = (
        jax.ShapeDtypeStruct((b, s, d), F32),
        jax.ShapeDtypeStruct((b, 512, s), BF16),
        jax.ShapeDtypeStruct((b, 2, 2, ns, V_ROWS, tm), BF16),
        jax.ShapeDtypeStruct((b, 32, s), F32),
        jax.ShapeDtypeStruct((b, 256, s), BF16),
        jax.ShapeDtypeStruct((b, 2, 2, s, 128), BF16),
        jax.ShapeDtypeStruct((b, 2, s, 128), F32),
        jax.ShapeDtypeStruct((b, s, 1024), F32),
    )
    out_specs = (
        pl.BlockSpec((None, tm, d), lambda bi, i: (bi, i, 0)),
        pl.BlockSpec((None, 512, tm), lambda bi, i: (bi, 0, i)),
        pl.BlockSpec((None, 2, 2, None, V_ROWS, tm), lambda bi, i: (bi, 0, 0, i, 0, 0)),
        pl.BlockSpec((None, 32, tm), lambda bi, i: (bi, 0, i)),
        pl.BlockSpec((None, 256, tm), lambda bi, i: (bi, 0, i)),
        pl.BlockSpec((None, 2, 2, tm, 128), lambda bi, i: (bi, 0, 0, i, 0)),
        pl.BlockSpec((None, 2, tm, 128), lambda bi, i: (bi, 0, i, 0)),
        pl.BlockSpec((None, tm, 1024), lambda bi, i: (bi, i, 0)),
    )
    in_specs = [
        pl.BlockSpec((None, tm, d), lambda bi, i: (bi, i, 0)),
        full((1, d)), full(wg.shape), full(wu.shape), full(wd.shape),
        full((1, d)), full(wt.shape), full(wn.shape),
        full((64, 1)), full((1, 128)), full((64, 1)),
        pl.BlockSpec((ROT_HALF, tm), lambda bi, i: (0, i)),
        pl.BlockSpec((ROT_HALF, tm), lambda bi, i: (0, i)),
        pl.BlockSpec((tm, 128), lambda bi, i: (i, 0)),
        pl.BlockSpec((tm, 128), lambda bi, i: (i, 0)),
        pl.BlockSpec((tm, 128), lambda bi, i: (i, 0)),
        full((128, 128)),
    ]
    return pl.pallas_call(
        _proj_body, grid=(b, ns), in_specs=in_specs, out_specs=out_specs, out_shape=out_shape,
        scratch_shapes=[pltpu.VMEM((tm, wg.shape[1]), BF16)],
        compiler_params=_cparams(("parallel", "parallel")), name="ffn1_mix_projection",
    )(x3d, ffn_gain.reshape(1, d), wg, wu, wd, mix_gain.reshape(1, d), wt, wn, q_gain.reshape(64, 1),
      jnp.tile(k_gain.reshape(1, 64), (1, 2)), mq_gain.reshape(64, 1), cos_t, sin_t, cn, sa, sb, bd)


def _rope_tables(s):
    pos = jnp.arange(s, dtype=F32)
    inv = ROPE_THETA ** (-(jnp.arange(0, ROT_DIM, 2, dtype=F32) / ROT_DIM))
    ang = pos[:, None] * inv[None, :]
    cos, sin = jnp.cos(ang), jnp.sin(ang)
    zeros = jnp.zeros((s, 64 - ROT_DIM), F32)
    cn = jnp.concatenate([cos, cos, jnp.ones((s, 64 - ROT_DIM), F32)], axis=1)
    sa = jnp.concatenate([-sin, jnp.zeros((s, ROT_HALF), F32), zeros], axis=1)
    sb = jnp.concatenate([jnp.zeros((s, ROT_HALF), F32), sin, zeros], axis=1)
    tile2 = lambda a: jnp.concatenate([a, a], axis=1)
    return cos.T, sin.T, tile2(cn), tile2(sa), tile2(sb)


def _cmp_body(kvc_ref, pos_ref, w1_ref, w2k_ref, w2vt_ref, kg_ref, kc_ref, vct_ref):
    nc = kvc_ref.shape[1] // CMP_STRIDE
    for kind in range(2):
        halves = []
        for part in range(2):
            x = jnp.concatenate(
                [(kvc_ref[kind, pl.ds(r, nc, stride=CMP_STRIDE), :]
                  + pos_ref[kind, part, :, r * 128:(r + 1) * 128]).astype(BF16) for r in range(CMP_STRIDE)],
                axis=1)
            halves.append(x)
        for g in range(NSA_GROUPS):
            second = _dot(halves[1], w1_ref[kind, g, 1])
            hid = _silu(_dot(halves[0], w1_ref[kind, g, 0]) + pltpu.roll(second, nc - 1, 0)).astype(BF16)
            if kind == 0:
                kc_ref[g] = _rms_rows(_dot(hid, w2k_ref[...]), kg_ref[...]).astype(BF16)
            else:
                vct_ref[g] = _dot_nt(w2vt_ref[...], hid).astype(BF16)


def _compress(kvc, pos, w1, w2k, w2vt, k_gain):
    b, _, s, _ = kvc.shape
    nc = s // CMP_STRIDE
    return pl.pallas_call(
        _cmp_body, grid=(b,),
        in_specs=[pl.BlockSpec((None, 2, s, 128), lambda bi: (bi, 0, 0, 0)),
                  _resident(pos.shape), _resident(w1.shape), _resident(w2k.shape), _resident(w2vt.shape),
                  _resident((1, 64))],
        out_specs=(pl.BlockSpec((None, 2, nc, 64), lambda bi: (bi, 0, 0, 0)),
                   pl.BlockSpec((None, 2, 64, nc), lambda bi: (bi, 0, 0, 0))),
        out_shape=(jax.ShapeDtypeStruct((b, 2, nc, 64), BF16), jax.ShapeDtypeStruct((b, 2, 64, nc), BF16)),
        compiler_params=_cparams(("parallel",)), name="nsa_compress",
    )(kvc, pos, w1, w2k, w2vt, k_gain.reshape(1, 64))


def _compress_weights(pos_k, pos_v, w1_k, w1_v):
    def pos_part(p):
        p = p.reshape(2, CMP_STRIDE, 1, 64)
        return jnp.broadcast_to(p, (2, CMP_STRIDE, NSA_GROUPS, 64)).reshape(2, 1, CMP_STRIDE * 128)

    def w1_part(w):
        hdim = w.shape[1]
        w = w.reshape(2, CMP_STRIDE, 1, 64, hdim)
        per_group = []
        for g in range(NSA_GROUPS):
            pads = [w if gg == g else jnp.zeros_like(w) for gg in range(NSA_GROUPS)]
            per_group.append(jnp.concatenate(pads, axis=2).reshape(2, CMP_STRIDE * 128, hdim))
        return jnp.stack(per_group)

    pos = jnp.stack([pos_part(pos_k), pos_part(pos_v)])
    w1 = jnp.stack([w1_part(w1_k), w1_part(w1_v)]).astype(BF16)
    return pos, w1


CMP_CLASS_ROWS = 128


def _cmpsel_variant(nc, nblk, qt_ref, kc_ref, vct_ref, oc_ref, bias_ref, s_scr):
    tq = qt_ref.shape[1]
    t0 = pl.program_id(2) * tq
    n_idx = lax.broadcasted_iota(jnp.int32, (nc, tq), 0)
    t_idx = t0 + lax.broadcasted_iota(jnp.int32, (nc, tq), 1)
    mask_bias = jnp.where(n_idx * CMP_STRIDE + (CMP_BLOCK - 1) <= t_idx, 0.0, NEG)
    t_row = t0 + lax.broadcasted_iota(jnp.int32, (1, tq), 1)
    kc = kc_ref[0:nc, :]
    for hh in range(NSA_HPG):
        s_scr[hh, 0:nc, :] = _dot(kc, qt_ref[hh * 64:(hh + 1) * 64, :]) + mask_bias
    jj = lax.broadcasted_iota(jnp.int32, (nblk, nc), 0)
    nn = lax.broadcasted_iota(jnp.int32, (nblk, nc), 1)
    ov = jnp.where((nn * CMP_STRIDE < jj * SLC_BLOCK + SLC_BLOCK)
                   & (nn * CMP_STRIDE + CMP_BLOCK > jj * SLC_BLOCK), 1.0, 0.0).astype(BF16)
    ones_rows = (lax.broadcasted_iota(jnp.int32, (V_ROWS - 64, nc), 0) == 0).astype(BF16)
    lhs = jnp.concatenate([vct_ref[:, 0:nc], ones_rows, ov], axis=0)
    imp = jnp.zeros((nblk, tq), F32)
    for hh in range(NSA_HPG):
        m = jnp.max(s_scr[hh, 0:nc, :], axis=0, keepdims=True)
        p = jnp.exp2(s_scr[hh, 0:nc, :] - m).astype(BF16)
        r = _dot(lhs, p)
        inv_l = jnp.where(t_row >= CMP_BLOCK - 1, 1.0 / r[64:65], 0.0)
        oc_ref[hh * 64:(hh + 1) * 64, :] = r[0:64] * inv_l
        imp = imp + r[V_ROWS:] * inv_l

    j = lax.broadcasted_iota(jnp.int32, (nblk, tq), 0)
    cur = (t0 + lax.broadcasted_iota(jnp.int32, (nblk, tq), 1)) >> SLC_SHIFT
    forced = (j == 0) | (j == cur) | (j == cur - 1)
    bias = jnp.where(forced & (j <= cur), 0.0, NEG)
    imp = jnp.where((j <= cur) & jnp.logical_not(forced), imp, -1.0)
    jf = j.astype(F32)
    for _ in range(SLC_TOPK - 3):
        v = jnp.max(imp, axis=0, keepdims=True)
        first = jnp.min(jnp.where(imp == v, jf, float(nblk)), axis=0, keepdims=True)
        pick = jf == first
        bias = jnp.where(pick & (v >= 0.0), 0.0, bias)
        imp = jnp.where(pick, -3e38, imp)
    bias_ref[0:nblk, :] = bias
    if nblk < bias_ref.shape[0]:
        bias_ref[nblk:, :] = jnp.full((bias_ref.shape[0] - nblk, tq), NEG, F32)


def _cmpsel_body(qt_ref, kc_ref, vct_ref, oc_ref, bias_ref, s_scr):
    tq = qt_ref.shape[1]
    nc_total = kc_ref.shape[0]
    tiles_per_class = CMP_CLASS_ROWS // (tq // CMP_STRIDE)
    cls = pl.program_id(2) // tiles_per_class
    for c in range(nc_total // CMP_CLASS_ROWS):
        nc = (c + 1) * CMP_CLASS_ROWS
        nblk = min(nc * CMP_STRIDE // SLC_BLOCK, bias_ref.shape[0])
        pl.when(cls == c)(functools.partial(_cmpsel_variant, nc, nblk, qt_ref, kc_ref, vct_ref,
                                            oc_ref, bias_ref, s_scr))


def _cmp_select(qt, kc, vct, *, tq=TOK_TILE):
    b, _, s = qt.shape
    nc = kc.shape[2]
    nblk = MAX_BLOCKS
    assert s // SLC_BLOCK <= MAX_BLOCKS and s // SLC_BLOCK >= SLC_TOPK and nc % CMP_CLASS_ROWS == 0
    return pl.pallas_call(
        _cmpsel_body, grid=(b, NSA_GROUPS, s // tq),
        in_specs=[pl.BlockSpec((None, 256, tq), lambda bi, g, i: (bi, g, i)),
                  pl.BlockSpec((None, None, nc, 64), lambda bi, g, i: (bi, g, 0, 0)),
                  pl.BlockSpec((None, None, 64, nc), lambda bi, g, i: (bi, g, 0, 0))],
        out_specs=(pl.BlockSpec((None, 256, tq), lambda bi, g, i: (bi, g, i)),
                   pl.BlockSpec((None, None, nblk, tq), lambda bi, g, i: (bi, g, 0, i))),
        out_shape=(jax.ShapeDtypeStruct((b, 512, s), F32),
                   jax.ShapeDtypeStruct((b, NSA_GROUPS, nblk, s), F32)),
        scratch_shapes=[pltpu.VMEM((NSA_HPG, nc, tq), F32)],
        compiler_params=_cparams(("parallel", "parallel", "parallel")), name="nsa_compressed_select",
    )(qt, kc, vct)


def _flash_step(s_ref, vt, m_ref, acc_ref):
    m_old = m_ref[...]
    m_new = jnp.maximum(m_old, jnp.max(s_ref[...], axis=0, keepdims=True))
    p = jnp.exp2(s_ref[...] - m_new)
    acc_ref[...] = jnp.exp2(m_old - m_new) * acc_ref[...] + _dot(vt, p.astype(BF16))
    m_ref[...] = m_new


SEL, WIN = 0, 1
BIAS_ROWS = 16


def _slcwin_body(qt_ref, bias_ref, kaug_ref, vt_ref, oc_ref, gt_ref, mb_ref, y_ref,
                 q_scr, m_scr, acc_scr, s_scr):
    tq = qt_ref.shape[1]
    tk = vt_ref.shape[3]
    assert tq == tk and WINDOW <= tk and tk == 8 * SLC_BLOCK
    diag = pl.program_id(2)

    qa, qb, qw = 0, 1, 2
    zeros = jnp.zeros((64, tq), BF16)
    for slot in (qa, qb, qw):
        for hh in range(NSA_HPG):
            q_scr[slot, hh, 0:64, :] = qt_ref[hh * 64:(hh + 1) * 64, :]
            q_scr[slot, hh, 64:128, :] = zeros
    m_scr[...] = jnp.full(m_scr.shape, NEG, F32)
    acc_scr[...] = jnp.zeros(acc_scr.shape, F32)

    def set_selection_bias(kt, slot):
        rows = bias_ref[pl.ds(pl.multiple_of(kt * 8, 8), 8), :]
        b16 = jnp.concatenate([rows, jnp.zeros_like(rows)], axis=0).astype(BF16)
        for hh in range(NSA_HPG):
            q_scr[slot, hh, 64:64 + BIAS_ROWS, :] = b16

    def tile_step(br, qslot, sslot, kt, mask_bias):
        k = kaug_ref[br, pl.ds(pl.multiple_of(kt * tk, tk), tk), :]
        v = vt_ref[br, kt]
        for hh in range(NSA_HPG):
            s = _dot(k, q_scr[qslot, hh])
            s_scr[sslot, hh] = s if mask_bias is None else s + mask_bias
        for hh in range(NSA_HPG):
            _flash_step(s_scr.at[sslot, hh], v, m_scr.at[br, hh], acc_scr.at[br, hh])

    def tile_pair(j, carry):
        kt = 2 * j
        set_selection_bias(kt, qa)
        set_selection_bias(kt + 1, qb)
        tile_step(SEL, qa, 0, kt, None)
        tile_step(SEL, qb, 1, kt + 1, None)
        return carry

    lax.fori_loop(0, diag >> 1, tile_pair, 0)

    @pl.when((diag & 1) == 1)
    def _():
        set_selection_bias(diag - 1, qa)
        tile_step(SEL, qa, 0, diag - 1, None)

    set_selection_bias(diag, qa)

    def masked_tile(j, carry):
        first = j == 0
        sel = j == 1
        kt = jnp.where(first, jnp.maximum(diag - 1, 0), diag)
        extra = jnp.where(first & (diag == 0), NEG, 0.0)
        tile_step(jnp.where(sel, SEL, WIN), jnp.where(sel, qa, qw), 0, kt, mb_ref[jnp.where(first, 1, 0)] + extra)
        return carry

    lax.fori_loop(0, 3, masked_tile, 0)

    gt = gt_ref[...]
    for hh in range(NSA_HPG):
        o_s = acc_scr[SEL, hh, 0:64, :] * (1.0 / acc_scr[SEL, hh, 64:65, :])
        o_w = acc_scr[WIN, hh, 0:64, :] * (1.0 / acc_scr[WIN, hh, 64:65, :])
        y_ref[hh * 64:(hh + 1) * 64, :] = (gt[3 * hh:3 * hh + 1] * oc_ref[hh * 64:(hh + 1) * 64, :]
                                          + gt[3 * hh + 1:3 * hh + 2] * o_s
                                          + gt[3 * hh + 2:3 * hh + 3] * o_w)


def _slc_win(qt, bias, kaug, vt, oc, gt):
    b, _, s = qt.shape
    nblk = bias.shape[2]
    ns, tk = vt.shape[3], vt.shape[5]
    tq = tk
    key_rel, t_rel = np.arange(tk)[:, None], np.arange(tq)[None, :]
    mask_bias = jnp.asarray(np.stack([np.where(key_rel <= t_rel, 0.0, NEG),
                                      np.where(t_rel + tk - key_rel < WINDOW, 0.0, NEG)]), F32)
    qblk = pl.BlockSpec((None, 256, tq), lambda bi, g, i: (bi, g, i))
    return pl.pallas_call(
        _slcwin_body, grid=(b, NSA_GROUPS, s // tq),
        in_specs=[qblk,
                  pl.BlockSpec((None, None, nblk, tq), lambda bi, g, i: (bi, g, 0, i)),
                  pl.BlockSpec((None, None, 2, s, 128), lambda bi, g, i: (bi, g, 0, 0, 0)),
                  pl.BlockSpec((None, None, 2, ns, V_ROWS, tk), lambda bi, g, i: (bi, g, 0, 0, 0, 0)),
                  qblk,
                  pl.BlockSpec((None, None, GATE_ROWS, tq), lambda bi, g, i: (bi, g, 0, i)),
                  pl.BlockSpec((2, tk, tq), lambda bi, g, i: (0, 0, 0))],
        out_specs=qblk,
        out_shape=jax.ShapeDtypeStruct((b, 512, s), F32),
        scratch_shapes=[pltpu.VMEM((3, NSA_HPG, 128, tq), BF16),
                        pltpu.VMEM((2, NSA_HPG, 1, tq), F32), pltpu.VMEM((2, NSA_HPG, V_ROWS, tq), F32),
                        pltpu.VMEM((2, NSA_HPG, tk, tq), F32)],
        compiler_params=_cparams(("parallel", "parallel", "arbitrary")), name="nsa_selected_window",
    )(qt, bias, kaug, vt, oc, gt.reshape(b, NSA_GROUPS, GATE_ROWS, s), mask_bias)


def _hgrn_consts():
    c = HGRN_CHUNK
    t = np.arange(c)
    lower = (t[None, :] <= t[:, None]).astype(np.float32)
    rows = [lower]
    masks = []
    for half in HGRN_LEVELS:
        mid = (t // (2 * half)) * (2 * half) + half - 1
        if half < 8:
            rows.append(lower[mid])
        same = (t[:, None] // (2 * half)) == (t[None, :] // (2 * half))
        right = (t[:, None] & half) != 0
        left = (t[None, :] & half) == 0
        masks.append((same & right & left).astype(np.float32))
    masks.append(np.eye(c, dtype=np.float32))
    mall = np.concatenate(rows, axis=0)
    lvl = np.stack([np.tile(mk.T, (1, HGRN_HEADS)) for mk in masks])
    bdm = np.kron(np.eye(HGRN_HEADS), np.ones((64, 64), np.float32))
    return jnp.asarray(mall, BF16), jnp.asarray(lvl, F32), jnp.asarray(bdm, F32), jnp.asarray(bdm / 64, BF16)


def _hgrn_body(hg_ref, lb_ref, og_ref, mall_ref, lvl_ref, bdm_ref, bdn_ref, y_ref, st_scr):
    c = HGRN_CHUNK
    w = HGRN_WIDTH

    @pl.when(pl.program_id(1) == 0)
    def _():
        st_scr[...] = jnp.zeros(st_scr.shape, F32)

    chunks = range(hg_ref.shape[0] // c)
    lb = lb_ref[...]
    lane = lax.broadcasted_iota(jnp.int32, (c, w), 1)
    head_masks = [(lane >> 6) == hh for hh in range(HGRN_HEADS)]
    nlev = len(HGRN_LEVELS)

    def stack_heads(x):
        x16 = x.astype(BF16)
        return jnp.concatenate([jnp.where(hm, x16, 0) for hm in head_masks], axis=0)

    def row_bcast(x, half):
        return jnp.concatenate([jnp.broadcast_to(x[p + half - 1:p + half, :], (2 * half, w))
                                for p in range(0, c, 2 * half)], axis=0)

    qa, kk, v16, logf = [], [], [], []
    for ci in chunks:
        rows = slice(ci * c, (ci + 1) * c)
        qa.append(_silu(hg_ref[rows, 0:w]) * (HEAD_DIM ** -0.5))
        fg = lb + (1.0 - lb) * _sigmoid(hg_ref[rows, w:2 * w])
        kk.append(1.0 - fg)
        logf.append(jnp.log(fg))
        v16.append(hg_ref[rows, 2 * w:3 * w].astype(BF16))

    mall = mall_ref[...]
    r_all = sum(_dot(mall, part) for part in _split3(jnp.concatenate(logf, axis=1)))
    bcum = [r_all[0:c, ci * w:(ci + 1) * w] for ci in chunks]

    attn = [lvl_ref[nlev] * _dot_nt(kk[ci].astype(BF16), stack_heads(qa[ci])) for ci in chunks]
    fine = 0
    for li, half in enumerate(HGRN_LEVELS):
        for ci in chunks:
            if half >= 8:
                ref_pt = row_bcast(bcum[ci], half)
            else:
                ref_pt = r_all[(fine + 1) * c:(fine + 2) * c, ci * w:(ci + 1) * w]
            e = jnp.exp(-jnp.abs(bcum[ci] - ref_pt))
            kt = (kk[ci] * e).astype(BF16)
            attn[ci] = attn[ci] + lvl_ref[li] * _dot_nt(kt, stack_heads(qa[ci] * e))
        if half < 8:
            fine += 1

    intra, upd, decay, qb = [], [], [], []
    for ci in chunks:
        x = _dot_tn(attn[ci].astype(BF16), v16[ci])
        intra.append(sum(jnp.where(head_masks[hh], x[hh * c:(hh + 1) * c], 0.0) for hh in range(HGRN_HEADS)))
        b_last = bcum[ci][c - 1:c, :]
        kl = (kk[ci] * jnp.exp(b_last - bcum[ci])).astype(BF16)
        upd.append(bdm_ref[...] * _dot_tn(v16[ci], kl))
        decay.append(jnp.exp(b_last))
        qb.append((qa[ci] * jnp.exp(bcum[ci])).astype(BF16))

    st = st_scr[...]
    inter = []
    for ci in chunks:
        inter.append(_dot_nt(qb[ci], st.astype(BF16)))
        st = st * decay[ci] + upd[ci]
    st_scr[...] = st

    for ci in chunks:
        rows = slice(ci * c, (ci + 1) * c)
        o = inter[ci] + intra[ci]
        hi, lo = _split2(o * o)
        ms = _dot(hi, bdn_ref[...]) + _dot(lo, bdn_ref[...])
        y_ref[rows, :] = o * lax.rsqrt(ms + EPS) * og_ref[...] * _silu(hg_ref[rows, 3 * w:4 * w])


def _hgrn(hg, lower_bound, out_gain):
    b, s, _ = hg.shape
    rows = HGRN_CB * HGRN_CHUNK
    mall, lvl, bdm, bdn = _hgrn_consts()
    full = lambda shape: pl.BlockSpec(shape, lambda bi, i: (0,) * len(shape))
    return pl.pallas_call(
        _hgrn_body, grid=(b, s // rows),
        in_specs=[pl.BlockSpec((None, rows, 4 * HGRN_WIDTH), lambda bi, i: (bi, i, 0)),
                  full((1, HGRN_WIDTH)), full((1, HGRN_WIDTH)),
                  full(mall.shape), full(lvl.shape), full(bdm.shape), full(bdn.shape)],
        out_specs=pl.BlockSpec((None, rows, HGRN_WIDTH), lambda bi, i: (bi, i, 0)),
        out_shape=jax.ShapeDtypeStruct((b, s, HGRN_WIDTH), F32),
        scratch_shapes=[pltpu.VMEM((HGRN_WIDTH, HGRN_WIDTH), F32)],
        compiler_params=_cparams(("parallel", "arbitrary")), name="hgrn2_chunks",
    )(hg, lower_bound.reshape(1, -1), out_gain.reshape(1, -1), mall, lvl, bdm, bdn)


def _memkv_body(mem_ref, mg_ref, wk_ref, wvt_ref, kg_ref, kh_ref, vht_ref):
    m = mem_ref.shape[0]
    mn = _rms_rows(mem_ref[...], mg_ref[...]).astype(BF16)
    k = _dot(mn, wk_ref[...])
    vt = _dot_nt(wvt_ref[...], mn)
    ones_rows = (lax.broadcasted_iota(jnp.int32, (V_ROWS - 64, m), 0) == 0).astype(BF16)
    for hh in range(MEM_HEADS):
        kh_ref[hh] = _rms_rows(k[:, hh * 64:(hh + 1) * 64], kg_ref[...]).astype(BF16)
        vht_ref[hh, 0:64, :] = vt[hh * 64:(hh + 1) * 64].astype(BF16)
        vht_ref[hh, 64:V_ROWS, :] = ones_rows


def _mem_kv(mem, mem_gain, wk, wvt, k_gain):
    b, m, d = mem.shape
    full = lambda shape: pl.BlockSpec(shape, lambda bi: (0,) * len(shape))
    return pl.pallas_call(
        _memkv_body, grid=(b,),
        in_specs=[pl.BlockSpec((None, m, d), lambda bi: (bi, 0, 0)), full((1, d)),
                  full(wk.shape), full(wvt.shape), full((1, 64))],
        out_specs=(pl.BlockSpec((None, MEM_HEADS, m, 64), lambda bi: (bi, 0, 0, 0)),
                   pl.BlockSpec((None, MEM_HEADS, V_ROWS, m), lambda bi: (bi, 0, 0, 0))),
        out_shape=(jax.ShapeDtypeStruct((b, MEM_HEADS, m, 64), BF16),
                   jax.ShapeDtypeStruct((b, MEM_HEADS, V_ROWS, m), BF16)),
        compiler_params=_cparams(("parallel",)), name="memory_kv",
    )(mem, mem_gain.reshape(1, d), wk, wvt, k_gain.reshape(1, 64))


def _out_body(x_ref, ynt_ref, yh_ref, qmt_ref, kh_ref, vht_ref, ng_ref, mg_ref, wo_ref,
              fg_ref, wg_ref, wu_ref, wd_ref, o_ref, a_scr):
    y_mem = []
    for hh in range(MEM_HEADS):
        s = _dot(kh_ref[hh], qmt_ref[hh * 64:(hh + 1) * 64, :])
        p = jnp.exp2(s - jnp.max(s, axis=0, keepdims=True))
        o = _dot(vht_ref[hh], p.astype(BF16))
        y_mem.append(o[0:64] * (1.0 / o[64:65]))
    mem = _rms_cols(jnp.concatenate(y_mem, axis=0), mg_ref[...]).astype(BF16)
    nsa = _rms_cols(ynt_ref[...], ng_ref[...]).astype(BF16)
    acc = _dot_tn(nsa, wo_ref[0:NSA_WIDTH, :])
    acc = acc + _dot(yh_ref[...].astype(BF16), wo_ref[NSA_WIDTH:NSA_WIDTH + HGRN_WIDTH, :])
    acc = acc + _dot_tn(mem, wo_ref[NSA_WIDTH + HGRN_WIDTH:, :])
    o_ref[...] = _ffn_half_step(x_ref[...] + acc, fg_ref, wg_ref, wu_ref, wd_ref, a_scr)


def _out_ffn(x3d, ynt, yh, qmt, kh, vht, nsa_gain, mem_gain, wo, ffn_gain, wg, wu, wd, *, tm=TOK_TILE):
    b, s, d = x3d.shape
    m = kh.shape[2]
    return pl.pallas_call(
        _out_body, grid=(b, s // tm),
        in_specs=[pl.BlockSpec((None, tm, d), lambda bi, i: (bi, i, 0)),
                  pl.BlockSpec((None, NSA_WIDTH, tm), lambda bi, i: (bi, 0, i)),
                  pl.BlockSpec((None, tm, HGRN_WIDTH), lambda bi, i: (bi, i, 0)),
                  pl.BlockSpec((None, MEM_WIDTH, tm), lambda bi, i: (bi, 0, i)),
                  pl.BlockSpec((None, MEM_HEADS, m, 64), lambda bi, i: (bi, 0, 0, 0)),
                  pl.BlockSpec((None, MEM_HEADS, V_ROWS, m), lambda bi, i: (bi, 0, 0, 0)),
                  _resident((NSA_WIDTH, 1)), _resident((MEM_WIDTH, 1)), _resident(wo.shape),
                  _resident((1, d)), _resident(wg.shape), _resident(wu.shape), _resident(wd.shape)],
        out_specs=pl.BlockSpec((None, tm, d), lambda bi, i: (bi, i, 0)),
        out_shape=jax.ShapeDtypeStruct((b, s, d), F32),
        scratch_shapes=[pltpu.VMEM((tm, wg.shape[1]), BF16)],
        compiler_params=_cparams(("parallel", "parallel")), name="mix_out_ffn2",
    )(x3d, ynt, yh, qmt, kh, vht, nsa_gain.reshape(-1, 1), mem_gain.reshape(-1, 1), wo,
      ffn_gain.reshape(1, d), wg, wu, wd)


def _layer(x, mem, ffn1, ffn2, mix_norm, w_in, w_out, nsa_q_norm, nsa_k_norm, cmp_pos_k, cmp_w1_k, cmp_w2_k,
           cmp_pos_v, cmp_w1_v, cmp_w2_v, nsa_out_norm, lower_bound, hgrn_out_norm,
           mem_norm, mem_w_k, mem_w_v, mem_q_norm, mem_k_norm, mem_out_norm):
    b, s, d = x.shape
    sizes = (512, 128, 128, 128, 128, 128, 128, 24, 256, 256, 256, 256, 256)
    offs = np.concatenate([[0], np.cumsum(sizes)])
    col = lambda i: w_in[:, offs[i]:offs[i + 1]]
    (q_a, k_c, v_c, k_s, v_s, k_w, v_w, g_a, q_h, f_h, i_h, g_h, q_m) = [col(i) for i in range(13)]
    gpad = jnp.zeros((d, GATE_ROWS - 3 * NSA_HPG), w_in.dtype)
    wt = jnp.concatenate([q_a, v_s, v_w, g_a[:, :3 * NSA_HPG], gpad, g_a[:, 3 * NSA_HPG:], gpad, q_m],
                         axis=1).T.astype(BF16)
    wn = jnp.concatenate([k_c, v_c, k_s, k_w, q_h, f_h, i_h, g_h], axis=1).astype(BF16)

    x1, qt, vt, gt, qmt, kaug, kvc, hg = _ffn_proj(
        x, *ffn1, mix_norm, wt, wn, nsa_q_norm, nsa_k_norm, mem_q_norm, _rope_tables(s))

    cmp_pos, cmp_w1 = _compress_weights(cmp_pos_k, cmp_pos_v, cmp_w1_k, cmp_w1_v)
    kc, vct = _compress(kvc, cmp_pos, cmp_w1, cmp_w2_k.astype(BF16), cmp_w2_v.T.astype(BF16), nsa_k_norm)
    oc, bias = _cmp_select(qt, kc, vct)
    y_nsa = _slc_win(qt, bias, kaug, vt, oc, gt)

    y_hgrn = _hgrn(hg, lower_bound, hgrn_out_norm)

    kh, vht = _mem_kv(mem, mem_norm, mem_w_k.astype(BF16), mem_w_v.T.astype(BF16), mem_k_norm)
    return _out_ffn(x1, y_nsa, y_hgrn, qmt, kh, vht, nsa_out_norm, mem_out_norm, w_out.astype(BF16), *ffn2)


def kernel(x, mem, ffn1_norm, ffn1_w_gate, ffn1_w_up, ffn1_w_down, mix_norm, w_in, w_out, nsa_q_norm, nsa_k_norm, cmp_pos_k, cmp_w1_k, cmp_w2_k, cmp_pos_v, cmp_w1_v, cmp_w2_v, nsa_out_norm, hgrn_lb_logits, hgrn_out_norm, mem_norm, mem_w_k, mem_w_v, mem_q_norm, mem_k_norm, mem_out_norm, ffn2_norm, ffn2_w_gate, ffn2_w_up, ffn2_w_down):
    b, s, d = x.shape
    depth = ffn1_norm.shape[0]
    lower_bounds = jnp.cumsum(jax.nn.softmax(hgrn_lb_logits.astype(F32), axis=0), axis=0)
    bf = lambda a: a.astype(BF16)
    for l in range(depth):
        x = _layer(x, mem, (ffn1_norm[l], bf(ffn1_w_gate[l]), bf(ffn1_w_up[l]), bf(ffn1_w_down[l])),
                   (ffn2_norm[l], bf(ffn2_w_gate[l]), bf(ffn2_w_up[l]), bf(ffn2_w_down[l])),
                   mix_norm[l], w_in[l], w_out[l], nsa_q_norm[l], nsa_k_norm[l],
                   cmp_pos_k[l], cmp_w1_k[l], cmp_w2_k[l], cmp_pos_v[l], cmp_w1_v[l], cmp_w2_v[l],
                   nsa_out_norm[l], lower_bounds[l], hgrn_out_norm[l],
                   mem_norm[l], mem_w_k[l], mem_w_v[l], mem_q_norm[l], mem_k_norm[l], mem_out_norm[l])
    return x
```

```python
import functools

import numpy as np
import jax
import jax.numpy as jnp
from jax import lax
from jax.experimental import pallas as pl
from jax.experimental.pallas import tpu as pltpu

F32 = jnp.float32
BF16 = jnp.bfloat16

HEAD_DIM = 64
ROT_DIM = 16
ROT_HALF = 8
ROPE_THETA = 500000.0
NSA_HEADS = 8
NSA_GROUPS = 2
NSA_HPG = 4
CMP_BLOCK = 32
CMP_STRIDE = 16
SLC_BLOCK = 64
SLC_SHIFT = 6
SLC_TOPK = 16
WINDOW = 512
FORCED_SCORE = 1e4
HGRN_HEADS = 4
HGRN_CHUNK = 64
HGRN_WIDTH = 256
MEM_HEADS = 4
MEM_WIDTH = 256
NSA_WIDTH = 512
EPS = 1e-6
NEG = -1e30
QK_SCALE_LOG2 = HEAD_DIM ** -0.5 * 1.4426950408889634

VMEM_LIMIT = 56 * 1024 * 1024
MAX_BLOCKS = 128
GATE_ROWS = 16
V_ROWS = 80
TOK_TILE = 512
HGRN_CB = 8
HGRN_LEVELS = (32, 16, 8, 4, 2, 1)

NT_DIMS = (((1,), (1,)), ((), ()))
TN_DIMS = (((0,), (0,)), ((), ()))


def _cparams(sem):
    return pltpu.CompilerParams(dimension_semantics=sem, vmem_limit_bytes=VMEM_LIMIT)


def _dot(a, b):
    return jnp.dot(a, b, preferred_element_type=F32)


def _dot_nt(a, b):
    return lax.dot_general(a, b, NT_DIMS, preferred_element_type=F32)


def _dot_tn(a, b):
    return lax.dot_general(a, b, TN_DIMS, preferred_element_type=F32)


def _sigmoid(x):
    return 1.0 / (1.0 + jnp.exp(-x))


def _silu(x):
    return x * _sigmoid(x)


def _split2(x):
    hi = x.astype(BF16)
    lo = (x - hi.astype(F32)).astype(BF16)
    return hi, lo


def _split3(x):
    hi = x.astype(BF16)
    r1 = x - hi.astype(F32)
    mid = r1.astype(BF16)
    lo = (r1 - mid.astype(F32)).astype(BF16)
    return hi, mid, lo


def _rms_rows(x, gain_row):
    ms = jnp.mean(x * x, axis=-1, keepdims=True)
    return x * lax.rsqrt(ms + EPS) * gain_row


def _rms_cols(x, gain_col):
    ms = jnp.mean(x * x, axis=0, keepdims=True)
    return x * lax.rsqrt(ms + EPS) * gain_col


def _seg_mean_sq(x, bd):
    hi, lo = _split2(x * x)
    return _dot(hi, bd) + _dot(lo, bd)


FFN_CHUNK = 256


def _ffn_half_step(x, g_ref, wg_ref, wu_ref, wd_ref, a_scr):
    xn = _rms_rows(x, g_ref[...]).astype(BF16)
    d_ff = wg_ref.shape[1]
    for c in range(d_ff // FFN_CHUNK):
        sl = slice(c * FFN_CHUNK, (c + 1) * FFN_CHUNK)
        g = _dot(xn, wg_ref[:, sl])
        u = _dot(xn, wu_ref[:, sl])
        a_scr[:, sl] = (_silu(g) * u).astype(BF16)
    return x + 0.5 * _dot(a_scr[...], wd_ref[...])


def _resident(shape):
    return pl.BlockSpec(shape, lambda *_: (0,) * len(shape), pipeline_mode=pl.Buffered(1))


def _rope_cols(xn, cos, sin):
    x0, x1 = xn[0:ROT_HALF], xn[ROT_HALF:ROT_DIM]
    return jnp.concatenate([x0 * cos - x1 * sin, x1 * cos + x0 * sin, xn[ROT_DIM:]], axis=0)


def _rope_rows(x, cn, sa, sb):
    return x * cn + pltpu.roll(x, 128 - ROT_HALF, 1) * sa + pltpu.roll(x, ROT_HALF, 1) * sb


def _proj_body(x_ref, fg_ref, wg_ref, wu_ref, wd_ref, mg_ref, wt_ref, wn_ref, qg_ref, kg_ref, mqg_ref,
               cos_ref, sin_ref, cn_ref, sa_ref, sb_ref, bd_ref,
               x1_ref, qt_ref, vt_ref, gt_ref, qmt_ref, kaug_ref, kvc_ref, hg_ref, a_scr):
    tm = x_ref.shape[0]
    assert tm == 8 * SLC_BLOCK
    x1 = _ffn_half_step(x_ref[...], fg_ref, wg_ref, wu_ref, wd_ref, a_scr)
    x1_ref[...] = x1
    h = _rms_rows(x1, mg_ref[...]).astype(BF16)

    pt = _dot_nt(wt_ref[...], h)
    cos, sin = cos_ref[...], sin_ref[...]
    qg = qg_ref[...]
    for hh in range(NSA_HEADS):
        xq = _rms_cols(pt[hh * 64:(hh + 1) * 64], qg)
        qt_ref[hh * 64:(hh + 1) * 64, :] = (_rope_cols(xq, cos, sin) * QK_SCALE_LOG2).astype(BF16)
    ones_rows = (lax.broadcasted_iota(jnp.int32, (V_ROWS - 64, tm), 0) == 0).astype(BF16)
    for g in range(NSA_GROUPS):
        for br in range(2):
            rows = 512 + br * 128 + g * 64
            vt_ref[g, br, 0:64, :] = pt[rows:rows + 64].astype(BF16)
            vt_ref[g, br, 64:V_ROWS, :] = ones_rows
    gt_ref[...] = _sigmoid(pt[768:800])
    mqg = mqg_ref[...]
    for hh in range(MEM_HEADS):
        xm = _rms_cols(pt[800 + hh * 64:864 + hh * 64], mqg)
        qmt_ref[hh * 64:(hh + 1) * 64, :] = (xm * QK_SCALE_LOG2).astype(BF16)

    pn = _dot(h, wn_ref[...])
    cn, sa, sb = cn_ref[...], sa_ref[...], sb_ref[...]
    bd, kg = bd_ref[...], kg_ref[...]
    kvc_ref[0] = _rope_rows(pn[:, 0:128], cn, sa, sb)
    kvc_ref[1] = pn[:, 128:256]
    ks = pn[:, 256:384]
    kw = pn[:, 384:512]
    ks = _rope_rows(ks * lax.rsqrt(_seg_mean_sq(ks, bd) + EPS) * kg, cn, sa, sb)
    kw = _rope_rows(kw * lax.rsqrt(_seg_mean_sq(kw, bd) + EPS) * kg, cn, sa, sb)
    lane = lax.broadcasted_iota(jnp.int32, (tm, 128), 1)
    row = lax.broadcasted_iota(jnp.int32, (tm, 128), 0)
    onehot = jnp.where(lane - 64 == (row >> SLC_SHIFT), 1.0, 0.0)
    lo_half = lane < 64
    kaug_ref[0, 0] = jnp.where(lo_half, ks, onehot).astype(BF16)
    kaug_ref[0, 1] = jnp.where(lo_half, kw, 0.0).astype(BF16)
    kaug_ref[1, 0] = jnp.where(lo_half, pltpu.roll(ks, 64, 1), onehot).astype(BF16)
    kaug_ref[1, 1] = jnp.where(lo_half, pltpu.roll(kw, 64, 1), 0.0).astype(BF16)
    hg_ref[...] = pn[:, 512:1536]


def _ffn_proj(x3d, ffn_gain, wg, wu, wd, mix_gain, wt, wn, q_gain, k_gain, mq_gain, rope):
    b, s, d = x3d.shape
    tm = TOK_TILE
    ns = s // tm
    cos_t, sin_t, cn, sa, sb = rope
    bd = jnp.asarray(np.kron(np.eye(2), np.full((64, 64), 1.0 / 64)), BF16)
    full = _resident
    out_shape = (
        jax.ShapeDtypeStruct((b, s, d), F32),
        jax.ShapeDtypeStruct((b, 512, s), BF16),
        jax.ShapeDtypeStruct((b, 2, 2, ns, V_ROWS, tm), BF16),
        jax.ShapeDtypeStruct((b, 32, s), F32),
        jax.ShapeDtypeStruct((b, 256, s), BF16),
        jax.ShapeDtypeStruct((b, 2, 2, s, 128), BF16),
        jax.ShapeDtypeStruct((b, 2, s, 128), F32),
        jax.ShapeDtypeStruct((b, s, 1024), F32),
    )
    out_specs = (
        pl.BlockSpec((None, tm, d), lambda bi, i: (bi, i, 0)),
        pl.BlockSpec((None, 512, tm), lambda bi, i: (bi, 0, i)),
        pl.BlockSpec((None, 2, 2, None, V_ROWS, tm), lambda bi, i: (bi, 0, 0, i, 0, 0)),
        pl.BlockSpec((None, 32, tm), lambda bi, i: (bi, 0, i)),
        pl.BlockSpec((None, 256, tm), lambda bi, i: (bi, 0, i)),
        pl.BlockSpec((None, 2, 2, tm, 128), lambda bi, i: (bi, 0, 0, i, 0)),
        pl.BlockSpec((None, 2, tm, 128), lambda bi, i: (bi, 0, i, 0)),
        pl.BlockSpec((None, tm, 1024), lambda bi, i: (bi, i, 0)),
    )
    in_specs = [
        pl.BlockSpec((None, tm, d), lambda bi, i: (bi, i, 0)),
        full((1, d)), full(wg.shape), full(wu.shape), full(wd.shape),
        full((1, d)), full(wt.shape), full(wn.shape),
        full((64, 1)), full((1, 128)), full((64, 1)),
        pl.BlockSpec((ROT_HALF, tm), lambda bi, i: (0, i)),
        pl.BlockSpec((ROT_HALF, tm), lambda bi, i: (0, i)),
        pl.BlockSpec((tm, 128), lambda bi, i: (i, 0)),
        pl.BlockSpec((tm, 128), lambda bi, i: (i, 0)),
        pl.BlockSpec((tm, 128), lambda bi, i: (i, 0)),
        full((128, 128)),
    ]
    return pl.pallas_call(
        _proj_body, grid=(b, ns), in_specs=in_specs, out_specs=out_specs, out_shape=out_shape,
        scratch_shapes=[pltpu.VMEM((tm, wg.shape[1]), BF16)],
        compiler_params=_cparams(("parallel", "parallel")), name="ffn1_mix_projection",
    )(x3d, ffn_gain.reshape(1, d), wg, wu, wd, mix_gain.reshape(1, d), wt, wn, q_gain.reshape(64, 1),
      jnp.tile(k_gain.reshape(1, 64), (1, 2)), mq_gain.reshape(64, 1), cos_t, sin_t, cn, sa, sb, bd)


def _rope_tables(s):
    pos = jnp.arange(s, dtype=F32)
    inv = ROPE_THETA ** (-(jnp.arange(0, ROT_DIM, 2, dtype=F32) / ROT_DIM))
    ang = pos[:, None] * inv[None, :]
    cos, sin = jnp.cos(ang), jnp.sin(ang)
    zeros = jnp.zeros((s, 64 - ROT_DIM), F32)
    cn = jnp.concatenate([cos, cos, jnp.ones((s, 64 - ROT_DIM), F32)], axis=1)
    sa = jnp.concatenate([-sin, jnp.zeros((s, ROT_HALF), F32), zeros], axis=1)
    sb = jnp.concatenate([jnp.zeros((s, ROT_HALF), F32), sin, zeros], axis=1)
    tile2 = lambda a: jnp.concatenate([a, a], axis=1)
    return cos.T, sin.T, tile2(cn), tile2(sa), tile2(sb)


def _cmp_body(kvc_ref, pos_ref, w1_ref, w2k_ref, w2vt_ref, kg_ref, kc_ref, vct_ref):
    nc = kvc_ref.shape[1] // CMP_STRIDE
    for kind in range(2):
        halves = []
        for part in range(2):
            x = jnp.concatenate(
                [(kvc_ref[kind, pl.ds(r, nc, stride=CMP_STRIDE), :]
                  + pos_ref[kind, part, :, r * 128:(r + 1) * 128]).astype(BF16) for r in range(CMP_STRIDE)],
                axis=1)
            halves.append(x)
        for g in range(NSA_GROUPS):
            second = _dot(halves[1], w1_ref[kind, g, 1])
            hid = _silu(_dot(halves[0], w1_ref[kind, g, 0]) + pltpu.roll(second, nc - 1, 0)).astype(BF16)
            if kind == 0:
                kc_ref[g] = _rms_rows(_dot(hid, w2k_ref[...]), kg_ref[...]).astype(BF16)
            else:
                vct_ref[g] = _dot_nt(w2vt_ref[...], hid).astype(BF16)


def _compress(kvc, pos, w1, w2k, w2vt, k_gain):
    b, _, s, _ = kvc.shape
    nc = s // CMP_STRIDE
    return pl.pallas_call(
        _cmp_body, grid=(b,),
        in_specs=[pl.BlockSpec((None, 2, s, 128), lambda bi: (bi, 0, 0, 0)),
                  _resident(pos.shape), _resident(w1.shape), _resident(w2k.shape), _resident(w2vt.shape),
                  _resident((1, 64))],
        out_specs=(pl.BlockSpec((None, 2, nc, 64), lambda bi: (bi, 0, 0, 0)),
                   pl.BlockSpec((None, 2, 64, nc), lambda bi: (bi, 0, 0, 0))),
        out_shape=(jax.ShapeDtypeStruct((b, 2, nc, 64), BF16), jax.ShapeDtypeStruct((b, 2, 64, nc), BF16)),
        compiler_params=_cparams(("parallel",)), name="nsa_compress",
    )(kvc, pos, w1, w2k, w2vt, k_gain.reshape(1, 64))


def _compress_weights(pos_k, pos_v, w1_k, w1_v):
    def pos_part(p):
        p = p.reshape(2, CMP_STRIDE, 1, 64)
        return jnp.broadcast_to(p, (2, CMP_STRIDE, NSA_GROUPS, 64)).reshape(2, 1, CMP_STRIDE * 128)

    def w1_part(w):
        hdim = w.shape[1]
        w = w.reshape(2, CMP_STRIDE, 1, 64, hdim)
        per_group = []
        for g in range(NSA_GROUPS):
            pads = [w if gg == g else jnp.zeros_like(w) for gg in range(NSA_GROUPS)]
            per_group.append(jnp.concatenate(pads, axis=2).reshape(2, CMP_STRIDE * 128, hdim))
        return jnp.stack(per_group)

    pos = jnp.stack([pos_part(pos_k), pos_part(pos_v)])
    w1 = jnp.stack([w1_part(w1_k), w1_part(w1_v)]).astype(BF16)
    return pos, w1


CMP_CLASS_ROWS = 128


def _cmpsel_variant(nc, nblk, qt_ref, kc_ref, vct_ref, oc_ref, bias_ref, s_scr):
    tq = qt_ref.shape[1]
    t0 = pl.program_id(2) * tq
    n_idx = lax.broadcasted_iota(jnp.int32, (nc, tq), 0)
    t_idx = t0 + lax.broadcasted_iota(jnp.int32, (nc, tq), 1)
    mask_bias = jnp.where(n_idx * CMP_STRIDE + (CMP_BLOCK - 1) <= t_idx, 0.0, NEG)
    t_row = t0 + lax.broadcasted_iota(jnp.int32, (1, tq), 1)
    kc = kc_ref[0:nc, :]
    for hh in range(NSA_HPG):
        s_scr[hh, 0:nc, :] = _dot(kc, qt_ref[hh * 64:(hh + 1) * 64, :]) + mask_bias
    jj = lax.broadcasted_iota(jnp.int32, (nblk, nc), 0)
    nn = lax.broadcasted_iota(jnp.int32, (nblk, nc), 1)
    ov = jnp.where((nn * CMP_STRIDE < jj * SLC_BLOCK + SLC_BLOCK)
                   & (nn * CMP_STRIDE + CMP_BLOCK > jj * SLC_BLOCK), 1.0, 0.0).astype(BF16)
    ones_rows = (lax.broadcasted_iota(jnp.int32, (V_ROWS - 64, nc), 0) == 0).astype(BF16)
    lhs = jnp.concatenate([vct_ref[:, 0:nc], ones_rows, ov], axis=0)
    imp = jnp.zeros((nblk, tq), F32)
    for hh in range(NSA_HPG):
        m = jnp.max(s_scr[hh, 0:nc, :], axis=0, keepdims=True)
        p = jnp.exp2(s_scr[hh, 0:nc, :] - m).astype(BF16)
        r = _dot(lhs, p)
        inv_l = jnp.where(t_row >= CMP_BLOCK - 1, 1.0 / r[64:65], 0.0)
        oc_ref[hh * 64:(hh + 1) * 64, :] = r[0:64] * inv_l
        imp = imp + r[V_ROWS:] * inv_l

    j = lax.broadcasted_iota(jnp.int32, (nblk, tq), 0)
    cur = (t0 + lax.broadcasted_iota(jnp.int32, (nblk, tq), 1)) >> SLC_SHIFT
    forced = (j == 0) | (j == cur) | (j == cur - 1)
    bias = jnp.where(forced & (j <= cur), 0.0, NEG)
    imp = jnp.where((j <= cur) & jnp.logical_not(forced), imp, -1.0)
    jf = j.astype(F32)
    for _ in range(SLC_TOPK - 3):
        v = jnp.max(imp, axis=0, keepdims=True)
        first = jnp.min(jnp.where(imp == v, jf, float(nblk)), axis=0, keepdims=True)
        pick = jf == first
        bias = jnp.where(pick & (v >= 0.0), 0.0, bias)
        imp = jnp.where(pick, -3e38, imp)
    bias_ref[0:nblk, :] = bias
    if nblk < bias_ref.shape[0]:
        bias_ref[nblk:, :] = jnp.full((bias_ref.shape[0] - nblk, tq), NEG, F32)


def _cmpsel_body(qt_ref, kc_ref, vct_ref, oc_ref, bias_ref, s_scr):
    tq = qt_ref.shape[1]
    nc_total = kc_ref.shape[0]
    tiles_per_class = CMP_CLASS_ROWS // (tq // CMP_STRIDE)
    cls = pl.program_id(2) // tiles_per_class
    for c in range(nc_total // CMP_CLASS_ROWS):
        nc = (c + 1) * CMP_CLASS_ROWS
        nblk = min(nc * CMP_STRIDE // SLC_BLOCK, bias_ref.shape[0])
        pl.when(cls == c)(functools.partial(_cmpsel_variant, nc, nblk, qt_ref, kc_ref, vct_ref,
                                            oc_ref, bias_ref, s_scr))


def _cmp_select(qt, kc, vct, *, tq=TOK_TILE):
    b, _, s = qt.shape
    nc = kc.shape[2]
    nblk = MAX_BLOCKS
    assert s // SLC_BLOCK <= MAX_BLOCKS and s // SLC_BLOCK >= SLC_TOPK and nc % CMP_CLASS_ROWS == 0
    return pl.pallas_call(
        _cmpsel_body, grid=(b, NSA_GROUPS, s // tq),
        in_specs=[pl.BlockSpec((None, 256, tq), lambda bi, g, i: (bi, g, i)),
                  pl.BlockSpec((None, None, nc, 64), lambda bi, g, i: (bi, g, 0, 0)),
                  pl.BlockSpec((None, None, 64, nc), lambda bi, g, i: (bi, g, 0, 0))],
        out_specs=(pl.BlockSpec((None, 256, tq), lambda bi, g, i: (bi, g, i)),
                   pl.BlockSpec((None, None, nblk, tq), lambda bi, g, i: (bi, g, 0, i))),
        out_shape=(jax.ShapeDtypeStruct((b, 512, s), F32),
                   jax.ShapeDtypeStruct((b, NSA_GROUPS, nblk, s), F32)),
        scratch_shapes=[pltpu.VMEM((NSA_HPG, nc, tq), F32)],
        compiler_params=_cparams(("parallel", "parallel", "parallel")), name="nsa_compressed_select",
    )(qt, kc, vct)


def _flash_step(s_ref, vt, m_ref, acc_ref):
    m_old = m_ref[...]
    m_new = jnp.maximum(m_old, jnp.max(s_ref[...], axis=0, keepdims=True))
    p = jnp.exp2(s_ref[...] - m_new)
    acc_ref[...] = jnp.exp2(m_old - m_new) * acc_ref[...] + _dot(vt, p.astype(BF16))
    m_ref[...] = m_new


SEL, WIN = 0, 1
BIAS_ROWS = 16


def _slcwin_body(qt_ref, bias_ref, kaug_ref, vt_ref, oc_ref, gt_ref, mb_ref, y_ref,
                 q_scr, m_scr, acc_scr, s_scr):
    tq = qt_ref.shape[1]
    tk = vt_ref.shape[3]
    assert tq == tk and WINDOW <= tk and tk == 8 * SLC_BLOCK
    diag = pl.program_id(2)

    qa, qb, qw = 0, 1, 2
    zeros = jnp.zeros((64, tq), BF16)
    for slot in (qa, qb, qw):
        for hh in range(NSA_HPG):
            q_scr[slot, hh, 0:64, :] = qt_ref[hh * 64:(hh + 1) * 64, :]
            q_scr[slot, hh, 64:128, :] = zeros
    m_scr[...] = jnp.full(m_scr.shape, NEG, F32)
    acc_scr[...] = jnp.zeros(acc_scr.shape, F32)

    def set_selection_bias(kt, slot):
        rows = bias_ref[pl.ds(pl.multiple_of(kt * 8, 8), 8), :]
        b16 = jnp.concatenate([rows, jnp.zeros_like(rows)], axis=0).astype(BF16)
        for hh in range(NSA_HPG):
            q_scr[slot, hh, 64:64 + BIAS_ROWS, :] = b16

    whole_tile = ((0, tk, 0, tq),)

    def tile_step(br, qslot, sslot, kt, mask_bias=None, blocks=whole_tile):
        for r0, nr, l0, nl in blocks:
            rows, lanes = slice(r0, r0 + nr), slice(l0, l0 + nl)
            k = kaug_ref[br, pl.ds(pl.multiple_of(kt * tk + r0, 128), nr), :]
            for hh in range(NSA_HPG):
                s = _dot(k, q_scr[qslot, hh, :, lanes])
                s_scr[sslot, hh, rows, lanes] = s if mask_bias is None else s + mask_bias(rows, lanes)
        for r0, nr, l0, nl in blocks:
            rows, lanes = slice(r0, r0 + nr), slice(l0, l0 + nl)
            v = vt_ref[br, kt, :, rows]
            for hh in range(NSA_HPG):
                _flash_step(s_scr.at[sslot, hh, rows, lanes], v,
                            m_scr.at[br, hh, :, lanes], acc_scr.at[br, hh, :, lanes])

    def tile_pair(j, carry):
        kt = 2 * j
        set_selection_bias(kt, qa)
        set_selection_bias(kt + 1, qb)
        tile_step(SEL, qa, 0, kt, None)
        tile_step(SEL, qb, 1, kt + 1, None)
        return carry

    lax.fori_loop(0, diag >> 1, tile_pair, 0)

    @pl.when((diag & 1) == 1)
    def _():
        set_selection_bias(diag - 1, qa)
        tile_step(SEL, qa, 0, diag - 1, None)

    set_selection_bias(diag, qa)

    assert WINDOW == tk
    half = tq // 2
    no_prev = jnp.where(diag == 0, NEG, 0.0)
    tile_step(WIN, qw, 0, jnp.maximum(diag - 1, 0), lambda rows, lanes: mb_ref[1, rows, lanes] + no_prev,
              ((0, tk, 0, half), (half, tk - half, half, tq - half)))

    def causal_tile(j, carry):
        sel = j == 0
        tile_step(jnp.where(sel, SEL, WIN), jnp.where(sel, qa, qw), 0, diag, lambda rows, lanes: mb_ref[0, rows, lanes],
                  ((0, half, 0, half), (0, tk, half, tq - half)))
        return carry

    lax.fori_loop(0, 2, causal_tile, 0)

    gt = gt_ref[...]
    for hh in range(NSA_HPG):
        o_s = acc_scr[SEL, hh, 0:64, :] * (1.0 / acc_scr[SEL, hh, 64:65, :])
        o_w = acc_scr[WIN, hh, 0:64, :] * (1.0 / acc_scr[WIN, hh, 64:65, :])
        y_ref[hh * 64:(hh + 1) * 64, :] = (gt[3 * hh:3 * hh + 1] * oc_ref[hh * 64:(hh + 1) * 64, :]
                                          + gt[3 * hh + 1:3 * hh + 2] * o_s
                                          + gt[3 * hh + 2:3 * hh + 3] * o_w)


def _slc_win(qt, bias, kaug, vt, oc, gt):
    b, _, s = qt.shape
    nblk = bias.shape[2]
    ns, tk = vt.shape[3], vt.shape[5]
    tq = tk
    key_rel, t_rel = np.arange(tk)[:, None], np.arange(tq)[None, :]
    mask_bias = jnp.asarray(np.stack([np.where(key_rel <= t_rel, 0.0, NEG),
                                      np.where(t_rel + tk - key_rel < WINDOW, 0.0, NEG)]), F32)
    qblk = pl.BlockSpec((None, 256, tq), lambda bi, g, i: (bi, g, i))
    return pl.pallas_call(
        _slcwin_body, grid=(b, NSA_GROUPS, s // tq),
        in_specs=[qblk,
                  pl.BlockSpec((None, None, nblk, tq), lambda bi, g, i: (bi, g, 0, i)),
                  pl.BlockSpec((None, None, 2, s, 128), lambda bi, g, i: (bi, g, 0, 0, 0)),
                  pl.BlockSpec((None, None, 2, ns, V_ROWS, tk), lambda bi, g, i: (bi, g, 0, 0, 0, 0)),
                  qblk,
                  pl.BlockSpec((None, None, GATE_ROWS, tq), lambda bi, g, i: (bi, g, 0, i)),
                  pl.BlockSpec((2, tk, tq), lambda bi, g, i: (0, 0, 0))],
        out_specs=qblk,
        out_shape=jax.ShapeDtypeStruct((b, 512, s), F32),
        scratch_shapes=[pltpu.VMEM((3, NSA_HPG, 128, tq), BF16),
                        pltpu.VMEM((2, NSA_HPG, 1, tq), F32), pltpu.VMEM((2, NSA_HPG, V_ROWS, tq), F32),
                        pltpu.VMEM((2, NSA_HPG, tk, tq), F32)],
        compiler_params=_cparams(("parallel", "parallel", "arbitrary")), name="nsa_selected_window",
    )(qt, bias, kaug, vt, oc, gt.reshape(b, NSA_GROUPS, GATE_ROWS, s), mask_bias)


def _hgrn_consts():
    c = HGRN_CHUNK
    t = np.arange(c)
    lower = (t[None, :] <= t[:, None]).astype(np.float32)
    rows = [lower]
    masks = []
    for half in HGRN_LEVELS:
        mid = (t // (2 * half)) * (2 * half) + half - 1
        if half < 8:
            rows.append(lower[mid])
        same = (t[:, None] // (2 * half)) == (t[None, :] // (2 * half))
        right = (t[:, None] & half) != 0
        left = (t[None, :] & half) == 0
        masks.append((same & right & left).astype(np.float32))
    masks.append(np.eye(c, dtype=np.float32))
    mall = np.concatenate(rows, axis=0)
    lvl = np.stack([np.tile(mk.T, (1, HGRN_HEADS)) for mk in masks])
    bdm = np.kron(np.eye(HGRN_HEADS), np.ones((64, 64), np.float32))
    return jnp.asarray(mall, BF16), jnp.asarray(lvl, F32), jnp.asarray(bdm, F32), jnp.asarray(bdm / 64, BF16)


def _hgrn_body(hg_ref, lb_ref, og_ref, mall_ref, lvl_ref, bdm_ref, bdn_ref, y_ref, st_scr):
    c = HGRN_CHUNK
    w = HGRN_WIDTH

    @pl.when(pl.program_id(1) == 0)
    def _():
        st_scr[...] = jnp.zeros(st_scr.shape, F32)

    chunks = range(hg_ref.shape[0] // c)
    lb = lb_ref[...]
    lane = lax.broadcasted_iota(jnp.int32, (c, w), 1)
    head_masks = [(lane >> 6) == hh for hh in range(HGRN_HEADS)]
    nlev = len(HGRN_LEVELS)

    def stack_heads(x):
        x16 = x.astype(BF16)
        return jnp.concatenate([jnp.where(hm, x16, 0) for hm in head_masks], axis=0)

    def row_bcast(x, half):
        return jnp.concatenate([jnp.broadcast_to(x[p + half - 1:p + half, :], (2 * half, w))
                                for p in range(0, c, 2 * half)], axis=0)

    qa, kk, v16, logf = [], [], [], []
    for ci in chunks:
        rows = slice(ci * c, (ci + 1) * c)
        qa.append(_silu(hg_ref[rows, 0:w]) * (HEAD_DIM ** -0.5))
        fg = lb + (1.0 - lb) * _sigmoid(hg_ref[rows, w:2 * w])
        kk.append(1.0 - fg)
        logf.append(jnp.log(fg))
        v16.append(hg_ref[rows, 2 * w:3 * w].astype(BF16))

    mall = mall_ref[...]
    r_all = sum(_dot(mall, part) for part in _split3(jnp.concatenate(logf, axis=1)))
    bcum = [r_all[0:c, ci * w:(ci + 1) * w] for ci in chunks]

    attn = [lvl_ref[nlev] * _dot_nt(kk[ci].astype(BF16), stack_heads(qa[ci])) for ci in chunks]
    fine = 0
    for li, half in enumerate(HGRN_LEVELS):
        for ci in chunks:
            if half >= 8:
                ref_pt = row_bcast(bcum[ci], half)
            else:
                ref_pt = r_all[(fine + 1) * c:(fine + 2) * c, ci * w:(ci + 1) * w]
            e = jnp.exp(-jnp.abs(bcum[ci] - ref_pt))
            kt = (kk[ci] * e).astype(BF16)
            attn[ci] = attn[ci] + lvl_ref[li] * _dot_nt(kt, stack_heads(qa[ci] * e))
        if half < 8:
            fine += 1

    intra, upd, decay, qb = [], [], [], []
    for ci in chunks:
        x = _dot_tn(attn[ci].astype(BF16), v16[ci])
        intra.append(sum(jnp.where(head_masks[hh], x[hh * c:(hh + 1) * c], 0.0) for hh in range(HGRN_HEADS)))
        b_last = bcum[ci][c - 1:c, :]
        kl = (kk[ci] * jnp.exp(b_last - bcum[ci])).astype(BF16)
        upd.append(bdm_ref[...] * _dot_tn(v16[ci], kl))
        decay.append(jnp.exp(b_last))
        qb.append((qa[ci] * jnp.exp(bcum[ci])).astype(BF16))

    st = st_scr[...]
    inter = []
    for ci in chunks:
        inter.append(_dot_nt(qb[ci], st.astype(BF16)))
        st = st * decay[ci] + upd[ci]
    st_scr[...] = st

    for ci in chunks:
        rows = slice(ci * c, (ci + 1) * c)
        o = inter[ci] + intra[ci]
        hi, lo = _split2(o * o)
        ms = _dot(hi, bdn_ref[...]) + _dot(lo, bdn_ref[...])
        y_ref[rows, :] = o * lax.rsqrt(ms + EPS) * og_ref[...] * _silu(hg_ref[rows, 3 * w:4 * w])


def _hgrn(hg, lower_bound, out_gain):
    b, s, _ = hg.shape
    rows = HGRN_CB * HGRN_CHUNK
    mall, lvl, bdm, bdn = _hgrn_consts()
    full = lambda shape: pl.BlockSpec(shape, lambda bi, i: (0,) * len(shape))
    return pl.pallas_call(
        _hgrn_body, grid=(b, s // rows),
        in_specs=[pl.BlockSpec((None, rows, 4 * HGRN_WIDTH), lambda bi, i: (bi, i, 0)),
                  full((1, HGRN_WIDTH)), full((1, HGRN_WIDTH)),
                  full(mall.shape), full(lvl.shape), full(bdm.shape), full(bdn.shape)],
        out_specs=pl.BlockSpec((None, rows, HGRN_WIDTH), lambda bi, i: (bi, i, 0)),
        out_shape=jax.ShapeDtypeStruct((b, s, HGRN_WIDTH), F32),
        scratch_shapes=[pltpu.VMEM((HGRN_WIDTH, HGRN_WIDTH), F32)],
        compiler_params=_cparams(("parallel", "arbitrary")), name="hgrn2_chunks",
    )(hg, lower_bound.reshape(1, -1), out_gain.reshape(1, -1), mall, lvl, bdm, bdn)


def _memkv_body(mem_ref, mg_ref, wk_ref, wvt_ref, kg_ref, kh_ref, vht_ref):
    m = mem_ref.shape[0]
    mn = _rms_rows(mem_ref[...], mg_ref[...]).astype(BF16)
    k = _dot(mn, wk_ref[...])
    vt = _dot_nt(wvt_ref[...], mn)
    ones_rows = (lax.broadcasted_iota(jnp.int32, (V_ROWS - 64, m), 0) == 0).astype(BF16)
    for hh in range(MEM_HEADS):
        kh_ref[hh] = _rms_rows(k[:, hh * 64:(hh + 1) * 64], kg_ref[...]).astype(BF16)
        vht_ref[hh, 0:64, :] = vt[hh * 64:(hh + 1) * 64].astype(BF16)
        vht_ref[hh, 64:V_ROWS, :] = ones_rows


def _mem_kv(mem, mem_gain, wk, wvt, k_gain):
    b, m, d = mem.shape
    full = lambda shape: pl.BlockSpec(shape, lambda bi: (0,) * len(shape))
    return pl.pallas_call(
        _memkv_body, grid=(b,),
        in_specs=[pl.BlockSpec((None, m, d), lambda bi: (bi, 0, 0)), full((1, d)),
                  full(wk.shape), full(wvt.shape), full((1, 64))],
        out_specs=(pl.BlockSpec((None, MEM_HEADS, m, 64), lambda bi: (bi, 0, 0, 0)),
                   pl.BlockSpec((None, MEM_HEADS, V_ROWS, m), lambda bi: (bi, 0, 0, 0))),
        out_shape=(jax.ShapeDtypeStruct((b, MEM_HEADS, m, 64), BF16),
                   jax.ShapeDtypeStruct((b, MEM_HEADS, V_ROWS, m), BF16)),
        compiler_params=_cparams(("parallel",)), name="memory_kv",
    )(mem, mem_gain.reshape(1, d), wk, wvt, k_gain.reshape(1, 64))


def _out_body(x_ref, ynt_ref, yh_ref, qmt_ref, kh_ref, vht_ref, ng_ref, mg_ref, wo_ref,
              fg_ref, wg_ref, wu_ref, wd_ref, o_ref, a_scr):
    y_mem = []
    for hh in range(MEM_HEADS):
        s = _dot(kh_ref[hh], qmt_ref[hh * 64:(hh + 1) * 64, :])
        p = jnp.exp2(s - jnp.max(s, axis=0, keepdims=True))
        o = _dot(vht_ref[hh], p.astype(BF16))
        y_mem.append(o[0:64] * (1.0 / o[64:65]))
    mem = _rms_cols(jnp.concatenate(y_mem, axis=0), mg_ref[...]).astype(BF16)
    nsa = _rms_cols(ynt_ref[...], ng_ref[...]).astype(BF16)
    acc = _dot_tn(nsa, wo_ref[0:NSA_WIDTH, :])
    acc = acc + _dot(yh_ref[...].astype(BF16), wo_ref[NSA_WIDTH:NSA_WIDTH + HGRN_WIDTH, :])
    acc = acc + _dot_tn(mem, wo_ref[NSA_WIDTH + HGRN_WIDTH:, :])
    o_ref[...] = _ffn_half_step(x_ref[...] + acc, fg_ref, wg_ref, wu_ref, wd_ref, a_scr)


def _out_ffn(x3d, ynt, yh, qmt, kh, vht, nsa_gain, mem_gain, wo, ffn_gain, wg, wu, wd, *, tm=TOK_TILE):
    b, s, d = x3d.shape
    m = kh.shape[2]
    return pl.pallas_call(
        _out_body, grid=(b, s // tm),
        in_specs=[pl.BlockSpec((None, tm, d), lambda bi, i: (bi, i, 0)),
                  pl.BlockSpec((None, NSA_WIDTH, tm), lambda bi, i: (bi, 0, i)),
                  pl.BlockSpec((None, tm, HGRN_WIDTH), lambda bi, i: (bi, i, 0)),
                  pl.BlockSpec((None, MEM_WIDTH, tm), lambda bi, i: (bi, 0, i)),
                  pl.BlockSpec((None, MEM_HEADS, m, 64), lambda bi, i: (bi, 0, 0, 0)),
                  pl.BlockSpec((None, MEM_HEADS, V_ROWS, m), lambda bi, i: (bi, 0, 0, 0)),
                  _resident((NSA_WIDTH, 1)), _resident((MEM_WIDTH, 1)), _resident(wo.shape),
                  _resident((1, d)), _resident(wg.shape), _resident(wu.shape), _resident(wd.shape)],
        out_specs=pl.BlockSpec((None, tm, d), lambda bi, i: (bi, i, 0)),
        out_shape=jax.ShapeDtypeStruct((b, s, d), F32),
        scratch_shapes=[pltpu.VMEM((tm, wg.shape[1]), BF16)],
        compiler_params=_cparams(("parallel", "parallel")), name="mix_out_ffn2",
    )(x3d, ynt, yh, qmt, kh, vht, nsa_gain.reshape(-1, 1), mem_gain.reshape(-1, 1), wo,
      ffn_gain.reshape(1, d), wg, wu, wd)


def _layer(x, mem, ffn1, ffn2, mix_norm, w_in, w_out, nsa_q_norm, nsa_k_norm, cmp_pos_k, cmp_w1_k, cmp_w2_k,
           cmp_pos_v, cmp_w1_v, cmp_w2_v, nsa_out_norm, lower_bound, hgrn_out_norm,
           mem_norm, mem_w_k, mem_w_v, mem_q_norm, mem_k_norm, mem_out_norm):
    b, s, d = x.shape
    sizes = (512, 128, 128, 128, 128, 128, 128, 24, 256, 256, 256, 256, 256)
    offs = np.concatenate([[0], np.cumsum(sizes)])
    col = lambda i: w_in[:, offs[i]:offs[i + 1]]
    (q_a, k_c, v_c, k_s, v_s, k_w, v_w, g_a, q_h, f_h, i_h, g_h, q_m) = [col(i) for i in range(13)]
    gpad = jnp.zeros((d, GATE_ROWS - 3 * NSA_HPG), w_in.dtype)
    wt = jnp.concatenate([q_a, v_s, v_w, g_a[:, :3 * NSA_HPG], gpad, g_a[:, 3 * NSA_HPG:], gpad, q_m],
                         axis=1).T.astype(BF16)
    wn = jnp.concatenate([k_c, v_c, k_s, k_w, q_h, f_h, i_h, g_h], axis=1).astype(BF16)

    x1, qt, vt, gt, qmt, kaug, kvc, hg = _ffn_proj(
        x, *ffn1, mix_norm, wt, wn, nsa_q_norm, nsa_k_norm, mem_q_norm, _rope_tables(s))

    cmp_pos, cmp_w1 = _compress_weights(cmp_pos_k, cmp_pos_v, cmp_w1_k, cmp_w1_v)
    kc, vct = _compress(kvc, cmp_pos, cmp_w1, cmp_w2_k.astype(BF16), cmp_w2_v.T.astype(BF16), nsa_k_norm)
    oc, bias = _cmp_select(qt, kc, vct)
    y_nsa = _slc_win(qt, bias, kaug, vt, oc, gt)

    y_hgrn = _hgrn(hg, lower_bound, hgrn_out_norm)

    kh, vht = _mem_kv(mem, mem_norm, mem_w_k.astype(BF16), mem_w_v.T.astype(BF16), mem_k_norm)
    return _out_ffn(x1, y_nsa, y_hgrn, qmt, kh, vht, nsa_out_norm, mem_out_norm, w_out.astype(BF16), *ffn2)


def kernel(x, mem, ffn1_norm, ffn1_w_gate, ffn1_w_up, ffn1_w_down, mix_norm, w_in, w_out, nsa_q_norm, nsa_k_norm, cmp_pos_k, cmp_w1_k, cmp_w2_k, cmp_pos_v, cmp_w1_v, cmp_w2_v, nsa_out_norm, hgrn_lb_logits, hgrn_out_norm, mem_norm, mem_w_k, mem_w_v, mem_q_norm, mem_k_norm, mem_out_norm, ffn2_norm, ffn2_w_gate, ffn2_w_up, ffn2_w_down):
    b, s, d = x.shape
    depth = ffn1_norm.shape[0]
    lower_bounds = jnp.cumsum(jax.nn.softmax(hgrn_lb_logits.astype(F32), axis=0), axis=0)
    bf = lambda a: a.astype(BF16)
    for l in range(depth):
        x = _layer(x, mem, (ffn1_norm[l], bf(ffn1_w_gate[l]), bf(ffn1_w_up[l]), bf(ffn1_w_down[l])),
                   (ffn2_norm[l], bf(ffn2_w_gate[l]), bf(ffn2_w_up[l]), bf(ffn2_w_down[l])),
                   mix_norm[l], w_in[l], w_out[l], nsa_q_norm[l], nsa_k_norm[l],
                   cmp_pos_k[l], cmp_w1_k[l], cmp_w2_k[l], cmp_pos_v[l], cmp_w1_v[l], cmp_w2_v[l],
                   nsa_out_norm[l], lower_bounds[l], hgrn_out_norm[l],
                   mem_norm[l], mem_w_k[l], mem_w_v[l], mem_q_norm[l], mem_k_norm[l], mem_out_norm[l])
    return x
```

```python
import functools

import numpy as np
import jax
import jax.numpy as jnp
from jax import lax
from jax.experimental import pallas as pl
from jax.experimental.pallas import tpu as pltpu

F32 = jnp.float32
BF16 = jnp.bfloat16

HEAD_DIM = 64
ROT_DIM = 16
ROT_HALF = 8
ROPE_THETA = 500000.0
NSA_HEADS = 8
NSA_GROUPS = 2
NSA_HPG = 4
CMP_BLOCK = 32
CMP_STRIDE = 16
SLC_BLOCK = 64
SLC_SHIFT = 6
SLC_TOPK = 16
WINDOW = 512
FORCED_SCORE = 1e4
HGRN_HEADS = 4
HGRN_CHUNK = 64
HGRN_WIDTH = 256
MEM_HEADS = 4
MEM_WIDTH = 256
NSA_WIDTH = 512
EPS = 1e-6
NEG = -1e30
QK_SCALE_LOG2 = HEAD_DIM ** -0.5 * 1.4426950408889634

VMEM_LIMIT = 56 * 1024 * 1024
MAX_BLOCKS = 128
GATE_ROWS = 16
V_ROWS = 80
TOK_TILE = 512
HGRN_CB = 8
HGRN_LEVELS = (32, 16, 8, 4, 2, 1)

NT_DIMS = (((1,), (1,)), ((), ()))
TN_DIMS = (((0,), (0,)), ((), ()))


def _cparams(sem):
    return pltpu.CompilerParams(dimension_semantics=sem, vmem_limit_bytes=VMEM_LIMIT)


def _dot(a, b):
    return jnp.dot(a, b, preferred_element_type=F32)


def _dot_nt(a, b):
    return lax.dot_general(a, b, NT_DIMS, preferred_element_type=F32)


def _dot_tn(a, b):
    return lax.dot_general(a, b, TN_DIMS, preferred_element_type=F32)


def _sigmoid(x):
    return 1.0 / (1.0 + jnp.exp(-x))


def _silu(x):
    return x * _sigmoid(x)


def _split2(x):
    hi = x.astype(BF16)
    lo = (x - hi.astype(F32)).astype(BF16)
    return hi, lo


def _split3(x):
    hi = x.astype(BF16)
    r1 = x - hi.astype(F32)
    mid = r1.astype(BF16)
    lo = (r1 - mid.astype(F32)).astype(BF16)
    return hi, mid, lo


def _rms_rows(x, gain_row):
    ms = jnp.mean(x * x, axis=-1, keepdims=True)
    return x * lax.rsqrt(ms + EPS) * gain_row


def _rms_cols(x, gain_col):
    ms = jnp.mean(x * x, axis=0, keepdims=True)
    return x * lax.rsqrt(ms + EPS) * gain_col


def _seg_mean_sq(x, bd):
    hi, lo = _split2(x * x)
    return _dot(hi, bd) + _dot(lo, bd)


FFN_CHUNK = 256


def _ffn_half_step(x, g_ref, wg_ref, wu_ref, wd_ref, a_scr):
    xn = _rms_rows(x, g_ref[...]).astype(BF16)
    d_ff = wg_ref.shape[1]
    for c in range(d_ff // FFN_CHUNK):
        sl = slice(c * FFN_CHUNK, (c + 1) * FFN_CHUNK)
        g = _dot(xn, wg_ref[:, sl])
        u = _dot(xn, wu_ref[:, sl])
        a_scr[:, sl] = (_silu(g) * u).astype(BF16)
    return x + 0.5 * _dot(a_scr[...], wd_ref[...])


def _resident(shape):
    return pl.BlockSpec(shape, lambda *_: (0,) * len(shape), pipeline_mode=pl.Buffered(1))


def _rope_cols(xn, cos, sin):
    x0, x1 = xn[0:ROT_HALF], xn[ROT_HALF:ROT_DIM]
    return jnp.concatenate([x0 * cos - x1 * sin, x1 * cos + x0 * sin, xn[ROT_DIM:]], axis=0)


def _rope_rows(x, cn, sa, sb):
    return x * cn + pltpu.roll(x, 128 - ROT_HALF, 1) * sa + pltpu.roll(x, ROT_HALF, 1) * sb


def _proj_body(x_ref, fg_ref, wg_ref, wu_ref, wd_ref, mg_ref, wt_ref, wn_ref, qg_ref, kg_ref, mqg_ref,
               cos_ref, sin_ref, cn_ref, sa_ref, sb_ref, bd_ref,
               x1_ref, qt_ref, vt_ref, gt_ref, qmt_ref, kaug_ref, kvc_ref, hg_ref, a_scr):
    tm = x_ref.shape[0]
    assert tm == 8 * SLC_BLOCK
    x1 = _ffn_half_step(x_ref[...], fg_ref, wg_ref, wu_ref, wd_ref, a_scr)
    x1_ref[...] = x1
    h = _rms_rows(x1, mg_ref[...]).astype(BF16)

    pt = _dot_nt(wt_ref[...], h)
    cos, sin = cos_ref[...], sin_ref[...]
    qg = qg_ref[...]
    for hh in range(NSA_HEADS):
        xq = _rms_cols(pt[hh * 64:(hh + 1) * 64], qg)
        qt_ref[hh * 64:(hh + 1) * 64, :] = (_rope_cols(xq, cos, sin) * QK_SCALE_LOG2).astype(BF16)
    ones_rows = (lax.broadcasted_iota(jnp.int32, (V_ROWS - 64, tm), 0) == 0).astype(BF16)
    for g in range(NSA_GROUPS):
        for br in range(2):
            rows = 512 + br * 128 + g * 64
            vt_ref[g, br, 0:64, :] = pt[rows:rows + 64].astype(BF16)
            vt_ref[g, br, 64:V_ROWS, :] = ones_rows
    gt_ref[...] = _sigmoid(pt[768:800])
    mqg = mqg_ref[...]
    for hh in range(MEM_HEADS):
        xm = _rms_cols(pt[800 + hh * 64:864 + hh * 64], mqg)
        qmt_ref[hh * 64:(hh + 1) * 64, :] = (xm * QK_SCALE_LOG2).astype(BF16)

    pn = _dot(h, wn_ref[...])
    cn, sa, sb = cn_ref[...], sa_ref[...], sb_ref[...]
    bd, kg = bd_ref[...], kg_ref[...]
    kvc_ref[0] = _rope_rows(pn[:, 0:128], cn, sa, sb)
    kvc_ref[1] = pn[:, 128:256]
    ks = pn[:, 256:384]
    kw = pn[:, 384:512]
    ks = _rope_rows(ks * lax.rsqrt(_seg_mean_sq(ks, bd) + EPS) * kg, cn, sa, sb)
    kw = _rope_rows(kw * lax.rsqrt(_seg_mean_sq(kw, bd) + EPS) * kg, cn, sa, sb)
    lane = lax.broadcasted_iota(jnp.int32, (tm, 128), 1)
    row = lax.broadcasted_iota(jnp.int32, (tm, 128), 0)
    onehot = jnp.where(lane - 64 == (row >> SLC_SHIFT), 1.0, 0.0)
    lo_half = lane < 64
    kaug_ref[0, 0] = jnp.where(lo_half, ks, onehot).astype(BF16)
    kaug_ref[0, 1] = jnp.where(lo_half, kw, 0.0).astype(BF16)
    kaug_ref[1, 0] = jnp.where(lo_half, pltpu.roll(ks, 64, 1), onehot).astype(BF16)
    kaug_ref[1, 1] = jnp.where(lo_half, pltpu.roll(kw, 64, 1), 0.0).astype(BF16)
    hg_ref[...] = pn[:, 512:1536]


def _ffn_proj(x3d, ffn_gain, wg, wu, wd, mix_gain, wt, wn, q_gain, k_gain, mq_gain, rope):
    b, s, d = x3d.shape
    tm = TOK_TILE
    ns = s // tm
    cos_t, sin_t, cn, sa, sb = rope
    bd = jnp.asarray(np.kron(np.eye(2), np.full((64, 64), 1.0 / 64)), BF16)
    full = _resident
    out_shape = (
        jax.ShapeDtypeStruct((b, s, d), F32),
        jax.ShapeDtypeStruct((b, 512, s), BF16),
        jax.ShapeDtypeStruct((b, 2, 2, ns, V_ROWS, tm), BF16),
        jax.ShapeDtypeStruct((b, 32, s), F32),
        jax.ShapeDtypeStruct((b, 256, s), BF16),
        jax.ShapeDtypeStruct((b, 2, 2, s, 128), BF16),
        jax.ShapeDtypeStruct((b, 2, s, 128), F32),
        jax.ShapeDtypeStruct((b, s, 1024), F32),
    )
    out_specs = (
        pl.BlockSpec((None, tm, d), lambda bi, i: (bi, i, 0)),
        pl.BlockSpec((None, 512, tm), lambda bi, i: (bi, 0, i)),
        pl.BlockSpec((None, 2, 2, None, V_ROWS, tm), lambda bi, i: (bi, 0, 0, i, 0, 0)),
        pl.BlockSpec((None, 32, tm), lambda bi, i: (bi, 0, i)),
        pl.BlockSpec((None, 256, tm), lambda bi, i: (bi, 0, i)),
        pl.BlockSpec((None, 2, 2, tm, 128), lambda bi, i: (bi, 0, 0, i, 0)),
        pl.BlockSpec((None, 2, tm, 128), lambda bi, i: (bi, 0, i, 0)),
        pl.BlockSpec((None, tm, 1024), lambda bi, i: (bi, i, 0)),
    )
    in_specs = [
        pl.BlockSpec((None, tm, d), lambda bi, i: (bi, i, 0)),
        full((1, d)), full(wg.shape), full(wu.shape), full(wd.shape),
        full((1, d)), full(wt.shape), full(wn.shape),
        full((64, 1)), full((1, 128)), full((64, 1)),
        pl.BlockSpec((ROT_HALF, tm), lambda bi, i: (0, i)),
        pl.BlockSpec((ROT_HALF, tm), lambda bi, i: (0, i)),
        pl.BlockSpec((tm, 128), lambda bi, i: (i, 0)),
        pl.BlockSpec((tm, 128), lambda bi, i: (i, 0)),
        pl.BlockSpec((tm, 128), lambda bi, i: (i, 0)),
        full((128, 128)),
    ]
    return pl.pallas_call(
        _proj_body, grid=(b, ns), in_specs=in_specs, out_specs=out_specs, out_shape=out_shape,
        scratch_shapes=[pltpu.VMEM((tm, wg.shape[1]), BF16)],
        compiler_params=_cparams(("parallel", "parallel")), name="ffn1_mix_projection",
    )(x3d, ffn_gain.reshape(1, d), wg, wu, wd, mix_gain.reshape(1, d), wt, wn, q_gain.reshape(64, 1),
      jnp.tile(k_gain.reshape(1, 64), (1, 2)), mq_gain.reshape(64, 1), cos_t, sin_t, cn, sa, sb, bd)


def _rope_tables(s):
    pos = jnp.arange(s, dtype=F32)
    inv = ROPE_THETA ** (-(jnp.arange(0, ROT_DIM, 2, dtype=F32) / ROT_DIM))
    ang = pos[:, None] * inv[None, :]
    cos, sin = jnp.cos(ang), jnp.sin(ang)
    zeros = jnp.zeros((s, 64 - ROT_DIM), F32)
    cn = jnp.concatenate([cos, cos, jnp.ones((s, 64 - ROT_DIM), F32)], axis=1)
    sa = jnp.concatenate([-sin, jnp.zeros((s, ROT_HALF), F32), zeros], axis=1)
    sb = jnp.concatenate([jnp.zeros((s, ROT_HALF), F32), sin, zeros], axis=1)
    tile2 = lambda a: jnp.concatenate([a, a], axis=1)
    return cos.T, sin.T, tile2(cn), tile2(sa), tile2(sb)


def _cmp_body(kvc_ref, pos_ref, w1_ref, w2k_ref, w2vt_ref, kg_ref, kc_ref, vct_ref):
    nc = kvc_ref.shape[1] // CMP_STRIDE
    for kind in range(2):
        halves = []
        for part in range(2):
            x = jnp.concatenate(
                [(kvc_ref[kind, pl.ds(r, nc, stride=CMP_STRIDE), :]
                  + pos_ref[kind, part, :, r * 128:(r + 1) * 128]).astype(BF16) for r in range(CMP_STRIDE)],
                axis=1)
            halves.append(x)
        for g in range(NSA_GROUPS):
            second = _dot(halves[1], w1_ref[kind, g, 1])
            hid = _silu(_dot(halves[0], w1_ref[kind, g, 0]) + pltpu.roll(second, nc - 1, 0)).astype(BF16)
            if kind == 0:
                kc_ref[g] = _rms_rows(_dot(hid, w2k_ref[...]), kg_ref[...]).astype(BF16)
            else:
                vct_ref[g] = _dot_nt(w2vt_ref[...], hid).astype(BF16)


def _compress(kvc, pos, w1, w2k, w2vt, k_gain):
    b, _, s, _ = kvc.shape
    nc = s // CMP_STRIDE
    return pl.pallas_call(
        _cmp_body, grid=(b,),
        in_specs=[pl.BlockSpec((None, 2, s, 128), lambda bi: (bi, 0, 0, 0)),
                  _resident(pos.shape), _resident(w1.shape), _resident(w2k.shape), _resident(w2vt.shape),
                  _resident((1, 64))],
        out_specs=(pl.BlockSpec((None, 2, nc, 64), lambda bi: (bi, 0, 0, 0)),
                   pl.BlockSpec((None, 2, 64, nc), lambda bi: (bi, 0, 0, 0))),
        out_shape=(jax.ShapeDtypeStruct((b, 2, nc, 64), BF16), jax.ShapeDtypeStruct((b, 2, 64, nc), BF16)),
        compiler_params=_cparams(("parallel",)), name="nsa_compress",
    )(kvc, pos, w1, w2k, w2vt, k_gain.reshape(1, 64))


def _compress_weights(pos_k, pos_v, w1_k, w1_v):
    def pos_part(p):
        p = p.reshape(2, CMP_STRIDE, 1, 64)
        return jnp.broadcast_to(p, (2, CMP_STRIDE, NSA_GROUPS, 64)).reshape(2, 1, CMP_STRIDE * 128)

    def w1_part(w):
        hdim = w.shape[1]
        w = w.reshape(2, CMP_STRIDE, 1, 64, hdim)
        per_group = []
        for g in range(NSA_GROUPS):
            pads = [w if gg == g else jnp.zeros_like(w) for gg in range(NSA_GROUPS)]
            per_group.append(jnp.concatenate(pads, axis=2).reshape(2, CMP_STRIDE * 128, hdim))
        return jnp.stack(per_group)

    pos = jnp.stack([pos_part(pos_k), pos_part(pos_v)])
    w1 = jnp.stack([w1_part(w1_k), w1_part(w1_v)]).astype(BF16)
    return pos, w1


CMP_CLASS_ROWS = 128


def _cmpsel_variant(nc, nblk, qt_ref, kc_ref, vct_ref, oc_ref, bias_ref, s_scr):
    tq = qt_ref.shape[1]
    t0 = pl.program_id(2) * tq
    n_idx = lax.broadcasted_iota(jnp.int32, (nc, tq), 0)
    t_idx = t0 + lax.broadcasted_iota(jnp.int32, (nc, tq), 1)
    mask_bias = jnp.where(n_idx * CMP_STRIDE + (CMP_BLOCK - 1) <= t_idx, 0.0, NEG)
    t_row = t0 + lax.broadcasted_iota(jnp.int32, (1, tq), 1)
    kc = kc_ref[0:nc, :]
    for hh in range(NSA_HPG):
        s_scr[hh, 0:nc, :] = _dot(kc, qt_ref[hh * 64:(hh + 1) * 64, :]) + mask_bias
    jj = lax.broadcasted_iota(jnp.int32, (nblk, nc), 0)
    nn = lax.broadcasted_iota(jnp.int32, (nblk, nc), 1)
    ov = jnp.where((nn * CMP_STRIDE < jj * SLC_BLOCK + SLC_BLOCK)
                   & (nn * CMP_STRIDE + CMP_BLOCK > jj * SLC_BLOCK), 1.0, 0.0).astype(BF16)
    ones_rows = (lax.broadcasted_iota(jnp.int32, (V_ROWS - 64, nc), 0) == 0).astype(BF16)
    lhs = jnp.concatenate([vct_ref[:, 0:nc], ones_rows, ov], axis=0)
    imp = jnp.zeros((nblk, tq), F32)
    for hh in range(NSA_HPG):
        m = jnp.max(s_scr[hh, 0:nc, :], axis=0, keepdims=True)
        p = jnp.exp2(s_scr[hh, 0:nc, :] - m).astype(BF16)
        r = _dot(lhs, p)
        inv_l = jnp.where(t_row >= CMP_BLOCK - 1, 1.0 / r[64:65], 0.0)
        oc_ref[hh * 64:(hh + 1) * 64, :] = r[0:64] * inv_l
        imp = imp + r[V_ROWS:] * inv_l

    j = lax.broadcasted_iota(jnp.int32, (nblk, tq), 0)
    cur = (t0 + lax.broadcasted_iota(jnp.int32, (nblk, tq), 1)) >> SLC_SHIFT
    forced = (j == 0) | (j == cur) | (j == cur - 1)
    bias = jnp.where(forced & (j <= cur), 0.0, NEG)
    imp = jnp.where((j <= cur) & jnp.logical_not(forced), imp, -1.0)
    jf = j.astype(F32)
    for _ in range(SLC_TOPK - 3):
        v = jnp.max(imp, axis=0, keepdims=True)
        first = jnp.min(jnp.where(imp == v, jf, float(nblk)), axis=0, keepdims=True)
        pick = jf == first
        bias = jnp.where(pick & (v >= 0.0), 0.0, bias)
        imp = jnp.where(pick, -3e38, imp)
    bias_ref[0:nblk, :] = bias
    if nblk < bias_ref.shape[0]:
        bias_ref[nblk:, :] = jnp.full((bias_ref.shape[0] - nblk, tq), NEG, F32)


def _cmpsel_body(qt_ref, kc_ref, vct_ref, oc_ref, bias_ref, s_scr):
    tq = qt_ref.shape[1]
    nc_total = kc_ref.shape[0]
    tiles_per_class = CMP_CLASS_ROWS // (tq // CMP_STRIDE)
    cls = pl.program_id(2) // tiles_per_class
    for c in range(nc_total // CMP_CLASS_ROWS):
        nc = (c + 1) * CMP_CLASS_ROWS
        nblk = min(nc * CMP_STRIDE // SLC_BLOCK, bias_ref.shape[0])
        pl.when(cls == c)(functools.partial(_cmpsel_variant, nc, nblk, qt_ref, kc_ref, vct_ref,
                                            oc_ref, bias_ref, s_scr))


def _cmp_select(qt, kc, vct, *, tq=TOK_TILE):
    b, _, s = qt.shape
    nc = kc.shape[2]
    nblk = MAX_BLOCKS
    assert s // SLC_BLOCK <= MAX_BLOCKS and s // SLC_BLOCK >= SLC_TOPK and nc % CMP_CLASS_ROWS == 0
    return pl.pallas_call(
        _cmpsel_body, grid=(b, NSA_GROUPS, s // tq),
        in_specs=[pl.BlockSpec((None, 256, tq), lambda bi, g, i: (bi, g, i)),
                  pl.BlockSpec((None, None, nc, 64), lambda bi, g, i: (bi, g, 0, 0)),
                  pl.BlockSpec((None, None, 64, nc), lambda bi, g, i: (bi, g, 0, 0))],
        out_specs=(pl.BlockSpec((None, 256, tq), lambda bi, g, i: (bi, g, i)),
                   pl.BlockSpec((None, None, nblk, tq), lambda bi, g, i: (bi, g, 0, i))),
        out_shape=(jax.ShapeDtypeStruct((b, 512, s), F32),
                   jax.ShapeDtypeStruct((b, NSA_GROUPS, nblk, s), F32)),
        scratch_shapes=[pltpu.VMEM((NSA_HPG, nc, tq), F32)],
        compiler_params=_cparams(("parallel", "parallel", "parallel")), name="nsa_compressed_select",
    )(qt, kc, vct)


def _flash_step(s_ref, vt, m_ref, acc_ref):
    m_old = m_ref[...]
    m_new = jnp.maximum(m_old, jnp.max(s_ref[...], axis=0, keepdims=True))
    p = jnp.exp2(s_ref[...] - m_new)
    acc_ref[...] = jnp.exp2(m_old - m_new) * acc_ref[...] + _dot(vt, p.astype(BF16))
    m_ref[...] = m_new


SEL, WIN = 0, 1
BIAS_ROWS = 16


MIN_DENOMINATOR = 2.0 ** -64


def _slcwin_body(m0_ref, qt_ref, bias_ref, kaug_ref, vt_ref, oc_ref, gt_ref, mb_ref, y_ref,
                 q_scr, m_scr, acc_scr, s_scr, p_scr):
    tq = qt_ref.shape[1]
    tk = vt_ref.shape[3]
    assert tq == tk and WINDOW == tk and tk == 8 * SLC_BLOCK
    diag = pl.program_id(2)
    m0 = m0_ref[0]

    qs, qs2, qw = 0, 1, 2
    zeros = jnp.zeros((64, tq), BF16)
    for slot in (qs, qs2, qw):
        for hh in range(NSA_HPG):
            q_scr[slot, hh, 0:64, :] = qt_ref[hh * 64:(hh + 1) * 64, :]
            q_scr[slot, hh, 64:128, :] = zeros

    def set_selection_bias(kt, slot=qs):
        rows = bias_ref[pl.ds(pl.multiple_of(kt * 8, 8), 8), :]
        b16 = jnp.concatenate([rows, jnp.zeros_like(rows)], axis=0).astype(BF16)
        for hh in range(NSA_HPG):
            q_scr[slot, hh, 64:64 + BIAS_ROWS, :] = b16

    whole_tile = ((0, tk, 0, tq),)

    def tiles_fixed_reference(tiles):
        chains = [(br, qslot, kt, mask_bias, slice(r0, r0 + nr), slice(l0, l0 + nl), hh)
                  for br, qslot, kt, mask_bias, blocks in tiles
                  for r0, nr, l0, nl in blocks for hh in range(NSA_HPG)]
        for c in range(len(chains) + 1):
            if c < len(chains):
                br, qslot, kt, mask_bias, rows, lanes, hh = chains[c]
                k = kaug_ref[br, pl.ds(pl.multiple_of(kt * tk + rows.start, 128), rows.stop - rows.start), :]
                s = _dot(k, q_scr[qslot, hh, :, lanes])
                if mask_bias is not None:
                    s = s + mask_bias(rows, lanes)
                p_scr[c % 4, rows, lanes] = jnp.exp2(s - m0).astype(BF16)
            if c >= 1:
                br, _, kt, _, rows, lanes, hh = chains[c - 1]
                acc_scr[br, hh, :, lanes] = (acc_scr[br, hh, :, lanes]
                                             + _dot(vt_ref[br, kt, :, rows], p_scr[(c - 1) % 4, rows, lanes]))

    def tile_running_max(br, qslot, kt, mask_bias=None, blocks=whole_tile):
        for r0, nr, l0, nl in blocks:
            rows, lanes = slice(r0, r0 + nr), slice(l0, l0 + nl)
            k = kaug_ref[br, pl.ds(pl.multiple_of(kt * tk + r0, 128), nr), :]
            for hh in range(NSA_HPG):
                s = _dot(k, q_scr[qslot, hh, :, lanes])
                s_scr[hh, rows, lanes] = s if mask_bias is None else s + mask_bias(rows, lanes)
        for r0, nr, l0, nl in blocks:
            rows, lanes = slice(r0, r0 + nr), slice(l0, l0 + nl)
            v = vt_ref[br, kt, :, rows]
            for hh in range(NSA_HPG):
                _flash_step(s_scr.at[hh, rows, lanes], v, m_scr.at[br, hh, :, lanes], acc_scr.at[br, hh, :, lanes])

    half = tq // 2
    band_blocks = ((0, tk, 0, half), (half, tk - half, half, tq - half))
    causal_blocks = ((0, half, 0, half), (0, tk, half, tq - half))

    prev = jnp.maximum(diag - 1, 0)
    no_prev = jnp.where(diag == 0, NEG, 0.0)
    band_bias = lambda rows, lanes: mb_ref[1, rows, lanes] + no_prev
    causal_bias = lambda rows, lanes: mb_ref[0, rows, lanes]

    acc_scr[...] = jnp.zeros(acc_scr.shape, F32)

    def tile_pair(j, carry):
        kt = 2 * j
        set_selection_bias(kt, qs)
        set_selection_bias(kt + 1, qs2)
        tiles_fixed_reference([(SEL, qs, kt, None, whole_tile), (SEL, qs2, kt + 1, None, whole_tile)])
        return carry

    lax.fori_loop(0, diag >> 1, tile_pair, 0)

    @pl.when((diag & 1) == 1)
    def _():
        set_selection_bias(diag - 1, qs)
        tiles_fixed_reference([(SEL, qs, diag - 1, None, whole_tile)])

    set_selection_bias(diag, qs)
    tiles_fixed_reference([(WIN, qw, prev, band_bias, band_blocks), (SEL, qs, diag, causal_bias, causal_blocks),
                           (WIN, qw, diag, causal_bias, causal_blocks)])

    denominators = acc_scr[:, :, 64:65, :]
    underflow = jnp.logical_not(jnp.min(denominators) > MIN_DENOMINATOR)

    @pl.when(underflow)
    def _():
        m_scr[...] = jnp.full(m_scr.shape, NEG, F32)
        acc_scr[...] = jnp.zeros(acc_scr.shape, F32)

        def full_tile(kt, carry):
            set_selection_bias(kt, qs)
            tile_running_max(SEL, qs, kt)
            return carry

        lax.fori_loop(0, diag, full_tile, 0)
        set_selection_bias(diag, qs)
        tile_running_max(WIN, qw, prev, band_bias, band_blocks)
        tile_running_max(SEL, qs, diag, causal_bias, causal_blocks)
        tile_running_max(WIN, qw, diag, causal_bias, causal_blocks)

    gt = gt_ref[...]
    for hh in range(NSA_HPG):
        o_s = acc_scr[SEL, hh, 0:64, :] * (1.0 / acc_scr[SEL, hh, 64:65, :])
        o_w = acc_scr[WIN, hh, 0:64, :] * (1.0 / acc_scr[WIN, hh, 64:65, :])
        y_ref[hh * 64:(hh + 1) * 64, :] = (gt[3 * hh:3 * hh + 1] * oc_ref[hh * 64:(hh + 1) * 64, :]
                                          + gt[3 * hh + 1:3 * hh + 2] * o_s
                                          + gt[3 * hh + 2:3 * hh + 3] * o_w)


def _score_bound(q_gain, k_gain):
    bound = HEAD_DIM * QK_SCALE_LOG2 * jnp.max(jnp.abs(q_gain)) * jnp.max(jnp.abs(k_gain))
    return (1.02 * bound).reshape(1).astype(F32)


def _slc_win(qt, bias, kaug, vt, oc, gt, score_bound):
    b, _, s = qt.shape
    nblk = bias.shape[2]
    ns, tk = vt.shape[3], vt.shape[5]
    tq = tk
    key_rel, t_rel = np.arange(tk)[:, None], np.arange(tq)[None, :]
    mask_bias = jnp.asarray(np.stack([np.where(key_rel <= t_rel, 0.0, NEG),
                                      np.where(t_rel + tk - key_rel < WINDOW, 0.0, NEG)]), F32)
    qblk = pl.BlockSpec((None, 256, tq), lambda bi, g, i: (bi, g, i))
    return pl.pallas_call(
        _slcwin_body, grid=(b, NSA_GROUPS, s // tq),
        in_specs=[pl.BlockSpec(memory_space=pltpu.SMEM), qblk,
                  pl.BlockSpec((None, None, nblk, tq), lambda bi, g, i: (bi, g, 0, i)),
                  pl.BlockSpec((None, None, 2, s, 128), lambda bi, g, i: (bi, g, 0, 0, 0)),
                  pl.BlockSpec((None, None, 2, ns, V_ROWS, tk), lambda bi, g, i: (bi, g, 0, 0, 0, 0)),
                  qblk,
                  pl.BlockSpec((None, None, GATE_ROWS, tq), lambda bi, g, i: (bi, g, 0, i)),
                  _resident((2, tk, tq))],
        out_specs=qblk,
        out_shape=jax.ShapeDtypeStruct((b, 512, s), F32),
        scratch_shapes=[pltpu.VMEM((3, NSA_HPG, 128, tq), BF16),
                        pltpu.VMEM((2, NSA_HPG, 1, tq), F32), pltpu.VMEM((2, NSA_HPG, V_ROWS, tq), F32),
                        pltpu.VMEM((NSA_HPG, tk, tq), F32), pltpu.VMEM((NSA_HPG, tk, tq), BF16)],
        compiler_params=_cparams(("parallel", "parallel", "arbitrary")), name="nsa_selected_window",
    )(score_bound, qt, bias, kaug, vt, oc, gt.reshape(b, NSA_GROUPS, GATE_ROWS, s), mask_bias)


def _hgrn_consts():
    c = HGRN_CHUNK
    t = np.arange(c)
    lower = (t[None, :] <= t[:, None]).astype(np.float32)
    rows = [lower]
    masks = []
    for half in HGRN_LEVELS:
        mid = (t // (2 * half)) * (2 * half) + half - 1
        if half < 8:
            rows.append(lower[mid])
        same = (t[:, None] // (2 * half)) == (t[None, :] // (2 * half))
        right = (t[:, None] & half) != 0
        left = (t[None, :] & half) == 0
        masks.append((same & right & left).astype(np.float32))
    masks.append(np.eye(c, dtype=np.float32))
    mall = np.concatenate(rows, axis=0)
    lvl = np.stack([np.tile(mk.T, (1, HGRN_HEADS)) for mk in masks])
    bdm = np.kron(np.eye(HGRN_HEADS), np.ones((64, 64), np.float32))
    return jnp.asarray(mall, BF16), jnp.asarray(lvl, F32), jnp.asarray(bdm, F32), jnp.asarray(bdm / 64, BF16)


def _hgrn_body(hg_ref, lb_ref, og_ref, mall_ref, lvl_ref, bdm_ref, bdn_ref, y_ref, st_scr):
    c = HGRN_CHUNK
    w = HGRN_WIDTH

    @pl.when(pl.program_id(1) == 0)
    def _():
        st_scr[...] = jnp.zeros(st_scr.shape, F32)

    chunks = range(hg_ref.shape[0] // c)
    lb = lb_ref[...]
    lane = lax.broadcasted_iota(jnp.int32, (c, w), 1)
    head_masks = [(lane >> 6) == hh for hh in range(HGRN_HEADS)]
    nlev = len(HGRN_LEVELS)

    def stack_heads(x):
        x16 = x.astype(BF16)
        return jnp.concatenate([jnp.where(hm, x16, 0) for hm in head_masks], axis=0)

    def row_bcast(x, half):
        return jnp.concatenate([jnp.broadcast_to(x[p + half - 1:p + half, :], (2 * half, w))
                                for p in range(0, c, 2 * half)], axis=0)

    qa, kk, v16, logf = [], [], [], []
    for ci in chunks:
        rows = slice(ci * c, (ci + 1) * c)
        qa.append(_silu(hg_ref[rows, 0:w]) * (HEAD_DIM ** -0.5))
        fg = lb + (1.0 - lb) * _sigmoid(hg_ref[rows, w:2 * w])
        kk.append(1.0 - fg)
        logf.append(jnp.log(fg))
        v16.append(hg_ref[rows, 2 * w:3 * w].astype(BF16))

    mall = mall_ref[...]
    r_all = sum(_dot(mall, part) for part in _split3(jnp.concatenate(logf, axis=1)))
    bcum = [r_all[0:c, ci * w:(ci + 1) * w] for ci in chunks]

    attn = [lvl_ref[nlev] * _dot_nt(kk[ci].astype(BF16), stack_heads(qa[ci])) for ci in chunks]
    fine = 0
    for li, half in enumerate(HGRN_LEVELS):
        for ci in chunks:
            if half >= 8:
                ref_pt = row_bcast(bcum[ci], half)
            else:
                ref_pt = r_all[(fine + 1) * c:(fine + 2) * c, ci * w:(ci + 1) * w]
            e = jnp.exp(-jnp.abs(bcum[ci] - ref_pt))
            kt = (kk[ci] * e).astype(BF16)
            attn[ci] = attn[ci] + lvl_ref[li] * _dot_nt(kt, stack_heads(qa[ci] * e))
        if half < 8:
            fine += 1

    intra, upd, decay, qb = [], [], [], []
    for ci in chunks:
        x = _dot_tn(attn[ci].astype(BF16), v16[ci])
        intra.append(sum(jnp.where(head_masks[hh], x[hh * c:(hh + 1) * c], 0.0) for hh in range(HGRN_HEADS)))
        b_last = bcum[ci][c - 1:c, :]
        kl = (kk[ci] * jnp.exp(b_last - bcum[ci])).astype(BF16)
        upd.append(bdm_ref[...] * _dot_tn(v16[ci], kl))
        decay.append(jnp.exp(b_last))
        qb.append((qa[ci] * jnp.exp(bcum[ci])).astype(BF16))

    st = st_scr[...]
    inter = []
    for ci in chunks:
        inter.append(_dot_nt(qb[ci], st.astype(BF16)))
        st = st * decay[ci] + upd[ci]
    st_scr[...] = st

    for ci in chunks:
        rows = slice(ci * c, (ci + 1) * c)
        o = inter[ci] + intra[ci]
        hi, lo = _split2(o * o)
        ms = _dot(hi, bdn_ref[...]) + _dot(lo, bdn_ref[...])
        y_ref[rows, :] = o * lax.rsqrt(ms + EPS) * og_ref[...] * _silu(hg_ref[rows, 3 * w:4 * w])


def _hgrn(hg, lower_bound, out_gain):
    b, s, _ = hg.shape
    rows = HGRN_CB * HGRN_CHUNK
    mall, lvl, bdm, bdn = _hgrn_consts()
    full = lambda shape: pl.BlockSpec(shape, lambda bi, i: (0,) * len(shape))
    return pl.pallas_call(
        _hgrn_body, grid=(b, s // rows),
        in_specs=[pl.BlockSpec((None, rows, 4 * HGRN_WIDTH), lambda bi, i: (bi, i, 0)),
                  full((1, HGRN_WIDTH)), full((1, HGRN_WIDTH)),
                  full(mall.shape), full(lvl.shape), full(bdm.shape), full(bdn.shape)],
        out_specs=pl.BlockSpec((None, rows, HGRN_WIDTH), lambda bi, i: (bi, i, 0)),
        out_shape=jax.ShapeDtypeStruct((b, s, HGRN_WIDTH), F32),
        scratch_shapes=[pltpu.VMEM((HGRN_WIDTH, HGRN_WIDTH), F32)],
        compiler_params=_cparams(("parallel", "arbitrary")), name="hgrn2_chunks",
    )(hg, lower_bound.reshape(1, -1), out_gain.reshape(1, -1), mall, lvl, bdm, bdn)


def _memkv_body(mem_ref, mg_ref, wk_ref, wvt_ref, kg_ref, kh_ref, vht_ref):
    m = mem_ref.shape[0]
    mn = _rms_rows(mem_ref[...], mg_ref[...]).astype(BF16)
    k = _dot(mn, wk_ref[...])
    vt = _dot_nt(wvt_ref[...], mn)
    ones_rows = (lax.broadcasted_iota(jnp.int32, (V_ROWS - 64, m), 0) == 0).astype(BF16)
    for hh in range(MEM_HEADS):
        kh_ref[hh] = _rms_rows(k[:, hh * 64:(hh + 1) * 64], kg_ref[...]).astype(BF16)
        vht_ref[hh, 0:64, :] = vt[hh * 64:(hh + 1) * 64].astype(BF16)
        vht_ref[hh, 64:V_ROWS, :] = ones_rows


def _mem_kv(mem, mem_gain, wk, wvt, k_gain):
    b, m, d = mem.shape
    full = lambda shape: pl.BlockSpec(shape, lambda bi: (0,) * len(shape))
    return pl.pallas_call(
        _memkv_body, grid=(b,),
        in_specs=[pl.BlockSpec((None, m, d), lambda bi: (bi, 0, 0)), full((1, d)),
                  full(wk.shape), full(wvt.shape), full((1, 64))],
        out_specs=(pl.BlockSpec((None, MEM_HEADS, m, 64), lambda bi: (bi, 0, 0, 0)),
                   pl.BlockSpec((None, MEM_HEADS, V_ROWS, m), lambda bi: (bi, 0, 0, 0))),
        out_shape=(jax.ShapeDtypeStruct((b, MEM_HEADS, m, 64), BF16),
                   jax.ShapeDtypeStruct((b, MEM_HEADS, V_ROWS, m), BF16)),
        compiler_params=_cparams(("parallel",)), name="memory_kv",
    )(mem, mem_gain.reshape(1, d), wk, wvt, k_gain.reshape(1, 64))


def _out_body(x_ref, ynt_ref, yh_ref, qmt_ref, kh_ref, vht_ref, ng_ref, mg_ref, wo_ref,
              fg_ref, wg_ref, wu_ref, wd_ref, o_ref, a_scr):
    y_mem = []
    for hh in range(MEM_HEADS):
        s = _dot(kh_ref[hh], qmt_ref[hh * 64:(hh + 1) * 64, :])
        p = jnp.exp2(s - jnp.max(s, axis=0, keepdims=True))
        o = _dot(vht_ref[hh], p.astype(BF16))
        y_mem.append(o[0:64] * (1.0 / o[64:65]))
    mem = _rms_cols(jnp.concatenate(y_mem, axis=0), mg_ref[...]).astype(BF16)
    nsa = _rms_cols(ynt_ref[...], ng_ref[...]).astype(BF16)
    acc = _dot_tn(nsa, wo_ref[0:NSA_WIDTH, :])
    acc = acc + _dot(yh_ref[...].astype(BF16), wo_ref[NSA_WIDTH:NSA_WIDTH + HGRN_WIDTH, :])
    acc = acc + _dot_tn(mem, wo_ref[NSA_WIDTH + HGRN_WIDTH:, :])
    o_ref[...] = _ffn_half_step(x_ref[...] + acc, fg_ref, wg_ref, wu_ref, wd_ref, a_scr)


def _out_ffn(x3d, ynt, yh, qmt, kh, vht, nsa_gain, mem_gain, wo, ffn_gain, wg, wu, wd, *, tm=TOK_TILE):
    b, s, d = x3d.shape
    m = kh.shape[2]
    return pl.pallas_call(
        _out_body, grid=(b, s // tm),
        in_specs=[pl.BlockSpec((None, tm, d), lambda bi, i: (bi, i, 0)),
                  pl.BlockSpec((None, NSA_WIDTH, tm), lambda bi, i: (bi, 0, i)),
                  pl.BlockSpec((None, tm, HGRN_WIDTH), lambda bi, i: (bi, i, 0)),
                  pl.BlockSpec((None, MEM_WIDTH, tm), lambda bi, i: (bi, 0, i)),
                  pl.BlockSpec((None, MEM_HEADS, m, 64), lambda bi, i: (bi, 0, 0, 0)),
                  pl.BlockSpec((None, MEM_HEADS, V_ROWS, m), lambda bi, i: (bi, 0, 0, 0)),
                  _resident((NSA_WIDTH, 1)), _resident((MEM_WIDTH, 1)), _resident(wo.shape),
                  _resident((1, d)), _resident(wg.shape), _resident(wu.shape), _resident(wd.shape)],
        out_specs=pl.BlockSpec((None, tm, d), lambda bi, i: (bi, i, 0)),
        out_shape=jax.ShapeDtypeStruct((b, s, d), F32),
        scratch_shapes=[pltpu.VMEM((tm, wg.shape[1]), BF16)],
        compiler_params=_cparams(("parallel", "parallel")), name="mix_out_ffn2",
    )(x3d, ynt, yh, qmt, kh, vht, nsa_gain.reshape(-1, 1), mem_gain.reshape(-1, 1), wo,
      ffn_gain.reshape(1, d), wg, wu, wd)


def _layer(x, mem, ffn1, ffn2, mix_norm, w_in, w_out, nsa_q_norm, nsa_k_norm, cmp_pos_k, cmp_w1_k, cmp_w2_k,
           cmp_pos_v, cmp_w1_v, cmp_w2_v, nsa_out_norm, lower_bound, hgrn_out_norm,
           mem_norm, mem_w_k, mem_w_v, mem_q_norm, mem_k_norm, mem_out_norm):
    b, s, d = x.shape
    sizes = (512, 128, 128, 128, 128, 128, 128, 24, 256, 256, 256, 256, 256)
    offs = np.concatenate([[0], np.cumsum(sizes)])
    col = lambda i: w_in[:, offs[i]:offs[i + 1]]
    (q_a, k_c, v_c, k_s, v_s, k_w, v_w, g_a, q_h, f_h, i_h, g_h, q_m) = [col(i) for i in range(13)]
    gpad = jnp.zeros((d, GATE_ROWS - 3 * NSA_HPG), w_in.dtype)
    wt = jnp.concatenate([q_a, v_s, v_w, g_a[:, :3 * NSA_HPG], gpad, g_a[:, 3 * NSA_HPG:], gpad, q_m],
                         axis=1).T.astype(BF16)
    wn = jnp.concatenate([k_c, v_c, k_s, k_w, q_h, f_h, i_h, g_h], axis=1).astype(BF16)

    x1, qt, vt, gt, qmt, kaug, kvc, hg = _ffn_proj(
        x, *ffn1, mix_norm, wt, wn, nsa_q_norm, nsa_k_norm, mem_q_norm, _rope_tables(s))

    cmp_pos, cmp_w1 = _compress_weights(cmp_pos_k, cmp_pos_v, cmp_w1_k, cmp_w1_v)
    kc, vct = _compress(kvc, cmp_pos, cmp_w1, cmp_w2_k.astype(BF16), cmp_w2_v.T.astype(BF16), nsa_k_norm)
    oc, bias = _cmp_select(qt, kc, vct)
    y_nsa = _slc_win(qt, bias, kaug, vt, oc, gt, _score_bound(nsa_q_norm, nsa_k_norm))

    y_hgrn = _hgrn(hg, lower_bound, hgrn_out_norm)

    kh, vht = _mem_kv(mem, mem_norm, mem_w_k.astype(BF16), mem_w_v.T.astype(BF16), mem_k_norm)
    return _out_ffn(x1, y_nsa, y_hgrn, qmt, kh, vht, nsa_out_norm, mem_out_norm, w_out.astype(BF16), *ffn2)


def kernel(x, mem, ffn1_norm, ffn1_w_gate, ffn1_w_up, ffn1_w_down, mix_norm, w_in, w_out, nsa_q_norm, nsa_k_norm, cmp_pos_k, cmp_w1_k, cmp_w2_k, cmp_pos_v, cmp_w1_v, cmp_w2_v, nsa_out_norm, hgrn_lb_logits, hgrn_out_norm, mem_norm, mem_w_k, mem_w_v, mem_q_norm, mem_k_norm, mem_out_norm, ffn2_norm, ffn2_w_gate, ffn2_w_up, ffn2_w_down):
    b, s, d = x.shape
    depth = ffn1_norm.shape[0]
    lower_bounds = jnp.cumsum(jax.nn.softmax(hgrn_lb_logits.astype(F32), axis=0), axis=0)
    bf = lambda a: a.astype(BF16)
    for l in range(depth):
        x = _layer(x, mem, (ffn1_norm[l], bf(ffn1_w_gate[l]), bf(ffn1_w_up[l]), bf(ffn1_w_down[l])),
                   (ffn2_norm[l], bf(ffn2_w_gate[l]), bf(ffn2_w_up[l]), bf(ffn2_w_down[l])),
                   mix_norm[l], w_in[l], w_out[l], nsa_q_norm[l], nsa_k_norm[l],
                   cmp_pos_k[l], cmp_w1_k[l], cmp_w2_k[l], cmp_pos_v[l], cmp_w1_v[l], cmp_w2_v[l],
                   nsa_out_norm[l], lower_bounds[l], hgrn_out_norm[l],
                   mem_norm[l], mem_w_k[l], mem_w_v[l], mem_q_norm[l], mem_k_norm[l], mem_out_norm[l])
    return x
```

```python
import functools

import numpy as np
import jax
import jax.numpy as jnp
from jax import lax
from jax.experimental import pallas as pl
from jax.experimental.pallas import tpu as pltpu

F32 = jnp.float32
BF16 = jnp.bfloat16

HEAD_DIM = 64
ROT_DIM = 16
ROT_HALF = 8
ROPE_THETA = 500000.0
NSA_HEADS = 8
NSA_GROUPS = 2
NSA_HPG = 4
CMP_BLOCK = 32
CMP_STRIDE = 16
SLC_BLOCK = 64
SLC_SHIFT = 6
SLC_TOPK = 16
WINDOW = 512
FORCED_SCORE = 1e4
HGRN_HEADS = 4
HGRN_CHUNK = 64
HGRN_WIDTH = 256
MEM_HEADS = 4
MEM_WIDTH = 256
NSA_WIDTH = 512
EPS = 1e-6
NEG = -1e30
QK_SCALE_LOG2 = HEAD_DIM ** -0.5 * 1.4426950408889634

VMEM_LIMIT = 56 * 1024 * 1024
MAX_BLOCKS = 128
GATE_ROWS = 16
V_ROWS = 80
TOK_TILE = 512
HGRN_CB = 8
HGRN_LEVELS = (32, 16, 8, 4, 2, 1)

NT_DIMS = (((1,), (1,)), ((), ()))
TN_DIMS = (((0,), (0,)), ((), ()))


def _cparams(sem):
    return pltpu.CompilerParams(dimension_semantics=sem, vmem_limit_bytes=VMEM_LIMIT)


def _dot(a, b):
    return jnp.dot(a, b, preferred_element_type=F32)


def _dot_nt(a, b):
    return lax.dot_general(a, b, NT_DIMS, preferred_element_type=F32)


def _dot_tn(a, b):
    return lax.dot_general(a, b, TN_DIMS, preferred_element_type=F32)


def _sigmoid(x):
    return 1.0 / (1.0 + jnp.exp(-x))


def _silu(x):
    return x * _sigmoid(x)


def _split2(x):
    hi = x.astype(BF16)
    lo = (x - hi.astype(F32)).astype(BF16)
    return hi, lo


def _split3(x):
    hi = x.astype(BF16)
    r1 = x - hi.astype(F32)
    mid = r1.astype(BF16)
    lo = (r1 - mid.astype(F32)).astype(BF16)
    return hi, mid, lo


def _rms_rows(x, gain_row):
    ms = jnp.mean(x * x, axis=-1, keepdims=True)
    return x * lax.rsqrt(ms + EPS) * gain_row


def _rms_cols(x, gain_col):
    ms = jnp.mean(x * x, axis=0, keepdims=True)
    return x * lax.rsqrt(ms + EPS) * gain_col


def _seg_mean_sq(x, bd):
    hi, lo = _split2(x * x)
    return _dot(hi, bd) + _dot(lo, bd)


FFN_CHUNK = 256


def _ffn_half_step(x, g_ref, wg_ref, wu_ref, wd_ref, a_scr):
    xn = _rms_rows(x, g_ref[...]).astype(BF16)
    d_ff = wg_ref.shape[1]
    for c in range(d_ff // FFN_CHUNK):
        sl = slice(c * FFN_CHUNK, (c + 1) * FFN_CHUNK)
        g = _dot(xn, wg_ref[:, sl])
        u = _dot(xn, wu_ref[:, sl])
        a_scr[:, sl] = (_silu(g) * u).astype(BF16)
    return x + 0.5 * _dot(a_scr[...], wd_ref[...])


def _resident(shape):
    return pl.BlockSpec(shape, lambda *_: (0,) * len(shape), pipeline_mode=pl.Buffered(1))


def _rope_cols(xn, cos, sin):
    x0, x1 = xn[0:ROT_HALF], xn[ROT_HALF:ROT_DIM]
    return jnp.concatenate([x0 * cos - x1 * sin, x1 * cos + x0 * sin, xn[ROT_DIM:]], axis=0)


def _rope_rows(x, cn, sa, sb):
    return x * cn + pltpu.roll(x, 128 - ROT_HALF, 1) * sa + pltpu.roll(x, ROT_HALF, 1) * sb


def _proj_body(x_ref, fg_ref, wg_ref, wu_ref, wd_ref, mg_ref, wt_ref, wn_ref, qg_ref, kg_ref, mqg_ref,
               cos_ref, sin_ref, cn_ref, sa_ref, sb_ref, bd_ref,
               x1_ref, qt_ref, vt_ref, gt_ref, qmt_ref, kaug_ref, kvc_ref, hg_ref, a_scr):
    tm = x_ref.shape[0]
    assert tm == 8 * SLC_BLOCK
    x1 = _ffn_half_step(x_ref[...], fg_ref, wg_ref, wu_ref, wd_ref, a_scr)
    x1_ref[...] = x1
    h = _rms_rows(x1, mg_ref[...]).astype(BF16)

    pt = _dot_nt(wt_ref[...], h)
    pn = _dot(h, wn_ref[...])
    cos, sin = cos_ref[...], sin_ref[...]
    qg = qg_ref[...]
    for hh in range(NSA_HEADS):
        xq = _rms_cols(pt[hh * 64:(hh + 1) * 64], qg)
        qt_ref[hh * 64:(hh + 1) * 64, :] = (_rope_cols(xq, cos, sin) * QK_SCALE_LOG2).astype(BF16)
    ones_rows = (lax.broadcasted_iota(jnp.int32, (V_ROWS - 64, tm), 0) == 0).astype(BF16)
    for g in range(NSA_GROUPS):
        for br in range(2):
            rows = 512 + br * 128 + g * 64
            vt_ref[g, br, 0:64, :] = pt[rows:rows + 64].astype(BF16)
            vt_ref[g, br, 64:V_ROWS, :] = ones_rows
    gt_ref[...] = _sigmoid(pt[768:800])
    mqg = mqg_ref[...]
    for hh in range(MEM_HEADS):
        xm = _rms_cols(pt[800 + hh * 64:864 + hh * 64], mqg)
        qmt_ref[hh * 64:(hh + 1) * 64, :] = (xm * QK_SCALE_LOG2).astype(BF16)

    cn, sa, sb = cn_ref[...], sa_ref[...], sb_ref[...]
    bd, kg = bd_ref[...], kg_ref[...]
    kvc_ref[0] = _rope_rows(pn[:, 0:128], cn, sa, sb)
    kvc_ref[1] = pn[:, 128:256]
    ks = pn[:, 256:384]
    kw = pn[:, 384:512]
    ks = _rope_rows(ks * lax.rsqrt(_seg_mean_sq(ks, bd) + EPS) * kg, cn, sa, sb)
    kw = _rope_rows(kw * lax.rsqrt(_seg_mean_sq(kw, bd) + EPS) * kg, cn, sa, sb)
    lane = lax.broadcasted_iota(jnp.int32, (tm, 128), 1)
    row = lax.broadcasted_iota(jnp.int32, (tm, 128), 0)
    onehot = jnp.where(lane - 64 == (row >> SLC_SHIFT), 1.0, 0.0)
    lo_half = lane < 64
    kaug_ref[0, 0] = jnp.where(lo_half, ks, onehot).astype(BF16)
    kaug_ref[0, 1] = jnp.where(lo_half, kw, 0.0).astype(BF16)
    kaug_ref[1, 0] = jnp.where(lo_half, pltpu.roll(ks, 64, 1), onehot).astype(BF16)
    kaug_ref[1, 1] = jnp.where(lo_half, pltpu.roll(kw, 64, 1), 0.0).astype(BF16)
    hg_ref[...] = pn[:, 512:1536]


def _ffn_proj(x3d, ffn_gain, wg, wu, wd, mix_gain, wt, wn, q_gain, k_gain, mq_gain, rope):
    b, s, d = x3d.shape
    tm = TOK_TILE
    ns = s // tm
    cos_t, sin_t, cn, sa, sb = rope
    bd = jnp.asarray(np.kron(np.eye(2), np.full((64, 64), 1.0 / 64)), BF16)
    full = _resident
    out_shape = (
        jax.ShapeDtypeStruct((b, s, d), F32),
        jax.ShapeDtypeStruct((b, 512, s), BF16),
        jax.ShapeDtypeStruct((b, 2, 2, ns, V_ROWS, tm), BF16),
        jax.ShapeDtypeStruct((b, 32, s), F32),
        jax.ShapeDtypeStruct((b, 256, s), BF16),
        jax.ShapeDtypeStruct((b, 2, 2, s, 128), BF16),
        jax.ShapeDtypeStruct((b, 2, s, 128), F32),
        jax.ShapeDtypeStruct((b, s, 1024), F32),
    )
    out_specs = (
        pl.BlockSpec((None, tm, d), lambda bi, i: (bi, i, 0)),
        pl.BlockSpec((None, 512, tm), lambda bi, i: (bi, 0, i)),
        pl.BlockSpec((None, 2, 2, None, V_ROWS, tm), lambda bi, i: (bi, 0, 0, i, 0, 0)),
        pl.BlockSpec((None, 32, tm), lambda bi, i: (bi, 0, i)),
        pl.BlockSpec((None, 256, tm), lambda bi, i: (bi, 0, i)),
        pl.BlockSpec((None, 2, 2, tm, 128), lambda bi, i: (bi, 0, 0, i, 0)),
        pl.BlockSpec((None, 2, tm, 128), lambda bi, i: (bi, 0, i, 0)),
        pl.BlockSpec((None, tm, 1024), lambda bi, i: (bi, i, 0)),
    )
    in_specs = [
        pl.BlockSpec((None, tm, d), lambda bi, i: (bi, i, 0)),
        full((1, d)), full(wg.shape), full(wu.shape), full(wd.shape),
        full((1, d)), full(wt.shape), full(wn.shape),
        full((64, 1)), full((1, 128)), full((64, 1)),
        pl.BlockSpec((ROT_HALF, tm), lambda bi, i: (0, i)),
        pl.BlockSpec((ROT_HALF, tm), lambda bi, i: (0, i)),
        pl.BlockSpec((tm, 128), lambda bi, i: (i, 0)),
        pl.BlockSpec((tm, 128), lambda bi, i: (i, 0)),
        pl.BlockSpec((tm, 128), lambda bi, i: (i, 0)),
        full((128, 128)),
    ]
    return pl.pallas_call(
        _proj_body, grid=(b, ns), in_specs=in_specs, out_specs=out_specs, out_shape=out_shape,
        scratch_shapes=[pltpu.VMEM((tm, wg.shape[1]), BF16)],
        compiler_params=_cparams(("parallel", "parallel")), name="ffn1_mix_projection",
    )(x3d, ffn_gain.reshape(1, d), wg, wu, wd, mix_gain.reshape(1, d), wt, wn, q_gain.reshape(64, 1),
      jnp.tile(k_gain.reshape(1, 64), (1, 2)), mq_gain.reshape(64, 1), cos_t, sin_t, cn, sa, sb, bd)


def _rope_tables(s):
    pos = jnp.arange(s, dtype=F32)
    inv = ROPE_THETA ** (-(jnp.arange(0, ROT_DIM, 2, dtype=F32) / ROT_DIM))
    ang = pos[:, None] * inv[None, :]
    cos, sin = jnp.cos(ang), jnp.sin(ang)
    zeros = jnp.zeros((s, 64 - ROT_DIM), F32)
    cn = jnp.concatenate([cos, cos, jnp.ones((s, 64 - ROT_DIM), F32)], axis=1)
    sa = jnp.concatenate([-sin, jnp.zeros((s, ROT_HALF), F32), zeros], axis=1)
    sb = jnp.concatenate([jnp.zeros((s, ROT_HALF), F32), sin, zeros], axis=1)
    tile2 = lambda a: jnp.concatenate([a, a], axis=1)
    return cos.T, sin.T, tile2(cn), tile2(sa), tile2(sb)


def _cmp_body(kvc_ref, pos_ref, w1_ref, w2k_ref, w2vt_ref, kg_ref, kc_ref, vct_ref):
    nc = kvc_ref.shape[1] // CMP_STRIDE
    for kind in range(2):
        halves = []
        for part in range(2):
            x = jnp.concatenate(
                [(kvc_ref[kind, pl.ds(r, nc, stride=CMP_STRIDE), :]
                  + pos_ref[kind, part, :, r * 128:(r + 1) * 128]).astype(BF16) for r in range(CMP_STRIDE)],
                axis=1)
            halves.append(x)
        for g in range(NSA_GROUPS):
            second = _dot(halves[1], w1_ref[kind, g, 1])
            hid = _silu(_dot(halves[0], w1_ref[kind, g, 0]) + pltpu.roll(second, nc - 1, 0)).astype(BF16)
            if kind == 0:
                kc_ref[g] = _rms_rows(_dot(hid, w2k_ref[...]), kg_ref[...]).astype(BF16)
            else:
                vct_ref[g] = _dot_nt(w2vt_ref[...], hid).astype(BF16)


def _compress(kvc, pos, w1, w2k, w2vt, k_gain):
    b, _, s, _ = kvc.shape
    nc = s // CMP_STRIDE
    return pl.pallas_call(
        _cmp_body, grid=(b,),
        in_specs=[pl.BlockSpec((None, 2, s, 128), lambda bi: (bi, 0, 0, 0)),
                  _resident(pos.shape), _resident(w1.shape), _resident(w2k.shape), _resident(w2vt.shape),
                  _resident((1, 64))],
        out_specs=(pl.BlockSpec((None, 2, nc, 64), lambda bi: (bi, 0, 0, 0)),
                   pl.BlockSpec((None, 2, 64, nc), lambda bi: (bi, 0, 0, 0))),
        out_shape=(jax.ShapeDtypeStruct((b, 2, nc, 64), BF16), jax.ShapeDtypeStruct((b, 2, 64, nc), BF16)),
        compiler_params=_cparams(("parallel",)), name="nsa_compress",
    )(kvc, pos, w1, w2k, w2vt, k_gain.reshape(1, 64))


def _compress_weights(pos_k, pos_v, w1_k, w1_v):
    def pos_part(p):
        p = p.reshape(2, CMP_STRIDE, 1, 64)
        return jnp.broadcast_to(p, (2, CMP_STRIDE, NSA_GROUPS, 64)).reshape(2, 1, CMP_STRIDE * 128)

    def w1_part(w):
        hdim = w.shape[1]
        w = w.reshape(2, CMP_STRIDE, 1, 64, hdim)
        per_group = []
        for g in range(NSA_GROUPS):
            pads = [w if gg == g else jnp.zeros_like(w) for gg in range(NSA_GROUPS)]
            per_group.append(jnp.concatenate(pads, axis=2).reshape(2, CMP_STRIDE * 128, hdim))
        return jnp.stack(per_group)

    pos = jnp.stack([pos_part(pos_k), pos_part(pos_v)])
    w1 = jnp.stack([w1_part(w1_k), w1_part(w1_v)]).astype(BF16)
    return pos, w1


CMP_CLASS_ROWS = 128


def _cmpsel_variant(nc, nblk, fixed_reference, m0_ref, qt_ref, kc_ref, vct_ref, oc_ref, bias_ref, s_scr, flag_scr):
    tq = qt_ref.shape[1]
    t0 = pl.program_id(2) * tq
    n_idx = lax.broadcasted_iota(jnp.int32, (nc, tq), 0)
    t_idx = t0 + lax.broadcasted_iota(jnp.int32, (nc, tq), 1)
    mask_bias = jnp.where(n_idx * CMP_STRIDE + (CMP_BLOCK - 1) <= t_idx, 0.0, NEG)
    sees_any = t0 + lax.broadcasted_iota(jnp.int32, (1, tq), 1) >= CMP_BLOCK - 1
    kc = kc_ref[0:nc, :]
    if fixed_reference:
        mask_bias = mask_bias - m0_ref[0]
    else:
        for hh in range(NSA_HPG):
            s_scr[hh, 0:nc, :] = _dot(kc, qt_ref[hh * 64:(hh + 1) * 64, :]) + mask_bias
    jj = lax.broadcasted_iota(jnp.int32, (nblk, nc), 0)
    nn = lax.broadcasted_iota(jnp.int32, (nblk, nc), 1)
    ov = jnp.where((nn * CMP_STRIDE < jj * SLC_BLOCK + SLC_BLOCK)
                   & (nn * CMP_STRIDE + CMP_BLOCK > jj * SLC_BLOCK), 1.0, 0.0).astype(BF16)
    ones_rows = (lax.broadcasted_iota(jnp.int32, (V_ROWS - 64, nc), 0) == 0).astype(BF16)
    lhs = jnp.concatenate([vct_ref[:, 0:nc], ones_rows, ov], axis=0)
    imp = jnp.zeros((nblk, tq), F32)
    l_min = jnp.full((1, tq), 1.0, F32)

    def probabilities(hh):
        if fixed_reference:
            return jnp.exp2(_dot(kc, qt_ref[hh * 64:(hh + 1) * 64, :]) + mask_bias).astype(BF16)
        m = jnp.max(s_scr[hh, 0:nc, :], axis=0, keepdims=True)
        return jnp.exp2(s_scr[hh, 0:nc, :] - m).astype(BF16)

    p_next = probabilities(0)
    for hh in range(NSA_HPG):
        p = p_next
        if hh + 1 < NSA_HPG:
            p_next = probabilities(hh + 1)
        r = _dot(lhs, p)
        l_min = jnp.minimum(l_min, jnp.where(sees_any, r[64:65], 1.0))
        inv_l = jnp.where(sees_any, 1.0 / r[64:65], 0.0)
        oc_ref[hh * 64:(hh + 1) * 64, :] = r[0:64] * inv_l
        imp = imp + r[V_ROWS:] * inv_l
    if fixed_reference:
        flag_scr[0] = jnp.where(jnp.min(l_min) > MIN_DENOMINATOR, 0, 1)

    j = lax.broadcasted_iota(jnp.int32, (nblk, tq), 0)
    cur = (t0 + lax.broadcasted_iota(jnp.int32, (nblk, tq), 1)) >> SLC_SHIFT
    forced = (j == 0) | (j == cur) | (j == cur - 1)
    bias = jnp.where(forced & (j <= cur), 0.0, NEG)
    imp = jnp.where((j <= cur) & jnp.logical_not(forced), imp, -1.0)
    jf = j.astype(F32)
    for _ in range(SLC_TOPK - 3):
        v = jnp.max(imp, axis=0, keepdims=True)
        first = jnp.min(jnp.where(imp == v, jf, float(nblk)), axis=0, keepdims=True)
        pick = jf == first
        bias = jnp.where(pick & (v >= 0.0), 0.0, bias)
        imp = jnp.where(pick, -3e38, imp)
    bias_ref[0:nblk, :] = bias
    if nblk < bias_ref.shape[0]:
        bias_ref[nblk:, :] = jnp.full((bias_ref.shape[0] - nblk, tq), NEG, F32)


def _cmpsel_body(m0_ref, qt_ref, kc_ref, vct_ref, oc_ref, bias_ref, s_scr, flag_scr):
    tq = qt_ref.shape[1]
    nc_total = kc_ref.shape[0]
    tiles_per_class = CMP_CLASS_ROWS // (tq // CMP_STRIDE)
    cls = pl.program_id(2) // tiles_per_class
    refs = (m0_ref, qt_ref, kc_ref, vct_ref, oc_ref, bias_ref, s_scr, flag_scr)
    for c in range(nc_total // CMP_CLASS_ROWS):
        nc = (c + 1) * CMP_CLASS_ROWS
        nblk = min(nc * CMP_STRIDE // SLC_BLOCK, bias_ref.shape[0])
        pl.when(cls == c)(functools.partial(_cmpsel_variant, nc, nblk, True, *refs))
    pl.when(flag_scr[0] != 0)(functools.partial(_cmpsel_variant, nc_total, bias_ref.shape[0], False, *refs))


def _cmp_select(qt, kc, vct, score_bound, *, tq=TOK_TILE):
    b, _, s = qt.shape
    nc = kc.shape[2]
    nblk = MAX_BLOCKS
    assert s // SLC_BLOCK <= MAX_BLOCKS and s // SLC_BLOCK >= SLC_TOPK and nc % CMP_CLASS_ROWS == 0
    return pl.pallas_call(
        _cmpsel_body, grid=(b, NSA_GROUPS, s // tq),
        in_specs=[pl.BlockSpec(memory_space=pltpu.SMEM),
                  pl.BlockSpec((None, 256, tq), lambda bi, g, i: (bi, g, i)),
                  pl.BlockSpec((None, None, nc, 64), lambda bi, g, i: (bi, g, 0, 0)),
                  pl.BlockSpec((None, None, 64, nc), lambda bi, g, i: (bi, g, 0, 0))],
        out_specs=(pl.BlockSpec((None, 256, tq), lambda bi, g, i: (bi, g, i)),
                   pl.BlockSpec((None, None, nblk, tq), lambda bi, g, i: (bi, g, 0, i))),
        out_shape=(jax.ShapeDtypeStruct((b, 512, s), F32),
                   jax.ShapeDtypeStruct((b, NSA_GROUPS, nblk, s), F32)),
        scratch_shapes=[pltpu.VMEM((NSA_HPG, nc, tq), F32), pltpu.SMEM((1,), jnp.int32)],
        compiler_params=_cparams(("parallel", "parallel", "parallel")), name="nsa_compressed_select",
    )(score_bound, qt, kc, vct)


def _flash_step(s_ref, vt, m_ref, acc_ref):
    m_old = m_ref[...]
    m_new = jnp.maximum(m_old, jnp.max(s_ref[...], axis=0, keepdims=True))
    p = jnp.exp2(s_ref[...] - m_new)
    acc_ref[...] = jnp.exp2(m_old - m_new) * acc_ref[...] + _dot(vt, p.astype(BF16))
    m_ref[...] = m_new


SEL, WIN = 0, 1
BIAS_ROWS = 16


MIN_DENOMINATOR = 2.0 ** -64


def _slcwin_body(m0_ref, qt_ref, bias_ref, kaug_ref, vt_ref, oc_ref, gt_ref, mb_ref, y_ref,
                 q_scr, m_scr, acc_scr, s_scr, p_scr):
    tq = qt_ref.shape[1]
    tk = vt_ref.shape[3]
    assert tq == tk and WINDOW == tk and tk == 8 * SLC_BLOCK
    diag = pl.program_id(2)
    m0 = m0_ref[0]

    qs, qs2, qw = 0, 1, 2
    zeros = jnp.zeros((64, tq), BF16)
    for slot in (qs, qs2, qw):
        for hh in range(NSA_HPG):
            q_scr[slot, hh, 0:64, :] = qt_ref[hh * 64:(hh + 1) * 64, :]
            q_scr[slot, hh, 64:128, :] = zeros

    def set_selection_bias(kt, slot=qs):
        rows = bias_ref[pl.ds(pl.multiple_of(kt * 8, 8), 8), :]
        b16 = jnp.concatenate([rows, jnp.zeros_like(rows)], axis=0).astype(BF16)
        for hh in range(NSA_HPG):
            q_scr[slot, hh, 64:64 + BIAS_ROWS, :] = b16

    def tiles_fixed_reference(tiles):
        chains = [(br, qslot, kt, mask_bias, hh) for br, qslot, kt, mask_bias in tiles for hh in range(NSA_HPG)]
        for c in range(len(chains) + 1):
            if c < len(chains):
                br, qslot, kt, mask_bias, hh = chains[c]
                s = _dot(kaug_ref[br, pl.ds(pl.multiple_of(kt * tk, tk), tk), :], q_scr[qslot, hh])
                if mask_bias is not None:
                    s = s + mask_bias()
                p_scr[c % 4] = jnp.exp2(s - m0).astype(BF16)
            if c >= 1:
                br, _, kt, _, hh = chains[c - 1]
                acc_scr[br, hh] = acc_scr[br, hh] + _dot(vt_ref[br, kt], p_scr[(c - 1) % 4])

    def tile_running_max(br, qslot, kt, mask_bias=None):
        k = kaug_ref[br, pl.ds(pl.multiple_of(kt * tk, tk), tk), :]
        for hh in range(NSA_HPG):
            s = _dot(k, q_scr[qslot, hh])
            s_scr[hh] = s if mask_bias is None else s + mask_bias()
        for hh in range(NSA_HPG):
            _flash_step(s_scr.at[hh], vt_ref[br, kt], m_scr.at[br, hh], acc_scr.at[br, hh])

    prev = jnp.maximum(diag - 1, 0)
    no_prev = jnp.where(diag == 0, NEG, 0.0)
    band_bias = lambda: mb_ref[1] + no_prev
    causal_bias = lambda: mb_ref[0]

    acc_scr[...] = jnp.zeros(acc_scr.shape, F32)

    def tile_pair(j, carry):
        kt = 2 * j
        set_selection_bias(kt, qs)
        set_selection_bias(kt + 1, qs2)
        tiles_fixed_reference([(SEL, qs, kt, None), (SEL, qs2, kt + 1, None)])
        return carry

    lax.fori_loop(0, diag >> 1, tile_pair, 0)

    @pl.when((diag & 1) == 1)
    def _():
        set_selection_bias(diag - 1, qs)
        tiles_fixed_reference([(SEL, qs, diag - 1, None)])

    set_selection_bias(diag, qs)
    tiles_fixed_reference([(WIN, qw, prev, band_bias), (SEL, qs, diag, causal_bias), (WIN, qw, diag, causal_bias)])

    denominators = acc_scr[:, :, 64:65, :]
    underflow = jnp.logical_not(jnp.min(denominators) > MIN_DENOMINATOR)

    @pl.when(underflow)
    def _():
        m_scr[...] = jnp.full(m_scr.shape, NEG, F32)
        acc_scr[...] = jnp.zeros(acc_scr.shape, F32)

        def full_tile(kt, carry):
            set_selection_bias(kt, qs)
            tile_running_max(SEL, qs, kt)
            return carry

        lax.fori_loop(0, diag, full_tile, 0)
        set_selection_bias(diag, qs)
        tile_running_max(WIN, qw, prev, band_bias)
        tile_running_max(SEL, qs, diag, causal_bias)
        tile_running_max(WIN, qw, diag, causal_bias)

    gt = gt_ref[...]
    for hh in range(NSA_HPG):
        o_s = acc_scr[SEL, hh, 0:64, :] * (1.0 / acc_scr[SEL, hh, 64:65, :])
        o_w = acc_scr[WIN, hh, 0:64, :] * (1.0 / acc_scr[WIN, hh, 64:65, :])
        y_ref[hh * 64:(hh + 1) * 64, :] = (gt[3 * hh:3 * hh + 1] * oc_ref[hh * 64:(hh + 1) * 64, :]
                                          + gt[3 * hh + 1:3 * hh + 2] * o_s
                                          + gt[3 * hh + 2:3 * hh + 3] * o_w)


def _score_bound(q_gain, k_gain):
    bound = HEAD_DIM * QK_SCALE_LOG2 * jnp.max(jnp.abs(q_gain)) * jnp.max(jnp.abs(k_gain))
    return (1.02 * bound).reshape(1).astype(F32)


def _slc_win(qt, bias, kaug, vt, oc, gt, score_bound):
    b, _, s = qt.shape
    nblk = bias.shape[2]
    ns, tk = vt.shape[3], vt.shape[5]
    tq = tk
    key_rel, t_rel = np.arange(tk)[:, None], np.arange(tq)[None, :]
    mask_bias = jnp.asarray(np.stack([np.where(key_rel <= t_rel, 0.0, NEG),
                                      np.where(t_rel + tk - key_rel < WINDOW, 0.0, NEG)]), F32)
    qblk = pl.BlockSpec((None, 256, tq), lambda bi, g, i: (bi, g, i))
    return pl.pallas_call(
        _slcwin_body, grid=(b, NSA_GROUPS, s // tq),
        in_specs=[pl.BlockSpec(memory_space=pltpu.SMEM), qblk,
                  pl.BlockSpec((None, None, nblk, tq), lambda bi, g, i: (bi, g, 0, i)),
                  pl.BlockSpec((None, None, 2, s, 128), lambda bi, g, i: (bi, g, 0, 0, 0)),
                  pl.BlockSpec((None, None, 2, ns, V_ROWS, tk), lambda bi, g, i: (bi, g, 0, 0, 0, 0)),
                  qblk,
                  pl.BlockSpec((None, None, GATE_ROWS, tq), lambda bi, g, i: (bi, g, 0, i)),
                  _resident((2, tk, tq))],
        out_specs=qblk,
        out_shape=jax.ShapeDtypeStruct((b, 512, s), F32),
        scratch_shapes=[pltpu.VMEM((3, NSA_HPG, 128, tq), BF16),
                        pltpu.VMEM((2, NSA_HPG, 1, tq), F32), pltpu.VMEM((2, NSA_HPG, V_ROWS, tq), F32),
                        pltpu.VMEM((NSA_HPG, tk, tq), F32), pltpu.VMEM((NSA_HPG, tk, tq), BF16)],
        compiler_params=_cparams(("parallel", "parallel", "arbitrary")), name="nsa_selected_window",
    )(score_bound, qt, bias, kaug, vt, oc, gt.reshape(b, NSA_GROUPS, GATE_ROWS, s), mask_bias)


def _hgrn_consts():
    c = HGRN_CHUNK
    t = np.arange(c)
    lower = (t[None, :] <= t[:, None]).astype(np.float32)
    rows = [lower]
    masks = []
    for half in HGRN_LEVELS:
        mid = (t // (2 * half)) * (2 * half) + half - 1
        if half < 8:
            rows.append(lower[mid])
        same = (t[:, None] // (2 * half)) == (t[None, :] // (2 * half))
        right = (t[:, None] & half) != 0
        left = (t[None, :] & half) == 0
        masks.append((same & right & left).astype(np.float32))
    masks.append(np.eye(c, dtype=np.float32))
    mall = np.concatenate(rows, axis=0)
    lvl = np.stack([np.tile(mk.T, (1, HGRN_HEADS)) for mk in masks])
    bdm = np.kron(np.eye(HGRN_HEADS), np.ones((64, 64), np.float32))
    return jnp.asarray(mall, BF16), jnp.asarray(lvl, F32), jnp.asarray(bdm, F32), jnp.asarray(bdm / 64, BF16)


def _hgrn_body(hg_ref, lb_ref, og_ref, mall_ref, lvl_ref, bdm_ref, bdn_ref, y_ref, st_scr):
    c = HGRN_CHUNK
    w = HGRN_WIDTH

    @pl.when(pl.program_id(1) == 0)
    def _():
        st_scr[...] = jnp.zeros(st_scr.shape, F32)

    chunks = range(hg_ref.shape[0] // c)
    lb = lb_ref[...]
    lane = lax.broadcasted_iota(jnp.int32, (c, w), 1)
    head_masks = [(lane >> 6) == hh for hh in range(HGRN_HEADS)]
    nlev = len(HGRN_LEVELS)

    def stack_heads(x):
        x16 = x.astype(BF16)
        return jnp.concatenate([jnp.where(hm, x16, 0) for hm in head_masks], axis=0)

    def row_bcast(x, half):
        return jnp.concatenate([jnp.broadcast_to(x[p + half - 1:p + half, :], (2 * half, w))
                                for p in range(0, c, 2 * half)], axis=0)

    qa, kk, v16, logf = [], [], [], []
    for ci in chunks:
        rows = slice(ci * c, (ci + 1) * c)
        qa.append(_silu(hg_ref[rows, 0:w]) * (HEAD_DIM ** -0.5))
        fg = lb + (1.0 - lb) * _sigmoid(hg_ref[rows, w:2 * w])
        kk.append(1.0 - fg)
        logf.append(jnp.log(fg))
        v16.append(hg_ref[rows, 2 * w:3 * w].astype(BF16))

    mall = mall_ref[...]
    r_all = sum(_dot(mall, part) for part in _split3(jnp.concatenate(logf, axis=1)))
    bcum = [r_all[0:c, ci * w:(ci + 1) * w] for ci in chunks]

    attn = [lvl_ref[nlev] * _dot_nt(kk[ci].astype(BF16), stack_heads(qa[ci])) for ci in chunks]
    fine = 0
    for li, half in enumerate(HGRN_LEVELS):
        for ci in chunks:
            if half >= 8:
                ref_pt = row_bcast(bcum[ci], half)
            else:
                ref_pt = r_all[(fine + 1) * c:(fine + 2) * c, ci * w:(ci + 1) * w]
            e = jnp.exp(-jnp.abs(bcum[ci] - ref_pt))
            kt = (kk[ci] * e).astype(BF16)
            attn[ci] = attn[ci] + lvl_ref[li] * _dot_nt(kt, stack_heads(qa[ci] * e))
        if half < 8:
            fine += 1

    intra, upd, decay, qb = [], [], [], []
    for ci in chunks:
        x = _dot_tn(attn[ci].astype(BF16), v16[ci])
        intra.append(sum(jnp.where(head_masks[hh], x[hh * c:(hh + 1) * c], 0.0) for hh in range(HGRN_HEADS)))
        b_last = bcum[ci][c - 1:c, :]
        kl = (kk[ci] * jnp.exp(b_last - bcum[ci])).astype(BF16)
        upd.append(bdm_ref[...] * _dot_tn(v16[ci], kl))
        decay.append(jnp.exp(b_last))
        qb.append((qa[ci] * jnp.exp(bcum[ci])).astype(BF16))

    st = st_scr[...]
    inter = []
    for ci in chunks:
        inter.append(_dot_nt(qb[ci], st.astype(BF16)))
        st = st * decay[ci] + upd[ci]
    st_scr[...] = st

    for ci in chunks:
        rows = slice(ci * c, (ci + 1) * c)
        o = inter[ci] + intra[ci]
        hi, lo = _split2(o * o)
        ms = _dot(hi, bdn_ref[...]) + _dot(lo, bdn_ref[...])
        y_ref[rows, :] = o * lax.rsqrt(ms + EPS) * og_ref[...] * _silu(hg_ref[rows, 3 * w:4 * w])


def _hgrn(hg, lower_bound, out_gain):
    b, s, _ = hg.shape
    rows = HGRN_CB * HGRN_CHUNK
    mall, lvl, bdm, bdn = _hgrn_consts()
    full = lambda shape: pl.BlockSpec(shape, lambda bi, i: (0,) * len(shape))
    return pl.pallas_call(
        _hgrn_body, grid=(b, s // rows),
        in_specs=[pl.BlockSpec((None, rows, 4 * HGRN_WIDTH), lambda bi, i: (bi, i, 0)),
                  full((1, HGRN_WIDTH)), full((1, HGRN_WIDTH)),
                  full(mall.shape), full(lvl.shape), full(bdm.shape), full(bdn.shape)],
        out_specs=pl.BlockSpec((None, rows, HGRN_WIDTH), lambda bi, i: (bi, i, 0)),
        out_shape=jax.ShapeDtypeStruct((b, s, HGRN_WIDTH), F32),
        scratch_shapes=[pltpu.VMEM((HGRN_WIDTH, HGRN_WIDTH), F32)],
        compiler_params=_cparams(("parallel", "arbitrary")), name="hgrn2_chunks",
    )(hg, lower_bound.reshape(1, -1), out_gain.reshape(1, -1), mall, lvl, bdm, bdn)


def _memkv_body(mem_ref, mg_ref, wk_ref, wvt_ref, kg_ref, kh_ref, vht_ref):
    m = mem_ref.shape[0]
    mn = _rms_rows(mem_ref[...], mg_ref[...]).astype(BF16)
    k = _dot(mn, wk_ref[...])
    vt = _dot_nt(wvt_ref[...], mn)
    ones_rows = (lax.broadcasted_iota(jnp.int32, (V_ROWS - 64, m), 0) == 0).astype(BF16)
    for hh in range(MEM_HEADS):
        kh_ref[hh] = _rms_rows(k[:, hh * 64:(hh + 1) * 64], kg_ref[...]).astype(BF16)
        vht_ref[hh, 0:64, :] = vt[hh * 64:(hh + 1) * 64].astype(BF16)
        vht_ref[hh, 64:V_ROWS, :] = ones_rows


def _mem_kv(mem, mem_gain, wk, wvt, k_gain):
    b, m, d = mem.shape
    full = lambda shape: pl.BlockSpec(shape, lambda bi: (0,) * len(shape))
    return pl.pallas_call(
        _memkv_body, grid=(b,),
        in_specs=[pl.BlockSpec((None, m, d), lambda bi: (bi, 0, 0)), full((1, d)),
                  full(wk.shape), full(wvt.shape), full((1, 64))],
        out_specs=(pl.BlockSpec((None, MEM_HEADS, m, 64), lambda bi: (bi, 0, 0, 0)),
                   pl.BlockSpec((None, MEM_HEADS, V_ROWS, m), lambda bi: (bi, 0, 0, 0))),
        out_shape=(jax.ShapeDtypeStruct((b, MEM_HEADS, m, 64), BF16),
                   jax.ShapeDtypeStruct((b, MEM_HEADS, V_ROWS, m), BF16)),
        compiler_params=_cparams(("parallel",)), name="memory_kv",
    )(mem, mem_gain.reshape(1, d), wk, wvt, k_gain.reshape(1, 64))


def _out_body(x_ref, ynt_ref, yh_ref, qmt_ref, kh_ref, vht_ref, ng_ref, mg_ref, wo_ref,
              fg_ref, wg_ref, wu_ref, wd_ref, o_ref, a_scr):
    y_mem = []
    for hh in range(MEM_HEADS):
        s = _dot(kh_ref[hh], qmt_ref[hh * 64:(hh + 1) * 64, :])
        p = jnp.exp2(s - jnp.max(s, axis=0, keepdims=True))
        o = _dot(vht_ref[hh], p.astype(BF16))
        y_mem.append(o[0:64] * (1.0 / o[64:65]))
    mem = _rms_cols(jnp.concatenate(y_mem, axis=0), mg_ref[...]).astype(BF16)
    nsa = _rms_cols(ynt_ref[...], ng_ref[...]).astype(BF16)
    acc = _dot_tn(nsa, wo_ref[0:NSA_WIDTH, :])
    acc = acc + _dot(yh_ref[...].astype(BF16), wo_ref[NSA_WIDTH:NSA_WIDTH + HGRN_WIDTH, :])
    acc = acc + _dot_tn(mem, wo_ref[NSA_WIDTH + HGRN_WIDTH:, :])
    o_ref[...] = _ffn_half_step(x_ref[...] + acc, fg_ref, wg_ref, wu_ref, wd_ref, a_scr)


def _out_ffn(x3d, ynt, yh, qmt, kh, vht, nsa_gain, mem_gain, wo, ffn_gain, wg, wu, wd, *, tm=TOK_TILE):
    b, s, d = x3d.shape
    m = kh.shape[2]
    return pl.pallas_call(
        _out_body, grid=(b, s // tm),
        in_specs=[pl.BlockSpec((None, tm, d), lambda bi, i: (bi, i, 0)),
                  pl.BlockSpec((None, NSA_WIDTH, tm), lambda bi, i: (bi, 0, i)),
                  pl.BlockSpec((None, tm, HGRN_WIDTH), lambda bi, i: (bi, i, 0)),
                  pl.BlockSpec((None, MEM_WIDTH, tm), lambda bi, i: (bi, 0, i)),
                  pl.BlockSpec((None, MEM_HEADS, m, 64), lambda bi, i: (bi, 0, 0, 0)),
                  pl.BlockSpec((None, MEM_HEADS, V_ROWS, m), lambda bi, i: (bi, 0, 0, 0)),
                  _resident((NSA_WIDTH, 1)), _resident((MEM_WIDTH, 1)), _resident(wo.shape),
                  _resident((1, d)), _resident(wg.shape), _resident(wu.shape), _resident(wd.shape)],
        out_specs=pl.BlockSpec((None, tm, d), lambda bi, i: (bi, i, 0)),
        out_shape=jax.ShapeDtypeStruct((b, s, d), F32),
        scratch_shapes=[pltpu.VMEM((tm, wg.shape[1]), BF16)],
        compiler_params=_cparams(("parallel", "parallel")), name="mix_out_ffn2",
    )(x3d, ynt, yh, qmt, kh, vht, nsa_gain.reshape(-1, 1), mem_gain.reshape(-1, 1), wo,
      ffn_gain.reshape(1, d), wg, wu, wd)


def _layer(x, mem, ffn1, ffn2, mix_norm, w_in, w_out, nsa_q_norm, nsa_k_norm, cmp_pos_k, cmp_w1_k, cmp_w2_k,
           cmp_pos_v, cmp_w1_v, cmp_w2_v, nsa_out_norm, lower_bound, hgrn_out_norm,
           mem_norm, mem_w_k, mem_w_v, mem_q_norm, mem_k_norm, mem_out_norm):
    b, s, d = x.shape
    sizes = (512, 128, 128, 128, 128, 128, 128, 24, 256, 256, 256, 256, 256)
    offs = np.concatenate([[0], np.cumsum(sizes)])
    col = lambda i: w_in[:, offs[i]:offs[i + 1]]
    (q_a, k_c, v_c, k_s, v_s, k_w, v_w, g_a, q_h, f_h, i_h, g_h, q_m) = [col(i) for i in range(13)]
    gpad = jnp.zeros((d, GATE_ROWS - 3 * NSA_HPG), w_in.dtype)
    wt = jnp.concatenate([q_a, v_s, v_w, g_a[:, :3 * NSA_HPG], gpad, g_a[:, 3 * NSA_HPG:], gpad, q_m],
                         axis=1).T.astype(BF16)
    wn = jnp.concatenate([k_c, v_c, k_s, k_w, q_h, f_h, i_h, g_h], axis=1).astype(BF16)

    x1, qt, vt, gt, qmt, kaug, kvc, hg = _ffn_proj(
        x, *ffn1, mix_norm, wt, wn, nsa_q_norm, nsa_k_norm, mem_q_norm, _rope_tables(s))

    cmp_pos, cmp_w1 = _compress_weights(cmp_pos_k, cmp_pos_v, cmp_w1_k, cmp_w1_v)
    kc, vct = _compress(kvc, cmp_pos, cmp_w1, cmp_w2_k.astype(BF16), cmp_w2_v.T.astype(BF16), nsa_k_norm)
    score_bound = _score_bound(nsa_q_norm, nsa_k_norm)
    oc, bias = _cmp_select(qt, kc, vct, score_bound)
    y_nsa = _slc_win(qt, bias, kaug, vt, oc, gt, score_bound)

    y_hgrn = _hgrn(hg, lower_bound, hgrn_out_norm)

    kh, vht = _mem_kv(mem, mem_norm, mem_w_k.astype(BF16), mem_w_v.T.astype(BF16), mem_k_norm)
    return _out_ffn(x1, y_nsa, y_hgrn, qmt, kh, vht, nsa_out_norm, mem_out_norm, w_out.astype(BF16), *ffn2)


def kernel(x, mem, ffn1_norm, ffn1_w_gate, ffn1_w_up, ffn1_w_down, mix_norm, w_in, w_out, nsa_q_norm, nsa_k_norm, cmp_pos_k, cmp_w1_k, cmp_w2_k, cmp_pos_v, cmp_w1_v, cmp_w2_v, nsa_out_norm, hgrn_lb_logits, hgrn_out_norm, mem_norm, mem_w_k, mem_w_v, mem_q_norm, mem_k_norm, mem_out_norm, ffn2_norm, ffn2_w_gate, ffn2_w_up, ffn2_w_down):
    b, s, d = x.shape
    depth = ffn1_norm.shape[0]
    lower_bounds = jnp.cumsum(jax.nn.softmax(hgrn_lb_logits.astype(F32), axis=0), axis=0)
    bf = lambda a: a.astype(BF16)
    for l in range(depth):
        x = _layer(x, mem, (ffn1_norm[l], bf(ffn1_w_gate[l]), bf(ffn1_w_up[l]), bf(ffn1_w_down[l])),
                   (ffn2_norm[l], bf(ffn2_w_gate[l]), bf(ffn2_w_up[l]), bf(ffn2_w_down[l])),
                   mix_norm[l], w_in[l], w_out[l], nsa_q_norm[l], nsa_k_norm[l],
                   cmp_pos_k[l], cmp_w1_k[l], cmp_w2_k[l], cmp_pos_v[l], cmp_w1_v[l], cmp_w2_v[l],
                   nsa_out_norm[l], lower_bounds[l], hgrn_out_norm[l],
                   mem_norm[l], mem_w_k[l], mem_w_v[l], mem_q_norm[l], mem_k_norm[l], mem_out_norm[l])
    return x
```

```python
import functools

import numpy as np
import jax
import jax.numpy as jnp
from jax import lax
from jax.experimental import pallas as pl
from jax.experimental.pallas import tpu as pltpu

F32 = jnp.float32
BF16 = jnp.bfloat16

HEAD_DIM = 64
ROT_DIM = 16
ROT_HALF = 8
ROPE_THETA = 500000.0
NSA_HEADS = 8
NSA_GROUPS = 2
NSA_HPG = 4
CMP_BLOCK = 32
CMP_STRIDE = 16
SLC_BLOCK = 64
SLC_SHIFT = 6
SLC_TOPK = 16
WINDOW = 512
FORCED_SCORE = 1e4
HGRN_HEADS = 4
HGRN_CHUNK = 64
HGRN_WIDTH = 256
MEM_HEADS = 4
MEM_WIDTH = 256
NSA_WIDTH = 512
EPS = 1e-6
NEG = -1e30
QK_SCALE_LOG2 = HEAD_DIM ** -0.5 * 1.4426950408889634
MIN_DENOMINATOR = 2.0 ** -64

VMEM_LIMIT = 56 * 1024 * 1024
MAX_BLOCKS = 128
GATE_ROWS = 16
V_ROWS = 80
TOK_TILE = 512
HGRN_CB = 8
HGRN_LEVELS = (32, 16, 8, 4, 2, 1)

NT_DIMS = (((1,), (1,)), ((), ()))
TN_DIMS = (((0,), (0,)), ((), ()))


def _cparams(sem):
    return pltpu.CompilerParams(dimension_semantics=sem, vmem_limit_bytes=VMEM_LIMIT)


def _dot(a, b):
    return jnp.dot(a, b, preferred_element_type=F32)


def _dot_nt(a, b):
    return lax.dot_general(a, b, NT_DIMS, preferred_element_type=F32)


def _dot_tn(a, b):
    return lax.dot_general(a, b, TN_DIMS, preferred_element_type=F32)


def _sigmoid(x):
    return 1.0 / (1.0 + jnp.exp(-x))


def _silu(x):
    return x * _sigmoid(x)


def _split2(x):
    hi = x.astype(BF16)
    lo = (x - hi.astype(F32)).astype(BF16)
    return hi, lo


def _split3(x):
    hi = x.astype(BF16)
    r1 = x - hi.astype(F32)
    mid = r1.astype(BF16)
    lo = (r1 - mid.astype(F32)).astype(BF16)
    return hi, mid, lo


def _rms_rows(x, gain_row):
    ms = jnp.mean(x * x, axis=-1, keepdims=True)
    return x * lax.rsqrt(ms + EPS) * gain_row


def _rms_cols(x, gain_col):
    ms = jnp.mean(x * x, axis=0, keepdims=True)
    return x * lax.rsqrt(ms + EPS) * gain_col


def _seg_mean_sq(x, bd):
    hi, lo = _split2(x * x)
    return _dot(hi, bd) + _dot(lo, bd)


FFN_CHUNK = 256


def _ffn_half_step(x, g_ref, wg_ref, wu_ref, wd_ref, a_scr):
    xn = _rms_rows(x, g_ref[...]).astype(BF16)
    d_ff = wg_ref.shape[1]
    for c in range(d_ff // FFN_CHUNK):
        sl = slice(c * FFN_CHUNK, (c + 1) * FFN_CHUNK)
        g = _dot(xn, wg_ref[:, sl])
        u = _dot(xn, wu_ref[:, sl])
        a_scr[:, sl] = (_silu(g) * u).astype(BF16)
    return x + 0.5 * _dot(a_scr[...], wd_ref[...])


def _resident(shape):
    return pl.BlockSpec(shape, lambda *_: (0,) * len(shape), pipeline_mode=pl.Buffered(1))


def _rope_cols(xn, cos, sin):
    x0, x1 = xn[0:ROT_HALF], xn[ROT_HALF:ROT_DIM]
    return jnp.concatenate([x0 * cos - x1 * sin, x1 * cos + x0 * sin, xn[ROT_DIM:]], axis=0)


def _rope_rows(x, cn, sa, sb):
    return x * cn + pltpu.roll(x, 128 - ROT_HALF, 1) * sa + pltpu.roll(x, ROT_HALF, 1) * sb


def _proj_body(x_ref, fg_ref, wg_ref, wu_ref, wd_ref, mg_ref, wt_ref, wn_ref, qg_ref, kg_ref, mqg_ref,
               cos_ref, sin_ref, cn_ref, sa_ref, sb_ref, bd_ref,
               x1_ref, qt_ref, vt_ref, gt_ref, qmt_ref, kaug_ref, kvc_ref, hg_ref, a_scr):
    tm = x_ref.shape[0]
    assert tm == 8 * SLC_BLOCK
    x1 = _ffn_half_step(x_ref[...], fg_ref, wg_ref, wu_ref, wd_ref, a_scr)
    x1_ref[...] = x1
    h = _rms_rows(x1, mg_ref[...]).astype(BF16)

    pt = _dot_nt(wt_ref[...], h)
    pn = _dot(h, wn_ref[...])
    cos, sin = cos_ref[...], sin_ref[...]
    qg = qg_ref[...]
    for hh in range(NSA_HEADS):
        xq = _rms_cols(pt[hh * 64:(hh + 1) * 64], qg)
        qt_ref[hh * 64:(hh + 1) * 64, :] = (_rope_cols(xq, cos, sin) * QK_SCALE_LOG2).astype(BF16)
    ones_rows = (lax.broadcasted_iota(jnp.int32, (V_ROWS - 64, tm), 0) == 0).astype(BF16)
    for g in range(NSA_GROUPS):
        for br in range(2):
            rows = 512 + br * 128 + g * 64
            vt_ref[g, br, 0:64, :] = pt[rows:rows + 64].astype(BF16)
            vt_ref[g, br, 64:V_ROWS, :] = ones_rows
    gt_ref[...] = _sigmoid(pt[768:800])
    mqg = mqg_ref[...]
    for hh in range(MEM_HEADS):
        xm = _rms_cols(pt[800 + hh * 64:864 + hh * 64], mqg)
        qmt_ref[hh * 64:(hh + 1) * 64, :] = (xm * QK_SCALE_LOG2).astype(BF16)

    cn, sa, sb = cn_ref[...], sa_ref[...], sb_ref[...]
    bd, kg = bd_ref[...], kg_ref[...]
    kvc_ref[0] = _rope_rows(pn[:, 0:128], cn, sa, sb)
    kvc_ref[1] = pn[:, 128:256]
    ks = pn[:, 256:384]
    kw = pn[:, 384:512]
    ks = _rope_rows(ks * lax.rsqrt(_seg_mean_sq(ks, bd) + EPS) * kg, cn, sa, sb)
    kw = _rope_rows(kw * lax.rsqrt(_seg_mean_sq(kw, bd) + EPS) * kg, cn, sa, sb)
    lane = lax.broadcasted_iota(jnp.int32, (tm, 128), 1)
    row = lax.broadcasted_iota(jnp.int32, (tm, 128), 0)
    onehot = jnp.where(lane - 64 == (row >> SLC_SHIFT), 1.0, 0.0)
    lo_half = lane < 64
    kaug_ref[0, 0] = jnp.where(lo_half, ks, onehot).astype(BF16)
    kaug_ref[0, 1] = jnp.where(lo_half, kw, 0.0).astype(BF16)
    kaug_ref[1, 0] = jnp.where(lo_half, pltpu.roll(ks, 64, 1), onehot).astype(BF16)
    kaug_ref[1, 1] = jnp.where(lo_half, pltpu.roll(kw, 64, 1), 0.0).astype(BF16)
    hg_ref[...] = pn[:, 512:1536]


def _ffn_proj(x3d, ffn_gain, wg, wu, wd, mix_gain, wt, wn, q_gain, k_gain, mq_gain, rope):
    b, s, d = x3d.shape
    tm = TOK_TILE
    ns = s // tm
    cos_t, sin_t, cn, sa, sb = rope
    bd = jnp.asarray(np.kron(np.eye(2), np.full((64, 64), 1.0 / 64)), BF16)
    full = _resident
    out_shape = (
        jax.ShapeDtypeStruct((b, s, d), F32),
        jax.ShapeDtypeStruct((b, 512, s), BF16),
        jax.ShapeDtypeStruct((b, 2, 2, ns, V_ROWS, tm), BF16),
        jax.ShapeDtypeStruct((b, 32, s), F32),
        jax.ShapeDtypeStruct((b, 256, s), BF16),
        jax.ShapeDtypeStruct((b, 2, 2, s, 128), BF16),
        jax.ShapeDtypeStruct((b, 2, s, 128), F32),
        jax.ShapeDtypeStruct((b, s, 1024), F32),
    )
    out_specs = (
        pl.BlockSpec((None, tm, d), lambda bi, i: (bi, i, 0)),
        pl.BlockSpec((None, 512, tm), lambda bi, i: (bi, 0, i)),
        pl.BlockSpec((None, 2, 2, None, V_ROWS, tm), lambda bi, i: (bi, 0, 0, i, 0, 0)),
        pl.BlockSpec((None, 32, tm), lambda bi, i: (bi, 0, i)),
        pl.BlockSpec((None, 256, tm), lambda bi, i: (bi, 0, i)),
        pl.BlockSpec((None, 2, 2, tm, 128), lambda bi, i: (bi, 0, 0, i, 0)),
        pl.BlockSpec((None, 2, tm, 128), lambda bi, i: (bi, 0, i, 0)),
        pl.BlockSpec((None, tm, 1024), lambda bi, i: (bi, i, 0)),
    )
    in_specs = [
        pl.BlockSpec((None, tm, d), lambda bi, i: (bi, i, 0)),
        full((1, d)), full(wg.shape), full(wu.shape), full(wd.shape),
        full((1, d)), full(wt.shape), full(wn.shape),
        full((64, 1)), full((1, 128)), full((64, 1)),
        pl.BlockSpec((ROT_HALF, tm), lambda bi, i: (0, i)),
        pl.BlockSpec((ROT_HALF, tm), lambda bi, i: (0, i)),
        pl.BlockSpec((tm, 128), lambda bi, i: (i, 0)),
        pl.BlockSpec((tm, 128), lambda bi, i: (i, 0)),
        pl.BlockSpec((tm, 128), lambda bi, i: (i, 0)),
        full((128, 128)),
    ]
    return pl.pallas_call(
        _proj_body, grid=(b, ns), in_specs=in_specs, out_specs=out_specs, out_shape=out_shape,
        scratch_shapes=[pltpu.VMEM((tm, wg.shape[1]), BF16)],
        compiler_params=_cparams(("parallel", "parallel")), name="ffn1_mix_projection",
    )(x3d, ffn_gain.reshape(1, d), wg, wu, wd, mix_gain.reshape(1, d), wt, wn, q_gain.reshape(64, 1),
      jnp.tile(k_gain.reshape(1, 64), (1, 2)), mq_gain.reshape(64, 1), cos_t, sin_t, cn, sa, sb, bd)


def _rope_tables(s):
    pos = jnp.arange(s, dtype=F32)
    inv = ROPE_THETA ** (-(jnp.arange(0, ROT_DIM, 2, dtype=F32) / ROT_DIM))
    ang = pos[:, None] * inv[None, :]
    cos, sin = jnp.cos(ang), jnp.sin(ang)
    zeros = jnp.zeros((s, 64 - ROT_DIM), F32)
    cn = jnp.concatenate([cos, cos, jnp.ones((s, 64 - ROT_DIM), F32)], axis=1)
    sa = jnp.concatenate([-sin, jnp.zeros((s, ROT_HALF), F32), zeros], axis=1)
    sb = jnp.concatenate([jnp.zeros((s, ROT_HALF), F32), sin, zeros], axis=1)
    tile2 = lambda a: jnp.concatenate([a, a], axis=1)
    return cos.T, sin.T, tile2(cn), tile2(sa), tile2(sb)


def _cmp_body(kvc_ref, pos_ref, w1_ref, w2k_ref, w2vt_ref, kg_ref, kc_ref, vct_ref):
    nc = kvc_ref.shape[1] // CMP_STRIDE
    for kind in range(2):
        halves = []
        for part in range(2):
            x = jnp.concatenate(
                [(kvc_ref[kind, pl.ds(r, nc, stride=CMP_STRIDE), :]
                  + pos_ref[kind, part, :, r * 128:(r + 1) * 128]).astype(BF16) for r in range(CMP_STRIDE)],
                axis=1)
            halves.append(x)
        for g in range(NSA_GROUPS):
            second = _dot(halves[1], w1_ref[kind, g, 1])
            hid = _silu(_dot(halves[0], w1_ref[kind, g, 0]) + pltpu.roll(second, nc - 1, 0)).astype(BF16)
            if kind == 0:
                kc_ref[g] = _rms_rows(_dot(hid, w2k_ref[...]), kg_ref[...]).astype(BF16)
            else:
                vct_ref[g] = _dot_nt(w2vt_ref[...], hid).astype(BF16)


def _compress(kvc, pos, w1, w2k, w2vt, k_gain):
    b, _, s, _ = kvc.shape
    nc = s // CMP_STRIDE
    return pl.pallas_call(
        _cmp_body, grid=(b,),
        in_specs=[pl.BlockSpec((None, 2, s, 128), lambda bi: (bi, 0, 0, 0)),
                  _resident(pos.shape), _resident(w1.shape), _resident(w2k.shape), _resident(w2vt.shape),
                  _resident((1, 64))],
        out_specs=(pl.BlockSpec((None, 2, nc, 64), lambda bi: (bi, 0, 0, 0)),
                   pl.BlockSpec((None, 2, 64, nc), lambda bi: (bi, 0, 0, 0))),
        out_shape=(jax.ShapeDtypeStruct((b, 2, nc, 64), BF16), jax.ShapeDtypeStruct((b, 2, 64, nc), BF16)),
        compiler_params=_cparams(("parallel",)), name="nsa_compress",
    )(kvc, pos, w1, w2k, w2vt, k_gain.reshape(1, 64))


def _compress_weights(pos_k, pos_v, w1_k, w1_v):
    def pos_part(p):
        p = p.reshape(2, CMP_STRIDE, 1, 64)
        return jnp.broadcast_to(p, (2, CMP_STRIDE, NSA_GROUPS, 64)).reshape(2, 1, CMP_STRIDE * 128)

    def w1_part(w):
        hdim = w.shape[1]
        w = w.reshape(2, CMP_STRIDE, 1, 64, hdim)
        per_group = []
        for g in range(NSA_GROUPS):
            pads = [w if gg == g else jnp.zeros_like(w) for gg in range(NSA_GROUPS)]
            per_group.append(jnp.concatenate(pads, axis=2).reshape(2, CMP_STRIDE * 128, hdim))
        return jnp.stack(per_group)

    pos = jnp.stack([pos_part(pos_k), pos_part(pos_v)])
    w1 = jnp.stack([w1_part(w1_k), w1_part(w1_v)]).astype(BF16)
    return pos, w1


CMP_CLASS_ROWS = 128


def _cmpsel_variant(nc, nblk, fixed_reference, m0_ref, qt_ref, kc_ref, vct_ref, oc_ref, bias_ref, s_scr, flag_scr):
    tq = qt_ref.shape[1]
    t0 = pl.program_id(2) * tq
    n_idx = lax.broadcasted_iota(jnp.int32, (nc, tq), 0)
    t_idx = t0 + lax.broadcasted_iota(jnp.int32, (nc, tq), 1)
    mask_bias = jnp.where(n_idx * CMP_STRIDE + (CMP_BLOCK - 1) <= t_idx, 0.0, NEG)
    sees_any = t0 + lax.broadcasted_iota(jnp.int32, (1, tq), 1) >= CMP_BLOCK - 1
    kc = kc_ref[0:nc, :]
    if fixed_reference:
        mask_bias = mask_bias - m0_ref[0]
    else:
        for hh in range(NSA_HPG):
            s_scr[hh, 0:nc, :] = _dot(kc, qt_ref[hh * 64:(hh + 1) * 64, :]) + mask_bias
    jj = lax.broadcasted_iota(jnp.int32, (nblk, nc), 0)
    nn = lax.broadcasted_iota(jnp.int32, (nblk, nc), 1)
    ov = jnp.where((nn * CMP_STRIDE < jj * SLC_BLOCK + SLC_BLOCK)
                   & (nn * CMP_STRIDE + CMP_BLOCK > jj * SLC_BLOCK), 1.0, 0.0).astype(BF16)
    ones_rows = (lax.broadcasted_iota(jnp.int32, (V_ROWS - 64, nc), 0) == 0).astype(BF16)
    lhs = jnp.concatenate([vct_ref[:, 0:nc], ones_rows, ov], axis=0)
    imp = jnp.zeros((nblk, tq), F32)
    l_min = jnp.full((1, tq), 1.0, F32)

    def probabilities(hh):
        if fixed_reference:
            return jnp.exp2(_dot(kc, qt_ref[hh * 64:(hh + 1) * 64, :]) + mask_bias).astype(BF16)
        m = jnp.max(s_scr[hh, 0:nc, :], axis=0, keepdims=True)
        return jnp.exp2(s_scr[hh, 0:nc, :] - m).astype(BF16)

    p_next = probabilities(0)
    for hh in range(NSA_HPG):
        p = p_next
        if hh + 1 < NSA_HPG:
            p_next = probabilities(hh + 1)
        r = _dot(lhs, p)
        l_min = jnp.minimum(l_min, jnp.where(sees_any, r[64:65], 1.0))
        inv_l = jnp.where(sees_any, 1.0 / r[64:65], 0.0)
        oc_ref[hh * 64:(hh + 1) * 64, :] = r[0:64] * inv_l
        imp = imp + r[V_ROWS:] * inv_l
    if fixed_reference:
        flag_scr[0] = jnp.where(jnp.min(l_min) > MIN_DENOMINATOR, 0, 1)

    j = lax.broadcasted_iota(jnp.int32, (nblk, tq), 0)
    cur = (t0 + lax.broadcasted_iota(jnp.int32, (nblk, tq), 1)) >> SLC_SHIFT
    forced = (j == 0) | (j == cur) | (j == cur - 1)
    bias = jnp.where(forced & (j <= cur), 0.0, NEG)
    imp = jnp.where((j <= cur) & jnp.logical_not(forced), imp, -1.0)
    jf = j.astype(F32)
    for _ in range(SLC_TOPK - 3):
        v = jnp.max(imp, axis=0, keepdims=True)
        first = jnp.min(jnp.where(imp == v, jf, float(nblk)), axis=0, keepdims=True)
        pick = jf == first
        bias = jnp.where(pick & (v >= 0.0), 0.0, bias)
        imp = jnp.where(pick, -3e38, imp)
    bias_ref[0:nblk, :] = bias
    if nblk < bias_ref.shape[0]:
        bias_ref[nblk:, :] = jnp.full((bias_ref.shape[0] - nblk, tq), NEG, F32)


def _cmpsel_body(m0_ref, qt_ref, kc_ref, vct_ref, oc_ref, bias_ref, s_scr, flag_scr):
    tq = qt_ref.shape[1]
    nc_total = kc_ref.shape[0]
    tiles_per_class = CMP_CLASS_ROWS // (tq // CMP_STRIDE)
    cls = pl.program_id(2) // tiles_per_class
    refs = (m0_ref, qt_ref, kc_ref, vct_ref, oc_ref, bias_ref, s_scr, flag_scr)
    for c in range(nc_total // CMP_CLASS_ROWS):
        nc = (c + 1) * CMP_CLASS_ROWS
        nblk = min(nc * CMP_STRIDE // SLC_BLOCK, bias_ref.shape[0])
        pl.when(cls == c)(functools.partial(_cmpsel_variant, nc, nblk, True, *refs))
    pl.when(flag_scr[0] != 0)(functools.partial(_cmpsel_variant, nc_total, bias_ref.shape[0], False, *refs))


def _cmp_select(qt, kc, vct, score_bound, *, tq=TOK_TILE):
    b, _, s = qt.shape
    nc = kc.shape[2]
    nblk = MAX_BLOCKS
    assert s // SLC_BLOCK <= MAX_BLOCKS and s // SLC_BLOCK >= SLC_TOPK and nc % CMP_CLASS_ROWS == 0
    return pl.pallas_call(
        _cmpsel_body, grid=(b, NSA_GROUPS, s // tq),
        in_specs=[pl.BlockSpec(memory_space=pltpu.SMEM),
                  pl.BlockSpec((None, 256, tq), lambda bi, g, i: (bi, g, i)),
                  pl.BlockSpec((None, None, nc, 64), lambda bi, g, i: (bi, g, 0, 0)),
                  pl.BlockSpec((None, None, 64, nc), lambda bi, g, i: (bi, g, 0, 0))],
        out_specs=(pl.BlockSpec((None, 256, tq), lambda bi, g, i: (bi, g, i)),
                   pl.BlockSpec((None, None, nblk, tq), lambda bi, g, i: (bi, g, 0, i))),
        out_shape=(jax.ShapeDtypeStruct((b, 512, s), F32),
                   jax.ShapeDtypeStruct((b, NSA_GROUPS, nblk, s), F32)),
        scratch_shapes=[pltpu.VMEM((NSA_HPG, nc, tq), F32), pltpu.SMEM((1,), jnp.int32)],
        compiler_params=_cparams(("parallel", "parallel", "parallel")), name="nsa_compressed_select",
    )(score_bound, qt, kc, vct)


def _flash_step(s_ref, vt, m_ref, acc_ref):
    m_old = m_ref[...]
    m_new = jnp.maximum(m_old, jnp.max(s_ref[...], axis=0, keepdims=True))
    p = jnp.exp2(s_ref[...] - m_new)
    acc_ref[...] = jnp.exp2(m_old - m_new) * acc_ref[...] + _dot(vt, p.astype(BF16))
    m_ref[...] = m_new


SEL, WIN = 0, 1
BIAS_ROWS = 16


def _slcwin_body(m0_ref, qt_ref, bias_ref, kaug_ref, vt_ref, oc_ref, gt_ref, mb_ref, y_ref,
                 q_scr, m_scr, acc_scr, s_scr, p_scr):
    tq = qt_ref.shape[1]
    tk = vt_ref.shape[3]
    assert tq == tk and WINDOW == tk and tk == 8 * SLC_BLOCK
    diag = pl.program_id(2)
    m0 = m0_ref[0]

    sel_slots = (0, 1, 2, 3)
    qs, qw = sel_slots[0], 4
    zeros = jnp.zeros((64, tq), BF16)
    for slot in sel_slots + (qw,):
        for hh in range(NSA_HPG):
            q_scr[slot, hh, 0:64, :] = qt_ref[hh * 64:(hh + 1) * 64, :]
            q_scr[slot, hh, 64:128, :] = zeros

    def set_selection_bias(kt, slot=qs):
        rows = bias_ref[pl.ds(pl.multiple_of(kt * 8, 8), 8), :]
        b16 = jnp.concatenate([rows, jnp.zeros_like(rows)], axis=0).astype(BF16)
        for hh in range(NSA_HPG):
            q_scr[slot, hh, 64:64 + BIAS_ROWS, :] = b16

    def tiles_fixed_reference(tiles):
        chains = [(br, qslot, kt, mask_bias, hh) for br, qslot, kt, mask_bias in tiles for hh in range(NSA_HPG)]
        for c in range(len(chains) + 1):
            if c < len(chains):
                br, qslot, kt, mask_bias, hh = chains[c]
                s = _dot(kaug_ref[br, pl.ds(pl.multiple_of(kt * tk, tk), tk), :], q_scr[qslot, hh])
                if mask_bias is not None:
                    s = s + mask_bias()
                p_scr[c % 4] = jnp.exp2(s - m0).astype(BF16)
            if c >= 1:
                br, _, kt, _, hh = chains[c - 1]
                acc_scr[br, hh] = acc_scr[br, hh] + _dot(vt_ref[br, kt], p_scr[(c - 1) % 4])

    def tile_running_max(br, qslot, kt, mask_bias=None):
        k = kaug_ref[br, pl.ds(pl.multiple_of(kt * tk, tk), tk), :]
        for hh in range(NSA_HPG):
            s = _dot(k, q_scr[qslot, hh])
            s_scr[hh] = s if mask_bias is None else s + mask_bias()
        for hh in range(NSA_HPG):
            _flash_step(s_scr.at[hh], vt_ref[br, kt], m_scr.at[br, hh], acc_scr.at[br, hh])

    prev = jnp.maximum(diag - 1, 0)
    no_prev = jnp.where(diag == 0, NEG, 0.0)
    band_bias = lambda: mb_ref[1] + no_prev
    causal_bias = lambda: mb_ref[0]

    acc_scr[...] = jnp.zeros(acc_scr.shape, F32)

    def unmasked_run(first_tile, count):
        for n in range(count):
            set_selection_bias(first_tile + n, sel_slots[n])
        tiles_fixed_reference([(SEL, sel_slots[n], first_tile + n, None) for n in range(count)])

    def tile_quad(j, carry):
        unmasked_run(4 * j, 4)
        return carry

    lax.fori_loop(0, diag >> 2, tile_quad, 0)
    pl.when((diag & 2) != 0)(lambda: unmasked_run((diag >> 2) * 4, 2))
    pl.when((diag & 1) != 0)(lambda: unmasked_run(diag - 1, 1))

    set_selection_bias(diag, qs)
    tiles_fixed_reference([(WIN, qw, prev, band_bias), (SEL, qs, diag, causal_bias), (WIN, qw, diag, causal_bias)])

    denominators = acc_scr[:, :, 64:65, :]
    underflow = jnp.logical_not(jnp.min(denominators) > MIN_DENOMINATOR)

    @pl.when(underflow)
    def _():
        m_scr[...] = jnp.full(m_scr.shape, NEG, F32)
        acc_scr[...] = jnp.zeros(acc_scr.shape, F32)

        def full_tile(kt, carry):
            set_selection_bias(kt, qs)
            tile_running_max(SEL, qs, kt)
            return carry

        lax.fori_loop(0, diag, full_tile, 0)
        set_selection_bias(diag, qs)
        tile_running_max(WIN, qw, prev, band_bias)
        tile_running_max(SEL, qs, diag, causal_bias)
        tile_running_max(WIN, qw, diag, causal_bias)

    gt = gt_ref[...]
    for hh in range(NSA_HPG):
        o_s = acc_scr[SEL, hh, 0:64, :] * (1.0 / acc_scr[SEL, hh, 64:65, :])
        o_w = acc_scr[WIN, hh, 0:64, :] * (1.0 / acc_scr[WIN, hh, 64:65, :])
        y_ref[hh * 64:(hh + 1) * 64, :] = (gt[3 * hh:3 * hh + 1] * oc_ref[hh * 64:(hh + 1) * 64, :]
                                          + gt[3 * hh + 1:3 * hh + 2] * o_s
                                          + gt[3 * hh + 2:3 * hh + 3] * o_w)


def _score_bound(q_gain, k_gain):
    bound = HEAD_DIM * QK_SCALE_LOG2 * jnp.max(jnp.abs(q_gain)) * jnp.max(jnp.abs(k_gain))
    return (1.02 * bound).reshape(1).astype(F32)


def _slc_win(qt, bias, kaug, vt, oc, gt, score_bound):
    b, _, s = qt.shape
    nblk = bias.shape[2]
    ns, tk = vt.shape[3], vt.shape[5]
    tq = tk
    key_rel, t_rel = np.arange(tk)[:, None], np.arange(tq)[None, :]
    mask_bias = jnp.asarray(np.stack([np.where(key_rel <= t_rel, 0.0, NEG),
                                      np.where(t_rel + tk - key_rel < WINDOW, 0.0, NEG)]), F32)
    qblk = pl.BlockSpec((None, 256, tq), lambda bi, g, i: (bi, g, i))
    return pl.pallas_call(
        _slcwin_body, grid=(b, NSA_GROUPS, s // tq),
        in_specs=[pl.BlockSpec(memory_space=pltpu.SMEM), qblk,
                  pl.BlockSpec((None, None, nblk, tq), lambda bi, g, i: (bi, g, 0, i)),
                  pl.BlockSpec((None, None, 2, s, 128), lambda bi, g, i: (bi, g, 0, 0, 0)),
                  pl.BlockSpec((None, None, 2, ns, V_ROWS, tk), lambda bi, g, i: (bi, g, 0, 0, 0, 0)),
                  qblk,
                  pl.BlockSpec((None, None, GATE_ROWS, tq), lambda bi, g, i: (bi, g, 0, i)),
                  _resident((2, tk, tq))],
        out_specs=qblk,
        out_shape=jax.ShapeDtypeStruct((b, 512, s), F32),
        scratch_shapes=[pltpu.VMEM((5, NSA_HPG, 128, tq), BF16),
                        pltpu.VMEM((2, NSA_HPG, 1, tq), F32), pltpu.VMEM((2, NSA_HPG, V_ROWS, tq), F32),
                        pltpu.VMEM((NSA_HPG, tk, tq), F32), pltpu.VMEM((NSA_HPG, tk, tq), BF16)],
        compiler_params=_cparams(("parallel", "parallel", "arbitrary")), name="nsa_selected_window",
    )(score_bound, qt, bias, kaug, vt, oc, gt.reshape(b, NSA_GROUPS, GATE_ROWS, s), mask_bias)


def _hgrn_consts():
    c = HGRN_CHUNK
    t = np.arange(c)
    lower = (t[None, :] <= t[:, None]).astype(np.float32)
    rows = [lower]
    masks = []
    for half in HGRN_LEVELS:
        mid = (t // (2 * half)) * (2 * half) + half - 1
        if half < 8:
            rows.append(lower[mid])
        same = (t[:, None] // (2 * half)) == (t[None, :] // (2 * half))
        right = (t[:, None] & half) != 0
        left = (t[None, :] & half) == 0
        masks.append((same & right & left).astype(np.float32))
    masks.append(np.eye(c, dtype=np.float32))
    mall = np.concatenate(rows, axis=0)
    lvl = np.stack([np.tile(mk.T, (1, HGRN_HEADS)) for mk in masks])
    bdm = np.kron(np.eye(HGRN_HEADS), np.ones((64, 64), np.float32))
    return jnp.asarray(mall, BF16), jnp.asarray(lvl, F32), jnp.asarray(bdm, F32), jnp.asarray(bdm / 64, BF16)


def _hgrn_body(hg_ref, lb_ref, og_ref, mall_ref, lvl_ref, bdm_ref, bdn_ref, y_ref, st_scr):
    c = HGRN_CHUNK
    w = HGRN_WIDTH

    @pl.when(pl.program_id(1) == 0)
    def _():
        st_scr[...] = jnp.zeros(st_scr.shape, F32)

    chunks = range(hg_ref.shape[0] // c)
    lb = lb_ref[...]
    lane = lax.broadcasted_iota(jnp.int32, (c, w), 1)
    head_masks = [(lane >> 6) == hh for hh in range(HGRN_HEADS)]
    nlev = len(HGRN_LEVELS)

    def stack_heads(x):
        x16 = x.astype(BF16)
        return jnp.concatenate([jnp.where(hm, x16, 0) for hm in head_masks], axis=0)

    def row_bcast(x, half):
        return jnp.concatenate([jnp.broadcast_to(x[p + half - 1:p + half, :], (2 * half, w))
                                for p in range(0, c, 2 * half)], axis=0)

    qa, kk, v16, logf = [], [], [], []
    for ci in chunks:
        rows = slice(ci * c, (ci + 1) * c)
        qa.append(_silu(hg_ref[rows, 0:w]) * (HEAD_DIM ** -0.5))
        fg = lb + (1.0 - lb) * _sigmoid(hg_ref[rows, w:2 * w])
        kk.append(1.0 - fg)
        logf.append(jnp.log2(fg))
        v16.append(hg_ref[rows, 2 * w:3 * w].astype(BF16))

    mall = mall_ref[...]
    r_all = sum(_dot(mall, part) for part in _split3(jnp.concatenate(logf, axis=1)))
    bcum = [r_all[0:c, ci * w:(ci + 1) * w] for ci in chunks]

    attn = [lvl_ref[nlev] * _dot_nt(kk[ci].astype(BF16), stack_heads(qa[ci])) for ci in chunks]
    fine = 0
    for li, half in enumerate(HGRN_LEVELS):
        for ci in chunks:
            if half >= 8:
                ref_pt = row_bcast(bcum[ci], half)
            else:
                ref_pt = r_all[(fine + 1) * c:(fine + 2) * c, ci * w:(ci + 1) * w]
            e = jnp.exp2(-jnp.abs(bcum[ci] - ref_pt))
            kt = (kk[ci] * e).astype(BF16)
            attn[ci] = attn[ci] + lvl_ref[li] * _dot_nt(kt, stack_heads(qa[ci] * e))
        if half < 8:
            fine += 1

    intra, upd, decay, qb = [], [], [], []
    for ci in chunks:
        x = _dot_tn(attn[ci].astype(BF16), v16[ci])
        intra.append(sum(jnp.where(head_masks[hh], x[hh * c:(hh + 1) * c], 0.0) for hh in range(HGRN_HEADS)))
        b_last = bcum[ci][c - 1:c, :]
        kl = (kk[ci] * jnp.exp2(b_last - bcum[ci])).astype(BF16)
        upd.append(bdm_ref[...] * _dot_tn(v16[ci], kl))
        decay.append(jnp.exp2(b_last))
        qb.append((qa[ci] * jnp.exp2(bcum[ci])).astype(BF16))

    st = st_scr[...]
    inter = []
    for ci in chunks:
        inter.append(_dot_nt(qb[ci], st.astype(BF16)))
        st = st * decay[ci] + upd[ci]
    st_scr[...] = st

    for ci in chunks:
        rows = slice(ci * c, (ci + 1) * c)
        o = inter[ci] + intra[ci]
        hi, lo = _split2(o * o)
        ms = _dot(hi, bdn_ref[...]) + _dot(lo, bdn_ref[...])
        y_ref[rows, :] = o * lax.rsqrt(ms + EPS) * og_ref[...] * _silu(hg_ref[rows, 3 * w:4 * w])


def _hgrn(hg, lower_bound, out_gain):
    b, s, _ = hg.shape
    rows = HGRN_CB * HGRN_CHUNK
    mall, lvl, bdm, bdn = _hgrn_consts()
    full = lambda shape: pl.BlockSpec(shape, lambda bi, i: (0,) * len(shape))
    return pl.pallas_call(
        _hgrn_body, grid=(b, s // rows),
        in_specs=[pl.BlockSpec((None, rows, 4 * HGRN_WIDTH), lambda bi, i: (bi, i, 0)),
                  full((1, HGRN_WIDTH)), full((1, HGRN_WIDTH)),
                  full(mall.shape), full(lvl.shape), full(bdm.shape), full(bdn.shape)],
        out_specs=pl.BlockSpec((None, rows, HGRN_WIDTH), lambda bi, i: (bi, i, 0)),
        out_shape=jax.ShapeDtypeStruct((b, s, HGRN_WIDTH), F32),
        scratch_shapes=[pltpu.VMEM((HGRN_WIDTH, HGRN_WIDTH), F32)],
        compiler_params=_cparams(("parallel", "arbitrary")), name="hgrn2_chunks",
    )(hg, lower_bound.reshape(1, -1), out_gain.reshape(1, -1), mall, lvl, bdm, bdn)


def _memkv_body(mem_ref, mg_ref, wk_ref, wvt_ref, kg_ref, kh_ref, vht_ref):
    m = mem_ref.shape[0]
    mn = _rms_rows(mem_ref[...], mg_ref[...]).astype(BF16)
    k = _dot(mn, wk_ref[...])
    vt = _dot_nt(wvt_ref[...], mn)
    ones_rows = (lax.broadcasted_iota(jnp.int32, (V_ROWS - 64, m), 0) == 0).astype(BF16)
    for hh in range(MEM_HEADS):
        kh_ref[hh] = _rms_rows(k[:, hh * 64:(hh + 1) * 64], kg_ref[...]).astype(BF16)
        vht_ref[hh, 0:64, :] = vt[hh * 64:(hh + 1) * 64].astype(BF16)
        vht_ref[hh, 64:V_ROWS, :] = ones_rows


def _mem_kv(mem, mem_gain, wk, wvt, k_gain):
    b, m, d = mem.shape
    full = lambda shape: pl.BlockSpec(shape, lambda bi: (0,) * len(shape))
    return pl.pallas_call(
        _memkv_body, grid=(b,),
        in_specs=[pl.BlockSpec((None, m, d), lambda bi: (bi, 0, 0)), full((1, d)),
                  full(wk.shape), full(wvt.shape), full((1, 64))],
        out_specs=(pl.BlockSpec((None, MEM_HEADS, m, 64), lambda bi: (bi, 0, 0, 0)),
                   pl.BlockSpec((None, MEM_HEADS, V_ROWS, m), lambda bi: (bi, 0, 0, 0))),
        out_shape=(jax.ShapeDtypeStruct((b, MEM_HEADS, m, 64), BF16),
                   jax.ShapeDtypeStruct((b, MEM_HEADS, V_ROWS, m), BF16)),
        compiler_params=_cparams(("parallel",)), name="memory_kv",
    )(mem, mem_gain.reshape(1, d), wk, wvt, k_gain.reshape(1, 64))


def _out_body(x_ref, ynt_ref, yh_ref, qmt_ref, kh_ref, vht_ref, ng_ref, mg_ref, wo_ref,
              fg_ref, wg_ref, wu_ref, wd_ref, o_ref, a_scr):
    y_mem = []
    for hh in range(MEM_HEADS):
        s = _dot(kh_ref[hh], qmt_ref[hh * 64:(hh + 1) * 64, :])
        p = jnp.exp2(s - jnp.max(s, axis=0, keepdims=True))
        o = _dot(vht_ref[hh], p.astype(BF16))
        y_mem.append(o[0:64] * (1.0 / o[64:65]))
    mem = _rms_cols(jnp.concatenate(y_mem, axis=0), mg_ref[...]).astype(BF16)
    nsa = _rms_cols(ynt_ref[...], ng_ref[...]).astype(BF16)
    acc = _dot_tn(nsa, wo_ref[0:NSA_WIDTH, :])
    acc = acc + _dot(yh_ref[...].astype(BF16), wo_ref[NSA_WIDTH:NSA_WIDTH + HGRN_WIDTH, :])
    acc = acc + _dot_tn(mem, wo_ref[NSA_WIDTH + HGRN_WIDTH:, :])
    o_ref[...] = _ffn_half_step(x_ref[...] + acc, fg_ref, wg_ref, wu_ref, wd_ref, a_scr)


def _out_ffn(x3d, ynt, yh, qmt, kh, vht, nsa_gain, mem_gain, wo, ffn_gain, wg, wu, wd, *, tm=TOK_TILE):
    b, s, d = x3d.shape
    m = kh.shape[2]
    return pl.pallas_call(
        _out_body, grid=(b, s // tm),
        in_specs=[pl.BlockSpec((None, tm, d), lambda bi, i: (bi, i, 0)),
                  pl.BlockSpec((None, NSA_WIDTH, tm), lambda bi, i: (bi, 0, i)),
                  pl.BlockSpec((None, tm, HGRN_WIDTH), lambda bi, i: (bi, i, 0)),
                  pl.BlockSpec((None, MEM_WIDTH, tm), lambda bi, i: (bi, 0, i)),
                  pl.BlockSpec((None, MEM_HEADS, m, 64), lambda bi, i: (bi, 0, 0, 0)),
                  pl.BlockSpec((None, MEM_HEADS, V_ROWS, m), lambda bi, i: (bi, 0, 0, 0)),
                  _resident((NSA_WIDTH, 1)), _resident((MEM_WIDTH, 1)), _resident(wo.shape),
                  _resident((1, d)), _resident(wg.shape), _resident(wu.shape), _resident(wd.shape)],
        out_specs=pl.BlockSpec((None, tm, d), lambda bi, i: (bi, i, 0)),
        out_shape=jax.ShapeDtypeStruct((b, s, d), F32),
        scratch_shapes=[pltpu.VMEM((tm, wg.shape[1]), BF16)],
        compiler_params=_cparams(("parallel", "parallel")), name="mix_out_ffn2",
    )(x3d, ynt, yh, qmt, kh, vht, nsa_gain.reshape(-1, 1), mem_gain.reshape(-1, 1), wo,
      ffn_gain.reshape(1, d), wg, wu, wd)


def _layer(x, mem, ffn1, ffn2, mix_norm, w_in, w_out, nsa_q_norm, nsa_k_norm, cmp_pos_k, cmp_w1_k, cmp_w2_k,
           cmp_pos_v, cmp_w1_v, cmp_w2_v, nsa_out_norm, lower_bound, hgrn_out_norm,
           mem_norm, mem_w_k, mem_w_v, mem_q_norm, mem_k_norm, mem_out_norm):
    b, s, d = x.shape
    sizes = (512, 128, 128, 128, 128, 128, 128, 24, 256, 256, 256, 256, 256)
    offs = np.concatenate([[0], np.cumsum(sizes)])
    col = lambda i: w_in[:, offs[i]:offs[i + 1]]
    (q_a, k_c, v_c, k_s, v_s, k_w, v_w, g_a, q_h, f_h, i_h, g_h, q_m) = [col(i) for i in range(13)]
    gpad = jnp.zeros((d, GATE_ROWS - 3 * NSA_HPG), w_in.dtype)
    wt = jnp.concatenate([q_a, v_s, v_w, g_a[:, :3 * NSA_HPG], gpad, g_a[:, 3 * NSA_HPG:], gpad, q_m],
                         axis=1).T.astype(BF16)
    wn = jnp.concatenate([k_c, v_c, k_s, k_w, q_h, f_h, i_h, g_h], axis=1).astype(BF16)

    x1, qt, vt, gt, qmt, kaug, kvc, hg = _ffn_proj(
        x, *ffn1, mix_norm, wt, wn, nsa_q_norm, nsa_k_norm, mem_q_norm, _rope_tables(s))

    cmp_pos, cmp_w1 = _compress_weights(cmp_pos_k, cmp_pos_v, cmp_w1_k, cmp_w1_v)
    kc, vct = _compress(kvc, cmp_pos, cmp_w1, cmp_w2_k.astype(BF16), cmp_w2_v.T.astype(BF16), nsa_k_norm)
    score_bound = _score_bound(nsa_q_norm, nsa_k_norm)
    oc, bias = _cmp_select(qt, kc, vct, score_bound)
    y_nsa = _slc_win(qt, bias, kaug, vt, oc, gt, score_bound)

    y_hgrn = _hgrn(hg, lower_bound, hgrn_out_norm)

    kh, vht = _mem_kv(mem, mem_norm, mem_w_k.astype(BF16), mem_w_v.T.astype(BF16), mem_k_norm)
    return _out_ffn(x1, y_nsa, y_hgrn, qmt, kh, vht, nsa_out_norm, mem_out_norm, w_out.astype(BF16), *ffn2)


def kernel(x, mem, ffn1_norm, ffn1_w_gate, ffn1_w_up, ffn1_w_down, mix_norm, w_in, w_out, nsa_q_norm, nsa_k_norm, cmp_pos_k, cmp_w1_k, cmp_w2_k, cmp_pos_v, cmp_w1_v, cmp_w2_v, nsa_out_norm, hgrn_lb_logits, hgrn_out_norm, mem_norm, mem_w_k, mem_w_v, mem_q_norm, mem_k_norm, mem_out_norm, ffn2_norm, ffn2_w_gate, ffn2_w_up, ffn2_w_down):
    b, s, d = x.shape
    depth = ffn1_norm.shape[0]
    lower_bounds = jnp.cumsum(jax.nn.softmax(hgrn_lb_logits.astype(F32), axis=0), axis=0)
    bf = lambda a: a.astype(BF16)
    for l in range(depth):
        x = _layer(x, mem, (ffn1_norm[l], bf(ffn1_w_gate[l]), bf(ffn1_w_up[l]), bf(ffn1_w_down[l])),
                   (ffn2_norm[l], bf(ffn2_w_gate[l]), bf(ffn2_w_up[l]), bf(ffn2_w_down[l])),
                   mix_norm[l], w_in[l], w_out[l], nsa_q_norm[l], nsa_k_norm[l],
                   cmp_pos_k[l], cmp_w1_k[l], cmp_w2_k[l], cmp_pos_v[l], cmp_w1_v[l], cmp_w2_v[l],
                   nsa_out_norm[l], lower_bounds[l], hgrn_out_norm[l],
                   mem_norm[l], mem_w_k[l], mem_w_v[l], mem_q_norm[l], mem_k_norm[l], mem_out_norm[l])
    return x
```

```python
import functools

import numpy as np
import jax
import jax.numpy as jnp
from jax import lax
from jax.experimental import pallas as pl
from jax.experimental.pallas import tpu as pltpu

F32 = jnp.float32
BF16 = jnp.bfloat16

HEAD_DIM = 64
ROT_DIM = 16
ROT_HALF = 8
ROPE_THETA = 500000.0
NSA_HEADS = 8
NSA_GROUPS = 2
NSA_HPG = 4
CMP_BLOCK = 32
CMP_STRIDE = 16
SLC_BLOCK = 64
SLC_SHIFT = 6
SLC_TOPK = 16
WINDOW = 512
FORCED_SCORE = 1e4
HGRN_HEADS = 4
HGRN_CHUNK = 64
HGRN_WIDTH = 256
MEM_HEADS = 4
MEM_WIDTH = 256
NSA_WIDTH = 512
EPS = 1e-6
NEG = -1e30
QK_SCALE_LOG2 = HEAD_DIM ** -0.5 * 1.4426950408889634
MIN_DENOMINATOR = 2.0 ** -64

VMEM_LIMIT = 56 * 1024 * 1024
MAX_BLOCKS = 128
GATE_ROWS = 16
V_ROWS = 80
TOK_TILE = 512
HGRN_CB = 8
HGRN_LEVELS = (32, 16, 8, 4, 2, 1)
HGRN_SUB = 16
HGRN_MAX_EXPONENT = 96.0

NT_DIMS = (((1,), (1,)), ((), ()))
TN_DIMS = (((0,), (0,)), ((), ()))


def _cparams(sem):
    return pltpu.CompilerParams(dimension_semantics=sem, vmem_limit_bytes=VMEM_LIMIT)


def _dot(a, b):
    return jnp.dot(a, b, preferred_element_type=F32)


def _dot_nt(a, b):
    return lax.dot_general(a, b, NT_DIMS, preferred_element_type=F32)


def _dot_tn(a, b):
    return lax.dot_general(a, b, TN_DIMS, preferred_element_type=F32)


def _sigmoid(x):
    return 1.0 / (1.0 + jnp.exp(-x))


def _silu(x):
    return x * _sigmoid(x)


def _split2(x):
    hi = x.astype(BF16)
    lo = (x - hi.astype(F32)).astype(BF16)
    return hi, lo


def _split3(x):
    hi = x.astype(BF16)
    r1 = x - hi.astype(F32)
    mid = r1.astype(BF16)
    lo = (r1 - mid.astype(F32)).astype(BF16)
    return hi, mid, lo


def _rms_rows(x, gain_row):
    ms = jnp.mean(x * x, axis=-1, keepdims=True)
    return x * lax.rsqrt(ms + EPS) * gain_row


def _rms_cols(x, gain_col):
    ms = jnp.mean(x * x, axis=0, keepdims=True)
    return x * lax.rsqrt(ms + EPS) * gain_col


def _seg_mean_sq(x, bd):
    hi, lo = _split2(x * x)
    return _dot(hi, bd) + _dot(lo, bd)


FFN_CHUNK = 256


def _ffn_half_step(x, g_ref, wg_ref, wu_ref, wd_ref, a_scr):
    xn = _rms_rows(x, g_ref[...]).astype(BF16)
    d_ff = wg_ref.shape[1]
    for c in range(d_ff // FFN_CHUNK):
        sl = slice(c * FFN_CHUNK, (c + 1) * FFN_CHUNK)
        g = _dot(xn, wg_ref[:, sl])
        u = _dot(xn, wu_ref[:, sl])
        a_scr[:, sl] = (_silu(g) * u).astype(BF16)
    return x + 0.5 * _dot(a_scr[...], wd_ref[...])


def _resident(shape):
    return pl.BlockSpec(shape, lambda *_: (0,) * len(shape), pipeline_mode=pl.Buffered(1))


def _rope_cols(xn, cos, sin):
    x0, x1 = xn[0:ROT_HALF], xn[ROT_HALF:ROT_DIM]
    return jnp.concatenate([x0 * cos - x1 * sin, x1 * cos + x0 * sin, xn[ROT_DIM:]], axis=0)


def _rope_rows(x, cn, sa, sb):
    return x * cn + pltpu.roll(x, 128 - ROT_HALF, 1) * sa + pltpu.roll(x, ROT_HALF, 1) * sb


def _proj_body(x_ref, fg_ref, wg_ref, wu_ref, wd_ref, mg_ref, wt_ref, wn_ref, qg_ref, kg_ref, mqg_ref,
               cos_ref, sin_ref, cn_ref, sa_ref, sb_ref, bd_ref,
               x1_ref, qt_ref, vt_ref, gt_ref, qmt_ref, kaug_ref, kvc_ref, hg_ref, a_scr):
    tm = x_ref.shape[0]
    assert tm == 8 * SLC_BLOCK
    x1 = _ffn_half_step(x_ref[...], fg_ref, wg_ref, wu_ref, wd_ref, a_scr)
    x1_ref[...] = x1
    h = _rms_rows(x1, mg_ref[...]).astype(BF16)

    pt = _dot_nt(wt_ref[...], h)
    pn = _dot(h, wn_ref[...])
    cos, sin = cos_ref[...], sin_ref[...]
    qg = qg_ref[...]
    for hh in range(NSA_HEADS):
        xq = _rms_cols(pt[hh * 64:(hh + 1) * 64], qg)
        qt_ref[hh * 64:(hh + 1) * 64, :] = (_rope_cols(xq, cos, sin) * QK_SCALE_LOG2).astype(BF16)
    ones_rows = (lax.broadcasted_iota(jnp.int32, (V_ROWS - 64, tm), 0) == 0).astype(BF16)
    for g in range(NSA_GROUPS):
        for br in range(2):
            rows = 512 + br * 128 + g * 64
            vt_ref[g, br, 0:64, :] = pt[rows:rows + 64].astype(BF16)
            vt_ref[g, br, 64:V_ROWS, :] = ones_rows
    gt_ref[...] = _sigmoid(pt[768:800])
    mqg = mqg_ref[...]
    for hh in range(MEM_HEADS):
        xm = _rms_cols(pt[800 + hh * 64:864 + hh * 64], mqg)
        qmt_ref[hh * 64:(hh + 1) * 64, :] = (xm * QK_SCALE_LOG2).astype(BF16)

    cn, sa, sb = cn_ref[...], sa_ref[...], sb_ref[...]
    bd, kg = bd_ref[...], kg_ref[...]
    kvc_ref[0] = _rope_rows(pn[:, 0:128], cn, sa, sb)
    kvc_ref[1] = pn[:, 128:256]
    ks = pn[:, 256:384]
    kw = pn[:, 384:512]
    ks = _rope_rows(ks * lax.rsqrt(_seg_mean_sq(ks, bd) + EPS) * kg, cn, sa, sb)
    kw = _rope_rows(kw * lax.rsqrt(_seg_mean_sq(kw, bd) + EPS) * kg, cn, sa, sb)
    lane = lax.broadcasted_iota(jnp.int32, (tm, 128), 1)
    row = lax.broadcasted_iota(jnp.int32, (tm, 128), 0)
    onehot = jnp.where(lane - 64 == (row >> SLC_SHIFT), 1.0, 0.0)
    lo_half = lane < 64
    kaug_ref[0, 0] = jnp.where(lo_half, ks, onehot).astype(BF16)
    kaug_ref[0, 1] = jnp.where(lo_half, kw, 0.0).astype(BF16)
    kaug_ref[1, 0] = jnp.where(lo_half, pltpu.roll(ks, 64, 1), onehot).astype(BF16)
    kaug_ref[1, 1] = jnp.where(lo_half, pltpu.roll(kw, 64, 1), 0.0).astype(BF16)
    hg_ref[...] = pn[:, 512:1536]


def _ffn_proj(x3d, ffn_gain, wg, wu, wd, mix_gain, wt, wn, q_gain, k_gain, mq_gain, rope):
    b, s, d = x3d.shape
    tm = TOK_TILE
    ns = s // tm
    cos_t, sin_t, cn, sa, sb = rope
    bd = jnp.asarray(np.kron(np.eye(2), np.full((64, 64), 1.0 / 64)), BF16)
    full = _resident
    out_shape = (
        jax.ShapeDtypeStruct((b, s, d), F32),
        jax.ShapeDtypeStruct((b, 512, s), BF16),
        jax.ShapeDtypeStruct((b, 2, 2, ns, V_ROWS, tm), BF16),
        jax.ShapeDtypeStruct((b, 32, s), F32),
        jax.ShapeDtypeStruct((b, 256, s), BF16),
        jax.ShapeDtypeStruct((b, 2, 2, s, 128), BF16),
        jax.ShapeDtypeStruct((b, 2, s, 128), F32),
        jax.ShapeDtypeStruct((b, s, 1024), F32),
    )
    out_specs = (
        pl.BlockSpec((None, tm, d), lambda bi, i: (bi, i, 0)),
        pl.BlockSpec((None, 512, tm), lambda bi, i: (bi, 0, i)),
        pl.BlockSpec((None, 2, 2, None, V_ROWS, tm), lambda bi, i: (bi, 0, 0, i, 0, 0)),
        pl.BlockSpec((None, 32, tm), lambda bi, i: (bi, 0, i)),
        pl.BlockSpec((None, 256, tm), lambda bi, i: (bi, 0, i)),
        pl.BlockSpec((None, 2, 2, tm, 128), lambda bi, i: (bi, 0, 0, i, 0)),
        pl.BlockSpec((None, 2, tm, 128), lambda bi, i: (bi, 0, i, 0)),
        pl.BlockSpec((None, tm, 1024), lambda bi, i: (bi, i, 0)),
    )
    in_specs = [
        pl.BlockSpec((None, tm, d), lambda bi, i: (bi, i, 0)),
        full((1, d)), full(wg.shape), full(wu.shape), full(wd.shape),
        full((1, d)), full(wt.shape), full(wn.shape),
        full((64, 1)), full((1, 128)), full((64, 1)),
        pl.BlockSpec((ROT_HALF, tm), lambda bi, i: (0, i)),
        pl.BlockSpec((ROT_HALF, tm), lambda bi, i: (0, i)),
        pl.BlockSpec((tm, 128), lambda bi, i: (i, 0)),
        pl.BlockSpec((tm, 128), lambda bi, i: (i, 0)),
        pl.BlockSpec((tm, 128), lambda bi, i: (i, 0)),
        full((128, 128)),
    ]
    return pl.pallas_call(
        _proj_body, grid=(b, ns), in_specs=in_specs, out_specs=out_specs, out_shape=out_shape,
        scratch_shapes=[pltpu.VMEM((tm, wg.shape[1]), BF16)],
        compiler_params=_cparams(("parallel", "parallel")), name="ffn1_mix_projection",
    )(x3d, ffn_gain.reshape(1, d), wg, wu, wd, mix_gain.reshape(1, d), wt, wn, q_gain.reshape(64, 1),
      jnp.tile(k_gain.reshape(1, 64), (1, 2)), mq_gain.reshape(64, 1), cos_t, sin_t, cn, sa, sb, bd)


def _rope_tables(s):
    pos = jnp.arange(s, dtype=F32)
    inv = ROPE_THETA ** (-(jnp.arange(0, ROT_DIM, 2, dtype=F32) / ROT_DIM))
    ang = pos[:, None] * inv[None, :]
    cos, sin = jnp.cos(ang), jnp.sin(ang)
    zeros = jnp.zeros((s, 64 - ROT_DIM), F32)
    cn = jnp.concatenate([cos, cos, jnp.ones((s, 64 - ROT_DIM), F32)], axis=1)
    sa = jnp.concatenate([-sin, jnp.zeros((s, ROT_HALF), F32), zeros], axis=1)
    sb = jnp.concatenate([jnp.zeros((s, ROT_HALF), F32), sin, zeros], axis=1)
    tile2 = lambda a: jnp.concatenate([a, a], axis=1)
    return cos.T, sin.T, tile2(cn), tile2(sa), tile2(sb)


def _cmp_body(kvc_ref, pos_ref, w1_ref, w2k_ref, w2vt_ref, kg_ref, kc_ref, vct_ref):
    nc = kvc_ref.shape[1] // CMP_STRIDE
    for kind in range(2):
        halves = []
        for part in range(2):
            x = jnp.concatenate(
                [(kvc_ref[kind, pl.ds(r, nc, stride=CMP_STRIDE), :]
                  + pos_ref[kind, part, :, r * 128:(r + 1) * 128]).astype(BF16) for r in range(CMP_STRIDE)],
                axis=1)
            halves.append(x)
        for g in range(NSA_GROUPS):
            second = _dot(halves[1], w1_ref[kind, g, 1])
            hid = _silu(_dot(halves[0], w1_ref[kind, g, 0]) + pltpu.roll(second, nc - 1, 0)).astype(BF16)
            if kind == 0:
                kc_ref[g] = _rms_rows(_dot(hid, w2k_ref[...]), kg_ref[...]).astype(BF16)
            else:
                vct_ref[g] = _dot_nt(w2vt_ref[...], hid).astype(BF16)


def _compress(kvc, pos, w1, w2k, w2vt, k_gain):
    b, _, s, _ = kvc.shape
    nc = s // CMP_STRIDE
    return pl.pallas_call(
        _cmp_body, grid=(b,),
        in_specs=[pl.BlockSpec((None, 2, s, 128), lambda bi: (bi, 0, 0, 0)),
                  _resident(pos.shape), _resident(w1.shape), _resident(w2k.shape), _resident(w2vt.shape),
                  _resident((1, 64))],
        out_specs=(pl.BlockSpec((None, 2, nc, 64), lambda bi: (bi, 0, 0, 0)),
                   pl.BlockSpec((None, 2, 64, nc), lambda bi: (bi, 0, 0, 0))),
        out_shape=(jax.ShapeDtypeStruct((b, 2, nc, 64), BF16), jax.ShapeDtypeStruct((b, 2, 64, nc), BF16)),
        compiler_params=_cparams(("parallel",)), name="nsa_compress",
    )(kvc, pos, w1, w2k, w2vt, k_gain.reshape(1, 64))


def _compress_weights(pos_k, pos_v, w1_k, w1_v):
    def pos_part(p):
        p = p.reshape(2, CMP_STRIDE, 1, 64)
        return jnp.broadcast_to(p, (2, CMP_STRIDE, NSA_GROUPS, 64)).reshape(2, 1, CMP_STRIDE * 128)

    def w1_part(w):
        hdim = w.shape[1]
        w = w.reshape(2, CMP_STRIDE, 1, 64, hdim)
        per_group = []
        for g in range(NSA_GROUPS):
            pads = [w if gg == g else jnp.zeros_like(w) for gg in range(NSA_GROUPS)]
            per_group.append(jnp.concatenate(pads, axis=2).reshape(2, CMP_STRIDE * 128, hdim))
        return jnp.stack(per_group)

    pos = jnp.stack([pos_part(pos_k), pos_part(pos_v)])
    w1 = jnp.stack([w1_part(w1_k), w1_part(w1_v)]).astype(BF16)
    return pos, w1


CMP_CLASS_ROWS = 128


def _cmpsel_variant(nc, nblk, fixed_reference, m0_ref, qt_ref, kc_ref, vct_ref, oc_ref, bias_ref, s_scr, flag_scr):
    tq = qt_ref.shape[1]
    t0 = pl.program_id(2) * tq
    n_idx = lax.broadcasted_iota(jnp.int32, (nc, tq), 0)
    t_idx = t0 + lax.broadcasted_iota(jnp.int32, (nc, tq), 1)
    mask_bias = jnp.where(n_idx * CMP_STRIDE + (CMP_BLOCK - 1) <= t_idx, 0.0, NEG)
    sees_any = t0 + lax.broadcasted_iota(jnp.int32, (1, tq), 1) >= CMP_BLOCK - 1
    kc = kc_ref[0:nc, :]
    if fixed_reference:
        mask_bias = mask_bias - m0_ref[0]
    else:
        for hh in range(NSA_HPG):
            s_scr[hh, 0:nc, :] = _dot(kc, qt_ref[hh * 64:(hh + 1) * 64, :]) + mask_bias
    jj = lax.broadcasted_iota(jnp.int32, (nblk, nc), 0)
    nn = lax.broadcasted_iota(jnp.int32, (nblk, nc), 1)
    ov = jnp.where((nn * CMP_STRIDE < jj * SLC_BLOCK + SLC_BLOCK)
                   & (nn * CMP_STRIDE + CMP_BLOCK > jj * SLC_BLOCK), 1.0, 0.0).astype(BF16)
    ones_rows = (lax.broadcasted_iota(jnp.int32, (V_ROWS - 64, nc), 0) == 0).astype(BF16)
    lhs = jnp.concatenate([vct_ref[:, 0:nc], ones_rows, ov], axis=0)
    imp = jnp.zeros((nblk, tq), F32)
    l_min = jnp.full((1, tq), 1.0, F32)

    def probabilities(hh):
        if fixed_reference:
            return jnp.exp2(_dot(kc, qt_ref[hh * 64:(hh + 1) * 64, :]) + mask_bias).astype(BF16)
        m = jnp.max(s_scr[hh, 0:nc, :], axis=0, keepdims=True)
        return jnp.exp2(s_scr[hh, 0:nc, :] - m).astype(BF16)

    p_next = probabilities(0)
    for hh in range(NSA_HPG):
        p = p_next
        if hh + 1 < NSA_HPG:
            p_next = probabilities(hh + 1)
        r = _dot(lhs, p)
        l_min = jnp.minimum(l_min, jnp.where(sees_any, r[64:65], 1.0))
        inv_l = jnp.where(sees_any, 1.0 / r[64:65], 0.0)
        oc_ref[hh * 64:(hh + 1) * 64, :] = r[0:64] * inv_l
        imp = imp + r[V_ROWS:] * inv_l
    if fixed_reference:
        flag_scr[0] = jnp.where(jnp.min(l_min) > MIN_DENOMINATOR, 0, 1)

    j = lax.broadcasted_iota(jnp.int32, (nblk, tq), 0)
    cur = (t0 + lax.broadcasted_iota(jnp.int32, (nblk, tq), 1)) >> SLC_SHIFT
    forced = (j == 0) | (j == cur) | (j == cur - 1)
    bias = jnp.where(forced & (j <= cur), 0.0, NEG)
    imp = jnp.where((j <= cur) & jnp.logical_not(forced), imp, -1.0)
    jf = j.astype(F32)
    for _ in range(SLC_TOPK - 3):
        v = jnp.max(imp, axis=0, keepdims=True)
        first = jnp.min(jnp.where(imp == v, jf, float(nblk)), axis=0, keepdims=True)
        pick = jf == first
        bias = jnp.where(pick & (v >= 0.0), 0.0, bias)
        imp = jnp.where(pick, -3e38, imp)
    bias_ref[0:nblk, :] = bias
    if nblk < bias_ref.shape[0]:
        bias_ref[nblk:, :] = jnp.full((bias_ref.shape[0] - nblk, tq), NEG, F32)


def _cmpsel_body(m0_ref, qt_ref, kc_ref, vct_ref, oc_ref, bias_ref, s_scr, flag_scr):
    tq = qt_ref.shape[1]
    nc_total = kc_ref.shape[0]
    tiles_per_class = CMP_CLASS_ROWS // (tq // CMP_STRIDE)
    cls = pl.program_id(2) // tiles_per_class
    refs = (m0_ref, qt_ref, kc_ref, vct_ref, oc_ref, bias_ref, s_scr, flag_scr)
    for c in range(nc_total // CMP_CLASS_ROWS):
        nc = (c + 1) * CMP_CLASS_ROWS
        nblk = min(nc * CMP_STRIDE // SLC_BLOCK, bias_ref.shape[0])
        pl.when(cls == c)(functools.partial(_cmpsel_variant, nc, nblk, True, *refs))
    pl.when(flag_scr[0] != 0)(functools.partial(_cmpsel_variant, nc_total, bias_ref.shape[0], False, *refs))


def _cmp_select(qt, kc, vct, score_bound, *, tq=TOK_TILE):
    b, _, s = qt.shape
    nc = kc.shape[2]
    nblk = MAX_BLOCKS
    assert s // SLC_BLOCK <= MAX_BLOCKS and s // SLC_BLOCK >= SLC_TOPK and nc % CMP_CLASS_ROWS == 0
    return pl.pallas_call(
        _cmpsel_body, grid=(b, NSA_GROUPS, s // tq),
        in_specs=[pl.BlockSpec(memory_space=pltpu.SMEM),
                  pl.BlockSpec((None, 256, tq), lambda bi, g, i: (bi, g, i)),
                  pl.BlockSpec((None, None, nc, 64), lambda bi, g, i: (bi, g, 0, 0)),
                  pl.BlockSpec((None, None, 64, nc), lambda bi, g, i: (bi, g, 0, 0))],
        out_specs=(pl.BlockSpec((None, 256, tq), lambda bi, g, i: (bi, g, i)),
                   pl.BlockSpec((None, None, nblk, tq), lambda bi, g, i: (bi, g, 0, i))),
        out_shape=(jax.ShapeDtypeStruct((b, 512, s), F32),
                   jax.ShapeDtypeStruct((b, NSA_GROUPS, nblk, s), F32)),
        scratch_shapes=[pltpu.VMEM((NSA_HPG, nc, tq), F32), pltpu.SMEM((1,), jnp.int32)],
        compiler_params=_cparams(("parallel", "parallel", "parallel")), name="nsa_compressed_select",
    )(score_bound, qt, kc, vct)


def _flash_step(s_ref, vt, m_ref, acc_ref):
    m_old = m_ref[...]
    m_new = jnp.maximum(m_old, jnp.max(s_ref[...], axis=0, keepdims=True))
    p = jnp.exp2(s_ref[...] - m_new)
    acc_ref[...] = jnp.exp2(m_old - m_new) * acc_ref[...] + _dot(vt, p.astype(BF16))
    m_ref[...] = m_new


SEL, WIN = 0, 1
BIAS_ROWS = 16


def _slcwin_body(m0_ref, qt_ref, bias_ref, kaug_ref, vt_ref, oc_ref, gt_ref, mb_ref, y_ref,
                 q_scr, m_scr, acc_scr, s_scr, p_scr):
    tq = qt_ref.shape[1]
    tk = vt_ref.shape[3]
    assert tq == tk and WINDOW == tk and tk == 8 * SLC_BLOCK
    diag = pl.program_id(2)
    m0 = m0_ref[0]

    sel_slots = (0, 1, 2, 3)
    qs, qw = sel_slots[0], 4
    zeros = jnp.zeros((64, tq), BF16)
    for slot in sel_slots + (qw,):
        for hh in range(NSA_HPG):
            q_scr[slot, hh, 0:64, :] = qt_ref[hh * 64:(hh + 1) * 64, :]
            q_scr[slot, hh, 64:128, :] = zeros

    def set_selection_bias(kt, slot=qs):
        rows = bias_ref[pl.ds(pl.multiple_of(kt * 8, 8), 8), :]
        b16 = jnp.concatenate([rows, jnp.zeros_like(rows)], axis=0).astype(BF16)
        for hh in range(NSA_HPG):
            q_scr[slot, hh, 64:64 + BIAS_ROWS, :] = b16

    def tiles_fixed_reference(tiles):
        chains = [(br, qslot, kt, mask_bias, hh) for br, qslot, kt, mask_bias in tiles for hh in range(NSA_HPG)]
        for c in range(len(chains) + 1):
            if c < len(chains):
                br, qslot, kt, mask_bias, hh = chains[c]
                s = _dot(kaug_ref[br, pl.ds(pl.multiple_of(kt * tk, tk), tk), :], q_scr[qslot, hh])
                if mask_bias is not None:
                    s = s + mask_bias()
                p_scr[c % 4] = jnp.exp2(s - m0).astype(BF16)
            if c >= 1:
                br, _, kt, _, hh = chains[c - 1]
                acc_scr[br, hh] = acc_scr[br, hh] + _dot(vt_ref[br, kt], p_scr[(c - 1) % 4])

    def tile_running_max(br, qslot, kt, mask_bias=None):
        k = kaug_ref[br, pl.ds(pl.multiple_of(kt * tk, tk), tk), :]
        for hh in range(NSA_HPG):
            s = _dot(k, q_scr[qslot, hh])
            s_scr[hh] = s if mask_bias is None else s + mask_bias()
        for hh in range(NSA_HPG):
            _flash_step(s_scr.at[hh], vt_ref[br, kt], m_scr.at[br, hh], acc_scr.at[br, hh])

    prev = jnp.maximum(diag - 1, 0)
    no_prev = jnp.where(diag == 0, NEG, 0.0)
    band_bias = lambda: mb_ref[1] + no_prev
    causal_bias = lambda: mb_ref[0]

    acc_scr[...] = jnp.zeros(acc_scr.shape, F32)

    def unmasked_run(first_tile, count):
        for n in range(count):
            set_selection_bias(first_tile + n, sel_slots[n])
        tiles_fixed_reference([(SEL, sel_slots[n], first_tile + n, None) for n in range(count)])

    def tile_quad(j, carry):
        unmasked_run(4 * j, 4)
        return carry

    lax.fori_loop(0, diag >> 2, tile_quad, 0)
    pl.when((diag & 2) != 0)(lambda: unmasked_run((diag >> 2) * 4, 2))
    pl.when((diag & 1) != 0)(lambda: unmasked_run(diag - 1, 1))

    set_selection_bias(diag, qs)
    tiles_fixed_reference([(WIN, qw, prev, band_bias), (SEL, qs, diag, causal_bias), (WIN, qw, diag, causal_bias)])

    denominators = acc_scr[:, :, 64:65, :]
    underflow = jnp.logical_not(jnp.min(denominators) > MIN_DENOMINATOR)

    @pl.when(underflow)
    def _():
        m_scr[...] = jnp.full(m_scr.shape, NEG, F32)
        acc_scr[...] = jnp.zeros(acc_scr.shape, F32)

        def full_tile(kt, carry):
            set_selection_bias(kt, qs)
            tile_running_max(SEL, qs, kt)
            return carry

        lax.fori_loop(0, diag, full_tile, 0)
        set_selection_bias(diag, qs)
        tile_running_max(WIN, qw, prev, band_bias)
        tile_running_max(SEL, qs, diag, causal_bias)
        tile_running_max(WIN, qw, diag, causal_bias)

    gt = gt_ref[...]
    for hh in range(NSA_HPG):
        o_s = acc_scr[SEL, hh, 0:64, :] * (1.0 / acc_scr[SEL, hh, 64:65, :])
        o_w = acc_scr[WIN, hh, 0:64, :] * (1.0 / acc_scr[WIN, hh, 64:65, :])
        y_ref[hh * 64:(hh + 1) * 64, :] = (gt[3 * hh:3 * hh + 1] * oc_ref[hh * 64:(hh + 1) * 64, :]
                                          + gt[3 * hh + 1:3 * hh + 2] * o_s
                                          + gt[3 * hh + 2:3 * hh + 3] * o_w)


def _score_bound(q_gain, k_gain):
    bound = HEAD_DIM * QK_SCALE_LOG2 * jnp.max(jnp.abs(q_gain)) * jnp.max(jnp.abs(k_gain))
    return (1.02 * bound).reshape(1).astype(F32)


def _slc_win(qt, bias, kaug, vt, oc, gt, score_bound):
    b, _, s = qt.shape
    nblk = bias.shape[2]
    ns, tk = vt.shape[3], vt.shape[5]
    tq = tk
    key_rel, t_rel = np.arange(tk)[:, None], np.arange(tq)[None, :]
    mask_bias = jnp.asarray(np.stack([np.where(key_rel <= t_rel, 0.0, NEG),
                                      np.where(t_rel + tk - key_rel < WINDOW, 0.0, NEG)]), F32)
    qblk = pl.BlockSpec((None, 256, tq), lambda bi, g, i: (bi, g, i))
    return pl.pallas_call(
        _slcwin_body, grid=(b, NSA_GROUPS, s // tq),
        in_specs=[pl.BlockSpec(memory_space=pltpu.SMEM), qblk,
                  pl.BlockSpec((None, None, nblk, tq), lambda bi, g, i: (bi, g, 0, i)),
                  pl.BlockSpec((None, None, 2, s, 128), lambda bi, g, i: (bi, g, 0, 0, 0)),
                  pl.BlockSpec((None, None, 2, ns, V_ROWS, tk), lambda bi, g, i: (bi, g, 0, 0, 0, 0)),
                  qblk,
                  pl.BlockSpec((None, None, GATE_ROWS, tq), lambda bi, g, i: (bi, g, 0, i)),
                  _resident((2, tk, tq))],
        out_specs=qblk,
        out_shape=jax.ShapeDtypeStruct((b, 512, s), F32),
        scratch_shapes=[pltpu.VMEM((5, NSA_HPG, 128, tq), BF16),
                        pltpu.VMEM((2, NSA_HPG, 1, tq), F32), pltpu.VMEM((2, NSA_HPG, V_ROWS, tq), F32),
                        pltpu.VMEM((NSA_HPG, tk, tq), F32), pltpu.VMEM((NSA_HPG, tk, tq), BF16)],
        compiler_params=_cparams(("parallel", "parallel", "arbitrary")), name="nsa_selected_window",
    )(score_bound, qt, bias, kaug, vt, oc, gt.reshape(b, NSA_GROUPS, GATE_ROWS, s), mask_bias)


def _hgrn_consts():
    c = HGRN_CHUNK
    t = np.arange(c)
    lower = (t[None, :] <= t[:, None]).astype(np.float32)
    rows = [lower]
    masks = []
    for half in HGRN_LEVELS:
        mid = (t // (2 * half)) * (2 * half) + half - 1
        if half < 8:
            rows.append(lower[mid])
        same = (t[:, None] // (2 * half)) == (t[None, :] // (2 * half))
        right = (t[:, None] & half) != 0
        left = (t[None, :] & half) == 0
        masks.append((same & right & left).astype(np.float32))
    masks.append(np.eye(c, dtype=np.float32))
    masks.append(((t[:, None] // HGRN_SUB == t[None, :] // HGRN_SUB) & (t[None, :] <= t[:, None])).astype(np.float32))
    mall = np.concatenate(rows, axis=0)
    lvl = np.stack([np.tile(mk.T, (1, HGRN_HEADS)) for mk in masks])
    bdm = np.kron(np.eye(HGRN_HEADS), np.ones((64, 64), np.float32))
    return jnp.asarray(mall, BF16), jnp.asarray(lvl, F32), jnp.asarray(bdm, F32), jnp.asarray(bdm / 64, BF16)


def _hgrn_body(hg_ref, lb_ref, og_ref, mall_ref, lvl_ref, bdm_ref, bdn_ref, y_ref, st_scr, attn_scr):
    c = HGRN_CHUNK
    w = HGRN_WIDTH

    @pl.when(pl.program_id(1) == 0)
    def _():
        st_scr[...] = jnp.zeros(st_scr.shape, F32)

    chunks = range(hg_ref.shape[0] // c)
    lb = lb_ref[...]
    lane = lax.broadcasted_iota(jnp.int32, (c, w), 1)
    head_masks = [(lane >> 6) == hh for hh in range(HGRN_HEADS)]
    nlev = len(HGRN_LEVELS)

    def stack_heads(x):
        x16 = x.astype(BF16)
        return jnp.concatenate([jnp.where(hm, x16, 0) for hm in head_masks], axis=0)

    def row_bcast(x, half):
        return jnp.concatenate([jnp.broadcast_to(x[p + half - 1:p + half, :], (2 * half, w))
                                for p in range(0, c, 2 * half)], axis=0)

    qa, kk, v16, logf = [], [], [], []
    for ci in chunks:
        rows = slice(ci * c, (ci + 1) * c)
        qa.append(_silu(hg_ref[rows, 0:w]) * (HEAD_DIM ** -0.5))
        fg = lb + (1.0 - lb) * _sigmoid(hg_ref[rows, w:2 * w])
        kk.append(1.0 - fg)
        logf.append(jnp.log2(fg))
        v16.append(hg_ref[rows, 2 * w:3 * w].astype(BF16))

    mall = mall_ref[...]
    parts = _split3(jnp.concatenate(logf, axis=1))
    b_all = sum(_dot(mall[0:c], part) for part in parts)
    bcum = [b_all[:, ci * w:(ci + 1) * w] for ci in chunks]

    def level(ci, li, ref_pt):
        e = jnp.exp2(-jnp.abs(bcum[ci] - ref_pt))
        return lvl_ref[li] * _dot_nt((kk[ci] * e).astype(BF16), stack_heads(qa[ci] * e))

    coarse = [li for li, half in enumerate(HGRN_LEVELS) if half >= HGRN_SUB]
    for li in coarse:
        for ci in chunks:
            contribution = level(ci, li, row_bcast(bcum[ci], HGRN_LEVELS[li]))
            attn_scr[ci] = contribution if li == coarse[0] else attn_scr[ci] + contribution

    def block_start(x):
        firsts = [jnp.zeros((HGRN_SUB, w), F32)]
        firsts += [jnp.broadcast_to(x[p - 1:p, :], (HGRN_SUB, w)) for p in range(HGRN_SUB, c, HGRN_SUB)]
        return jnp.concatenate(firsts, axis=0)

    expo = [block_start(bcum[ci]) - bcum[ci] for ci in chunks]
    largest = expo[0]
    for ci in chunks[1:]:
        largest = jnp.maximum(largest, expo[ci])
    single_reference_ok = jnp.max(largest) < HGRN_MAX_EXPONENT

    @pl.when(single_reference_ok)
    def _():
        for ci in chunks:
            kt = (kk[ci] * jnp.exp2(expo[ci])).astype(BF16)
            attn_scr[ci] = attn_scr[ci] + lvl_ref[nlev + 1] * _dot_nt(kt, stack_heads(qa[ci] * jnp.exp2(-expo[ci])))

    @pl.when(jnp.logical_not(single_reference_ok))
    def _():
        r_fine = sum(_dot(mall[c:], part) for part in parts)
        for ci in chunks:
            attn_scr[ci] = attn_scr[ci] + lvl_ref[nlev] * _dot_nt(kk[ci].astype(BF16), stack_heads(qa[ci]))
        fine = 0
        for li, half in enumerate(HGRN_LEVELS):
            if half >= HGRN_SUB:
                continue
            for ci in chunks:
                if half >= 8:
                    ref_pt = row_bcast(bcum[ci], half)
                else:
                    ref_pt = r_fine[fine * c:(fine + 1) * c, ci * w:(ci + 1) * w]
                attn_scr[ci] = attn_scr[ci] + level(ci, li, ref_pt)
            if half < 8:
                fine += 1

    attn = [attn_scr[ci] for ci in chunks]

    intra, upd, decay, qb = [], [], [], []
    for ci in chunks:
        x = _dot_tn(attn[ci].astype(BF16), v16[ci])
        intra.append(sum(jnp.where(head_masks[hh], x[hh * c:(hh + 1) * c], 0.0) for hh in range(HGRN_HEADS)))
        b_last = bcum[ci][c - 1:c, :]
        kl = (kk[ci] * jnp.exp2(b_last - bcum[ci])).astype(BF16)
        upd.append(bdm_ref[...] * _dot_tn(v16[ci], kl))
        decay.append(jnp.exp2(b_last))
        qb.append((qa[ci] * jnp.exp2(bcum[ci])).astype(BF16))

    st = st_scr[...]
    inter = []
    for ci in chunks:
        inter.append(_dot_nt(qb[ci], st.astype(BF16)))
        st = st * decay[ci] + upd[ci]
    st_scr[...] = st

    for ci in chunks:
        rows = slice(ci * c, (ci + 1) * c)
        o = inter[ci] + intra[ci]
        hi, lo = _split2(o * o)
        ms = _dot(hi, bdn_ref[...]) + _dot(lo, bdn_ref[...])
        y_ref[rows, :] = o * lax.rsqrt(ms + EPS) * og_ref[...] * _silu(hg_ref[rows, 3 * w:4 * w])


def _hgrn(hg, lower_bound, out_gain):
    b, s, _ = hg.shape
    rows = HGRN_CB * HGRN_CHUNK
    mall, lvl, bdm, bdn = _hgrn_consts()
    full = lambda shape: pl.BlockSpec(shape, lambda bi, i: (0,) * len(shape))
    return pl.pallas_call(
        _hgrn_body, grid=(b, s // rows),
        in_specs=[pl.BlockSpec((None, rows, 4 * HGRN_WIDTH), lambda bi, i: (bi, i, 0)),
                  full((1, HGRN_WIDTH)), full((1, HGRN_WIDTH)),
                  full(mall.shape), full(lvl.shape), full(bdm.shape), full(bdn.shape)],
        out_specs=pl.BlockSpec((None, rows, HGRN_WIDTH), lambda bi, i: (bi, i, 0)),
        out_shape=jax.ShapeDtypeStruct((b, s, HGRN_WIDTH), F32),
        scratch_shapes=[pltpu.VMEM((HGRN_WIDTH, HGRN_WIDTH), F32),
                        pltpu.VMEM((HGRN_CB, HGRN_CHUNK, HGRN_HEADS * HGRN_CHUNK), F32)],
        compiler_params=_cparams(("parallel", "arbitrary")), name="hgrn2_chunks",
    )(hg, lower_bound.reshape(1, -1), out_gain.reshape(1, -1), mall, lvl, bdm, bdn)


def _memkv_body(mem_ref, mg_ref, wk_ref, wvt_ref, kg_ref, kh_ref, vht_ref):
    m = mem_ref.shape[0]
    mn = _rms_rows(mem_ref[...], mg_ref[...]).astype(BF16)
    k = _dot(mn, wk_ref[...])
    vt = _dot_nt(wvt_ref[...], mn)
    ones_rows = (lax.broadcasted_iota(jnp.int32, (V_ROWS - 64, m), 0) == 0).astype(BF16)
    for hh in range(MEM_HEADS):
        kh_ref[hh] = _rms_rows(k[:, hh * 64:(hh + 1) * 64], kg_ref[...]).astype(BF16)
        vht_ref[hh, 0:64, :] = vt[hh * 64:(hh + 1) * 64].astype(BF16)
        vht_ref[hh, 64:V_ROWS, :] = ones_rows


def _mem_kv(mem, mem_gain, wk, wvt, k_gain):
    b, m, d = mem.shape
    full = lambda shape: pl.BlockSpec(shape, lambda bi: (0,) * len(shape))
    return pl.pallas_call(
        _memkv_body, grid=(b,),
        in_specs=[pl.BlockSpec((None, m, d), lambda bi: (bi, 0, 0)), full((1, d)),
                  full(wk.shape), full(wvt.shape), full((1, 64))],
        out_specs=(pl.BlockSpec((None, MEM_HEADS, m, 64), lambda bi: (bi, 0, 0, 0)),
                   pl.BlockSpec((None, MEM_HEADS, V_ROWS, m), lambda bi: (bi, 0, 0, 0))),
        out_shape=(jax.ShapeDtypeStruct((b, MEM_HEADS, m, 64), BF16),
                   jax.ShapeDtypeStruct((b, MEM_HEADS, V_ROWS, m), BF16)),
        compiler_params=_cparams(("parallel",)), name="memory_kv",
    )(mem, mem_gain.reshape(1, d), wk, wvt, k_gain.reshape(1, 64))


def _out_body(x_ref, ynt_ref, yh_ref, qmt_ref, kh_ref, vht_ref, ng_ref, mg_ref, wo_ref,
              fg_ref, wg_ref, wu_ref, wd_ref, o_ref, a_scr):
    y_mem = []
    for hh in range(MEM_HEADS):
        s = _dot(kh_ref[hh], qmt_ref[hh * 64:(hh + 1) * 64, :])
        p = jnp.exp2(s - jnp.max(s, axis=0, keepdims=True))
        o = _dot(vht_ref[hh], p.astype(BF16))
        y_mem.append(o[0:64] * (1.0 / o[64:65]))
    mem = _rms_cols(jnp.concatenate(y_mem, axis=0), mg_ref[...]).astype(BF16)
    nsa = _rms_cols(ynt_ref[...], ng_ref[...]).astype(BF16)
    acc = _dot_tn(nsa, wo_ref[0:NSA_WIDTH, :])
    acc = acc + _dot(yh_ref[...].astype(BF16), wo_ref[NSA_WIDTH:NSA_WIDTH + HGRN_WIDTH, :])
    acc = acc + _dot_tn(mem, wo_ref[NSA_WIDTH + HGRN_WIDTH:, :])
    o_ref[...] = _ffn_half_step(x_ref[...] + acc, fg_ref, wg_ref, wu_ref, wd_ref, a_scr)


def _out_ffn(x3d, ynt, yh, qmt, kh, vht, nsa_gain, mem_gain, wo, ffn_gain, wg, wu, wd, *, tm=TOK_TILE):
    b, s, d = x3d.shape
    m = kh.shape[2]
    return pl.pallas_call(
        _out_body, grid=(b, s // tm),
        in_specs=[pl.BlockSpec((None, tm, d), lambda bi, i: (bi, i, 0)),
                  pl.BlockSpec((None, NSA_WIDTH, tm), lambda bi, i: (bi, 0, i)),
                  pl.BlockSpec((None, tm, HGRN_WIDTH), lambda bi, i: (bi, i, 0)),
                  pl.BlockSpec((None, MEM_WIDTH, tm), lambda bi, i: (bi, 0, i)),
                  pl.BlockSpec((None, MEM_HEADS, m, 64), lambda bi, i: (bi, 0, 0, 0)),
                  pl.BlockSpec((None, MEM_HEADS, V_ROWS, m), lambda bi, i: (bi, 0, 0, 0)),
                  _resident((NSA_WIDTH, 1)), _resident((MEM_WIDTH, 1)), _resident(wo.shape),
                  _resident((1, d)), _resident(wg.shape), _resident(wu.shape), _resident(wd.shape)],
        out_specs=pl.BlockSpec((None, tm, d), lambda bi, i: (bi, i, 0)),
        out_shape=jax.ShapeDtypeStruct((b, s, d), F32),
        scratch_shapes=[pltpu.VMEM((tm, wg.shape[1]), BF16)],
        compiler_params=_cparams(("parallel", "parallel")), name="mix_out_ffn2",
    )(x3d, ynt, yh, qmt, kh, vht, nsa_gain.reshape(-1, 1), mem_gain.reshape(-1, 1), wo,
      ffn_gain.reshape(1, d), wg, wu, wd)


def _layer(x, mem, ffn1, ffn2, mix_norm, w_in, w_out, nsa_q_norm, nsa_k_norm, cmp_pos_k, cmp_w1_k, cmp_w2_k,
           cmp_pos_v, cmp_w1_v, cmp_w2_v, nsa_out_norm, lower_bound, hgrn_out_norm,
           mem_norm, mem_w_k, mem_w_v, mem_q_norm, mem_k_norm, mem_out_norm):
    b, s, d = x.shape
    sizes = (512, 128, 128, 128, 128, 128, 128, 24, 256, 256, 256, 256, 256)
    offs = np.concatenate([[0], np.cumsum(sizes)])
    col = lambda i: w_in[:, offs[i]:offs[i + 1]]
    (q_a, k_c, v_c, k_s, v_s, k_w, v_w, g_a, q_h, f_h, i_h, g_h, q_m) = [col(i) for i in range(13)]
    gpad = jnp.zeros((d, GATE_ROWS - 3 * NSA_HPG), w_in.dtype)
    wt = jnp.concatenate([q_a, v_s, v_w, g_a[:, :3 * NSA_HPG], gpad, g_a[:, 3 * NSA_HPG:], gpad, q_m],
                         axis=1).T.astype(BF16)
    wn = jnp.concatenate([k_c, v_c, k_s, k_w, q_h, f_h, i_h, g_h], axis=1).astype(BF16)

    x1, qt, vt, gt, qmt, kaug, kvc, hg = _ffn_proj(
        x, *ffn1, mix_norm, wt, wn, nsa_q_norm, nsa_k_norm, mem_q_norm, _rope_tables(s))

    cmp_pos, cmp_w1 = _compress_weights(cmp_pos_k, cmp_pos_v, cmp_w1_k, cmp_w1_v)
    kc, vct = _compress(kvc, cmp_pos, cmp_w1, cmp_w2_k.astype(BF16), cmp_w2_v.T.astype(BF16), nsa_k_norm)
    score_bound = _score_bound(nsa_q_norm, nsa_k_norm)
    oc, bias = _cmp_select(qt, kc, vct, score_bound)
    y_nsa = _slc_win(qt, bias, kaug, vt, oc, gt, score_bound)

    y_hgrn = _hgrn(hg, lower_bound, hgrn_out_norm)

    kh, vht = _mem_kv(mem, mem_norm, mem_w_k.astype(BF16), mem_w_v.T.astype(BF16), mem_k_norm)
    return _out_ffn(x1, y_nsa, y_hgrn, qmt, kh, vht, nsa_out_norm, mem_out_norm, w_out.astype(BF16), *ffn2)


def kernel(x, mem, ffn1_norm, ffn1_w_gate, ffn1_w_up, ffn1_w_down, mix_norm, w_in, w_out, nsa_q_norm, nsa_k_norm, cmp_pos_k, cmp_w1_k, cmp_w2_k, cmp_pos_v, cmp_w1_v, cmp_w2_v, nsa_out_norm, hgrn_lb_logits, hgrn_out_norm, mem_norm, mem_w_k, mem_w_v, mem_q_norm, mem_k_norm, mem_out_norm, ffn2_norm, ffn2_w_gate, ffn2_w_up, ffn2_w_down):
    b, s, d = x.shape
    depth = ffn1_norm.shape[0]
    lower_bounds = jnp.cumsum(jax.nn.softmax(hgrn_lb_logits.astype(F32), axis=0), axis=0)
    bf = lambda a: a.astype(BF16)
    for l in range(depth):
        x = _layer(x, mem, (ffn1_norm[l], bf(ffn1_w_gate[l]), bf(ffn1_w_up[l]), bf(ffn1_w_down[l])),
                   (ffn2_norm[l], bf(ffn2_w_gate[l]), bf(ffn2_w_up[l]), bf(ffn2_w_down[l])),
                   mix_norm[l], w_in[l], w_out[l], nsa_q_norm[l], nsa_k_norm[l],
                   cmp_pos_k[l], cmp_w1_k[l], cmp_w2_k[l], cmp_pos_v[l], cmp_w1_v[l], cmp_w2_v[l],
                   nsa_out_norm[l], lower_bounds[l], hgrn_out_norm[l],
                   mem_norm[l], mem_w_k[l], mem_w_v[l], mem_q_norm[l], mem_k_norm[l], mem_out_norm[l])
    return x
```

```python
import functools

import numpy as np
import jax
import jax.numpy as jnp
from jax import lax
from jax.experimental import pallas as pl
from jax.experimental.pallas import tpu as pltpu

F32 = jnp.float32
BF16 = jnp.bfloat16

HEAD_DIM = 64
ROT_DIM = 16
ROT_HALF = 8
ROPE_THETA = 500000.0
NSA_HEADS = 8
NSA_GROUPS = 2
NSA_HPG = 4
CMP_BLOCK = 32
CMP_STRIDE = 16
SLC_BLOCK = 64
SLC_SHIFT = 6
SLC_TOPK = 16
WINDOW = 512
FORCED_SCORE = 1e4
HGRN_HEADS = 4
HGRN_CHUNK = 64
HGRN_WIDTH = 256
MEM_HEADS = 4
MEM_WIDTH = 256
NSA_WIDTH = 512
EPS = 1e-6
NEG = -1e30
QK_SCALE_LOG2 = HEAD_DIM ** -0.5 * 1.4426950408889634
MIN_DENOMINATOR = 2.0 ** -64

VMEM_LIMIT = 56 * 1024 * 1024
MAX_BLOCKS = 128
GATE_ROWS = 16
V_ROWS = 80
TOK_TILE = 512
HGRN_CB = 8
HGRN_LEVELS = (32, 16, 8, 4, 2, 1)
HGRN_SUB = 64
HGRN_MAX_EXPONENT = 96.0

NT_DIMS = (((1,), (1,)), ((), ()))
TN_DIMS = (((0,), (0,)), ((), ()))


def _cparams(sem):
    return pltpu.CompilerParams(dimension_semantics=sem, vmem_limit_bytes=VMEM_LIMIT)


def _dot(a, b):
    return jnp.dot(a, b, preferred_element_type=F32)


def _dot_nt(a, b):
    return lax.dot_general(a, b, NT_DIMS, preferred_element_type=F32)


def _dot_tn(a, b):
    return lax.dot_general(a, b, TN_DIMS, preferred_element_type=F32)


def _sigmoid(x):
    return 1.0 / (1.0 + jnp.exp(-x))


def _silu(x):
    return x * _sigmoid(x)


def _split2(x):
    hi = x.astype(BF16)
    lo = (x - hi.astype(F32)).astype(BF16)
    return hi, lo


def _split3(x):
    hi = x.astype(BF16)
    r1 = x - hi.astype(F32)
    mid = r1.astype(BF16)
    lo = (r1 - mid.astype(F32)).astype(BF16)
    return hi, mid, lo


def _rms_rows(x, gain_row):
    ms = jnp.mean(x * x, axis=-1, keepdims=True)
    return x * lax.rsqrt(ms + EPS) * gain_row


def _rms_cols(x, gain_col):
    ms = jnp.mean(x * x, axis=0, keepdims=True)
    return x * lax.rsqrt(ms + EPS) * gain_col


def _seg_mean_sq(x, bd):
    hi, lo = _split2(x * x)
    return _dot(hi, bd) + _dot(lo, bd)


FFN_CHUNK = 256


def _ffn_half_step(x, g_ref, wg_ref, wu_ref, wd_ref, a_scr):
    xn = _rms_rows(x, g_ref[...]).astype(BF16)
    d_ff = wg_ref.shape[1]
    for c in range(d_ff // FFN_CHUNK):
        sl = slice(c * FFN_CHUNK, (c + 1) * FFN_CHUNK)
        g = _dot(xn, wg_ref[:, sl])
        u = _dot(xn, wu_ref[:, sl])
        a_scr[:, sl] = (_silu(g) * u).astype(BF16)
    return x + 0.5 * _dot(a_scr[...], wd_ref[...])


def _resident(shape):
    return pl.BlockSpec(shape, lambda *_: (0,) * len(shape), pipeline_mode=pl.Buffered(1))


def _rope_cols(xn, cos, sin):
    x0, x1 = xn[0:ROT_HALF], xn[ROT_HALF:ROT_DIM]
    return jnp.concatenate([x0 * cos - x1 * sin, x1 * cos + x0 * sin, xn[ROT_DIM:]], axis=0)


def _rope_rows(x, cn, sa, sb):
    return x * cn + pltpu.roll(x, 128 - ROT_HALF, 1) * sa + pltpu.roll(x, ROT_HALF, 1) * sb


def _proj_body(x_ref, fg_ref, wg_ref, wu_ref, wd_ref, mg_ref, wt_ref, wn_ref, qg_ref, kg_ref, mqg_ref,
               cos_ref, sin_ref, cn_ref, sa_ref, sb_ref, bd_ref,
               x1_ref, qt_ref, vt_ref, gt_ref, qmt_ref, kaug_ref, kvc_ref, hg_ref, a_scr):
    tm = x_ref.shape[0]
    assert tm == 8 * SLC_BLOCK
    x1 = _ffn_half_step(x_ref[...], fg_ref, wg_ref, wu_ref, wd_ref, a_scr)
    x1_ref[...] = x1
    h = _rms_rows(x1, mg_ref[...]).astype(BF16)

    qg, mqg = qg_ref[...], mqg_ref[...]
    half = tm // 2
    ones_rows = (lax.broadcasted_iota(jnp.int32, (V_ROWS - 64, half), 0) == 0).astype(BF16)
    for part in range(2):
        tok = slice(part * half, (part + 1) * half)
        pt = _dot_nt(wt_ref[...], h[tok, :])
        cos, sin = cos_ref[:, tok], sin_ref[:, tok]
        for hh in range(NSA_HEADS):
            xq = _rms_cols(pt[hh * 64:(hh + 1) * 64], qg)
            qt_ref[hh * 64:(hh + 1) * 64, tok] = (_rope_cols(xq, cos, sin) * QK_SCALE_LOG2).astype(BF16)
        for g in range(NSA_GROUPS):
            for br in range(2):
                rows = 512 + br * 128 + g * 64
                vt_ref[g, br, 0:64, tok] = pt[rows:rows + 64].astype(BF16)
                vt_ref[g, br, 64:V_ROWS, tok] = ones_rows
        gt_ref[:, tok] = _sigmoid(pt[768:800])
        for hh in range(MEM_HEADS):
            xm = _rms_cols(pt[800 + hh * 64:864 + hh * 64], mqg)
            qmt_ref[hh * 64:(hh + 1) * 64, tok] = (xm * QK_SCALE_LOG2).astype(BF16)

    cn, sa, sb = cn_ref[...], sa_ref[...], sb_ref[...]
    bd, kg = bd_ref[...], kg_ref[...]
    pc = _dot(h, wn_ref[:, 0:256])
    pk = _dot(h, wn_ref[:, 256:512])
    kvc_ref[0] = _rope_rows(pc[:, 0:128], cn, sa, sb)
    kvc_ref[1] = pc[:, 128:256]
    ph = _dot(h, wn_ref[:, 512:1024])
    ks = pk[:, 0:128]
    kw = pk[:, 128:256]
    ks = _rope_rows(ks * lax.rsqrt(_seg_mean_sq(ks, bd) + EPS) * kg, cn, sa, sb)
    kw = _rope_rows(kw * lax.rsqrt(_seg_mean_sq(kw, bd) + EPS) * kg, cn, sa, sb)
    lane = lax.broadcasted_iota(jnp.int32, (tm, 128), 1)
    row = lax.broadcasted_iota(jnp.int32, (tm, 128), 0)
    onehot = jnp.where(lane - 64 == (row >> SLC_SHIFT), 1.0, 0.0)
    lo_half = lane < 64
    kaug_ref[0, 0] = jnp.where(lo_half, ks, onehot).astype(BF16)
    kaug_ref[0, 1] = jnp.where(lo_half, kw, 0.0).astype(BF16)
    kaug_ref[1, 0] = jnp.where(lo_half, pltpu.roll(ks, 64, 1), onehot).astype(BF16)
    kaug_ref[1, 1] = jnp.where(lo_half, pltpu.roll(kw, 64, 1), 0.0).astype(BF16)
    hg_ref[:, 0:512] = ph
    hg_ref[:, 512:1024] = _dot(h, wn_ref[:, 1024:1536])


def _ffn_proj(x3d, ffn_gain, wg, wu, wd, mix_gain, wt, wn, q_gain, k_gain, mq_gain, rope):
    b, s, d = x3d.shape
    tm = TOK_TILE
    ns = s // tm
    cos_t, sin_t, cn, sa, sb = rope
    bd = jnp.asarray(np.kron(np.eye(2), np.full((64, 64), 1.0 / 64)), BF16)
    full = _resident
    out_shape = (
        jax.ShapeDtypeStruct((b, s, d), F32),
        jax.ShapeDtypeStruct((b, 512, s), BF16),
        jax.ShapeDtypeStruct((b, 2, 2, ns, V_ROWS, tm), BF16),
        jax.ShapeDtypeStruct((b, 32, s), F32),
        jax.ShapeDtypeStruct((b, 256, s), BF16),
        jax.ShapeDtypeStruct((b, 2, 2, s, 128), BF16),
        jax.ShapeDtypeStruct((b, 2, s, 128), F32),
        jax.ShapeDtypeStruct((b, s, 1024), F32),
    )
    out_specs = (
        pl.BlockSpec((None, tm, d), lambda bi, i: (bi, i, 0)),
        pl.BlockSpec((None, 512, tm), lambda bi, i: (bi, 0, i)),
        pl.BlockSpec((None, 2, 2, None, V_ROWS, tm), lambda bi, i: (bi, 0, 0, i, 0, 0)),
        pl.BlockSpec((None, 32, tm), lambda bi, i: (bi, 0, i)),
        pl.BlockSpec((None, 256, tm), lambda bi, i: (bi, 0, i)),
        pl.BlockSpec((None, 2, 2, tm, 128), lambda bi, i: (bi, 0, 0, i, 0)),
        pl.BlockSpec((None, 2, tm, 128), lambda bi, i: (bi, 0, i, 0)),
        pl.BlockSpec((None, tm, 1024), lambda bi, i: (bi, i, 0)),
    )
    in_specs = [
        pl.BlockSpec((None, tm, d), lambda bi, i: (bi, i, 0)),
        full((1, d)), full(wg.shape), full(wu.shape), full(wd.shape),
        full((1, d)), full(wt.shape), full(wn.shape),
        full((64, 1)), full((1, 128)), full((64, 1)),
        pl.BlockSpec((ROT_HALF, tm), lambda bi, i: (0, i)),
        pl.BlockSpec((ROT_HALF, tm), lambda bi, i: (0, i)),
        pl.BlockSpec((tm, 128), lambda bi, i: (i, 0)),
        pl.BlockSpec((tm, 128), lambda bi, i: (i, 0)),
        pl.BlockSpec((tm, 128), lambda bi, i: (i, 0)),
        full((128, 128)),
    ]
    return pl.pallas_call(
        _proj_body, grid=(b, ns), in_specs=in_specs, out_specs=out_specs, out_shape=out_shape,
        scratch_shapes=[pltpu.VMEM((tm, wg.shape[1]), BF16)],
        compiler_params=_cparams(("parallel", "parallel")), name="ffn1_mix_projection",
    )(x3d, ffn_gain.reshape(1, d), wg, wu, wd, mix_gain.reshape(1, d), wt, wn, q_gain.reshape(64, 1),
      jnp.tile(k_gain.reshape(1, 64), (1, 2)), mq_gain.reshape(64, 1), cos_t, sin_t, cn, sa, sb, bd)


def _rope_tables(s):
    pos = jnp.arange(s, dtype=F32)
    inv = ROPE_THETA ** (-(jnp.arange(0, ROT_DIM, 2, dtype=F32) / ROT_DIM))
    ang = pos[:, None] * inv[None, :]
    cos, sin = jnp.cos(ang), jnp.sin(ang)
    zeros = jnp.zeros((s, 64 - ROT_DIM), F32)
    cn = jnp.concatenate([cos, cos, jnp.ones((s, 64 - ROT_DIM), F32)], axis=1)
    sa = jnp.concatenate([-sin, jnp.zeros((s, ROT_HALF), F32), zeros], axis=1)
    sb = jnp.concatenate([jnp.zeros((s, ROT_HALF), F32), sin, zeros], axis=1)
    tile2 = lambda a: jnp.concatenate([a, a], axis=1)
    return cos.T, sin.T, tile2(cn), tile2(sa), tile2(sb)


def _cmp_body(kvc_ref, pos_ref, w1_ref, w2k_ref, w2vt_ref, kg_ref, kc_ref, vct_ref):
    nc = kvc_ref.shape[1] // CMP_STRIDE
    for kind in range(2):
        halves = []
        for part in range(2):
            x = jnp.concatenate(
                [(kvc_ref[kind, pl.ds(r, nc, stride=CMP_STRIDE), :]
                  + pos_ref[kind, part, :, r * 128:(r + 1) * 128]).astype(BF16) for r in range(CMP_STRIDE)],
                axis=1)
            halves.append(x)
        for g in range(NSA_GROUPS):
            second = _dot(halves[1], w1_ref[kind, g, 1])
            hid = _silu(_dot(halves[0], w1_ref[kind, g, 0]) + pltpu.roll(second, nc - 1, 0)).astype(BF16)
            if kind == 0:
                kc_ref[g] = _rms_rows(_dot(hid, w2k_ref[...]), kg_ref[...]).astype(BF16)
            else:
                vct_ref[g] = _dot_nt(w2vt_ref[...], hid).astype(BF16)


def _compress(kvc, pos, w1, w2k, w2vt, k_gain):
    b, _, s, _ = kvc.shape
    nc = s // CMP_STRIDE
    return pl.pallas_call(
        _cmp_body, grid=(b,),
        in_specs=[pl.BlockSpec((None, 2, s, 128), lambda bi: (bi, 0, 0, 0)),
                  _resident(pos.shape), _resident(w1.shape), _resident(w2k.shape), _resident(w2vt.shape),
                  _resident((1, 64))],
        out_specs=(pl.BlockSpec((None, 2, nc, 64), lambda bi: (bi, 0, 0, 0)),
                   pl.BlockSpec((None, 2, 64, nc), lambda bi: (bi, 0, 0, 0))),
        out_shape=(jax.ShapeDtypeStruct((b, 2, nc, 64), BF16), jax.ShapeDtypeStruct((b, 2, 64, nc), BF16)),
        compiler_params=_cparams(("parallel",)), name="nsa_compress",
    )(kvc, pos, w1, w2k, w2vt, k_gain.reshape(1, 64))


def _compress_weights(pos_k, pos_v, w1_k, w1_v):
    def pos_part(p):
        p = p.reshape(2, CMP_STRIDE, 1, 64)
        return jnp.broadcast_to(p, (2, CMP_STRIDE, NSA_GROUPS, 64)).reshape(2, 1, CMP_STRIDE * 128)

    def w1_part(w):
        hdim = w.shape[1]
        w = w.reshape(2, CMP_STRIDE, 1, 64, hdim)
        per_group = []
        for g in range(NSA_GROUPS):
            pads = [w if gg == g else jnp.zeros_like(w) for gg in range(NSA_GROUPS)]
            per_group.append(jnp.concatenate(pads, axis=2).reshape(2, CMP_STRIDE * 128, hdim))
        return jnp.stack(per_group)

    pos = jnp.stack([pos_part(pos_k), pos_part(pos_v)])
    w1 = jnp.stack([w1_part(w1_k), w1_part(w1_v)]).astype(BF16)
    return pos, w1


CMP_CLASS_ROWS = 128


def _cmpsel_variant(nc, nblk, fixed_reference, m0_ref, qt_ref, kc_ref, vct_ref, oc_ref, bias_ref, s_scr, flag_scr):
    tq = qt_ref.shape[1]
    t0 = pl.program_id(2) * tq
    n_idx = lax.broadcasted_iota(jnp.int32, (nc, tq), 0)
    t_idx = t0 + lax.broadcasted_iota(jnp.int32, (nc, tq), 1)
    mask_bias = jnp.where(n_idx * CMP_STRIDE + (CMP_BLOCK - 1) <= t_idx, 0.0, NEG)
    sees_any = t0 + lax.broadcasted_iota(jnp.int32, (1, tq), 1) >= CMP_BLOCK - 1
    kc = kc_ref[0:nc, :]
    if fixed_reference:
        mask_bias = mask_bias - m0_ref[0]
    else:
        for hh in range(NSA_HPG):
            s_scr[hh, 0:nc, :] = _dot(kc, qt_ref[hh * 64:(hh + 1) * 64, :]) + mask_bias
    jj = lax.broadcasted_iota(jnp.int32, (nblk, nc), 0)
    nn = lax.broadcasted_iota(jnp.int32, (nblk, nc), 1)
    ov = jnp.where((nn * CMP_STRIDE < jj * SLC_BLOCK + SLC_BLOCK)
                   & (nn * CMP_STRIDE + CMP_BLOCK > jj * SLC_BLOCK), 1.0, 0.0).astype(BF16)
    ones_rows = (lax.broadcasted_iota(jnp.int32, (V_ROWS - 64, nc), 0) == 0).astype(BF16)
    lhs = jnp.concatenate([vct_ref[:, 0:nc], ones_rows, ov], axis=0)
    imp = jnp.zeros((nblk, tq), F32)
    l_min = jnp.full((1, tq), 1.0, F32)

    def probabilities(hh):
        if fixed_reference:
            return jnp.exp2(_dot(kc, qt_ref[hh * 64:(hh + 1) * 64, :]) + mask_bias).astype(BF16)
        m = jnp.max(s_scr[hh, 0:nc, :], axis=0, keepdims=True)
        return jnp.exp2(s_scr[hh, 0:nc, :] - m).astype(BF16)

    p_next = probabilities(0)
    for hh in range(NSA_HPG):
        p = p_next
        if hh + 1 < NSA_HPG:
            p_next = probabilities(hh + 1)
        r = _dot(lhs, p)
        l_min = jnp.minimum(l_min, jnp.where(sees_any, r[64:65], 1.0))
        inv_l = jnp.where(sees_any, 1.0 / r[64:65], 0.0)
        oc_ref[hh * 64:(hh + 1) * 64, :] = r[0:64] * inv_l
        imp = imp + r[V_ROWS:] * inv_l
    if fixed_reference:
        flag_scr[0] = jnp.where(jnp.min(l_min) > MIN_DENOMINATOR, 0, 1)

    j = lax.broadcasted_iota(jnp.int32, (nblk, tq), 0)
    cur = (t0 + lax.broadcasted_iota(jnp.int32, (nblk, tq), 1)) >> SLC_SHIFT
    forced = (j == 0) | (j == cur) | (j == cur - 1)
    bias = jnp.where(forced & (j <= cur), 0.0, NEG)
    imp = jnp.where((j <= cur) & jnp.logical_not(forced), imp, -1.0)
    jf = j.astype(F32)
    for _ in range(SLC_TOPK - 3):
        v = jnp.max(imp, axis=0, keepdims=True)
        first = jnp.min(jnp.where(imp == v, jf, float(nblk)), axis=0, keepdims=True)
        pick = jf == first
        bias = jnp.where(pick & (v >= 0.0), 0.0, bias)
        imp = jnp.where(pick, -3e38, imp)
    bias_ref[0:nblk, :] = bias
    if nblk < bias_ref.shape[0]:
        bias_ref[nblk:, :] = jnp.full((bias_ref.shape[0] - nblk, tq), NEG, F32)


def _cmpsel_body(m0_ref, qt_ref, kc_ref, vct_ref, oc_ref, bias_ref, s_scr, flag_scr):
    tq = qt_ref.shape[1]
    nc_total = kc_ref.shape[0]
    tiles_per_class = CMP_CLASS_ROWS // (tq // CMP_STRIDE)
    cls = pl.program_id(2) // tiles_per_class
    refs = (m0_ref, qt_ref, kc_ref, vct_ref, oc_ref, bias_ref, s_scr, flag_scr)
    for c in range(nc_total // CMP_CLASS_ROWS):
        nc = (c + 1) * CMP_CLASS_ROWS
        nblk = min(nc * CMP_STRIDE // SLC_BLOCK, bias_ref.shape[0])
        pl.when(cls == c)(functools.partial(_cmpsel_variant, nc, nblk, True, *refs))
    pl.when(flag_scr[0] != 0)(functools.partial(_cmpsel_variant, nc_total, bias_ref.shape[0], False, *refs))


def _cmp_select(qt, kc, vct, score_bound, *, tq=TOK_TILE):
    b, _, s = qt.shape
    nc = kc.shape[2]
    nblk = MAX_BLOCKS
    assert s // SLC_BLOCK <= MAX_BLOCKS and s // SLC_BLOCK >= SLC_TOPK and nc % CMP_CLASS_ROWS == 0
    return pl.pallas_call(
        _cmpsel_body, grid=(b, NSA_GROUPS, s // tq),
        in_specs=[pl.BlockSpec(memory_space=pltpu.SMEM),
                  pl.BlockSpec((None, 256, tq), lambda bi, g, i: (bi, g, i)),
                  pl.BlockSpec((None, None, nc, 64), lambda bi, g, i: (bi, g, 0, 0)),
                  pl.BlockSpec((None, None, 64, nc), lambda bi, g, i: (bi, g, 0, 0))],
        out_specs=(pl.BlockSpec((None, 256, tq), lambda bi, g, i: (bi, g, i)),
                   pl.BlockSpec((None, None, nblk, tq), lambda bi, g, i: (bi, g, 0, i))),
        out_shape=(jax.ShapeDtypeStruct((b, 512, s), F32),
                   jax.ShapeDtypeStruct((b, NSA_GROUPS, nblk, s), F32)),
        scratch_shapes=[pltpu.VMEM((NSA_HPG, nc, tq), F32), pltpu.SMEM((1,), jnp.int32)],
        compiler_params=_cparams(("parallel", "parallel", "parallel")), name="nsa_compressed_select",
    )(score_bound, qt, kc, vct)


def _flash_step(s_ref, vt, m_ref, acc_ref):
    m_old = m_ref[...]
    m_new = jnp.maximum(m_old, jnp.max(s_ref[...], axis=0, keepdims=True))
    p = jnp.exp2(s_ref[...] - m_new)
    acc_ref[...] = jnp.exp2(m_old - m_new) * acc_ref[...] + _dot(vt, p.astype(BF16))
    m_ref[...] = m_new


SEL, WIN = 0, 1
BIAS_ROWS = 16


def _slcwin_body(m0_ref, qt_ref, bias_ref, kaug_ref, vt_ref, oc_ref, gt_ref, mb_ref, y_ref,
                 q_scr, m_scr, acc_scr, s_scr, p_scr):
    tq = qt_ref.shape[1]
    tk = vt_ref.shape[3]
    assert tq == tk and WINDOW == tk and tk == 8 * SLC_BLOCK
    diag = pl.program_id(2)
    m0 = m0_ref[0]

    sel_slots = (0, 1, 2, 3)
    qs, qw = sel_slots[0], 4
    zeros = jnp.zeros((64, tq), BF16)
    for slot in sel_slots + (qw,):
        for hh in range(NSA_HPG):
            q_scr[slot, hh, 0:64, :] = qt_ref[hh * 64:(hh + 1) * 64, :]
            q_scr[slot, hh, 64:128, :] = zeros

    def set_selection_bias(kt, slot=qs):
        rows = bias_ref[pl.ds(pl.multiple_of(kt * 8, 8), 8), :]
        b16 = jnp.concatenate([rows, jnp.zeros_like(rows)], axis=0).astype(BF16)
        for hh in range(NSA_HPG):
            q_scr[slot, hh, 64:64 + BIAS_ROWS, :] = b16

    def tiles_fixed_reference(tiles):
        chains = [(br, qslot, kt, mask_bias, hh) for br, qslot, kt, mask_bias in tiles for hh in range(NSA_HPG)]
        for c in range(len(chains) + 1):
            if c < len(chains):
                br, qslot, kt, mask_bias, hh = chains[c]
                s = _dot(kaug_ref[br, pl.ds(pl.multiple_of(kt * tk, tk), tk), :], q_scr[qslot, hh])
                if mask_bias is not None:
                    s = s + mask_bias()
                p_scr[c % 4] = jnp.exp2(s - m0).astype(BF16)
            if c >= 1:
                br, _, kt, _, hh = chains[c - 1]
                acc_scr[br, hh] = acc_scr[br, hh] + _dot(vt_ref[br, kt], p_scr[(c - 1) % 4])

    def tile_running_max(br, qslot, kt, mask_bias=None):
        k = kaug_ref[br, pl.ds(pl.multiple_of(kt * tk, tk), tk), :]
        for hh in range(NSA_HPG):
            s = _dot(k, q_scr[qslot, hh])
            s_scr[hh] = s if mask_bias is None else s + mask_bias()
        for hh in range(NSA_HPG):
            _flash_step(s_scr.at[hh], vt_ref[br, kt], m_scr.at[br, hh], acc_scr.at[br, hh])

    prev = jnp.maximum(diag - 1, 0)
    no_prev = jnp.where(diag == 0, NEG, 0.0)
    band_bias = lambda: mb_ref[1] + no_prev
    causal_bias = lambda: mb_ref[0]

    acc_scr[...] = jnp.zeros(acc_scr.shape, F32)

    def unmasked_run(first_tile, count):
        for n in range(count):
            set_selection_bias(first_tile + n, sel_slots[n])
        tiles_fixed_reference([(SEL, sel_slots[n], first_tile + n, None) for n in range(count)])

    def tile_quad(j, carry):
        unmasked_run(4 * j, 4)
        return carry

    lax.fori_loop(0, diag >> 2, tile_quad, 0)
    pl.when((diag & 2) != 0)(lambda: unmasked_run((diag >> 2) * 4, 2))
    pl.when((diag & 1) != 0)(lambda: unmasked_run(diag - 1, 1))

    set_selection_bias(diag, qs)
    tiles_fixed_reference([(WIN, qw, prev, band_bias), (SEL, qs, diag, causal_bias), (WIN, qw, diag, causal_bias)])

    denominators = acc_scr[:, :, 64:65, :]
    underflow = jnp.logical_not(jnp.min(denominators) > MIN_DENOMINATOR)

    @pl.when(underflow)
    def _():
        m_scr[...] = jnp.full(m_scr.shape, NEG, F32)
        acc_scr[...] = jnp.zeros(acc_scr.shape, F32)

        def full_tile(kt, carry):
            set_selection_bias(kt, qs)
            tile_running_max(SEL, qs, kt)
            return carry

        lax.fori_loop(0, diag, full_tile, 0)
        set_selection_bias(diag, qs)
        tile_running_max(WIN, qw, prev, band_bias)
        tile_running_max(SEL, qs, diag, causal_bias)
        tile_running_max(WIN, qw, diag, causal_bias)

    gt = gt_ref[...]
    for hh in range(NSA_HPG):
        o_s = acc_scr[SEL, hh, 0:64, :] * (1.0 / acc_scr[SEL, hh, 64:65, :])
        o_w = acc_scr[WIN, hh, 0:64, :] * (1.0 / acc_scr[WIN, hh, 64:65, :])
        y_ref[hh * 64:(hh + 1) * 64, :] = (gt[3 * hh:3 * hh + 1] * oc_ref[hh * 64:(hh + 1) * 64, :]
                                          + gt[3 * hh + 1:3 * hh + 2] * o_s
                                          + gt[3 * hh + 2:3 * hh + 3] * o_w)


def _score_bound(q_gain, k_gain):
    bound = HEAD_DIM * QK_SCALE_LOG2 * jnp.max(jnp.abs(q_gain)) * jnp.max(jnp.abs(k_gain))
    return (1.02 * bound).reshape(1).astype(F32)


def _slc_win(qt, bias, kaug, vt, oc, gt, score_bound):
    b, _, s = qt.shape
    nblk = bias.shape[2]
    ns, tk = vt.shape[3], vt.shape[5]
    tq = tk
    key_rel, t_rel = np.arange(tk)[:, None], np.arange(tq)[None, :]
    mask_bias = jnp.asarray(np.stack([np.where(key_rel <= t_rel, 0.0, NEG),
                                      np.where(t_rel + tk - key_rel < WINDOW, 0.0, NEG)]), F32)
    qblk = pl.BlockSpec((None, 256, tq), lambda bi, g, i: (bi, g, i))
    return pl.pallas_call(
        _slcwin_body, grid=(b, NSA_GROUPS, s // tq),
        in_specs=[pl.BlockSpec(memory_space=pltpu.SMEM), qblk,
                  pl.BlockSpec((None, None, nblk, tq), lambda bi, g, i: (bi, g, 0, i)),
                  pl.BlockSpec((None, None, 2, s, 128), lambda bi, g, i: (bi, g, 0, 0, 0)),
                  pl.BlockSpec((None, None, 2, ns, V_ROWS, tk), lambda bi, g, i: (bi, g, 0, 0, 0, 0)),
                  qblk,
                  pl.BlockSpec((None, None, GATE_ROWS, tq), lambda bi, g, i: (bi, g, 0, i)),
                  _resident((2, tk, tq))],
        out_specs=qblk,
        out_shape=jax.ShapeDtypeStruct((b, 512, s), F32),
        scratch_shapes=[pltpu.VMEM((5, NSA_HPG, 128, tq), BF16),
                        pltpu.VMEM((2, NSA_HPG, 1, tq), F32), pltpu.VMEM((2, NSA_HPG, V_ROWS, tq), F32),
                        pltpu.VMEM((NSA_HPG, tk, tq), F32), pltpu.VMEM((NSA_HPG, tk, tq), BF16)],
        compiler_params=_cparams(("parallel", "parallel", "arbitrary")), name="nsa_selected_window",
    )(score_bound, qt, bias, kaug, vt, oc, gt.reshape(b, NSA_GROUPS, GATE_ROWS, s), mask_bias)


def _hgrn_consts():
    c = HGRN_CHUNK
    t = np.arange(c)
    lower = (t[None, :] <= t[:, None]).astype(np.float32)
    rows = [lower]
    masks = []
    for half in HGRN_LEVELS:
        mid = (t // (2 * half)) * (2 * half) + half - 1
        if half < 8:
            rows.append(lower[mid])
        same = (t[:, None] // (2 * half)) == (t[None, :] // (2 * half))
        right = (t[:, None] & half) != 0
        left = (t[None, :] & half) == 0
        masks.append((same & right & left).astype(np.float32))
    masks.append(np.eye(c, dtype=np.float32))
    masks.append(((t[:, None] // HGRN_SUB == t[None, :] // HGRN_SUB) & (t[None, :] <= t[:, None])).astype(np.float32))
    mall = np.concatenate(rows, axis=0)
    lvl = np.stack([np.tile(mk.T, (1, HGRN_HEADS)) for mk in masks])
    bdm = np.kron(np.eye(HGRN_HEADS), np.ones((64, 64), np.float32))
    return jnp.asarray(mall, BF16), jnp.asarray(lvl, F32), jnp.asarray(bdm, F32), jnp.asarray(bdm / 64, BF16)


def _hgrn_body(hg_ref, lb_ref, og_ref, mall_ref, lvl_ref, bdm_ref, bdn_ref, y_ref, st_scr, attn_scr):
    c = HGRN_CHUNK
    w = HGRN_WIDTH

    @pl.when(pl.program_id(1) == 0)
    def _():
        st_scr[...] = jnp.zeros(st_scr.shape, F32)

    chunks = range(hg_ref.shape[0] // c)
    lb = lb_ref[...]
    lane = lax.broadcasted_iota(jnp.int32, (c, w), 1)
    head_masks = [(lane >> 6) == hh for hh in range(HGRN_HEADS)]
    nlev = len(HGRN_LEVELS)

    def stack_heads(x):
        x16 = x.astype(BF16)
        return jnp.concatenate([jnp.where(hm, x16, 0) for hm in head_masks], axis=0)

    def row_bcast(x, half):
        return jnp.concatenate([jnp.broadcast_to(x[p + half - 1:p + half, :], (2 * half, w))
                                for p in range(0, c, 2 * half)], axis=0)

    qa, kk, v16, logf = [], [], [], []
    for ci in chunks:
        rows = slice(ci * c, (ci + 1) * c)
        qa.append(_silu(hg_ref[rows, 0:w]) * (HEAD_DIM ** -0.5))
        fg = lb + (1.0 - lb) * _sigmoid(hg_ref[rows, w:2 * w])
        kk.append(1.0 - fg)
        logf.append(jnp.log2(fg))
        v16.append(hg_ref[rows, 2 * w:3 * w].astype(BF16))

    mall = mall_ref[...]
    parts = _split3(jnp.concatenate(logf, axis=1))
    b_all = sum(_dot(mall[0:c], part) for part in parts)
    bcum = [b_all[:, ci * w:(ci + 1) * w] for ci in chunks]

    def level(ci, li, ref_pt):
        e = jnp.exp2(-jnp.abs(bcum[ci] - ref_pt))
        return lvl_ref[li] * _dot_nt((kk[ci] * e).astype(BF16), stack_heads(qa[ci] * e))

    coarse = [li for li, half in enumerate(HGRN_LEVELS) if half >= HGRN_SUB]
    if not coarse:
        attn_scr[...] = jnp.zeros(attn_scr.shape, F32)
    for li in coarse:
        for ci in chunks:
            contribution = level(ci, li, row_bcast(bcum[ci], HGRN_LEVELS[li]))
            attn_scr[ci] = contribution if li == coarse[0] else attn_scr[ci] + contribution

    def block_start(x):
        firsts = [jnp.zeros((HGRN_SUB, w), F32)]
        firsts += [jnp.broadcast_to(x[p - 1:p, :], (HGRN_SUB, w)) for p in range(HGRN_SUB, c, HGRN_SUB)]
        return jnp.concatenate(firsts, axis=0)

    expo = [block_start(bcum[ci]) - bcum[ci] for ci in chunks]
    largest = expo[0]
    for ci in chunks[1:]:
        largest = jnp.maximum(largest, expo[ci])
    single_reference_ok = jnp.max(largest) < HGRN_MAX_EXPONENT

    @pl.when(single_reference_ok)
    def _():
        for ci in chunks:
            kt = (kk[ci] * jnp.exp2(expo[ci])).astype(BF16)
            attn_scr[ci] = attn_scr[ci] + lvl_ref[nlev + 1] * _dot_nt(kt, stack_heads(qa[ci] * jnp.exp2(-expo[ci])))

    @pl.when(jnp.logical_not(single_reference_ok))
    def _():
        r_fine = sum(_dot(mall[c:], part) for part in parts)
        for ci in chunks:
            attn_scr[ci] = attn_scr[ci] + lvl_ref[nlev] * _dot_nt(kk[ci].astype(BF16), stack_heads(qa[ci]))
        fine = 0
        for li, half in enumerate(HGRN_LEVELS):
            if half >= HGRN_SUB:
                continue
            for ci in chunks:
                if half >= 8:
                    ref_pt = row_bcast(bcum[ci], half)
                else:
                    ref_pt = r_fine[fine * c:(fine + 1) * c, ci * w:(ci + 1) * w]
                attn_scr[ci] = attn_scr[ci] + level(ci, li, ref_pt)
            if half < 8:
                fine += 1

    attn = [attn_scr[ci] for ci in chunks]

    intra, upd, decay, qb = [], [], [], []
    for ci in chunks:
        x = _dot_tn(attn[ci].astype(BF16), v16[ci])
        intra.append(sum(jnp.where(head_masks[hh], x[hh * c:(hh + 1) * c], 0.0) for hh in range(HGRN_HEADS)))
        b_last = bcum[ci][c - 1:c, :]
        kl = (kk[ci] * jnp.exp2(b_last - bcum[ci])).astype(BF16)
        upd.append(bdm_ref[...] * _dot_tn(v16[ci], kl))
        decay.append(jnp.exp2(b_last))
        qb.append((qa[ci] * jnp.exp2(bcum[ci])).astype(BF16))

    st = st_scr[...]
    inter = []
    for ci in chunks:
        inter.append(_dot_nt(qb[ci], st.astype(BF16)))
        st = st * decay[ci] + upd[ci]
    st_scr[...] = st

    for ci in chunks:
        rows = slice(ci * c, (ci + 1) * c)
        o = inter[ci] + intra[ci]
        hi, lo = _split2(o * o)
        ms = _dot(hi, bdn_ref[...]) + _dot(lo, bdn_ref[...])
        y_ref[rows, :] = o * lax.rsqrt(ms + EPS) * og_ref[...] * _silu(hg_ref[rows, 3 * w:4 * w])


def _hgrn(hg, lower_bound, out_gain):
    b, s, _ = hg.shape
    rows = HGRN_CB * HGRN_CHUNK
    mall, lvl, bdm, bdn = _hgrn_consts()
    full = lambda shape: pl.BlockSpec(shape, lambda bi, i: (0,) * len(shape))
    return pl.pallas_call(
        _hgrn_body, grid=(b, s // rows),
        in_specs=[pl.BlockSpec((None, rows, 4 * HGRN_WIDTH), lambda bi, i: (bi, i, 0)),
                  full((1, HGRN_WIDTH)), full((1, HGRN_WIDTH)),
                  full(mall.shape), full(lvl.shape), full(bdm.shape), full(bdn.shape)],
        out_specs=pl.BlockSpec((None, rows, HGRN_WIDTH), lambda bi, i: (bi, i, 0)),
        out_shape=jax.ShapeDtypeStruct((b, s, HGRN_WIDTH), F32),
        scratch_shapes=[pltpu.VMEM((HGRN_WIDTH, HGRN_WIDTH), F32),
                        pltpu.VMEM((HGRN_CB, HGRN_CHUNK, HGRN_HEADS * HGRN_CHUNK), F32)],
        compiler_params=_cparams(("parallel", "arbitrary")), name="hgrn2_chunks",
    )(hg, lower_bound.reshape(1, -1), out_gain.reshape(1, -1), mall, lvl, bdm, bdn)


def _memkv_body(mem_ref, mg_ref, wk_ref, wvt_ref, kg_ref, kh_ref, vht_ref):
    m = mem_ref.shape[0]
    mn = _rms_rows(mem_ref[...], mg_ref[...]).astype(BF16)
    k = _dot(mn, wk_ref[...])
    vt = _dot_nt(wvt_ref[...], mn)
    ones_rows = (lax.broadcasted_iota(jnp.int32, (V_ROWS - 64, m), 0) == 0).astype(BF16)
    for hh in range(MEM_HEADS):
        kh_ref[hh] = _rms_rows(k[:, hh * 64:(hh + 1) * 64], kg_ref[...]).astype(BF16)
        vht_ref[hh, 0:64, :] = vt[hh * 64:(hh + 1) * 64].astype(BF16)
        vht_ref[hh, 64:V_ROWS, :] = ones_rows


def _mem_kv(mem, mem_gain, wk, wvt, k_gain):
    b, m, d = mem.shape
    full = lambda shape: pl.BlockSpec(shape, lambda bi: (0,) * len(shape))
    return pl.pallas_call(
        _memkv_body, grid=(b,),
        in_specs=[pl.BlockSpec((None, m, d), lambda bi: (bi, 0, 0)), full((1, d)),
                  full(wk.shape), full(wvt.shape), full((1, 64))],
        out_specs=(pl.BlockSpec((None, MEM_HEADS, m, 64), lambda bi: (bi, 0, 0, 0)),
                   pl.BlockSpec((None, MEM_HEADS, V_ROWS, m), lambda bi: (bi, 0, 0, 0))),
        out_shape=(jax.ShapeDtypeStruct((b, MEM_HEADS, m, 64), BF16),
                   jax.ShapeDtypeStruct((b, MEM_HEADS, V_ROWS, m), BF16)),
        compiler_params=_cparams(("parallel",)), name="memory_kv",
    )(mem, mem_gain.reshape(1, d), wk, wvt, k_gain.reshape(1, 64))


def _out_body(x_ref, ynt_ref, yh_ref, qmt_ref, kh_ref, vht_ref, ng_ref, mg_ref, wo_ref,
              fg_ref, wg_ref, wu_ref, wd_ref, o_ref, a_scr):
    scores = [_dot(kh_ref[hh], qmt_ref[hh * 64:(hh + 1) * 64, :]) for hh in range(MEM_HEADS)]
    nsa = _rms_cols(ynt_ref[...], ng_ref[...]).astype(BF16)
    acc = _dot_tn(nsa, wo_ref[0:NSA_WIDTH, :])
    acc = acc + _dot(yh_ref[...].astype(BF16), wo_ref[NSA_WIDTH:NSA_WIDTH + HGRN_WIDTH, :])
    y_mem = []
    for s in scores:
        hh = len(y_mem)
        p = jnp.exp2(s - jnp.max(s, axis=0, keepdims=True))
        o = _dot(vht_ref[hh], p.astype(BF16))
        y_mem.append(o[0:64] * (1.0 / o[64:65]))
    mem = _rms_cols(jnp.concatenate(y_mem, axis=0), mg_ref[...]).astype(BF16)
    acc = acc + _dot_tn(mem, wo_ref[NSA_WIDTH + HGRN_WIDTH:, :])
    o_ref[...] = _ffn_half_step(x_ref[...] + acc, fg_ref, wg_ref, wu_ref, wd_ref, a_scr)


def _out_ffn(x3d, ynt, yh, qmt, kh, vht, nsa_gain, mem_gain, wo, ffn_gain, wg, wu, wd, *, tm=TOK_TILE):
    b, s, d = x3d.shape
    m = kh.shape[2]
    return pl.pallas_call(
        _out_body, grid=(b, s // tm),
        in_specs=[pl.BlockSpec((None, tm, d), lambda bi, i: (bi, i, 0)),
                  pl.BlockSpec((None, NSA_WIDTH, tm), lambda bi, i: (bi, 0, i)),
                  pl.BlockSpec((None, tm, HGRN_WIDTH), lambda bi, i: (bi, i, 0)),
                  pl.BlockSpec((None, MEM_WIDTH, tm), lambda bi, i: (bi, 0, i)),
                  pl.BlockSpec((None, MEM_HEADS, m, 64), lambda bi, i: (bi, 0, 0, 0)),
                  pl.BlockSpec((None, MEM_HEADS, V_ROWS, m), lambda bi, i: (bi, 0, 0, 0)),
                  _resident((NSA_WIDTH, 1)), _resident((MEM_WIDTH, 1)), _resident(wo.shape),
                  _resident((1, d)), _resident(wg.shape), _resident(wu.shape), _resident(wd.shape)],
        out_specs=pl.BlockSpec((None, tm, d), lambda bi, i: (bi, i, 0)),
        out_shape=jax.ShapeDtypeStruct((b, s, d), F32),
        scratch_shapes=[pltpu.VMEM((tm, wg.shape[1]), BF16)],
        compiler_params=_cparams(("parallel", "parallel")), name="mix_out_ffn2",
    )(x3d, ynt, yh, qmt, kh, vht, nsa_gain.reshape(-1, 1), mem_gain.reshape(-1, 1), wo,
      ffn_gain.reshape(1, d), wg, wu, wd)


def _layer(x, mem, ffn1, ffn2, mix_norm, w_in, w_out, nsa_q_norm, nsa_k_norm, cmp_pos_k, cmp_w1_k, cmp_w2_k,
           cmp_pos_v, cmp_w1_v, cmp_w2_v, nsa_out_norm, lower_bound, hgrn_out_norm,
           mem_norm, mem_w_k, mem_w_v, mem_q_norm, mem_k_norm, mem_out_norm):
    b, s, d = x.shape
    sizes = (512, 128, 128, 128, 128, 128, 128, 24, 256, 256, 256, 256, 256)
    offs = np.concatenate([[0], np.cumsum(sizes)])
    col = lambda i: w_in[:, offs[i]:offs[i + 1]]
    (q_a, k_c, v_c, k_s, v_s, k_w, v_w, g_a, q_h, f_h, i_h, g_h, q_m) = [col(i) for i in range(13)]
    gpad = jnp.zeros((d, GATE_ROWS - 3 * NSA_HPG), w_in.dtype)
    wt = jnp.concatenate([q_a, v_s, v_w, g_a[:, :3 * NSA_HPG], gpad, g_a[:, 3 * NSA_HPG:], gpad, q_m],
                         axis=1).T.astype(BF16)
    wn = jnp.concatenate([k_c, v_c, k_s, k_w, q_h, f_h, i_h, g_h], axis=1).astype(BF16)

    x1, qt, vt, gt, qmt, kaug, kvc, hg = _ffn_proj(
        x, *ffn1, mix_norm, wt, wn, nsa_q_norm, nsa_k_norm, mem_q_norm, _rope_tables(s))

    cmp_pos, cmp_w1 = _compress_weights(cmp_pos_k, cmp_pos_v, cmp_w1_k, cmp_w1_v)
    kc, vct = _compress(kvc, cmp_pos, cmp_w1, cmp_w2_k.astype(BF16), cmp_w2_v.T.astype(BF16), nsa_k_norm)
    score_bound = _score_bound(nsa_q_norm, nsa_k_norm)
    oc, bias = _cmp_select(qt, kc, vct, score_bound)
    y_nsa = _slc_win(qt, bias, kaug, vt, oc, gt, score_bound)

    y_hgrn = _hgrn(hg, lower_bound, hgrn_out_norm)

    kh, vht = _mem_kv(mem, mem_norm, mem_w_k.astype(BF16), mem_w_v.T.astype(BF16), mem_k_norm)
    return _out_ffn(x1, y_nsa, y_hgrn, qmt, kh, vht, nsa_out_norm, mem_out_norm, w_out.astype(BF16), *ffn2)


def kernel(x, mem, ffn1_norm, ffn1_w_gate, ffn1_w_up, ffn1_w_down, mix_norm, w_in, w_out, nsa_q_norm, nsa_k_norm, cmp_pos_k, cmp_w1_k, cmp_w2_k, cmp_pos_v, cmp_w1_v, cmp_w2_v, nsa_out_norm, hgrn_lb_logits, hgrn_out_norm, mem_norm, mem_w_k, mem_w_v, mem_q_norm, mem_k_norm, mem_out_norm, ffn2_norm, ffn2_w_gate, ffn2_w_up, ffn2_w_down):
    b, s, d = x.shape
    depth = ffn1_norm.shape[0]
    lower_bounds = jnp.cumsum(jax.nn.softmax(hgrn_lb_logits.astype(F32), axis=0), axis=0)
    bf = lambda a: a.astype(BF16)
    for l in range(depth):
        x = _layer(x, mem, (ffn1_norm[l], bf(ffn1_w_gate[l]), bf(ffn1_w_up[l]), bf(ffn1_w_down[l])),
                   (ffn2_norm[l], bf(ffn2_w_gate[l]), bf(ffn2_w_up[l]), bf(ffn2_w_down[l])),
                   mix_norm[l], w_in[l], w_out[l], nsa_q_norm[l], nsa_k_norm[l],
                   cmp_pos_k[l], cmp_w1_k[l], cmp_w2_k[l], cmp_pos_v[l], cmp_w1_v[l], cmp_w2_v[l],
                   nsa_out_norm[l], lower_bounds[l], hgrn_out_norm[l],
                   mem_norm[l], mem_w_k[l], mem_w_v[l], mem_q_norm[l], mem_k_norm[l], mem_out_norm[l])
    return x
```

```python
import functools

import numpy as np
import jax
import jax.numpy as jnp
from jax import lax
from jax.experimental import pallas as pl
from jax.experimental.pallas import tpu as pltpu

F32 = jnp.float32
BF16 = jnp.bfloat16

HEAD_DIM = 64
ROT_DIM = 16
ROT_HALF = 8
ROPE_THETA = 500000.0
NSA_HEADS = 8
NSA_GROUPS = 2
NSA_HPG = 4
CMP_BLOCK = 32
CMP_STRIDE = 16
SLC_BLOCK = 64
SLC_SHIFT = 6
SLC_TOPK = 16
WINDOW = 512
FORCED_SCORE = 1e4
HGRN_HEADS = 4
HGRN_CHUNK = 64
HGRN_WIDTH = 256
MEM_HEADS = 4
MEM_WIDTH = 256
NSA_WIDTH = 512
EPS = 1e-6
NEG = -1e30
QK_SCALE_LOG2 = HEAD_DIM ** -0.5 * 1.4426950408889634
MIN_DENOMINATOR = 2.0 ** -64

VMEM_LIMIT = 56 * 1024 * 1024
MAX_BLOCKS = 128
GATE_ROWS = 16
V_ROWS = 80
TOK_TILE = 512
HGRN_CB = 8
HGRN_LEVELS = (32, 16, 8, 4, 2, 1)
HGRN_SUB = 64
HGRN_MAX_EXPONENT = 96.0

NT_DIMS = (((1,), (1,)), ((), ()))
TN_DIMS = (((0,), (0,)), ((), ()))


def _cparams(sem):
    return pltpu.CompilerParams(dimension_semantics=sem, vmem_limit_bytes=VMEM_LIMIT)


def _dot(a, b):
    return jnp.dot(a, b, preferred_element_type=F32)


def _dot_nt(a, b):
    return lax.dot_general(a, b, NT_DIMS, preferred_element_type=F32)


def _dot_tn(a, b):
    return lax.dot_general(a, b, TN_DIMS, preferred_element_type=F32)


def _sigmoid(x):
    return 1.0 / (1.0 + jnp.exp(-x))


def _silu(x):
    return x * _sigmoid(x)


def _split2(x):
    hi = x.astype(BF16)
    lo = (x - hi.astype(F32)).astype(BF16)
    return hi, lo


def _split3(x):
    hi = x.astype(BF16)
    r1 = x - hi.astype(F32)
    mid = r1.astype(BF16)
    lo = (r1 - mid.astype(F32)).astype(BF16)
    return hi, mid, lo


def _rms_rows(x, gain_row):
    ms = jnp.mean(x * x, axis=-1, keepdims=True)
    return x * lax.rsqrt(ms + EPS) * gain_row


def _rms_cols(x, gain_col):
    ms = jnp.mean(x * x, axis=0, keepdims=True)
    return x * lax.rsqrt(ms + EPS) * gain_col


def _seg_mean_sq(x, bd):
    hi, lo = _split2(x * x)
    return _dot(hi, bd) + _dot(lo, bd)


FFN_CHUNK = 256


def _ffn_half_step(x, g_ref, wg_ref, wu_ref, wd_ref, a_scr):
    xn = _rms_rows(x, g_ref[...]).astype(BF16)
    d_ff = wg_ref.shape[1]
    for c in range(d_ff // FFN_CHUNK):
        sl = slice(c * FFN_CHUNK, (c + 1) * FFN_CHUNK)
        g = _dot(xn, wg_ref[:, sl])
        u = _dot(xn, wu_ref[:, sl])
        a_scr[:, sl] = (_silu(g) * u).astype(BF16)
    return x + 0.5 * _dot(a_scr[...], wd_ref[...])


def _resident(shape):
    return pl.BlockSpec(shape, lambda *_: (0,) * len(shape), pipeline_mode=pl.Buffered(1))


def _rope_cols(xn, cos, sin):
    x0, x1 = xn[0:ROT_HALF], xn[ROT_HALF:ROT_DIM]
    return jnp.concatenate([x0 * cos - x1 * sin, x1 * cos + x0 * sin, xn[ROT_DIM:]], axis=0)


def _rope_rows(x, cn, sa, sb):
    return x * cn + pltpu.roll(x, 128 - ROT_HALF, 1) * sa + pltpu.roll(x, ROT_HALF, 1) * sb


def _proj_body(x_ref, fg_ref, wg_ref, wu_ref, wd_ref, mg_ref, wt_ref, wn_ref, qg_ref, kg_ref, mqg_ref,
               cos_ref, sin_ref, cn_ref, sa_ref, sb_ref, bd_ref,
               x1_ref, qt_ref, vt_ref, gt_ref, qmt_ref, kaug_ref, kvc_ref, hg_ref, a_scr):
    tm = x_ref.shape[0]
    assert tm == 8 * SLC_BLOCK
    x1 = _ffn_half_step(x_ref[...], fg_ref, wg_ref, wu_ref, wd_ref, a_scr)
    x1_ref[...] = x1
    h = _rms_rows(x1, mg_ref[...]).astype(BF16)

    qg, mqg = qg_ref[...], mqg_ref[...]
    half = tm // 2
    ones_rows = (lax.broadcasted_iota(jnp.int32, (V_ROWS - 64, half), 0) == 0).astype(BF16)
    for part in range(2):
        tok = slice(part * half, (part + 1) * half)
        pt = _dot_nt(wt_ref[...], h[tok, :])
        cos, sin = cos_ref[:, tok], sin_ref[:, tok]
        for hh in range(NSA_HEADS):
            xq = _rms_cols(pt[hh * 64:(hh + 1) * 64], qg)
            qt_ref[hh * 64:(hh + 1) * 64, tok] = (_rope_cols(xq, cos, sin) * QK_SCALE_LOG2).astype(BF16)
        for g in range(NSA_GROUPS):
            for br in range(2):
                rows = 512 + br * 128 + g * 64
                vt_ref[g, br, 0:64, tok] = pt[rows:rows + 64].astype(BF16)
                vt_ref[g, br, 64:V_ROWS, tok] = ones_rows
        gt_ref[:, tok] = _sigmoid(pt[768:800])
        for hh in range(MEM_HEADS):
            xm = _rms_cols(pt[800 + hh * 64:864 + hh * 64], mqg)
            qmt_ref[hh * 64:(hh + 1) * 64, tok] = (xm * QK_SCALE_LOG2).astype(BF16)

    cn, sa, sb = cn_ref[...], sa_ref[...], sb_ref[...]
    bd, kg = bd_ref[...], kg_ref[...]
    pc = _dot(h, wn_ref[:, 0:256])
    pk = _dot(h, wn_ref[:, 256:512])
    kvc_ref[0] = _rope_rows(pc[:, 0:128], cn, sa, sb)
    kvc_ref[1] = pc[:, 128:256]
    ph = _dot(h, wn_ref[:, 512:1024])
    ks = pk[:, 0:128]
    kw = pk[:, 128:256]
    ks = _rope_rows(ks * lax.rsqrt(_seg_mean_sq(ks, bd) + EPS) * kg, cn, sa, sb)
    kw = _rope_rows(kw * lax.rsqrt(_seg_mean_sq(kw, bd) + EPS) * kg, cn, sa, sb)
    lane = lax.broadcasted_iota(jnp.int32, (tm, 128), 1)
    row = lax.broadcasted_iota(jnp.int32, (tm, 128), 0)
    onehot = jnp.where(lane - 64 == (row >> SLC_SHIFT), 1.0, 0.0)
    lo_half = lane < 64
    kaug_ref[0, 0] = jnp.where(lo_half, ks, onehot).astype(BF16)
    kaug_ref[0, 1] = jnp.where(lo_half, kw, 0.0).astype(BF16)
    kaug_ref[1, 0] = jnp.where(lo_half, pltpu.roll(ks, 64, 1), onehot).astype(BF16)
    kaug_ref[1, 1] = jnp.where(lo_half, pltpu.roll(kw, 64, 1), 0.0).astype(BF16)
    hg_ref[:, 0:512] = ph
    hg_ref[:, 512:1024] = _dot(h, wn_ref[:, 1024:1536])


def _ffn_proj(x3d, ffn_gain, wg, wu, wd, mix_gain, wt, wn, q_gain, k_gain, mq_gain, rope):
    b, s, d = x3d.shape
    tm = TOK_TILE
    ns = s // tm
    cos_t, sin_t, cn, sa, sb = rope
    bd = jnp.asarray(np.kron(np.eye(2), np.full((64, 64), 1.0 / 64)), BF16)
    full = _resident
    out_shape = (
        jax.ShapeDtypeStruct((b, s, d), F32),
        jax.ShapeDtypeStruct((b, 512, s), BF16),
        jax.ShapeDtypeStruct((b, 2, 2, ns, V_ROWS, tm), BF16),
        jax.ShapeDtypeStruct((b, 32, s), F32),
        jax.ShapeDtypeStruct((b, 256, s), BF16),
        jax.ShapeDtypeStruct((b, 2, 2, s, 128), BF16),
        jax.ShapeDtypeStruct((b, 2, s, 128), F32),
        jax.ShapeDtypeStruct((b, s, 1024), F32),
    )
    out_specs = (
        pl.BlockSpec((None, tm, d), lambda bi, i: (bi, i, 0)),
        pl.BlockSpec((None, 512, tm), lambda bi, i: (bi, 0, i)),
        pl.BlockSpec((None, 2, 2, None, V_ROWS, tm), lambda bi, i: (bi, 0, 0, i, 0, 0)),
        pl.BlockSpec((None, 32, tm), lambda bi, i: (bi, 0, i)),
        pl.BlockSpec((None, 256, tm), lambda bi, i: (bi, 0, i)),
        pl.BlockSpec((None, 2, 2, tm, 128), lambda bi, i: (bi, 0, 0, i, 0)),
        pl.BlockSpec((None, 2, tm, 128), lambda bi, i: (bi, 0, i, 0)),
        pl.BlockSpec((None, tm, 1024), lambda bi, i: (bi, i, 0)),
    )
    in_specs = [
        pl.BlockSpec((None, tm, d), lambda bi, i: (bi, i, 0)),
        full((1, d)), full(wg.shape), full(wu.shape), full(wd.shape),
        full((1, d)), full(wt.shape), full(wn.shape),
        full((64, 1)), full((1, 128)), full((64, 1)),
        pl.BlockSpec((ROT_HALF, tm), lambda bi, i: (0, i)),
        pl.BlockSpec((ROT_HALF, tm), lambda bi, i: (0, i)),
        pl.BlockSpec((tm, 128), lambda bi, i: (i, 0)),
        pl.BlockSpec((tm, 128), lambda bi, i: (i, 0)),
        pl.BlockSpec((tm, 128), lambda bi, i: (i, 0)),
        full((128, 128)),
    ]
    return pl.pallas_call(
        _proj_body, grid=(b, ns), in_specs=in_specs, out_specs=out_specs, out_shape=out_shape,
        scratch_shapes=[pltpu.VMEM((tm, wg.shape[1]), BF16)],
        compiler_params=_cparams(("parallel", "parallel")), name="ffn1_mix_projection",
    )(x3d, ffn_gain.reshape(1, d), wg, wu, wd, mix_gain.reshape(1, d), wt, wn, q_gain.reshape(64, 1),
      jnp.tile(k_gain.reshape(1, 64), (1, 2)), mq_gain.reshape(64, 1), cos_t, sin_t, cn, sa, sb, bd)


def _rope_tables(s):
    pos = jnp.arange(s, dtype=F32)
    inv = ROPE_THETA ** (-(jnp.arange(0, ROT_DIM, 2, dtype=F32) / ROT_DIM))
    ang = pos[:, None] * inv[None, :]
    cos, sin = jnp.cos(ang), jnp.sin(ang)
    zeros = jnp.zeros((s, 64 - ROT_DIM), F32)
    cn = jnp.concatenate([cos, cos, jnp.ones((s, 64 - ROT_DIM), F32)], axis=1)
    sa = jnp.concatenate([-sin, jnp.zeros((s, ROT_HALF), F32), zeros], axis=1)
    sb = jnp.concatenate([jnp.zeros((s, ROT_HALF), F32), sin, zeros], axis=1)
    tile2 = lambda a: jnp.concatenate([a, a], axis=1)
    return cos.T, sin.T, tile2(cn), tile2(sa), tile2(sb)


def _cmp_body(kvc_ref, pos_ref, w1_ref, w2k_ref, w2vt_ref, kg_ref, kc_ref, vct_ref):
    nc = kvc_ref.shape[1] // CMP_STRIDE
    for kind in range(2):
        halves = []
        for part in range(2):
            x = jnp.concatenate(
                [(kvc_ref[kind, pl.ds(r, nc, stride=CMP_STRIDE), :]
                  + pos_ref[kind, part, :, r * 128:(r + 1) * 128]).astype(BF16) for r in range(CMP_STRIDE)],
                axis=1)
            halves.append(x)
        for g in range(NSA_GROUPS):
            second = _dot(halves[1], w1_ref[kind, g, 1])
            hid = _silu(_dot(halves[0], w1_ref[kind, g, 0]) + pltpu.roll(second, nc - 1, 0)).astype(BF16)
            if kind == 0:
                kc_ref[g] = _rms_rows(_dot(hid, w2k_ref[...]), kg_ref[...]).astype(BF16)
            else:
                vct_ref[g] = _dot_nt(w2vt_ref[...], hid).astype(BF16)


def _compress(kvc, pos, w1, w2k, w2vt, k_gain):
    b, _, s, _ = kvc.shape
    nc = s // CMP_STRIDE
    return pl.pallas_call(
        _cmp_body, grid=(b,),
        in_specs=[pl.BlockSpec((None, 2, s, 128), lambda bi: (bi, 0, 0, 0)),
                  _resident(pos.shape), _resident(w1.shape), _resident(w2k.shape), _resident(w2vt.shape),
                  _resident((1, 64))],
        out_specs=(pl.BlockSpec((None, 2, nc, 64), lambda bi: (bi, 0, 0, 0)),
                   pl.BlockSpec((None, 2, 64, nc), lambda bi: (bi, 0, 0, 0))),
        out_shape=(jax.ShapeDtypeStruct((b, 2, nc, 64), BF16), jax.ShapeDtypeStruct((b, 2, 64, nc), BF16)),
        compiler_params=_cparams(("parallel",)), name="nsa_compress",
    )(kvc, pos, w1, w2k, w2vt, k_gain.reshape(1, 64))


def _compress_weights(pos_k, pos_v, w1_k, w1_v):
    def pos_part(p):
        p = p.reshape(2, CMP_STRIDE, 1, 64)
        return jnp.broadcast_to(p, (2, CMP_STRIDE, NSA_GROUPS, 64)).reshape(2, 1, CMP_STRIDE * 128)

    def w1_part(w):
        hdim = w.shape[1]
        w = w.reshape(2, CMP_STRIDE, 1, 64, hdim)
        per_group = []
        for g in range(NSA_GROUPS):
            pads = [w if gg == g else jnp.zeros_like(w) for gg in range(NSA_GROUPS)]
            per_group.append(jnp.concatenate(pads, axis=2).reshape(2, CMP_STRIDE * 128, hdim))
        return jnp.stack(per_group)

    pos = jnp.stack([pos_part(pos_k), pos_part(pos_v)])
    w1 = jnp.stack([w1_part(w1_k), w1_part(w1_v)]).astype(BF16)
    return pos, w1


CMP_CLASS_ROWS = 128


def _cmpsel_variant(nc, nblk, fixed_reference, m0_ref, qt_ref, kc_ref, vct_ref, oc_ref, bias_ref, s_scr, flag_scr):
    tq = qt_ref.shape[1]
    t0 = pl.program_id(2) * tq
    n_idx = lax.broadcasted_iota(jnp.int32, (nc, tq), 0)
    t_idx = t0 + lax.broadcasted_iota(jnp.int32, (nc, tq), 1)
    mask_bias = jnp.where(n_idx * CMP_STRIDE + (CMP_BLOCK - 1) <= t_idx, 0.0, NEG)
    sees_any = t0 + lax.broadcasted_iota(jnp.int32, (1, tq), 1) >= CMP_BLOCK - 1
    kc = kc_ref[0:nc, :]
    if fixed_reference:
        mask_bias = mask_bias - m0_ref[0]
    else:
        for hh in range(NSA_HPG):
            s_scr[hh, 0:nc, :] = _dot(kc, qt_ref[hh * 64:(hh + 1) * 64, :]) + mask_bias
    jj = lax.broadcasted_iota(jnp.int32, (nblk, nc), 0)
    nn = lax.broadcasted_iota(jnp.int32, (nblk, nc), 1)
    ov = jnp.where((nn * CMP_STRIDE < jj * SLC_BLOCK + SLC_BLOCK)
                   & (nn * CMP_STRIDE + CMP_BLOCK > jj * SLC_BLOCK), 1.0, 0.0).astype(BF16)
    ones_rows = (lax.broadcasted_iota(jnp.int32, (V_ROWS - 64, nc), 0) == 0).astype(BF16)
    lhs = jnp.concatenate([vct_ref[:, 0:nc], ones_rows, ov], axis=0)
    imp = jnp.zeros((nblk, tq), F32)
    l_min = jnp.full((1, tq), 1.0, F32)

    def probabilities(hh):
        if fixed_reference:
            return jnp.exp2(_dot(kc, qt_ref[hh * 64:(hh + 1) * 64, :]) + mask_bias).astype(BF16)
        m = jnp.max(s_scr[hh, 0:nc, :], axis=0, keepdims=True)
        return jnp.exp2(s_scr[hh, 0:nc, :] - m).astype(BF16)

    p_next = probabilities(0)
    for hh in range(NSA_HPG):
        p = p_next
        if hh + 1 < NSA_HPG:
            p_next = probabilities(hh + 1)
        r = _dot(lhs, p)
        l_min = jnp.minimum(l_min, jnp.where(sees_any, r[64:65], 1.0))
        inv_l = jnp.where(sees_any, 1.0 / r[64:65], 0.0)
        oc_ref[hh * 64:(hh + 1) * 64, :] = r[0:64] * inv_l
        imp = imp + r[V_ROWS:] * inv_l
    if fixed_reference:
        flag_scr[0] = jnp.where(jnp.min(l_min) > MIN_DENOMINATOR, 0, 1)

    j = lax.broadcasted_iota(jnp.int32, (nblk, tq), 0)
    cur = (t0 + lax.broadcasted_iota(jnp.int32, (nblk, tq), 1)) >> SLC_SHIFT
    forced = (j == 0) | (j == cur) | (j == cur - 1)
    picks = SLC_TOPK - 3
    forced_bias = jnp.where(forced & (j <= cur), 0.0, NEG)
    imp = jnp.where((j <= cur) & jnp.logical_not(forced), imp, -1.0)
    if nblk < bias_ref.shape[0]:
        bias_ref[nblk:, :] = jnp.full((bias_ref.shape[0] - nblk, tq), NEG, F32)

    rest = imp
    for _ in range(picks):
        v = jnp.max(rest, axis=0, keepdims=True)
        rest = jnp.where(rest == v, -3e38, rest)
    chosen = (imp >= v) & (imp >= 0.0)
    count = jnp.sum(jnp.where(chosen, 1.0, 0.0), axis=0, keepdims=True)
    valid = jnp.sum(jnp.where(imp >= 0.0, 1.0, 0.0), axis=0, keepdims=True)
    bias_ref[0:nblk, :] = jnp.where(chosen, 0.0, forced_bias)
    has_tie = jnp.max(jnp.abs(count - jnp.minimum(valid, float(picks)))) > 0.0

    @pl.when(has_tie)
    def _():
        jf = j.astype(F32)
        bias, rest = forced_bias, imp
        for _ in range(picks):
            v = jnp.max(rest, axis=0, keepdims=True)
            first = jnp.min(jnp.where(rest == v, jf, float(nblk)), axis=0, keepdims=True)
            pick = jf == first
            bias = jnp.where(pick & (v >= 0.0), 0.0, bias)
            rest = jnp.where(pick, -3e38, rest)
        bias_ref[0:nblk, :] = bias


def _cmpsel_body(m0_ref, qt_ref, kc_ref, vct_ref, oc_ref, bias_ref, s_scr, flag_scr):
    tq = qt_ref.shape[1]
    nc_total = kc_ref.shape[0]
    tiles_per_class = CMP_CLASS_ROWS // (tq // CMP_STRIDE)
    cls = pl.program_id(2) // tiles_per_class
    refs = (m0_ref, qt_ref, kc_ref, vct_ref, oc_ref, bias_ref, s_scr, flag_scr)
    for c in range(nc_total // CMP_CLASS_ROWS):
        nc = (c + 1) * CMP_CLASS_ROWS
        nblk = min(nc * CMP_STRIDE // SLC_BLOCK, bias_ref.shape[0])
        pl.when(cls == c)(functools.partial(_cmpsel_variant, nc, nblk, True, *refs))
    pl.when(flag_scr[0] != 0)(functools.partial(_cmpsel_variant, nc_total, bias_ref.shape[0], False, *refs))


def _cmp_select(qt, kc, vct, score_bound, *, tq=TOK_TILE):
    b, _, s = qt.shape
    nc = kc.shape[2]
    nblk = MAX_BLOCKS
    assert s // SLC_BLOCK <= MAX_BLOCKS and s // SLC_BLOCK >= SLC_TOPK and nc % CMP_CLASS_ROWS == 0
    return pl.pallas_call(
        _cmpsel_body, grid=(b, NSA_GROUPS, s // tq),
        in_specs=[pl.BlockSpec(memory_space=pltpu.SMEM),
                  pl.BlockSpec((None, 256, tq), lambda bi, g, i: (bi, g, i)),
                  pl.BlockSpec((None, None, nc, 64), lambda bi, g, i: (bi, g, 0, 0)),
                  pl.BlockSpec((None, None, 64, nc), lambda bi, g, i: (bi, g, 0, 0))],
        out_specs=(pl.BlockSpec((None, 256, tq), lambda bi, g, i: (bi, g, i)),
                   pl.BlockSpec((None, None, nblk, tq), lambda bi, g, i: (bi, g, 0, i))),
        out_shape=(jax.ShapeDtypeStruct((b, 512, s), F32),
                   jax.ShapeDtypeStruct((b, NSA_GROUPS, nblk, s), F32)),
        scratch_shapes=[pltpu.VMEM((NSA_HPG, nc, tq), F32), pltpu.SMEM((1,), jnp.int32)],
        compiler_params=_cparams(("parallel", "parallel", "parallel")), name="nsa_compressed_select",
    )(score_bound, qt, kc, vct)


def _flash_step(s_ref, vt, m_ref, acc_ref):
    m_old = m_ref[...]
    m_new = jnp.maximum(m_old, jnp.max(s_ref[...], axis=0, keepdims=True))
    p = jnp.exp2(s_ref[...] - m_new)
    acc_ref[...] = jnp.exp2(m_old - m_new) * acc_ref[...] + _dot(vt, p.astype(BF16))
    m_ref[...] = m_new


SEL, WIN = 0, 1
BIAS_ROWS = 16


def _slcwin_body(m0_ref, qt_ref, bias_ref, kaug_ref, vt_ref, oc_ref, gt_ref, mb_ref, y_ref,
                 q_scr, m_scr, acc_scr, s_scr, p_scr):
    tq = qt_ref.shape[1]
    tk = vt_ref.shape[3]
    assert tq == tk and WINDOW == tk and tk == 8 * SLC_BLOCK
    diag = pl.program_id(2)
    m0 = m0_ref[0]

    sel_slots = (0, 1, 2, 3)
    qs, qw = sel_slots[0], 4
    zeros = jnp.zeros((64, tq), BF16)
    for slot in sel_slots + (qw,):
        for hh in range(NSA_HPG):
            q_scr[slot, hh, 0:64, :] = qt_ref[hh * 64:(hh + 1) * 64, :]
            q_scr[slot, hh, 64:128, :] = zeros

    def set_selection_bias(kt, slot=qs):
        rows = bias_ref[pl.ds(pl.multiple_of(kt * 8, 8), 8), :]
        b16 = jnp.concatenate([rows, jnp.zeros_like(rows)], axis=0).astype(BF16)
        for hh in range(NSA_HPG):
            q_scr[slot, hh, 64:64 + BIAS_ROWS, :] = b16

    def tiles_fixed_reference(tiles):
        chains = [(br, qslot, kt, mask_bias, hh) for br, qslot, kt, mask_bias in tiles for hh in range(NSA_HPG)]
        for c in range(len(chains) + 1):
            if c < len(chains):
                br, qslot, kt, mask_bias, hh = chains[c]
                s = _dot(kaug_ref[br, pl.ds(pl.multiple_of(kt * tk, tk), tk), :], q_scr[qslot, hh])
                if mask_bias is not None:
                    s = s + mask_bias()
                p_scr[c % 4] = jnp.exp2(s - m0).astype(BF16)
            if c >= 1:
                br, _, kt, _, hh = chains[c - 1]
                acc_scr[br, hh] = acc_scr[br, hh] + _dot(vt_ref[br, kt], p_scr[(c - 1) % 4])

    def tile_running_max(br, qslot, kt, mask_bias=None):
        k = kaug_ref[br, pl.ds(pl.multiple_of(kt * tk, tk), tk), :]
        for hh in range(NSA_HPG):
            s = _dot(k, q_scr[qslot, hh])
            s_scr[hh] = s if mask_bias is None else s + mask_bias()
        for hh in range(NSA_HPG):
            _flash_step(s_scr.at[hh], vt_ref[br, kt], m_scr.at[br, hh], acc_scr.at[br, hh])

    prev = jnp.maximum(diag - 1, 0)
    no_prev = jnp.where(diag == 0, NEG, 0.0)
    band_bias = lambda: mb_ref[1] + no_prev
    causal_bias = lambda: mb_ref[0]

    acc_scr[...] = jnp.zeros(acc_scr.shape, F32)

    def unmasked_run(first_tile, count):
        for n in range(count):
            set_selection_bias(first_tile + n, sel_slots[n])
        tiles_fixed_reference([(SEL, sel_slots[n], first_tile + n, None) for n in range(count)])

    def tile_quad(j, carry):
        unmasked_run(4 * j, 4)
        return carry

    lax.fori_loop(0, diag >> 2, tile_quad, 0)
    pl.when((diag & 2) != 0)(lambda: unmasked_run((diag >> 2) * 4, 2))
    pl.when((diag & 1) != 0)(lambda: unmasked_run(diag - 1, 1))

    set_selection_bias(diag, qs)
    tiles_fixed_reference([(WIN, qw, prev, band_bias), (SEL, qs, diag, causal_bias), (WIN, qw, diag, causal_bias)])

    denominators = acc_scr[:, :, 64:65, :]
    underflow = jnp.logical_not(jnp.min(denominators) > MIN_DENOMINATOR)

    @pl.when(underflow)
    def _():
        m_scr[...] = jnp.full(m_scr.shape, NEG, F32)
        acc_scr[...] = jnp.zeros(acc_scr.shape, F32)

        def full_tile(kt, carry):
            set_selection_bias(kt, qs)
            tile_running_max(SEL, qs, kt)
            return carry

        lax.fori_loop(0, diag, full_tile, 0)
        set_selection_bias(diag, qs)
        tile_running_max(WIN, qw, prev, band_bias)
        tile_running_max(SEL, qs, diag, causal_bias)
        tile_running_max(WIN, qw, diag, causal_bias)

    gt = gt_ref[...]
    for hh in range(NSA_HPG):
        o_s = acc_scr[SEL, hh, 0:64, :] * (1.0 / acc_scr[SEL, hh, 64:65, :])
        o_w = acc_scr[WIN, hh, 0:64, :] * (1.0 / acc_scr[WIN, hh, 64:65, :])
        y_ref[hh * 64:(hh + 1) * 64, :] = (gt[3 * hh:3 * hh + 1] * oc_ref[hh * 64:(hh + 1) * 64, :]
                                          + gt[3 * hh + 1:3 * hh + 2] * o_s
                                          + gt[3 * hh + 2:3 * hh + 3] * o_w)


def _score_bound(q_gain, k_gain):
    bound = HEAD_DIM * QK_SCALE_LOG2 * jnp.max(jnp.abs(q_gain)) * jnp.max(jnp.abs(k_gain))
    return (1.02 * bound).reshape(1).astype(F32)


def _slc_win(qt, bias, kaug, vt, oc, gt, score_bound):
    b, _, s = qt.shape
    nblk = bias.shape[2]
    ns, tk = vt.shape[3], vt.shape[5]
    tq = tk
    key_rel, t_rel = np.arange(tk)[:, None], np.arange(tq)[None, :]
    mask_bias = jnp.asarray(np.stack([np.where(key_rel <= t_rel, 0.0, NEG),
                                      np.where(t_rel + tk - key_rel < WINDOW, 0.0, NEG)]), F32)
    qblk = pl.BlockSpec((None, 256, tq), lambda bi, g, i: (bi, g, i))
    return pl.pallas_call(
        _slcwin_body, grid=(b, NSA_GROUPS, s // tq),
        in_specs=[pl.BlockSpec(memory_space=pltpu.SMEM), qblk,
                  pl.BlockSpec((None, None, nblk, tq), lambda bi, g, i: (bi, g, 0, i)),
                  pl.BlockSpec((None, None, 2, s, 128), lambda bi, g, i: (bi, g, 0, 0, 0)),
                  pl.BlockSpec((None, None, 2, ns, V_ROWS, tk), lambda bi, g, i: (bi, g, 0, 0, 0, 0)),
                  qblk,
                  pl.BlockSpec((None, None, GATE_ROWS, tq), lambda bi, g, i: (bi, g, 0, i)),
                  _resident((2, tk, tq))],
        out_specs=qblk,
        out_shape=jax.ShapeDtypeStruct((b, 512, s), F32),
        scratch_shapes=[pltpu.VMEM((5, NSA_HPG, 128, tq), BF16),
                        pltpu.VMEM((2, NSA_HPG, 1, tq), F32), pltpu.VMEM((2, NSA_HPG, V_ROWS, tq), F32),
                        pltpu.VMEM((NSA_HPG, tk, tq), F32), pltpu.VMEM((NSA_HPG, tk, tq), BF16)],
        compiler_params=_cparams(("parallel", "parallel", "arbitrary")), name="nsa_selected_window",
    )(score_bound, qt, bias, kaug, vt, oc, gt.reshape(b, NSA_GROUPS, GATE_ROWS, s), mask_bias)


def _hgrn_consts():
    c = HGRN_CHUNK
    t = np.arange(c)
    lower = (t[None, :] <= t[:, None]).astype(np.float32)
    rows = [lower]
    masks = []
    for half in HGRN_LEVELS:
        mid = (t // (2 * half)) * (2 * half) + half - 1
        if half < 8:
            rows.append(lower[mid])
        same = (t[:, None] // (2 * half)) == (t[None, :] // (2 * half))
        right = (t[:, None] & half) != 0
        left = (t[None, :] & half) == 0
        masks.append((same & right & left).astype(np.float32))
    masks.append(np.eye(c, dtype=np.float32))
    masks.append(((t[:, None] // HGRN_SUB == t[None, :] // HGRN_SUB) & (t[None, :] <= t[:, None])).astype(np.float32))
    mall = np.concatenate(rows, axis=0)
    lvl = np.stack([np.tile(mk.T, (1, HGRN_HEADS)) for mk in masks])
    bdm = np.kron(np.eye(HGRN_HEADS), np.ones((64, 64), np.float32))
    return jnp.asarray(mall, BF16), jnp.asarray(lvl, F32), jnp.asarray(bdm, F32), jnp.asarray(bdm / 64, BF16)


def _hgrn_body(hg_ref, lb_ref, og_ref, mall_ref, lvl_ref, bdm_ref, bdn_ref, y_ref, st_scr, attn_scr):
    c = HGRN_CHUNK
    w = HGRN_WIDTH

    @pl.when(pl.program_id(1) == 0)
    def _():
        st_scr[...] = jnp.zeros(st_scr.shape, F32)

    chunks = range(hg_ref.shape[0] // c)
    lb = lb_ref[...]
    lane = lax.broadcasted_iota(jnp.int32, (c, w), 1)
    head_masks = [(lane >> 6) == hh for hh in range(HGRN_HEADS)]
    nlev = len(HGRN_LEVELS)

    def stack_heads(x):
        x16 = x.astype(BF16)
        return jnp.concatenate([jnp.where(hm, x16, 0) for hm in head_masks], axis=0)

    def row_bcast(x, half):
        return jnp.concatenate([jnp.broadcast_to(x[p + half - 1:p + half, :], (2 * half, w))
                                for p in range(0, c, 2 * half)], axis=0)

    qa, kk, v16, logf = [], [], [], []
    for ci in chunks:
        rows = slice(ci * c, (ci + 1) * c)
        qa.append(_silu(hg_ref[rows, 0:w]) * (HEAD_DIM ** -0.5))
        fg = lb + (1.0 - lb) * _sigmoid(hg_ref[rows, w:2 * w])
        kk.append(1.0 - fg)
        logf.append(jnp.log2(fg))
        v16.append(hg_ref[rows, 2 * w:3 * w].astype(BF16))

    mall = mall_ref[...]
    parts = _split3(jnp.concatenate(logf, axis=1))
    b_all = sum(_dot(mall[0:c], part) for part in parts)
    bcum = [b_all[:, ci * w:(ci + 1) * w] for ci in chunks]

    def level(ci, li, ref_pt):
        e = jnp.exp2(-jnp.abs(bcum[ci] - ref_pt))
        return lvl_ref[li] * _dot_nt((kk[ci] * e).astype(BF16), stack_heads(qa[ci] * e))

    coarse = [li for li, half in enumerate(HGRN_LEVELS) if half >= HGRN_SUB]
    if not coarse:
        attn_scr[...] = jnp.zeros(attn_scr.shape, F32)
    for li in coarse:
        for ci in chunks:
            contribution = level(ci, li, row_bcast(bcum[ci], HGRN_LEVELS[li]))
            attn_scr[ci] = contribution if li == coarse[0] else attn_scr[ci] + contribution

    def block_start(x):
        firsts = [jnp.zeros((HGRN_SUB, w), F32)]
        firsts += [jnp.broadcast_to(x[p - 1:p, :], (HGRN_SUB, w)) for p in range(HGRN_SUB, c, HGRN_SUB)]
        return jnp.concatenate(firsts, axis=0)

    expo = [block_start(bcum[ci]) - bcum[ci] for ci in chunks]
    largest = expo[0]
    for ci in chunks[1:]:
        largest = jnp.maximum(largest, expo[ci])
    single_reference_ok = jnp.max(largest) < HGRN_MAX_EXPONENT

    @pl.when(single_reference_ok)
    def _():
        for ci in chunks:
            kt = (kk[ci] * jnp.exp2(expo[ci])).astype(BF16)
            attn_scr[ci] = attn_scr[ci] + lvl_ref[nlev + 1] * _dot_nt(kt, stack_heads(qa[ci] * jnp.exp2(-expo[ci])))

    @pl.when(jnp.logical_not(single_reference_ok))
    def _():
        r_fine = sum(_dot(mall[c:], part) for part in parts)
        for ci in chunks:
            attn_scr[ci] = attn_scr[ci] + lvl_ref[nlev] * _dot_nt(kk[ci].astype(BF16), stack_heads(qa[ci]))
        fine = 0
        for li, half in enumerate(HGRN_LEVELS):
            if half >= HGRN_SUB:
                continue
            for ci in chunks:
                if half >= 8:
                    ref_pt = row_bcast(bcum[ci], half)
                else:
                    ref_pt = r_fine[fine * c:(fine + 1) * c, ci * w:(ci + 1) * w]
                attn_scr[ci] = attn_scr[ci] + level(ci, li, ref_pt)
            if half < 8:
                fine += 1

    attn = [attn_scr[ci] for ci in chunks]

    intra, upd, decay, qb = [], [], [], []
    for ci in chunks:
        x = _dot_tn(attn[ci].astype(BF16), v16[ci])
        intra.append(sum(jnp.where(head_masks[hh], x[hh * c:(hh + 1) * c], 0.0) for hh in range(HGRN_HEADS)))
        b_last = bcum[ci][c - 1:c, :]
        kl = (kk[ci] * jnp.exp2(b_last - bcum[ci])).astype(BF16)
        upd.append(bdm_ref[...] * _dot_tn(v16[ci], kl))
        decay.append(jnp.exp2(b_last))
        qb.append((qa[ci] * jnp.exp2(bcum[ci])).astype(BF16))

    st = st_scr[...]
    inter = []
    for ci in chunks:
        inter.append(_dot_nt(qb[ci], st.astype(BF16)))
        st = st * decay[ci] + upd[ci]
    st_scr[...] = st

    for ci in chunks:
        rows = slice(ci * c, (ci + 1) * c)
        o = inter[ci] + intra[ci]
        hi, lo = _split2(o * o)
        ms = _dot(hi, bdn_ref[...]) + _dot(lo, bdn_ref[...])
        y_ref[rows, :] = o * lax.rsqrt(ms + EPS) * og_ref[...] * _silu(hg_ref[rows, 3 * w:4 * w])


def _hgrn(hg, lower_bound, out_gain):
    b, s, _ = hg.shape
    rows = HGRN_CB * HGRN_CHUNK
    mall, lvl, bdm, bdn = _hgrn_consts()
    full = lambda shape: pl.BlockSpec(shape, lambda bi, i: (0,) * len(shape))
    return pl.pallas_call(
        _hgrn_body, grid=(b, s // rows),
        in_specs=[pl.BlockSpec((None, rows, 4 * HGRN_WIDTH), lambda bi, i: (bi, i, 0)),
                  full((1, HGRN_WIDTH)), full((1, HGRN_WIDTH)),
                  full(mall.shape), full(lvl.shape), full(bdm.shape), full(bdn.shape)],
        out_specs=pl.BlockSpec((None, rows, HGRN_WIDTH), lambda bi, i: (bi, i, 0)),
        out_shape=jax.ShapeDtypeStruct((b, s, HGRN_WIDTH), F32),
        scratch_shapes=[pltpu.VMEM((HGRN_WIDTH, HGRN_WIDTH), F32),
                        pltpu.VMEM((HGRN_CB, HGRN_CHUNK, HGRN_HEADS * HGRN_CHUNK), F32)],
        compiler_params=_cparams(("parallel", "arbitrary")), name="hgrn2_chunks",
    )(hg, lower_bound.reshape(1, -1), out_gain.reshape(1, -1), mall, lvl, bdm, bdn)


def _memkv_body(mem_ref, mg_ref, wk_ref, wvt_ref, kg_ref, kh_ref, vht_ref):
    m = mem_ref.shape[0]
    mn = _rms_rows(mem_ref[...], mg_ref[...]).astype(BF16)
    k = _dot(mn, wk_ref[...])
    vt = _dot_nt(wvt_ref[...], mn)
    ones_rows = (lax.broadcasted_iota(jnp.int32, (V_ROWS - 64, m), 0) == 0).astype(BF16)
    for hh in range(MEM_HEADS):
        kh_ref[hh] = _rms_rows(k[:, hh * 64:(hh + 1) * 64], kg_ref[...]).astype(BF16)
        vht_ref[hh, 0:64, :] = vt[hh * 64:(hh + 1) * 64].astype(BF16)
        vht_ref[hh, 64:V_ROWS, :] = ones_rows


def _mem_kv(mem, mem_gain, wk, wvt, k_gain):
    b, m, d = mem.shape
    full = lambda shape: pl.BlockSpec(shape, lambda bi: (0,) * len(shape))
    return pl.pallas_call(
        _memkv_body, grid=(b,),
        in_specs=[pl.BlockSpec((None, m, d), lambda bi: (bi, 0, 0)), full((1, d)),
                  full(wk.shape), full(wvt.shape), full((1, 64))],
        out_specs=(pl.BlockSpec((None, MEM_HEADS, m, 64), lambda bi: (bi, 0, 0, 0)),
                   pl.BlockSpec((None, MEM_HEADS, V_ROWS, m), lambda bi: (bi, 0, 0, 0))),
        out_shape=(jax.ShapeDtypeStruct((b, MEM_HEADS, m, 64), BF16),
                   jax.ShapeDtypeStruct((b, MEM_HEADS, V_ROWS, m), BF16)),
        compiler_params=_cparams(("parallel",)), name="memory_kv",
    )(mem, mem_gain.reshape(1, d), wk, wvt, k_gain.reshape(1, 64))


def _out_body(x_ref, ynt_ref, yh_ref, qmt_ref, kh_ref, vht_ref, ng_ref, mg_ref, wo_ref,
              fg_ref, wg_ref, wu_ref, wd_ref, o_ref, a_scr):
    scores = [_dot(kh_ref[hh], qmt_ref[hh * 64:(hh + 1) * 64, :]) for hh in range(MEM_HEADS)]
    nsa = _rms_cols(ynt_ref[...], ng_ref[...]).astype(BF16)
    acc = _dot_tn(nsa, wo_ref[0:NSA_WIDTH, :])
    acc = acc + _dot(yh_ref[...].astype(BF16), wo_ref[NSA_WIDTH:NSA_WIDTH + HGRN_WIDTH, :])
    y_mem = []
    for s in scores:
        hh = len(y_mem)
        p = jnp.exp2(s - jnp.max(s, axis=0, keepdims=True))
        o = _dot(vht_ref[hh], p.astype(BF16))
        y_mem.append(o[0:64] * (1.0 / o[64:65]))
    mem = _rms_cols(jnp.concatenate(y_mem, axis=0), mg_ref[...]).astype(BF16)
    acc = acc + _dot_tn(mem, wo_ref[NSA_WIDTH + HGRN_WIDTH:, :])
    o_ref[...] = _ffn_half_step(x_ref[...] + acc, fg_ref, wg_ref, wu_ref, wd_ref, a_scr)


def _out_ffn(x3d, ynt, yh, qmt, kh, vht, nsa_gain, mem_gain, wo, ffn_gain, wg, wu, wd, *, tm=TOK_TILE):
    b, s, d = x3d.shape
    m = kh.shape[2]
    return pl.pallas_call(
        _out_body, grid=(b, s // tm),
        in_specs=[pl.BlockSpec((None, tm, d), lambda bi, i: (bi, i, 0)),
                  pl.BlockSpec((None, NSA_WIDTH, tm), lambda bi, i: (bi, 0, i)),
                  pl.BlockSpec((None, tm, HGRN_WIDTH), lambda bi, i: (bi, i, 0)),
                  pl.BlockSpec((None, MEM_WIDTH, tm), lambda bi, i: (bi, 0, i)),
                  pl.BlockSpec((None, MEM_HEADS, m, 64), lambda bi, i: (bi, 0, 0, 0)),
                  pl.BlockSpec((None, MEM_HEADS, V_ROWS, m), lambda bi, i: (bi, 0, 0, 0)),
                  _resident((NSA_WIDTH, 1)), _resident((MEM_WIDTH, 1)), _resident(wo.shape),
                  _resident((1, d)), _resident(wg.shape), _resident(wu.shape), _resident(wd.shape)],
        out_specs=pl.BlockSpec((None, tm, d), lambda bi, i: (bi, i, 0)),
        out_shape=jax.ShapeDtypeStruct((b, s, d), F32),
        scratch_shapes=[pltpu.VMEM((tm, wg.shape[1]), BF16)],
        compiler_params=_cparams(("parallel", "parallel")), name="mix_out_ffn2",
    )(x3d, ynt, yh, qmt, kh, vht, nsa_gain.reshape(-1, 1), mem_gain.reshape(-1, 1), wo,
      ffn_gain.reshape(1, d), wg, wu, wd)


def _layer(x, mem, ffn1, ffn2, mix_norm, w_in, w_out, nsa_q_norm, nsa_k_norm, cmp_pos_k, cmp_w1_k, cmp_w2_k,
           cmp_pos_v, cmp_w1_v, cmp_w2_v, nsa_out_norm, lower_bound, hgrn_out_norm,
           mem_norm, mem_w_k, mem_w_v, mem_q_norm, mem_k_norm, mem_out_norm):
    b, s, d = x.shape
    sizes = (512, 128, 128, 128, 128, 128, 128, 24, 256, 256, 256, 256, 256)
    offs = np.concatenate([[0], np.cumsum(sizes)])
    w16 = w_in.astype(BF16)
    col = lambda i: w16[:, offs[i]:offs[i + 1]]
    (q_a, k_c, v_c, k_s, v_s, k_w, v_w, g_a, q_h, f_h, i_h, g_h, q_m) = [col(i) for i in range(13)]
    gpad = jnp.zeros((d, GATE_ROWS - 3 * NSA_HPG), BF16)
    wt = jnp.concatenate([q_a, v_s, v_w, g_a[:, :3 * NSA_HPG], gpad, g_a[:, 3 * NSA_HPG:], gpad, q_m], axis=1).T
    wn = jnp.concatenate([k_c, v_c, k_s, k_w, q_h, f_h, i_h, g_h], axis=1)

    x1, qt, vt, gt, qmt, kaug, kvc, hg = _ffn_proj(
        x, *ffn1, mix_norm, wt, wn, nsa_q_norm, nsa_k_norm, mem_q_norm, _rope_tables(s))

    cmp_pos, cmp_w1 = _compress_weights(cmp_pos_k, cmp_pos_v, cmp_w1_k, cmp_w1_v)
    kc, vct = _compress(kvc, cmp_pos, cmp_w1, cmp_w2_k.astype(BF16), cmp_w2_v.T.astype(BF16), nsa_k_norm)
    score_bound = _score_bound(nsa_q_norm, nsa_k_norm)
    oc, bias = _cmp_select(qt, kc, vct, score_bound)
    y_nsa = _slc_win(qt, bias, kaug, vt, oc, gt, score_bound)

    y_hgrn = _hgrn(hg, lower_bound, hgrn_out_norm)

    kh, vht = _mem_kv(mem, mem_norm, mem_w_k.astype(BF16), mem_w_v.T.astype(BF16), mem_k_norm)
    return _out_ffn(x1, y_nsa, y_hgrn, qmt, kh, vht, nsa_out_norm, mem_out_norm, w_out.astype(BF16), *ffn2)


def kernel(x, mem, ffn1_norm, ffn1_w_gate, ffn1_w_up, ffn1_w_down, mix_norm, w_in, w_out, nsa_q_norm, nsa_k_norm, cmp_pos_k, cmp_w1_k, cmp_w2_k, cmp_pos_v, cmp_w1_v, cmp_w2_v, nsa_out_norm, hgrn_lb_logits, hgrn_out_norm, mem_norm, mem_w_k, mem_w_v, mem_q_norm, mem_k_norm, mem_out_norm, ffn2_norm, ffn2_w_gate, ffn2_w_up, ffn2_w_down):
    b, s, d = x.shape
    depth = ffn1_norm.shape[0]
    lower_bounds = jnp.cumsum(jax.nn.softmax(hgrn_lb_logits.astype(F32), axis=0), axis=0)
    bf = lambda a: a.astype(BF16)
    for l in range(depth):
        x = _layer(x, mem, (ffn1_norm[l], bf(ffn1_w_gate[l]), bf(ffn1_w_up[l]), bf(ffn1_w_down[l])),
                   (ffn2_norm[l], bf(ffn2_w_gate[l]), bf(ffn2_w_up[l]), bf(ffn2_w_down[l])),
                   mix_norm[l], w_in[l], w_out[l], nsa_q_norm[l], nsa_k_norm[l],
                   cmp_pos_k[l], cmp_w1_k[l], cmp_w2_k[l], cmp_pos_v[l], cmp_w1_v[l], cmp_w2_v[l],
                   nsa_out_norm[l], lower_bounds[l], hgrn_out_norm[l],
                   mem_norm[l], mem_w_k[l], mem_w_v[l], mem_q_norm[l], mem_k_norm[l], mem_out_norm[l])
    return x
```

```python
import functools

import numpy as np
import jax
import jax.numpy as jnp
from jax import lax
from jax.experimental import pallas as pl
from jax.experimental.pallas import tpu as pltpu

F32 = jnp.float32
BF16 = jnp.bfloat16

HEAD_DIM = 64
ROT_DIM = 16
ROT_HALF = 8
ROPE_THETA = 500000.0
NSA_HEADS = 8
NSA_GROUPS = 2
NSA_HPG = 4
CMP_BLOCK = 32
CMP_STRIDE = 16
SLC_BLOCK = 64
SLC_SHIFT = 6
SLC_TOPK = 16
WINDOW = 512
FORCED_SCORE = 1e4
HGRN_HEADS = 4
HGRN_CHUNK = 64
HGRN_WIDTH = 256
MEM_HEADS = 4
MEM_WIDTH = 256
NSA_WIDTH = 512
EPS = 1e-6
NEG = -1e30
QK_SCALE_LOG2 = HEAD_DIM ** -0.5 * 1.4426950408889634
MIN_DENOMINATOR = 2.0 ** -64

VMEM_LIMIT = 56 * 1024 * 1024
MAX_BLOCKS = 128
GATE_ROWS = 16
V_ROWS = 80
TOK_TILE = 512
HGRN_CB = 16
HGRN_LEVELS = (32, 16, 8, 4, 2, 1)
HGRN_SUB = 64
HGRN_MAX_EXPONENT = 96.0

NT_DIMS = (((1,), (1,)), ((), ()))
TN_DIMS = (((0,), (0,)), ((), ()))


def _cparams(sem):
    return pltpu.CompilerParams(dimension_semantics=sem, vmem_limit_bytes=VMEM_LIMIT)


def _dot(a, b):
    return jnp.dot(a, b, preferred_element_type=F32)


def _dot_nt(a, b):
    return lax.dot_general(a, b, NT_DIMS, preferred_element_type=F32)


def _dot_tn(a, b):
    return lax.dot_general(a, b, TN_DIMS, preferred_element_type=F32)


def _sigmoid(x):
    return 1.0 / (1.0 + jnp.exp(-x))


def _silu(x):
    return x * _sigmoid(x)


def _split2(x):
    hi = x.astype(BF16)
    lo = (x - hi.astype(F32)).astype(BF16)
    return hi, lo


def _split3(x):
    hi = x.astype(BF16)
    r1 = x - hi.astype(F32)
    mid = r1.astype(BF16)
    lo = (r1 - mid.astype(F32)).astype(BF16)
    return hi, mid, lo


def _rms_rows(x, gain_row):
    ms = jnp.mean(x * x, axis=-1, keepdims=True)
    return x * lax.rsqrt(ms + EPS) * gain_row


def _rms_cols(x, gain_col):
    ms = jnp.mean(x * x, axis=0, keepdims=True)
    return x * lax.rsqrt(ms + EPS) * gain_col


def _seg_mean_sq(x, bd):
    hi, lo = _split2(x * x)
    return _dot(hi, bd) + _dot(lo, bd)


FFN_CHUNK = 256


def _ffn_half_step(x, g_ref, wg_ref, wu_ref, wd_ref, a_scr):
    xn = _rms_rows(x, g_ref[...]).astype(BF16)
    d_ff = wg_ref.shape[1]
    for c in range(d_ff // FFN_CHUNK):
        sl = slice(c * FFN_CHUNK, (c + 1) * FFN_CHUNK)
        g = _dot(xn, wg_ref[:, sl])
        u = _dot(xn, wu_ref[:, sl])
        a_scr[:, sl] = (_silu(g) * u).astype(BF16)
    return x + 0.5 * _dot(a_scr[...], wd_ref[...])


def _resident(shape):
    return pl.BlockSpec(shape, lambda *_: (0,) * len(shape), pipeline_mode=pl.Buffered(1))


def _rope_cols(xn, cos, sin):
    x0, x1 = xn[0:ROT_HALF], xn[ROT_HALF:ROT_DIM]
    return jnp.concatenate([x0 * cos - x1 * sin, x1 * cos + x0 * sin, xn[ROT_DIM:]], axis=0)


def _rope_rows(x, cn, sa, sb):
    return x * cn + pltpu.roll(x, 128 - ROT_HALF, 1) * sa + pltpu.roll(x, ROT_HALF, 1) * sb


def _proj_body(x_ref, fg_ref, wg_ref, wu_ref, wd_ref, mg_ref, wt_ref, wn_ref, qg_ref, kg_ref, mqg_ref,
               cos_ref, sin_ref, cn_ref, sa_ref, sb_ref, bd_ref,
               x1_ref, qt_ref, vt_ref, gt_ref, qmt_ref, kaug_ref, kvc_ref, hg_ref, a_scr):
    tm = x_ref.shape[0]
    assert tm == 8 * SLC_BLOCK
    x1 = _ffn_half_step(x_ref[...], fg_ref, wg_ref, wu_ref, wd_ref, a_scr)
    x1_ref[...] = x1
    h = _rms_rows(x1, mg_ref[...]).astype(BF16)

    qg, mqg = qg_ref[...], mqg_ref[...]
    half = tm // 2
    ones_rows = (lax.broadcasted_iota(jnp.int32, (V_ROWS - 64, half), 0) == 0).astype(BF16)
    for part in range(2):
        tok = slice(part * half, (part + 1) * half)
        pt = _dot_nt(wt_ref[...], h[tok, :])
        cos, sin = cos_ref[:, tok], sin_ref[:, tok]
        for hh in range(NSA_HEADS):
            xq = _rms_cols(pt[hh * 64:(hh + 1) * 64], qg)
            qt_ref[hh * 64:(hh + 1) * 64, tok] = (_rope_cols(xq, cos, sin) * QK_SCALE_LOG2).astype(BF16)
        for g in range(NSA_GROUPS):
            for br in range(2):
                rows = 512 + br * 128 + g * 64
                vt_ref[g, br, 0:64, tok] = pt[rows:rows + 64].astype(BF16)
                vt_ref[g, br, 64:V_ROWS, tok] = ones_rows
        gt_ref[:, tok] = _sigmoid(pt[768:800])
        for hh in range(MEM_HEADS):
            xm = _rms_cols(pt[800 + hh * 64:864 + hh * 64], mqg)
            qmt_ref[hh * 64:(hh + 1) * 64, tok] = (xm * QK_SCALE_LOG2).astype(BF16)

    cn, sa, sb = cn_ref[...], sa_ref[...], sb_ref[...]
    bd, kg = bd_ref[...], kg_ref[...]
    pc = _dot(h, wn_ref[:, 0:256])
    pk = _dot(h, wn_ref[:, 256:512])
    kvc_ref[0] = _rope_rows(pc[:, 0:128], cn, sa, sb)
    kvc_ref[1] = pc[:, 128:256]
    ph = _dot(h, wn_ref[:, 512:1024])
    ks = pk[:, 0:128]
    kw = pk[:, 128:256]
    ks = _rope_rows(ks * lax.rsqrt(_seg_mean_sq(ks, bd) + EPS) * kg, cn, sa, sb)
    kw = _rope_rows(kw * lax.rsqrt(_seg_mean_sq(kw, bd) + EPS) * kg, cn, sa, sb)
    lane = lax.broadcasted_iota(jnp.int32, (tm, 128), 1)
    row = lax.broadcasted_iota(jnp.int32, (tm, 128), 0)
    onehot = jnp.where(lane - 64 == (row >> SLC_SHIFT), 1.0, 0.0)
    lo_half = lane < 64
    kaug_ref[0, 0] = jnp.where(lo_half, ks, onehot).astype(BF16)
    kaug_ref[0, 1] = jnp.where(lo_half, kw, 0.0).astype(BF16)
    kaug_ref[1, 0] = jnp.where(lo_half, pltpu.roll(ks, 64, 1), onehot).astype(BF16)
    kaug_ref[1, 1] = jnp.where(lo_half, pltpu.roll(kw, 64, 1), 0.0).astype(BF16)
    hg_ref[:, 0:512] = ph
    hg_ref[:, 512:1024] = _dot(h, wn_ref[:, 1024:1536])


def _ffn_proj(x3d, ffn_gain, wg, wu, wd, mix_gain, wt, wn, q_gain, k_gain, mq_gain, rope):
    b, s, d = x3d.shape
    tm = TOK_TILE
    ns = s // tm
    cos_t, sin_t, cn, sa, sb = rope
    bd = jnp.asarray(np.kron(np.eye(2), np.full((64, 64), 1.0 / 64)), BF16)
    full = _resident
    out_shape = (
        jax.ShapeDtypeStruct((b, s, d), F32),
        jax.ShapeDtypeStruct((b, 512, s), BF16),
        jax.ShapeDtypeStruct((b, 2, 2, ns, V_ROWS, tm), BF16),
        jax.ShapeDtypeStruct((b, 32, s), F32),
        jax.ShapeDtypeStruct((b, 256, s), BF16),
        jax.ShapeDtypeStruct((b, 2, 2, s, 128), BF16),
        jax.ShapeDtypeStruct((b, 2, s, 128), F32),
        jax.ShapeDtypeStruct((b, s, 1024), F32),
    )
    out_specs = (
        pl.BlockSpec((None, tm, d), lambda bi, i: (bi, i, 0)),
        pl.BlockSpec((None, 512, tm), lambda bi, i: (bi, 0, i)),
        pl.BlockSpec((None, 2, 2, None, V_ROWS, tm), lambda bi, i: (bi, 0, 0, i, 0, 0)),
        pl.BlockSpec((None, 32, tm), lambda bi, i: (bi, 0, i)),
        pl.BlockSpec((None, 256, tm), lambda bi, i: (bi, 0, i)),
        pl.BlockSpec((None, 2, 2, tm, 128), lambda bi, i: (bi, 0, 0, i, 0)),
        pl.BlockSpec((None, 2, tm, 128), lambda bi, i: (bi, 0, i, 0)),
        pl.BlockSpec((None, tm, 1024), lambda bi, i: (bi, i, 0)),
    )
    in_specs = [
        pl.BlockSpec((None, tm, d), lambda bi, i: (bi, i, 0)),
        full((1, d)), full(wg.shape), full(wu.shape), full(wd.shape),
        full((1, d)), full(wt.shape), full(wn.shape),
        full((64, 1)), full((1, 128)), full((64, 1)),
        pl.BlockSpec((ROT_HALF, tm), lambda bi, i: (0, i)),
        pl.BlockSpec((ROT_HALF, tm), lambda bi, i: (0, i)),
        pl.BlockSpec((tm, 128), lambda bi, i: (i, 0)),
        pl.BlockSpec((tm, 128), lambda bi, i: (i, 0)),
        pl.BlockSpec((tm, 128), lambda bi, i: (i, 0)),
        full((128, 128)),
    ]
    return pl.pallas_call(
        _proj_body, grid=(b, ns), in_specs=in_specs, out_specs=out_specs, out_shape=out_shape,
        scratch_shapes=[pltpu.VMEM((tm, wg.shape[1]), BF16)],
        compiler_params=_cparams(("parallel", "parallel")), name="ffn1_mix_projection",
    )(x3d, ffn_gain.reshape(1, d), wg, wu, wd, mix_gain.reshape(1, d), wt, wn, q_gain.reshape(64, 1),
      jnp.tile(k_gain.reshape(1, 64), (1, 2)), mq_gain.reshape(64, 1), cos_t, sin_t, cn, sa, sb, bd)


def _rope_tables(s):
    pos = jnp.arange(s, dtype=F32)
    inv = ROPE_THETA ** (-(jnp.arange(0, ROT_DIM, 2, dtype=F32) / ROT_DIM))
    ang = pos[:, None] * inv[None, :]
    cos, sin = jnp.cos(ang), jnp.sin(ang)
    zeros = jnp.zeros((s, 64 - ROT_DIM), F32)
    cn = jnp.concatenate([cos, cos, jnp.ones((s, 64 - ROT_DIM), F32)], axis=1)
    sa = jnp.concatenate([-sin, jnp.zeros((s, ROT_HALF), F32), zeros], axis=1)
    sb = jnp.concatenate([jnp.zeros((s, ROT_HALF), F32), sin, zeros], axis=1)
    tile2 = lambda a: jnp.concatenate([a, a], axis=1)
    return cos.T, sin.T, tile2(cn), tile2(sa), tile2(sb)


def _cmp_body(kvc_ref, pos_ref, w1_ref, w2k_ref, w2vt_ref, kg_ref, kc_ref, vct_ref):
    nc = kvc_ref.shape[1] // CMP_STRIDE
    for kind in range(2):
        halves = []
        for part in range(2):
            x = jnp.concatenate(
                [(kvc_ref[kind, pl.ds(r, nc, stride=CMP_STRIDE), :]
                  + pos_ref[kind, part, :, r * 128:(r + 1) * 128]).astype(BF16) for r in range(CMP_STRIDE)],
                axis=1)
            halves.append(x)
        for g in range(NSA_GROUPS):
            second = _dot(halves[1], w1_ref[kind, g, 1])
            hid = _silu(_dot(halves[0], w1_ref[kind, g, 0]) + pltpu.roll(second, nc - 1, 0)).astype(BF16)
            if kind == 0:
                kc_ref[g] = _rms_rows(_dot(hid, w2k_ref[...]), kg_ref[...]).astype(BF16)
            else:
                vct_ref[g] = _dot_nt(w2vt_ref[...], hid).astype(BF16)


def _compress(kvc, pos, w1, w2k, w2vt, k_gain):
    b, _, s, _ = kvc.shape
    nc = s // CMP_STRIDE
    return pl.pallas_call(
        _cmp_body, grid=(b,),
        in_specs=[pl.BlockSpec((None, 2, s, 128), lambda bi: (bi, 0, 0, 0)),
                  _resident(pos.shape), _resident(w1.shape), _resident(w2k.shape), _resident(w2vt.shape),
                  _resident((1, 64))],
        out_specs=(pl.BlockSpec((None, 2, nc, 64), lambda bi: (bi, 0, 0, 0)),
                   pl.BlockSpec((None, 2, 64, nc), lambda bi: (bi, 0, 0, 0))),
        out_shape=(jax.ShapeDtypeStruct((b, 2, nc, 64), BF16), jax.ShapeDtypeStruct((b, 2, 64, nc), BF16)),
        compiler_params=_cparams(("parallel",)), name="nsa_compress",
    )(kvc, pos, w1, w2k, w2vt, k_gain.reshape(1, 64))


def _compress_weights(pos_k, pos_v, w1_k, w1_v):
    def pos_part(p):
        p = p.reshape(2, CMP_STRIDE, 1, 64)
        return jnp.broadcast_to(p, (2, CMP_STRIDE, NSA_GROUPS, 64)).reshape(2, 1, CMP_STRIDE * 128)

    def w1_part(w):
        hdim = w.shape[1]
        w = w.reshape(2, CMP_STRIDE, 1, 64, hdim)
        per_group = []
        for g in range(NSA_GROUPS):
            pads = [w if gg == g else jnp.zeros_like(w) for gg in range(NSA_GROUPS)]
            per_group.append(jnp.concatenate(pads, axis=2).reshape(2, CMP_STRIDE * 128, hdim))
        return jnp.stack(per_group)

    pos = jnp.stack([pos_part(pos_k), pos_part(pos_v)])
    w1 = jnp.stack([w1_part(w1_k), w1_part(w1_v)]).astype(BF16)
    return pos, w1


CMP_CLASS_ROWS = 128


def _cmpsel_variant(nc, nblk, fixed_reference, m0_ref, qt_ref, kc_ref, vct_ref, oc_ref, bias_ref, s_scr, flag_scr):
    tq = qt_ref.shape[1]
    t0 = pl.program_id(2) * tq
    n_idx = lax.broadcasted_iota(jnp.int32, (nc, tq), 0)
    t_idx = t0 + lax.broadcasted_iota(jnp.int32, (nc, tq), 1)
    mask_bias = jnp.where(n_idx * CMP_STRIDE + (CMP_BLOCK - 1) <= t_idx, 0.0, NEG)
    sees_any = t0 + lax.broadcasted_iota(jnp.int32, (1, tq), 1) >= CMP_BLOCK - 1
    kc = kc_ref[0:nc, :]
    if fixed_reference:
        mask_bias = mask_bias - m0_ref[0]
    else:
        for hh in range(NSA_HPG):
            s_scr[hh, 0:nc, :] = _dot(kc, qt_ref[hh * 64:(hh + 1) * 64, :]) + mask_bias
    jj = lax.broadcasted_iota(jnp.int32, (nblk, nc), 0)
    nn = lax.broadcasted_iota(jnp.int32, (nblk, nc), 1)
    ov = jnp.where((nn * CMP_STRIDE < jj * SLC_BLOCK + SLC_BLOCK)
                   & (nn * CMP_STRIDE + CMP_BLOCK > jj * SLC_BLOCK), 1.0, 0.0).astype(BF16)
    ones_rows = (lax.broadcasted_iota(jnp.int32, (V_ROWS - 64, nc), 0) == 0).astype(BF16)
    lhs = jnp.concatenate([vct_ref[:, 0:nc], ones_rows, ov], axis=0)
    imp = jnp.zeros((nblk, tq), F32)
    l_min = jnp.full((1, tq), 1.0, F32)

    def probabilities(hh):
        if fixed_reference:
            return jnp.exp2(_dot(kc, qt_ref[hh * 64:(hh + 1) * 64, :]) + mask_bias).astype(BF16)
        m = jnp.max(s_scr[hh, 0:nc, :], axis=0, keepdims=True)
        return jnp.exp2(s_scr[hh, 0:nc, :] - m).astype(BF16)

    p_next = probabilities(0)
    for hh in range(NSA_HPG):
        p = p_next
        if hh + 1 < NSA_HPG:
            p_next = probabilities(hh + 1)
        r = _dot(lhs, p)
        l_min = jnp.minimum(l_min, jnp.where(sees_any, r[64:65], 1.0))
        inv_l = jnp.where(sees_any, 1.0 / r[64:65], 0.0)
        oc_ref[hh * 64:(hh + 1) * 64, :] = r[0:64] * inv_l
        imp = imp + r[V_ROWS:] * inv_l
    if fixed_reference:
        flag_scr[0] = jnp.where(jnp.min(l_min) > MIN_DENOMINATOR, 0, 1)

    j = lax.broadcasted_iota(jnp.int32, (nblk, tq), 0)
    cur = (t0 + lax.broadcasted_iota(jnp.int32, (nblk, tq), 1)) >> SLC_SHIFT
    forced = (j == 0) | (j == cur) | (j == cur - 1)
    picks = SLC_TOPK - 3
    forced_bias = jnp.where(forced & (j <= cur), 0.0, NEG)
    imp = jnp.where((j <= cur) & jnp.logical_not(forced), imp, -1.0)
    if nblk < bias_ref.shape[0]:
        bias_ref[nblk:, :] = jnp.full((bias_ref.shape[0] - nblk, tq), NEG, F32)

    rest = imp
    for _ in range(picks):
        v = jnp.max(rest, axis=0, keepdims=True)
        rest = jnp.where(rest == v, -3e38, rest)
    chosen = (imp >= v) & (imp >= 0.0)
    count = jnp.sum(jnp.where(chosen, 1.0, 0.0), axis=0, keepdims=True)
    valid = jnp.sum(jnp.where(imp >= 0.0, 1.0, 0.0), axis=0, keepdims=True)
    bias_ref[0:nblk, :] = jnp.where(chosen, 0.0, forced_bias)
    has_tie = jnp.max(jnp.abs(count - jnp.minimum(valid, float(picks)))) > 0.0

    @pl.when(has_tie)
    def _():
        jf = j.astype(F32)
        bias, rest = forced_bias, imp
        for _ in range(picks):
            v = jnp.max(rest, axis=0, keepdims=True)
            first = jnp.min(jnp.where(rest == v, jf, float(nblk)), axis=0, keepdims=True)
            pick = jf == first
            bias = jnp.where(pick & (v >= 0.0), 0.0, bias)
            rest = jnp.where(pick, -3e38, rest)
        bias_ref[0:nblk, :] = bias


def _cmpsel_body(m0_ref, qt_ref, kc_ref, vct_ref, oc_ref, bias_ref, s_scr, flag_scr):
    tq = qt_ref.shape[1]
    nc_total = kc_ref.shape[0]
    tiles_per_class = CMP_CLASS_ROWS // (tq // CMP_STRIDE)
    cls = pl.program_id(2) // tiles_per_class
    refs = (m0_ref, qt_ref, kc_ref, vct_ref, oc_ref, bias_ref, s_scr, flag_scr)
    for c in range(nc_total // CMP_CLASS_ROWS):
        nc = (c + 1) * CMP_CLASS_ROWS
        nblk = min(nc * CMP_STRIDE // SLC_BLOCK, bias_ref.shape[0])
        pl.when(cls == c)(functools.partial(_cmpsel_variant, nc, nblk, True, *refs))
    pl.when(flag_scr[0] != 0)(functools.partial(_cmpsel_variant, nc_total, bias_ref.shape[0], False, *refs))


def _cmp_select(qt, kc, vct, score_bound, *, tq=TOK_TILE):
    b, _, s = qt.shape
    nc = kc.shape[2]
    nblk = MAX_BLOCKS
    assert s // SLC_BLOCK <= MAX_BLOCKS and s // SLC_BLOCK >= SLC_TOPK and nc % CMP_CLASS_ROWS == 0
    return pl.pallas_call(
        _cmpsel_body, grid=(b, NSA_GROUPS, s // tq),
        in_specs=[pl.BlockSpec(memory_space=pltpu.SMEM),
                  pl.BlockSpec((None, 256, tq), lambda bi, g, i: (bi, g, i)),
                  pl.BlockSpec((None, None, nc, 64), lambda bi, g, i: (bi, g, 0, 0)),
                  pl.BlockSpec((None, None, 64, nc), lambda bi, g, i: (bi, g, 0, 0))],
        out_specs=(pl.BlockSpec((None, 256, tq), lambda bi, g, i: (bi, g, i)),
                   pl.BlockSpec((None, None, nblk, tq), lambda bi, g, i: (bi, g, 0, i))),
        out_shape=(jax.ShapeDtypeStruct((b, 512, s), F32),
                   jax.ShapeDtypeStruct((b, NSA_GROUPS, nblk, s), F32)),
        scratch_shapes=[pltpu.VMEM((NSA_HPG, nc, tq), F32), pltpu.SMEM((1,), jnp.int32)],
        compiler_params=_cparams(("parallel", "parallel", "parallel")), name="nsa_compressed_select",
    )(score_bound, qt, kc, vct)


def _flash_step(s_ref, vt, m_ref, acc_ref):
    m_old = m_ref[...]
    m_new = jnp.maximum(m_old, jnp.max(s_ref[...], axis=0, keepdims=True))
    p = jnp.exp2(s_ref[...] - m_new)
    acc_ref[...] = jnp.exp2(m_old - m_new) * acc_ref[...] + _dot(vt, p.astype(BF16))
    m_ref[...] = m_new


SEL, WIN = 0, 1
BIAS_ROWS = 16


def _slcwin_body(m0_ref, qt_ref, bias_ref, kaug_ref, vt_ref, oc_ref, gt_ref, mb_ref, y_ref,
                 q_scr, m_scr, acc_scr, s_scr, p_scr):
    tq = qt_ref.shape[1]
    tk = vt_ref.shape[3]
    assert tq == tk and WINDOW == tk and tk == 8 * SLC_BLOCK
    diag = pl.program_id(2)
    m0 = m0_ref[0]

    sel_slots = (0, 1, 2, 3)
    qs, qw = sel_slots[0], 4
    zeros = jnp.zeros((64, tq), BF16)
    for slot in sel_slots + (qw,):
        for hh in range(NSA_HPG):
            q_scr[slot, hh, 0:64, :] = qt_ref[hh * 64:(hh + 1) * 64, :]
            q_scr[slot, hh, 64:128, :] = zeros

    def set_selection_bias(kt, slot=qs):
        rows = bias_ref[pl.ds(pl.multiple_of(kt * 8, 8), 8), :]
        b16 = jnp.concatenate([rows, jnp.zeros_like(rows)], axis=0).astype(BF16)
        for hh in range(NSA_HPG):
            q_scr[slot, hh, 64:64 + BIAS_ROWS, :] = b16

    def tiles_fixed_reference(tiles):
        chains = [(br, qslot, kt, mask_bias, hh) for br, qslot, kt, mask_bias in tiles for hh in range(NSA_HPG)]
        for c in range(len(chains) + 1):
            if c < len(chains):
                br, qslot, kt, mask_bias, hh = chains[c]
                s = _dot(kaug_ref[br, pl.ds(pl.multiple_of(kt * tk, tk), tk), :], q_scr[qslot, hh])
                if mask_bias is not None:
                    s = s + mask_bias()
                p_scr[c % 4] = jnp.exp2(s - m0).astype(BF16)
            if c >= 1:
                br, _, kt, _, hh = chains[c - 1]
                acc_scr[br, hh] = acc_scr[br, hh] + _dot(vt_ref[br, kt], p_scr[(c - 1) % 4])

    def tile_running_max(br, qslot, kt, mask_bias=None):
        k = kaug_ref[br, pl.ds(pl.multiple_of(kt * tk, tk), tk), :]
        for hh in range(NSA_HPG):
            s = _dot(k, q_scr[qslot, hh])
            s_scr[hh] = s if mask_bias is None else s + mask_bias()
        for hh in range(NSA_HPG):
            _flash_step(s_scr.at[hh], vt_ref[br, kt], m_scr.at[br, hh], acc_scr.at[br, hh])

    prev = jnp.maximum(diag - 1, 0)
    no_prev = jnp.where(diag == 0, NEG, 0.0)
    band_bias = lambda: mb_ref[1] + no_prev
    causal_bias = lambda: mb_ref[0]

    acc_scr[...] = jnp.zeros(acc_scr.shape, F32)

    def unmasked_run(first_tile, count):
        for n in range(count):
            set_selection_bias(first_tile + n, sel_slots[n])
        tiles_fixed_reference([(SEL, sel_slots[n], first_tile + n, None) for n in range(count)])

    def tile_quad(j, carry):
        unmasked_run(4 * j, 4)
        return carry

    lax.fori_loop(0, diag >> 2, tile_quad, 0)
    pl.when((diag & 2) != 0)(lambda: unmasked_run((diag >> 2) * 4, 2))
    pl.when((diag & 1) != 0)(lambda: unmasked_run(diag - 1, 1))

    set_selection_bias(diag, qs)
    tiles_fixed_reference([(WIN, qw, prev, band_bias), (SEL, qs, diag, causal_bias), (WIN, qw, diag, causal_bias)])

    denominators = acc_scr[:, :, 64:65, :]
    underflow = jnp.logical_not(jnp.min(denominators) > MIN_DENOMINATOR)

    @pl.when(underflow)
    def _():
        m_scr[...] = jnp.full(m_scr.shape, NEG, F32)
        acc_scr[...] = jnp.zeros(acc_scr.shape, F32)

        def full_tile(kt, carry):
            set_selection_bias(kt, qs)
            tile_running_max(SEL, qs, kt)
            return carry

        lax.fori_loop(0, diag, full_tile, 0)
        set_selection_bias(diag, qs)
        tile_running_max(WIN, qw, prev, band_bias)
        tile_running_max(SEL, qs, diag, causal_bias)
        tile_running_max(WIN, qw, diag, causal_bias)

    gt = gt_ref[...]
    for hh in range(NSA_HPG):
        o_s = acc_scr[SEL, hh, 0:64, :] * (1.0 / acc_scr[SEL, hh, 64:65, :])
        o_w = acc_scr[WIN, hh, 0:64, :] * (1.0 / acc_scr[WIN, hh, 64:65, :])
        y_ref[hh * 64:(hh + 1) * 64, :] = (gt[3 * hh:3 * hh + 1] * oc_ref[hh * 64:(hh + 1) * 64, :]
                                          + gt[3 * hh + 1:3 * hh + 2] * o_s
                                          + gt[3 * hh + 2:3 * hh + 3] * o_w)


def _score_bound(q_gain, k_gain):
    bound = HEAD_DIM * QK_SCALE_LOG2 * jnp.max(jnp.abs(q_gain)) * jnp.max(jnp.abs(k_gain))
    return (1.02 * bound).reshape(1).astype(F32)


def _slc_win(qt, bias, kaug, vt, oc, gt, score_bound):
    b, _, s = qt.shape
    nblk = bias.shape[2]
    ns, tk = vt.shape[3], vt.shape[5]
    tq = tk
    key_rel, t_rel = np.arange(tk)[:, None], np.arange(tq)[None, :]
    mask_bias = jnp.asarray(np.stack([np.where(key_rel <= t_rel, 0.0, NEG),
                                      np.where(t_rel + tk - key_rel < WINDOW, 0.0, NEG)]), F32)
    qblk = pl.BlockSpec((None, 256, tq), lambda bi, g, i: (bi, g, i))
    return pl.pallas_call(
        _slcwin_body, grid=(b, NSA_GROUPS, s // tq),
        in_specs=[pl.BlockSpec(memory_space=pltpu.SMEM), qblk,
                  pl.BlockSpec((None, None, nblk, tq), lambda bi, g, i: (bi, g, 0, i)),
                  pl.BlockSpec((None, None, 2, s, 128), lambda bi, g, i: (bi, g, 0, 0, 0)),
                  pl.BlockSpec((None, None, 2, ns, V_ROWS, tk), lambda bi, g, i: (bi, g, 0, 0, 0, 0)),
                  qblk,
                  pl.BlockSpec((None, None, GATE_ROWS, tq), lambda bi, g, i: (bi, g, 0, i)),
                  _resident((2, tk, tq))],
        out_specs=qblk,
        out_shape=jax.ShapeDtypeStruct((b, 512, s), F32),
        scratch_shapes=[pltpu.VMEM((5, NSA_HPG, 128, tq), BF16),
                        pltpu.VMEM((2, NSA_HPG, 1, tq), F32), pltpu.VMEM((2, NSA_HPG, V_ROWS, tq), F32),
                        pltpu.VMEM((NSA_HPG, tk, tq), F32), pltpu.VMEM((NSA_HPG, tk, tq), BF16)],
        compiler_params=_cparams(("parallel", "parallel", "arbitrary")), name="nsa_selected_window",
    )(score_bound, qt, bias, kaug, vt, oc, gt.reshape(b, NSA_GROUPS, GATE_ROWS, s), mask_bias)


def _hgrn_consts():
    c = HGRN_CHUNK
    t = np.arange(c)
    lower = (t[None, :] <= t[:, None]).astype(np.float32)
    rows = [lower]
    masks = []
    for half in HGRN_LEVELS:
        mid = (t // (2 * half)) * (2 * half) + half - 1
        if half < 8:
            rows.append(lower[mid])
        same = (t[:, None] // (2 * half)) == (t[None, :] // (2 * half))
        right = (t[:, None] & half) != 0
        left = (t[None, :] & half) == 0
        masks.append((same & right & left).astype(np.float32))
    masks.append(np.eye(c, dtype=np.float32))
    masks.append(((t[:, None] // HGRN_SUB == t[None, :] // HGRN_SUB) & (t[None, :] <= t[:, None])).astype(np.float32))
    mall = np.concatenate(rows, axis=0)
    lvl = np.stack([np.tile(mk.T, (1, HGRN_HEADS)) for mk in masks])
    bdm = np.kron(np.eye(HGRN_HEADS), np.ones((64, 64), np.float32))
    return jnp.asarray(mall, BF16), jnp.asarray(lvl, F32), jnp.asarray(bdm, F32), jnp.asarray(bdm / 64, BF16)


def _hgrn_body(hg_ref, lb_ref, og_ref, mall_ref, lvl_ref, bdm_ref, bdn_ref, y_ref, st_scr, attn_scr):
    c = HGRN_CHUNK
    w = HGRN_WIDTH

    @pl.when(pl.program_id(1) == 0)
    def _():
        st_scr[...] = jnp.zeros(st_scr.shape, F32)

    chunks = range(hg_ref.shape[0] // c)
    lb = lb_ref[...]
    lane = lax.broadcasted_iota(jnp.int32, (c, w), 1)
    head_masks = [(lane >> 6) == hh for hh in range(HGRN_HEADS)]
    nlev = len(HGRN_LEVELS)

    def stack_heads(x):
        x16 = x.astype(BF16)
        return jnp.concatenate([jnp.where(hm, x16, 0) for hm in head_masks], axis=0)

    def row_bcast(x, half):
        return jnp.concatenate([jnp.broadcast_to(x[p + half - 1:p + half, :], (2 * half, w))
                                for p in range(0, c, 2 * half)], axis=0)

    qa, kk, v16, logf = [], [], [], []
    for ci in chunks:
        rows = slice(ci * c, (ci + 1) * c)
        qa.append(_silu(hg_ref[rows, 0:w]) * (HEAD_DIM ** -0.5))
        fg = lb + (1.0 - lb) * _sigmoid(hg_ref[rows, w:2 * w])
        kk.append(1.0 - fg)
        logf.append(jnp.log2(fg))
        v16.append(hg_ref[rows, 2 * w:3 * w].astype(BF16))

    mall = mall_ref[...]
    parts = _split3(jnp.concatenate(logf, axis=1))
    b_all = sum(_dot(mall[0:c], part) for part in parts)
    bcum = [b_all[:, ci * w:(ci + 1) * w] for ci in chunks]

    def level(ci, li, ref_pt):
        e = jnp.exp2(-jnp.abs(bcum[ci] - ref_pt))
        return lvl_ref[li] * _dot_nt((kk[ci] * e).astype(BF16), stack_heads(qa[ci] * e))

    coarse = [li for li, half in enumerate(HGRN_LEVELS) if half >= HGRN_SUB]
    if not coarse:
        attn_scr[...] = jnp.zeros(attn_scr.shape, F32)
    for li in coarse:
        for ci in chunks:
            contribution = level(ci, li, row_bcast(bcum[ci], HGRN_LEVELS[li]))
            attn_scr[ci] = contribution if li == coarse[0] else attn_scr[ci] + contribution

    def block_start(x):
        firsts = [jnp.zeros((HGRN_SUB, w), F32)]
        firsts += [jnp.broadcast_to(x[p - 1:p, :], (HGRN_SUB, w)) for p in range(HGRN_SUB, c, HGRN_SUB)]
        return jnp.concatenate(firsts, axis=0)

    expo = [block_start(bcum[ci]) - bcum[ci] for ci in chunks]
    largest = expo[0]
    for ci in chunks[1:]:
        largest = jnp.maximum(largest, expo[ci])
    single_reference_ok = jnp.max(largest) < HGRN_MAX_EXPONENT

    @pl.when(single_reference_ok)
    def _():
        for ci in chunks:
            kt = (kk[ci] * jnp.exp2(expo[ci])).astype(BF16)
            attn_scr[ci] = attn_scr[ci] + lvl_ref[nlev + 1] * _dot_nt(kt, stack_heads(qa[ci] * jnp.exp2(-expo[ci])))

    @pl.when(jnp.logical_not(single_reference_ok))
    def _():
        r_fine = sum(_dot(mall[c:], part) for part in parts)
        for ci in chunks:
            attn_scr[ci] = attn_scr[ci] + lvl_ref[nlev] * _dot_nt(kk[ci].astype(BF16), stack_heads(qa[ci]))
        fine = 0
        for li, half in enumerate(HGRN_LEVELS):
            if half >= HGRN_SUB:
                continue
            for ci in chunks:
                if half >= 8:
                    ref_pt = row_bcast(bcum[ci], half)
                else:
                    ref_pt = r_fine[fine * c:(fine + 1) * c, ci * w:(ci + 1) * w]
                attn_scr[ci] = attn_scr[ci] + level(ci, li, ref_pt)
            if half < 8:
                fine += 1

    attn = [attn_scr[ci] for ci in chunks]

    intra, upd, decay, qb = [], [], [], []
    for ci in chunks:
        x = _dot_tn(attn[ci].astype(BF16), v16[ci])
        intra.append(sum(jnp.where(head_masks[hh], x[hh * c:(hh + 1) * c], 0.0) for hh in range(HGRN_HEADS)))
        b_last = bcum[ci][c - 1:c, :]
        kl = (kk[ci] * jnp.exp2(b_last - bcum[ci])).astype(BF16)
        upd.append(bdm_ref[...] * _dot_tn(v16[ci], kl))
        decay.append(jnp.exp2(b_last))
        qb.append((qa[ci] * jnp.exp2(bcum[ci])).astype(BF16))

    st = st_scr[...]
    inter = []
    for ci in chunks:
        inter.append(_dot_nt(qb[ci], st.astype(BF16)))
        st = st * decay[ci] + upd[ci]
    st_scr[...] = st

    for ci in chunks:
        rows = slice(ci * c, (ci + 1) * c)
        o = inter[ci] + intra[ci]
        hi, lo = _split2(o * o)
        ms = _dot(hi, bdn_ref[...]) + _dot(lo, bdn_ref[...])
        y_ref[rows, :] = o * lax.rsqrt(ms + EPS) * og_ref[...] * _silu(hg_ref[rows, 3 * w:4 * w])


def _hgrn(hg, lower_bound, out_gain):
    b, s, _ = hg.shape
    rows = HGRN_CB * HGRN_CHUNK
    mall, lvl, bdm, bdn = _hgrn_consts()
    full = lambda shape: pl.BlockSpec(shape, lambda bi, i: (0,) * len(shape))
    return pl.pallas_call(
        _hgrn_body, grid=(b, s // rows),
        in_specs=[pl.BlockSpec((None, rows, 4 * HGRN_WIDTH), lambda bi, i: (bi, i, 0)),
                  full((1, HGRN_WIDTH)), full((1, HGRN_WIDTH)),
                  full(mall.shape), full(lvl.shape), full(bdm.shape), full(bdn.shape)],
        out_specs=pl.BlockSpec((None, rows, HGRN_WIDTH), lambda bi, i: (bi, i, 0)),
        out_shape=jax.ShapeDtypeStruct((b, s, HGRN_WIDTH), F32),
        scratch_shapes=[pltpu.VMEM((HGRN_WIDTH, HGRN_WIDTH), F32),
                        pltpu.VMEM((HGRN_CB, HGRN_CHUNK, HGRN_HEADS * HGRN_CHUNK), F32)],
        compiler_params=_cparams(("parallel", "arbitrary")), name="hgrn2_chunks",
    )(hg, lower_bound.reshape(1, -1), out_gain.reshape(1, -1), mall, lvl, bdm, bdn)


def _memkv_body(mem_ref, mg_ref, wk_ref, wvt_ref, kg_ref, kh_ref, vht_ref):
    m = mem_ref.shape[0]
    mn = _rms_rows(mem_ref[...], mg_ref[...]).astype(BF16)
    k = _dot(mn, wk_ref[...])
    vt = _dot_nt(wvt_ref[...], mn)
    ones_rows = (lax.broadcasted_iota(jnp.int32, (V_ROWS - 64, m), 0) == 0).astype(BF16)
    for hh in range(MEM_HEADS):
        kh_ref[hh] = _rms_rows(k[:, hh * 64:(hh + 1) * 64], kg_ref[...]).astype(BF16)
        vht_ref[hh, 0:64, :] = vt[hh * 64:(hh + 1) * 64].astype(BF16)
        vht_ref[hh, 64:V_ROWS, :] = ones_rows


def _mem_kv(mem, mem_gain, wk, wvt, k_gain):
    b, m, d = mem.shape
    full = lambda shape: pl.BlockSpec(shape, lambda bi: (0,) * len(shape))
    return pl.pallas_call(
        _memkv_body, grid=(b,),
        in_specs=[pl.BlockSpec((None, m, d), lambda bi: (bi, 0, 0)), full((1, d)),
                  full(wk.shape), full(wvt.shape), full((1, 64))],
        out_specs=(pl.BlockSpec((None, MEM_HEADS, m, 64), lambda bi: (bi, 0, 0, 0)),
                   pl.BlockSpec((None, MEM_HEADS, V_ROWS, m), lambda bi: (bi, 0, 0, 0))),
        out_shape=(jax.ShapeDtypeStruct((b, MEM_HEADS, m, 64), BF16),
                   jax.ShapeDtypeStruct((b, MEM_HEADS, V_ROWS, m), BF16)),
        compiler_params=_cparams(("parallel",)), name="memory_kv",
    )(mem, mem_gain.reshape(1, d), wk, wvt, k_gain.reshape(1, 64))


def _out_body(x_ref, ynt_ref, yh_ref, qmt_ref, kh_ref, vht_ref, ng_ref, mg_ref, wo_ref,
              fg_ref, wg_ref, wu_ref, wd_ref, o_ref, a_scr):
    scores = [_dot(kh_ref[hh], qmt_ref[hh * 64:(hh + 1) * 64, :]) for hh in range(MEM_HEADS)]
    nsa = _rms_cols(ynt_ref[...], ng_ref[...]).astype(BF16)
    acc = _dot_tn(nsa, wo_ref[0:NSA_WIDTH, :])
    acc = acc + _dot(yh_ref[...].astype(BF16), wo_ref[NSA_WIDTH:NSA_WIDTH + HGRN_WIDTH, :])
    y_mem = []
    for s in scores:
        hh = len(y_mem)
        p = jnp.exp2(s - jnp.max(s, axis=0, keepdims=True))
        o = _dot(vht_ref[hh], p.astype(BF16))
        y_mem.append(o[0:64] * (1.0 / o[64:65]))
    mem = _rms_cols(jnp.concatenate(y_mem, axis=0), mg_ref[...]).astype(BF16)
    acc = acc + _dot_tn(mem, wo_ref[NSA_WIDTH + HGRN_WIDTH:, :])
    o_ref[...] = _ffn_half_step(x_ref[...] + acc, fg_ref, wg_ref, wu_ref, wd_ref, a_scr)


def _out_ffn(x3d, ynt, yh, qmt, kh, vht, nsa_gain, mem_gain, wo, ffn_gain, wg, wu, wd, *, tm=TOK_TILE):
    b, s, d = x3d.shape
    m = kh.shape[2]
    return pl.pallas_call(
        _out_body, grid=(b, s // tm),
        in_specs=[pl.BlockSpec((None, tm, d), lambda bi, i: (bi, i, 0)),
                  pl.BlockSpec((None, NSA_WIDTH, tm), lambda bi, i: (bi, 0, i)),
                  pl.BlockSpec((None, tm, HGRN_WIDTH), lambda bi, i: (bi, i, 0)),
                  pl.BlockSpec((None, MEM_WIDTH, tm), lambda bi, i: (bi, 0, i)),
                  pl.BlockSpec((None, MEM_HEADS, m, 64), lambda bi, i: (bi, 0, 0, 0)),
                  pl.BlockSpec((None, MEM_HEADS, V_ROWS, m), lambda bi, i: (bi, 0, 0, 0)),
                  _resident((NSA_WIDTH, 1)), _resident((MEM_WIDTH, 1)), _resident(wo.shape),
                  _resident((1, d)), _resident(wg.shape), _resident(wu.shape), _resident(wd.shape)],
        out_specs=pl.BlockSpec((None, tm, d), lambda bi, i: (bi, i, 0)),
        out_shape=jax.ShapeDtypeStruct((b, s, d), F32),
        scratch_shapes=[pltpu.VMEM((tm, wg.shape[1]), BF16)],
        compiler_params=_cparams(("parallel", "parallel")), name="mix_out_ffn2",
    )(x3d, ynt, yh, qmt, kh, vht, nsa_gain.reshape(-1, 1), mem_gain.reshape(-1, 1), wo,
      ffn_gain.reshape(1, d), wg, wu, wd)


def _layer(x, mem, ffn1, ffn2, mix_norm, w_in, w_out, nsa_q_norm, nsa_k_norm, cmp_pos_k, cmp_w1_k, cmp_w2_k,
           cmp_pos_v, cmp_w1_v, cmp_w2_v, nsa_out_norm, lower_bound, hgrn_out_norm,
           mem_norm, mem_w_k, mem_w_v, mem_q_norm, mem_k_norm, mem_out_norm):
    b, s, d = x.shape
    sizes = (512, 128, 128, 128, 128, 128, 128, 24, 256, 256, 256, 256, 256)
    offs = np.concatenate([[0], np.cumsum(sizes)])
    w16 = w_in.astype(BF16)
    col = lambda i: w16[:, offs[i]:offs[i + 1]]
    (q_a, k_c, v_c, k_s, v_s, k_w, v_w, g_a, q_h, f_h, i_h, g_h, q_m) = [col(i) for i in range(13)]
    gpad = jnp.zeros((d, GATE_ROWS - 3 * NSA_HPG), BF16)
    wt = jnp.concatenate([q_a, v_s, v_w, g_a[:, :3 * NSA_HPG], gpad, g_a[:, 3 * NSA_HPG:], gpad, q_m], axis=1).T
    wn = jnp.concatenate([k_c, v_c, k_s, k_w, q_h, f_h, i_h, g_h], axis=1)

    x1, qt, vt, gt, qmt, kaug, kvc, hg = _ffn_proj(
        x, *ffn1, mix_norm, wt, wn, nsa_q_norm, nsa_k_norm, mem_q_norm, _rope_tables(s))

    cmp_pos, cmp_w1 = _compress_weights(cmp_pos_k, cmp_pos_v, cmp_w1_k, cmp_w1_v)
    kc, vct = _compress(kvc, cmp_pos, cmp_w1, cmp_w2_k.astype(BF16), cmp_w2_v.T.astype(BF16), nsa_k_norm)
    score_bound = _score_bound(nsa_q_norm, nsa_k_norm)
    oc, bias = _cmp_select(qt, kc, vct, score_bound)
    y_nsa = _slc_win(qt, bias, kaug, vt, oc, gt, score_bound)

    y_hgrn = _hgrn(hg, lower_bound, hgrn_out_norm)

    kh, vht = _mem_kv(mem, mem_norm, mem_w_k.astype(BF16), mem_w_v.T.astype(BF16), mem_k_norm)
    return _out_ffn(x1, y_nsa, y_hgrn, qmt, kh, vht, nsa_out_norm, mem_out_norm, w_out.astype(BF16), *ffn2)


def kernel(x, mem, ffn1_norm, ffn1_w_gate, ffn1_w_up, ffn1_w_down, mix_norm, w_in, w_out, nsa_q_norm, nsa_k_norm, cmp_pos_k, cmp_w1_k, cmp_w2_k, cmp_pos_v, cmp_w1_v, cmp_w2_v, nsa_out_norm, hgrn_lb_logits, hgrn_out_norm, mem_norm, mem_w_k, mem_w_v, mem_q_norm, mem_k_norm, mem_out_norm, ffn2_norm, ffn2_w_gate, ffn2_w_up, ffn2_w_down):
    b, s, d = x.shape
    depth = ffn1_norm.shape[0]
    lower_bounds = jnp.cumsum(jax.nn.softmax(hgrn_lb_logits.astype(F32), axis=0), axis=0)
    bf = lambda a: a.astype(BF16)
    for l in range(depth):
        x = _layer(x, mem, (ffn1_norm[l], bf(ffn1_w_gate[l]), bf(ffn1_w_up[l]), bf(ffn1_w_down[l])),
                   (ffn2_norm[l], bf(ffn2_w_gate[l]), bf(ffn2_w_up[l]), bf(ffn2_w_down[l])),
                   mix_norm[l], w_in[l], w_out[l], nsa_q_norm[l], nsa_k_norm[l],
                   cmp_pos_k[l], cmp_w1_k[l], cmp_w2_k[l], cmp_pos_v[l], cmp_w1_v[l], cmp_w2_v[l],
                   nsa_out_norm[l], lower_bounds[l], hgrn_out_norm[l],
                   mem_norm[l], mem_w_k[l], mem_w_v[l], mem_q_norm[l], mem_k_norm[l], mem_out_norm[l])
    return x
```

```python
import functools

import numpy as np
import jax
import jax.numpy as jnp
from jax import lax
from jax.experimental import pallas as pl
from jax.experimental.pallas import tpu as pltpu

F32 = jnp.float32
BF16 = jnp.bfloat16

HEAD_DIM = 64
ROT_DIM = 16
ROT_HALF = 8
ROPE_THETA = 500000.0
NSA_HEADS = 8
NSA_GROUPS = 2
NSA_HPG = 4
CMP_BLOCK = 32
CMP_STRIDE = 16
SLC_BLOCK = 64
SLC_SHIFT = 6
SLC_TOPK = 16
WINDOW = 512
FORCED_SCORE = 1e4
HGRN_HEADS = 4
HGRN_CHUNK = 64
HGRN_WIDTH = 256
MEM_HEADS = 4
MEM_WIDTH = 256
NSA_WIDTH = 512
EPS = 1e-6
NEG = -1e30
QK_SCALE_LOG2 = HEAD_DIM ** -0.5 * 1.4426950408889634
MIN_DENOMINATOR = 2.0 ** -64

VMEM_LIMIT = 56 * 1024 * 1024
MAX_BLOCKS = 128
GATE_ROWS = 16
V_ROWS = 80
TOK_TILE = 512
HGRN_CB = 16
HGRN_LEVELS = (32, 16, 8, 4, 2, 1)
HGRN_SUB = 64
HGRN_MAX_EXPONENT = 96.0

NT_DIMS = (((1,), (1,)), ((), ()))
TN_DIMS = (((0,), (0,)), ((), ()))


def _cparams(sem):
    return pltpu.CompilerParams(dimension_semantics=sem, vmem_limit_bytes=VMEM_LIMIT)


def _dot(a, b):
    return jnp.dot(a, b, preferred_element_type=F32)


def _dot_nt(a, b):
    return lax.dot_general(a, b, NT_DIMS, preferred_element_type=F32)


def _dot_tn(a, b):
    return lax.dot_general(a, b, TN_DIMS, preferred_element_type=F32)


def _sigmoid(x):
    return 1.0 / (1.0 + jnp.exp(-x))


def _silu(x):
    return x * _sigmoid(x)


def _split2(x):
    hi = x.astype(BF16)
    lo = (x - hi.astype(F32)).astype(BF16)
    return hi, lo


def _split3(x):
    hi = x.astype(BF16)
    r1 = x - hi.astype(F32)
    mid = r1.astype(BF16)
    lo = (r1 - mid.astype(F32)).astype(BF16)
    return hi, mid, lo


def _rms_rows(x, gain_row):
    ms = jnp.mean(x * x, axis=-1, keepdims=True)
    return x * lax.rsqrt(ms + EPS) * gain_row


def _rms_cols(x, gain_col):
    ms = jnp.mean(x * x, axis=0, keepdims=True)
    return x * lax.rsqrt(ms + EPS) * gain_col


def _seg_mean_sq(x, bd):
    hi, lo = _split2(x * x)
    return _dot(hi, bd) + _dot(lo, bd)


FFN_CHUNK = 256


def _ffn_half_step(x, g_ref, wg_ref, wu_ref, wd_ref, a_scr):
    xn = _rms_rows(x, g_ref[...]).astype(BF16)
    d_ff = wg_ref.shape[1]
    for c in range(d_ff // FFN_CHUNK):
        sl = slice(c * FFN_CHUNK, (c + 1) * FFN_CHUNK)
        g = _dot(xn, wg_ref[:, sl])
        u = _dot(xn, wu_ref[:, sl])
        a_scr[:, sl] = (_silu(g) * u).astype(BF16)
    return x + 0.5 * _dot(a_scr[...], wd_ref[...])


def _resident(shape):
    return pl.BlockSpec(shape, lambda *_: (0,) * len(shape), pipeline_mode=pl.Buffered(1))


def _rope_cols(xn, cos, sin):
    x0, x1 = xn[0:ROT_HALF], xn[ROT_HALF:ROT_DIM]
    return jnp.concatenate([x0 * cos - x1 * sin, x1 * cos + x0 * sin, xn[ROT_DIM:]], axis=0)


def _rope_rows(x, cn, sa, sb):
    return x * cn + pltpu.roll(x, 128 - ROT_HALF, 1) * sa + pltpu.roll(x, ROT_HALF, 1) * sb


def _proj_body(x_ref, fg_ref, wg_ref, wu_ref, wd_ref, mg_ref, wt_ref, wn_ref, qg_ref, kg_ref, mqg_ref,
               cos_ref, sin_ref, cn_ref, sa_ref, sb_ref, bd_ref,
               x1_ref, qt_ref, vt_ref, gt_ref, qmt_ref, kaug_ref, kvc_ref, hg_ref, a_scr):
    tm = x_ref.shape[0]
    assert tm == 8 * SLC_BLOCK
    x1 = _ffn_half_step(x_ref[...], fg_ref, wg_ref, wu_ref, wd_ref, a_scr)
    x1_ref[...] = x1
    h = _rms_rows(x1, mg_ref[...]).astype(BF16)

    qg, mqg = qg_ref[...], mqg_ref[...]
    half = tm // 2
    ones_rows = (lax.broadcasted_iota(jnp.int32, (V_ROWS - 64, half), 0) == 0).astype(BF16)
    for part in range(2):
        tok = slice(part * half, (part + 1) * half)
        pt = _dot_nt(wt_ref[...], h[tok, :])
        cos, sin = cos_ref[:, tok], sin_ref[:, tok]
        for hh in range(NSA_HEADS):
            xq = _rms_cols(pt[hh * 64:(hh + 1) * 64], qg)
            qt_ref[hh * 64:(hh + 1) * 64, tok] = (_rope_cols(xq, cos, sin) * QK_SCALE_LOG2).astype(BF16)
        for g in range(NSA_GROUPS):
            for br in range(2):
                rows = 512 + br * 128 + g * 64
                vt_ref[g, br, 0:64, tok] = pt[rows:rows + 64].astype(BF16)
                vt_ref[g, br, 64:V_ROWS, tok] = ones_rows
        gt_ref[:, tok] = _sigmoid(pt[768:800])
        for hh in range(MEM_HEADS):
            xm = _rms_cols(pt[800 + hh * 64:864 + hh * 64], mqg)
            qmt_ref[hh * 64:(hh + 1) * 64, tok] = (xm * QK_SCALE_LOG2).astype(BF16)

    cn, sa, sb = cn_ref[...], sa_ref[...], sb_ref[...]
    bd, kg = bd_ref[...], kg_ref[...]
    pc = _dot(h, wn_ref[:, 0:256])
    pk = _dot(h, wn_ref[:, 256:512])
    kvc_ref[0] = _rope_rows(pc[:, 0:128], cn, sa, sb)
    kvc_ref[1] = pc[:, 128:256]
    ph = _dot(h, wn_ref[:, 512:1024])
    ks = pk[:, 0:128]
    kw = pk[:, 128:256]
    ks = _rope_rows(ks * lax.rsqrt(_seg_mean_sq(ks, bd) + EPS) * kg, cn, sa, sb)
    kw = _rope_rows(kw * lax.rsqrt(_seg_mean_sq(kw, bd) + EPS) * kg, cn, sa, sb)
    lane = lax.broadcasted_iota(jnp.int32, (tm, 128), 1)
    row = lax.broadcasted_iota(jnp.int32, (tm, 128), 0)
    onehot = jnp.where(lane - 64 == (row >> SLC_SHIFT), 1.0, 0.0)
    lo_half = lane < 64
    kaug_ref[0, 0] = jnp.where(lo_half, ks, onehot).astype(BF16)
    kaug_ref[0, 1] = jnp.where(lo_half, kw, 0.0).astype(BF16)
    kaug_ref[1, 0] = jnp.where(lo_half, pltpu.roll(ks, 64, 1), onehot).astype(BF16)
    kaug_ref[1, 1] = jnp.where(lo_half, pltpu.roll(kw, 64, 1), 0.0).astype(BF16)
    hg_ref[:, 0:512] = ph
    hg_ref[:, 512:1024] = _dot(h, wn_ref[:, 1024:1536])


def _ffn_proj(x3d, ffn_gain, wg, wu, wd, mix_gain, wt, wn, q_gain, k_gain, mq_gain, rope):
    b, s, d = x3d.shape
    tm = TOK_TILE
    ns = s // tm
    cos_t, sin_t, cn, sa, sb = rope
    bd = jnp.asarray(np.kron(np.eye(2), np.full((64, 64), 1.0 / 64)), BF16)
    full = _resident
    out_shape = (
        jax.ShapeDtypeStruct((b, s, d), F32),
        jax.ShapeDtypeStruct((b, 512, s), BF16),
        jax.ShapeDtypeStruct((b, 2, 2, ns, V_ROWS, tm), BF16),
        jax.ShapeDtypeStruct((b, 32, s), F32),
        jax.ShapeDtypeStruct((b, 256, s), BF16),
        jax.ShapeDtypeStruct((b, 2, 2, s, 128), BF16),
        jax.ShapeDtypeStruct((b, 2, s, 128), F32),
        jax.ShapeDtypeStruct((b, s, 1024), F32),
    )
    out_specs = (
        pl.BlockSpec((None, tm, d), lambda bi, i: (bi, i, 0)),
        pl.BlockSpec((None, 512, tm), lambda bi, i: (bi, 0, i)),
        pl.BlockSpec((None, 2, 2, None, V_ROWS, tm), lambda bi, i: (bi, 0, 0, i, 0, 0)),
        pl.BlockSpec((None, 32, tm), lambda bi, i: (bi, 0, i)),
        pl.BlockSpec((None, 256, tm), lambda bi, i: (bi, 0, i)),
        pl.BlockSpec((None, 2, 2, tm, 128), lambda bi, i: (bi, 0, 0, i, 0)),
        pl.BlockSpec((None, 2, tm, 128), lambda bi, i: (bi, 0, i, 0)),
        pl.BlockSpec((None, tm, 1024), lambda bi, i: (bi, i, 0)),
    )
    in_specs = [
        pl.BlockSpec((None, tm, d), lambda bi, i: (bi, i, 0)),
        full((1, d)), full(wg.shape), full(wu.shape), full(wd.shape),
        full((1, d)), full(wt.shape), full(wn.shape),
        full((64, 1)), full((1, 128)), full((64, 1)),
        pl.BlockSpec((ROT_HALF, tm), lambda bi, i: (0, i)),
        pl.BlockSpec((ROT_HALF, tm), lambda bi, i: (0, i)),
        pl.BlockSpec((tm, 128), lambda bi, i: (i, 0)),
        pl.BlockSpec((tm, 128), lambda bi, i: (i, 0)),
        pl.BlockSpec((tm, 128), lambda bi, i: (i, 0)),
        full((128, 128)),
    ]
    return pl.pallas_call(
        _proj_body, grid=(b, ns), in_specs=in_specs, out_specs=out_specs, out_shape=out_shape,
        scratch_shapes=[pltpu.VMEM((tm, wg.shape[1]), BF16)],
        compiler_params=_cparams(("parallel", "parallel")), name="ffn1_mix_projection",
    )(x3d, ffn_gain.reshape(1, d), wg, wu, wd, mix_gain.reshape(1, d), wt, wn, q_gain.reshape(64, 1),
      jnp.tile(k_gain.reshape(1, 64), (1, 2)), mq_gain.reshape(64, 1), cos_t, sin_t, cn, sa, sb, bd)


def _rope_tables(s):
    pos = jnp.arange(s, dtype=F32)
    inv = ROPE_THETA ** (-(jnp.arange(0, ROT_DIM, 2, dtype=F32) / ROT_DIM))
    ang = pos[:, None] * inv[None, :]
    cos, sin = jnp.cos(ang), jnp.sin(ang)
    zeros = jnp.zeros((s, 64 - ROT_DIM), F32)
    cn = jnp.concatenate([cos, cos, jnp.ones((s, 64 - ROT_DIM), F32)], axis=1)
    sa = jnp.concatenate([-sin, jnp.zeros((s, ROT_HALF), F32), zeros], axis=1)
    sb = jnp.concatenate([jnp.zeros((s, ROT_HALF), F32), sin, zeros], axis=1)
    tile2 = lambda a: jnp.concatenate([a, a], axis=1)
    return cos.T, sin.T, tile2(cn), tile2(sa), tile2(sb)


def _cmp_body(kvc_ref, pos_ref, w1_ref, w2k_ref, w2vt_ref, kg_ref, kc_ref, vct_ref):
    nc = kvc_ref.shape[1] // CMP_STRIDE
    for kind in range(2):
        halves = []
        for part in range(2):
            x = jnp.concatenate(
                [(kvc_ref[kind, pl.ds(r, nc, stride=CMP_STRIDE), :]
                  + pos_ref[kind, part, :, r * 128:(r + 1) * 128]).astype(BF16) for r in range(CMP_STRIDE)],
                axis=1)
            halves.append(x)
        for g in range(NSA_GROUPS):
            second = _dot(halves[1], w1_ref[kind, g, 1])
            hid = _silu(_dot(halves[0], w1_ref[kind, g, 0]) + pltpu.roll(second, nc - 1, 0)).astype(BF16)
            if kind == 0:
                kc_ref[g] = _rms_rows(_dot(hid, w2k_ref[...]), kg_ref[...]).astype(BF16)
            else:
                vct_ref[g] = _dot_nt(w2vt_ref[...], hid).astype(BF16)


def _compress(kvc, pos, w1, w2k, w2vt, k_gain):
    b, _, s, _ = kvc.shape
    nc = s // CMP_STRIDE
    return pl.pallas_call(
        _cmp_body, grid=(b,),
        in_specs=[pl.BlockSpec((None, 2, s, 128), lambda bi: (bi, 0, 0, 0)),
                  _resident(pos.shape), _resident(w1.shape), _resident(w2k.shape), _resident(w2vt.shape),
                  _resident((1, 64))],
        out_specs=(pl.BlockSpec((None, 2, nc, 64), lambda bi: (bi, 0, 0, 0)),
                   pl.BlockSpec((None, 2, 64, nc), lambda bi: (bi, 0, 0, 0))),
        out_shape=(jax.ShapeDtypeStruct((b, 2, nc, 64), BF16), jax.ShapeDtypeStruct((b, 2, 64, nc), BF16)),
        compiler_params=_cparams(("parallel",)), name="nsa_compress",
    )(kvc, pos, w1, w2k, w2vt, k_gain.reshape(1, 64))


def _compress_weights(pos_k, pos_v, w1_k, w1_v):
    def pos_part(p):
        p = p.reshape(2, CMP_STRIDE, 1, 64)
        return jnp.broadcast_to(p, (2, CMP_STRIDE, NSA_GROUPS, 64)).reshape(2, 1, CMP_STRIDE * 128)

    def w1_part(w):
        hdim = w.shape[1]
        w = w.reshape(2, CMP_STRIDE, 1, 64, hdim)
        per_group = []
        for g in range(NSA_GROUPS):
            pads = [w if gg == g else jnp.zeros_like(w) for gg in range(NSA_GROUPS)]
            per_group.append(jnp.concatenate(pads, axis=2).reshape(2, CMP_STRIDE * 128, hdim))
        return jnp.stack(per_group)

    pos = jnp.stack([pos_part(pos_k), pos_part(pos_v)])
    w1 = jnp.stack([w1_part(w1_k), w1_part(w1_v)]).astype(BF16)
    return pos, w1


CMP_CLASS_ROWS = 128


def _cmpsel_variant(nc, nblk, fixed_reference, m0_ref, qt_ref, kc_ref, vct_ref, oc_ref, bias_ref, s_scr, flag_scr):
    tq = qt_ref.shape[1]
    t0 = pl.program_id(2) * tq
    n_idx = lax.broadcasted_iota(jnp.int32, (nc, tq), 0)
    t_idx = t0 + lax.broadcasted_iota(jnp.int32, (nc, tq), 1)
    mask_bias = jnp.where(n_idx * CMP_STRIDE + (CMP_BLOCK - 1) <= t_idx, 0.0, NEG)
    sees_any = t0 + lax.broadcasted_iota(jnp.int32, (1, tq), 1) >= CMP_BLOCK - 1
    kc = kc_ref[0:nc, :]
    if fixed_reference:
        mask_bias = mask_bias - m0_ref[0]
    else:
        for hh in range(NSA_HPG):
            s_scr[hh, 0:nc, :] = _dot(kc, qt_ref[hh * 64:(hh + 1) * 64, :]) + mask_bias
    jj = lax.broadcasted_iota(jnp.int32, (nblk, nc), 0)
    nn = lax.broadcasted_iota(jnp.int32, (nblk, nc), 1)
    ov = jnp.where((nn * CMP_STRIDE < jj * SLC_BLOCK + SLC_BLOCK)
                   & (nn * CMP_STRIDE + CMP_BLOCK > jj * SLC_BLOCK), 1.0, 0.0).astype(BF16)
    ones_rows = (lax.broadcasted_iota(jnp.int32, (V_ROWS - 64, nc), 0) == 0).astype(BF16)
    lhs = jnp.concatenate([vct_ref[:, 0:nc], ones_rows, ov], axis=0)
    imp = jnp.zeros((nblk, tq), F32)
    l_min = jnp.full((1, tq), 1.0, F32)

    def probabilities(hh):
        if fixed_reference:
            return jnp.exp2(_dot(kc, qt_ref[hh * 64:(hh + 1) * 64, :]) + mask_bias).astype(BF16)
        m = jnp.max(s_scr[hh, 0:nc, :], axis=0, keepdims=True)
        return jnp.exp2(s_scr[hh, 0:nc, :] - m).astype(BF16)

    p_next = probabilities(0)
    for hh in range(NSA_HPG):
        p = p_next
        if hh + 1 < NSA_HPG:
            p_next = probabilities(hh + 1)
        r = _dot(lhs, p)
        l_min = jnp.minimum(l_min, jnp.where(sees_any, r[64:65], 1.0))
        inv_l = jnp.where(sees_any, 1.0 / r[64:65], 0.0)
        oc_ref[hh * 64:(hh + 1) * 64, :] = r[0:64] * inv_l
        imp = imp + r[V_ROWS:] * inv_l
    if fixed_reference:
        flag_scr[0] = jnp.where(jnp.min(l_min) > MIN_DENOMINATOR, 0, 1)

    j = lax.broadcasted_iota(jnp.int32, (nblk, tq), 0)
    cur = (t0 + lax.broadcasted_iota(jnp.int32, (nblk, tq), 1)) >> SLC_SHIFT
    forced = (j == 0) | (j == cur) | (j == cur - 1)
    picks = SLC_TOPK - 3
    forced_bias = jnp.where(forced & (j <= cur), 0.0, NEG)
    imp = jnp.where((j <= cur) & jnp.logical_not(forced), imp, -1.0)
    if nblk < bias_ref.shape[0]:
        bias_ref[nblk:, :] = jnp.full((bias_ref.shape[0] - nblk, tq), NEG, F32)

    rest = imp
    for _ in range(picks):
        v = jnp.max(rest, axis=0, keepdims=True)
        rest = jnp.where(rest == v, -3e38, rest)
    chosen = (imp >= v) & (imp >= 0.0)
    count = jnp.sum(jnp.where(chosen, 1.0, 0.0), axis=0, keepdims=True)
    valid = jnp.sum(jnp.where(imp >= 0.0, 1.0, 0.0), axis=0, keepdims=True)
    bias_ref[0:nblk, :] = jnp.where(chosen, 0.0, forced_bias)
    has_tie = jnp.max(jnp.abs(count - jnp.minimum(valid, float(picks)))) > 0.0

    @pl.when(has_tie)
    def _():
        jf = j.astype(F32)
        bias, rest = forced_bias, imp
        for _ in range(picks):
            v = jnp.max(rest, axis=0, keepdims=True)
            first = jnp.min(jnp.where(rest == v, jf, float(nblk)), axis=0, keepdims=True)
            pick = jf == first
            bias = jnp.where(pick & (v >= 0.0), 0.0, bias)
            rest = jnp.where(pick, -3e38, rest)
        bias_ref[0:nblk, :] = bias


def _cmpsel_body(m0_ref, qt_ref, kc_ref, vct_ref, oc_ref, bias_ref, s_scr, flag_scr):
    tq = qt_ref.shape[1]
    nc_total = kc_ref.shape[0]
    tiles_per_class = CMP_CLASS_ROWS // (tq // CMP_STRIDE)
    cls = pl.program_id(2) // tiles_per_class
    refs = (m0_ref, qt_ref, kc_ref, vct_ref, oc_ref, bias_ref, s_scr, flag_scr)
    for c in range(nc_total // CMP_CLASS_ROWS):
        nc = (c + 1) * CMP_CLASS_ROWS
        nblk = min(nc * CMP_STRIDE // SLC_BLOCK, bias_ref.shape[0])
        pl.when(cls == c)(functools.partial(_cmpsel_variant, nc, nblk, True, *refs))
    pl.when(flag_scr[0] != 0)(functools.partial(_cmpsel_variant, nc_total, bias_ref.shape[0], False, *refs))


def _cmp_select(qt, kc, vct, score_bound, *, tq=TOK_TILE):
    b, _, s = qt.shape
    nc = kc.shape[2]
    nblk = MAX_BLOCKS
    assert s // SLC_BLOCK <= MAX_BLOCKS and s // SLC_BLOCK >= SLC_TOPK and nc % CMP_CLASS_ROWS == 0
    return pl.pallas_call(
        _cmpsel_body, grid=(b, NSA_GROUPS, s // tq),
        in_specs=[pl.BlockSpec(memory_space=pltpu.SMEM),
                  pl.BlockSpec((None, 256, tq), lambda bi, g, i: (bi, g, i)),
                  pl.BlockSpec((None, None, nc, 64), lambda bi, g, i: (bi, g, 0, 0)),
                  pl.BlockSpec((None, None, 64, nc), lambda bi, g, i: (bi, g, 0, 0))],
        out_specs=(pl.BlockSpec((None, 256, tq), lambda bi, g, i: (bi, g, i)),
                   pl.BlockSpec((None, None, nblk, tq), lambda bi, g, i: (bi, g, 0, i))),
        out_shape=(jax.ShapeDtypeStruct((b, 512, s), F32),
                   jax.ShapeDtypeStruct((b, NSA_GROUPS, nblk, s), F32)),
        scratch_shapes=[pltpu.VMEM((NSA_HPG, nc, tq), F32), pltpu.SMEM((1,), jnp.int32)],
        compiler_params=_cparams(("parallel", "parallel", "parallel")), name="nsa_compressed_select",
    )(score_bound, qt, kc, vct)


def _flash_step(s_ref, vt, m_ref, acc_ref):
    m_old = m_ref[...]
    m_new = jnp.maximum(m_old, jnp.max(s_ref[...], axis=0, keepdims=True))
    p = jnp.exp2(s_ref[...] - m_new)
    acc_ref[...] = jnp.exp2(m_old - m_new) * acc_ref[...] + _dot(vt, p.astype(BF16))
    m_ref[...] = m_new


SEL, WIN = 0, 1
BIAS_ROWS = 16


def _slcwin_body(m0_ref, qt_ref, bias_ref, kaug_ref, vt_ref, oc_ref, gt_ref, mb_ref, y_ref,
                 q_scr, m_scr, acc_scr, s_scr, p_scr):
    tq = qt_ref.shape[1]
    tk = vt_ref.shape[3]
    assert tq == tk and WINDOW == tk and tk == 8 * SLC_BLOCK
    diag = pl.program_id(2)
    m0 = m0_ref[0]

    sel_slots = (0, 1, 2, 3)
    qs, qw = sel_slots[0], 4
    zeros = jnp.zeros((64, tq), BF16)
    for slot in sel_slots + (qw,):
        for hh in range(NSA_HPG):
            q_scr[slot, hh, 0:64, :] = qt_ref[hh * 64:(hh + 1) * 64, :]
            q_scr[slot, hh, 64:128, :] = zeros

    def set_selection_bias(kt, slot=qs):
        rows = bias_ref[pl.ds(pl.multiple_of(kt * 8, 8), 8), :]
        b16 = jnp.concatenate([rows, jnp.zeros_like(rows)], axis=0).astype(BF16)
        for hh in range(NSA_HPG):
            q_scr[slot, hh, 64:64 + BIAS_ROWS, :] = b16

    def tiles_fixed_reference(tiles):
        chains = [(br, qslot, kt, mask_bias, hh) for br, qslot, kt, mask_bias in tiles for hh in range(NSA_HPG)]
        for c in range(len(chains) + 1):
            if c < len(chains):
                br, qslot, kt, mask_bias, hh = chains[c]
                s = _dot(kaug_ref[br, pl.ds(pl.multiple_of(kt * tk, tk), tk), :], q_scr[qslot, hh])
                if mask_bias is not None:
                    s = s + mask_bias()
                p_scr[c % 4] = jnp.exp2((s - m0).astype(BF16))
            if c >= 1:
                br, _, kt, _, hh = chains[c - 1]
                acc_scr[br, hh] = acc_scr[br, hh] + _dot(vt_ref[br, kt], p_scr[(c - 1) % 4])

    def tile_running_max(br, qslot, kt, mask_bias=None):
        k = kaug_ref[br, pl.ds(pl.multiple_of(kt * tk, tk), tk), :]
        for hh in range(NSA_HPG):
            s = _dot(k, q_scr[qslot, hh])
            s_scr[hh] = s if mask_bias is None else s + mask_bias()
        for hh in range(NSA_HPG):
            _flash_step(s_scr.at[hh], vt_ref[br, kt], m_scr.at[br, hh], acc_scr.at[br, hh])

    prev = jnp.maximum(diag - 1, 0)
    no_prev = jnp.where(diag == 0, NEG, 0.0)
    band_bias = lambda: mb_ref[1] + no_prev
    causal_bias = lambda: mb_ref[0]

    acc_scr[...] = jnp.zeros(acc_scr.shape, F32)

    def unmasked_run(first_tile, count):
        for n in range(count):
            set_selection_bias(first_tile + n, sel_slots[n])
        tiles_fixed_reference([(SEL, sel_slots[n], first_tile + n, None) for n in range(count)])

    def tile_quad(j, carry):
        unmasked_run(4 * j, 4)
        return carry

    lax.fori_loop(0, diag >> 2, tile_quad, 0)
    pl.when((diag & 2) != 0)(lambda: unmasked_run((diag >> 2) * 4, 2))
    pl.when((diag & 1) != 0)(lambda: unmasked_run(diag - 1, 1))

    set_selection_bias(diag, qs)
    tiles_fixed_reference([(WIN, qw, prev, band_bias), (SEL, qs, diag, causal_bias), (WIN, qw, diag, causal_bias)])

    denominators = acc_scr[:, :, 64:65, :]
    underflow = jnp.logical_not(jnp.min(denominators) > MIN_DENOMINATOR)

    @pl.when(underflow)
    def _():
        m_scr[...] = jnp.full(m_scr.shape, NEG, F32)
        acc_scr[...] = jnp.zeros(acc_scr.shape, F32)

        def full_tile(kt, carry):
            set_selection_bias(kt, qs)
            tile_running_max(SEL, qs, kt)
            return carry

        lax.fori_loop(0, diag, full_tile, 0)
        set_selection_bias(diag, qs)
        tile_running_max(WIN, qw, prev, band_bias)
        tile_running_max(SEL, qs, diag, causal_bias)
        tile_running_max(WIN, qw, diag, causal_bias)

    gt = gt_ref[...]
    for hh in range(NSA_HPG):
        o_s = acc_scr[SEL, hh, 0:64, :] * (1.0 / acc_scr[SEL, hh, 64:65, :])
        o_w = acc_scr[WIN, hh, 0:64, :] * (1.0 / acc_scr[WIN, hh, 64:65, :])
        y_ref[hh * 64:(hh + 1) * 64, :] = (gt[3 * hh:3 * hh + 1] * oc_ref[hh * 64:(hh + 1) * 64, :]
                                          + gt[3 * hh + 1:3 * hh + 2] * o_s
                                          + gt[3 * hh + 2:3 * hh + 3] * o_w)


def _score_bound(q_gain, k_gain):
    bound = HEAD_DIM * QK_SCALE_LOG2 * jnp.max(jnp.abs(q_gain)) * jnp.max(jnp.abs(k_gain))
    return (1.02 * bound).reshape(1).astype(F32)


def _slc_win(qt, bias, kaug, vt, oc, gt, score_bound):
    b, _, s = qt.shape
    nblk = bias.shape[2]
    ns, tk = vt.shape[3], vt.shape[5]
    tq = tk
    key_rel, t_rel = np.arange(tk)[:, None], np.arange(tq)[None, :]
    mask_bias = jnp.asarray(np.stack([np.where(key_rel <= t_rel, 0.0, NEG),
                                      np.where(t_rel + tk - key_rel < WINDOW, 0.0, NEG)]), F32)
    qblk = pl.BlockSpec((None, 256, tq), lambda bi, g, i: (bi, g, i))
    return pl.pallas_call(
        _slcwin_body, grid=(b, NSA_GROUPS, s // tq),
        in_specs=[pl.BlockSpec(memory_space=pltpu.SMEM), qblk,
                  pl.BlockSpec((None, None, nblk, tq), lambda bi, g, i: (bi, g, 0, i)),
                  pl.BlockSpec((None, None, 2, s, 128), lambda bi, g, i: (bi, g, 0, 0, 0)),
                  pl.BlockSpec((None, None, 2, ns, V_ROWS, tk), lambda bi, g, i: (bi, g, 0, 0, 0, 0)),
                  qblk,
                  pl.BlockSpec((None, None, GATE_ROWS, tq), lambda bi, g, i: (bi, g, 0, i)),
                  _resident((2, tk, tq))],
        out_specs=qblk,
        out_shape=jax.ShapeDtypeStruct((b, 512, s), F32),
        scratch_shapes=[pltpu.VMEM((5, NSA_HPG, 128, tq), BF16),
                        pltpu.VMEM((2, NSA_HPG, 1, tq), F32), pltpu.VMEM((2, NSA_HPG, V_ROWS, tq), F32),
                        pltpu.VMEM((NSA_HPG, tk, tq), F32), pltpu.VMEM((NSA_HPG, tk, tq), BF16)],
        compiler_params=_cparams(("parallel", "parallel", "arbitrary")), name="nsa_selected_window",
    )(score_bound, qt, bias, kaug, vt, oc, gt.reshape(b, NSA_GROUPS, GATE_ROWS, s), mask_bias)


def _hgrn_consts():
    c = HGRN_CHUNK
    t = np.arange(c)
    lower = (t[None, :] <= t[:, None]).astype(np.float32)
    rows = [lower]
    masks = []
    for half in HGRN_LEVELS:
        mid = (t // (2 * half)) * (2 * half) + half - 1
        if half < 8:
            rows.append(lower[mid])
        same = (t[:, None] // (2 * half)) == (t[None, :] // (2 * half))
        right = (t[:, None] & half) != 0
        left = (t[None, :] & half) == 0
        masks.append((same & right & left).astype(np.float32))
    masks.append(np.eye(c, dtype=np.float32))
    masks.append(((t[:, None] // HGRN_SUB == t[None, :] // HGRN_SUB) & (t[None, :] <= t[:, None])).astype(np.float32))
    mall = np.concatenate(rows, axis=0)
    lvl = np.stack([np.tile(mk.T, (1, HGRN_HEADS)) for mk in masks])
    bdm = np.kron(np.eye(HGRN_HEADS), np.ones((64, 64), np.float32))
    return jnp.asarray(mall, BF16), jnp.asarray(lvl, F32), jnp.asarray(bdm, F32), jnp.asarray(bdm / 64, BF16)


def _hgrn_body(hg_ref, lb_ref, og_ref, mall_ref, lvl_ref, bdm_ref, bdn_ref, y_ref, st_scr, attn_scr):
    c = HGRN_CHUNK
    w = HGRN_WIDTH

    @pl.when(pl.program_id(1) == 0)
    def _():
        st_scr[...] = jnp.zeros(st_scr.shape, F32)

    chunks = range(hg_ref.shape[0] // c)
    lb = lb_ref[...]
    lane = lax.broadcasted_iota(jnp.int32, (c, w), 1)
    head_masks = [(lane >> 6) == hh for hh in range(HGRN_HEADS)]
    nlev = len(HGRN_LEVELS)

    def stack_heads(x):
        x16 = x.astype(BF16)
        return jnp.concatenate([jnp.where(hm, x16, 0) for hm in head_masks], axis=0)

    def row_bcast(x, half):
        return jnp.concatenate([jnp.broadcast_to(x[p + half - 1:p + half, :], (2 * half, w))
                                for p in range(0, c, 2 * half)], axis=0)

    qa, kk, v16, logf = [], [], [], []
    for ci in chunks:
        rows = slice(ci * c, (ci + 1) * c)
        qa.append(_silu(hg_ref[rows, 0:w]) * (HEAD_DIM ** -0.5))
        fg = lb + (1.0 - lb) * _sigmoid(hg_ref[rows, w:2 * w])
        kk.append(1.0 - fg)
        logf.append(jnp.log2(fg))
        v16.append(hg_ref[rows, 2 * w:3 * w].astype(BF16))

    mall = mall_ref[...]
    parts = _split3(jnp.concatenate(logf, axis=1))
    b_all = sum(_dot(mall[0:c], part) for part in parts)
    bcum = [b_all[:, ci * w:(ci + 1) * w] for ci in chunks]

    def level(ci, li, ref_pt):
        e = jnp.exp2(-jnp.abs(bcum[ci] - ref_pt))
        return lvl_ref[li] * _dot_nt((kk[ci] * e).astype(BF16), stack_heads(qa[ci] * e))

    coarse = [li for li, half in enumerate(HGRN_LEVELS) if half >= HGRN_SUB]
    if not coarse:
        attn_scr[...] = jnp.zeros(attn_scr.shape, F32)
    for li in coarse:
        for ci in chunks:
            contribution = level(ci, li, row_bcast(bcum[ci], HGRN_LEVELS[li]))
            attn_scr[ci] = contribution if li == coarse[0] else attn_scr[ci] + contribution

    def block_start(x):
        firsts = [jnp.zeros((HGRN_SUB, w), F32)]
        firsts += [jnp.broadcast_to(x[p - 1:p, :], (HGRN_SUB, w)) for p in range(HGRN_SUB, c, HGRN_SUB)]
        return jnp.concatenate(firsts, axis=0)

    expo = [block_start(bcum[ci]) - bcum[ci] for ci in chunks]
    largest = expo[0]
    for ci in chunks[1:]:
        largest = jnp.maximum(largest, expo[ci])
    single_reference_ok = jnp.max(largest) < HGRN_MAX_EXPONENT

    @pl.when(single_reference_ok)
    def _():
        for ci in chunks:
            kt = (kk[ci] * jnp.exp2(expo[ci])).astype(BF16)
            attn_scr[ci] = attn_scr[ci] + lvl_ref[nlev + 1] * _dot_nt(kt, stack_heads(qa[ci] * jnp.exp2(-expo[ci])))

    @pl.when(jnp.logical_not(single_reference_ok))
    def _():
        r_fine = sum(_dot(mall[c:], part) for part in parts)
        for ci in chunks:
            attn_scr[ci] = attn_scr[ci] + lvl_ref[nlev] * _dot_nt(kk[ci].astype(BF16), stack_heads(qa[ci]))
        fine = 0
        for li, half in enumerate(HGRN_LEVELS):
            if half >= HGRN_SUB:
                continue
            for ci in chunks:
                if half >= 8:
                    ref_pt = row_bcast(bcum[ci], half)
                else:
                    ref_pt = r_fine[fine * c:(fine + 1) * c, ci * w:(ci + 1) * w]
                attn_scr[ci] = attn_scr[ci] + level(ci, li, ref_pt)
            if half < 8:
                fine += 1

    attn = [attn_scr[ci] for ci in chunks]

    intra, upd, decay, qb = [], [], [], []
    for ci in chunks:
        x = _dot_tn(attn[ci].astype(BF16), v16[ci])
        intra.append(sum(jnp.where(head_masks[hh], x[hh * c:(hh + 1) * c], 0.0) for hh in range(HGRN_HEADS)))
        b_last = bcum[ci][c - 1:c, :]
        kl = (kk[ci] * jnp.exp2(b_last - bcum[ci])).astype(BF16)
        upd.append(bdm_ref[...] * _dot_tn(v16[ci], kl))
        decay.append(jnp.exp2(b_last))
        qb.append((qa[ci] * jnp.exp2(bcum[ci])).astype(BF16))

    st = st_scr[...]
    inter = []
    for ci in chunks:
        inter.append(_dot_nt(qb[ci], st.astype(BF16)))
        st = st * decay[ci] + upd[ci]
    st_scr[...] = st

    for ci in chunks:
        rows = slice(ci * c, (ci + 1) * c)
        o = inter[ci] + intra[ci]
        hi, lo = _split2(o * o)
        ms = _dot(hi, bdn_ref[...]) + _dot(lo, bdn_ref[...])
        y_ref[rows, :] = o * lax.rsqrt(ms + EPS) * og_ref[...] * _silu(hg_ref[rows, 3 * w:4 * w])


def _hgrn(hg, lower_bound, out_gain):
    b, s, _ = hg.shape
    rows = HGRN_CB * HGRN_CHUNK
    mall, lvl, bdm, bdn = _hgrn_consts()
    full = lambda shape: pl.BlockSpec(shape, lambda bi, i: (0,) * len(shape))
    return pl.pallas_call(
        _hgrn_body, grid=(b, s // rows),
        in_specs=[pl.BlockSpec((None, rows, 4 * HGRN_WIDTH), lambda bi, i: (bi, i, 0)),
                  full((1, HGRN_WIDTH)), full((1, HGRN_WIDTH)),
                  full(mall.shape), full(lvl.shape), full(bdm.shape), full(bdn.shape)],
        out_specs=pl.BlockSpec((None, rows, HGRN_WIDTH), lambda bi, i: (bi, i, 0)),
        out_shape=jax.ShapeDtypeStruct((b, s, HGRN_WIDTH), F32),
        scratch_shapes=[pltpu.VMEM((HGRN_WIDTH, HGRN_WIDTH), F32),
                        pltpu.VMEM((HGRN_CB, HGRN_CHUNK, HGRN_HEADS * HGRN_CHUNK), F32)],
        compiler_params=_cparams(("parallel", "arbitrary")), name="hgrn2_chunks",
    )(hg, lower_bound.reshape(1, -1), out_gain.reshape(1, -1), mall, lvl, bdm, bdn)


def _memkv_body(mem_ref, mg_ref, wk_ref, wvt_ref, kg_ref, kh_ref, vht_ref):
    m = mem_ref.shape[0]
    mn = _rms_rows(mem_ref[...], mg_ref[...]).astype(BF16)
    k = _dot(mn, wk_ref[...])
    vt = _dot_nt(wvt_ref[...], mn)
    ones_rows = (lax.broadcasted_iota(jnp.int32, (V_ROWS - 64, m), 0) == 0).astype(BF16)
    for hh in range(MEM_HEADS):
        kh_ref[hh] = _rms_rows(k[:, hh * 64:(hh + 1) * 64], kg_ref[...]).astype(BF16)
        vht_ref[hh, 0:64, :] = vt[hh * 64:(hh + 1) * 64].astype(BF16)
        vht_ref[hh, 64:V_ROWS, :] = ones_rows


def _mem_kv(mem, mem_gain, wk, wvt, k_gain):
    b, m, d = mem.shape
    full = lambda shape: pl.BlockSpec(shape, lambda bi: (0,) * len(shape))
    return pl.pallas_call(
        _memkv_body, grid=(b,),
        in_specs=[pl.BlockSpec((None, m, d), lambda bi: (bi, 0, 0)), full((1, d)),
                  full(wk.shape), full(wvt.shape), full((1, 64))],
        out_specs=(pl.BlockSpec((None, MEM_HEADS, m, 64), lambda bi: (bi, 0, 0, 0)),
                   pl.BlockSpec((None, MEM_HEADS, V_ROWS, m), lambda bi: (bi, 0, 0, 0))),
        out_shape=(jax.ShapeDtypeStruct((b, MEM_HEADS, m, 64), BF16),
                   jax.ShapeDtypeStruct((b, MEM_HEADS, V_ROWS, m), BF16)),
        compiler_params=_cparams(("parallel",)), name="memory_kv",
    )(mem, mem_gain.reshape(1, d), wk, wvt, k_gain.reshape(1, 64))


def _out_body(x_ref, ynt_ref, yh_ref, qmt_ref, kh_ref, vht_ref, ng_ref, mg_ref, wo_ref,
              fg_ref, wg_ref, wu_ref, wd_ref, o_ref, a_scr):
    scores = [_dot(kh_ref[hh], qmt_ref[hh * 64:(hh + 1) * 64, :]) for hh in range(MEM_HEADS)]
    nsa = _rms_cols(ynt_ref[...], ng_ref[...]).astype(BF16)
    acc = _dot_tn(nsa, wo_ref[0:NSA_WIDTH, :])
    acc = acc + _dot(yh_ref[...].astype(BF16), wo_ref[NSA_WIDTH:NSA_WIDTH + HGRN_WIDTH, :])
    y_mem = []
    for s in scores:
        hh = len(y_mem)
        p = jnp.exp2(s - jnp.max(s, axis=0, keepdims=True))
        o = _dot(vht_ref[hh], p.astype(BF16))
        y_mem.append(o[0:64] * (1.0 / o[64:65]))
    mem = _rms_cols(jnp.concatenate(y_mem, axis=0), mg_ref[...]).astype(BF16)
    acc = acc + _dot_tn(mem, wo_ref[NSA_WIDTH + HGRN_WIDTH:, :])
    o_ref[...] = _ffn_half_step(x_ref[...] + acc, fg_ref, wg_ref, wu_ref, wd_ref, a_scr)


def _out_ffn(x3d, ynt, yh, qmt, kh, vht, nsa_gain, mem_gain, wo, ffn_gain, wg, wu, wd, *, tm=TOK_TILE):
    b, s, d = x3d.shape
    m = kh.shape[2]
    return pl.pallas_call(
        _out_body, grid=(b, s // tm),
        in_specs=[pl.BlockSpec((None, tm, d), lambda bi, i: (bi, i, 0)),
                  pl.BlockSpec((None, NSA_WIDTH, tm), lambda bi, i: (bi, 0, i)),
                  pl.BlockSpec((None, tm, HGRN_WIDTH), lambda bi, i: (bi, i, 0)),
                  pl.BlockSpec((None, MEM_WIDTH, tm), lambda bi, i: (bi, 0, i)),
                  pl.BlockSpec((None, MEM_HEADS, m, 64), lambda bi, i: (bi, 0, 0, 0)),
                  pl.BlockSpec((None, MEM_HEADS, V_ROWS, m), lambda bi, i: (bi, 0, 0, 0)),
                  _resident((NSA_WIDTH, 1)), _resident((MEM_WIDTH, 1)), _resident(wo.shape),
                  _resident((1, d)), _resident(wg.shape), _resident(wu.shape), _resident(wd.shape)],
        out_specs=pl.BlockSpec((None, tm, d), lambda bi, i: (bi, i, 0)),
        out_shape=jax.ShapeDtypeStruct((b, s, d), F32),
        scratch_shapes=[pltpu.VMEM((tm, wg.shape[1]), BF16)],
        compiler_params=_cparams(("parallel", "parallel")), name="mix_out_ffn2",
    )(x3d, ynt, yh, qmt, kh, vht, nsa_gain.reshape(-1, 1), mem_gain.reshape(-1, 1), wo,
      ffn_gain.reshape(1, d), wg, wu, wd)


def _layer(x, mem, ffn1, ffn2, mix_norm, w_in, w_out, nsa_q_norm, nsa_k_norm, cmp_pos_k, cmp_w1_k, cmp_w2_k,
           cmp_pos_v, cmp_w1_v, cmp_w2_v, nsa_out_norm, lower_bound, hgrn_out_norm,
           mem_norm, mem_w_k, mem_w_v, mem_q_norm, mem_k_norm, mem_out_norm):
    b, s, d = x.shape
    sizes = (512, 128, 128, 128, 128, 128, 128, 24, 256, 256, 256, 256, 256)
    offs = np.concatenate([[0], np.cumsum(sizes)])
    w16 = w_in.astype(BF16)
    col = lambda i: w16[:, offs[i]:offs[i + 1]]
    (q_a, k_c, v_c, k_s, v_s, k_w, v_w, g_a, q_h, f_h, i_h, g_h, q_m) = [col(i) for i in range(13)]
    gpad = jnp.zeros((d, GATE_ROWS - 3 * NSA_HPG), BF16)
    wt = jnp.concatenate([q_a, v_s, v_w, g_a[:, :3 * NSA_HPG], gpad, g_a[:, 3 * NSA_HPG:], gpad, q_m], axis=1).T
    wn = jnp.concatenate([k_c, v_c, k_s, k_w, q_h, f_h, i_h, g_h], axis=1)

    x1, qt, vt, gt, qmt, kaug, kvc, hg = _ffn_proj(
        x, *ffn1, mix_norm, wt, wn, nsa_q_norm, nsa_k_norm, mem_q_norm, _rope_tables(s))

    cmp_pos, cmp_w1 = _compress_weights(cmp_pos_k, cmp_pos_v, cmp_w1_k, cmp_w1_v)
    kc, vct = _compress(kvc, cmp_pos, cmp_w1, cmp_w2_k.astype(BF16), cmp_w2_v.T.astype(BF16), nsa_k_norm)
    score_bound = _score_bound(nsa_q_norm, nsa_k_norm)
    oc, bias = _cmp_select(qt, kc, vct, score_bound)
    y_nsa = _slc_win(qt, bias, kaug, vt, oc, gt, score_bound)

    y_hgrn = _hgrn(hg, lower_bound, hgrn_out_norm)

    kh, vht = _mem_kv(mem, mem_norm, mem_w_k.astype(BF16), mem_w_v.T.astype(BF16), mem_k_norm)
    return _out_ffn(x1, y_nsa, y_hgrn, qmt, kh, vht, nsa_out_norm, mem_out_norm, w_out.astype(BF16), *ffn2)


def kernel(x, mem, ffn1_norm, ffn1_w_gate, ffn1_w_up, ffn1_w_down, mix_norm, w_in, w_out, nsa_q_norm, nsa_k_norm, cmp_pos_k, cmp_w1_k, cmp_w2_k, cmp_pos_v, cmp_w1_v, cmp_w2_v, nsa_out_norm, hgrn_lb_logits, hgrn_out_norm, mem_norm, mem_w_k, mem_w_v, mem_q_norm, mem_k_norm, mem_out_norm, ffn2_norm, ffn2_w_gate, ffn2_w_up, ffn2_w_down):
    b, s, d = x.shape
    depth = ffn1_norm.shape[0]
    lower_bounds = jnp.cumsum(jax.nn.softmax(hgrn_lb_logits.astype(F32), axis=0), axis=0)
    bf = lambda a: a.astype(BF16)
    for l in range(depth):
        x = _layer(x, mem, (ffn1_norm[l], bf(ffn1_w_gate[l]), bf(ffn1_w_up[l]), bf(ffn1_w_down[l])),
                   (ffn2_norm[l], bf(ffn2_w_gate[l]), bf(ffn2_w_up[l]), bf(ffn2_w_down[l])),
                   mix_norm[l], w_in[l], w_out[l], nsa_q_norm[l], nsa_k_norm[l],
                   cmp_pos_k[l], cmp_w1_k[l], cmp_w2_k[l], cmp_pos_v[l], cmp_w1_v[l], cmp_w2_v[l],
                   nsa_out_norm[l], lower_bounds[l], hgrn_out_norm[l],
                   mem_norm[l], mem_w_k[l], mem_w_v[l], mem_q_norm[l], mem_k_norm[l], mem_out_norm[l])
    return x
```

```python
import functools

import numpy as np
import jax
import jax.numpy as jnp
from jax import lax
from jax.experimental import pallas as pl
from jax.experimental.pallas import tpu as pltpu

F32 = jnp.float32
BF16 = jnp.bfloat16

HEAD_DIM = 64
ROT_DIM = 16
ROT_HALF = 8
ROPE_THETA = 500000.0
NSA_HEADS = 8
NSA_GROUPS = 2
NSA_HPG = 4
CMP_BLOCK = 32
CMP_STRIDE = 16
SLC_BLOCK = 64
SLC_SHIFT = 6
SLC_TOPK = 16
WINDOW = 512
FORCED_SCORE = 1e4
HGRN_HEADS = 4
HGRN_CHUNK = 64
HGRN_WIDTH = 256
MEM_HEADS = 4
MEM_WIDTH = 256
NSA_WIDTH = 512
EPS = 1e-6
NEG = -1e30
QK_SCALE_LOG2 = HEAD_DIM ** -0.5 * 1.4426950408889634
MIN_DENOMINATOR = 2.0 ** -64

VMEM_LIMIT = 56 * 1024 * 1024
MAX_BLOCKS = 128
GATE_ROWS = 16
V_ROWS = 80
TOK_TILE = 512
HGRN_CB = 16
HGRN_LEVELS = (32, 16, 8, 4, 2, 1)
HGRN_SUB = 64
HGRN_MAX_EXPONENT = 96.0

NT_DIMS = (((1,), (1,)), ((), ()))
TN_DIMS = (((0,), (0,)), ((), ()))


def _cparams(sem):
    return pltpu.CompilerParams(dimension_semantics=sem, vmem_limit_bytes=VMEM_LIMIT)


def _dot(a, b):
    return jnp.dot(a, b, preferred_element_type=F32)


def _dot_nt(a, b):
    return lax.dot_general(a, b, NT_DIMS, preferred_element_type=F32)


def _dot_tn(a, b):
    return lax.dot_general(a, b, TN_DIMS, preferred_element_type=F32)


def _sigmoid(x):
    return 1.0 / (1.0 + jnp.exp(-x))


def _silu(x):
    return x * _sigmoid(x)


def _split2(x):
    hi = x.astype(BF16)
    lo = (x - hi.astype(F32)).astype(BF16)
    return hi, lo


def _split3(x):
    hi = x.astype(BF16)
    r1 = x - hi.astype(F32)
    mid = r1.astype(BF16)
    lo = (r1 - mid.astype(F32)).astype(BF16)
    return hi, mid, lo


def _rms_rows(x, gain_row):
    ms = jnp.mean(x * x, axis=-1, keepdims=True)
    return x * lax.rsqrt(ms + EPS) * gain_row


def _rms_cols(x, gain_col):
    ms = jnp.mean(x * x, axis=0, keepdims=True)
    return x * lax.rsqrt(ms + EPS) * gain_col


def _seg_mean_sq(x, bd):
    hi, lo = _split2(x * x)
    return _dot(hi, bd) + _dot(lo, bd)


FFN_CHUNK = 256


def _ffn_half_step(x, g_ref, wg_ref, wu_ref, wd_ref, a_scr):
    xn = _rms_rows(x, g_ref[...]).astype(BF16)
    d_ff = wg_ref.shape[1]
    for c in range(d_ff // FFN_CHUNK):
        sl = slice(c * FFN_CHUNK, (c + 1) * FFN_CHUNK)
        g = _dot(xn, wg_ref[:, sl])
        u = _dot(xn, wu_ref[:, sl])
        a_scr[:, sl] = (_silu(g) * u).astype(BF16)
    return x + 0.5 * _dot(a_scr[...], wd_ref[...])


def _resident(shape):
    return pl.BlockSpec(shape, lambda *_: (0,) * len(shape), pipeline_mode=pl.Buffered(1))


def _rope_cols(xn, cos, sin):
    x0, x1 = xn[0:ROT_HALF], xn[ROT_HALF:ROT_DIM]
    return jnp.concatenate([x0 * cos - x1 * sin, x1 * cos + x0 * sin, xn[ROT_DIM:]], axis=0)


def _rope_rows(x, cn, sa, sb):
    return x * cn + pltpu.roll(x, 128 - ROT_HALF, 1) * sa + pltpu.roll(x, ROT_HALF, 1) * sb


def _proj_body(x_ref, fg_ref, wg_ref, wu_ref, wd_ref, mg_ref, wt_ref, wn_ref, qg_ref, kg_ref, mqg_ref,
               cos_ref, sin_ref, cn_ref, sa_ref, sb_ref, bd_ref,
               x1_ref, qt_ref, vt_ref, gt_ref, qmt_ref, kaug_ref, kvc_ref, hg_ref, a_scr):
    tm = x_ref.shape[0]
    assert tm == 8 * SLC_BLOCK
    x1 = _ffn_half_step(x_ref[...], fg_ref, wg_ref, wu_ref, wd_ref, a_scr)
    x1_ref[...] = x1
    h = _rms_rows(x1, mg_ref[...]).astype(BF16)

    qg, mqg = qg_ref[...], mqg_ref[...]
    half = tm // 2
    ones_rows = (lax.broadcasted_iota(jnp.int32, (V_ROWS - 64, half), 0) == 0).astype(BF16)
    for part in range(2):
        tok = slice(part * half, (part + 1) * half)
        pt = _dot_nt(wt_ref[...], h[tok, :])
        cos, sin = cos_ref[:, tok], sin_ref[:, tok]
        for hh in range(NSA_HEADS):
            xq = _rms_cols(pt[hh * 64:(hh + 1) * 64], qg)
            qt_ref[hh * 64:(hh + 1) * 64, tok] = (_rope_cols(xq, cos, sin) * QK_SCALE_LOG2).astype(BF16)
        for g in range(NSA_GROUPS):
            for br in range(2):
                rows = 512 + br * 128 + g * 64
                vt_ref[g, br, 0:64, tok] = pt[rows:rows + 64].astype(BF16)
                vt_ref[g, br, 64:V_ROWS, tok] = ones_rows
        gt_ref[:, tok] = _sigmoid(pt[768:800])
        for hh in range(MEM_HEADS):
            xm = _rms_cols(pt[800 + hh * 64:864 + hh * 64], mqg)
            qmt_ref[hh * 64:(hh + 1) * 64, tok] = (xm * QK_SCALE_LOG2).astype(BF16)

    cn, sa, sb = cn_ref[...], sa_ref[...], sb_ref[...]
    bd, kg = bd_ref[...], kg_ref[...]
    pc = _dot(h, wn_ref[:, 0:256])
    pk = _dot(h, wn_ref[:, 256:512])
    kvc_ref[0] = _rope_rows(pc[:, 0:128], cn, sa, sb)
    kvc_ref[1] = pc[:, 128:256]
    ph = _dot(h, wn_ref[:, 512:1024])
    ks = pk[:, 0:128]
    kw = pk[:, 128:256]
    ks = _rope_rows(ks * lax.rsqrt(_seg_mean_sq(ks, bd) + EPS) * kg, cn, sa, sb)
    kw = _rope_rows(kw * lax.rsqrt(_seg_mean_sq(kw, bd) + EPS) * kg, cn, sa, sb)
    lane = lax.broadcasted_iota(jnp.int32, (tm, 128), 1)
    row = lax.broadcasted_iota(jnp.int32, (tm, 128), 0)
    onehot = jnp.where(lane - 64 == (row >> SLC_SHIFT), 1.0, 0.0)
    lo_half = lane < 64
    kaug_ref[0, 0] = jnp.where(lo_half, ks, onehot).astype(BF16)
    kaug_ref[0, 1] = jnp.where(lo_half, kw, 0.0).astype(BF16)
    kaug_ref[1, 0] = jnp.where(lo_half, pltpu.roll(ks, 64, 1), onehot).astype(BF16)
    kaug_ref[1, 1] = jnp.where(lo_half, pltpu.roll(kw, 64, 1), 0.0).astype(BF16)
    hg_ref[:, 0:512] = ph
    hg_ref[:, 512:1024] = _dot(h, wn_ref[:, 1024:1536])


def _ffn_proj(x3d, ffn_gain, wg, wu, wd, mix_gain, wt, wn, q_gain, k_gain, mq_gain, rope):
    b, s, d = x3d.shape
    tm = TOK_TILE
    ns = s // tm
    cos_t, sin_t, cn, sa, sb = rope
    bd = jnp.asarray(np.kron(np.eye(2), np.full((64, 64), 1.0 / 64)), BF16)
    full = _resident
    out_shape = (
        jax.ShapeDtypeStruct((b, s, d), F32),
        jax.ShapeDtypeStruct((b, 512, s), BF16),
        jax.ShapeDtypeStruct((b, 2, 2, ns, V_ROWS, tm), BF16),
        jax.ShapeDtypeStruct((b, 32, s), F32),
        jax.ShapeDtypeStruct((b, 256, s), BF16),
        jax.ShapeDtypeStruct((b, 2, 2, s, 128), BF16),
        jax.ShapeDtypeStruct((b, 2, s, 128), F32),
        jax.ShapeDtypeStruct((b, s, 1024), F32),
    )
    out_specs = (
        pl.BlockSpec((None, tm, d), lambda bi, i: (bi, i, 0)),
        pl.BlockSpec((None, 512, tm), lambda bi, i: (bi, 0, i)),
        pl.BlockSpec((None, 2, 2, None, V_ROWS, tm), lambda bi, i: (bi, 0, 0, i, 0, 0)),
        pl.BlockSpec((None, 32, tm), lambda bi, i: (bi, 0, i)),
        pl.BlockSpec((None, 256, tm), lambda bi, i: (bi, 0, i)),
        pl.BlockSpec((None, 2, 2, tm, 128), lambda bi, i: (bi, 0, 0, i, 0)),
        pl.BlockSpec((None, 2, tm, 128), lambda bi, i: (bi, 0, i, 0)),
        pl.BlockSpec((None, tm, 1024), lambda bi, i: (bi, i, 0)),
    )
    in_specs = [
        pl.BlockSpec((None, tm, d), lambda bi, i: (bi, i, 0)),
        full((1, d)), full(wg.shape), full(wu.shape), full(wd.shape),
        full((1, d)), full(wt.shape), full(wn.shape),
        full((64, 1)), full((1, 128)), full((64, 1)),
        pl.BlockSpec((ROT_HALF, tm), lambda bi, i: (0, i)),
        pl.BlockSpec((ROT_HALF, tm), lambda bi, i: (0, i)),
        pl.BlockSpec((tm, 128), lambda bi, i: (i, 0)),
        pl.BlockSpec((tm, 128), lambda bi, i: (i, 0)),
        pl.BlockSpec((tm, 128), lambda bi, i: (i, 0)),
        full((128, 128)),
    ]
    return pl.pallas_call(
        _proj_body, grid=(b, ns), in_specs=in_specs, out_specs=out_specs, out_shape=out_shape,
        scratch_shapes=[pltpu.VMEM((tm, wg.shape[1]), BF16)],
        compiler_params=_cparams(("parallel", "parallel")), name="ffn1_mix_projection",
    )(x3d, ffn_gain.reshape(1, d), wg, wu, wd, mix_gain.reshape(1, d), wt, wn, q_gain.reshape(64, 1),
      jnp.tile(k_gain.reshape(1, 64), (1, 2)), mq_gain.reshape(64, 1), cos_t, sin_t, cn, sa, sb, bd)


def _rope_tables(s):
    pos = jnp.arange(s, dtype=F32)
    inv = ROPE_THETA ** (-(jnp.arange(0, ROT_DIM, 2, dtype=F32) / ROT_DIM))
    ang = pos[:, None] * inv[None, :]
    cos, sin = jnp.cos(ang), jnp.sin(ang)
    zeros = jnp.zeros((s, 64 - ROT_DIM), F32)
    cn = jnp.concatenate([cos, cos, jnp.ones((s, 64 - ROT_DIM), F32)], axis=1)
    sa = jnp.concatenate([-sin, jnp.zeros((s, ROT_HALF), F32), zeros], axis=1)
    sb = jnp.concatenate([jnp.zeros((s, ROT_HALF), F32), sin, zeros], axis=1)
    tile2 = lambda a: jnp.concatenate([a, a], axis=1)
    return cos.T, sin.T, tile2(cn), tile2(sa), tile2(sb)


def _cmp_body(kvc_ref, pos_ref, w1_ref, w2k_ref, w2vt_ref, kg_ref, kc_ref, vct_ref):
    nc = kvc_ref.shape[1] // CMP_STRIDE
    for kind in range(2):
        halves = []
        for part in range(2):
            x = jnp.concatenate(
                [(kvc_ref[kind, pl.ds(r, nc, stride=CMP_STRIDE), :]
                  + pos_ref[kind, part, :, r * 128:(r + 1) * 128]).astype(BF16) for r in range(CMP_STRIDE)],
                axis=1)
            halves.append(x)
        for g in range(NSA_GROUPS):
            second = _dot(halves[1], w1_ref[kind, g, 1])
            hid = _silu(_dot(halves[0], w1_ref[kind, g, 0]) + pltpu.roll(second, nc - 1, 0)).astype(BF16)
            if kind == 0:
                kc_ref[g] = _rms_rows(_dot(hid, w2k_ref[...]), kg_ref[...]).astype(BF16)
            else:
                vct_ref[g] = _dot_nt(w2vt_ref[...], hid).astype(BF16)


def _compress(kvc, pos, w1, w2k, w2vt, k_gain):
    b, _, s, _ = kvc.shape
    nc = s // CMP_STRIDE
    return pl.pallas_call(
        _cmp_body, grid=(b,),
        in_specs=[pl.BlockSpec((None, 2, s, 128), lambda bi: (bi, 0, 0, 0)),
                  _resident(pos.shape), _resident(w1.shape), _resident(w2k.shape), _resident(w2vt.shape),
                  _resident((1, 64))],
        out_specs=(pl.BlockSpec((None, 2, nc, 64), lambda bi: (bi, 0, 0, 0)),
                   pl.BlockSpec((None, 2, 64, nc), lambda bi: (bi, 0, 0, 0))),
        out_shape=(jax.ShapeDtypeStruct((b, 2, nc, 64), BF16), jax.ShapeDtypeStruct((b, 2, 64, nc), BF16)),
        compiler_params=_cparams(("parallel",)), name="nsa_compress",
    )(kvc, pos, w1, w2k, w2vt, k_gain.reshape(1, 64))


def _compress_weights(pos_k, pos_v, w1_k, w1_v):
    def pos_part(p):
        p = p.reshape(2, CMP_STRIDE, 1, 64)
        return jnp.broadcast_to(p, (2, CMP_STRIDE, NSA_GROUPS, 64)).reshape(2, 1, CMP_STRIDE * 128)

    def w1_part(w):
        hdim = w.shape[1]
        w = w.astype(BF16).reshape(1, 2, CMP_STRIDE, 1, 64, hdim)
        own_group = jnp.eye(NSA_GROUPS, dtype=BF16).reshape(NSA_GROUPS, 1, 1, NSA_GROUPS, 1, 1)
        return (w * own_group).reshape(NSA_GROUPS, 2, CMP_STRIDE * 128, hdim)

    pos = jnp.stack([pos_part(pos_k), pos_part(pos_v)])
    w1 = jnp.stack([w1_part(w1_k), w1_part(w1_v)])
    return pos, w1


CMP_CLASS_ROWS = 128


def _cmpsel_variant(nc, nblk, fixed_reference, m0_ref, qt_ref, kc_ref, vct_ref, oc_ref, bias_ref, s_scr, flag_scr):
    tq = qt_ref.shape[1]
    t0 = pl.program_id(2) * tq
    n_idx = lax.broadcasted_iota(jnp.int32, (nc, tq), 0)
    t_idx = t0 + lax.broadcasted_iota(jnp.int32, (nc, tq), 1)
    mask_bias = jnp.where(n_idx * CMP_STRIDE + (CMP_BLOCK - 1) <= t_idx, 0.0, NEG)
    sees_any = t0 + lax.broadcasted_iota(jnp.int32, (1, tq), 1) >= CMP_BLOCK - 1
    kc = kc_ref[0:nc, :]
    if fixed_reference:
        mask_bias = mask_bias - m0_ref[0]
    else:
        for hh in range(NSA_HPG):
            s_scr[hh, 0:nc, :] = _dot(kc, qt_ref[hh * 64:(hh + 1) * 64, :]) + mask_bias
    jj = lax.broadcasted_iota(jnp.int32, (nblk, nc), 0)
    nn = lax.broadcasted_iota(jnp.int32, (nblk, nc), 1)
    ov = jnp.where((nn * CMP_STRIDE < jj * SLC_BLOCK + SLC_BLOCK)
                   & (nn * CMP_STRIDE + CMP_BLOCK > jj * SLC_BLOCK), 1.0, 0.0).astype(BF16)
    ones_rows = (lax.broadcasted_iota(jnp.int32, (V_ROWS - 64, nc), 0) == 0).astype(BF16)
    lhs = jnp.concatenate([vct_ref[:, 0:nc], ones_rows, ov], axis=0)
    imp = jnp.zeros((nblk, tq), F32)
    l_min = jnp.full((1, tq), 1.0, F32)

    def probabilities(hh):
        if fixed_reference:
            return jnp.exp2(_dot(kc, qt_ref[hh * 64:(hh + 1) * 64, :]) + mask_bias).astype(BF16)
        m = jnp.max(s_scr[hh, 0:nc, :], axis=0, keepdims=True)
        return jnp.exp2(s_scr[hh, 0:nc, :] - m).astype(BF16)

    p_next = probabilities(0)
    for hh in range(NSA_HPG):
        p = p_next
        if hh + 1 < NSA_HPG:
            p_next = probabilities(hh + 1)
        r = _dot(lhs, p)
        l_min = jnp.minimum(l_min, jnp.where(sees_any, r[64:65], 1.0))
        inv_l = jnp.where(sees_any, 1.0 / r[64:65], 0.0)
        oc_ref[hh * 64:(hh + 1) * 64, :] = r[0:64] * inv_l
        imp = imp + r[V_ROWS:] * inv_l
    if fixed_reference:
        flag_scr[0] = jnp.where(jnp.min(l_min) > MIN_DENOMINATOR, 0, 1)

    j = lax.broadcasted_iota(jnp.int32, (nblk, tq), 0)
    cur = (t0 + lax.broadcasted_iota(jnp.int32, (nblk, tq), 1)) >> SLC_SHIFT
    forced = (j == 0) | (j == cur) | (j == cur - 1)
    picks = SLC_TOPK - 3
    forced_bias = jnp.where(forced & (j <= cur), 0.0, NEG)
    imp = jnp.where((j <= cur) & jnp.logical_not(forced), imp, -1.0)
    if nblk < bias_ref.shape[0]:
        bias_ref[nblk:, :] = jnp.full((bias_ref.shape[0] - nblk, tq), NEG, F32)

    rest = imp
    for _ in range(picks):
        v = jnp.max(rest, axis=0, keepdims=True)
        rest = jnp.where(rest == v, -3e38, rest)
    chosen = (imp >= v) & (imp >= 0.0)
    count = jnp.sum(jnp.where(chosen, 1.0, 0.0), axis=0, keepdims=True)
    valid = jnp.sum(jnp.where(imp >= 0.0, 1.0, 0.0), axis=0, keepdims=True)
    bias_ref[0:nblk, :] = jnp.where(chosen, 0.0, forced_bias)
    has_tie = jnp.max(jnp.abs(count - jnp.minimum(valid, float(picks)))) > 0.0

    @pl.when(has_tie)
    def _():
        jf = j.astype(F32)
        bias, rest = forced_bias, imp
        for _ in range(picks):
            v = jnp.max(rest, axis=0, keepdims=True)
            first = jnp.min(jnp.where(rest == v, jf, float(nblk)), axis=0, keepdims=True)
            pick = jf == first
            bias = jnp.where(pick & (v >= 0.0), 0.0, bias)
            rest = jnp.where(pick, -3e38, rest)
        bias_ref[0:nblk, :] = bias


def _cmpsel_body(m0_ref, qt_ref, kc_ref, vct_ref, oc_ref, bias_ref, s_scr, flag_scr):
    tq = qt_ref.shape[1]
    nc_total = kc_ref.shape[0]
    tiles_per_class = CMP_CLASS_ROWS // (tq // CMP_STRIDE)
    cls = pl.program_id(2) // tiles_per_class
    refs = (m0_ref, qt_ref, kc_ref, vct_ref, oc_ref, bias_ref, s_scr, flag_scr)
    for c in range(nc_total // CMP_CLASS_ROWS):
        nc = (c + 1) * CMP_CLASS_ROWS
        nblk = min(nc * CMP_STRIDE // SLC_BLOCK, bias_ref.shape[0])
        pl.when(cls == c)(functools.partial(_cmpsel_variant, nc, nblk, True, *refs))
    pl.when(flag_scr[0] != 0)(functools.partial(_cmpsel_variant, nc_total, bias_ref.shape[0], False, *refs))


def _cmp_select(qt, kc, vct, score_bound, *, tq=TOK_TILE):
    b, _, s = qt.shape
    nc = kc.shape[2]
    nblk = MAX_BLOCKS
    assert s // SLC_BLOCK <= MAX_BLOCKS and s // SLC_BLOCK >= SLC_TOPK and nc % CMP_CLASS_ROWS == 0
    return pl.pallas_call(
        _cmpsel_body, grid=(b, NSA_GROUPS, s // tq),
        in_specs=[pl.BlockSpec(memory_space=pltpu.SMEM),
                  pl.BlockSpec((None, 256, tq), lambda bi, g, i: (bi, g, i)),
                  pl.BlockSpec((None, None, nc, 64), lambda bi, g, i: (bi, g, 0, 0)),
                  pl.BlockSpec((None, None, 64, nc), lambda bi, g, i: (bi, g, 0, 0))],
        out_specs=(pl.BlockSpec((None, 256, tq), lambda bi, g, i: (bi, g, i)),
                   pl.BlockSpec((None, None, nblk, tq), lambda bi, g, i: (bi, g, 0, i))),
        out_shape=(jax.ShapeDtypeStruct((b, 512, s), F32),
                   jax.ShapeDtypeStruct((b, NSA_GROUPS, nblk, s), F32)),
        scratch_shapes=[pltpu.VMEM((NSA_HPG, nc, tq), F32), pltpu.SMEM((1,), jnp.int32)],
        compiler_params=_cparams(("parallel", "parallel", "parallel")), name="nsa_compressed_select",
    )(score_bound, qt, kc, vct)


def _flash_step(s_ref, vt, m_ref, acc_ref):
    m_old = m_ref[...]
    m_new = jnp.maximum(m_old, jnp.max(s_ref[...], axis=0, keepdims=True))
    p = jnp.exp2(s_ref[...] - m_new)
    acc_ref[...] = jnp.exp2(m_old - m_new) * acc_ref[...] + _dot(vt, p.astype(BF16))
    m_ref[...] = m_new


SEL, WIN = 0, 1
BIAS_ROWS = 16


def _slcwin_body(m0_ref, qt_ref, bias_ref, kaug_ref, vt_ref, oc_ref, gt_ref, mb_ref, y_ref,
                 q_scr, m_scr, acc_scr, s_scr, p_scr):
    tq = qt_ref.shape[1]
    tk = vt_ref.shape[3]
    assert tq == tk and WINDOW == tk and tk == 8 * SLC_BLOCK
    diag = pl.program_id(2)
    m0 = m0_ref[0]

    sel_slots = (0, 1, 2, 3)
    qs, qw = sel_slots[0], 4
    zeros = jnp.zeros((64, tq), BF16)
    for slot in sel_slots + (qw,):
        for hh in range(NSA_HPG):
            q_scr[slot, hh, 0:64, :] = qt_ref[hh * 64:(hh + 1) * 64, :]
            q_scr[slot, hh, 64:128, :] = zeros

    def set_selection_bias(kt, slot=qs):
        rows = bias_ref[pl.ds(pl.multiple_of(kt * 8, 8), 8), :]
        b16 = jnp.concatenate([rows, jnp.zeros_like(rows)], axis=0).astype(BF16)
        for hh in range(NSA_HPG):
            q_scr[slot, hh, 64:64 + BIAS_ROWS, :] = b16

    def tiles_fixed_reference(tiles):
        chains = [(br, qslot, kt, mask_bias, hh) for br, qslot, kt, mask_bias in tiles for hh in range(NSA_HPG)]
        for c in range(len(chains) + 1):
            if c < len(chains):
                br, qslot, kt, mask_bias, hh = chains[c]
                s = _dot(kaug_ref[br, pl.ds(pl.multiple_of(kt * tk, tk), tk), :], q_scr[qslot, hh])
                if mask_bias is not None:
                    s = s + mask_bias()
                p_scr[c % 4] = jnp.exp2(s - m0).astype(BF16)
            if c >= 1:
                br, _, kt, _, hh = chains[c - 1]
                acc_scr[br, hh] = acc_scr[br, hh] + _dot(vt_ref[br, kt], p_scr[(c - 1) % 4])

    def tile_running_max(br, qslot, kt, mask_bias=None):
        k = kaug_ref[br, pl.ds(pl.multiple_of(kt * tk, tk), tk), :]
        for hh in range(NSA_HPG):
            s = _dot(k, q_scr[qslot, hh])
            s_scr[hh] = s if mask_bias is None else s + mask_bias()
        for hh in range(NSA_HPG):
            _flash_step(s_scr.at[hh], vt_ref[br, kt], m_scr.at[br, hh], acc_scr.at[br, hh])

    prev = jnp.maximum(diag - 1, 0)
    no_prev = jnp.where(diag == 0, NEG, 0.0)
    band_bias = lambda: mb_ref[1] + no_prev
    causal_bias = lambda: mb_ref[0]

    acc_scr[...] = jnp.zeros(acc_scr.shape, F32)

    def unmasked_run(first_tile, count):
        for n in range(count):
            set_selection_bias(first_tile + n, sel_slots[n])
        tiles_fixed_reference([(SEL, sel_slots[n], first_tile + n, None) for n in range(count)])

    def tile_quad(j, carry):
        unmasked_run(4 * j, 4)
        return carry

    lax.fori_loop(0, diag >> 2, tile_quad, 0)
    pl.when((diag & 2) != 0)(lambda: unmasked_run((diag >> 2) * 4, 2))
    pl.when((diag & 1) != 0)(lambda: unmasked_run(diag - 1, 1))

    set_selection_bias(diag, qs)
    tiles_fixed_reference([(WIN, qw, prev, band_bias), (SEL, qs, diag, causal_bias), (WIN, qw, diag, causal_bias)])

    denominators = acc_scr[:, :, 64:65, :]
    underflow = jnp.logical_not(jnp.min(denominators) > MIN_DENOMINATOR)

    @pl.when(underflow)
    def _():
        m_scr[...] = jnp.full(m_scr.shape, NEG, F32)
        acc_scr[...] = jnp.zeros(acc_scr.shape, F32)

        def full_tile(kt, carry):
            set_selection_bias(kt, qs)
            tile_running_max(SEL, qs, kt)
            return carry

        lax.fori_loop(0, diag, full_tile, 0)
        set_selection_bias(diag, qs)
        tile_running_max(WIN, qw, prev, band_bias)
        tile_running_max(SEL, qs, diag, causal_bias)
        tile_running_max(WIN, qw, diag, causal_bias)

    gt = gt_ref[...]
    for hh in range(NSA_HPG):
        o_s = acc_scr[SEL, hh, 0:64, :] * (1.0 / acc_scr[SEL, hh, 64:65, :])
        o_w = acc_scr[WIN, hh, 0:64, :] * (1.0 / acc_scr[WIN, hh, 64:65, :])
        y_ref[hh * 64:(hh + 1) * 64, :] = (gt[3 * hh:3 * hh + 1] * oc_ref[hh * 64:(hh + 1) * 64, :]
                                          + gt[3 * hh + 1:3 * hh + 2] * o_s
                                          + gt[3 * hh + 2:3 * hh + 3] * o_w)


def _score_bound(q_gain, k_gain):
    bound = HEAD_DIM * QK_SCALE_LOG2 * jnp.max(jnp.abs(q_gain)) * jnp.max(jnp.abs(k_gain))
    return (1.02 * bound).reshape(1).astype(F32)


def _slc_win(qt, bias, kaug, vt, oc, gt, score_bound):
    b, _, s = qt.shape
    nblk = bias.shape[2]
    ns, tk = vt.shape[3], vt.shape[5]
    tq = tk
    key_rel, t_rel = np.arange(tk)[:, None], np.arange(tq)[None, :]
    mask_bias = jnp.asarray(np.stack([np.where(key_rel <= t_rel, 0.0, NEG),
                                      np.where(t_rel + tk - key_rel < WINDOW, 0.0, NEG)]), F32)
    qblk = pl.BlockSpec((None, 256, tq), lambda bi, g, i: (bi, g, i))
    return pl.pallas_call(
        _slcwin_body, grid=(b, NSA_GROUPS, s // tq),
        in_specs=[pl.BlockSpec(memory_space=pltpu.SMEM), qblk,
                  pl.BlockSpec((None, None, nblk, tq), lambda bi, g, i: (bi, g, 0, i)),
                  pl.BlockSpec((None, None, 2, s, 128), lambda bi, g, i: (bi, g, 0, 0, 0)),
                  pl.BlockSpec((None, None, 2, ns, V_ROWS, tk), lambda bi, g, i: (bi, g, 0, 0, 0, 0)),
                  qblk,
                  pl.BlockSpec((None, None, GATE_ROWS, tq), lambda bi, g, i: (bi, g, 0, i)),
                  _resident((2, tk, tq))],
        out_specs=qblk,
        out_shape=jax.ShapeDtypeStruct((b, 512, s), F32),
        scratch_shapes=[pltpu.VMEM((5, NSA_HPG, 128, tq), BF16),
                        pltpu.VMEM((2, NSA_HPG, 1, tq), F32), pltpu.VMEM((2, NSA_HPG, V_ROWS, tq), F32),
                        pltpu.VMEM((NSA_HPG, tk, tq), F32), pltpu.VMEM((NSA_HPG, tk, tq), BF16)],
        compiler_params=_cparams(("parallel", "parallel", "arbitrary")), name="nsa_selected_window",
    )(score_bound, qt, bias, kaug, vt, oc, gt.reshape(b, NSA_GROUPS, GATE_ROWS, s), mask_bias)


def _hgrn_consts():
    c = HGRN_CHUNK
    t = np.arange(c)
    lower = (t[None, :] <= t[:, None]).astype(np.float32)
    rows = [lower]
    masks = []
    for half in HGRN_LEVELS:
        mid = (t // (2 * half)) * (2 * half) + half - 1
        if half < 8:
            rows.append(lower[mid])
        same = (t[:, None] // (2 * half)) == (t[None, :] // (2 * half))
        right = (t[:, None] & half) != 0
        left = (t[None, :] & half) == 0
        masks.append((same & right & left).astype(np.float32))
    masks.append(np.eye(c, dtype=np.float32))
    masks.append(((t[:, None] // HGRN_SUB == t[None, :] // HGRN_SUB) & (t[None, :] <= t[:, None])).astype(np.float32))
    mall = np.concatenate(rows, axis=0)
    lvl = np.stack([np.tile(mk.T, (1, HGRN_HEADS)) for mk in masks])
    bdm = np.kron(np.eye(HGRN_HEADS), np.ones((64, 64), np.float32))
    return jnp.asarray(mall, BF16), jnp.asarray(lvl, F32), jnp.asarray(bdm, F32), jnp.asarray(bdm / 64, BF16)


def _hgrn_body(hg_ref, lb_ref, og_ref, mall_ref, lvl_ref, bdm_ref, bdn_ref, y_ref, st_scr, attn_scr):
    c = HGRN_CHUNK
    w = HGRN_WIDTH

    @pl.when(pl.program_id(1) == 0)
    def _():
        st_scr[...] = jnp.zeros(st_scr.shape, F32)

    chunks = range(hg_ref.shape[0] // c)
    lb = lb_ref[...]
    lane = lax.broadcasted_iota(jnp.int32, (c, w), 1)
    head_masks = [(lane >> 6) == hh for hh in range(HGRN_HEADS)]
    nlev = len(HGRN_LEVELS)

    def stack_heads(x):
        x16 = x.astype(BF16)
        return jnp.concatenate([jnp.where(hm, x16, 0) for hm in head_masks], axis=0)

    def row_bcast(x, half):
        return jnp.concatenate([jnp.broadcast_to(x[p + half - 1:p + half, :], (2 * half, w))
                                for p in range(0, c, 2 * half)], axis=0)

    qa, kk, v16, logf = [], [], [], []
    for ci in chunks:
        rows = slice(ci * c, (ci + 1) * c)
        qa.append(_silu(hg_ref[rows, 0:w]) * (HEAD_DIM ** -0.5))
        fg = lb + (1.0 - lb) * _sigmoid(hg_ref[rows, w:2 * w])
        kk.append(1.0 - fg)
        logf.append(jnp.log2(fg))
        v16.append(hg_ref[rows, 2 * w:3 * w].astype(BF16))

    mall = mall_ref[...]
    parts = _split3(jnp.concatenate(logf, axis=1))
    b_all = sum(_dot(mall[0:c], part) for part in parts)
    bcum = [b_all[:, ci * w:(ci + 1) * w] for ci in chunks]

    def level(ci, li, ref_pt):
        e = jnp.exp2(-jnp.abs(bcum[ci] - ref_pt))
        return lvl_ref[li] * _dot_nt((kk[ci] * e).astype(BF16), stack_heads(qa[ci] * e))

    coarse = [li for li, half in enumerate(HGRN_LEVELS) if half >= HGRN_SUB]
    if not coarse:
        attn_scr[...] = jnp.zeros(attn_scr.shape, F32)
    for li in coarse:
        for ci in chunks:
            contribution = level(ci, li, row_bcast(bcum[ci], HGRN_LEVELS[li]))
            attn_scr[ci] = contribution if li == coarse[0] else attn_scr[ci] + contribution

    def block_start(x):
        firsts = [jnp.zeros((HGRN_SUB, w), F32)]
        firsts += [jnp.broadcast_to(x[p - 1:p, :], (HGRN_SUB, w)) for p in range(HGRN_SUB, c, HGRN_SUB)]
        return jnp.concatenate(firsts, axis=0)

    expo = [block_start(bcum[ci]) - bcum[ci] for ci in chunks]
    largest = expo[0]
    for ci in chunks[1:]:
        largest = jnp.maximum(largest, expo[ci])
    single_reference_ok = jnp.max(largest) < HGRN_MAX_EXPONENT

    @pl.when(single_reference_ok)
    def _():
        for ci in chunks:
            kt = (kk[ci] * jnp.exp2(expo[ci])).astype(BF16)
            attn_scr[ci] = attn_scr[ci] + lvl_ref[nlev + 1] * _dot_nt(kt, stack_heads(qa[ci] * jnp.exp2(-expo[ci])))

    @pl.when(jnp.logical_not(single_reference_ok))
    def _():
        r_fine = sum(_dot(mall[c:], part) for part in parts)
        for ci in chunks:
            attn_scr[ci] = attn_scr[ci] + lvl_ref[nlev] * _dot_nt(kk[ci].astype(BF16), stack_heads(qa[ci]))
        fine = 0
        for li, half in enumerate(HGRN_LEVELS):
            if half >= HGRN_SUB:
                continue
            for ci in chunks:
                if half >= 8:
                    ref_pt = row_bcast(bcum[ci], half)
                else:
                    ref_pt = r_fine[fine * c:(fine + 1) * c, ci * w:(ci + 1) * w]
                attn_scr[ci] = attn_scr[ci] + level(ci, li, ref_pt)
            if half < 8:
                fine += 1

    attn = [attn_scr[ci] for ci in chunks]

    intra, upd, decay, qb = [], [], [], []
    for ci in chunks:
        x = _dot_tn(attn[ci].astype(BF16), v16[ci])
        intra.append(sum(jnp.where(head_masks[hh], x[hh * c:(hh + 1) * c], 0.0) for hh in range(HGRN_HEADS)))
        b_last = bcum[ci][c - 1:c, :]
        kl = (kk[ci] * jnp.exp2(b_last - bcum[ci])).astype(BF16)
        upd.append(bdm_ref[...] * _dot_tn(v16[ci], kl))
        decay.append(jnp.exp2(b_last))
        qb.append((qa[ci] * jnp.exp2(bcum[ci])).astype(BF16))

    st = st_scr[...]
    inter = []
    for ci in chunks:
        inter.append(_dot_nt(qb[ci], st.astype(BF16)))
        st = st * decay[ci] + upd[ci]
    st_scr[...] = st

    for ci in chunks:
        rows = slice(ci * c, (ci + 1) * c)
        o = inter[ci] + intra[ci]
        hi, lo = _split2(o * o)
        ms = _dot(hi, bdn_ref[...]) + _dot(lo, bdn_ref[...])
        y_ref[rows, :] = o * lax.rsqrt(ms + EPS) * og_ref[...] * _silu(hg_ref[rows, 3 * w:4 * w])


def _hgrn(hg, lower_bound, out_gain):
    b, s, _ = hg.shape
    rows = HGRN_CB * HGRN_CHUNK
    mall, lvl, bdm, bdn = _hgrn_consts()
    full = lambda shape: pl.BlockSpec(shape, lambda bi, i: (0,) * len(shape))
    return pl.pallas_call(
        _hgrn_body, grid=(b, s // rows),
        in_specs=[pl.BlockSpec((None, rows, 4 * HGRN_WIDTH), lambda bi, i: (bi, i, 0)),
                  full((1, HGRN_WIDTH)), full((1, HGRN_WIDTH)),
                  full(mall.shape), full(lvl.shape), full(bdm.shape), full(bdn.shape)],
        out_specs=pl.BlockSpec((None, rows, HGRN_WIDTH), lambda bi, i: (bi, i, 0)),
        out_shape=jax.ShapeDtypeStruct((b, s, HGRN_WIDTH), F32),
        scratch_shapes=[pltpu.VMEM((HGRN_WIDTH, HGRN_WIDTH), F32),
                        pltpu.VMEM((HGRN_CB, HGRN_CHUNK, HGRN_HEADS * HGRN_CHUNK), F32)],
        compiler_params=_cparams(("parallel", "arbitrary")), name="hgrn2_chunks",
    )(hg, lower_bound.reshape(1, -1), out_gain.reshape(1, -1), mall, lvl, bdm, bdn)


def _memkv_body(mem_ref, mg_ref, wk_ref, wvt_ref, kg_ref, kh_ref, vht_ref):
    m = mem_ref.shape[0]
    mn = _rms_rows(mem_ref[...], mg_ref[...]).astype(BF16)
    k = _dot(mn, wk_ref[...])
    vt = _dot_nt(wvt_ref[...], mn)
    ones_rows = (lax.broadcasted_iota(jnp.int32, (V_ROWS - 64, m), 0) == 0).astype(BF16)
    for hh in range(MEM_HEADS):
        kh_ref[hh] = _rms_rows(k[:, hh * 64:(hh + 1) * 64], kg_ref[...]).astype(BF16)
        vht_ref[hh, 0:64, :] = vt[hh * 64:(hh + 1) * 64].astype(BF16)
        vht_ref[hh, 64:V_ROWS, :] = ones_rows


def _mem_kv(mem, mem_gain, wk, wvt, k_gain):
    b, m, d = mem.shape
    full = lambda shape: pl.BlockSpec(shape, lambda bi: (0,) * len(shape))
    return pl.pallas_call(
        _memkv_body, grid=(b,),
        in_specs=[pl.BlockSpec((None, m, d), lambda bi: (bi, 0, 0)), full((1, d)),
                  full(wk.shape), full(wvt.shape), full((1, 64))],
        out_specs=(pl.BlockSpec((None, MEM_HEADS, m, 64), lambda bi: (bi, 0, 0, 0)),
                   pl.BlockSpec((None, MEM_HEADS, V_ROWS, m), lambda bi: (bi, 0, 0, 0))),
        out_shape=(jax.ShapeDtypeStruct((b, MEM_HEADS, m, 64), BF16),
                   jax.ShapeDtypeStruct((b, MEM_HEADS, V_ROWS, m), BF16)),
        compiler_params=_cparams(("parallel",)), name="memory_kv",
    )(mem, mem_gain.reshape(1, d), wk, wvt, k_gain.reshape(1, 64))


def _out_body(x_ref, ynt_ref, yh_ref, qmt_ref, kh_ref, vht_ref, ng_ref, mg_ref, wo_ref,
              fg_ref, wg_ref, wu_ref, wd_ref, o_ref, a_scr):
    scores = [_dot(kh_ref[hh], qmt_ref[hh * 64:(hh + 1) * 64, :]) for hh in range(MEM_HEADS)]
    nsa = _rms_cols(ynt_ref[...], ng_ref[...]).astype(BF16)
    acc = _dot_tn(nsa, wo_ref[0:NSA_WIDTH, :])
    acc = acc + _dot(yh_ref[...].astype(BF16), wo_ref[NSA_WIDTH:NSA_WIDTH + HGRN_WIDTH, :])
    y_mem = []
    for s in scores:
        hh = len(y_mem)
        p = jnp.exp2(s - jnp.max(s, axis=0, keepdims=True))
        o = _dot(vht_ref[hh], p.astype(BF16))
        y_mem.append(o[0:64] * (1.0 / o[64:65]))
    mem = _rms_cols(jnp.concatenate(y_mem, axis=0), mg_ref[...]).astype(BF16)
    acc = acc + _dot_tn(mem, wo_ref[NSA_WIDTH + HGRN_WIDTH:, :])
    o_ref[...] = _ffn_half_step(x_ref[...] + acc, fg_ref, wg_ref, wu_ref, wd_ref, a_scr)


def _out_ffn(x3d, ynt, yh, qmt, kh, vht, nsa_gain, mem_gain, wo, ffn_gain, wg, wu, wd, *, tm=TOK_TILE):
    b, s, d = x3d.shape
    m = kh.shape[2]
    return pl.pallas_call(
        _out_body, grid=(b, s // tm),
        in_specs=[pl.BlockSpec((None, tm, d), lambda bi, i: (bi, i, 0)),
                  pl.BlockSpec((None, NSA_WIDTH, tm), lambda bi, i: (bi, 0, i)),
                  pl.BlockSpec((None, tm, HGRN_WIDTH), lambda bi, i: (bi, i, 0)),
                  pl.BlockSpec((None, MEM_WIDTH, tm), lambda bi, i: (bi, 0, i)),
                  pl.BlockSpec((None, MEM_HEADS, m, 64), lambda bi, i: (bi, 0, 0, 0)),
                  pl.BlockSpec((None, MEM_HEADS, V_ROWS, m), lambda bi, i: (bi, 0, 0, 0)),
                  _resident((NSA_WIDTH, 1)), _resident((MEM_WIDTH, 1)), _resident(wo.shape),
                  _resident((1, d)), _resident(wg.shape), _resident(wu.shape), _resident(wd.shape)],
        out_specs=pl.BlockSpec((None, tm, d), lambda bi, i: (bi, i, 0)),
        out_shape=jax.ShapeDtypeStruct((b, s, d), F32),
        scratch_shapes=[pltpu.VMEM((tm, wg.shape[1]), BF16)],
        compiler_params=_cparams(("parallel", "parallel")), name="mix_out_ffn2",
    )(x3d, ynt, yh, qmt, kh, vht, nsa_gain.reshape(-1, 1), mem_gain.reshape(-1, 1), wo,
      ffn_gain.reshape(1, d), wg, wu, wd)


def _layer(x, mem, ffn1, ffn2, mix_norm, w_in, w_out, nsa_q_norm, nsa_k_norm, cmp_pos_k, cmp_w1_k, cmp_w2_k,
           cmp_pos_v, cmp_w1_v, cmp_w2_v, nsa_out_norm, lower_bound, hgrn_out_norm,
           mem_norm, mem_w_k, mem_w_v, mem_q_norm, mem_k_norm, mem_out_norm):
    b, s, d = x.shape
    sizes = (512, 128, 128, 128, 128, 128, 128, 24, 256, 256, 256, 256, 256)
    offs = np.concatenate([[0], np.cumsum(sizes)])
    w16 = w_in.astype(BF16)
    col = lambda i: w16[:, offs[i]:offs[i + 1]]
    (q_a, k_c, v_c, k_s, v_s, k_w, v_w, g_a, q_h, f_h, i_h, g_h, q_m) = [col(i) for i in range(13)]
    gpad = jnp.zeros((d, GATE_ROWS - 3 * NSA_HPG), BF16)
    wt = jnp.concatenate([q_a, v_s, v_w, g_a[:, :3 * NSA_HPG], gpad, g_a[:, 3 * NSA_HPG:], gpad, q_m], axis=1).T
    wn = jnp.concatenate([k_c, v_c, k_s, k_w, q_h, f_h, i_h, g_h], axis=1)

    x1, qt, vt, gt, qmt, kaug, kvc, hg = _ffn_proj(
        x, *ffn1, mix_norm, wt, wn, nsa_q_norm, nsa_k_norm, mem_q_norm, _rope_tables(s))

    cmp_pos, cmp_w1 = _compress_weights(cmp_pos_k, cmp_pos_v, cmp_w1_k, cmp_w1_v)
    kc, vct = _compress(kvc, cmp_pos, cmp_w1, cmp_w2_k.astype(BF16), cmp_w2_v.T.astype(BF16), nsa_k_norm)
    score_bound = _score_bound(nsa_q_norm, nsa_k_norm)
    oc, bias = _cmp_select(qt, kc, vct, score_bound)
    y_nsa = _slc_win(qt, bias, kaug, vt, oc, gt, score_bound)

    y_hgrn = _hgrn(hg, lower_bound, hgrn_out_norm)

    kh, vht = _mem_kv(mem, mem_norm, mem_w_k.astype(BF16), mem_w_v.T.astype(BF16), mem_k_norm)
    return _out_ffn(x1, y_nsa, y_hgrn, qmt, kh, vht, nsa_out_norm, mem_out_norm, w_out.astype(BF16), *ffn2)


def kernel(x, mem, ffn1_norm, ffn1_w_gate, ffn1_w_up, ffn1_w_down, mix_norm, w_in, w_out, nsa_q_norm, nsa_k_norm, cmp_pos_k, cmp_w1_k, cmp_w2_k, cmp_pos_v, cmp_w1_v, cmp_w2_v, nsa_out_norm, hgrn_lb_logits, hgrn_out_norm, mem_norm, mem_w_k, mem_w_v, mem_q_norm, mem_k_norm, mem_out_norm, ffn2_norm, ffn2_w_gate, ffn2_w_up, ffn2_w_down):
    b, s, d = x.shape
    depth = ffn1_norm.shape[0]
    lower_bounds = jnp.cumsum(jax.nn.softmax(hgrn_lb_logits.astype(F32), axis=0), axis=0)
    bf = lambda a: a.astype(BF16)
    for l in range(depth):
        x = _layer(x, mem, (ffn1_norm[l], bf(ffn1_w_gate[l]), bf(ffn1_w_up[l]), bf(ffn1_w_down[l])),
                   (ffn2_norm[l], bf(ffn2_w_gate[l]), bf(ffn2_w_up[l]), bf(ffn2_w_down[l])),
                   mix_norm[l], w_in[l], w_out[l], nsa_q_norm[l], nsa_k_norm[l],
                   cmp_pos_k[l], cmp_w1_k[l], cmp_w2_k[l], cmp_pos_v[l], cmp_w1_v[l], cmp_w2_v[l],
                   nsa_out_norm[l], lower_bounds[l], hgrn_out_norm[l],
                   mem_norm[l], mem_w_k[l], mem_w_v[l], mem_q_norm[l], mem_k_norm[l], mem_out_norm[l])
    return x
```

```python
import functools

import numpy as np
import jax
import jax.numpy as jnp
from jax import lax
from jax.experimental import pallas as pl
from jax.experimental.pallas import tpu as pltpu

F32 = jnp.float32
BF16 = jnp.bfloat16

HEAD_DIM = 64
ROT_DIM = 16
ROT_HALF = 8
ROPE_THETA = 500000.0
NSA_HEADS = 8
NSA_GROUPS = 2
NSA_HPG = 4
CMP_BLOCK = 32
CMP_STRIDE = 16
SLC_BLOCK = 64
SLC_SHIFT = 6
SLC_TOPK = 16
WINDOW = 512
FORCED_SCORE = 1e4
HGRN_HEADS = 4
HGRN_CHUNK = 64
HGRN_WIDTH = 256
MEM_HEADS = 4
MEM_WIDTH = 256
NSA_WIDTH = 512
EPS = 1e-6
NEG = -1e30
QK_SCALE_LOG2 = HEAD_DIM ** -0.5 * 1.4426950408889634
MIN_DENOMINATOR = 2.0 ** -64

VMEM_LIMIT = 56 * 1024 * 1024
MAX_BLOCKS = 128
GATE_ROWS = 16
V_ROWS = 80
TOK_TILE = 512
HGRN_CB = 16
HGRN_LEVELS = (32, 16, 8, 4, 2, 1)
HGRN_SUB = 64
HGRN_MAX_EXPONENT = 96.0

NT_DIMS = (((1,), (1,)), ((), ()))
TN_DIMS = (((0,), (0,)), ((), ()))


def _cparams(sem):
    return pltpu.CompilerParams(dimension_semantics=sem, vmem_limit_bytes=VMEM_LIMIT)


def _dot(a, b):
    return jnp.dot(a, b, preferred_element_type=F32)


def _dot_nt(a, b):
    return lax.dot_general(a, b, NT_DIMS, preferred_element_type=F32)


def _dot_tn(a, b):
    return lax.dot_general(a, b, TN_DIMS, preferred_element_type=F32)


def _sigmoid(x):
    return 1.0 / (1.0 + jnp.exp(-x))


def _silu(x):
    return x * _sigmoid(x)


def _split2(x):
    hi = x.astype(BF16)
    lo = (x - hi.astype(F32)).astype(BF16)
    return hi, lo


def _split3(x):
    hi = x.astype(BF16)
    r1 = x - hi.astype(F32)
    mid = r1.astype(BF16)
    lo = (r1 - mid.astype(F32)).astype(BF16)
    return hi, mid, lo


def _rms_rows(x, gain_row):
    ms = jnp.mean(x * x, axis=-1, keepdims=True)
    return x * lax.rsqrt(ms + EPS) * gain_row


def _rms_cols(x, gain_col):
    ms = jnp.mean(x * x, axis=0, keepdims=True)
    return x * lax.rsqrt(ms + EPS) * gain_col


def _seg_mean_sq(x, bd):
    hi, lo = _split2(x * x)
    return _dot(hi, bd) + _dot(lo, bd)


FFN_CHUNK = 256


def _ffn_half_step(x, g_ref, wg_ref, wu_ref, wd_ref, a_scr):
    xn = _rms_rows(x, g_ref[...]).astype(BF16)
    d_ff = wg_ref.shape[1]
    for c in range(d_ff // FFN_CHUNK):
        sl = slice(c * FFN_CHUNK, (c + 1) * FFN_CHUNK)
        g = _dot(xn, wg_ref[:, sl])
        u = _dot(xn, wu_ref[:, sl])
        a_scr[:, sl] = (_silu(g) * u).astype(BF16)
    return x + 0.5 * _dot(a_scr[...], wd_ref[...])


def _resident(shape):
    return pl.BlockSpec(shape, lambda *_: (0,) * len(shape), pipeline_mode=pl.Buffered(1))


def _rope_cols(xn, cos, sin):
    x0, x1 = xn[0:ROT_HALF], xn[ROT_HALF:ROT_DIM]
    return jnp.concatenate([x0 * cos - x1 * sin, x1 * cos + x0 * sin, xn[ROT_DIM:]], axis=0)


def _rope_rows(x, cn, sa, sb):
    return x * cn + pltpu.roll(x, 128 - ROT_HALF, 1) * sa + pltpu.roll(x, ROT_HALF, 1) * sb


def _proj_body(x_ref, fg_ref, wg_ref, wu_ref, wd_ref, mg_ref, wt_ref, wn_ref, qg_ref, kg_ref, mqg_ref,
               cos_ref, sin_ref, cn_ref, sa_ref, sb_ref, bd_ref,
               x1_ref, qt_ref, vt_ref, gt_ref, qmt_ref, kaug_ref, kvc_ref, hg_ref, a_scr):
    tm = x_ref.shape[0]
    assert tm == 8 * SLC_BLOCK
    x1 = _ffn_half_step(x_ref[...], fg_ref, wg_ref, wu_ref, wd_ref, a_scr)
    x1_ref[...] = x1
    h = _rms_rows(x1, mg_ref[...]).astype(BF16)

    qg, mqg = qg_ref[...], mqg_ref[...]
    half = tm // 2
    ones_rows = (lax.broadcasted_iota(jnp.int32, (V_ROWS - 64, half), 0) == 0).astype(BF16)
    for part in range(2):
        tok = slice(part * half, (part + 1) * half)
        pt = _dot_nt(wt_ref[...], h[tok, :])
        cos, sin = cos_ref[:, tok], sin_ref[:, tok]
        for hh in range(NSA_HEADS):
            xq = _rms_cols(pt[hh * 64:(hh + 1) * 64], qg)
            qt_ref[hh * 64:(hh + 1) * 64, tok] = (_rope_cols(xq, cos, sin) * QK_SCALE_LOG2).astype(BF16)
        for g in range(NSA_GROUPS):
            for br in range(2):
                rows = 512 + br * 128 + g * 64
                vt_ref[g, br, 0:64, tok] = pt[rows:rows + 64].astype(BF16)
                vt_ref[g, br, 64:V_ROWS, tok] = ones_rows
        gt_ref[:, tok] = _sigmoid(pt[768:800])
        for hh in range(MEM_HEADS):
            xm = _rms_cols(pt[800 + hh * 64:864 + hh * 64], mqg)
            qmt_ref[hh * 64:(hh + 1) * 64, tok] = (xm * QK_SCALE_LOG2).astype(BF16)

    cn, sa, sb = cn_ref[...], sa_ref[...], sb_ref[...]
    bd, kg = bd_ref[...], kg_ref[...]
    pc = _dot(h, wn_ref[:, 0:256])
    pk = _dot(h, wn_ref[:, 256:512])
    kvc_ref[0] = _rope_rows(pc[:, 0:128], cn, sa, sb)
    kvc_ref[1] = pc[:, 128:256]
    ph = _dot(h, wn_ref[:, 512:1024])
    ks = pk[:, 0:128]
    kw = pk[:, 128:256]
    ks = _rope_rows(ks * lax.rsqrt(_seg_mean_sq(ks, bd) + EPS) * kg, cn, sa, sb)
    kw = _rope_rows(kw * lax.rsqrt(_seg_mean_sq(kw, bd) + EPS) * kg, cn, sa, sb)
    lane = lax.broadcasted_iota(jnp.int32, (tm, 128), 1)
    row = lax.broadcasted_iota(jnp.int32, (tm, 128), 0)
    onehot = jnp.where(lane - 64 == (row >> SLC_SHIFT), 1.0, 0.0)
    lo_half = lane < 64
    kaug_ref[0, 0] = jnp.where(lo_half, ks, onehot).astype(BF16)
    kaug_ref[0, 1] = jnp.where(lo_half, kw, 0.0).astype(BF16)
    kaug_ref[1, 0] = jnp.where(lo_half, pltpu.roll(ks, 64, 1), onehot).astype(BF16)
    kaug_ref[1, 1] = jnp.where(lo_half, pltpu.roll(kw, 64, 1), 0.0).astype(BF16)
    hg_ref[:, 0:512] = ph
    hg_ref[:, 512:1024] = _dot(h, wn_ref[:, 1024:1536])


def _ffn_proj(x3d, ffn_gain, wg, wu, wd, mix_gain, wt, wn, q_gain, k_gain, mq_gain, rope):
    b, s, d = x3d.shape
    tm = TOK_TILE
    ns = s // tm
    cos_t, sin_t, cn, sa, sb = rope
    bd = jnp.asarray(np.kron(np.eye(2), np.full((64, 64), 1.0 / 64)), BF16)
    full = _resident
    out_shape = (
        jax.ShapeDtypeStruct((b, s, d), F32),
        jax.ShapeDtypeStruct((b, 512, s), BF16),
        jax.ShapeDtypeStruct((b, 2, 2, ns, V_ROWS, tm), BF16),
        jax.ShapeDtypeStruct((b, 32, s), F32),
        jax.ShapeDtypeStruct((b, 256, s), BF16),
        jax.ShapeDtypeStruct((b, 2, 2, s, 128), BF16),
        jax.ShapeDtypeStruct((b, 2, s, 128), F32),
        jax.ShapeDtypeStruct((b, s, 1024), F32),
    )
    out_specs = (
        pl.BlockSpec((None, tm, d), lambda bi, i: (bi, i, 0)),
        pl.BlockSpec((None, 512, tm), lambda bi, i: (bi, 0, i)),
        pl.BlockSpec((None, 2, 2, None, V_ROWS, tm), lambda bi, i: (bi, 0, 0, i, 0, 0)),
        pl.BlockSpec((None, 32, tm), lambda bi, i: (bi, 0, i)),
        pl.BlockSpec((None, 256, tm), lambda bi, i: (bi, 0, i)),
        pl.BlockSpec((None, 2, 2, tm, 128), lambda bi, i: (bi, 0, 0, i, 0)),
        pl.BlockSpec((None, 2, tm, 128), lambda bi, i: (bi, 0, i, 0)),
        pl.BlockSpec((None, tm, 1024), lambda bi, i: (bi, i, 0)),
    )
    in_specs = [
        pl.BlockSpec((None, tm, d), lambda bi, i: (bi, i, 0)),
        full((1, d)), full(wg.shape), full(wu.shape), full(wd.shape),
        full((1, d)), full(wt.shape), full(wn.shape),
        full((64, 1)), full((1, 128)), full((64, 1)),
        pl.BlockSpec((ROT_HALF, tm), lambda bi, i: (0, i)),
        pl.BlockSpec((ROT_HALF, tm), lambda bi, i: (0, i)),
        pl.BlockSpec((tm, 128), lambda bi, i: (i, 0)),
        pl.BlockSpec((tm, 128), lambda bi, i: (i, 0)),
        pl.BlockSpec((tm, 128), lambda bi, i: (i, 0)),
        full((128, 128)),
    ]
    return pl.pallas_call(
        _proj_body, grid=(b, ns), in_specs=in_specs, out_specs=out_specs, out_shape=out_shape,
        scratch_shapes=[pltpu.VMEM((tm, wg.shape[1]), BF16)],
        compiler_params=_cparams(("parallel", "parallel")), name="ffn1_mix_projection",
    )(x3d, ffn_gain.reshape(1, d), wg, wu, wd, mix_gain.reshape(1, d), wt, wn, q_gain.reshape(64, 1),
      jnp.tile(k_gain.reshape(1, 64), (1, 2)), mq_gain.reshape(64, 1), cos_t, sin_t, cn, sa, sb, bd)


def _rope_tables(s):
    pos = np.arange(s, dtype=np.float64)
    inv = ROPE_THETA ** (-(np.arange(0, ROT_DIM, 2, dtype=np.float64) / ROT_DIM))
    ang = pos[:, None] * inv[None, :]
    cos, sin = np.cos(ang), np.sin(ang)
    zeros = np.zeros((s, 64 - ROT_DIM))
    cn = np.concatenate([cos, cos, np.ones((s, 64 - ROT_DIM))], axis=1)
    sa = np.concatenate([-sin, np.zeros((s, ROT_HALF)), zeros], axis=1)
    sb = np.concatenate([np.zeros((s, ROT_HALF)), sin, zeros], axis=1)
    tile2 = lambda a: np.concatenate([a, a], axis=1)
    return tuple(jnp.asarray(a, F32) for a in (cos.T, sin.T, tile2(cn), tile2(sa), tile2(sb)))


def _cmp_body(kvc_ref, pos_ref, w1_ref, w2k_ref, w2vt_ref, kg_ref, kc_ref, vct_ref):
    nc = kvc_ref.shape[1] // CMP_STRIDE
    for kind in range(2):
        halves = []
        for part in range(2):
            x = jnp.concatenate(
                [(kvc_ref[kind, pl.ds(r, nc, stride=CMP_STRIDE), :]
                  + pos_ref[kind, part, :, r * 128:(r + 1) * 128]).astype(BF16) for r in range(CMP_STRIDE)],
                axis=1)
            halves.append(x)
        for g in range(NSA_GROUPS):
            second = _dot(halves[1], w1_ref[kind, g, 1])
            hid = _silu(_dot(halves[0], w1_ref[kind, g, 0]) + pltpu.roll(second, nc - 1, 0)).astype(BF16)
            if kind == 0:
                kc_ref[g] = _rms_rows(_dot(hid, w2k_ref[...]), kg_ref[...]).astype(BF16)
            else:
                vct_ref[g] = _dot_nt(w2vt_ref[...], hid).astype(BF16)


def _compress(kvc, pos, w1, w2k, w2vt, k_gain):
    b, _, s, _ = kvc.shape
    nc = s // CMP_STRIDE
    return pl.pallas_call(
        _cmp_body, grid=(b,),
        in_specs=[pl.BlockSpec((None, 2, s, 128), lambda bi: (bi, 0, 0, 0)),
                  _resident(pos.shape), _resident(w1.shape), _resident(w2k.shape), _resident(w2vt.shape),
                  _resident((1, 64))],
        out_specs=(pl.BlockSpec((None, 2, nc, 64), lambda bi: (bi, 0, 0, 0)),
                   pl.BlockSpec((None, 2, 64, nc), lambda bi: (bi, 0, 0, 0))),
        out_shape=(jax.ShapeDtypeStruct((b, 2, nc, 64), BF16), jax.ShapeDtypeStruct((b, 2, 64, nc), BF16)),
        compiler_params=_cparams(("parallel",)), name="nsa_compress",
    )(kvc, pos, w1, w2k, w2vt, k_gain.reshape(1, 64))


def _compress_weights(pos_k, pos_v, w1_k, w1_v):
    def pos_part(p):
        p = p.reshape(2, CMP_STRIDE, 1, 64)
        return jnp.broadcast_to(p, (2, CMP_STRIDE, NSA_GROUPS, 64)).reshape(2, 1, CMP_STRIDE * 128)

    def w1_part(w):
        hdim = w.shape[1]
        w = w.astype(BF16).reshape(1, 2, CMP_STRIDE, 1, 64, hdim)
        own_group = jnp.eye(NSA_GROUPS, dtype=BF16).reshape(NSA_GROUPS, 1, 1, NSA_GROUPS, 1, 1)
        return (w * own_group).reshape(NSA_GROUPS, 2, CMP_STRIDE * 128, hdim)

    pos = jnp.stack([pos_part(pos_k), pos_part(pos_v)])
    w1 = jnp.stack([w1_part(w1_k), w1_part(w1_v)])
    return pos, w1


CMP_CLASS_ROWS = 128


def _cmpsel_variant(nc, nblk, fixed_reference, m0_ref, qt_ref, kc_ref, vct_ref, oc_ref, bias_ref, s_scr, flag_scr):
    tq = qt_ref.shape[1]
    t0 = pl.program_id(2) * tq
    n_idx = lax.broadcasted_iota(jnp.int32, (nc, tq), 0)
    t_idx = t0 + lax.broadcasted_iota(jnp.int32, (nc, tq), 1)
    mask_bias = jnp.where(n_idx * CMP_STRIDE + (CMP_BLOCK - 1) <= t_idx, 0.0, NEG)
    sees_any = t0 + lax.broadcasted_iota(jnp.int32, (1, tq), 1) >= CMP_BLOCK - 1
    kc = kc_ref[0:nc, :]
    if fixed_reference:
        mask_bias = mask_bias - m0_ref[0]
    else:
        for hh in range(NSA_HPG):
            s_scr[hh, 0:nc, :] = _dot(kc, qt_ref[hh * 64:(hh + 1) * 64, :]) + mask_bias
    jj = lax.broadcasted_iota(jnp.int32, (nblk, nc), 0)
    nn = lax.broadcasted_iota(jnp.int32, (nblk, nc), 1)
    ov = jnp.where((nn * CMP_STRIDE < jj * SLC_BLOCK + SLC_BLOCK)
                   & (nn * CMP_STRIDE + CMP_BLOCK > jj * SLC_BLOCK), 1.0, 0.0).astype(BF16)
    ones_rows = (lax.broadcasted_iota(jnp.int32, (V_ROWS - 64, nc), 0) == 0).astype(BF16)
    lhs = jnp.concatenate([vct_ref[:, 0:nc], ones_rows, ov], axis=0)
    imp = jnp.zeros((nblk, tq), F32)
    l_min = jnp.full((1, tq), 1.0, F32)

    def probabilities(hh):
        if fixed_reference:
            return jnp.exp2(_dot(kc, qt_ref[hh * 64:(hh + 1) * 64, :]) + mask_bias).astype(BF16)
        m = jnp.max(s_scr[hh, 0:nc, :], axis=0, keepdims=True)
        return jnp.exp2(s_scr[hh, 0:nc, :] - m).astype(BF16)

    p_next = probabilities(0)
    for hh in range(NSA_HPG):
        p = p_next
        if hh + 1 < NSA_HPG:
            p_next = probabilities(hh + 1)
        r = _dot(lhs, p)
        l_min = jnp.minimum(l_min, jnp.where(sees_any, r[64:65], 1.0))
        inv_l = jnp.where(sees_any, 1.0 / r[64:65], 0.0)
        oc_ref[hh * 64:(hh + 1) * 64, :] = r[0:64] * inv_l
        imp = imp + r[V_ROWS:] * inv_l
    if fixed_reference:
        flag_scr[0] = jnp.where(jnp.min(l_min) > MIN_DENOMINATOR, 0, 1)

    j = lax.broadcasted_iota(jnp.int32, (nblk, tq), 0)
    cur = (t0 + lax.broadcasted_iota(jnp.int32, (nblk, tq), 1)) >> SLC_SHIFT
    forced = (j == 0) | (j == cur) | (j == cur - 1)
    picks = SLC_TOPK - 3
    forced_bias = jnp.where(forced & (j <= cur), 0.0, NEG)
    imp = jnp.where((j <= cur) & jnp.logical_not(forced), imp, -1.0)
    if nblk < bias_ref.shape[0]:
        bias_ref[nblk:, :] = jnp.full((bias_ref.shape[0] - nblk, tq), NEG, F32)

    rest = imp
    for _ in range(picks):
        v = jnp.max(rest, axis=0, keepdims=True)
        rest = jnp.where(rest == v, -3e38, rest)
    chosen = (imp >= v) & (imp >= 0.0)
    count = jnp.sum(jnp.where(chosen, 1.0, 0.0), axis=0, keepdims=True)
    valid = jnp.sum(jnp.where(imp >= 0.0, 1.0, 0.0), axis=0, keepdims=True)
    bias_ref[0:nblk, :] = jnp.where(chosen, 0.0, forced_bias)
    has_tie = jnp.max(jnp.abs(count - jnp.minimum(valid, float(picks)))) > 0.0

    @pl.when(has_tie)
    def _():
        jf = j.astype(F32)
        bias, rest = forced_bias, imp
        for _ in range(picks):
            v = jnp.max(rest, axis=0, keepdims=True)
            first = jnp.min(jnp.where(rest == v, jf, float(nblk)), axis=0, keepdims=True)
            pick = jf == first
            bias = jnp.where(pick & (v >= 0.0), 0.0, bias)
            rest = jnp.where(pick, -3e38, rest)
        bias_ref[0:nblk, :] = bias


def _cmpsel_body(m0_ref, qt_ref, kc_ref, vct_ref, oc_ref, bias_ref, s_scr, flag_scr):
    tq = qt_ref.shape[1]
    nc_total = kc_ref.shape[0]
    tiles_per_class = CMP_CLASS_ROWS // (tq // CMP_STRIDE)
    cls = pl.program_id(2) // tiles_per_class
    refs = (m0_ref, qt_ref, kc_ref, vct_ref, oc_ref, bias_ref, s_scr, flag_scr)
    for c in range(nc_total // CMP_CLASS_ROWS):
        nc = (c + 1) * CMP_CLASS_ROWS
        nblk = min(nc * CMP_STRIDE // SLC_BLOCK, bias_ref.shape[0])
        pl.when(cls == c)(functools.partial(_cmpsel_variant, nc, nblk, True, *refs))
    pl.when(flag_scr[0] != 0)(functools.partial(_cmpsel_variant, nc_total, bias_ref.shape[0], False, *refs))


def _cmp_select(qt, kc, vct, score_bound, *, tq=TOK_TILE):
    b, _, s = qt.shape
    nc = kc.shape[2]
    nblk = MAX_BLOCKS
    assert s // SLC_BLOCK <= MAX_BLOCKS and s // SLC_BLOCK >= SLC_TOPK and nc % CMP_CLASS_ROWS == 0
    return pl.pallas_call(
        _cmpsel_body, grid=(b, NSA_GROUPS, s // tq),
        in_specs=[pl.BlockSpec(memory_space=pltpu.SMEM),
                  pl.BlockSpec((None, 256, tq), lambda bi, g, i: (bi, g, i)),
                  pl.BlockSpec((None, None, nc, 64), lambda bi, g, i: (bi, g, 0, 0)),
                  pl.BlockSpec((None, None, 64, nc), lambda bi, g, i: (bi, g, 0, 0))],
        out_specs=(pl.BlockSpec((None, 256, tq), lambda bi, g, i: (bi, g, i)),
                   pl.BlockSpec((None, None, nblk, tq), lambda bi, g, i: (bi, g, 0, i))),
        out_shape=(jax.ShapeDtypeStruct((b, 512, s), F32),
                   jax.ShapeDtypeStruct((b, NSA_GROUPS, nblk, s), F32)),
        scratch_shapes=[pltpu.VMEM((NSA_HPG, nc, tq), F32), pltpu.SMEM((1,), jnp.int32)],
        compiler_params=_cparams(("parallel", "parallel", "parallel")), name="nsa_compressed_select",
    )(score_bound, qt, kc, vct)


def _flash_step(s_ref, vt, m_ref, acc_ref):
    m_old = m_ref[...]
    m_new = jnp.maximum(m_old, jnp.max(s_ref[...], axis=0, keepdims=True))
    p = jnp.exp2(s_ref[...] - m_new)
    acc_ref[...] = jnp.exp2(m_old - m_new) * acc_ref[...] + _dot(vt, p.astype(BF16))
    m_ref[...] = m_new


SEL, WIN = 0, 1
BIAS_ROWS = 16


def _slcwin_body(m0_ref, qt_ref, bias_ref, kaug_ref, vt_ref, oc_ref, gt_ref, mb_ref, y_ref,
                 q_scr, m_scr, acc_scr, s_scr, p_scr):
    tq = qt_ref.shape[1]
    tk = vt_ref.shape[3]
    assert tq == tk and WINDOW == tk and tk == 8 * SLC_BLOCK
    diag = pl.program_id(2)
    m0 = m0_ref[0]

    sel_slots = (0, 1, 2, 3)
    qs, qw = sel_slots[0], 4
    zeros = jnp.zeros((64, tq), BF16)
    for slot in sel_slots + (qw,):
        for hh in range(NSA_HPG):
            q_scr[slot, hh, 0:64, :] = qt_ref[hh * 64:(hh + 1) * 64, :]
            q_scr[slot, hh, 64:128, :] = zeros

    def set_selection_bias(kt, slot=qs):
        rows = bias_ref[pl.ds(pl.multiple_of(kt * 8, 8), 8), :]
        b16 = jnp.concatenate([rows, jnp.zeros_like(rows)], axis=0).astype(BF16)
        for hh in range(NSA_HPG):
            q_scr[slot, hh, 64:64 + BIAS_ROWS, :] = b16

    def tiles_fixed_reference(tiles):
        chains = [(br, qslot, kt, mask_bias, hh) for br, qslot, kt, mask_bias in tiles for hh in range(NSA_HPG)]
        for c in range(len(chains) + 1):
            if c < len(chains):
                br, qslot, kt, mask_bias, hh = chains[c]
                s = _dot(kaug_ref[br, pl.ds(pl.multiple_of(kt * tk, tk), tk), :], q_scr[qslot, hh])
                if mask_bias is not None:
                    s = s + mask_bias()
                p_scr[c % 4] = jnp.exp2(s - m0).astype(BF16)
            if c >= 1:
                br, _, kt, _, hh = chains[c - 1]
                acc_scr[br, hh] = acc_scr[br, hh] + _dot(vt_ref[br, kt], p_scr[(c - 1) % 4])

    def tile_running_max(br, qslot, kt, mask_bias=None):
        k = kaug_ref[br, pl.ds(pl.multiple_of(kt * tk, tk), tk), :]
        for hh in range(NSA_HPG):
            s = _dot(k, q_scr[qslot, hh])
            s_scr[hh] = s if mask_bias is None else s + mask_bias()
        for hh in range(NSA_HPG):
            _flash_step(s_scr.at[hh], vt_ref[br, kt], m_scr.at[br, hh], acc_scr.at[br, hh])

    prev = jnp.maximum(diag - 1, 0)
    no_prev = jnp.where(diag == 0, NEG, 0.0)
    band_bias = lambda: mb_ref[1] + no_prev
    causal_bias = lambda: mb_ref[0]

    acc_scr[...] = jnp.zeros(acc_scr.shape, F32)

    def unmasked_run(first_tile, count):
        for n in range(count):
            set_selection_bias(first_tile + n, sel_slots[n])
        tiles_fixed_reference([(SEL, sel_slots[n], first_tile + n, None) for n in range(count)])

    def tile_quad(j, carry):
        unmasked_run(4 * j, 4)
        return carry

    lax.fori_loop(0, diag >> 2, tile_quad, 0)
    pl.when((diag & 2) != 0)(lambda: unmasked_run((diag >> 2) * 4, 2))
    pl.when((diag & 1) != 0)(lambda: unmasked_run(diag - 1, 1))

    set_selection_bias(diag, qs)
    tiles_fixed_reference([(WIN, qw, prev, band_bias), (SEL, qs, diag, causal_bias), (WIN, qw, diag, causal_bias)])

    denominators = acc_scr[:, :, 64:65, :]
    underflow = jnp.logical_not(jnp.min(denominators) > MIN_DENOMINATOR)

    @pl.when(underflow)
    def _():
        m_scr[...] = jnp.full(m_scr.shape, NEG, F32)
        acc_scr[...] = jnp.zeros(acc_scr.shape, F32)

        def full_tile(kt, carry):
            set_selection_bias(kt, qs)
            tile_running_max(SEL, qs, kt)
            return carry

        lax.fori_loop(0, diag, full_tile, 0)
        set_selection_bias(diag, qs)
        tile_running_max(WIN, qw, prev, band_bias)
        tile_running_max(SEL, qs, diag, causal_bias)
        tile_running_max(WIN, qw, diag, causal_bias)

    gt = gt_ref[...]
    for hh in range(NSA_HPG):
        o_s = acc_scr[SEL, hh, 0:64, :] * (1.0 / acc_scr[SEL, hh, 64:65, :])
        o_w = acc_scr[WIN, hh, 0:64, :] * (1.0 / acc_scr[WIN, hh, 64:65, :])
        y_ref[hh * 64:(hh + 1) * 64, :] = (gt[3 * hh:3 * hh + 1] * oc_ref[hh * 64:(hh + 1) * 64, :]
                                          + gt[3 * hh + 1:3 * hh + 2] * o_s
                                          + gt[3 * hh + 2:3 * hh + 3] * o_w)


def _score_bound(q_gain, k_gain):
    bound = HEAD_DIM * QK_SCALE_LOG2 * jnp.max(jnp.abs(q_gain)) * jnp.max(jnp.abs(k_gain))
    return (1.02 * bound).reshape(1).astype(F32)


def _slc_win(qt, bias, kaug, vt, oc, gt, score_bound):
    b, _, s = qt.shape
    nblk = bias.shape[2]
    ns, tk = vt.shape[3], vt.shape[5]
    tq = tk
    key_rel, t_rel = np.arange(tk)[:, None], np.arange(tq)[None, :]
    mask_bias = jnp.asarray(np.stack([np.where(key_rel <= t_rel, 0.0, NEG),
                                      np.where(t_rel + tk - key_rel < WINDOW, 0.0, NEG)]), F32)
    qblk = pl.BlockSpec((None, 256, tq), lambda bi, g, i: (bi, g, i))
    return pl.pallas_call(
        _slcwin_body, grid=(b, NSA_GROUPS, s // tq),
        in_specs=[pl.BlockSpec(memory_space=pltpu.SMEM), qblk,
                  pl.BlockSpec((None, None, nblk, tq), lambda bi, g, i: (bi, g, 0, i)),
                  pl.BlockSpec((None, None, 2, s, 128), lambda bi, g, i: (bi, g, 0, 0, 0)),
                  pl.BlockSpec((None, None, 2, ns, V_ROWS, tk), lambda bi, g, i: (bi, g, 0, 0, 0, 0)),
                  qblk,
                  pl.BlockSpec((None, None, GATE_ROWS, tq), lambda bi, g, i: (bi, g, 0, i)),
                  _resident((2, tk, tq))],
        out_specs=qblk,
        out_shape=jax.ShapeDtypeStruct((b, 512, s), F32),
        scratch_shapes=[pltpu.VMEM((5, NSA_HPG, 128, tq), BF16),
                        pltpu.VMEM((2, NSA_HPG, 1, tq), F32), pltpu.VMEM((2, NSA_HPG, V_ROWS, tq), F32),
                        pltpu.VMEM((NSA_HPG, tk, tq), F32), pltpu.VMEM((NSA_HPG, tk, tq), BF16)],
        compiler_params=_cparams(("parallel", "parallel", "arbitrary")), name="nsa_selected_window",
    )(score_bound, qt, bias, kaug, vt, oc, gt.reshape(b, NSA_GROUPS, GATE_ROWS, s), mask_bias)


def _hgrn_consts():
    c = HGRN_CHUNK
    t = np.arange(c)
    lower = (t[None, :] <= t[:, None]).astype(np.float32)
    rows = [lower]
    masks = []
    for half in HGRN_LEVELS:
        mid = (t // (2 * half)) * (2 * half) + half - 1
        if half < 8:
            rows.append(lower[mid])
        same = (t[:, None] // (2 * half)) == (t[None, :] // (2 * half))
        right = (t[:, None] & half) != 0
        left = (t[None, :] & half) == 0
        masks.append((same & right & left).astype(np.float32))
    masks.append(np.eye(c, dtype=np.float32))
    masks.append(((t[:, None] // HGRN_SUB == t[None, :] // HGRN_SUB) & (t[None, :] <= t[:, None])).astype(np.float32))
    mall = np.concatenate(rows, axis=0)
    lvl = np.stack([np.tile(mk.T, (1, HGRN_HEADS)) for mk in masks])
    bdm = np.kron(np.eye(HGRN_HEADS), np.ones((64, 64), np.float32))
    return jnp.asarray(mall, BF16), jnp.asarray(lvl, F32), jnp.asarray(bdm, F32), jnp.asarray(bdm / 64, BF16)


def _hgrn_body(hg_ref, lb_ref, og_ref, mall_ref, lvl_ref, bdm_ref, bdn_ref, y_ref, st_scr, attn_scr):
    c = HGRN_CHUNK
    w = HGRN_WIDTH

    @pl.when(pl.program_id(1) == 0)
    def _():
        st_scr[...] = jnp.zeros(st_scr.shape, F32)

    chunks = range(hg_ref.shape[0] // c)
    lb = lb_ref[...]
    lane = lax.broadcasted_iota(jnp.int32, (c, w), 1)
    head_masks = [(lane >> 6) == hh for hh in range(HGRN_HEADS)]
    nlev = len(HGRN_LEVELS)

    def stack_heads(x):
        x16 = x.astype(BF16)
        return jnp.concatenate([jnp.where(hm, x16, 0) for hm in head_masks], axis=0)

    def row_bcast(x, half):
        return jnp.concatenate([jnp.broadcast_to(x[p + half - 1:p + half, :], (2 * half, w))
                                for p in range(0, c, 2 * half)], axis=0)

    qa, kk, v16, logf = [], [], [], []
    for ci in chunks:
        rows = slice(ci * c, (ci + 1) * c)
        qa.append(_silu(hg_ref[rows, 0:w]) * (HEAD_DIM ** -0.5))
        fg = lb + (1.0 - lb) * _sigmoid(hg_ref[rows, w:2 * w])
        kk.append(1.0 - fg)
        logf.append(jnp.log2(fg))
        v16.append(hg_ref[rows, 2 * w:3 * w].astype(BF16))

    mall = mall_ref[...]
    parts = _split3(jnp.concatenate(logf, axis=1))
    b_all = sum(_dot(mall[0:c], part) for part in parts)
    bcum = [b_all[:, ci * w:(ci + 1) * w] for ci in chunks]

    def level(ci, li, ref_pt):
        e = jnp.exp2(-jnp.abs(bcum[ci] - ref_pt))
        return lvl_ref[li] * _dot_nt((kk[ci] * e).astype(BF16), stack_heads(qa[ci] * e))

    coarse = [li for li, half in enumerate(HGRN_LEVELS) if half >= HGRN_SUB]
    if not coarse:
        attn_scr[...] = jnp.zeros(attn_scr.shape, F32)
    for li in coarse:
        for ci in chunks:
            contribution = level(ci, li, row_bcast(bcum[ci], HGRN_LEVELS[li]))
            attn_scr[ci] = contribution if li == coarse[0] else attn_scr[ci] + contribution

    def block_start(x):
        firsts = [jnp.zeros((HGRN_SUB, w), F32)]
        firsts += [jnp.broadcast_to(x[p - 1:p, :], (HGRN_SUB, w)) for p in range(HGRN_SUB, c, HGRN_SUB)]
        return jnp.concatenate(firsts, axis=0)

    expo = [block_start(bcum[ci]) - bcum[ci] for ci in chunks]
    largest = expo[0]
    for ci in chunks[1:]:
        largest = jnp.maximum(largest, expo[ci])
    single_reference_ok = jnp.max(largest) < HGRN_MAX_EXPONENT

    @pl.when(single_reference_ok)
    def _():
        for ci in chunks:
            kt = (kk[ci] * jnp.exp2(expo[ci])).astype(BF16)
            attn_scr[ci] = attn_scr[ci] + lvl_ref[nlev + 1] * _dot_nt(kt, stack_heads(qa[ci] * jnp.exp2(-expo[ci])))

    @pl.when(jnp.logical_not(single_reference_ok))
    def _():
        r_fine = sum(_dot(mall[c:], part) for part in parts)
        for ci in chunks:
            attn_scr[ci] = attn_scr[ci] + lvl_ref[nlev] * _dot_nt(kk[ci].astype(BF16), stack_heads(qa[ci]))
        fine = 0
        for li, half in enumerate(HGRN_LEVELS):
            if half >= HGRN_SUB:
                continue
            for ci in chunks:
                if half >= 8:
                    ref_pt = row_bcast(bcum[ci], half)
                else:
                    ref_pt = r_fine[fine * c:(fine + 1) * c, ci * w:(ci + 1) * w]
                attn_scr[ci] = attn_scr[ci] + level(ci, li, ref_pt)
            if half < 8:
                fine += 1

    attn = [attn_scr[ci] for ci in chunks]

    intra, upd, decay, qb = [], [], [], []
    for ci in chunks:
        x = _dot_tn(attn[ci].astype(BF16), v16[ci])
        intra.append(sum(jnp.where(head_masks[hh], x[hh * c:(hh + 1) * c], 0.0) for hh in range(HGRN_HEADS)))
        b_last = bcum[ci][c - 1:c, :]
        kl = (kk[ci] * jnp.exp2(b_last - bcum[ci])).astype(BF16)
        upd.append(bdm_ref[...] * _dot_tn(v16[ci], kl))
        decay.append(jnp.exp2(b_last))
        qb.append((qa[ci] * jnp.exp2(bcum[ci])).astype(BF16))

    st = st_scr[...]
    inter = []
    for ci in chunks:
        inter.append(_dot_nt(qb[ci], st.astype(BF16)))
        st = st * decay[ci] + upd[ci]
    st_scr[...] = st

    for ci in chunks:
        rows = slice(ci * c, (ci + 1) * c)
        o = inter[ci] + intra[ci]
        hi, lo = _split2(o * o)
        ms = _dot(hi, bdn_ref[...]) + _dot(lo, bdn_ref[...])
        y_ref[rows, :] = o * lax.rsqrt(ms + EPS) * og_ref[...] * _silu(hg_ref[rows, 3 * w:4 * w])


def _hgrn(hg, lower_bound, out_gain):
    b, s, _ = hg.shape
    rows = HGRN_CB * HGRN_CHUNK
    mall, lvl, bdm, bdn = _hgrn_consts()
    full = lambda shape: pl.BlockSpec(shape, lambda bi, i: (0,) * len(shape))
    return pl.pallas_call(
        _hgrn_body, grid=(b, s // rows),
        in_specs=[pl.BlockSpec((None, rows, 4 * HGRN_WIDTH), lambda bi, i: (bi, i, 0)),
                  full((1, HGRN_WIDTH)), full((1, HGRN_WIDTH)),
                  full(mall.shape), full(lvl.shape), full(bdm.shape), full(bdn.shape)],
        out_specs=pl.BlockSpec((None, rows, HGRN_WIDTH), lambda bi, i: (bi, i, 0)),
        out_shape=jax.ShapeDtypeStruct((b, s, HGRN_WIDTH), F32),
        scratch_shapes=[pltpu.VMEM((HGRN_WIDTH, HGRN_WIDTH), F32),
                        pltpu.VMEM((HGRN_CB, HGRN_CHUNK, HGRN_HEADS * HGRN_CHUNK), F32)],
        compiler_params=_cparams(("parallel", "arbitrary")), name="hgrn2_chunks",
    )(hg, lower_bound.reshape(1, -1), out_gain.reshape(1, -1), mall, lvl, bdm, bdn)


def _memkv_body(mem_ref, mg_ref, wk_ref, wvt_ref, kg_ref, kh_ref, vht_ref):
    m = mem_ref.shape[0]
    mn = _rms_rows(mem_ref[...], mg_ref[...]).astype(BF16)
    k = _dot(mn, wk_ref[...])
    vt = _dot_nt(wvt_ref[...], mn)
    ones_rows = (lax.broadcasted_iota(jnp.int32, (V_ROWS - 64, m), 0) == 0).astype(BF16)
    for hh in range(MEM_HEADS):
        kh_ref[hh] = _rms_rows(k[:, hh * 64:(hh + 1) * 64], kg_ref[...]).astype(BF16)
        vht_ref[hh, 0:64, :] = vt[hh * 64:(hh + 1) * 64].astype(BF16)
        vht_ref[hh, 64:V_ROWS, :] = ones_rows


def _mem_kv(mem, mem_gain, wk, wvt, k_gain):
    b, m, d = mem.shape
    full = lambda shape: pl.BlockSpec(shape, lambda bi: (0,) * len(shape))
    return pl.pallas_call(
        _memkv_body, grid=(b,),
        in_specs=[pl.BlockSpec((None, m, d), lambda bi: (bi, 0, 0)), full((1, d)),
                  full(wk.shape), full(wvt.shape), full((1, 64))],
        out_specs=(pl.BlockSpec((None, MEM_HEADS, m, 64), lambda bi: (bi, 0, 0, 0)),
                   pl.BlockSpec((None, MEM_HEADS, V_ROWS, m), lambda bi: (bi, 0, 0, 0))),
        out_shape=(jax.ShapeDtypeStruct((b, MEM_HEADS, m, 64), BF16),
                   jax.ShapeDtypeStruct((b, MEM_HEADS, V_ROWS, m), BF16)),
        compiler_params=_cparams(("parallel",)), name="memory_kv",
    )(mem, mem_gain.reshape(1, d), wk, wvt, k_gain.reshape(1, 64))


def _out_body(x_ref, ynt_ref, yh_ref, qmt_ref, kh_ref, vht_ref, ng_ref, mg_ref, wo_ref,
              fg_ref, wg_ref, wu_ref, wd_ref, o_ref, a_scr):
    scores = [_dot(kh_ref[hh], qmt_ref[hh * 64:(hh + 1) * 64, :]) for hh in range(MEM_HEADS)]
    nsa = _rms_cols(ynt_ref[...], ng_ref[...]).astype(BF16)
    acc = _dot_tn(nsa, wo_ref[0:NSA_WIDTH, :])
    acc = acc + _dot(yh_ref[...].astype(BF16), wo_ref[NSA_WIDTH:NSA_WIDTH + HGRN_WIDTH, :])
    y_mem = []
    for s in scores:
        hh = len(y_mem)
        p = jnp.exp2(s - jnp.max(s, axis=0, keepdims=True))
        o = _dot(vht_ref[hh], p.astype(BF16))
        y_mem.append(o[0:64] * (1.0 / o[64:65]))
    mem = _rms_cols(jnp.concatenate(y_mem, axis=0), mg_ref[...]).astype(BF16)
    acc = acc + _dot_tn(mem, wo_ref[NSA_WIDTH + HGRN_WIDTH:, :])
    o_ref[...] = _ffn_half_step(x_ref[...] + acc, fg_ref, wg_ref, wu_ref, wd_ref, a_scr)


def _out_ffn(x3d, ynt, yh, qmt, kh, vht, nsa_gain, mem_gain, wo, ffn_gain, wg, wu, wd, *, tm=TOK_TILE):
    b, s, d = x3d.shape
    m = kh.shape[2]
    return pl.pallas_call(
        _out_body, grid=(b, s // tm),
        in_specs=[pl.BlockSpec((None, tm, d), lambda bi, i: (bi, i, 0)),
                  pl.BlockSpec((None, NSA_WIDTH, tm), lambda bi, i: (bi, 0, i)),
                  pl.BlockSpec((None, tm, HGRN_WIDTH), lambda bi, i: (bi, i, 0)),
                  pl.BlockSpec((None, MEM_WIDTH, tm), lambda bi, i: (bi, 0, i)),
                  pl.BlockSpec((None, MEM_HEADS, m, 64), lambda bi, i: (bi, 0, 0, 0)),
                  pl.BlockSpec((None, MEM_HEADS, V_ROWS, m), lambda bi, i: (bi, 0, 0, 0)),
                  _resident((NSA_WIDTH, 1)), _resident((MEM_WIDTH, 1)), _resident(wo.shape),
                  _resident((1, d)), _resident(wg.shape), _resident(wu.shape), _resident(wd.shape)],
        out_specs=pl.BlockSpec((None, tm, d), lambda bi, i: (bi, i, 0)),
        out_shape=jax.ShapeDtypeStruct((b, s, d), F32),
        scratch_shapes=[pltpu.VMEM((tm, wg.shape[1]), BF16)],
        compiler_params=_cparams(("parallel", "parallel")), name="mix_out_ffn2",
    )(x3d, ynt, yh, qmt, kh, vht, nsa_gain.reshape(-1, 1), mem_gain.reshape(-1, 1), wo,
      ffn_gain.reshape(1, d), wg, wu, wd)


def _layer(x, mem, ffn1, ffn2, mix_norm, w_in, w_out, nsa_q_norm, nsa_k_norm, cmp_pos_k, cmp_w1_k, cmp_w2_k,
           cmp_pos_v, cmp_w1_v, cmp_w2_v, nsa_out_norm, lower_bound, hgrn_out_norm,
           mem_norm, mem_w_k, mem_w_v, mem_q_norm, mem_k_norm, mem_out_norm):
    b, s, d = x.shape
    sizes = (512, 128, 128, 128, 128, 128, 128, 24, 256, 256, 256, 256, 256)
    offs = np.concatenate([[0], np.cumsum(sizes)])
    w16 = w_in.astype(BF16)
    col = lambda i: w16[:, offs[i]:offs[i + 1]]
    (q_a, k_c, v_c, k_s, v_s, k_w, v_w, g_a, q_h, f_h, i_h, g_h, q_m) = [col(i) for i in range(13)]
    gpad = jnp.zeros((d, GATE_ROWS - 3 * NSA_HPG), BF16)
    wt = jnp.concatenate([q_a, v_s, v_w, g_a[:, :3 * NSA_HPG], gpad, g_a[:, 3 * NSA_HPG:], gpad, q_m], axis=1).T
    wn = jnp.concatenate([k_c, v_c, k_s, k_w, q_h, f_h, i_h, g_h], axis=1)

    x1, qt, vt, gt, qmt, kaug, kvc, hg = _ffn_proj(
        x, *ffn1, mix_norm, wt, wn, nsa_q_norm, nsa_k_norm, mem_q_norm, _rope_tables(s))

    cmp_pos, cmp_w1 = _compress_weights(cmp_pos_k, cmp_pos_v, cmp_w1_k, cmp_w1_v)
    kc, vct = _compress(kvc, cmp_pos, cmp_w1, cmp_w2_k.astype(BF16), cmp_w2_v.T.astype(BF16), nsa_k_norm)
    score_bound = _score_bound(nsa_q_norm, nsa_k_norm)
    oc, bias = _cmp_select(qt, kc, vct, score_bound)
    y_nsa = _slc_win(qt, bias, kaug, vt, oc, gt, score_bound)

    y_hgrn = _hgrn(hg, lower_bound, hgrn_out_norm)

    kh, vht = _mem_kv(mem, mem_norm, mem_w_k.astype(BF16), mem_w_v.T.astype(BF16), mem_k_norm)
    return _out_ffn(x1, y_nsa, y_hgrn, qmt, kh, vht, nsa_out_norm, mem_out_norm, w_out.astype(BF16), *ffn2)


def kernel(x, mem, ffn1_norm, ffn1_w_gate, ffn1_w_up, ffn1_w_down, mix_norm, w_in, w_out, nsa_q_norm, nsa_k_norm, cmp_pos_k, cmp_w1_k, cmp_w2_k, cmp_pos_v, cmp_w1_v, cmp_w2_v, nsa_out_norm, hgrn_lb_logits, hgrn_out_norm, mem_norm, mem_w_k, mem_w_v, mem_q_norm, mem_k_norm, mem_out_norm, ffn2_norm, ffn2_w_gate, ffn2_w_up, ffn2_w_down):
    b, s, d = x.shape
    depth = ffn1_norm.shape[0]
    lower_bounds = jnp.cumsum(jax.nn.softmax(hgrn_lb_logits.astype(F32), axis=0), axis=0)
    bf = lambda a: a.astype(BF16)
    for l in range(depth):
        x = _layer(x, mem, (ffn1_norm[l], bf(ffn1_w_gate[l]), bf(ffn1_w_up[l]), bf(ffn1_w_down[l])),
                   (ffn2_norm[l], bf(ffn2_w_gate[l]), bf(ffn2_w_up[l]), bf(ffn2_w_down[l])),
                   mix_norm[l], w_in[l], w_out[l], nsa_q_norm[l], nsa_k_norm[l],
                   cmp_pos_k[l], cmp_w1_k[l], cmp_w2_k[l], cmp_pos_v[l], cmp_w1_v[l], cmp_w2_v[l],
                   nsa_out_norm[l], lower_bounds[l], hgrn_out_norm[l],
                   mem_norm[l], mem_w_k[l], mem_w_v[l], mem_q_norm[l], mem_k_norm[l], mem_out_norm[l])
    return x
```

```python
import functools

import numpy as np
import jax
import jax.numpy as jnp
from jax import lax
from jax.experimental import pallas as pl
from jax.experimental.pallas import tpu as pltpu

F32 = jnp.float32
BF16 = jnp.bfloat16

HEAD_DIM = 64
ROT_DIM = 16
ROT_HALF = 8
ROPE_THETA = 500000.0
NSA_HEADS = 8
NSA_GROUPS = 2
NSA_HPG = 4
CMP_BLOCK = 32
CMP_STRIDE = 16
SLC_BLOCK = 64
SLC_SHIFT = 6
SLC_TOPK = 16
WINDOW = 512
FORCED_SCORE = 1e4
HGRN_HEADS = 4
HGRN_CHUNK = 64
HGRN_WIDTH = 256
MEM_HEADS = 4
MEM_WIDTH = 256
NSA_WIDTH = 512
EPS = 1e-6
NEG = -1e30
QK_SCALE_LOG2 = HEAD_DIM ** -0.5 * 1.4426950408889634
MIN_DENOMINATOR = 2.0 ** -64

VMEM_LIMIT = 56 * 1024 * 1024
MAX_BLOCKS = 128
GATE_ROWS = 16
V_ROWS = 80
TOK_TILE = 512
HGRN_CB = 16
HGRN_LEVELS = (32, 16, 8, 4, 2, 1)
HGRN_SUB = 64
HGRN_MAX_EXPONENT = 96.0

NT_DIMS = (((1,), (1,)), ((), ()))
TN_DIMS = (((0,), (0,)), ((), ()))


def _cparams(sem):
    return pltpu.CompilerParams(dimension_semantics=sem, vmem_limit_bytes=VMEM_LIMIT)


def _dot(a, b):
    return jnp.dot(a, b, preferred_element_type=F32)


def _dot_nt(a, b):
    return lax.dot_general(a, b, NT_DIMS, preferred_element_type=F32)


def _dot_tn(a, b):
    return lax.dot_general(a, b, TN_DIMS, preferred_element_type=F32)


def _sigmoid(x):
    return 1.0 / (1.0 + jnp.exp(-x))


def _silu(x):
    return x * _sigmoid(x)


def _split2(x):
    hi = x.astype(BF16)
    lo = (x - hi.astype(F32)).astype(BF16)
    return hi, lo


def _split3(x):
    hi = x.astype(BF16)
    r1 = x - hi.astype(F32)
    mid = r1.astype(BF16)
    lo = (r1 - mid.astype(F32)).astype(BF16)
    return hi, mid, lo


def _rms_rows(x, gain_row):
    ms = jnp.mean(x * x, axis=-1, keepdims=True)
    return x * lax.rsqrt(ms + EPS) * gain_row


def _rms_cols(x, gain_col):
    ms = jnp.mean(x * x, axis=0, keepdims=True)
    return x * lax.rsqrt(ms + EPS) * gain_col


def _seg_mean_sq(x, bd):
    hi, lo = _split2(x * x)
    return _dot(hi, bd) + _dot(lo, bd)


FFN_CHUNK = 256


def _ffn_half_step(x, g_ref, wg_ref, wu_ref, wd_ref, a_scr):
    xn = _rms_rows(x, g_ref[...]).astype(BF16)
    d_ff = wg_ref.shape[1]
    for c in range(d_ff // FFN_CHUNK):
        sl = slice(c * FFN_CHUNK, (c + 1) * FFN_CHUNK)
        g = _dot(xn, wg_ref[:, sl])
        u = _dot(xn, wu_ref[:, sl])
        a_scr[:, sl] = (_silu(g) * u).astype(BF16)
    return x + 0.5 * _dot(a_scr[...], wd_ref[...])


def _resident(shape):
    return pl.BlockSpec(shape, lambda *_: (0,) * len(shape), pipeline_mode=pl.Buffered(1))


def _rope_cols(xn, cos, sin):
    x0, x1 = xn[0:ROT_HALF], xn[ROT_HALF:ROT_DIM]
    return jnp.concatenate([x0 * cos - x1 * sin, x1 * cos + x0 * sin, xn[ROT_DIM:]], axis=0)


def _rope_rows(x, cn, sa, sb):
    return x * cn + pltpu.roll(x, 128 - ROT_HALF, 1) * sa + pltpu.roll(x, ROT_HALF, 1) * sb


def _proj_body(x_ref, fg_ref, wg_ref, wu_ref, wd_ref, mg_ref, wt_ref, wn_ref, qg_ref, kg_ref, mqg_ref,
               cos_ref, sin_ref, cn_ref, sa_ref, sb_ref, bd_ref,
               x1_ref, qt_ref, vt_ref, gt_ref, qmt_ref, kaug_ref, kvc_ref, hg_ref, a_scr):
    tm = x_ref.shape[0]
    assert tm == 8 * SLC_BLOCK
    x1 = _ffn_half_step(x_ref[...], fg_ref, wg_ref, wu_ref, wd_ref, a_scr)
    x1_ref[...] = x1
    h = _rms_rows(x1, mg_ref[...]).astype(BF16)

    qg, mqg = qg_ref[...], mqg_ref[...]
    half = tm // 2
    ones_rows = (lax.broadcasted_iota(jnp.int32, (V_ROWS - 64, half), 0) == 0).astype(BF16)
    for part in range(2):
        tok = slice(part * half, (part + 1) * half)
        pt = _dot_nt(wt_ref[...], h[tok, :])
        cos, sin = cos_ref[:, tok], sin_ref[:, tok]
        for hh in range(NSA_HEADS):
            xq = _rms_cols(pt[hh * 64:(hh + 1) * 64], qg)
            qt_ref[hh * 64:(hh + 1) * 64, tok] = (_rope_cols(xq, cos, sin) * QK_SCALE_LOG2).astype(BF16)
        for g in range(NSA_GROUPS):
            for br in range(2):
                rows = 512 + br * 128 + g * 64
                vt_ref[g, br, 0:64, tok] = pt[rows:rows + 64].astype(BF16)
                vt_ref[g, br, 64:V_ROWS, tok] = ones_rows
        gt_ref[:, tok] = _sigmoid(pt[768:800])
        for hh in range(MEM_HEADS):
            xm = _rms_cols(pt[800 + hh * 64:864 + hh * 64], mqg)
            qmt_ref[hh * 64:(hh + 1) * 64, tok] = (xm * QK_SCALE_LOG2).astype(BF16)

    cn, sa, sb = cn_ref[...], sa_ref[...], sb_ref[...]
    bd, kg = bd_ref[...], kg_ref[...]
    pc = _dot(h, wn_ref[:, 0:256])
    pk = _dot(h, wn_ref[:, 256:512])
    kvc_ref[0] = _rope_rows(pc[:, 0:128], cn, sa, sb)
    kvc_ref[1] = pc[:, 128:256]
    ph = _dot(h, wn_ref[:, 512:1024])
    ks = pk[:, 0:128]
    kw = pk[:, 128:256]
    ks = _rope_rows(ks * lax.rsqrt(_seg_mean_sq(ks, bd) + EPS) * kg, cn, sa, sb)
    kw = _rope_rows(kw * lax.rsqrt(_seg_mean_sq(kw, bd) + EPS) * kg, cn, sa, sb)
    lane = lax.broadcasted_iota(jnp.int32, (tm, 128), 1)
    row = lax.broadcasted_iota(jnp.int32, (tm, 128), 0)
    onehot = jnp.where(lane - 64 == (row >> SLC_SHIFT), 1.0, 0.0)
    lo_half = lane < 64
    kaug_ref[0, 0] = jnp.where(lo_half, ks, onehot).astype(BF16)
    kaug_ref[0, 1] = jnp.where(lo_half, kw, 0.0).astype(BF16)
    kaug_ref[1, 0] = jnp.where(lo_half, pltpu.roll(ks, 64, 1), onehot).astype(BF16)
    kaug_ref[1, 1] = jnp.where(lo_half, pltpu.roll(kw, 64, 1), 0.0).astype(BF16)
    hg_ref[:, 0:512] = ph
    hg_ref[:, 512:1024] = _dot(h, wn_ref[:, 1024:1536])


def _ffn_proj(x3d, ffn_gain, wg, wu, wd, mix_gain, wt, wn, q_gain, k_gain, mq_gain, rope):
    b, s, d = x3d.shape
    tm = TOK_TILE
    ns = s // tm
    cos_t, sin_t, cn, sa, sb = rope
    bd = jnp.asarray(np.kron(np.eye(2), np.full((64, 64), 1.0 / 64)), BF16)
    full = _resident
    out_shape = (
        jax.ShapeDtypeStruct((b, s, d), F32),
        jax.ShapeDtypeStruct((b, 512, s), BF16),
        jax.ShapeDtypeStruct((b, 2, 2, ns, V_ROWS, tm), BF16),
        jax.ShapeDtypeStruct((b, 32, s), F32),
        jax.ShapeDtypeStruct((b, 256, s), BF16),
        jax.ShapeDtypeStruct((b, 2, 2, s, 128), BF16),
        jax.ShapeDtypeStruct((b, 2, s, 128), F32),
        jax.ShapeDtypeStruct((b, s, 1024), F32),
    )
    out_specs = (
        pl.BlockSpec((None, tm, d), lambda bi, i: (bi, i, 0)),
        pl.BlockSpec((None, 512, tm), lambda bi, i: (bi, 0, i)),
        pl.BlockSpec((None, 2, 2, None, V_ROWS, tm), lambda bi, i: (bi, 0, 0, i, 0, 0)),
        pl.BlockSpec((None, 32, tm), lambda bi, i: (bi, 0, i)),
        pl.BlockSpec((None, 256, tm), lambda bi, i: (bi, 0, i)),
        pl.BlockSpec((None, 2, 2, tm, 128), lambda bi, i: (bi, 0, 0, i, 0)),
        pl.BlockSpec((None, 2, tm, 128), lambda bi, i: (bi, 0, i, 0)),
        pl.BlockSpec((None, tm, 1024), lambda bi, i: (bi, i, 0)),
    )
    in_specs = [
        pl.BlockSpec((None, tm, d), lambda bi, i: (bi, i, 0)),
        full((1, d)), full(wg.shape), full(wu.shape), full(wd.shape),
        full((1, d)), full(wt.shape), full(wn.shape),
        full((64, 1)), full((1, 128)), full((64, 1)),
        pl.BlockSpec((ROT_HALF, tm), lambda bi, i: (0, i)),
        pl.BlockSpec((ROT_HALF, tm), lambda bi, i: (0, i)),
        pl.BlockSpec((tm, 128), lambda bi, i: (i, 0)),
        pl.BlockSpec((tm, 128), lambda bi, i: (i, 0)),
        pl.BlockSpec((tm, 128), lambda bi, i: (i, 0)),
        full((128, 128)),
    ]
    return pl.pallas_call(
        _proj_body, grid=(b, ns), in_specs=in_specs, out_specs=out_specs, out_shape=out_shape,
        scratch_shapes=[pltpu.VMEM((tm, wg.shape[1]), BF16)],
        compiler_params=_cparams(("parallel", "parallel")), name="ffn1_mix_projection",
    )(x3d, ffn_gain.reshape(1, d), wg, wu, wd, mix_gain.reshape(1, d), wt, wn, q_gain.reshape(64, 1),
      jnp.tile(k_gain.reshape(1, 64), (1, 2)), mq_gain.reshape(64, 1), cos_t, sin_t, cn, sa, sb, bd)


def _rope_tables(s):
    pos = np.arange(s, dtype=np.float64)
    inv = ROPE_THETA ** (-(np.arange(0, ROT_DIM, 2, dtype=np.float64) / ROT_DIM))
    ang = pos[:, None] * inv[None, :]
    cos, sin = np.cos(ang), np.sin(ang)
    zeros = np.zeros((s, 64 - ROT_DIM))
    cn = np.concatenate([cos, cos, np.ones((s, 64 - ROT_DIM))], axis=1)
    sa = np.concatenate([-sin, np.zeros((s, ROT_HALF)), zeros], axis=1)
    sb = np.concatenate([np.zeros((s, ROT_HALF)), sin, zeros], axis=1)
    tile2 = lambda a: np.concatenate([a, a], axis=1)
    return tuple(jnp.asarray(a, F32) for a in (cos.T, sin.T, tile2(cn), tile2(sa), tile2(sb)))


def _cmp_body(kvc_ref, pos_ref, w1_ref, w2k_ref, w2vt_ref, kg_ref, kc_ref, vct_ref):
    nc = kvc_ref.shape[1] // CMP_STRIDE
    for kind in range(2):
        halves = []
        for part in range(2):
            x = jnp.concatenate(
                [(kvc_ref[kind, pl.ds(r, nc, stride=CMP_STRIDE), :]
                  + pos_ref[kind, part, :, r * 128:(r + 1) * 128]).astype(BF16) for r in range(CMP_STRIDE)],
                axis=1)
            halves.append(x)
        for g in range(NSA_GROUPS):
            second = _dot(halves[1], w1_ref[kind, g, 1])
            hid = _silu(_dot(halves[0], w1_ref[kind, g, 0]) + pltpu.roll(second, nc - 1, 0)).astype(BF16)
            if kind == 0:
                kc_ref[g] = _rms_rows(_dot(hid, w2k_ref[...]), kg_ref[...]).astype(BF16)
            else:
                vct_ref[g] = _dot_nt(w2vt_ref[...], hid).astype(BF16)


def _compress(kvc, pos, w1, w2k, w2vt, k_gain):
    b, _, s, _ = kvc.shape
    nc = s // CMP_STRIDE
    return pl.pallas_call(
        _cmp_body, grid=(b,),
        in_specs=[pl.BlockSpec((None, 2, s, 128), lambda bi: (bi, 0, 0, 0)),
                  _resident(pos.shape), _resident(w1.shape), _resident(w2k.shape), _resident(w2vt.shape),
                  _resident((1, 64))],
        out_specs=(pl.BlockSpec((None, 2, nc, 64), lambda bi: (bi, 0, 0, 0)),
                   pl.BlockSpec((None, 2, 64, nc), lambda bi: (bi, 0, 0, 0))),
        out_shape=(jax.ShapeDtypeStruct((b, 2, nc, 64), BF16), jax.ShapeDtypeStruct((b, 2, 64, nc), BF16)),
        compiler_params=_cparams(("parallel",)), name="nsa_compress",
    )(kvc, pos, w1, w2k, w2vt, k_gain.reshape(1, 64))


def _compress_weights(pos_k, pos_v, w1_k, w1_v):
    def pos_part(p):
        p = p.reshape(2, CMP_STRIDE, 1, 64)
        return jnp.broadcast_to(p, (2, CMP_STRIDE, NSA_GROUPS, 64)).reshape(2, 1, CMP_STRIDE * 128)

    def w1_part(w):
        hdim = w.shape[1]
        w = w.astype(BF16).reshape(1, 2, CMP_STRIDE, 1, 64, hdim)
        own_group = jnp.eye(NSA_GROUPS, dtype=BF16).reshape(NSA_GROUPS, 1, 1, NSA_GROUPS, 1, 1)
        return (w * own_group).reshape(NSA_GROUPS, 2, CMP_STRIDE * 128, hdim)

    pos = jnp.stack([pos_part(pos_k), pos_part(pos_v)])
    w1 = jnp.stack([w1_part(w1_k), w1_part(w1_v)])
    return pos, w1


CMP_CLASS_ROWS = 128


def _cmpsel_variant(nc, nblk, fixed_reference, m0_ref, qt_ref, kc_ref, vct_ref, oc_ref, bias_ref, s_scr, flag_scr):
    tq = qt_ref.shape[1]
    t0 = pl.program_id(2) * tq
    n_idx = lax.broadcasted_iota(jnp.int32, (nc, tq), 0)
    t_idx = t0 + lax.broadcasted_iota(jnp.int32, (nc, tq), 1)
    mask_bias = jnp.where(n_idx * CMP_STRIDE + (CMP_BLOCK - 1) <= t_idx, 0.0, NEG)
    sees_any = t0 + lax.broadcasted_iota(jnp.int32, (1, tq), 1) >= CMP_BLOCK - 1
    kc = kc_ref[0:nc, :]
    if fixed_reference:
        mask_bias = mask_bias - m0_ref[0]
    else:
        for hh in range(NSA_HPG):
            s_scr[hh, 0:nc, :] = _dot(kc, qt_ref[hh * 64:(hh + 1) * 64, :]) + mask_bias
    jj = lax.broadcasted_iota(jnp.int32, (nblk, nc), 0)
    nn = lax.broadcasted_iota(jnp.int32, (nblk, nc), 1)
    ov = jnp.where((nn * CMP_STRIDE < jj * SLC_BLOCK + SLC_BLOCK)
                   & (nn * CMP_STRIDE + CMP_BLOCK > jj * SLC_BLOCK), 1.0, 0.0).astype(BF16)
    ones_rows = (lax.broadcasted_iota(jnp.int32, (V_ROWS - 64, nc), 0) == 0).astype(BF16)
    lhs = jnp.concatenate([vct_ref[:, 0:nc], ones_rows, ov], axis=0)
    imp = jnp.zeros((nblk, tq), F32)
    l_min = jnp.full((1, tq), 1.0, F32)

    def probabilities(hh):
        if fixed_reference:
            return jnp.exp2(_dot(kc, qt_ref[hh * 64:(hh + 1) * 64, :]) + mask_bias).astype(BF16)
        m = jnp.max(s_scr[hh, 0:nc, :], axis=0, keepdims=True)
        return jnp.exp2(s_scr[hh, 0:nc, :] - m).astype(BF16)

    p_next = probabilities(0)
    for hh in range(NSA_HPG):
        p = p_next
        if hh + 1 < NSA_HPG:
            p_next = probabilities(hh + 1)
        r = _dot(lhs, p)
        l_min = jnp.minimum(l_min, jnp.where(sees_any, r[64:65], 1.0))
        inv_l = jnp.where(sees_any, 1.0 / r[64:65], 0.0)
        oc_ref[hh * 64:(hh + 1) * 64, :] = r[0:64] * inv_l
        imp = imp + r[V_ROWS:] * inv_l
    if fixed_reference:
        flag_scr[0] = jnp.where(jnp.min(l_min) > MIN_DENOMINATOR, 0, 1)

    j = lax.broadcasted_iota(jnp.int32, (nblk, tq), 0)
    cur = (t0 + lax.broadcasted_iota(jnp.int32, (nblk, tq), 1)) >> SLC_SHIFT
    forced = (j == 0) | (j == cur) | (j == cur - 1)
    picks = SLC_TOPK - 3
    forced_bias = jnp.where(forced & (j <= cur), 0.0, NEG)
    imp = jnp.where((j <= cur) & jnp.logical_not(forced), imp, -1.0)
    if nblk < bias_ref.shape[0]:
        bias_ref[nblk:, :] = jnp.full((bias_ref.shape[0] - nblk, tq), NEG, F32)

    rest = imp
    for _ in range(picks):
        v = jnp.max(rest, axis=0, keepdims=True)
        rest = jnp.where(rest == v, -3e38, rest)
    chosen = (imp >= v) & (imp >= 0.0)
    count = jnp.sum(jnp.where(chosen, 1.0, 0.0), axis=0, keepdims=True)
    valid = jnp.sum(jnp.where(imp >= 0.0, 1.0, 0.0), axis=0, keepdims=True)
    bias_ref[0:nblk, :] = jnp.where(chosen, 0.0, forced_bias)
    has_tie = jnp.max(jnp.abs(count - jnp.minimum(valid, float(picks)))) > 0.0

    @pl.when(has_tie)
    def _():
        jf = j.astype(F32)
        bias, rest = forced_bias, imp
        for _ in range(picks):
            v = jnp.max(rest, axis=0, keepdims=True)
            first = jnp.min(jnp.where(rest == v, jf, float(nblk)), axis=0, keepdims=True)
            pick = jf == first
            bias = jnp.where(pick & (v >= 0.0), 0.0, bias)
            rest = jnp.where(pick, -3e38, rest)
        bias_ref[0:nblk, :] = bias


def _cmpsel_body(m0_ref, qt_ref, kc_ref, vct_ref, oc_ref, bias_ref, s_scr, flag_scr):
    tq = qt_ref.shape[1]
    nc_total = kc_ref.shape[0]
    tiles_per_class = CMP_CLASS_ROWS // (tq // CMP_STRIDE)
    cls = pl.program_id(2) // tiles_per_class
    refs = (m0_ref, qt_ref, kc_ref, vct_ref, oc_ref, bias_ref, s_scr, flag_scr)
    for c in range(nc_total // CMP_CLASS_ROWS):
        nc = (c + 1) * CMP_CLASS_ROWS
        nblk = min(nc * CMP_STRIDE // SLC_BLOCK, bias_ref.shape[0])
        pl.when(cls == c)(functools.partial(_cmpsel_variant, nc, nblk, True, *refs))
    pl.when(flag_scr[0] != 0)(functools.partial(_cmpsel_variant, nc_total, bias_ref.shape[0], False, *refs))


def _cmp_select(qt, kc, vct, score_bound, *, tq=TOK_TILE):
    b, _, s = qt.shape
    nc = kc.shape[2]
    nblk = MAX_BLOCKS
    assert s // SLC_BLOCK <= MAX_BLOCKS and s // SLC_BLOCK >= SLC_TOPK and nc % CMP_CLASS_ROWS == 0
    return pl.pallas_call(
        _cmpsel_body, grid=(b, NSA_GROUPS, s // tq),
        in_specs=[pl.BlockSpec(memory_space=pltpu.SMEM),
                  pl.BlockSpec((None, 256, tq), lambda bi, g, i: (bi, g, i)),
                  pl.BlockSpec((None, None, nc, 64), lambda bi, g, i: (bi, g, 0, 0)),
                  pl.BlockSpec((None, None, 64, nc), lambda bi, g, i: (bi, g, 0, 0))],
        out_specs=(pl.BlockSpec((None, 256, tq), lambda bi, g, i: (bi, g, i)),
                   pl.BlockSpec((None, None, nblk, tq), lambda bi, g, i: (bi, g, 0, i))),
        out_shape=(jax.ShapeDtypeStruct((b, 512, s), F32),
                   jax.ShapeDtypeStruct((b, NSA_GROUPS, nblk, s), F32)),
        scratch_shapes=[pltpu.VMEM((NSA_HPG, nc, tq), F32), pltpu.SMEM((1,), jnp.int32)],
        compiler_params=_cparams(("parallel", "parallel", "parallel")), name="nsa_compressed_select",
    )(score_bound, qt, kc, vct)


def _flash_step(s_ref, vt, m_ref, l_ref, acc_ref):
    m_old = m_ref[...]
    m_new = jnp.maximum(m_old, jnp.max(s_ref[...], axis=0, keepdims=True))
    p = jnp.exp2(s_ref[...] - m_new)
    alpha = jnp.exp2(m_old - m_new)
    l_ref[...] = alpha * l_ref[...] + jnp.sum(p, axis=0, keepdims=True)
    acc_ref[...] = alpha * acc_ref[...] + _dot(vt, p.astype(BF16))
    m_ref[...] = m_new


SEL, WIN = 0, 1
BIAS_ROWS = 16


def _slcwin_body(m0_ref, qt_ref, bias_ref, kaug_ref, vt_ref, oc_ref, gt_ref, mb_ref, y_ref,
                 q_scr, m_scr, l_scr, acc_scr, s_scr, p_scr):
    tq = qt_ref.shape[1]
    tk = vt_ref.shape[3]
    assert tq == tk and WINDOW == tk and tk == 8 * SLC_BLOCK
    diag = pl.program_id(2)
    m0 = m0_ref[0]

    sel_slots = (0, 1, 2, 3)
    qs, qw = sel_slots[0], 4
    zeros = jnp.zeros((64, tq), BF16)
    for slot in sel_slots + (qw,):
        for hh in range(NSA_HPG):
            q_scr[slot, hh, 0:64, :] = qt_ref[hh * 64:(hh + 1) * 64, :]
            q_scr[slot, hh, 64:128, :] = zeros

    def set_selection_bias(kt, slot=qs):
        rows = bias_ref[pl.ds(pl.multiple_of(kt * 8, 8), 8), :]
        b16 = jnp.concatenate([rows, jnp.zeros_like(rows)], axis=0).astype(BF16)
        for hh in range(NSA_HPG):
            q_scr[slot, hh, 64:64 + BIAS_ROWS, :] = b16

    def tiles_fixed_reference(tiles):
        chains = [(br, qslot, kt, mask_bias, hh) for br, qslot, kt, mask_bias in tiles for hh in range(NSA_HPG)]
        for c in range(len(chains) + 1):
            if c < len(chains):
                br, qslot, kt, mask_bias, hh = chains[c]
                s = _dot(kaug_ref[br, pl.ds(pl.multiple_of(kt * tk, tk), tk), :], q_scr[qslot, hh])
                if mask_bias is not None:
                    s = s + mask_bias()
                p = jnp.exp2(s - m0)
                l_scr[br, hh] = l_scr[br, hh] + jnp.sum(p, axis=0, keepdims=True)
                p_scr[c % 4] = p.astype(BF16)
            if c >= 1:
                br, _, kt, _, hh = chains[c - 1]
                acc_scr[br, hh] = acc_scr[br, hh] + _dot(vt_ref[br, kt, 0:64, :], p_scr[(c - 1) % 4])

    def tile_running_max(br, qslot, kt, mask_bias=None):
        k = kaug_ref[br, pl.ds(pl.multiple_of(kt * tk, tk), tk), :]
        for hh in range(NSA_HPG):
            s = _dot(k, q_scr[qslot, hh])
            s_scr[hh] = s if mask_bias is None else s + mask_bias()
        for hh in range(NSA_HPG):
            _flash_step(s_scr.at[hh], vt_ref[br, kt, 0:64, :], m_scr.at[br, hh], l_scr.at[br, hh], acc_scr.at[br, hh])

    prev = jnp.maximum(diag - 1, 0)
    no_prev = jnp.where(diag == 0, NEG, 0.0)
    band_bias = lambda: mb_ref[1] + no_prev
    causal_bias = lambda: mb_ref[0]

    acc_scr[...] = jnp.zeros(acc_scr.shape, F32)
    l_scr[...] = jnp.zeros(l_scr.shape, F32)

    def unmasked_run(first_tile, count):
        for n in range(count):
            set_selection_bias(first_tile + n, sel_slots[n])
        tiles_fixed_reference([(SEL, sel_slots[n], first_tile + n, None) for n in range(count)])

    def tile_quad(j, carry):
        unmasked_run(4 * j, 4)
        return carry

    lax.fori_loop(0, diag >> 2, tile_quad, 0)
    pl.when((diag & 2) != 0)(lambda: unmasked_run((diag >> 2) * 4, 2))
    pl.when((diag & 1) != 0)(lambda: unmasked_run(diag - 1, 1))

    set_selection_bias(diag, qs)
    tiles_fixed_reference([(WIN, qw, prev, band_bias), (SEL, qs, diag, causal_bias), (WIN, qw, diag, causal_bias)])

    underflow = jnp.logical_not(jnp.min(l_scr[...]) > MIN_DENOMINATOR)

    @pl.when(underflow)
    def _():
        m_scr[...] = jnp.full(m_scr.shape, NEG, F32)
        acc_scr[...] = jnp.zeros(acc_scr.shape, F32)
        l_scr[...] = jnp.zeros(l_scr.shape, F32)

        def full_tile(kt, carry):
            set_selection_bias(kt, qs)
            tile_running_max(SEL, qs, kt)
            return carry

        lax.fori_loop(0, diag, full_tile, 0)
        set_selection_bias(diag, qs)
        tile_running_max(WIN, qw, prev, band_bias)
        tile_running_max(SEL, qs, diag, causal_bias)
        tile_running_max(WIN, qw, diag, causal_bias)

    gt = gt_ref[...]
    for hh in range(NSA_HPG):
        o_s = acc_scr[SEL, hh] * (1.0 / l_scr[SEL, hh])
        o_w = acc_scr[WIN, hh] * (1.0 / l_scr[WIN, hh])
        y_ref[hh * 64:(hh + 1) * 64, :] = (gt[3 * hh:3 * hh + 1] * oc_ref[hh * 64:(hh + 1) * 64, :]
                                          + gt[3 * hh + 1:3 * hh + 2] * o_s
                                          + gt[3 * hh + 2:3 * hh + 3] * o_w)


def _score_bound(q_gain, k_gain):
    bound = HEAD_DIM * QK_SCALE_LOG2 * jnp.max(jnp.abs(q_gain)) * jnp.max(jnp.abs(k_gain))
    return (1.02 * bound).reshape(1).astype(F32)


def _slc_win(qt, bias, kaug, vt, oc, gt, score_bound):
    b, _, s = qt.shape
    nblk = bias.shape[2]
    ns, tk = vt.shape[3], vt.shape[5]
    tq = tk
    key_rel, t_rel = np.arange(tk)[:, None], np.arange(tq)[None, :]
    mask_bias = jnp.asarray(np.stack([np.where(key_rel <= t_rel, 0.0, NEG),
                                      np.where(t_rel + tk - key_rel < WINDOW, 0.0, NEG)]), F32)
    qblk = pl.BlockSpec((None, 256, tq), lambda bi, g, i: (bi, g, i))
    return pl.pallas_call(
        _slcwin_body, grid=(b, NSA_GROUPS, s // tq),
        in_specs=[pl.BlockSpec(memory_space=pltpu.SMEM), qblk,
                  pl.BlockSpec((None, None, nblk, tq), lambda bi, g, i: (bi, g, 0, i)),
                  pl.BlockSpec((None, None, 2, s, 128), lambda bi, g, i: (bi, g, 0, 0, 0)),
                  pl.BlockSpec((None, None, 2, ns, V_ROWS, tk), lambda bi, g, i: (bi, g, 0, 0, 0, 0)),
                  qblk,
                  pl.BlockSpec((None, None, GATE_ROWS, tq), lambda bi, g, i: (bi, g, 0, i)),
                  _resident((2, tk, tq))],
        out_specs=qblk,
        out_shape=jax.ShapeDtypeStruct((b, 512, s), F32),
        scratch_shapes=[pltpu.VMEM((5, NSA_HPG, 128, tq), BF16),
                        pltpu.VMEM((2, NSA_HPG, 1, tq), F32), pltpu.VMEM((2, NSA_HPG, 1, tq), F32),
                        pltpu.VMEM((2, NSA_HPG, 64, tq), F32),
                        pltpu.VMEM((NSA_HPG, tk, tq), F32), pltpu.VMEM((NSA_HPG, tk, tq), BF16)],
        compiler_params=_cparams(("parallel", "parallel", "arbitrary")), name="nsa_selected_window",
    )(score_bound, qt, bias, kaug, vt, oc, gt.reshape(b, NSA_GROUPS, GATE_ROWS, s), mask_bias)


def _hgrn_consts():
    c = HGRN_CHUNK
    t = np.arange(c)
    lower = (t[None, :] <= t[:, None]).astype(np.float32)
    rows = [lower]
    masks = []
    for half in HGRN_LEVELS:
        mid = (t // (2 * half)) * (2 * half) + half - 1
        if half < 8:
            rows.append(lower[mid])
        same = (t[:, None] // (2 * half)) == (t[None, :] // (2 * half))
        right = (t[:, None] & half) != 0
        left = (t[None, :] & half) == 0
        masks.append((same & right & left).astype(np.float32))
    masks.append(np.eye(c, dtype=np.float32))
    masks.append(((t[:, None] // HGRN_SUB == t[None, :] // HGRN_SUB) & (t[None, :] <= t[:, None])).astype(np.float32))
    mall = np.concatenate(rows, axis=0)
    lvl = np.stack([np.tile(mk.T, (1, HGRN_HEADS)) for mk in masks])
    bdm = np.kron(np.eye(HGRN_HEADS), np.ones((64, 64), np.float32))
    return jnp.asarray(mall, BF16), jnp.asarray(lvl, F32), jnp.asarray(bdm, F32), jnp.asarray(bdm / 64, BF16)


def _hgrn_body(hg_ref, lb_ref, og_ref, mall_ref, lvl_ref, bdm_ref, bdn_ref, y_ref, st_scr, attn_scr):
    c = HGRN_CHUNK
    w = HGRN_WIDTH

    @pl.when(pl.program_id(1) == 0)
    def _():
        st_scr[...] = jnp.zeros(st_scr.shape, F32)

    chunks = range(hg_ref.shape[0] // c)
    lb = lb_ref[...]
    lane = lax.broadcasted_iota(jnp.int32, (c, w), 1)
    head_masks = [(lane >> 6) == hh for hh in range(HGRN_HEADS)]
    nlev = len(HGRN_LEVELS)

    def stack_heads(x):
        x16 = x.astype(BF16)
        return jnp.concatenate([jnp.where(hm, x16, 0) for hm in head_masks], axis=0)

    def row_bcast(x, half):
        return jnp.concatenate([jnp.broadcast_to(x[p + half - 1:p + half, :], (2 * half, w))
                                for p in range(0, c, 2 * half)], axis=0)

    qa, kk, v16, logf = [], [], [], []
    for ci in chunks:
        rows = slice(ci * c, (ci + 1) * c)
        qa.append(_silu(hg_ref[rows, 0:w]) * (HEAD_DIM ** -0.5))
        fg = lb + (1.0 - lb) * _sigmoid(hg_ref[rows, w:2 * w])
        kk.append(1.0 - fg)
        logf.append(jnp.log2(fg))
        v16.append(hg_ref[rows, 2 * w:3 * w].astype(BF16))

    mall = mall_ref[...]
    parts = _split3(jnp.concatenate(logf, axis=1))
    b_all = sum(_dot(mall[0:c], part) for part in parts)
    bcum = [b_all[:, ci * w:(ci + 1) * w] for ci in chunks]

    def level(ci, li, ref_pt):
        e = jnp.exp2(-jnp.abs(bcum[ci] - ref_pt))
        return lvl_ref[li] * _dot_nt((kk[ci] * e).astype(BF16), stack_heads(qa[ci] * e))

    coarse = [li for li, half in enumerate(HGRN_LEVELS) if half >= HGRN_SUB]
    if not coarse:
        attn_scr[...] = jnp.zeros(attn_scr.shape, F32)
    for li in coarse:
        for ci in chunks:
            contribution = level(ci, li, row_bcast(bcum[ci], HGRN_LEVELS[li]))
            attn_scr[ci] = contribution if li == coarse[0] else attn_scr[ci] + contribution

    def block_start(x):
        firsts = [jnp.zeros((HGRN_SUB, w), F32)]
        firsts += [jnp.broadcast_to(x[p - 1:p, :], (HGRN_SUB, w)) for p in range(HGRN_SUB, c, HGRN_SUB)]
        return jnp.concatenate(firsts, axis=0)

    expo = [block_start(bcum[ci]) - bcum[ci] for ci in chunks]
    largest = expo[0]
    for ci in chunks[1:]:
        largest = jnp.maximum(largest, expo[ci])
    single_reference_ok = jnp.max(largest) < HGRN_MAX_EXPONENT

    @pl.when(single_reference_ok)
    def _():
        for ci in chunks:
            kt = (kk[ci] * jnp.exp2(expo[ci])).astype(BF16)
            attn_scr[ci] = attn_scr[ci] + lvl_ref[nlev + 1] * _dot_nt(kt, stack_heads(qa[ci] * jnp.exp2(-expo[ci])))

    @pl.when(jnp.logical_not(single_reference_ok))
    def _():
        r_fine = sum(_dot(mall[c:], part) for part in parts)
        for ci in chunks:
            attn_scr[ci] = attn_scr[ci] + lvl_ref[nlev] * _dot_nt(kk[ci].astype(BF16), stack_heads(qa[ci]))
        fine = 0
        for li, half in enumerate(HGRN_LEVELS):
            if half >= HGRN_SUB:
                continue
            for ci in chunks:
                if half >= 8:
                    ref_pt = row_bcast(bcum[ci], half)
                else:
                    ref_pt = r_fine[fine * c:(fine + 1) * c, ci * w:(ci + 1) * w]
                attn_scr[ci] = attn_scr[ci] + level(ci, li, ref_pt)
            if half < 8:
                fine += 1

    attn = [attn_scr[ci] for ci in chunks]

    intra, upd, decay, qb = [], [], [], []
    for ci in chunks:
        x = _dot_tn(attn[ci].astype(BF16), v16[ci])
        intra.append(sum(jnp.where(head_masks[hh], x[hh * c:(hh + 1) * c], 0.0) for hh in range(HGRN_HEADS)))
        b_last = bcum[ci][c - 1:c, :]
        kl = (kk[ci] * jnp.exp2(b_last - bcum[ci])).astype(BF16)
        upd.append(bdm_ref[...] * _dot_tn(v16[ci], kl))
        decay.append(jnp.exp2(b_last))
        qb.append((qa[ci] * jnp.exp2(bcum[ci])).astype(BF16))

    st = st_scr[...]
    inter = []
    for ci in chunks:
        inter.append(_dot_nt(qb[ci], st.astype(BF16)))
        st = st * decay[ci] + upd[ci]
    st_scr[...] = st

    for ci in chunks:
        rows = slice(ci * c, (ci + 1) * c)
        o = inter[ci] + intra[ci]
        hi, lo = _split2(o * o)
        ms = _dot(hi, bdn_ref[...]) + _dot(lo, bdn_ref[...])
        y_ref[rows, :] = o * lax.rsqrt(ms + EPS) * og_ref[...] * _silu(hg_ref[rows, 3 * w:4 * w])


def _hgrn(hg, lower_bound, out_gain):
    b, s, _ = hg.shape
    rows = HGRN_CB * HGRN_CHUNK
    mall, lvl, bdm, bdn = _hgrn_consts()
    full = lambda shape: pl.BlockSpec(shape, lambda bi, i: (0,) * len(shape))
    return pl.pallas_call(
        _hgrn_body, grid=(b, s // rows),
        in_specs=[pl.BlockSpec((None, rows, 4 * HGRN_WIDTH), lambda bi, i: (bi, i, 0)),
                  full((1, HGRN_WIDTH)), full((1, HGRN_WIDTH)),
                  full(mall.shape), full(lvl.shape), full(bdm.shape), full(bdn.shape)],
        out_specs=pl.BlockSpec((None, rows, HGRN_WIDTH), lambda bi, i: (bi, i, 0)),
        out_shape=jax.ShapeDtypeStruct((b, s, HGRN_WIDTH), F32),
        scratch_shapes=[pltpu.VMEM((HGRN_WIDTH, HGRN_WIDTH), F32),
                        pltpu.VMEM((HGRN_CB, HGRN_CHUNK, HGRN_HEADS * HGRN_CHUNK), F32)],
        compiler_params=_cparams(("parallel", "arbitrary")), name="hgrn2_chunks",
    )(hg, lower_bound.reshape(1, -1), out_gain.reshape(1, -1), mall, lvl, bdm, bdn)


def _memkv_body(mem_ref, mg_ref, wk_ref, wvt_ref, kg_ref, kh_ref, vht_ref):
    m = mem_ref.shape[0]
    mn = _rms_rows(mem_ref[...], mg_ref[...]).astype(BF16)
    k = _dot(mn, wk_ref[...])
    vt = _dot_nt(wvt_ref[...], mn)
    ones_rows = (lax.broadcasted_iota(jnp.int32, (V_ROWS - 64, m), 0) == 0).astype(BF16)
    for hh in range(MEM_HEADS):
        kh_ref[hh] = _rms_rows(k[:, hh * 64:(hh + 1) * 64], kg_ref[...]).astype(BF16)
        vht_ref[hh, 0:64, :] = vt[hh * 64:(hh + 1) * 64].astype(BF16)
        vht_ref[hh, 64:V_ROWS, :] = ones_rows


def _mem_kv(mem, mem_gain, wk, wvt, k_gain):
    b, m, d = mem.shape
    full = lambda shape: pl.BlockSpec(shape, lambda bi: (0,) * len(shape))
    return pl.pallas_call(
        _memkv_body, grid=(b,),
        in_specs=[pl.BlockSpec((None, m, d), lambda bi: (bi, 0, 0)), full((1, d)),
                  full(wk.shape), full(wvt.shape), full((1, 64))],
        out_specs=(pl.BlockSpec((None, MEM_HEADS, m, 64), lambda bi: (bi, 0, 0, 0)),
                   pl.BlockSpec((None, MEM_HEADS, V_ROWS, m), lambda bi: (bi, 0, 0, 0))),
        out_shape=(jax.ShapeDtypeStruct((b, MEM_HEADS, m, 64), BF16),
                   jax.ShapeDtypeStruct((b, MEM_HEADS, V_ROWS, m), BF16)),
        compiler_params=_cparams(("parallel",)), name="memory_kv",
    )(mem, mem_gain.reshape(1, d), wk, wvt, k_gain.reshape(1, 64))


def _out_body(x_ref, ynt_ref, yh_ref, qmt_ref, kh_ref, vht_ref, ng_ref, mg_ref, wo_ref,
              fg_ref, wg_ref, wu_ref, wd_ref, o_ref, a_scr):
    scores = [_dot(kh_ref[hh], qmt_ref[hh * 64:(hh + 1) * 64, :]) for hh in range(MEM_HEADS)]
    nsa = _rms_cols(ynt_ref[...], ng_ref[...]).astype(BF16)
    acc = _dot_tn(nsa, wo_ref[0:NSA_WIDTH, :])
    acc = acc + _dot(yh_ref[...].astype(BF16), wo_ref[NSA_WIDTH:NSA_WIDTH + HGRN_WIDTH, :])
    y_mem = []
    for s in scores:
        hh = len(y_mem)
        p = jnp.exp2(s - jnp.max(s, axis=0, keepdims=True))
        o = _dot(vht_ref[hh], p.astype(BF16))
        y_mem.append(o[0:64] * (1.0 / o[64:65]))
    mem = _rms_cols(jnp.concatenate(y_mem, axis=0), mg_ref[...]).astype(BF16)
    acc = acc + _dot_tn(mem, wo_ref[NSA_WIDTH + HGRN_WIDTH:, :])
    o_ref[...] = _ffn_half_step(x_ref[...] + acc, fg_ref, wg_ref, wu_ref, wd_ref, a_scr)


def _out_ffn(x3d, ynt, yh, qmt, kh, vht, nsa_gain, mem_gain, wo, ffn_gain, wg, wu, wd, *, tm=TOK_TILE):
    b, s, d = x3d.shape
    m = kh.shape[2]
    return pl.pallas_call(
        _out_body, grid=(b, s // tm),
        in_specs=[pl.BlockSpec((None, tm, d), lambda bi, i: (bi, i, 0)),
                  pl.BlockSpec((None, NSA_WIDTH, tm), lambda bi, i: (bi, 0, i)),
                  pl.BlockSpec((None, tm, HGRN_WIDTH), lambda bi, i: (bi, i, 0)),
                  pl.BlockSpec((None, MEM_WIDTH, tm), lambda bi, i: (bi, 0, i)),
                  pl.BlockSpec((None, MEM_HEADS, m, 64), lambda bi, i: (bi, 0, 0, 0)),
                  pl.BlockSpec((None, MEM_HEADS, V_ROWS, m), lambda bi, i: (bi, 0, 0, 0)),
                  _resident((NSA_WIDTH, 1)), _resident((MEM_WIDTH, 1)), _resident(wo.shape),
                  _resident((1, d)), _resident(wg.shape), _resident(wu.shape), _resident(wd.shape)],
        out_specs=pl.BlockSpec((None, tm, d), lambda bi, i: (bi, i, 0)),
        out_shape=jax.ShapeDtypeStruct((b, s, d), F32),
        scratch_shapes=[pltpu.VMEM((tm, wg.shape[1]), BF16)],
        compiler_params=_cparams(("parallel", "parallel")), name="mix_out_ffn2",
    )(x3d, ynt, yh, qmt, kh, vht, nsa_gain.reshape(-1, 1), mem_gain.reshape(-1, 1), wo,
      ffn_gain.reshape(1, d), wg, wu, wd)


def _layer(x, mem, ffn1, ffn2, mix_norm, w_in, w_out, nsa_q_norm, nsa_k_norm, cmp_pos_k, cmp_w1_k, cmp_w2_k,
           cmp_pos_v, cmp_w1_v, cmp_w2_v, nsa_out_norm, lower_bound, hgrn_out_norm,
           mem_norm, mem_w_k, mem_w_v, mem_q_norm, mem_k_norm, mem_out_norm):
    b, s, d = x.shape
    sizes = (512, 128, 128, 128, 128, 128, 128, 24, 256, 256, 256, 256, 256)
    offs = np.concatenate([[0], np.cumsum(sizes)])
    w16 = w_in.astype(BF16)
    col = lambda i: w16[:, offs[i]:offs[i + 1]]
    (q_a, k_c, v_c, k_s, v_s, k_w, v_w, g_a, q_h, f_h, i_h, g_h, q_m) = [col(i) for i in range(13)]
    gpad = jnp.zeros((d, GATE_ROWS - 3 * NSA_HPG), BF16)
    wt = jnp.concatenate([q_a, v_s, v_w, g_a[:, :3 * NSA_HPG], gpad, g_a[:, 3 * NSA_HPG:], gpad, q_m], axis=1).T
    wn = jnp.concatenate([k_c, v_c, k_s, k_w, q_h, f_h, i_h, g_h], axis=1)

    x1, qt, vt, gt, qmt, kaug, kvc, hg = _ffn_proj(
        x, *ffn1, mix_norm, wt, wn, nsa_q_norm, nsa_k_norm, mem_q_norm, _rope_tables(s))

    cmp_pos, cmp_w1 = _compress_weights(cmp_pos_k, cmp_pos_v, cmp_w1_k, cmp_w1_v)
    kc, vct = _compress(kvc, cmp_pos, cmp_w1, cmp_w2_k.astype(BF16), cmp_w2_v.T.astype(BF16), nsa_k_norm)
    score_bound = _score_bound(nsa_q_norm, nsa_k_norm)
    oc, bias = _cmp_select(qt, kc, vct, score_bound)
    y_nsa = _slc_win(qt, bias, kaug, vt, oc, gt, score_bound)

    y_hgrn = _hgrn(hg, lower_bound, hgrn_out_norm)

    kh, vht = _mem_kv(mem, mem_norm, mem_w_k.astype(BF16), mem_w_v.T.astype(BF16), mem_k_norm)
    return _out_ffn(x1, y_nsa, y_hgrn, qmt, kh, vht, nsa_out_norm, mem_out_norm, w_out.astype(BF16), *ffn2)


def kernel(x, mem, ffn1_norm, ffn1_w_gate, ffn1_w_up, ffn1_w_down, mix_norm, w_in, w_out, nsa_q_norm, nsa_k_norm, cmp_pos_k, cmp_w1_k, cmp_w2_k, cmp_pos_v, cmp_w1_v, cmp_w2_v, nsa_out_norm, hgrn_lb_logits, hgrn_out_norm, mem_norm, mem_w_k, mem_w_v, mem_q_norm, mem_k_norm, mem_out_norm, ffn2_norm, ffn2_w_gate, ffn2_w_up, ffn2_w_down):
    b, s, d = x.shape
    depth = ffn1_norm.shape[0]
    lower_bounds = jnp.cumsum(jax.nn.softmax(hgrn_lb_logits.astype(F32), axis=0), axis=0)
    bf = lambda a: a.astype(BF16)
    for l in range(depth):
        x = _layer(x, mem, (ffn1_norm[l], bf(ffn1_w_gate[l]), bf(ffn1_w_up[l]), bf(ffn1_w_down[l])),
                   (ffn2_norm[l], bf(ffn2_w_gate[l]), bf(ffn2_w_up[l]), bf(ffn2_w_down[l])),
                   mix_norm[l], w_in[l], w_out[l], nsa_q_norm[l], nsa_k_norm[l],
                   cmp_pos_k[l], cmp_w1_k[l], cmp_w2_k[l], cmp_pos_v[l], cmp_w1_v[l], cmp_w2_v[l],
                   nsa_out_norm[l], lower_bounds[l], hgrn_out_norm[l],
                   mem_norm[l], mem_w_k[l], mem_w_v[l], mem_q_norm[l], mem_k_norm[l], mem_out_norm[l])
    return x
```

```python
import functools

import numpy as np
import jax
import jax.numpy as jnp
from jax import lax
from jax.experimental import pallas as pl
from jax.experimental.pallas import tpu as pltpu

F32 = jnp.float32
BF16 = jnp.bfloat16

HEAD_DIM = 64
ROT_DIM = 16
ROT_HALF = 8
ROPE_THETA = 500000.0
NSA_HEADS = 8
NSA_GROUPS = 2
NSA_HPG = 4
CMP_BLOCK = 32
CMP_STRIDE = 16
SLC_BLOCK = 64
SLC_SHIFT = 6
SLC_TOPK = 16
WINDOW = 512
FORCED_SCORE = 1e4
HGRN_HEADS = 4
HGRN_CHUNK = 64
HGRN_WIDTH = 256
MEM_HEADS = 4
MEM_WIDTH = 256
NSA_WIDTH = 512
EPS = 1e-6
NEG = -1e30
QK_SCALE_LOG2 = HEAD_DIM ** -0.5 * 1.4426950408889634
MIN_DENOMINATOR = 2.0 ** -64

VMEM_LIMIT = 56 * 1024 * 1024
MAX_BLOCKS = 128
GATE_ROWS = 16
V_ROWS = 80
TOK_TILE = 512
HGRN_CB = 16
HGRN_LEVELS = (32, 16, 8, 4, 2, 1)
HGRN_SUB = 64
HGRN_MAX_EXPONENT = 96.0

NT_DIMS = (((1,), (1,)), ((), ()))
TN_DIMS = (((0,), (0,)), ((), ()))


def _cparams(sem):
    return pltpu.CompilerParams(dimension_semantics=sem, vmem_limit_bytes=VMEM_LIMIT)


def _dot(a, b):
    return jnp.dot(a, b, preferred_element_type=F32)


def _dot_nt(a, b):
    return lax.dot_general(a, b, NT_DIMS, preferred_element_type=F32)


def _dot_tn(a, b):
    return lax.dot_general(a, b, TN_DIMS, preferred_element_type=F32)


def _sigmoid(x):
    return 1.0 / (1.0 + jnp.exp(-x))


def _silu(x):
    return x * _sigmoid(x)


def _split2(x):
    hi = x.astype(BF16)
    lo = (x - hi.astype(F32)).astype(BF16)
    return hi, lo


def _split3(x):
    hi = x.astype(BF16)
    r1 = x - hi.astype(F32)
    mid = r1.astype(BF16)
    lo = (r1 - mid.astype(F32)).astype(BF16)
    return hi, mid, lo


def _rms_rows(x, gain_row):
    ms = jnp.mean(x * x, axis=-1, keepdims=True)
    return x * lax.rsqrt(ms + EPS) * gain_row


def _rms_cols(x, gain_col):
    ms = jnp.mean(x * x, axis=0, keepdims=True)
    return x * lax.rsqrt(ms + EPS) * gain_col


def _seg_mean_sq(x, bd):
    hi, lo = _split2(x * x)
    return _dot(hi, bd) + _dot(lo, bd)


FFN_CHUNK = 256


def _ffn_half_step(x, g_ref, wg_ref, wu_ref, wd_ref, a_scr):
    xg = (x * g_ref[...]).astype(BF16)
    inv_rms = lax.rsqrt(jnp.mean(x * x, axis=-1, keepdims=True) + EPS)
    d_ff = wg_ref.shape[1]
    for c in range(d_ff // FFN_CHUNK):
        sl = slice(c * FFN_CHUNK, (c + 1) * FFN_CHUNK)
        g = _dot(xg, wg_ref[:, sl]) * inv_rms
        u = _dot(xg, wu_ref[:, sl]) * inv_rms
        a_scr[:, sl] = (_silu(g) * u).astype(BF16)
    return x + 0.5 * _dot(a_scr[...], wd_ref[...])


def _resident(shape):
    return pl.BlockSpec(shape, lambda *_: (0,) * len(shape), pipeline_mode=pl.Buffered(1))


def _rope_cols(xn, cos, sin):
    x0, x1 = xn[0:ROT_HALF], xn[ROT_HALF:ROT_DIM]
    return jnp.concatenate([x0 * cos - x1 * sin, x1 * cos + x0 * sin, xn[ROT_DIM:]], axis=0)


def _rope_rows(x, cn, sa, sb):
    return x * cn + pltpu.roll(x, 128 - ROT_HALF, 1) * sa + pltpu.roll(x, ROT_HALF, 1) * sb


def _proj_body(x_ref, fg_ref, wg_ref, wu_ref, wd_ref, mg_ref, wt_ref, wn_ref, qg_ref, kg_ref, mqg_ref,
               cos_ref, sin_ref, cn_ref, sa_ref, sb_ref, bd_ref,
               x1_ref, qt_ref, vt_ref, gt_ref, qmt_ref, kaug_ref, kvc_ref, hg_ref, a_scr):
    tm = x_ref.shape[0]
    assert tm == 8 * SLC_BLOCK
    x1 = _ffn_half_step(x_ref[...], fg_ref, wg_ref, wu_ref, wd_ref, a_scr)
    x1_ref[...] = x1
    h = _rms_rows(x1, mg_ref[...]).astype(BF16)

    qg, mqg = qg_ref[...], mqg_ref[...]
    half = tm // 2
    ones_rows = (lax.broadcasted_iota(jnp.int32, (V_ROWS - 64, half), 0) == 0).astype(BF16)
    for part in range(2):
        tok = slice(part * half, (part + 1) * half)
        pt = _dot_nt(wt_ref[...], h[tok, :])
        cos, sin = cos_ref[:, tok], sin_ref[:, tok]
        for hh in range(NSA_HEADS):
            xq = _rms_cols(pt[hh * 64:(hh + 1) * 64], qg)
            qt_ref[hh * 64:(hh + 1) * 64, tok] = (_rope_cols(xq, cos, sin) * QK_SCALE_LOG2).astype(BF16)
        for g in range(NSA_GROUPS):
            for br in range(2):
                rows = 512 + br * 128 + g * 64
                vt_ref[g, br, 0:64, tok] = pt[rows:rows + 64].astype(BF16)
                vt_ref[g, br, 64:V_ROWS, tok] = ones_rows
        gt_ref[:, tok] = _sigmoid(pt[768:800])
        for hh in range(MEM_HEADS):
            xm = _rms_cols(pt[800 + hh * 64:864 + hh * 64], mqg)
            qmt_ref[hh * 64:(hh + 1) * 64, tok] = (xm * QK_SCALE_LOG2).astype(BF16)

    cn, sa, sb = cn_ref[...], sa_ref[...], sb_ref[...]
    bd, kg = bd_ref[...], kg_ref[...]
    pc = _dot(h, wn_ref[:, 0:256])
    pk = _dot(h, wn_ref[:, 256:512])
    kvc_ref[0] = _rope_rows(pc[:, 0:128], cn, sa, sb)
    kvc_ref[1] = pc[:, 128:256]
    ph = _dot(h, wn_ref[:, 512:1024])
    ks = pk[:, 0:128]
    kw = pk[:, 128:256]
    ks = _rope_rows(ks * lax.rsqrt(_seg_mean_sq(ks, bd) + EPS) * kg, cn, sa, sb)
    kw = _rope_rows(kw * lax.rsqrt(_seg_mean_sq(kw, bd) + EPS) * kg, cn, sa, sb)
    lane = lax.broadcasted_iota(jnp.int32, (tm, 128), 1)
    row = lax.broadcasted_iota(jnp.int32, (tm, 128), 0)
    onehot = jnp.where(lane - 64 == (row >> SLC_SHIFT), 1.0, 0.0)
    lo_half = lane < 64
    kaug_ref[0, 0] = jnp.where(lo_half, ks, onehot).astype(BF16)
    kaug_ref[0, 1] = jnp.where(lo_half, kw, 0.0).astype(BF16)
    kaug_ref[1, 0] = jnp.where(lo_half, pltpu.roll(ks, 64, 1), onehot).astype(BF16)
    kaug_ref[1, 1] = jnp.where(lo_half, pltpu.roll(kw, 64, 1), 0.0).astype(BF16)
    hg_ref[:, 0:512] = ph
    hg_ref[:, 512:1024] = _dot(h, wn_ref[:, 1024:1536])


def _ffn_proj(x3d, ffn_gain, wg, wu, wd, mix_gain, wt, wn, q_gain, k_gain, mq_gain, rope):
    b, s, d = x3d.shape
    tm = TOK_TILE
    ns = s // tm
    cos_t, sin_t, cn, sa, sb = rope
    bd = jnp.asarray(np.kron(np.eye(2), np.full((64, 64), 1.0 / 64)), BF16)
    full = _resident
    out_shape = (
        jax.ShapeDtypeStruct((b, s, d), F32),
        jax.ShapeDtypeStruct((b, 512, s), BF16),
        jax.ShapeDtypeStruct((b, 2, 2, ns, V_ROWS, tm), BF16),
        jax.ShapeDtypeStruct((b, 32, s), F32),
        jax.ShapeDtypeStruct((b, 256, s), BF16),
        jax.ShapeDtypeStruct((b, 2, 2, s, 128), BF16),
        jax.ShapeDtypeStruct((b, 2, s, 128), F32),
        jax.ShapeDtypeStruct((b, s, 1024), F32),
    )
    out_specs = (
        pl.BlockSpec((None, tm, d), lambda bi, i: (bi, i, 0)),
        pl.BlockSpec((None, 512, tm), lambda bi, i: (bi, 0, i)),
        pl.BlockSpec((None, 2, 2, None, V_ROWS, tm), lambda bi, i: (bi, 0, 0, i, 0, 0)),
        pl.BlockSpec((None, 32, tm), lambda bi, i: (bi, 0, i)),
        pl.BlockSpec((None, 256, tm), lambda bi, i: (bi, 0, i)),
        pl.BlockSpec((None, 2, 2, tm, 128), lambda bi, i: (bi, 0, 0, i, 0)),
        pl.BlockSpec((None, 2, tm, 128), lambda bi, i: (bi, 0, i, 0)),
        pl.BlockSpec((None, tm, 1024), lambda bi, i: (bi, i, 0)),
    )
    in_specs = [
        pl.BlockSpec((None, tm, d), lambda bi, i: (bi, i, 0)),
        full((1, d)), full(wg.shape), full(wu.shape), full(wd.shape),
        full((1, d)), full(wt.shape), full(wn.shape),
        full((64, 1)), full((1, 128)), full((64, 1)),
        pl.BlockSpec((ROT_HALF, tm), lambda bi, i: (0, i)),
        pl.BlockSpec((ROT_HALF, tm), lambda bi, i: (0, i)),
        pl.BlockSpec((tm, 128), lambda bi, i: (i, 0)),
        pl.BlockSpec((tm, 128), lambda bi, i: (i, 0)),
        pl.BlockSpec((tm, 128), lambda bi, i: (i, 0)),
        full((128, 128)),
    ]
    return pl.pallas_call(
        _proj_body, grid=(b, ns), in_specs=in_specs, out_specs=out_specs, out_shape=out_shape,
        scratch_shapes=[pltpu.VMEM((tm, wg.shape[1]), BF16)],
        compiler_params=_cparams(("parallel", "parallel")), name="ffn1_mix_projection",
    )(x3d, ffn_gain.reshape(1, d), wg, wu, wd, mix_gain.reshape(1, d), wt, wn, q_gain.reshape(64, 1),
      jnp.tile(k_gain.reshape(1, 64), (1, 2)), mq_gain.reshape(64, 1), cos_t, sin_t, cn, sa, sb, bd)


def _rope_tables(s):
    pos = np.arange(s, dtype=np.float64)
    inv = ROPE_THETA ** (-(np.arange(0, ROT_DIM, 2, dtype=np.float64) / ROT_DIM))
    ang = pos[:, None] * inv[None, :]
    cos, sin = np.cos(ang), np.sin(ang)
    zeros = np.zeros((s, 64 - ROT_DIM))
    cn = np.concatenate([cos, cos, np.ones((s, 64 - ROT_DIM))], axis=1)
    sa = np.concatenate([-sin, np.zeros((s, ROT_HALF)), zeros], axis=1)
    sb = np.concatenate([np.zeros((s, ROT_HALF)), sin, zeros], axis=1)
    tile2 = lambda a: np.concatenate([a, a], axis=1)
    return tuple(jnp.asarray(a, F32) for a in (cos.T, sin.T, tile2(cn), tile2(sa), tile2(sb)))


def _cmp_body(kvc_ref, pos_ref, w1_ref, w2k_ref, w2vt_ref, kg_ref, kc_ref, vct_ref):
    nc = kvc_ref.shape[1] // CMP_STRIDE
    for kind in range(2):
        halves = []
        for part in range(2):
            x = jnp.concatenate(
                [(kvc_ref[kind, pl.ds(r, nc, stride=CMP_STRIDE), :]
                  + pos_ref[kind, part, :, r * 128:(r + 1) * 128]).astype(BF16) for r in range(CMP_STRIDE)],
                axis=1)
            halves.append(x)
        for g in range(NSA_GROUPS):
            second = _dot(halves[1], w1_ref[kind, g, 1])
            hid = _silu(_dot(halves[0], w1_ref[kind, g, 0]) + pltpu.roll(second, nc - 1, 0)).astype(BF16)
            if kind == 0:
                kc_ref[g] = _rms_rows(_dot(hid, w2k_ref[...]), kg_ref[...]).astype(BF16)
            else:
                vct_ref[g] = _dot_nt(w2vt_ref[...], hid).astype(BF16)


def _compress(kvc, pos, w1, w2k, w2vt, k_gain):
    b, _, s, _ = kvc.shape
    nc = s // CMP_STRIDE
    return pl.pallas_call(
        _cmp_body, grid=(b,),
        in_specs=[pl.BlockSpec((None, 2, s, 128), lambda bi: (bi, 0, 0, 0)),
                  _resident(pos.shape), _resident(w1.shape), _resident(w2k.shape), _resident(w2vt.shape),
                  _resident((1, 64))],
        out_specs=(pl.BlockSpec((None, 2, nc, 64), lambda bi: (bi, 0, 0, 0)),
                   pl.BlockSpec((None, 2, 64, nc), lambda bi: (bi, 0, 0, 0))),
        out_shape=(jax.ShapeDtypeStruct((b, 2, nc, 64), BF16), jax.ShapeDtypeStruct((b, 2, 64, nc), BF16)),
        compiler_params=_cparams(("parallel",)), name="nsa_compress",
    )(kvc, pos, w1, w2k, w2vt, k_gain.reshape(1, 64))


def _compress_weights(pos_k, pos_v, w1_k, w1_v):
    def pos_part(p):
        p = p.reshape(2, CMP_STRIDE, 1, 64)
        return jnp.broadcast_to(p, (2, CMP_STRIDE, NSA_GROUPS, 64)).reshape(2, 1, CMP_STRIDE * 128)

    def w1_part(w):
        hdim = w.shape[1]
        w = w.astype(BF16).reshape(1, 2, CMP_STRIDE, 1, 64, hdim)
        own_group = jnp.eye(NSA_GROUPS, dtype=BF16).reshape(NSA_GROUPS, 1, 1, NSA_GROUPS, 1, 1)
        return (w * own_group).reshape(NSA_GROUPS, 2, CMP_STRIDE * 128, hdim)

    pos = jnp.stack([pos_part(pos_k), pos_part(pos_v)])
    w1 = jnp.stack([w1_part(w1_k), w1_part(w1_v)])
    return pos, w1


CMP_CLASS_ROWS = 128


def _cmpsel_variant(nc, nblk, fixed_reference, m0_ref, qt_ref, kc_ref, vct_ref, oc_ref, bias_ref, s_scr, flag_scr):
    tq = qt_ref.shape[1]
    t0 = pl.program_id(2) * tq
    n_idx = lax.broadcasted_iota(jnp.int32, (nc, tq), 0)
    t_idx = t0 + lax.broadcasted_iota(jnp.int32, (nc, tq), 1)
    mask_bias = jnp.where(n_idx * CMP_STRIDE + (CMP_BLOCK - 1) <= t_idx, 0.0, NEG)
    sees_any = t0 + lax.broadcasted_iota(jnp.int32, (1, tq), 1) >= CMP_BLOCK - 1
    kc = kc_ref[0:nc, :]
    if fixed_reference:
        mask_bias = mask_bias - m0_ref[0]
    else:
        for hh in range(NSA_HPG):
            s_scr[hh, 0:nc, :] = _dot(kc, qt_ref[hh * 64:(hh + 1) * 64, :]) + mask_bias
    jj = lax.broadcasted_iota(jnp.int32, (nblk, nc), 0)
    nn = lax.broadcasted_iota(jnp.int32, (nblk, nc), 1)
    ov = jnp.where((nn * CMP_STRIDE < jj * SLC_BLOCK + SLC_BLOCK)
                   & (nn * CMP_STRIDE + CMP_BLOCK > jj * SLC_BLOCK), 1.0, 0.0).astype(BF16)
    ones_rows = (lax.broadcasted_iota(jnp.int32, (V_ROWS - 64, nc), 0) == 0).astype(BF16)
    lhs = jnp.concatenate([vct_ref[:, 0:nc], ones_rows, ov], axis=0)
    imp = jnp.zeros((nblk, tq), F32)
    l_min = jnp.full((1, tq), 1.0, F32)

    def probabilities(hh):
        if fixed_reference:
            return jnp.exp2(_dot(kc, qt_ref[hh * 64:(hh + 1) * 64, :]) + mask_bias).astype(BF16)
        m = jnp.max(s_scr[hh, 0:nc, :], axis=0, keepdims=True)
        return jnp.exp2(s_scr[hh, 0:nc, :] - m).astype(BF16)

    p_next = probabilities(0)
    for hh in range(NSA_HPG):
        p = p_next
        if hh + 1 < NSA_HPG:
            p_next = probabilities(hh + 1)
        r = _dot(lhs, p)
        l_min = jnp.minimum(l_min, jnp.where(sees_any, r[64:65], 1.0))
        inv_l = jnp.where(sees_any, 1.0 / r[64:65], 0.0)
        oc_ref[hh * 64:(hh + 1) * 64, :] = r[0:64] * inv_l
        imp = imp + r[V_ROWS:] * inv_l
    if fixed_reference:
        flag_scr[0] = jnp.where(jnp.min(l_min) > MIN_DENOMINATOR, 0, 1)

    j = lax.broadcasted_iota(jnp.int32, (nblk, tq), 0)
    cur = (t0 + lax.broadcasted_iota(jnp.int32, (nblk, tq), 1)) >> SLC_SHIFT
    forced = (j == 0) | (j == cur) | (j == cur - 1)
    picks = SLC_TOPK - 3
    forced_bias = jnp.where(forced & (j <= cur), 0.0, NEG)
    imp = jnp.where((j <= cur) & jnp.logical_not(forced), imp, -1.0)
    if nblk < bias_ref.shape[0]:
        bias_ref[nblk:, :] = jnp.full((bias_ref.shape[0] - nblk, tq), NEG, F32)

    rest = imp
    for _ in range(picks):
        v = jnp.max(rest, axis=0, keepdims=True)
        rest = jnp.where(rest == v, -3e38, rest)
    chosen = (imp >= v) & (imp >= 0.0)
    count = jnp.sum(jnp.where(chosen, 1.0, 0.0), axis=0, keepdims=True)
    valid = jnp.sum(jnp.where(imp >= 0.0, 1.0, 0.0), axis=0, keepdims=True)
    bias_ref[0:nblk, :] = jnp.where(chosen, 0.0, forced_bias)
    has_tie = jnp.max(jnp.abs(count - jnp.minimum(valid, float(picks)))) > 0.0

    @pl.when(has_tie)
    def _():
        jf = j.astype(F32)
        bias, rest = forced_bias, imp
        for _ in range(picks):
            v = jnp.max(rest, axis=0, keepdims=True)
            first = jnp.min(jnp.where(rest == v, jf, float(nblk)), axis=0, keepdims=True)
            pick = jf == first
            bias = jnp.where(pick & (v >= 0.0), 0.0, bias)
            rest = jnp.where(pick, -3e38, rest)
        bias_ref[0:nblk, :] = bias


def _cmpsel_body(m0_ref, qt_ref, kc_ref, vct_ref, oc_ref, bias_ref, s_scr, flag_scr):
    tq = qt_ref.shape[1]
    nc_total = kc_ref.shape[0]
    tiles_per_class = CMP_CLASS_ROWS // (tq // CMP_STRIDE)
    cls = pl.program_id(2) // tiles_per_class
    refs = (m0_ref, qt_ref, kc_ref, vct_ref, oc_ref, bias_ref, s_scr, flag_scr)
    for c in range(nc_total // CMP_CLASS_ROWS):
        nc = (c + 1) * CMP_CLASS_ROWS
        nblk = min(nc * CMP_STRIDE // SLC_BLOCK, bias_ref.shape[0])
        pl.when(cls == c)(functools.partial(_cmpsel_variant, nc, nblk, True, *refs))
    pl.when(flag_scr[0] != 0)(functools.partial(_cmpsel_variant, nc_total, bias_ref.shape[0], False, *refs))


def _cmp_select(qt, kc, vct, score_bound, *, tq=TOK_TILE):
    b, _, s = qt.shape
    nc = kc.shape[2]
    nblk = MAX_BLOCKS
    assert s // SLC_BLOCK <= MAX_BLOCKS and s // SLC_BLOCK >= SLC_TOPK and nc % CMP_CLASS_ROWS == 0
    return pl.pallas_call(
        _cmpsel_body, grid=(b, NSA_GROUPS, s // tq),
        in_specs=[pl.BlockSpec(memory_space=pltpu.SMEM),
                  pl.BlockSpec((None, 256, tq), lambda bi, g, i: (bi, g, i)),
                  pl.BlockSpec((None, None, nc, 64), lambda bi, g, i: (bi, g, 0, 0)),
                  pl.BlockSpec((None, None, 64, nc), lambda bi, g, i: (bi, g, 0, 0))],
        out_specs=(pl.BlockSpec((None, 256, tq), lambda bi, g, i: (bi, g, i)),
                   pl.BlockSpec((None, None, nblk, tq), lambda bi, g, i: (bi, g, 0, i))),
        out_shape=(jax.ShapeDtypeStruct((b, 512, s), F32),
                   jax.ShapeDtypeStruct((b, NSA_GROUPS, nblk, s), F32)),
        scratch_shapes=[pltpu.VMEM((NSA_HPG, nc, tq), F32), pltpu.SMEM((1,), jnp.int32)],
        compiler_params=_cparams(("parallel", "parallel", "parallel")), name="nsa_compressed_select",
    )(score_bound, qt, kc, vct)


def _flash_step(s_ref, vt, m_ref, acc_ref):
    m_old = m_ref[...]
    m_new = jnp.maximum(m_old, jnp.max(s_ref[...], axis=0, keepdims=True))
    p = jnp.exp2(s_ref[...] - m_new)
    acc_ref[...] = jnp.exp2(m_old - m_new) * acc_ref[...] + _dot(vt, p.astype(BF16))
    m_ref[...] = m_new


SEL, WIN = 0, 1
BIAS_ROWS = 16


def _slcwin_body(m0_ref, qt_ref, bias_ref, kaug_ref, vt_ref, oc_ref, gt_ref, mb_ref, y_ref,
                 q_scr, m_scr, acc_scr, s_scr, p_scr):
    tq = qt_ref.shape[1]
    tk = vt_ref.shape[3]
    assert tq == tk and WINDOW == tk and tk == 8 * SLC_BLOCK
    diag = pl.program_id(2)
    m0 = m0_ref[0]

    sel_slots = (0, 1, 2, 3)
    qs, qw = sel_slots[0], 4
    zeros = jnp.zeros((64, tq), BF16)
    for slot in sel_slots + (qw,):
        for hh in range(NSA_HPG):
            q_scr[slot, hh, 0:64, :] = qt_ref[hh * 64:(hh + 1) * 64, :]
            q_scr[slot, hh, 64:128, :] = zeros

    def set_selection_bias(kt, slot=qs):
        rows = bias_ref[pl.ds(pl.multiple_of(kt * 8, 8), 8), :]
        b16 = jnp.concatenate([rows, jnp.zeros_like(rows)], axis=0).astype(BF16)
        for hh in range(NSA_HPG):
            q_scr[slot, hh, 64:64 + BIAS_ROWS, :] = b16

    def tiles_fixed_reference(tiles):
        chains = [(br, qslot, kt, mask_bias, hh) for br, qslot, kt, mask_bias in tiles for hh in range(NSA_HPG)]
        for c in range(len(chains) + 1):
            if c < len(chains):
                br, qslot, kt, mask_bias, hh = chains[c]
                s = _dot(kaug_ref[br, pl.ds(pl.multiple_of(kt * tk, tk), tk), :], q_scr[qslot, hh])
                if mask_bias is not None:
                    s = s + mask_bias()
                p_scr[c % 4] = jnp.exp2(s - m0).astype(BF16)
            if c >= 1:
                br, _, kt, _, hh = chains[c - 1]
                acc_scr[br, hh] = acc_scr[br, hh] + _dot(vt_ref[br, kt], p_scr[(c - 1) % 4])

    def tile_running_max(br, qslot, kt, mask_bias=None):
        k = kaug_ref[br, pl.ds(pl.multiple_of(kt * tk, tk), tk), :]
        for hh in range(NSA_HPG):
            s = _dot(k, q_scr[qslot, hh])
            s_scr[hh] = s if mask_bias is None else s + mask_bias()
        for hh in range(NSA_HPG):
            _flash_step(s_scr.at[hh], vt_ref[br, kt], m_scr.at[br, hh], acc_scr.at[br, hh])

    prev = jnp.maximum(diag - 1, 0)
    no_prev = jnp.where(diag == 0, NEG, 0.0)
    band_bias = lambda: mb_ref[1] + no_prev
    causal_bias = lambda: mb_ref[0]

    acc_scr[...] = jnp.zeros(acc_scr.shape, F32)

    def unmasked_run(first_tile, count):
        for n in range(count):
            set_selection_bias(first_tile + n, sel_slots[n])
        tiles_fixed_reference([(SEL, sel_slots[n], first_tile + n, None) for n in range(count)])

    def tile_quad(j, carry):
        unmasked_run(4 * j, 4)
        return carry

    lax.fori_loop(0, diag >> 2, tile_quad, 0)
    pl.when((diag & 2) != 0)(lambda: unmasked_run((diag >> 2) * 4, 2))
    pl.when((diag & 1) != 0)(lambda: unmasked_run(diag - 1, 1))

    set_selection_bias(diag, qs)
    tiles_fixed_reference([(WIN, qw, prev, band_bias), (SEL, qs, diag, causal_bias), (WIN, qw, diag, causal_bias)])

    denominators = acc_scr[:, :, 64:65, :]
    underflow = jnp.logical_not(jnp.min(denominators) > MIN_DENOMINATOR)

    @pl.when(underflow)
    def _():
        m_scr[...] = jnp.full(m_scr.shape, NEG, F32)
        acc_scr[...] = jnp.zeros(acc_scr.shape, F32)

        def full_tile(kt, carry):
            set_selection_bias(kt, qs)
            tile_running_max(SEL, qs, kt)
            return carry

        lax.fori_loop(0, diag, full_tile, 0)
        set_selection_bias(diag, qs)
        tile_running_max(WIN, qw, prev, band_bias)
        tile_running_max(SEL, qs, diag, causal_bias)
        tile_running_max(WIN, qw, diag, causal_bias)

    gt = gt_ref[...]
    for hh in range(NSA_HPG):
        o_s = acc_scr[SEL, hh, 0:64, :] * (1.0 / acc_scr[SEL, hh, 64:65, :])
        o_w = acc_scr[WIN, hh, 0:64, :] * (1.0 / acc_scr[WIN, hh, 64:65, :])
        y_ref[hh * 64:(hh + 1) * 64, :] = (gt[3 * hh:3 * hh + 1] * oc_ref[hh * 64:(hh + 1) * 64, :]
                                          + gt[3 * hh + 1:3 * hh + 2] * o_s
                                          + gt[3 * hh + 2:3 * hh + 3] * o_w)


def _score_bound(q_gain, k_gain):
    bound = HEAD_DIM * QK_SCALE_LOG2 * jnp.max(jnp.abs(q_gain)) * jnp.max(jnp.abs(k_gain))
    return (1.02 * bound).reshape(1).astype(F32)


def _slc_win(qt, bias, kaug, vt, oc, gt, score_bound):
    b, _, s = qt.shape
    nblk = bias.shape[2]
    ns, tk = vt.shape[3], vt.shape[5]
    tq = tk
    key_rel, t_rel = np.arange(tk)[:, None], np.arange(tq)[None, :]
    mask_bias = jnp.asarray(np.stack([np.where(key_rel <= t_rel, 0.0, NEG),
                                      np.where(t_rel + tk - key_rel < WINDOW, 0.0, NEG)]), F32)
    qblk = pl.BlockSpec((None, 256, tq), lambda bi, g, i: (bi, g, i))
    return pl.pallas_call(
        _slcwin_body, grid=(b, NSA_GROUPS, s // tq),
        in_specs=[pl.BlockSpec(memory_space=pltpu.SMEM), qblk,
                  pl.BlockSpec((None, None, nblk, tq), lambda bi, g, i: (bi, g, 0, i)),
                  pl.BlockSpec((None, None, 2, s, 128), lambda bi, g, i: (bi, g, 0, 0, 0)),
                  pl.BlockSpec((None, None, 2, ns, V_ROWS, tk), lambda bi, g, i: (bi, g, 0, 0, 0, 0)),
                  qblk,
                  pl.BlockSpec((None, None, GATE_ROWS, tq), lambda bi, g, i: (bi, g, 0, i)),
                  _resident((2, tk, tq))],
        out_specs=qblk,
        out_shape=jax.ShapeDtypeStruct((b, 512, s), F32),
        scratch_shapes=[pltpu.VMEM((5, NSA_HPG, 128, tq), BF16),
                        pltpu.VMEM((2, NSA_HPG, 1, tq), F32), pltpu.VMEM((2, NSA_HPG, V_ROWS, tq), F32),
                        pltpu.VMEM((NSA_HPG, tk, tq), F32), pltpu.VMEM((NSA_HPG, tk, tq), BF16)],
        compiler_params=_cparams(("parallel", "parallel", "arbitrary")), name="nsa_selected_window",
    )(score_bound, qt, bias, kaug, vt, oc, gt.reshape(b, NSA_GROUPS, GATE_ROWS, s), mask_bias)


def _hgrn_consts():
    c = HGRN_CHUNK
    t = np.arange(c)
    lower = (t[None, :] <= t[:, None]).astype(np.float32)
    rows = [lower]
    masks = []
    for half in HGRN_LEVELS:
        mid = (t // (2 * half)) * (2 * half) + half - 1
        if half < 8:
            rows.append(lower[mid])
        same = (t[:, None] // (2 * half)) == (t[None, :] // (2 * half))
        right = (t[:, None] & half) != 0
        left = (t[None, :] & half) == 0
        masks.append((same & right & left).astype(np.float32))
    masks.append(np.eye(c, dtype=np.float32))
    masks.append(((t[:, None] // HGRN_SUB == t[None, :] // HGRN_SUB) & (t[None, :] <= t[:, None])).astype(np.float32))
    mall = np.concatenate(rows, axis=0)
    lvl = np.stack([np.tile(mk.T, (1, HGRN_HEADS)) for mk in masks])
    bdm = np.kron(np.eye(HGRN_HEADS), np.ones((64, 64), np.float32))
    return jnp.asarray(mall, BF16), jnp.asarray(lvl, F32), jnp.asarray(bdm, F32), jnp.asarray(bdm / 64, BF16)


def _hgrn_body(hg_ref, lb_ref, og_ref, mall_ref, lvl_ref, bdm_ref, bdn_ref, y_ref, st_scr, attn_scr):
    c = HGRN_CHUNK
    w = HGRN_WIDTH

    @pl.when(pl.program_id(1) == 0)
    def _():
        st_scr[...] = jnp.zeros(st_scr.shape, F32)

    chunks = range(hg_ref.shape[0] // c)
    lb = lb_ref[...]
    lane = lax.broadcasted_iota(jnp.int32, (c, w), 1)
    head_masks = [(lane >> 6) == hh for hh in range(HGRN_HEADS)]
    nlev = len(HGRN_LEVELS)

    def stack_heads(x):
        x16 = x.astype(BF16)
        return jnp.concatenate([jnp.where(hm, x16, 0) for hm in head_masks], axis=0)

    def row_bcast(x, half):
        return jnp.concatenate([jnp.broadcast_to(x[p + half - 1:p + half, :], (2 * half, w))
                                for p in range(0, c, 2 * half)], axis=0)

    qa, kk, v16, logf = [], [], [], []
    for ci in chunks:
        rows = slice(ci * c, (ci + 1) * c)
        qa.append(_silu(hg_ref[rows, 0:w]) * (HEAD_DIM ** -0.5))
        fg = lb + (1.0 - lb) * _sigmoid(hg_ref[rows, w:2 * w])
        kk.append(1.0 - fg)
        logf.append(jnp.log2(fg))
        v16.append(hg_ref[rows, 2 * w:3 * w].astype(BF16))

    mall = mall_ref[...]
    parts = _split3(jnp.concatenate(logf, axis=1))
    b_all = sum(_dot(mall[0:c], part) for part in parts)
    bcum = [b_all[:, ci * w:(ci + 1) * w] for ci in chunks]

    def level(ci, li, ref_pt):
        e = jnp.exp2(-jnp.abs(bcum[ci] - ref_pt))
        return lvl_ref[li] * _dot_nt((kk[ci] * e).astype(BF16), stack_heads(qa[ci] * e))

    coarse = [li for li, half in enumerate(HGRN_LEVELS) if half >= HGRN_SUB]
    if not coarse:
        attn_scr[...] = jnp.zeros(attn_scr.shape, F32)
    for li in coarse:
        for ci in chunks:
            contribution = level(ci, li, row_bcast(bcum[ci], HGRN_LEVELS[li]))
            attn_scr[ci] = contribution if li == coarse[0] else attn_scr[ci] + contribution

    def block_start(x):
        firsts = [jnp.zeros((HGRN_SUB, w), F32)]
        firsts += [jnp.broadcast_to(x[p - 1:p, :], (HGRN_SUB, w)) for p in range(HGRN_SUB, c, HGRN_SUB)]
        return jnp.concatenate(firsts, axis=0)

    expo = [block_start(bcum[ci]) - bcum[ci] for ci in chunks]
    largest = expo[0]
    for ci in chunks[1:]:
        largest = jnp.maximum(largest, expo[ci])
    single_reference_ok = jnp.max(largest) < HGRN_MAX_EXPONENT

    @pl.when(single_reference_ok)
    def _():
        for ci in chunks:
            kt = (kk[ci] * jnp.exp2(expo[ci])).astype(BF16)
            attn_scr[ci] = attn_scr[ci] + lvl_ref[nlev + 1] * _dot_nt(kt, stack_heads(qa[ci] * jnp.exp2(-expo[ci])))

    @pl.when(jnp.logical_not(single_reference_ok))
    def _():
        r_fine = sum(_dot(mall[c:], part) for part in parts)
        for ci in chunks:
            attn_scr[ci] = attn_scr[ci] + lvl_ref[nlev] * _dot_nt(kk[ci].astype(BF16), stack_heads(qa[ci]))
        fine = 0
        for li, half in enumerate(HGRN_LEVELS):
            if half >= HGRN_SUB:
                continue
            for ci in chunks:
                if half >= 8:
                    ref_pt = row_bcast(bcum[ci], half)
                else:
                    ref_pt = r_fine[fine * c:(fine + 1) * c, ci * w:(ci + 1) * w]
                attn_scr[ci] = attn_scr[ci] + level(ci, li, ref_pt)
            if half < 8:
                fine += 1

    attn = [attn_scr[ci] for ci in chunks]

    intra, upd, decay, qb = [], [], [], []
    for ci in chunks:
        x = _dot_tn(attn[ci].astype(BF16), v16[ci])
        intra.append(sum(jnp.where(head_masks[hh], x[hh * c:(hh + 1) * c], 0.0) for hh in range(HGRN_HEADS)))
        b_last = bcum[ci][c - 1:c, :]
        kl = (kk[ci] * jnp.exp2(b_last - bcum[ci])).astype(BF16)
        upd.append(bdm_ref[...] * _dot_tn(v16[ci], kl))
        decay.append(jnp.exp2(b_last))
        qb.append((qa[ci] * jnp.exp2(bcum[ci])).astype(BF16))

    st = st_scr[...]
    inter = []
    for ci in chunks:
        inter.append(_dot_nt(qb[ci], st.astype(BF16)))
        st = st * decay[ci] + upd[ci]
    st_scr[...] = st

    for ci in chunks:
        rows = slice(ci * c, (ci + 1) * c)
        o = inter[ci] + intra[ci]
        hi, lo = _split2(o * o)
        ms = _dot(hi, bdn_ref[...]) + _dot(lo, bdn_ref[...])
        y_ref[rows, :] = o * lax.rsqrt(ms + EPS) * og_ref[...] * _silu(hg_ref[rows, 3 * w:4 * w])


def _hgrn(hg, lower_bound, out_gain):
    b, s, _ = hg.shape
    rows = HGRN_CB * HGRN_CHUNK
    mall, lvl, bdm, bdn = _hgrn_consts()
    full = lambda shape: pl.BlockSpec(shape, lambda bi, i: (0,) * len(shape))
    return pl.pallas_call(
        _hgrn_body, grid=(b, s // rows),
        in_specs=[pl.BlockSpec((None, rows, 4 * HGRN_WIDTH), lambda bi, i: (bi, i, 0)),
                  full((1, HGRN_WIDTH)), full((1, HGRN_WIDTH)),
                  full(mall.shape), full(lvl.shape), full(bdm.shape), full(bdn.shape)],
        out_specs=pl.BlockSpec((None, rows, HGRN_WIDTH), lambda bi, i: (bi, i, 0)),
        out_shape=jax.ShapeDtypeStruct((b, s, HGRN_WIDTH), F32),
        scratch_shapes=[pltpu.VMEM((HGRN_WIDTH, HGRN_WIDTH), F32),
                        pltpu.VMEM((HGRN_CB, HGRN_CHUNK, HGRN_HEADS * HGRN_CHUNK), F32)],
        compiler_params=_cparams(("parallel", "arbitrary")), name="hgrn2_chunks",
    )(hg, lower_bound.reshape(1, -1), out_gain.reshape(1, -1), mall, lvl, bdm, bdn)


def _memkv_body(mem_ref, mg_ref, wk_ref, wvt_ref, kg_ref, kh_ref, vht_ref):
    m = mem_ref.shape[0]
    mn = _rms_rows(mem_ref[...], mg_ref[...]).astype(BF16)
    k = _dot(mn, wk_ref[...])
    vt = _dot_nt(wvt_ref[...], mn)
    ones_rows = (lax.broadcasted_iota(jnp.int32, (V_ROWS - 64, m), 0) == 0).astype(BF16)
    for hh in range(MEM_HEADS):
        kh_ref[hh] = _rms_rows(k[:, hh * 64:(hh + 1) * 64], kg_ref[...]).astype(BF16)
        vht_ref[hh, 0:64, :] = vt[hh * 64:(hh + 1) * 64].astype(BF16)
        vht_ref[hh, 64:V_ROWS, :] = ones_rows


def _mem_kv(mem, mem_gain, wk, wvt, k_gain):
    b, m, d = mem.shape
    full = lambda shape: pl.BlockSpec(shape, lambda bi: (0,) * len(shape))
    return pl.pallas_call(
        _memkv_body, grid=(b,),
        in_specs=[pl.BlockSpec((None, m, d), lambda bi: (bi, 0, 0)), full((1, d)),
                  full(wk.shape), full(wvt.shape), full((1, 64))],
        out_specs=(pl.BlockSpec((None, MEM_HEADS, m, 64), lambda bi: (bi, 0, 0, 0)),
                   pl.BlockSpec((None, MEM_HEADS, V_ROWS, m), lambda bi: (bi, 0, 0, 0))),
        out_shape=(jax.ShapeDtypeStruct((b, MEM_HEADS, m, 64), BF16),
                   jax.ShapeDtypeStruct((b, MEM_HEADS, V_ROWS, m), BF16)),
        compiler_params=_cparams(("parallel",)), name="memory_kv",
    )(mem, mem_gain.reshape(1, d), wk, wvt, k_gain.reshape(1, 64))


def _out_body(x_ref, ynt_ref, yh_ref, qmt_ref, kh_ref, vht_ref, ng_ref, mg_ref, wo_ref,
              fg_ref, wg_ref, wu_ref, wd_ref, o_ref, a_scr):
    scores = [_dot(kh_ref[hh], qmt_ref[hh * 64:(hh + 1) * 64, :]) for hh in range(MEM_HEADS)]
    nsa = _rms_cols(ynt_ref[...], ng_ref[...]).astype(BF16)
    acc = _dot_tn(nsa, wo_ref[0:NSA_WIDTH, :])
    acc = acc + _dot(yh_ref[...].astype(BF16), wo_ref[NSA_WIDTH:NSA_WIDTH + HGRN_WIDTH, :])
    y_mem = []
    for s in scores:
        hh = len(y_mem)
        p = jnp.exp2(s - jnp.max(s, axis=0, keepdims=True))
        o = _dot(vht_ref[hh], p.astype(BF16))
        y_mem.append(o[0:64] * (1.0 / o[64:65]))
    mem = _rms_cols(jnp.concatenate(y_mem, axis=0), mg_ref[...]).astype(BF16)
    acc = acc + _dot_tn(mem, wo_ref[NSA_WIDTH + HGRN_WIDTH:, :])
    o_ref[...] = _ffn_half_step(x_ref[...] + acc, fg_ref, wg_ref, wu_ref, wd_ref, a_scr)


def _out_ffn(x3d, ynt, yh, qmt, kh, vht, nsa_gain, mem_gain, wo, ffn_gain, wg, wu, wd, *, tm=TOK_TILE):
    b, s, d = x3d.shape
    m = kh.shape[2]
    return pl.pallas_call(
        _out_body, grid=(b, s // tm),
        in_specs=[pl.BlockSpec((None, tm, d), lambda bi, i: (bi, i, 0)),
                  pl.BlockSpec((None, NSA_WIDTH, tm), lambda bi, i: (bi, 0, i)),
                  pl.BlockSpec((None, tm, HGRN_WIDTH), lambda bi, i: (bi, i, 0)),
                  pl.BlockSpec((None, MEM_WIDTH, tm), lambda bi, i: (bi, 0, i)),
                  pl.BlockSpec((None, MEM_HEADS, m, 64), lambda bi, i: (bi, 0, 0, 0)),
                  pl.BlockSpec((None, MEM_HEADS, V_ROWS, m), lambda bi, i: (bi, 0, 0, 0)),
                  _resident((NSA_WIDTH, 1)), _resident((MEM_WIDTH, 1)), _resident(wo.shape),
                  _resident((1, d)), _resident(wg.shape), _resident(wu.shape), _resident(wd.shape)],
        out_specs=pl.BlockSpec((None, tm, d), lambda bi, i: (bi, i, 0)),
        out_shape=jax.ShapeDtypeStruct((b, s, d), F32),
        scratch_shapes=[pltpu.VMEM((tm, wg.shape[1]), BF16)],
        compiler_params=_cparams(("parallel", "parallel")), name="mix_out_ffn2",
    )(x3d, ynt, yh, qmt, kh, vht, nsa_gain.reshape(-1, 1), mem_gain.reshape(-1, 1), wo,
      ffn_gain.reshape(1, d), wg, wu, wd)


def _layer(x, mem, ffn1, ffn2, mix_norm, w_in, w_out, nsa_q_norm, nsa_k_norm, cmp_pos_k, cmp_w1_k, cmp_w2_k,
           cmp_pos_v, cmp_w1_v, cmp_w2_v, nsa_out_norm, lower_bound, hgrn_out_norm,
           mem_norm, mem_w_k, mem_w_v, mem_q_norm, mem_k_norm, mem_out_norm):
    b, s, d = x.shape
    sizes = (512, 128, 128, 128, 128, 128, 128, 24, 256, 256, 256, 256, 256)
    offs = np.concatenate([[0], np.cumsum(sizes)])
    w16 = w_in.astype(BF16)
    col = lambda i: w16[:, offs[i]:offs[i + 1]]
    (q_a, k_c, v_c, k_s, v_s, k_w, v_w, g_a, q_h, f_h, i_h, g_h, q_m) = [col(i) for i in range(13)]
    gpad = jnp.zeros((d, GATE_ROWS - 3 * NSA_HPG), BF16)
    wt = jnp.concatenate([q_a, v_s, v_w, g_a[:, :3 * NSA_HPG], gpad, g_a[:, 3 * NSA_HPG:], gpad, q_m], axis=1).T
    wn = jnp.concatenate([k_c, v_c, k_s, k_w, q_h, f_h, i_h, g_h], axis=1)

    x1, qt, vt, gt, qmt, kaug, kvc, hg = _ffn_proj(
        x, *ffn1, mix_norm, wt, wn, nsa_q_norm, nsa_k_norm, mem_q_norm, _rope_tables(s))

    cmp_pos, cmp_w1 = _compress_weights(cmp_pos_k, cmp_pos_v, cmp_w1_k, cmp_w1_v)
    kc, vct = _compress(kvc, cmp_pos, cmp_w1, cmp_w2_k.astype(BF16), cmp_w2_v.T.astype(BF16), nsa_k_norm)
    score_bound = _score_bound(nsa_q_norm, nsa_k_norm)
    oc, bias = _cmp_select(qt, kc, vct, score_bound)
    y_nsa = _slc_win(qt, bias, kaug, vt, oc, gt, score_bound)

    y_hgrn = _hgrn(hg, lower_bound, hgrn_out_norm)

    kh, vht = _mem_kv(mem, mem_norm, mem_w_k.astype(BF16), mem_w_v.T.astype(BF16), mem_k_norm)
    return _out_ffn(x1, y_nsa, y_hgrn, qmt, kh, vht, nsa_out_norm, mem_out_norm, w_out.astype(BF16), *ffn2)


def kernel(x, mem, ffn1_norm, ffn1_w_gate, ffn1_w_up, ffn1_w_down, mix_norm, w_in, w_out, nsa_q_norm, nsa_k_norm, cmp_pos_k, cmp_w1_k, cmp_w2_k, cmp_pos_v, cmp_w1_v, cmp_w2_v, nsa_out_norm, hgrn_lb_logits, hgrn_out_norm, mem_norm, mem_w_k, mem_w_v, mem_q_norm, mem_k_norm, mem_out_norm, ffn2_norm, ffn2_w_gate, ffn2_w_up, ffn2_w_down):
    b, s, d = x.shape
    depth = ffn1_norm.shape[0]
    lower_bounds = jnp.cumsum(jax.nn.softmax(hgrn_lb_logits.astype(F32), axis=0), axis=0)
    bf = lambda a: a.astype(BF16)
    for l in range(depth):
        x = _layer(x, mem, (ffn1_norm[l], bf(ffn1_w_gate[l]), bf(ffn1_w_up[l]), bf(ffn1_w_down[l])),
                   (ffn2_norm[l], bf(ffn2_w_gate[l]), bf(ffn2_w_up[l]), bf(ffn2_w_down[l])),
                   mix_norm[l], w_in[l], w_out[l], nsa_q_norm[l], nsa_k_norm[l],
                   cmp_pos_k[l], cmp_w1_k[l], cmp_w2_k[l], cmp_pos_v[l], cmp_w1_v[l], cmp_w2_v[l],
                   nsa_out_norm[l], lower_bounds[l], hgrn_out_norm[l],
                   mem_norm[l], mem_w_k[l], mem_w_v[l], mem_q_norm[l], mem_k_norm[l], mem_out_norm[l])
    return x
```

```python
import functools

import numpy as np
import jax
import jax.numpy as jnp
from jax import lax
from jax.experimental import pallas as pl
from jax.experimental.pallas import tpu as pltpu

F32 = jnp.float32
BF16 = jnp.bfloat16

HEAD_DIM = 64
ROT_DIM = 16
ROT_HALF = 8
ROPE_THETA = 500000.0
NSA_HEADS = 8
NSA_GROUPS = 2
NSA_HPG = 4
CMP_BLOCK = 32
CMP_STRIDE = 16
SLC_BLOCK = 64
SLC_SHIFT = 6
SLC_TOPK = 16
WINDOW = 512
FORCED_SCORE = 1e4
HGRN_HEADS = 4
HGRN_CHUNK = 64
HGRN_WIDTH = 256
MEM_HEADS = 4
MEM_WIDTH = 256
NSA_WIDTH = 512
EPS = 1e-6
NEG = -1e30
QK_SCALE_LOG2 = HEAD_DIM ** -0.5 * 1.4426950408889634
MIN_DENOMINATOR = 2.0 ** -64

VMEM_LIMIT = 56 * 1024 * 1024
MAX_BLOCKS = 128
GATE_ROWS = 16
V_ROWS = 80
TOK_TILE = 512
HGRN_CB = 16
HGRN_LEVELS = (32, 16, 8, 4, 2, 1)
HGRN_SUB = 64
HGRN_MAX_EXPONENT = 96.0

NT_DIMS = (((1,), (1,)), ((), ()))
TN_DIMS = (((0,), (0,)), ((), ()))


def _cparams(sem):
    return pltpu.CompilerParams(dimension_semantics=sem, vmem_limit_bytes=VMEM_LIMIT)


def _dot(a, b):
    return jnp.dot(a, b, preferred_element_type=F32)


def _dot_nt(a, b):
    return lax.dot_general(a, b, NT_DIMS, preferred_element_type=F32)


def _dot_tn(a, b):
    return lax.dot_general(a, b, TN_DIMS, preferred_element_type=F32)


def _sigmoid(x):
    return 1.0 / (1.0 + jnp.exp(-x))


def _silu(x):
    return x * _sigmoid(x)


def _split2(x):
    hi = x.astype(BF16)
    lo = (x - hi.astype(F32)).astype(BF16)
    return hi, lo


def _split3(x):
    hi = x.astype(BF16)
    r1 = x - hi.astype(F32)
    mid = r1.astype(BF16)
    lo = (r1 - mid.astype(F32)).astype(BF16)
    return hi, mid, lo


def _rms_rows(x, gain_row):
    ms = jnp.mean(x * x, axis=-1, keepdims=True)
    return x * lax.rsqrt(ms + EPS) * gain_row


def _rms_cols(x, gain_col):
    ms = jnp.mean(x * x, axis=0, keepdims=True)
    return x * lax.rsqrt(ms + EPS) * gain_col


def _seg_mean_sq(x, bd):
    hi, lo = _split2(x * x)
    return _dot(hi, bd) + _dot(lo, bd)


FFN_CHUNK = 256


def _ffn_half_step(x, g_ref, wg_ref, wu_ref, wd_ref, a_scr):
    xg = (x * g_ref[...]).astype(BF16)
    inv_rms = lax.rsqrt(jnp.mean(x * x, axis=-1, keepdims=True) + EPS)
    d_ff = wg_ref.shape[1]
    for c in range(d_ff // FFN_CHUNK):
        sl = slice(c * FFN_CHUNK, (c + 1) * FFN_CHUNK)
        g = _dot(xg, wg_ref[:, sl]) * inv_rms
        u = _dot(xg, wu_ref[:, sl]) * inv_rms
        a_scr[:, sl] = (_silu(g) * u).astype(BF16)
    return x + 0.5 * _dot(a_scr[...], wd_ref[...])


def _resident(shape):
    return pl.BlockSpec(shape, lambda *_: (0,) * len(shape), pipeline_mode=pl.Buffered(1))


def _rope_cols(xn, cos, sin):
    x0, x1 = xn[0:ROT_HALF], xn[ROT_HALF:ROT_DIM]
    return jnp.concatenate([x0 * cos - x1 * sin, x1 * cos + x0 * sin, xn[ROT_DIM:]], axis=0)


def _rope_rows(x, cn, sa, sb):
    return x * cn + pltpu.roll(x, 128 - ROT_HALF, 1) * sa + pltpu.roll(x, ROT_HALF, 1) * sb


def _proj_body(x_ref, fg_ref, wg_ref, wu_ref, wd_ref, mg_ref, wt_ref, wn_ref, qg_ref, kg_ref, mqg_ref,
               cos_ref, sin_ref, cn_ref, sa_ref, sb_ref, bd_ref,
               x1_ref, qt_ref, vt_ref, gt_ref, qmt_ref, kaug_ref, kvc_ref, hg_ref, a_scr):
    tm = x_ref.shape[0]
    assert tm == 8 * SLC_BLOCK
    x1 = _ffn_half_step(x_ref[...], fg_ref, wg_ref, wu_ref, wd_ref, a_scr)
    x1_ref[...] = x1
    h = _rms_rows(x1, mg_ref[...]).astype(BF16)

    qg, mqg = qg_ref[...], mqg_ref[...]
    half = tm // 2
    ones_rows = (lax.broadcasted_iota(jnp.int32, (V_ROWS - 64, half), 0) == 0).astype(BF16)
    for part in range(2):
        tok = slice(part * half, (part + 1) * half)
        pt = _dot_nt(wt_ref[...], h[tok, :])
        cos, sin = cos_ref[:, tok], sin_ref[:, tok]
        for hh in range(NSA_HEADS):
            xq = _rms_cols(pt[hh * 64:(hh + 1) * 64], qg)
            qt_ref[hh * 64:(hh + 1) * 64, tok] = (_rope_cols(xq, cos, sin) * QK_SCALE_LOG2).astype(BF16)
        for g in range(NSA_GROUPS):
            for br in range(2):
                rows = 512 + br * 128 + g * 64
                vt_ref[g, br, 0:64, tok] = pt[rows:rows + 64].astype(BF16)
                vt_ref[g, br, 64:V_ROWS, tok] = ones_rows
        gt_ref[:, tok] = _sigmoid(pt[768:800])
        for hh in range(MEM_HEADS):
            xm = _rms_cols(pt[800 + hh * 64:864 + hh * 64], mqg)
            qmt_ref[hh * 64:(hh + 1) * 64, tok] = (xm * QK_SCALE_LOG2).astype(BF16)

    cn, sa, sb = cn_ref[...], sa_ref[...], sb_ref[...]
    bd, kg = bd_ref[...], kg_ref[...]
    pc = _dot(h, wn_ref[:, 0:256])
    pk = _dot(h, wn_ref[:, 256:512])
    kvc_ref[0] = _rope_rows(pc[:, 0:128], cn, sa, sb)
    kvc_ref[1] = pc[:, 128:256]
    ph = _dot(h, wn_ref[:, 512:1024])
    ks = pk[:, 0:128]
    kw = pk[:, 128:256]
    ks = _rope_rows(ks * lax.rsqrt(_seg_mean_sq(ks, bd) + EPS) * kg, cn, sa, sb)
    kw = _rope_rows(kw * lax.rsqrt(_seg_mean_sq(kw, bd) + EPS) * kg, cn, sa, sb)
    lane = lax.broadcasted_iota(jnp.int32, (tm, 128), 1)
    row = lax.broadcasted_iota(jnp.int32, (tm, 128), 0)
    onehot = jnp.where(lane - 64 == (row >> SLC_SHIFT), 1.0, 0.0)
    lo_half = lane < 64
    kaug_ref[0, 0] = jnp.where(lo_half, ks, onehot).astype(BF16)
    kaug_ref[0, 1] = jnp.where(lo_half, kw, 0.0).astype(BF16)
    kaug_ref[1, 0] = jnp.where(lo_half, pltpu.roll(ks, 64, 1), onehot).astype(BF16)
    kaug_ref[1, 1] = jnp.where(lo_half, pltpu.roll(kw, 64, 1), 0.0).astype(BF16)
    hg_ref[:, 0:512] = ph
    hg_ref[:, 512:1024] = _dot(h, wn_ref[:, 1024:1536])


def _ffn_proj(x3d, ffn_gain, wg, wu, wd, mix_gain, wt, wn, q_gain, k_gain, mq_gain, rope):
    b, s, d = x3d.shape
    tm = TOK_TILE
    ns = s // tm
    cos_t, sin_t, cn, sa, sb = rope
    bd = jnp.asarray(np.kron(np.eye(2), np.full((64, 64), 1.0 / 64)), BF16)
    full = _resident
    out_shape = (
        jax.ShapeDtypeStruct((b, s, d), F32),
        jax.ShapeDtypeStruct((b, 512, s), BF16),
        jax.ShapeDtypeStruct((b, 2, 2, ns, V_ROWS, tm), BF16),
        jax.ShapeDtypeStruct((b, 32, s), F32),
        jax.ShapeDtypeStruct((b, 256, s), BF16),
        jax.ShapeDtypeStruct((b, 2, 2, s, 128), BF16),
        jax.ShapeDtypeStruct((b, 2, s, 128), F32),
        jax.ShapeDtypeStruct((b, s, 1024), F32),
    )
    out_specs = (
        pl.BlockSpec((None, tm, d), lambda bi, i: (bi, i, 0)),
        pl.BlockSpec((None, 512, tm), lambda bi, i: (bi, 0, i)),
        pl.BlockSpec((None, 2, 2, None, V_ROWS, tm), lambda bi, i: (bi, 0, 0, i, 0, 0)),
        pl.BlockSpec((None, 32, tm), lambda bi, i: (bi, 0, i)),
        pl.BlockSpec((None, 256, tm), lambda bi, i: (bi, 0, i)),
        pl.BlockSpec((None, 2, 2, tm, 128), lambda bi, i: (bi, 0, 0, i, 0)),
        pl.BlockSpec((None, 2, tm, 128), lambda bi, i: (bi, 0, i, 0)),
        pl.BlockSpec((None, tm, 1024), lambda bi, i: (bi, i, 0)),
    )
    in_specs = [
        pl.BlockSpec((None, tm, d), lambda bi, i: (bi, i, 0)),
        full((1, d)), full(wg.shape), full(wu.shape), full(wd.shape),
        full((1, d)), full(wt.shape), full(wn.shape),
        full((64, 1)), full((1, 128)), full((64, 1)),
        pl.BlockSpec((ROT_HALF, tm), lambda bi, i: (0, i)),
        pl.BlockSpec((ROT_HALF, tm), lambda bi, i: (0, i)),
        pl.BlockSpec((tm, 128), lambda bi, i: (i, 0)),
        pl.BlockSpec((tm, 128), lambda bi, i: (i, 0)),
        pl.BlockSpec((tm, 128), lambda bi, i: (i, 0)),
        full((128, 128)),
    ]
    return pl.pallas_call(
        _proj_body, grid=(b, ns), in_specs=in_specs, out_specs=out_specs, out_shape=out_shape,
        scratch_shapes=[pltpu.VMEM((tm, wg.shape[1]), BF16)],
        compiler_params=_cparams(("parallel", "parallel")), name="ffn1_mix_projection",
    )(x3d, ffn_gain.reshape(1, d), wg, wu, wd, mix_gain.reshape(1, d), wt, wn, q_gain.reshape(64, 1),
      jnp.tile(k_gain.reshape(1, 64), (1, 2)), mq_gain.reshape(64, 1), cos_t, sin_t, cn, sa, sb, bd)


def _rope_tables(s):
    pos = np.arange(s, dtype=np.float64)
    inv = ROPE_THETA ** (-(np.arange(0, ROT_DIM, 2, dtype=np.float64) / ROT_DIM))
    ang = pos[:, None] * inv[None, :]
    cos, sin = np.cos(ang), np.sin(ang)
    zeros = np.zeros((s, 64 - ROT_DIM))
    cn = np.concatenate([cos, cos, np.ones((s, 64 - ROT_DIM))], axis=1)
    sa = np.concatenate([-sin, np.zeros((s, ROT_HALF)), zeros], axis=1)
    sb = np.concatenate([np.zeros((s, ROT_HALF)), sin, zeros], axis=1)
    tile2 = lambda a: np.concatenate([a, a], axis=1)
    return tuple(jnp.asarray(a, F32) for a in (cos.T, sin.T, tile2(cn), tile2(sa), tile2(sb)))


def _cmp_body(kvc_ref, pos_ref, w1_ref, w2k_ref, w2vt_ref, kg_ref, kc_ref, vct_ref):
    nc = kvc_ref.shape[1] // CMP_STRIDE
    for kind in range(2):
        halves = []
        for part in range(2):
            x = jnp.concatenate(
                [(kvc_ref[kind, pl.ds(r, nc, stride=CMP_STRIDE), :]
                  + pos_ref[kind, part, :, r * 128:(r + 1) * 128]).astype(BF16) for r in range(CMP_STRIDE)],
                axis=1)
            halves.append(x)
        for g in range(NSA_GROUPS):
            second = _dot(halves[1], w1_ref[kind, g, 1])
            hid = _silu(_dot(halves[0], w1_ref[kind, g, 0]) + pltpu.roll(second, nc - 1, 0)).astype(BF16)
            if kind == 0:
                kc_ref[g] = _rms_rows(_dot(hid, w2k_ref[...]), kg_ref[...]).astype(BF16)
            else:
                vct_ref[g] = _dot_nt(w2vt_ref[...], hid).astype(BF16)


def _compress(kvc, pos, w1, w2k, w2vt, k_gain):
    b, _, s, _ = kvc.shape
    nc = s // CMP_STRIDE
    return pl.pallas_call(
        _cmp_body, grid=(b,),
        in_specs=[pl.BlockSpec((None, 2, s, 128), lambda bi: (bi, 0, 0, 0)),
                  _resident(pos.shape), _resident(w1.shape), _resident(w2k.shape), _resident(w2vt.shape),
                  _resident((1, 64))],
        out_specs=(pl.BlockSpec((None, 2, nc, 64), lambda bi: (bi, 0, 0, 0)),
                   pl.BlockSpec((None, 2, 64, nc), lambda bi: (bi, 0, 0, 0))),
        out_shape=(jax.ShapeDtypeStruct((b, 2, nc, 64), BF16), jax.ShapeDtypeStruct((b, 2, 64, nc), BF16)),
        compiler_params=_cparams(("parallel",)), name="nsa_compress",
    )(kvc, pos, w1, w2k, w2vt, k_gain.reshape(1, 64))


def _compress_weights(pos_k, pos_v, w1_k, w1_v):
    def pos_part(p):
        p = p.reshape(2, CMP_STRIDE, 1, 64)
        return jnp.broadcast_to(p, (2, CMP_STRIDE, NSA_GROUPS, 64)).reshape(2, 1, CMP_STRIDE * 128)

    def w1_part(w):
        hdim = w.shape[1]
        w = w.astype(BF16).reshape(1, 2, CMP_STRIDE, 1, 64, hdim)
        own_group = jnp.eye(NSA_GROUPS, dtype=BF16).reshape(NSA_GROUPS, 1, 1, NSA_GROUPS, 1, 1)
        return (w * own_group).reshape(NSA_GROUPS, 2, CMP_STRIDE * 128, hdim)

    pos = jnp.stack([pos_part(pos_k), pos_part(pos_v)])
    w1 = jnp.stack([w1_part(w1_k), w1_part(w1_v)])
    return pos, w1


CMP_CLASS_ROWS = 128


def _cmpsel_variant(nc, nblk, fixed_reference, m0_ref, qt_ref, kc_ref, vct_ref, oc_ref, bias_ref, s_scr, flag_scr):
    tq = qt_ref.shape[1]
    t0 = pl.program_id(2) * tq
    n_idx = lax.broadcasted_iota(jnp.int32, (nc, tq), 0)
    t_idx = t0 + lax.broadcasted_iota(jnp.int32, (nc, tq), 1)
    mask_bias = jnp.where(n_idx * CMP_STRIDE + (CMP_BLOCK - 1) <= t_idx, 0.0, NEG)
    sees_any = t0 + lax.broadcasted_iota(jnp.int32, (1, tq), 1) >= CMP_BLOCK - 1
    kc = kc_ref[0:nc, :]
    if fixed_reference:
        mask_bias = mask_bias - m0_ref[0]
    else:
        for hh in range(NSA_HPG):
            s_scr[hh, 0:nc, :] = _dot(kc, qt_ref[hh * 64:(hh + 1) * 64, :]) + mask_bias
    jj = lax.broadcasted_iota(jnp.int32, (nblk, nc), 0)
    nn = lax.broadcasted_iota(jnp.int32, (nblk, nc), 1)
    ov = jnp.where((nn * CMP_STRIDE < jj * SLC_BLOCK + SLC_BLOCK)
                   & (nn * CMP_STRIDE + CMP_BLOCK > jj * SLC_BLOCK), 1.0, 0.0).astype(BF16)
    ones_rows = (lax.broadcasted_iota(jnp.int32, (V_ROWS - 64, nc), 0) == 0).astype(BF16)
    lhs = jnp.concatenate([vct_ref[:, 0:nc], ones_rows, ov], axis=0)
    imp = jnp.zeros((nblk, tq), F32)
    l_min = jnp.full((1, tq), 1.0, F32)

    def probabilities(hh):
        if fixed_reference:
            return jnp.exp2(_dot(kc, qt_ref[hh * 64:(hh + 1) * 64, :]) + mask_bias).astype(BF16)
        m = jnp.max(s_scr[hh, 0:nc, :], axis=0, keepdims=True)
        return jnp.exp2(s_scr[hh, 0:nc, :] - m).astype(BF16)

    p_next = probabilities(0)
    for hh in range(NSA_HPG):
        p = p_next
        if hh + 1 < NSA_HPG:
            p_next = probabilities(hh + 1)
        r = _dot(lhs, p)
        l_min = jnp.minimum(l_min, jnp.where(sees_any, r[64:65], 1.0))
        inv_l = jnp.where(sees_any, 1.0 / r[64:65], 0.0)
        oc_ref[hh * 64:(hh + 1) * 64, :] = r[0:64] * inv_l
        imp = imp + r[V_ROWS:] * inv_l
    if fixed_reference:
        flag_scr[0] = jnp.where(jnp.min(l_min) > MIN_DENOMINATOR, 0, 1)

    j = lax.broadcasted_iota(jnp.int32, (nblk, tq), 0)
    cur = (t0 + lax.broadcasted_iota(jnp.int32, (nblk, tq), 1)) >> SLC_SHIFT
    forced = (j == 0) | (j == cur) | (j == cur - 1)
    picks = SLC_TOPK - 3
    forced_bias = jnp.where(forced & (j <= cur), 0.0, NEG)
    imp = jnp.where((j <= cur) & jnp.logical_not(forced), imp, -1.0)
    if nblk < bias_ref.shape[0]:
        bias_ref[nblk:, :] = jnp.full((bias_ref.shape[0] - nblk, tq), NEG, F32)

    rest = imp
    for _ in range(picks):
        v = jnp.max(rest, axis=0, keepdims=True)
        rest = jnp.where(rest == v, -3e38, rest)
    chosen = (imp >= v) & (imp >= 0.0)
    count = jnp.sum(jnp.where(chosen, 1.0, 0.0), axis=0, keepdims=True)
    valid = jnp.sum(jnp.where(imp >= 0.0, 1.0, 0.0), axis=0, keepdims=True)
    bias_ref[0:nblk, :] = jnp.where(chosen, 0.0, forced_bias)
    has_tie = jnp.max(jnp.abs(count - jnp.minimum(valid, float(picks)))) > 0.0

    @pl.when(has_tie)
    def _():
        jf = j.astype(F32)
        bias, rest = forced_bias, imp
        for _ in range(picks):
            v = jnp.max(rest, axis=0, keepdims=True)
            first = jnp.min(jnp.where(rest == v, jf, float(nblk)), axis=0, keepdims=True)
            pick = jf == first
            bias = jnp.where(pick & (v >= 0.0), 0.0, bias)
            rest = jnp.where(pick, -3e38, rest)
        bias_ref[0:nblk, :] = bias


def _cmpsel_body(m0_ref, qt_ref, kc_ref, vct_ref, oc_ref, bias_ref, s_scr, flag_scr):
    tq = qt_ref.shape[1]
    nc_total = kc_ref.shape[0]
    tiles_per_class = CMP_CLASS_ROWS // (tq // CMP_STRIDE)
    cls = pl.program_id(2) // tiles_per_class
    refs = (m0_ref, qt_ref, kc_ref, vct_ref, oc_ref, bias_ref, s_scr, flag_scr)
    for c in range(nc_total // CMP_CLASS_ROWS):
        nc = (c + 1) * CMP_CLASS_ROWS
        nblk = min(nc * CMP_STRIDE // SLC_BLOCK, bias_ref.shape[0])
        pl.when(cls == c)(functools.partial(_cmpsel_variant, nc, nblk, True, *refs))
    pl.when(flag_scr[0] != 0)(functools.partial(_cmpsel_variant, nc_total, bias_ref.shape[0], False, *refs))


def _cmp_select(qt, kc, vct, score_bound, *, tq=TOK_TILE):
    b, _, s = qt.shape
    nc = kc.shape[2]
    nblk = MAX_BLOCKS
    assert s // SLC_BLOCK <= MAX_BLOCKS and s // SLC_BLOCK >= SLC_TOPK and nc % CMP_CLASS_ROWS == 0
    return pl.pallas_call(
        _cmpsel_body, grid=(b, NSA_GROUPS, s // tq),
        in_specs=[pl.BlockSpec(memory_space=pltpu.SMEM),
                  pl.BlockSpec((None, 256, tq), lambda bi, g, i: (bi, g, i)),
                  pl.BlockSpec((None, None, nc, 64), lambda bi, g, i: (bi, g, 0, 0)),
                  pl.BlockSpec((None, None, 64, nc), lambda bi, g, i: (bi, g, 0, 0))],
        out_specs=(pl.BlockSpec((None, 256, tq), lambda bi, g, i: (bi, g, i)),
                   pl.BlockSpec((None, None, nblk, tq), lambda bi, g, i: (bi, g, 0, i))),
        out_shape=(jax.ShapeDtypeStruct((b, 512, s), F32),
                   jax.ShapeDtypeStruct((b, NSA_GROUPS, nblk, s), F32)),
        scratch_shapes=[pltpu.VMEM((NSA_HPG, nc, tq), F32), pltpu.SMEM((1,), jnp.int32)],
        compiler_params=_cparams(("parallel", "parallel", "parallel")), name="nsa_compressed_select",
    )(score_bound, qt, kc, vct)


def _flash_step(s_ref, vt, m_ref, acc_ref):
    m_old = m_ref[...]
    m_new = jnp.maximum(m_old, jnp.max(s_ref[...], axis=0, keepdims=True))
    p = jnp.exp2(s_ref[...] - m_new)
    acc_ref[...] = jnp.exp2(m_old - m_new) * acc_ref[...] + _dot(vt, p.astype(BF16))
    m_ref[...] = m_new


SEL, WIN = 0, 1
BIAS_ROWS = 16


def _slcwin_body(m0_ref, qt_ref, bias_ref, kaug_ref, vt_ref, oc_ref, gt_ref, mb_ref, y_ref,
                 q_scr, m_scr, acc_scr, s_scr, p_scr):
    tq = qt_ref.shape[1]
    tk = vt_ref.shape[3]
    assert tq == tk and WINDOW == tk and tk == 8 * SLC_BLOCK
    diag = pl.program_id(2)
    m0 = m0_ref[0]

    sel_slots = (0, 1, 2, 3)
    qs, qw = sel_slots[0], 4
    zeros = jnp.zeros((64, tq), BF16)
    for slot in sel_slots + (qw,):
        for hh in range(NSA_HPG):
            q_scr[slot, hh, 0:64, :] = qt_ref[hh * 64:(hh + 1) * 64, :]
            q_scr[slot, hh, 64:128, :] = zeros

    def set_selection_bias(kt, slot=qs):
        rows = bias_ref[pl.ds(pl.multiple_of(kt * 8, 8), 8), :]
        b16 = jnp.concatenate([rows, jnp.zeros_like(rows)], axis=0).astype(BF16)
        for hh in range(NSA_HPG):
            q_scr[slot, hh, 64:64 + BIAS_ROWS, :] = b16

    def tiles_fixed_reference(tiles):
        chains = [(br, qslot, kt, mask_bias, hh) for br, qslot, kt, mask_bias in tiles for hh in range(NSA_HPG)]
        for c in range(len(chains) + 1):
            if c < len(chains):
                br, qslot, kt, mask_bias, hh = chains[c]
                s = _dot(kaug_ref[br, pl.ds(pl.multiple_of(kt * tk, tk), tk), :], q_scr[qslot, hh])
                if mask_bias is not None:
                    s = s + mask_bias()
                p_scr[c % 4] = jnp.exp2(s - m0).astype(BF16)
            if c >= 1:
                br, _, kt, _, hh = chains[c - 1]
                acc_scr[br, hh] = acc_scr[br, hh] + _dot(vt_ref[br, kt], p_scr[(c - 1) % 4])

    def tile_running_max(br, qslot, kt, mask_bias=None):
        k = kaug_ref[br, pl.ds(pl.multiple_of(kt * tk, tk), tk), :]
        for hh in range(NSA_HPG):
            s = _dot(k, q_scr[qslot, hh])
            s_scr[hh] = s if mask_bias is None else s + mask_bias()
        for hh in range(NSA_HPG):
            _flash_step(s_scr.at[hh], vt_ref[br, kt], m_scr.at[br, hh], acc_scr.at[br, hh])

    prev = jnp.maximum(diag - 1, 0)
    no_prev = jnp.where(diag == 0, NEG, 0.0)
    band_bias = lambda: mb_ref[1] + no_prev
    causal_bias = lambda: mb_ref[0]

    acc_scr[...] = jnp.zeros(acc_scr.shape, F32)

    def unmasked_tiles(first_tile, count, slots):
        for n in range(count):
            set_selection_bias(first_tile + n, slots[n])
        return [(SEL, slots[n], first_tile + n, None) for n in range(count)]

    def tile_quad(j, carry):
        tiles_fixed_reference(unmasked_tiles(4 * j, 4, sel_slots))
        return carry

    lax.fori_loop(0, diag >> 2, tile_quad, 0)

    def tail(leftover):
        tiles = unmasked_tiles(diag - leftover, leftover, sel_slots[1:])
        set_selection_bias(diag, qs)
        tiles_fixed_reference(tiles + [(WIN, qw, prev, band_bias), (SEL, qs, diag, causal_bias),
                                       (WIN, qw, diag, causal_bias)])

    for leftover in range(4):
        pl.when((diag & 3) == leftover)(functools.partial(tail, leftover))

    denominators = acc_scr[:, :, 64:65, :]
    underflow = jnp.logical_not(jnp.min(denominators) > MIN_DENOMINATOR)

    @pl.when(underflow)
    def _():
        m_scr[...] = jnp.full(m_scr.shape, NEG, F32)
        acc_scr[...] = jnp.zeros(acc_scr.shape, F32)

        def full_tile(kt, carry):
            set_selection_bias(kt, qs)
            tile_running_max(SEL, qs, kt)
            return carry

        lax.fori_loop(0, diag, full_tile, 0)
        set_selection_bias(diag, qs)
        tile_running_max(WIN, qw, prev, band_bias)
        tile_running_max(SEL, qs, diag, causal_bias)
        tile_running_max(WIN, qw, diag, causal_bias)

    gt = gt_ref[...]
    for hh in range(NSA_HPG):
        o_s = acc_scr[SEL, hh, 0:64, :] * (1.0 / acc_scr[SEL, hh, 64:65, :])
        o_w = acc_scr[WIN, hh, 0:64, :] * (1.0 / acc_scr[WIN, hh, 64:65, :])
        y_ref[hh * 64:(hh + 1) * 64, :] = (gt[3 * hh:3 * hh + 1] * oc_ref[hh * 64:(hh + 1) * 64, :]
                                          + gt[3 * hh + 1:3 * hh + 2] * o_s
                                          + gt[3 * hh + 2:3 * hh + 3] * o_w)


def _score_bound(q_gain, k_gain):
    bound = HEAD_DIM * QK_SCALE_LOG2 * jnp.max(jnp.abs(q_gain)) * jnp.max(jnp.abs(k_gain))
    return (1.02 * bound).reshape(1).astype(F32)


def _slc_win(qt, bias, kaug, vt, oc, gt, score_bound):
    b, _, s = qt.shape
    nblk = bias.shape[2]
    ns, tk = vt.shape[3], vt.shape[5]
    tq = tk
    key_rel, t_rel = np.arange(tk)[:, None], np.arange(tq)[None, :]
    mask_bias = jnp.asarray(np.stack([np.where(key_rel <= t_rel, 0.0, NEG),
                                      np.where(t_rel + tk - key_rel < WINDOW, 0.0, NEG)]), F32)
    qblk = pl.BlockSpec((None, 256, tq), lambda bi, g, i: (bi, g, i))
    return pl.pallas_call(
        _slcwin_body, grid=(b, NSA_GROUPS, s // tq),
        in_specs=[pl.BlockSpec(memory_space=pltpu.SMEM), qblk,
                  pl.BlockSpec((None, None, nblk, tq), lambda bi, g, i: (bi, g, 0, i)),
                  pl.BlockSpec((None, None, 2, s, 128), lambda bi, g, i: (bi, g, 0, 0, 0)),
                  pl.BlockSpec((None, None, 2, ns, V_ROWS, tk), lambda bi, g, i: (bi, g, 0, 0, 0, 0)),
                  qblk,
                  pl.BlockSpec((None, None, GATE_ROWS, tq), lambda bi, g, i: (bi, g, 0, i)),
                  _resident((2, tk, tq))],
        out_specs=qblk,
        out_shape=jax.ShapeDtypeStruct((b, 512, s), F32),
        scratch_shapes=[pltpu.VMEM((5, NSA_HPG, 128, tq), BF16),
                        pltpu.VMEM((2, NSA_HPG, 1, tq), F32), pltpu.VMEM((2, NSA_HPG, V_ROWS, tq), F32),
                        pltpu.VMEM((NSA_HPG, tk, tq), F32), pltpu.VMEM((NSA_HPG, tk, tq), BF16)],
        compiler_params=_cparams(("parallel", "parallel", "arbitrary")), name="nsa_selected_window",
    )(score_bound, qt, bias, kaug, vt, oc, gt.reshape(b, NSA_GROUPS, GATE_ROWS, s), mask_bias)


def _hgrn_consts():
    c = HGRN_CHUNK
    t = np.arange(c)
    lower = (t[None, :] <= t[:, None]).astype(np.float32)
    rows = [lower]
    masks = []
    for half in HGRN_LEVELS:
        mid = (t // (2 * half)) * (2 * half) + half - 1
        if half < 8:
            rows.append(lower[mid])
        same = (t[:, None] // (2 * half)) == (t[None, :] // (2 * half))
        right = (t[:, None] & half) != 0
        left = (t[None, :] & half) == 0
        masks.append((same & right & left).astype(np.float32))
    masks.append(np.eye(c, dtype=np.float32))
    masks.append(((t[:, None] // HGRN_SUB == t[None, :] // HGRN_SUB) & (t[None, :] <= t[:, None])).astype(np.float32))
    mall = np.concatenate(rows, axis=0)
    lvl = np.stack([np.tile(mk.T, (1, HGRN_HEADS)) for mk in masks])
    bdm = np.kron(np.eye(HGRN_HEADS), np.ones((64, 64), np.float32))
    return jnp.asarray(mall, BF16), jnp.asarray(lvl, F32), jnp.asarray(bdm, F32), jnp.asarray(bdm / 64, BF16)


def _hgrn_body(hg_ref, lb_ref, og_ref, mall_ref, lvl_ref, bdm_ref, bdn_ref, y_ref, st_scr, attn_scr):
    c = HGRN_CHUNK
    w = HGRN_WIDTH

    @pl.when(pl.program_id(1) == 0)
    def _():
        st_scr[...] = jnp.zeros(st_scr.shape, F32)

    chunks = range(hg_ref.shape[0] // c)
    lb = lb_ref[...]
    lane = lax.broadcasted_iota(jnp.int32, (c, w), 1)
    head_masks = [(lane >> 6) == hh for hh in range(HGRN_HEADS)]
    nlev = len(HGRN_LEVELS)

    def stack_heads(x):
        x16 = x.astype(BF16)
        return jnp.concatenate([jnp.where(hm, x16, 0) for hm in head_masks], axis=0)

    def row_bcast(x, half):
        return jnp.concatenate([jnp.broadcast_to(x[p + half - 1:p + half, :], (2 * half, w))
                                for p in range(0, c, 2 * half)], axis=0)

    qa, kk, v16, logf = [], [], [], []
    for ci in chunks:
        rows = slice(ci * c, (ci + 1) * c)
        qa.append(_silu(hg_ref[rows, 0:w]) * (HEAD_DIM ** -0.5))
        fg = lb + (1.0 - lb) * _sigmoid(hg_ref[rows, w:2 * w])
        kk.append(1.0 - fg)
        logf.append(jnp.log2(fg))
        v16.append(hg_ref[rows, 2 * w:3 * w].astype(BF16))

    mall = mall_ref[...]
    parts = _split3(jnp.concatenate(logf, axis=1))
    b_all = sum(_dot(mall[0:c], part) for part in parts)
    bcum = [b_all[:, ci * w:(ci + 1) * w] for ci in chunks]

    def level(ci, li, ref_pt):
        e = jnp.exp2(-jnp.abs(bcum[ci] - ref_pt))
        return lvl_ref[li] * _dot_nt((kk[ci] * e).astype(BF16), stack_heads(qa[ci] * e))

    coarse = [li for li, half in enumerate(HGRN_LEVELS) if half >= HGRN_SUB]
    if not coarse:
        attn_scr[...] = jnp.zeros(attn_scr.shape, F32)
    for li in coarse:
        for ci in chunks:
            contribution = level(ci, li, row_bcast(bcum[ci], HGRN_LEVELS[li]))
            attn_scr[ci] = contribution if li == coarse[0] else attn_scr[ci] + contribution

    def block_start(x):
        firsts = [jnp.zeros((HGRN_SUB, w), F32)]
        firsts += [jnp.broadcast_to(x[p - 1:p, :], (HGRN_SUB, w)) for p in range(HGRN_SUB, c, HGRN_SUB)]
        return jnp.concatenate(firsts, axis=0)

    expo = [block_start(bcum[ci]) - bcum[ci] for ci in chunks]
    largest = expo[0]
    for ci in chunks[1:]:
        largest = jnp.maximum(largest, expo[ci])
    single_reference_ok = jnp.max(largest) < HGRN_MAX_EXPONENT

    @pl.when(single_reference_ok)
    def _():
        for ci in chunks:
            kt = (kk[ci] * jnp.exp2(expo[ci])).astype(BF16)
            attn_scr[ci] = attn_scr[ci] + lvl_ref[nlev + 1] * _dot_nt(kt, stack_heads(qa[ci] * jnp.exp2(-expo[ci])))

    @pl.when(jnp.logical_not(single_reference_ok))
    def _():
        r_fine = sum(_dot(mall[c:], part) for part in parts)
        for ci in chunks:
            attn_scr[ci] = attn_scr[ci] + lvl_ref[nlev] * _dot_nt(kk[ci].astype(BF16), stack_heads(qa[ci]))
        fine = 0
        for li, half in enumerate(HGRN_LEVELS):
            if half >= HGRN_SUB:
                continue
            for ci in chunks:
                if half >= 8:
                    ref_pt = row_bcast(bcum[ci], half)
                else:
                    ref_pt = r_fine[fine * c:(fine + 1) * c, ci * w:(ci + 1) * w]
                attn_scr[ci] = attn_scr[ci] + level(ci, li, ref_pt)
            if half < 8:
                fine += 1

    attn = [attn_scr[ci] for ci in chunks]

    intra, upd, decay, qb = [], [], [], []
    for ci in chunks:
        x = _dot_tn(attn[ci].astype(BF16), v16[ci])
        intra.append(sum(jnp.where(head_masks[hh], x[hh * c:(hh + 1) * c], 0.0) for hh in range(HGRN_HEADS)))
        b_last = bcum[ci][c - 1:c, :]
        kl = (kk[ci] * jnp.exp2(b_last - bcum[ci])).astype(BF16)
        upd.append(bdm_ref[...] * _dot_tn(v16[ci], kl))
        decay.append(jnp.exp2(b_last))
        qb.append((qa[ci] * jnp.exp2(bcum[ci])).astype(BF16))

    st = st_scr[...]
    inter = []
    for ci in chunks:
        inter.append(_dot_nt(qb[ci], st.astype(BF16)))
        st = st * decay[ci] + upd[ci]
    st_scr[...] = st

    for ci in chunks:
        rows = slice(ci * c, (ci + 1) * c)
        o = inter[ci] + intra[ci]
        hi, lo = _split2(o * o)
        ms = _dot(hi, bdn_ref[...]) + _dot(lo, bdn_ref[...])
        y_ref[rows, :] = o * lax.rsqrt(ms + EPS) * og_ref[...] * _silu(hg_ref[rows, 3 * w:4 * w])


def _hgrn(hg, lower_bound, out_gain):
    b, s, _ = hg.shape
    rows = HGRN_CB * HGRN_CHUNK
    mall, lvl, bdm, bdn = _hgrn_consts()
    full = lambda shape: pl.BlockSpec(shape, lambda bi, i: (0,) * len(shape))
    return pl.pallas_call(
        _hgrn_body, grid=(b, s // rows),
        in_specs=[pl.BlockSpec((None, rows, 4 * HGRN_WIDTH), lambda bi, i: (bi, i, 0)),
                  full((1, HGRN_WIDTH)), full((1, HGRN_WIDTH)),
                  full(mall.shape), full(lvl.shape), full(bdm.shape), full(bdn.shape)],
        out_specs=pl.BlockSpec((None, rows, HGRN_WIDTH), lambda bi, i: (bi, i, 0)),
        out_shape=jax.ShapeDtypeStruct((b, s, HGRN_WIDTH), F32),
        scratch_shapes=[pltpu.VMEM((HGRN_WIDTH, HGRN_WIDTH), F32),
                        pltpu.VMEM((HGRN_CB, HGRN_CHUNK, HGRN_HEADS * HGRN_CHUNK), F32)],
        compiler_params=_cparams(("parallel", "arbitrary")), name="hgrn2_chunks",
    )(hg, lower_bound.reshape(1, -1), out_gain.reshape(1, -1), mall, lvl, bdm, bdn)


def _memkv_body(mem_ref, mg_ref, wk_ref, wvt_ref, kg_ref, kh_ref, vht_ref):
    m = mem_ref.shape[0]
    mn = _rms_rows(mem_ref[...], mg_ref[...]).astype(BF16)
    k = _dot(mn, wk_ref[...])
    vt = _dot_nt(wvt_ref[...], mn)
    ones_rows = (lax.broadcasted_iota(jnp.int32, (V_ROWS - 64, m), 0) == 0).astype(BF16)
    for hh in range(MEM_HEADS):
        kh_ref[hh] = _rms_rows(k[:, hh * 64:(hh + 1) * 64], kg_ref[...]).astype(BF16)
        vht_ref[hh, 0:64, :] = vt[hh * 64:(hh + 1) * 64].astype(BF16)
        vht_ref[hh, 64:V_ROWS, :] = ones_rows


def _mem_kv(mem, mem_gain, wk, wvt, k_gain):
    b, m, d = mem.shape
    full = lambda shape: pl.BlockSpec(shape, lambda bi: (0,) * len(shape))
    return pl.pallas_call(
        _memkv_body, grid=(b,),
        in_specs=[pl.BlockSpec((None, m, d), lambda bi: (bi, 0, 0)), full((1, d)),
                  full(wk.shape), full(wvt.shape), full((1, 64))],
        out_specs=(pl.BlockSpec((None, MEM_HEADS, m, 64), lambda bi: (bi, 0, 0, 0)),
                   pl.BlockSpec((None, MEM_HEADS, V_ROWS, m), lambda bi: (bi, 0, 0, 0))),
        out_shape=(jax.ShapeDtypeStruct((b, MEM_HEADS, m, 64), BF16),
                   jax.ShapeDtypeStruct((b, MEM_HEADS, V_ROWS, m), BF16)),
        compiler_params=_cparams(("parallel",)), name="memory_kv",
    )(mem, mem_gain.reshape(1, d), wk, wvt, k_gain.reshape(1, 64))


def _out_body(x_ref, ynt_ref, yh_ref, qmt_ref, kh_ref, vht_ref, ng_ref, mg_ref, wo_ref,
              fg_ref, wg_ref, wu_ref, wd_ref, o_ref, a_scr):
    scores = [_dot(kh_ref[hh], qmt_ref[hh * 64:(hh + 1) * 64, :]) for hh in range(MEM_HEADS)]
    nsa = _rms_cols(ynt_ref[...], ng_ref[...]).astype(BF16)
    acc = _dot_tn(nsa, wo_ref[0:NSA_WIDTH, :])
    acc = acc + _dot(yh_ref[...].astype(BF16), wo_ref[NSA_WIDTH:NSA_WIDTH + HGRN_WIDTH, :])
    y_mem = []
    for s in scores:
        hh = len(y_mem)
        p = jnp.exp2(s - jnp.max(s, axis=0, keepdims=True))
        o = _dot(vht_ref[hh], p.astype(BF16))
        y_mem.append(o[0:64] * (1.0 / o[64:65]))
    mem = _rms_cols(jnp.concatenate(y_mem, axis=0), mg_ref[...]).astype(BF16)
    acc = acc + _dot_tn(mem, wo_ref[NSA_WIDTH + HGRN_WIDTH:, :])
    o_ref[...] = _ffn_half_step(x_ref[...] + acc, fg_ref, wg_ref, wu_ref, wd_ref, a_scr)


def _out_ffn(x3d, ynt, yh, qmt, kh, vht, nsa_gain, mem_gain, wo, ffn_gain, wg, wu, wd, *, tm=TOK_TILE):
    b, s, d = x3d.shape
    m = kh.shape[2]
    return pl.pallas_call(
        _out_body, grid=(b, s // tm),
        in_specs=[pl.BlockSpec((None, tm, d), lambda bi, i: (bi, i, 0)),
                  pl.BlockSpec((None, NSA_WIDTH, tm), lambda bi, i: (bi, 0, i)),
                  pl.BlockSpec((None, tm, HGRN_WIDTH), lambda bi, i: (bi, i, 0)),
                  pl.BlockSpec((None, MEM_WIDTH, tm), lambda bi, i: (bi, 0, i)),
                  pl.BlockSpec((None, MEM_HEADS, m, 64), lambda bi, i: (bi, 0, 0, 0)),
                  pl.BlockSpec((None, MEM_HEADS, V_ROWS, m), lambda bi, i: (bi, 0, 0, 0)),
                  _resident((NSA_WIDTH, 1)), _resident((MEM_WIDTH, 1)), _resident(wo.shape),
                  _resident((1, d)), _resident(wg.shape), _resident(wu.shape), _resident(wd.shape)],
        out_specs=pl.BlockSpec((None, tm, d), lambda bi, i: (bi, i, 0)),
        out_shape=jax.ShapeDtypeStruct((b, s, d), F32),
        scratch_shapes=[pltpu.VMEM((tm, wg.shape[1]), BF16)],
        compiler_params=_cparams(("parallel", "parallel")), name="mix_out_ffn2",
    )(x3d, ynt, yh, qmt, kh, vht, nsa_gain.reshape(-1, 1), mem_gain.reshape(-1, 1), wo,
      ffn_gain.reshape(1, d), wg, wu, wd)


def _layer(x, mem, ffn1, ffn2, mix_norm, w_in, w_out, nsa_q_norm, nsa_k_norm, cmp_pos_k, cmp_w1_k, cmp_w2_k,
           cmp_pos_v, cmp_w1_v, cmp_w2_v, nsa_out_norm, lower_bound, hgrn_out_norm,
           mem_norm, mem_w_k, mem_w_v, mem_q_norm, mem_k_norm, mem_out_norm):
    b, s, d = x.shape
    sizes = (512, 128, 128, 128, 128, 128, 128, 24, 256, 256, 256, 256, 256)
    offs = np.concatenate([[0], np.cumsum(sizes)])
    w16 = w_in.astype(BF16)
    col = lambda i: w16[:, offs[i]:offs[i + 1]]
    (q_a, k_c, v_c, k_s, v_s, k_w, v_w, g_a, q_h, f_h, i_h, g_h, q_m) = [col(i) for i in range(13)]
    gpad = jnp.zeros((d, GATE_ROWS - 3 * NSA_HPG), BF16)
    wt = jnp.concatenate([q_a, v_s, v_w, g_a[:, :3 * NSA_HPG], gpad, g_a[:, 3 * NSA_HPG:], gpad, q_m], axis=1).T
    wn = jnp.concatenate([k_c, v_c, k_s, k_w, q_h, f_h, i_h, g_h], axis=1)

    x1, qt, vt, gt, qmt, kaug, kvc, hg = _ffn_proj(
        x, *ffn1, mix_norm, wt, wn, nsa_q_norm, nsa_k_norm, mem_q_norm, _rope_tables(s))

    cmp_pos, cmp_w1 = _compress_weights(cmp_pos_k, cmp_pos_v, cmp_w1_k, cmp_w1_v)
    kc, vct = _compress(kvc, cmp_pos, cmp_w1, cmp_w2_k.astype(BF16), cmp_w2_v.T.astype(BF16), nsa_k_norm)
    score_bound = _score_bound(nsa_q_norm, nsa_k_norm)
    oc, bias = _cmp_select(qt, kc, vct, score_bound)
    y_nsa = _slc_win(qt, bias, kaug, vt, oc, gt, score_bound)

    y_hgrn = _hgrn(hg, lower_bound, hgrn_out_norm)

    kh, vht = _mem_kv(mem, mem_norm, mem_w_k.astype(BF16), mem_w_v.T.astype(BF16), mem_k_norm)
    return _out_ffn(x1, y_nsa, y_hgrn, qmt, kh, vht, nsa_out_norm, mem_out_norm, w_out.astype(BF16), *ffn2)


def kernel(x, mem, ffn1_norm, ffn1_w_gate, ffn1_w_up, ffn1_w_down, mix_norm, w_in, w_out, nsa_q_norm, nsa_k_norm, cmp_pos_k, cmp_w1_k, cmp_w2_k, cmp_pos_v, cmp_w1_v, cmp_w2_v, nsa_out_norm, hgrn_lb_logits, hgrn_out_norm, mem_norm, mem_w_k, mem_w_v, mem_q_norm, mem_k_norm, mem_out_norm, ffn2_norm, ffn2_w_gate, ffn2_w_up, ffn2_w_down):
    b, s, d = x.shape
    depth = ffn1_norm.shape[0]
    lower_bounds = jnp.cumsum(jax.nn.softmax(hgrn_lb_logits.astype(F32), axis=0), axis=0)
    bf = lambda a: a.astype(BF16)
    for l in range(depth):
        x = _layer(x, mem, (ffn1_norm[l], bf(ffn1_w_gate[l]), bf(ffn1_w_up[l]), bf(ffn1_w_down[l])),
                   (ffn2_norm[l], bf(ffn2_w_gate[l]), bf(ffn2_w_up[l]), bf(ffn2_w_down[l])),
                   mix_norm[l], w_in[l], w_out[l], nsa_q_norm[l], nsa_k_norm[l],
                   cmp_pos_k[l], cmp_w1_k[l], cmp_w2_k[l], cmp_pos_v[l], cmp_w1_v[l], cmp_w2_v[l],
                   nsa_out_norm[l], lower_bounds[l], hgrn_out_norm[l],
                   mem_norm[l], mem_w_k[l], mem_w_v[l], mem_q_norm[l], mem_k_norm[l], mem_out_norm[l])
    return x
```

```python
import functools

import numpy as np
import jax
import jax.numpy as jnp
from jax import lax
from jax.experimental import pallas as pl
from jax.experimental.pallas import tpu as pltpu

F32 = jnp.float32
BF16 = jnp.bfloat16

HEAD_DIM = 64
ROT_DIM = 16
ROT_HALF = 8
ROPE_THETA = 500000.0
NSA_HEADS = 8
NSA_GROUPS = 2
NSA_HPG = 4
CMP_BLOCK = 32
CMP_STRIDE = 16
SLC_BLOCK = 64
SLC_SHIFT = 6
SLC_TOPK = 16
WINDOW = 512
FORCED_SCORE = 1e4
HGRN_HEADS = 4
HGRN_CHUNK = 64
HGRN_WIDTH = 256
MEM_HEADS = 4
MEM_WIDTH = 256
NSA_WIDTH = 512
EPS = 1e-6
NEG = -1e30
QK_SCALE_LOG2 = HEAD_DIM ** -0.5 * 1.4426950408889634
MIN_DENOMINATOR = 2.0 ** -64

VMEM_LIMIT = 56 * 1024 * 1024
MAX_BLOCKS = 128
GATE_ROWS = 16
V_ROWS = 80
TOK_TILE = 512
HGRN_CB = 32
HGRN_LEVELS = (32, 16, 8, 4, 2, 1)
HGRN_SUB = 64
HGRN_MAX_EXPONENT = 96.0

NT_DIMS = (((1,), (1,)), ((), ()))
TN_DIMS = (((0,), (0,)), ((), ()))


def _cparams(sem):
    return pltpu.CompilerParams(dimension_semantics=sem, vmem_limit_bytes=VMEM_LIMIT)


def _dot(a, b):
    return jnp.dot(a, b, preferred_element_type=F32)


def _dot_nt(a, b):
    return lax.dot_general(a, b, NT_DIMS, preferred_element_type=F32)


def _dot_tn(a, b):
    return lax.dot_general(a, b, TN_DIMS, preferred_element_type=F32)


def _sigmoid(x):
    return 1.0 / (1.0 + jnp.exp(-x))


def _silu(x):
    return x * _sigmoid(x)


def _split2(x):
    hi = x.astype(BF16)
    lo = (x - hi.astype(F32)).astype(BF16)
    return hi, lo


def _split3(x):
    hi = x.astype(BF16)
    r1 = x - hi.astype(F32)
    mid = r1.astype(BF16)
    lo = (r1 - mid.astype(F32)).astype(BF16)
    return hi, mid, lo


def _rms_rows(x, gain_row):
    ms = jnp.mean(x * x, axis=-1, keepdims=True)
    return x * lax.rsqrt(ms + EPS) * gain_row


def _rms_cols(x, gain_col):
    ms = jnp.mean(x * x, axis=0, keepdims=True)
    return x * lax.rsqrt(ms + EPS) * gain_col


def _seg_mean_sq(x, bd):
    hi, lo = _split2(x * x)
    return _dot(hi, bd) + _dot(lo, bd)


FFN_CHUNK = 256


def _ffn_half_step(x, g_ref, wg_ref, wu_ref, wd_ref, a_scr):
    xg = (x * g_ref[...]).astype(BF16)
    inv_rms = lax.rsqrt(jnp.mean(x * x, axis=-1, keepdims=True) + EPS)
    d_ff = wg_ref.shape[1]
    for c in range(d_ff // FFN_CHUNK):
        sl = slice(c * FFN_CHUNK, (c + 1) * FFN_CHUNK)
        g = _dot(xg, wg_ref[:, sl]) * inv_rms
        u = _dot(xg, wu_ref[:, sl]) * inv_rms
        a_scr[:, sl] = (_silu(g) * u).astype(BF16)
    return x + 0.5 * _dot(a_scr[...], wd_ref[...])


def _resident(shape):
    return pl.BlockSpec(shape, lambda *_: (0,) * len(shape), pipeline_mode=pl.Buffered(1))


def _rope_cols(xn, cos, sin):
    x0, x1 = xn[0:ROT_HALF], xn[ROT_HALF:ROT_DIM]
    return jnp.concatenate([x0 * cos - x1 * sin, x1 * cos + x0 * sin, xn[ROT_DIM:]], axis=0)


def _rope_rows(x, cn, sa, sb):
    return x * cn + pltpu.roll(x, 128 - ROT_HALF, 1) * sa + pltpu.roll(x, ROT_HALF, 1) * sb


def _proj_body(x_ref, fg_ref, wg_ref, wu_ref, wd_ref, mg_ref, wt_ref, wn_ref, qg_ref, kg_ref, mqg_ref,
               cos_ref, sin_ref, cn_ref, sa_ref, sb_ref, bd_ref,
               x1_ref, qt_ref, vt_ref, gt_ref, qmt_ref, kaug_ref, kvc_ref, hg_ref, a_scr):
    tm = x_ref.shape[0]
    assert tm == 8 * SLC_BLOCK
    x1 = _ffn_half_step(x_ref[...], fg_ref, wg_ref, wu_ref, wd_ref, a_scr)
    x1_ref[...] = x1
    h = _rms_rows(x1, mg_ref[...]).astype(BF16)

    qg, mqg = qg_ref[...], mqg_ref[...]
    half = tm // 2
    ones_rows = (lax.broadcasted_iota(jnp.int32, (V_ROWS - 64, half), 0) == 0).astype(BF16)
    for part in range(2):
        tok = slice(part * half, (part + 1) * half)
        pt = _dot_nt(wt_ref[...], h[tok, :])
        cos, sin = cos_ref[:, tok], sin_ref[:, tok]
        for hh in range(NSA_HEADS):
            xq = _rms_cols(pt[hh * 64:(hh + 1) * 64], qg)
            qt_ref[hh * 64:(hh + 1) * 64, tok] = (_rope_cols(xq, cos, sin) * QK_SCALE_LOG2).astype(BF16)
        for g in range(NSA_GROUPS):
            for br in range(2):
                rows = 512 + br * 128 + g * 64
                vt_ref[g, br, 0:64, tok] = pt[rows:rows + 64].astype(BF16)
                vt_ref[g, br, 64:V_ROWS, tok] = ones_rows
        gt_ref[:, tok] = _sigmoid(pt[768:800])
        for hh in range(MEM_HEADS):
            xm = _rms_cols(pt[800 + hh * 64:864 + hh * 64], mqg)
            qmt_ref[hh * 64:(hh + 1) * 64, tok] = (xm * QK_SCALE_LOG2).astype(BF16)

    cn, sa, sb = cn_ref[...], sa_ref[...], sb_ref[...]
    bd, kg = bd_ref[...], kg_ref[...]
    pc = _dot(h, wn_ref[:, 0:256])
    pk = _dot(h, wn_ref[:, 256:512])
    kvc_ref[0] = _rope_rows(pc[:, 0:128], cn, sa, sb)
    kvc_ref[1] = pc[:, 128:256]
    ph = _dot(h, wn_ref[:, 512:1024])
    ks = pk[:, 0:128]
    kw = pk[:, 128:256]
    ks = _rope_rows(ks * lax.rsqrt(_seg_mean_sq(ks, bd) + EPS) * kg, cn, sa, sb)
    kw = _rope_rows(kw * lax.rsqrt(_seg_mean_sq(kw, bd) + EPS) * kg, cn, sa, sb)
    lane = lax.broadcasted_iota(jnp.int32, (tm, 128), 1)
    row = lax.broadcasted_iota(jnp.int32, (tm, 128), 0)
    onehot = jnp.where(lane - 64 == (row >> SLC_SHIFT), 1.0, 0.0)
    lo_half = lane < 64
    kaug_ref[0, 0] = jnp.where(lo_half, ks, onehot).astype(BF16)
    kaug_ref[0, 1] = jnp.where(lo_half, kw, 0.0).astype(BF16)
    kaug_ref[1, 0] = jnp.where(lo_half, pltpu.roll(ks, 64, 1), onehot).astype(BF16)
    kaug_ref[1, 1] = jnp.where(lo_half, pltpu.roll(kw, 64, 1), 0.0).astype(BF16)
    hg_ref[:, 0:512] = ph
    hg_ref[:, 512:1024] = _dot(h, wn_ref[:, 1024:1536])


def _ffn_proj(x3d, ffn_gain, wg, wu, wd, mix_gain, wt, wn, q_gain, k_gain, mq_gain, rope):
    b, s, d = x3d.shape
    tm = TOK_TILE
    ns = s // tm
    cos_t, sin_t, cn, sa, sb = rope
    bd = jnp.asarray(np.kron(np.eye(2), np.full((64, 64), 1.0 / 64)), BF16)
    full = _resident
    out_shape = (
        jax.ShapeDtypeStruct((b, s, d), F32),
        jax.ShapeDtypeStruct((b, 512, s), BF16),
        jax.ShapeDtypeStruct((b, 2, 2, ns, V_ROWS, tm), BF16),
        jax.ShapeDtypeStruct((b, 32, s), F32),
        jax.ShapeDtypeStruct((b, 256, s), BF16),
        jax.ShapeDtypeStruct((b, 2, 2, s, 128), BF16),
        jax.ShapeDtypeStruct((b, 2, s, 128), F32),
        jax.ShapeDtypeStruct((b, s, 1024), F32),
    )
    out_specs = (
        pl.BlockSpec((None, tm, d), lambda bi, i: (bi, i, 0)),
        pl.BlockSpec((None, 512, tm), lambda bi, i: (bi, 0, i)),
        pl.BlockSpec((None, 2, 2, None, V_ROWS, tm), lambda bi, i: (bi, 0, 0, i, 0, 0)),
        pl.BlockSpec((None, 32, tm), lambda bi, i: (bi, 0, i)),
        pl.BlockSpec((None, 256, tm), lambda bi, i: (bi, 0, i)),
        pl.BlockSpec((None, 2, 2, tm, 128), lambda bi, i: (bi, 0, 0, i, 0)),
        pl.BlockSpec((None, 2, tm, 128), lambda bi, i: (bi, 0, i, 0)),
        pl.BlockSpec((None, tm, 1024), lambda bi, i: (bi, i, 0)),
    )
    in_specs = [
        pl.BlockSpec((None, tm, d), lambda bi, i: (bi, i, 0)),
        full((1, d)), full(wg.shape), full(wu.shape), full(wd.shape),
        full((1, d)), full(wt.shape), full(wn.shape),
        full((64, 1)), full((1, 128)), full((64, 1)),
        pl.BlockSpec((ROT_HALF, tm), lambda bi, i: (0, i)),
        pl.BlockSpec((ROT_HALF, tm), lambda bi, i: (0, i)),
        pl.BlockSpec((tm, 128), lambda bi, i: (i, 0)),
        pl.BlockSpec((tm, 128), lambda bi, i: (i, 0)),
        pl.BlockSpec((tm, 128), lambda bi, i: (i, 0)),
        full((128, 128)),
    ]
    return pl.pallas_call(
        _proj_body, grid=(b, ns), in_specs=in_specs, out_specs=out_specs, out_shape=out_shape,
        scratch_shapes=[pltpu.VMEM((tm, wg.shape[1]), BF16)],
        compiler_params=_cparams(("parallel", "parallel")), name="ffn1_mix_projection",
    )(x3d, ffn_gain.reshape(1, d), wg, wu, wd, mix_gain.reshape(1, d), wt, wn, q_gain.reshape(64, 1),
      jnp.tile(k_gain.reshape(1, 64), (1, 2)), mq_gain.reshape(64, 1), cos_t, sin_t, cn, sa, sb, bd)


def _rope_tables(s):
    pos = np.arange(s, dtype=np.float64)
    inv = ROPE_THETA ** (-(np.arange(0, ROT_DIM, 2, dtype=np.float64) / ROT_DIM))
    ang = pos[:, None] * inv[None, :]
    cos, sin = np.cos(ang), np.sin(ang)
    zeros = np.zeros((s, 64 - ROT_DIM))
    cn = np.concatenate([cos, cos, np.ones((s, 64 - ROT_DIM))], axis=1)
    sa = np.concatenate([-sin, np.zeros((s, ROT_HALF)), zeros], axis=1)
    sb = np.concatenate([np.zeros((s, ROT_HALF)), sin, zeros], axis=1)
    tile2 = lambda a: np.concatenate([a, a], axis=1)
    return tuple(jnp.asarray(a, F32) for a in (cos.T, sin.T, tile2(cn), tile2(sa), tile2(sb)))


def _cmp_body(kvc_ref, pos_ref, w1_ref, w2k_ref, w2vt_ref, kg_ref, kc_ref, vct_ref):
    nc = kvc_ref.shape[1] // CMP_STRIDE
    for kind in range(2):
        halves = []
        for part in range(2):
            x = jnp.concatenate(
                [(kvc_ref[kind, pl.ds(r, nc, stride=CMP_STRIDE), :]
                  + pos_ref[kind, part, :, r * 128:(r + 1) * 128]).astype(BF16) for r in range(CMP_STRIDE)],
                axis=1)
            halves.append(x)
        for g in range(NSA_GROUPS):
            second = _dot(halves[1], w1_ref[kind, g, 1])
            hid = _silu(_dot(halves[0], w1_ref[kind, g, 0]) + pltpu.roll(second, nc - 1, 0)).astype(BF16)
            if kind == 0:
                kc_ref[g] = _rms_rows(_dot(hid, w2k_ref[...]), kg_ref[...]).astype(BF16)
            else:
                vct_ref[g] = _dot_nt(w2vt_ref[...], hid).astype(BF16)


def _compress(kvc, pos, w1, w2k, w2vt, k_gain):
    b, _, s, _ = kvc.shape
    nc = s // CMP_STRIDE
    return pl.pallas_call(
        _cmp_body, grid=(b,),
        in_specs=[pl.BlockSpec((None, 2, s, 128), lambda bi: (bi, 0, 0, 0)),
                  _resident(pos.shape), _resident(w1.shape), _resident(w2k.shape), _resident(w2vt.shape),
                  _resident((1, 64))],
        out_specs=(pl.BlockSpec((None, 2, nc, 64), lambda bi: (bi, 0, 0, 0)),
                   pl.BlockSpec((None, 2, 64, nc), lambda bi: (bi, 0, 0, 0))),
        out_shape=(jax.ShapeDtypeStruct((b, 2, nc, 64), BF16), jax.ShapeDtypeStruct((b, 2, 64, nc), BF16)),
        compiler_params=_cparams(("parallel",)), name="nsa_compress",
    )(kvc, pos, w1, w2k, w2vt, k_gain.reshape(1, 64))


def _compress_weights(pos_k, pos_v, w1_k, w1_v):
    def pos_part(p):
        p = p.reshape(2, CMP_STRIDE, 1, 64)
        return jnp.broadcast_to(p, (2, CMP_STRIDE, NSA_GROUPS, 64)).reshape(2, 1, CMP_STRIDE * 128)

    def w1_part(w):
        hdim = w.shape[1]
        w = w.astype(BF16).reshape(1, 2, CMP_STRIDE, 1, 64, hdim)
        own_group = jnp.eye(NSA_GROUPS, dtype=BF16).reshape(NSA_GROUPS, 1, 1, NSA_GROUPS, 1, 1)
        return (w * own_group).reshape(NSA_GROUPS, 2, CMP_STRIDE * 128, hdim)

    pos = jnp.stack([pos_part(pos_k), pos_part(pos_v)])
    w1 = jnp.stack([w1_part(w1_k), w1_part(w1_v)])
    return pos, w1


CMP_CLASS_ROWS = 128


def _cmpsel_variant(nc, nblk, fixed_reference, m0_ref, qt_ref, kc_ref, vct_ref, oc_ref, bias_ref, s_scr, flag_scr):
    tq = qt_ref.shape[1]
    t0 = pl.program_id(2) * tq
    n_idx = lax.broadcasted_iota(jnp.int32, (nc, tq), 0)
    t_idx = t0 + lax.broadcasted_iota(jnp.int32, (nc, tq), 1)
    mask_bias = jnp.where(n_idx * CMP_STRIDE + (CMP_BLOCK - 1) <= t_idx, 0.0, NEG)
    sees_any = t0 + lax.broadcasted_iota(jnp.int32, (1, tq), 1) >= CMP_BLOCK - 1
    kc = kc_ref[0:nc, :]
    if fixed_reference:
        mask_bias = mask_bias - m0_ref[0]
    else:
        for hh in range(NSA_HPG):
            s_scr[hh, 0:nc, :] = _dot(kc, qt_ref[hh * 64:(hh + 1) * 64, :]) + mask_bias
    jj = lax.broadcasted_iota(jnp.int32, (nblk, nc), 0)
    nn = lax.broadcasted_iota(jnp.int32, (nblk, nc), 1)
    ov = jnp.where((nn * CMP_STRIDE < jj * SLC_BLOCK + SLC_BLOCK)
                   & (nn * CMP_STRIDE + CMP_BLOCK > jj * SLC_BLOCK), 1.0, 0.0).astype(BF16)
    ones_rows = (lax.broadcasted_iota(jnp.int32, (V_ROWS - 64, nc), 0) == 0).astype(BF16)
    lhs = jnp.concatenate([vct_ref[:, 0:nc], ones_rows, ov], axis=0)
    imp = jnp.zeros((nblk, tq), F32)
    l_min = jnp.full((1, tq), 1.0, F32)

    def probabilities(hh):
        if fixed_reference:
            return jnp.exp2(_dot(kc, qt_ref[hh * 64:(hh + 1) * 64, :]) + mask_bias).astype(BF16)
        m = jnp.max(s_scr[hh, 0:nc, :], axis=0, keepdims=True)
        return jnp.exp2(s_scr[hh, 0:nc, :] - m).astype(BF16)

    p_next = probabilities(0)
    for hh in range(NSA_HPG):
        p = p_next
        if hh + 1 < NSA_HPG:
            p_next = probabilities(hh + 1)
        r = _dot(lhs, p)
        l_min = jnp.minimum(l_min, jnp.where(sees_any, r[64:65], 1.0))
        inv_l = jnp.where(sees_any, 1.0 / r[64:65], 0.0)
        oc_ref[hh * 64:(hh + 1) * 64, :] = r[0:64] * inv_l
        imp = imp + r[V_ROWS:] * inv_l
    if fixed_reference:
        flag_scr[0] = jnp.where(jnp.min(l_min) > MIN_DENOMINATOR, 0, 1)

    j = lax.broadcasted_iota(jnp.int32, (nblk, tq), 0)
    cur = (t0 + lax.broadcasted_iota(jnp.int32, (nblk, tq), 1)) >> SLC_SHIFT
    forced = (j == 0) | (j == cur) | (j == cur - 1)
    picks = SLC_TOPK - 3
    forced_bias = jnp.where(forced & (j <= cur), 0.0, NEG)
    imp = jnp.where((j <= cur) & jnp.logical_not(forced), imp, -1.0)
    if nblk < bias_ref.shape[0]:
        bias_ref[nblk:, :] = jnp.full((bias_ref.shape[0] - nblk, tq), NEG, F32)

    rest = imp
    for _ in range(picks):
        v = jnp.max(rest, axis=0, keepdims=True)
        rest = jnp.where(rest == v, -3e38, rest)
    chosen = (imp >= v) & (imp >= 0.0)
    count = jnp.sum(jnp.where(chosen, 1.0, 0.0), axis=0, keepdims=True)
    valid = jnp.sum(jnp.where(imp >= 0.0, 1.0, 0.0), axis=0, keepdims=True)
    bias_ref[0:nblk, :] = jnp.where(chosen, 0.0, forced_bias)
    has_tie = jnp.max(jnp.abs(count - jnp.minimum(valid, float(picks)))) > 0.0

    @pl.when(has_tie)
    def _():
        jf = j.astype(F32)
        bias, rest = forced_bias, imp
        for _ in range(picks):
            v = jnp.max(rest, axis=0, keepdims=True)
            first = jnp.min(jnp.where(rest == v, jf, float(nblk)), axis=0, keepdims=True)
            pick = jf == first
            bias = jnp.where(pick & (v >= 0.0), 0.0, bias)
            rest = jnp.where(pick, -3e38, rest)
        bias_ref[0:nblk, :] = bias


def _cmpsel_body(m0_ref, qt_ref, kc_ref, vct_ref, oc_ref, bias_ref, s_scr, flag_scr):
    tq = qt_ref.shape[1]
    nc_total = kc_ref.shape[0]
    tiles_per_class = CMP_CLASS_ROWS // (tq // CMP_STRIDE)
    cls = pl.program_id(2) // tiles_per_class
    refs = (m0_ref, qt_ref, kc_ref, vct_ref, oc_ref, bias_ref, s_scr, flag_scr)
    for c in range(nc_total // CMP_CLASS_ROWS):
        nc = (c + 1) * CMP_CLASS_ROWS
        nblk = min(nc * CMP_STRIDE // SLC_BLOCK, bias_ref.shape[0])
        pl.when(cls == c)(functools.partial(_cmpsel_variant, nc, nblk, True, *refs))
    pl.when(flag_scr[0] != 0)(functools.partial(_cmpsel_variant, nc_total, bias_ref.shape[0], False, *refs))


def _cmp_select(qt, kc, vct, score_bound, *, tq=TOK_TILE):
    b, _, s = qt.shape
    nc = kc.shape[2]
    nblk = MAX_BLOCKS
    assert s // SLC_BLOCK <= MAX_BLOCKS and s // SLC_BLOCK >= SLC_TOPK and nc % CMP_CLASS_ROWS == 0
    return pl.pallas_call(
        _cmpsel_body, grid=(b, NSA_GROUPS, s // tq),
        in_specs=[pl.BlockSpec(memory_space=pltpu.SMEM),
                  pl.BlockSpec((None, 256, tq), lambda bi, g, i: (bi, g, i)),
                  pl.BlockSpec((None, None, nc, 64), lambda bi, g, i: (bi, g, 0, 0)),
                  pl.BlockSpec((None, None, 64, nc), lambda bi, g, i: (bi, g, 0, 0))],
        out_specs=(pl.BlockSpec((None, 256, tq), lambda bi, g, i: (bi, g, i)),
                   pl.BlockSpec((None, None, nblk, tq), lambda bi, g, i: (bi, g, 0, i))),
        out_shape=(jax.ShapeDtypeStruct((b, 512, s), F32),
                   jax.ShapeDtypeStruct((b, NSA_GROUPS, nblk, s), F32)),
        scratch_shapes=[pltpu.VMEM((NSA_HPG, nc, tq), F32), pltpu.SMEM((1,), jnp.int32)],
        compiler_params=_cparams(("parallel", "parallel", "parallel")), name="nsa_compressed_select",
    )(score_bound, qt, kc, vct)


def _flash_step(s_ref, vt, m_ref, acc_ref):
    m_old = m_ref[...]
    m_new = jnp.maximum(m_old, jnp.max(s_ref[...], axis=0, keepdims=True))
    p = jnp.exp2(s_ref[...] - m_new)
    acc_ref[...] = jnp.exp2(m_old - m_new) * acc_ref[...] + _dot(vt, p.astype(BF16))
    m_ref[...] = m_new


SEL, WIN = 0, 1
BIAS_ROWS = 16


def _slcwin_body(m0_ref, qt_ref, bias_ref, kaug_ref, vt_ref, oc_ref, gt_ref, mb_ref, y_ref,
                 q_scr, m_scr, acc_scr, s_scr, p_scr):
    tq = qt_ref.shape[1]
    tk = vt_ref.shape[3]
    assert tq == tk and WINDOW == tk and tk == 8 * SLC_BLOCK
    diag = pl.program_id(2)
    m0 = m0_ref[0]

    sel_slots = (0, 1, 2, 3)
    qs, qw = sel_slots[0], 4
    zeros = jnp.zeros((64, tq), BF16)
    for slot in sel_slots + (qw,):
        for hh in range(NSA_HPG):
            q_scr[slot, hh, 0:64, :] = qt_ref[hh * 64:(hh + 1) * 64, :]
            q_scr[slot, hh, 64:128, :] = zeros

    def set_selection_bias(kt, slot=qs):
        rows = bias_ref[pl.ds(pl.multiple_of(kt * 8, 8), 8), :]
        b16 = jnp.concatenate([rows, jnp.zeros_like(rows)], axis=0).astype(BF16)
        for hh in range(NSA_HPG):
            q_scr[slot, hh, 64:64 + BIAS_ROWS, :] = b16

    def tiles_fixed_reference(tiles):
        chains = [(br, qslot, kt, mask_bias, hh) for br, qslot, kt, mask_bias in tiles for hh in range(NSA_HPG)]
        for c in range(len(chains) + 1):
            if c < len(chains):
                br, qslot, kt, mask_bias, hh = chains[c]
                s = _dot(kaug_ref[br, pl.ds(pl.multiple_of(kt * tk, tk), tk), :], q_scr[qslot, hh])
                if mask_bias is not None:
                    s = s + mask_bias()
                p_scr[c % 4] = jnp.exp2(s - m0).astype(BF16)
            if c >= 1:
                br, _, kt, _, hh = chains[c - 1]
                acc_scr[br, hh] = acc_scr[br, hh] + _dot(vt_ref[br, kt], p_scr[(c - 1) % 4])

    def tile_running_max(br, qslot, kt, mask_bias=None):
        k = kaug_ref[br, pl.ds(pl.multiple_of(kt * tk, tk), tk), :]
        for hh in range(NSA_HPG):
            s = _dot(k, q_scr[qslot, hh])
            s_scr[hh] = s if mask_bias is None else s + mask_bias()
        for hh in range(NSA_HPG):
            _flash_step(s_scr.at[hh], vt_ref[br, kt], m_scr.at[br, hh], acc_scr.at[br, hh])

    prev = jnp.maximum(diag - 1, 0)
    no_prev = jnp.where(diag == 0, NEG, 0.0)
    band_bias = lambda: mb_ref[1] + no_prev
    causal_bias = lambda: mb_ref[0]

    acc_scr[...] = jnp.zeros(acc_scr.shape, F32)

    def unmasked_tiles(first_tile, count, slots):
        for n in range(count):
            set_selection_bias(first_tile + n, slots[n])
        return [(SEL, slots[n], first_tile + n, None) for n in range(count)]

    def tile_quad(j, carry):
        tiles_fixed_reference(unmasked_tiles(4 * j, 4, sel_slots))
        return carry

    lax.fori_loop(0, diag >> 2, tile_quad, 0)

    def tail(leftover):
        tiles = unmasked_tiles(diag - leftover, leftover, sel_slots[1:])
        set_selection_bias(diag, qs)
        tiles_fixed_reference(tiles + [(WIN, qw, prev, band_bias), (SEL, qs, diag, causal_bias),
                                       (WIN, qw, diag, causal_bias)])

    for leftover in range(4):
        pl.when((diag & 3) == leftover)(functools.partial(tail, leftover))

    denominators = acc_scr[:, :, 64:65, :]
    underflow = jnp.logical_not(jnp.min(denominators) > MIN_DENOMINATOR)

    @pl.when(underflow)
    def _():
        m_scr[...] = jnp.full(m_scr.shape, NEG, F32)
        acc_scr[...] = jnp.zeros(acc_scr.shape, F32)

        def full_tile(kt, carry):
            set_selection_bias(kt, qs)
            tile_running_max(SEL, qs, kt)
            return carry

        lax.fori_loop(0, diag, full_tile, 0)
        set_selection_bias(diag, qs)
        tile_running_max(WIN, qw, prev, band_bias)
        tile_running_max(SEL, qs, diag, causal_bias)
        tile_running_max(WIN, qw, diag, causal_bias)

    gt = gt_ref[...]
    for hh in range(NSA_HPG):
        o_s = acc_scr[SEL, hh, 0:64, :] * (1.0 / acc_scr[SEL, hh, 64:65, :])
        o_w = acc_scr[WIN, hh, 0:64, :] * (1.0 / acc_scr[WIN, hh, 64:65, :])
        y_ref[hh * 64:(hh + 1) * 64, :] = (gt[3 * hh:3 * hh + 1] * oc_ref[hh * 64:(hh + 1) * 64, :]
                                          + gt[3 * hh + 1:3 * hh + 2] * o_s
                                          + gt[3 * hh + 2:3 * hh + 3] * o_w)


def _score_bound(q_gain, k_gain):
    bound = HEAD_DIM * QK_SCALE_LOG2 * jnp.max(jnp.abs(q_gain)) * jnp.max(jnp.abs(k_gain))
    return (1.02 * bound).reshape(1).astype(F32)


def _slc_win(qt, bias, kaug, vt, oc, gt, score_bound):
    b, _, s = qt.shape
    nblk = bias.shape[2]
    ns, tk = vt.shape[3], vt.shape[5]
    tq = tk
    key_rel, t_rel = np.arange(tk)[:, None], np.arange(tq)[None, :]
    mask_bias = jnp.asarray(np.stack([np.where(key_rel <= t_rel, 0.0, NEG),
                                      np.where(t_rel + tk - key_rel < WINDOW, 0.0, NEG)]), F32)
    qblk = pl.BlockSpec((None, 256, tq), lambda bi, g, i: (bi, g, i))
    return pl.pallas_call(
        _slcwin_body, grid=(b, NSA_GROUPS, s // tq),
        in_specs=[pl.BlockSpec(memory_space=pltpu.SMEM), qblk,
                  pl.BlockSpec((None, None, nblk, tq), lambda bi, g, i: (bi, g, 0, i)),
                  pl.BlockSpec((None, None, 2, s, 128), lambda bi, g, i: (bi, g, 0, 0, 0)),
                  pl.BlockSpec((None, None, 2, ns, V_ROWS, tk), lambda bi, g, i: (bi, g, 0, 0, 0, 0)),
                  qblk,
                  pl.BlockSpec((None, None, GATE_ROWS, tq), lambda bi, g, i: (bi, g, 0, i)),
                  _resident((2, tk, tq))],
        out_specs=qblk,
        out_shape=jax.ShapeDtypeStruct((b, 512, s), F32),
        scratch_shapes=[pltpu.VMEM((5, NSA_HPG, 128, tq), BF16),
                        pltpu.VMEM((2, NSA_HPG, 1, tq), F32), pltpu.VMEM((2, NSA_HPG, V_ROWS, tq), F32),
                        pltpu.VMEM((NSA_HPG, tk, tq), F32), pltpu.VMEM((NSA_HPG, tk, tq), BF16)],
        compiler_params=_cparams(("parallel", "parallel", "arbitrary")), name="nsa_selected_window",
    )(score_bound, qt, bias, kaug, vt, oc, gt.reshape(b, NSA_GROUPS, GATE_ROWS, s), mask_bias)


def _hgrn_consts():
    c = HGRN_CHUNK
    t = np.arange(c)
    lower = (t[None, :] <= t[:, None]).astype(np.float32)
    rows = [lower]
    masks = []
    for half in HGRN_LEVELS:
        mid = (t // (2 * half)) * (2 * half) + half - 1
        if half < 8:
            rows.append(lower[mid])
        same = (t[:, None] // (2 * half)) == (t[None, :] // (2 * half))
        right = (t[:, None] & half) != 0
        left = (t[None, :] & half) == 0
        masks.append((same & right & left).astype(np.float32))
    masks.append(np.eye(c, dtype=np.float32))
    masks.append(((t[:, None] // HGRN_SUB == t[None, :] // HGRN_SUB) & (t[None, :] <= t[:, None])).astype(np.float32))
    mall = np.concatenate(rows, axis=0)
    lvl = np.stack([np.tile(mk.T, (1, HGRN_HEADS)) for mk in masks])
    bdm = np.kron(np.eye(HGRN_HEADS), np.ones((64, 64), np.float32))
    return jnp.asarray(mall, BF16), jnp.asarray(lvl, F32), jnp.asarray(bdm, F32), jnp.asarray(bdm / 64, BF16)


def _hgrn_body(hg_ref, lb_ref, og_ref, mall_ref, lvl_ref, bdm_ref, bdn_ref, y_ref, st_scr, attn_scr):
    c = HGRN_CHUNK
    w = HGRN_WIDTH

    @pl.when(pl.program_id(1) == 0)
    def _():
        st_scr[...] = jnp.zeros(st_scr.shape, F32)

    chunks = range(hg_ref.shape[0] // c)
    lb = lb_ref[...]
    lane = lax.broadcasted_iota(jnp.int32, (c, w), 1)
    head_masks = [(lane >> 6) == hh for hh in range(HGRN_HEADS)]
    nlev = len(HGRN_LEVELS)

    def stack_heads(x):
        x16 = x.astype(BF16)
        return jnp.concatenate([jnp.where(hm, x16, 0) for hm in head_masks], axis=0)

    def row_bcast(x, half):
        return jnp.concatenate([jnp.broadcast_to(x[p + half - 1:p + half, :], (2 * half, w))
                                for p in range(0, c, 2 * half)], axis=0)

    qa, kk, v16, logf = [], [], [], []
    for ci in chunks:
        rows = slice(ci * c, (ci + 1) * c)
        qa.append(_silu(hg_ref[rows, 0:w]) * (HEAD_DIM ** -0.5))
        fg = lb + (1.0 - lb) * _sigmoid(hg_ref[rows, w:2 * w])
        kk.append(1.0 - fg)
        logf.append(jnp.log2(fg))
        v16.append(hg_ref[rows, 2 * w:3 * w].astype(BF16))

    mall = mall_ref[...]
    parts = _split3(jnp.concatenate(logf, axis=1))
    b_all = sum(_dot(mall[0:c], part) for part in parts)
    bcum = [b_all[:, ci * w:(ci + 1) * w] for ci in chunks]

    def level(ci, li, ref_pt):
        e = jnp.exp2(-jnp.abs(bcum[ci] - ref_pt))
        return lvl_ref[li] * _dot_nt((kk[ci] * e).astype(BF16), stack_heads(qa[ci] * e))

    coarse = [li for li, half in enumerate(HGRN_LEVELS) if half >= HGRN_SUB]
    if not coarse:
        attn_scr[...] = jnp.zeros(attn_scr.shape, F32)
    for li in coarse:
        for ci in chunks:
            contribution = level(ci, li, row_bcast(bcum[ci], HGRN_LEVELS[li]))
            attn_scr[ci] = contribution if li == coarse[0] else attn_scr[ci] + contribution

    def block_start(x):
        firsts = [jnp.zeros((HGRN_SUB, w), F32)]
        firsts += [jnp.broadcast_to(x[p - 1:p, :], (HGRN_SUB, w)) for p in range(HGRN_SUB, c, HGRN_SUB)]
        return jnp.concatenate(firsts, axis=0)

    expo = [block_start(bcum[ci]) - bcum[ci] for ci in chunks]
    largest = expo[0]
    for ci in chunks[1:]:
        largest = jnp.maximum(largest, expo[ci])
    single_reference_ok = jnp.max(largest) < HGRN_MAX_EXPONENT

    @pl.when(single_reference_ok)
    def _():
        for ci in chunks:
            kt = (kk[ci] * jnp.exp2(expo[ci])).astype(BF16)
            attn_scr[ci] = attn_scr[ci] + lvl_ref[nlev + 1] * _dot_nt(kt, stack_heads(qa[ci] * jnp.exp2(-expo[ci])))

    @pl.when(jnp.logical_not(single_reference_ok))
    def _():
        r_fine = sum(_dot(mall[c:], part) for part in parts)
        for ci in chunks:
            attn_scr[ci] = attn_scr[ci] + lvl_ref[nlev] * _dot_nt(kk[ci].astype(BF16), stack_heads(qa[ci]))
        fine = 0
        for li, half in enumerate(HGRN_LEVELS):
            if half >= HGRN_SUB:
                continue
            for ci in chunks:
                if half >= 8:
                    ref_pt = row_bcast(bcum[ci], half)
                else:
                    ref_pt = r_fine[fine * c:(fine + 1) * c, ci * w:(ci + 1) * w]
                attn_scr[ci] = attn_scr[ci] + level(ci, li, ref_pt)
            if half < 8:
                fine += 1

    attn = [attn_scr[ci] for ci in chunks]

    intra, upd, decay, qb = [], [], [], []
    for ci in chunks:
        x = _dot_tn(attn[ci].astype(BF16), v16[ci])
        intra.append(sum(jnp.where(head_masks[hh], x[hh * c:(hh + 1) * c], 0.0) for hh in range(HGRN_HEADS)))
        b_last = bcum[ci][c - 1:c, :]
        kl = (kk[ci] * jnp.exp2(b_last - bcum[ci])).astype(BF16)
        upd.append(bdm_ref[...] * _dot_tn(v16[ci], kl))
        decay.append(jnp.exp2(b_last))
        qb.append((qa[ci] * jnp.exp2(bcum[ci])).astype(BF16))

    st = st_scr[...]
    inter = []
    for ci in chunks:
        inter.append(_dot_nt(qb[ci], st.astype(BF16)))
        st = st * decay[ci] + upd[ci]
    st_scr[...] = st

    for ci in chunks:
        rows = slice(ci * c, (ci + 1) * c)
        o = inter[ci] + intra[ci]
        hi, lo = _split2(o * o)
        ms = _dot(hi, bdn_ref[...]) + _dot(lo, bdn_ref[...])
        y_ref[rows, :] = o * lax.rsqrt(ms + EPS) * og_ref[...] * _silu(hg_ref[rows, 3 * w:4 * w])


def _hgrn(hg, lower_bound, out_gain):
    b, s, _ = hg.shape
    rows = HGRN_CB * HGRN_CHUNK
    mall, lvl, bdm, bdn = _hgrn_consts()
    full = lambda shape: pl.BlockSpec(shape, lambda bi, i: (0,) * len(shape))
    return pl.pallas_call(
        _hgrn_body, grid=(b, s // rows),
        in_specs=[pl.BlockSpec((None, rows, 4 * HGRN_WIDTH), lambda bi, i: (bi, i, 0)),
                  full((1, HGRN_WIDTH)), full((1, HGRN_WIDTH)),
                  full(mall.shape), full(lvl.shape), full(bdm.shape), full(bdn.shape)],
        out_specs=pl.BlockSpec((None, rows, HGRN_WIDTH), lambda bi, i: (bi, i, 0)),
        out_shape=jax.ShapeDtypeStruct((b, s, HGRN_WIDTH), F32),
        scratch_shapes=[pltpu.VMEM((HGRN_WIDTH, HGRN_WIDTH), F32),
                        pltpu.VMEM((HGRN_CB, HGRN_CHUNK, HGRN_HEADS * HGRN_CHUNK), F32)],
        compiler_params=_cparams(("parallel", "arbitrary")), name="hgrn2_chunks",
    )(hg, lower_bound.reshape(1, -1), out_gain.reshape(1, -1), mall, lvl, bdm, bdn)


def _memkv_body(mem_ref, mg_ref, wk_ref, wvt_ref, kg_ref, kh_ref, vht_ref):
    m = mem_ref.shape[0]
    mn = _rms_rows(mem_ref[...], mg_ref[...]).astype(BF16)
    k = _dot(mn, wk_ref[...])
    vt = _dot_nt(wvt_ref[...], mn)
    ones_rows = (lax.broadcasted_iota(jnp.int32, (V_ROWS - 64, m), 0) == 0).astype(BF16)
    for hh in range(MEM_HEADS):
        kh_ref[hh] = _rms_rows(k[:, hh * 64:(hh + 1) * 64], kg_ref[...]).astype(BF16)
        vht_ref[hh, 0:64, :] = vt[hh * 64:(hh + 1) * 64].astype(BF16)
        vht_ref[hh, 64:V_ROWS, :] = ones_rows


def _mem_kv(mem, mem_gain, wk, wvt, k_gain):
    b, m, d = mem.shape
    full = lambda shape: pl.BlockSpec(shape, lambda bi: (0,) * len(shape))
    return pl.pallas_call(
        _memkv_body, grid=(b,),
        in_specs=[pl.BlockSpec((None, m, d), lambda bi: (bi, 0, 0)), full((1, d)),
                  full(wk.shape), full(wvt.shape), full((1, 64))],
        out_specs=(pl.BlockSpec((None, MEM_HEADS, m, 64), lambda bi: (bi, 0, 0, 0)),
                   pl.BlockSpec((None, MEM_HEADS, V_ROWS, m), lambda bi: (bi, 0, 0, 0))),
        out_shape=(jax.ShapeDtypeStruct((b, MEM_HEADS, m, 64), BF16),
                   jax.ShapeDtypeStruct((b, MEM_HEADS, V_ROWS, m), BF16)),
        compiler_params=_cparams(("parallel",)), name="memory_kv",
    )(mem, mem_gain.reshape(1, d), wk, wvt, k_gain.reshape(1, 64))


def _out_body(x_ref, ynt_ref, yh_ref, qmt_ref, kh_ref, vht_ref, ng_ref, mg_ref, wo_ref,
              fg_ref, wg_ref, wu_ref, wd_ref, o_ref, a_scr):
    scores = [_dot(kh_ref[hh], qmt_ref[hh * 64:(hh + 1) * 64, :]) for hh in range(MEM_HEADS)]
    nsa = _rms_cols(ynt_ref[...], ng_ref[...]).astype(BF16)
    acc = _dot_tn(nsa, wo_ref[0:NSA_WIDTH, :])
    acc = acc + _dot(yh_ref[...].astype(BF16), wo_ref[NSA_WIDTH:NSA_WIDTH + HGRN_WIDTH, :])
    y_mem = []
    for s in scores:
        hh = len(y_mem)
        p = jnp.exp2(s - jnp.max(s, axis=0, keepdims=True))
        o = _dot(vht_ref[hh], p.astype(BF16))
        y_mem.append(o[0:64] * (1.0 / o[64:65]))
    mem = _rms_cols(jnp.concatenate(y_mem, axis=0), mg_ref[...]).astype(BF16)
    acc = acc + _dot_tn(mem, wo_ref[NSA_WIDTH + HGRN_WIDTH:, :])
    o_ref[...] = _ffn_half_step(x_ref[...] + acc, fg_ref, wg_ref, wu_ref, wd_ref, a_scr)


def _out_ffn(x3d, ynt, yh, qmt, kh, vht, nsa_gain, mem_gain, wo, ffn_gain, wg, wu, wd, *, tm=TOK_TILE):
    b, s, d = x3d.shape
    m = kh.shape[2]
    return pl.pallas_call(
        _out_body, grid=(b, s // tm),
        in_specs=[pl.BlockSpec((None, tm, d), lambda bi, i: (bi, i, 0)),
                  pl.BlockSpec((None, NSA_WIDTH, tm), lambda bi, i: (bi, 0, i)),
                  pl.BlockSpec((None, tm, HGRN_WIDTH), lambda bi, i: (bi, i, 0)),
                  pl.BlockSpec((None, MEM_WIDTH, tm), lambda bi, i: (bi, 0, i)),
                  pl.BlockSpec((None, MEM_HEADS, m, 64), lambda bi, i: (bi, 0, 0, 0)),
                  pl.BlockSpec((None, MEM_HEADS, V_ROWS, m), lambda bi, i: (bi, 0, 0, 0)),
                  _resident((NSA_WIDTH, 1)), _resident((MEM_WIDTH, 1)), _resident(wo.shape),
                  _resident((1, d)), _resident(wg.shape), _resident(wu.shape), _resident(wd.shape)],
        out_specs=pl.BlockSpec((None, tm, d), lambda bi, i: (bi, i, 0)),
        out_shape=jax.ShapeDtypeStruct((b, s, d), F32),
        scratch_shapes=[pltpu.VMEM((tm, wg.shape[1]), BF16)],
        compiler_params=_cparams(("parallel", "parallel")), name="mix_out_ffn2",
    )(x3d, ynt, yh, qmt, kh, vht, nsa_gain.reshape(-1, 1), mem_gain.reshape(-1, 1), wo,
      ffn_gain.reshape(1, d), wg, wu, wd)


def _layer(x, mem, ffn1, ffn2, mix_norm, w_in, w_out, nsa_q_norm, nsa_k_norm, cmp_pos_k, cmp_w1_k, cmp_w2_k,
           cmp_pos_v, cmp_w1_v, cmp_w2_v, nsa_out_norm, lower_bound, hgrn_out_norm,
           mem_norm, mem_w_k, mem_w_v, mem_q_norm, mem_k_norm, mem_out_norm):
    b, s, d = x.shape
    sizes = (512, 128, 128, 128, 128, 128, 128, 24, 256, 256, 256, 256, 256)
    offs = np.concatenate([[0], np.cumsum(sizes)])
    w16 = w_in.astype(BF16)
    col = lambda i: w16[:, offs[i]:offs[i + 1]]
    (q_a, k_c, v_c, k_s, v_s, k_w, v_w, g_a, q_h, f_h, i_h, g_h, q_m) = [col(i) for i in range(13)]
    gpad = jnp.zeros((d, GATE_ROWS - 3 * NSA_HPG), BF16)
    wt = jnp.concatenate([q_a, v_s, v_w, g_a[:, :3 * NSA_HPG], gpad, g_a[:, 3 * NSA_HPG:], gpad, q_m], axis=1).T
    wn = jnp.concatenate([k_c, v_c, k_s, k_w, q_h, f_h, i_h, g_h], axis=1)

    x1, qt, vt, gt, qmt, kaug, kvc, hg = _ffn_proj(
        x, *ffn1, mix_norm, wt, wn, nsa_q_norm, nsa_k_norm, mem_q_norm, _rope_tables(s))

    cmp_pos, cmp_w1 = _compress_weights(cmp_pos_k, cmp_pos_v, cmp_w1_k, cmp_w1_v)
    kc, vct = _compress(kvc, cmp_pos, cmp_w1, cmp_w2_k.astype(BF16), cmp_w2_v.T.astype(BF16), nsa_k_norm)
    score_bound = _score_bound(nsa_q_norm, nsa_k_norm)
    oc, bias = _cmp_select(qt, kc, vct, score_bound)
    y_nsa = _slc_win(qt, bias, kaug, vt, oc, gt, score_bound)

    y_hgrn = _hgrn(hg, lower_bound, hgrn_out_norm)

    kh, vht = _mem_kv(mem, mem_norm, mem_w_k.astype(BF16), mem_w_v.T.astype(BF16), mem_k_norm)
    return _out_ffn(x1, y_nsa, y_hgrn, qmt, kh, vht, nsa_out_norm, mem_out_norm, w_out.astype(BF16), *ffn2)


def kernel(x, mem, ffn1_norm, ffn1_w_gate, ffn1_w_up, ffn1_w_down, mix_norm, w_in, w_out, nsa_q_norm, nsa_k_norm, cmp_pos_k, cmp_w1_k, cmp_w2_k, cmp_pos_v, cmp_w1_v, cmp_w2_v, nsa_out_norm, hgrn_lb_logits, hgrn_out_norm, mem_norm, mem_w_k, mem_w_v, mem_q_norm, mem_k_norm, mem_out_norm, ffn2_norm, ffn2_w_gate, ffn2_w_up, ffn2_w_down):
    b, s, d = x.shape
    depth = ffn1_norm.shape[0]
    lower_bounds = jnp.cumsum(jax.nn.softmax(hgrn_lb_logits.astype(F32), axis=0), axis=0)
    bf = lambda a: a.astype(BF16)
    for l in range(depth):
        x = _layer(x, mem, (ffn1_norm[l], bf(ffn1_w_gate[l]), bf(ffn1_w_up[l]), bf(ffn1_w_down[l])),
                   (ffn2_norm[l], bf(ffn2_w_gate[l]), bf(ffn2_w_up[l]), bf(ffn2_w_down[l])),
                   mix_norm[l], w_in[l], w_out[l], nsa_q_norm[l], nsa_k_norm[l],
                   cmp_pos_k[l], cmp_w1_k[l], cmp_w2_k[l], cmp_pos_v[l], cmp_w1_v[l], cmp_w2_v[l],
                   nsa_out_norm[l], lower_bounds[l], hgrn_out_norm[l],
                   mem_norm[l], mem_w_k[l], mem_w_v[l], mem_q_norm[l], mem_k_norm[l], mem_out_norm[l])
    return x
```

```python
import functools

import numpy as np
import jax
import jax.numpy as jnp
from jax import lax
from jax.experimental import pallas as pl
from jax.experimental.pallas import tpu as pltpu

F32 = jnp.float32
BF16 = jnp.bfloat16

HEAD_DIM = 64
ROT_DIM = 16
ROT_HALF = 8
ROPE_THETA = 500000.0
NSA_HEADS = 8
NSA_GROUPS = 2
NSA_HPG = 4
CMP_BLOCK = 32
CMP_STRIDE = 16
SLC_BLOCK = 64
SLC_SHIFT = 6
SLC_TOPK = 16
WINDOW = 512
FORCED_SCORE = 1e4
HGRN_HEADS = 4
HGRN_CHUNK = 64
HGRN_WIDTH = 256
MEM_HEADS = 4
MEM_WIDTH = 256
NSA_WIDTH = 512
EPS = 1e-6
NEG = -1e30
QK_SCALE_LOG2 = HEAD_DIM ** -0.5 * 1.4426950408889634
MIN_DENOMINATOR = 2.0 ** -64

VMEM_LIMIT = 56 * 1024 * 1024
MAX_BLOCKS = 128
GATE_ROWS = 16
V_ROWS = 80
TOK_TILE = 512
HGRN_CB = 32
HGRN_LEVELS = (32, 16, 8, 4, 2, 1)
HGRN_SUB = 64
HGRN_MAX_EXPONENT = 96.0

NT_DIMS = (((1,), (1,)), ((), ()))
TN_DIMS = (((0,), (0,)), ((), ()))


def _cparams(sem):
    return pltpu.CompilerParams(dimension_semantics=sem, vmem_limit_bytes=VMEM_LIMIT)


def _dot(a, b):
    return jnp.dot(a, b, preferred_element_type=F32)


def _dot_nt(a, b):
    return lax.dot_general(a, b, NT_DIMS, preferred_element_type=F32)


def _dot_tn(a, b):
    return lax.dot_general(a, b, TN_DIMS, preferred_element_type=F32)


def _sigmoid(x):
    return 1.0 / (1.0 + jnp.exp(-x))


def _silu(x):
    return x * _sigmoid(x)


def _split2(x):
    hi = x.astype(BF16)
    lo = (x - hi.astype(F32)).astype(BF16)
    return hi, lo


def _split3(x):
    hi = x.astype(BF16)
    r1 = x - hi.astype(F32)
    mid = r1.astype(BF16)
    lo = (r1 - mid.astype(F32)).astype(BF16)
    return hi, mid, lo


def _rms_rows(x, gain_row):
    ms = jnp.mean(x * x, axis=-1, keepdims=True)
    return x * lax.rsqrt(ms + EPS) * gain_row


def _rms_cols(x, gain_col):
    ms = jnp.mean(x * x, axis=0, keepdims=True)
    return x * lax.rsqrt(ms + EPS) * gain_col


def _seg_mean_sq(x, bd):
    hi, lo = _split2(x * x)
    return _dot(hi, bd) + _dot(lo, bd)


FFN_CHUNK = 256


def _ffn_half_step(x, g_ref, wg_ref, wu_ref, wd_ref, a_scr):
    xg = (x * g_ref[...]).astype(BF16)
    inv_rms = lax.rsqrt(jnp.mean(x * x, axis=-1, keepdims=True) + EPS)
    d_ff = wg_ref.shape[1]
    for c in range(d_ff // FFN_CHUNK):
        sl = slice(c * FFN_CHUNK, (c + 1) * FFN_CHUNK)
        g = _dot(xg, wg_ref[:, sl]) * inv_rms
        u = _dot(xg, wu_ref[:, sl]) * inv_rms
        a_scr[:, sl] = (_silu(g) * u).astype(BF16)
    return x + 0.5 * _dot(a_scr[...], wd_ref[...])


def _resident(shape):
    return pl.BlockSpec(shape, lambda *_: (0,) * len(shape), pipeline_mode=pl.Buffered(1))


def _rope_cols(xn, cos, sin):
    x0, x1 = xn[0:ROT_HALF], xn[ROT_HALF:ROT_DIM]
    return jnp.concatenate([x0 * cos - x1 * sin, x1 * cos + x0 * sin, xn[ROT_DIM:]], axis=0)


def _rope_rows(x, cn, sa, sb):
    return x * cn + pltpu.roll(x, 128 - ROT_HALF, 1) * sa + pltpu.roll(x, ROT_HALF, 1) * sb


def _proj_body(x_ref, fg_ref, wg_ref, wu_ref, wd_ref, mg_ref, wt_ref, wn_ref, qg_ref, kg_ref, mqg_ref,
               cos_ref, sin_ref, cn_ref, sa_ref, sb_ref, bd_ref,
               x1_ref, qt_ref, vt_ref, gt_ref, qmt_ref, kaug_ref, kvc_ref, hg_ref, a_scr):
    tm = x_ref.shape[0]
    assert tm == 8 * SLC_BLOCK
    x1 = _ffn_half_step(x_ref[...], fg_ref, wg_ref, wu_ref, wd_ref, a_scr)
    x1_ref[...] = x1
    h = _rms_rows(x1, mg_ref[...]).astype(BF16)

    qg, mqg = qg_ref[...], mqg_ref[...]
    half = tm // 2
    ones_rows = (lax.broadcasted_iota(jnp.int32, (V_ROWS - 64, half), 0) == 0).astype(BF16)
    for part in range(2):
        tok = slice(part * half, (part + 1) * half)
        pt = _dot_nt(wt_ref[...], h[tok, :])
        cos, sin = cos_ref[:, tok], sin_ref[:, tok]
        for hh in range(NSA_HEADS):
            xq = _rms_cols(pt[hh * 64:(hh + 1) * 64], qg)
            qt_ref[hh * 64:(hh + 1) * 64, tok] = (_rope_cols(xq, cos, sin) * QK_SCALE_LOG2).astype(BF16)
        for g in range(NSA_GROUPS):
            for br in range(2):
                rows = 512 + br * 128 + g * 64
                vt_ref[g, br, 0:64, tok] = pt[rows:rows + 64].astype(BF16)
                vt_ref[g, br, 64:V_ROWS, tok] = ones_rows
        gt_ref[:, tok] = _sigmoid(pt[768:800])
        for hh in range(MEM_HEADS):
            xm = _rms_cols(pt[800 + hh * 64:864 + hh * 64], mqg)
            qmt_ref[hh * 64:(hh + 1) * 64, tok] = (xm * QK_SCALE_LOG2).astype(BF16)

    cn, sa, sb = cn_ref[...], sa_ref[...], sb_ref[...]
    bd, kg = bd_ref[...], kg_ref[...]
    pc = _dot(h, wn_ref[:, 0:256])
    pk = _dot(h, wn_ref[:, 256:512])
    kvc_ref[0] = _rope_rows(pc[:, 0:128], cn, sa, sb)
    kvc_ref[1] = pc[:, 128:256]
    ph = _dot(h, wn_ref[:, 512:1024])
    ks = pk[:, 0:128]
    kw = pk[:, 128:256]
    ks = _rope_rows(ks * lax.rsqrt(_seg_mean_sq(ks, bd) + EPS) * kg, cn, sa, sb)
    kw = _rope_rows(kw * lax.rsqrt(_seg_mean_sq(kw, bd) + EPS) * kg, cn, sa, sb)
    lane = lax.broadcasted_iota(jnp.int32, (tm, 128), 1)
    row = lax.broadcasted_iota(jnp.int32, (tm, 128), 0)
    onehot = jnp.where(lane - 64 == (row >> SLC_SHIFT), 1.0, 0.0)
    lo_half = lane < 64
    kaug_ref[0, 0] = jnp.where(lo_half, ks, onehot).astype(BF16)
    kaug_ref[0, 1] = jnp.where(lo_half, kw, 0.0).astype(BF16)
    kaug_ref[1, 0] = jnp.where(lo_half, pltpu.roll(ks, 64, 1), onehot).astype(BF16)
    kaug_ref[1, 1] = jnp.where(lo_half, pltpu.roll(kw, 64, 1), 0.0).astype(BF16)
    hg_ref[:, 0:512] = ph
    hg_ref[:, 512:1024] = _dot(h, wn_ref[:, 1024:1536])


def _ffn_proj(x3d, ffn_gain, wg, wu, wd, mix_gain, wt, wn, q_gain, k_gain, mq_gain, rope):
    b, s, d = x3d.shape
    tm = TOK_TILE
    ns = s // tm
    cos_t, sin_t, cn, sa, sb = rope
    bd = jnp.asarray(np.kron(np.eye(2), np.full((64, 64), 1.0 / 64)), BF16)
    full = _resident
    out_shape = (
        jax.ShapeDtypeStruct((b, s, d), F32),
        jax.ShapeDtypeStruct((b, 512, s), BF16),
        jax.ShapeDtypeStruct((b, 2, 2, ns, V_ROWS, tm), BF16),
        jax.ShapeDtypeStruct((b, 32, s), F32),
        jax.ShapeDtypeStruct((b, 256, s), BF16),
        jax.ShapeDtypeStruct((b, 2, 2, s, 128), BF16),
        jax.ShapeDtypeStruct((b, 2, s, 128), F32),
        jax.ShapeDtypeStruct((b, s, 1024), F32),
    )
    out_specs = (
        pl.BlockSpec((None, tm, d), lambda bi, i: (bi, i, 0)),
        pl.BlockSpec((None, 512, tm), lambda bi, i: (bi, 0, i)),
        pl.BlockSpec((None, 2, 2, None, V_ROWS, tm), lambda bi, i: (bi, 0, 0, i, 0, 0)),
        pl.BlockSpec((None, 32, tm), lambda bi, i: (bi, 0, i)),
        pl.BlockSpec((None, 256, tm), lambda bi, i: (bi, 0, i)),
        pl.BlockSpec((None, 2, 2, tm, 128), lambda bi, i: (bi, 0, 0, i, 0)),
        pl.BlockSpec((None, 2, tm, 128), lambda bi, i: (bi, 0, i, 0)),
        pl.BlockSpec((None, tm, 1024), lambda bi, i: (bi, i, 0)),
    )
    in_specs = [
        pl.BlockSpec((None, tm, d), lambda bi, i: (bi, i, 0)),
        full((1, d)), full(wg.shape), full(wu.shape), full(wd.shape),
        full((1, d)), full(wt.shape), full(wn.shape),
        full((64, 1)), full((1, 128)), full((64, 1)),
        pl.BlockSpec((ROT_HALF, tm), lambda bi, i: (0, i)),
        pl.BlockSpec((ROT_HALF, tm), lambda bi, i: (0, i)),
        pl.BlockSpec((tm, 128), lambda bi, i: (i, 0)),
        pl.BlockSpec((tm, 128), lambda bi, i: (i, 0)),
        pl.BlockSpec((tm, 128), lambda bi, i: (i, 0)),
        full((128, 128)),
    ]
    return pl.pallas_call(
        _proj_body, grid=(b, ns), in_specs=in_specs, out_specs=out_specs, out_shape=out_shape,
        scratch_shapes=[pltpu.VMEM((tm, wg.shape[1]), BF16)],
        compiler_params=_cparams(("parallel", "parallel")), name="ffn1_mix_projection",
    )(x3d, ffn_gain.reshape(1, d), wg, wu, wd, mix_gain.reshape(1, d), wt, wn, q_gain.reshape(64, 1),
      jnp.tile(k_gain.reshape(1, 64), (1, 2)), mq_gain.reshape(64, 1), cos_t, sin_t, cn, sa, sb, bd)


def _rope_tables(s):
    pos = np.arange(s, dtype=np.float64)
    inv = ROPE_THETA ** (-(np.arange(0, ROT_DIM, 2, dtype=np.float64) / ROT_DIM))
    ang = pos[:, None] * inv[None, :]
    cos, sin = np.cos(ang), np.sin(ang)
    zeros = np.zeros((s, 64 - ROT_DIM))
    cn = np.concatenate([cos, cos, np.ones((s, 64 - ROT_DIM))], axis=1)
    sa = np.concatenate([-sin, np.zeros((s, ROT_HALF)), zeros], axis=1)
    sb = np.concatenate([np.zeros((s, ROT_HALF)), sin, zeros], axis=1)
    tile2 = lambda a: np.concatenate([a, a], axis=1)
    return tuple(jnp.asarray(a, F32) for a in (cos.T, sin.T, tile2(cn), tile2(sa), tile2(sb)))


def _cmp_body(kvc_ref, pos_ref, w1_ref, w2k_ref, w2vt_ref, kg_ref, kc_ref, vct_ref):
    nc = kvc_ref.shape[1] // CMP_STRIDE
    for kind in range(2):
        halves = []
        for part in range(2):
            x = jnp.concatenate(
                [(kvc_ref[kind, pl.ds(r, nc, stride=CMP_STRIDE), :]
                  + pos_ref[kind, part, :, r * 128:(r + 1) * 128]).astype(BF16) for r in range(CMP_STRIDE)],
                axis=1)
            halves.append(x)
        for g in range(NSA_GROUPS):
            second = _dot(halves[1], w1_ref[kind, g, 1])
            hid = _silu(_dot(halves[0], w1_ref[kind, g, 0]) + pltpu.roll(second, nc - 1, 0)).astype(BF16)
            if kind == 0:
                kc_ref[g] = _rms_rows(_dot(hid, w2k_ref[...]), kg_ref[...]).astype(BF16)
            else:
                vct_ref[g] = _dot_nt(w2vt_ref[...], hid).astype(BF16)


def _compress(kvc, pos, w1, w2k, w2vt, k_gain):
    b, _, s, _ = kvc.shape
    nc = s // CMP_STRIDE
    return pl.pallas_call(
        _cmp_body, grid=(b,),
        in_specs=[pl.BlockSpec((None, 2, s, 128), lambda bi: (bi, 0, 0, 0)),
                  _resident(pos.shape), _resident(w1.shape), _resident(w2k.shape), _resident(w2vt.shape),
                  _resident((1, 64))],
        out_specs=(pl.BlockSpec((None, 2, nc, 64), lambda bi: (bi, 0, 0, 0)),
                   pl.BlockSpec((None, 2, 64, nc), lambda bi: (bi, 0, 0, 0))),
        out_shape=(jax.ShapeDtypeStruct((b, 2, nc, 64), BF16), jax.ShapeDtypeStruct((b, 2, 64, nc), BF16)),
        compiler_params=_cparams(("parallel",)), name="nsa_compress",
    )(kvc, pos, w1, w2k, w2vt, k_gain.reshape(1, 64))


def _compress_weights(pos_k, pos_v, w1_k, w1_v):
    def pos_part(p):
        p = p.reshape(2, CMP_STRIDE, 1, 64)
        return jnp.broadcast_to(p, (2, CMP_STRIDE, NSA_GROUPS, 64)).reshape(2, 1, CMP_STRIDE * 128)

    def w1_part(w):
        hdim = w.shape[1]
        w = w.astype(BF16).reshape(1, 2, CMP_STRIDE, 1, 64, hdim)
        own_group = jnp.eye(NSA_GROUPS, dtype=BF16).reshape(NSA_GROUPS, 1, 1, NSA_GROUPS, 1, 1)
        return (w * own_group).reshape(NSA_GROUPS, 2, CMP_STRIDE * 128, hdim)

    pos = jnp.stack([pos_part(pos_k), pos_part(pos_v)])
    w1 = jnp.stack([w1_part(w1_k), w1_part(w1_v)])
    return pos, w1


CMP_CLASS_ROWS = 128


def _cmpsel_variant(nc, nblk, fixed_reference, m0_ref, qt_ref, kc_ref, vct_ref, oc_ref, bias_ref, s_scr, flag_scr):
    tq = qt_ref.shape[1]
    t0 = pl.program_id(2) * tq
    n_idx = lax.broadcasted_iota(jnp.int32, (nc, tq), 0)
    t_idx = t0 + lax.broadcasted_iota(jnp.int32, (nc, tq), 1)
    mask_bias = jnp.where(n_idx * CMP_STRIDE + (CMP_BLOCK - 1) <= t_idx, 0.0, NEG)
    sees_any = t0 + lax.broadcasted_iota(jnp.int32, (1, tq), 1) >= CMP_BLOCK - 1
    kc = kc_ref[0:nc, :]
    if fixed_reference:
        mask_bias = mask_bias - m0_ref[0]
    else:
        for hh in range(NSA_HPG):
            s_scr[hh, 0:nc, :] = _dot(kc, qt_ref[hh * 64:(hh + 1) * 64, :]) + mask_bias
    jj = lax.broadcasted_iota(jnp.int32, (nblk, nc), 0)
    nn = lax.broadcasted_iota(jnp.int32, (nblk, nc), 1)
    ov = jnp.where((nn * CMP_STRIDE < jj * SLC_BLOCK + SLC_BLOCK)
                   & (nn * CMP_STRIDE + CMP_BLOCK > jj * SLC_BLOCK), 1.0, 0.0).astype(BF16)
    ones_rows = (lax.broadcasted_iota(jnp.int32, (V_ROWS - 64, nc), 0) == 0).astype(BF16)
    lhs = jnp.concatenate([vct_ref[:, 0:nc], ones_rows, ov], axis=0)
    imp = jnp.zeros((nblk, tq), F32)
    l_min = jnp.full((1, tq), 1.0, F32)

    def probabilities(hh):
        if fixed_reference:
            return jnp.exp2(_dot(kc, qt_ref[hh * 64:(hh + 1) * 64, :]) + mask_bias).astype(BF16)
        m = jnp.max(s_scr[hh, 0:nc, :], axis=0, keepdims=True)
        return jnp.exp2(s_scr[hh, 0:nc, :] - m).astype(BF16)

    p_next = probabilities(0)
    for hh in range(NSA_HPG):
        p = p_next
        if hh + 1 < NSA_HPG:
            p_next = probabilities(hh + 1)
        r = _dot(lhs, p)
        l_min = jnp.minimum(l_min, jnp.where(sees_any, r[64:65], 1.0))
        inv_l = jnp.where(sees_any, 1.0 / r[64:65], 0.0)
        oc_ref[hh * 64:(hh + 1) * 64, :] = r[0:64] * inv_l
        imp = imp + r[V_ROWS:] * inv_l
    if fixed_reference:
        flag_scr[0] = jnp.where(jnp.min(l_min) > MIN_DENOMINATOR, 0, 1)

    j = lax.broadcasted_iota(jnp.int32, (nblk, tq), 0)
    cur = (t0 + lax.broadcasted_iota(jnp.int32, (nblk, tq), 1)) >> SLC_SHIFT
    forced = (j == 0) | (j == cur) | (j == cur - 1)
    picks = SLC_TOPK - 3
    forced_bias = jnp.where(forced & (j <= cur), 0.0, NEG)
    imp = jnp.where((j <= cur) & jnp.logical_not(forced), imp, -1.0)
    if nblk < bias_ref.shape[0]:
        bias_ref[nblk:, :] = jnp.full((bias_ref.shape[0] - nblk, tq), NEG, F32)

    rest = imp
    for _ in range(picks):
        v = jnp.max(rest, axis=0, keepdims=True)
        rest = jnp.where(rest == v, -3e38, rest)
    chosen = (imp >= v) & (imp >= 0.0)
    count = jnp.sum(jnp.where(chosen, 1.0, 0.0), axis=0, keepdims=True)
    valid = jnp.sum(jnp.where(imp >= 0.0, 1.0, 0.0), axis=0, keepdims=True)
    bias_ref[0:nblk, :] = jnp.where(chosen, 0.0, forced_bias)
    has_tie = jnp.max(jnp.abs(count - jnp.minimum(valid, float(picks)))) > 0.0

    @pl.when(has_tie)
    def _():
        jf = j.astype(F32)
        bias, rest = forced_bias, imp
        for _ in range(picks):
            v = jnp.max(rest, axis=0, keepdims=True)
            first = jnp.min(jnp.where(rest == v, jf, float(nblk)), axis=0, keepdims=True)
            pick = jf == first
            bias = jnp.where(pick & (v >= 0.0), 0.0, bias)
            rest = jnp.where(pick, -3e38, rest)
        bias_ref[0:nblk, :] = bias


def _cmpsel_body(m0_ref, qt_ref, kc_ref, vct_ref, oc_ref, bias_ref, s_scr, flag_scr):
    tq = qt_ref.shape[1]
    nc_total = kc_ref.shape[0]
    tiles_per_class = CMP_CLASS_ROWS // (tq // CMP_STRIDE)
    cls = pl.program_id(2) // tiles_per_class
    refs = (m0_ref, qt_ref, kc_ref, vct_ref, oc_ref, bias_ref, s_scr, flag_scr)
    for c in range(nc_total // CMP_CLASS_ROWS):
        nc = (c + 1) * CMP_CLASS_ROWS
        nblk = min(nc * CMP_STRIDE // SLC_BLOCK, bias_ref.shape[0])
        pl.when(cls == c)(functools.partial(_cmpsel_variant, nc, nblk, True, *refs))
    pl.when(flag_scr[0] != 0)(functools.partial(_cmpsel_variant, nc_total, bias_ref.shape[0], False, *refs))


def _cmp_select(qt, kc, vct, score_bound, *, tq=TOK_TILE):
    b, _, s = qt.shape
    nc = kc.shape[2]
    nblk = MAX_BLOCKS
    assert s // SLC_BLOCK <= MAX_BLOCKS and s // SLC_BLOCK >= SLC_TOPK and nc % CMP_CLASS_ROWS == 0
    return pl.pallas_call(
        _cmpsel_body, grid=(b, NSA_GROUPS, s // tq),
        in_specs=[pl.BlockSpec(memory_space=pltpu.SMEM),
                  pl.BlockSpec((None, 256, tq), lambda bi, g, i: (bi, g, i)),
                  pl.BlockSpec((None, None, nc, 64), lambda bi, g, i: (bi, g, 0, 0)),
                  pl.BlockSpec((None, None, 64, nc), lambda bi, g, i: (bi, g, 0, 0))],
        out_specs=(pl.BlockSpec((None, 256, tq), lambda bi, g, i: (bi, g, i)),
                   pl.BlockSpec((None, None, nblk, tq), lambda bi, g, i: (bi, g, 0, i))),
        out_shape=(jax.ShapeDtypeStruct((b, 512, s), F32),
                   jax.ShapeDtypeStruct((b, NSA_GROUPS, nblk, s), F32)),
        scratch_shapes=[pltpu.VMEM((NSA_HPG, nc, tq), F32), pltpu.SMEM((1,), jnp.int32)],
        compiler_params=_cparams(("parallel", "parallel", "parallel")), name="nsa_compressed_select",
    )(score_bound, qt, kc, vct)


def _flash_step(s_ref, vt, m_ref, acc_ref):
    m_old = m_ref[...]
    m_new = jnp.maximum(m_old, jnp.max(s_ref[...], axis=0, keepdims=True))
    p = jnp.exp2(s_ref[...] - m_new)
    acc_ref[...] = jnp.exp2(m_old - m_new) * acc_ref[...] + _dot(vt, p.astype(BF16))
    m_ref[...] = m_new


SEL, WIN = 0, 1
BIAS_ROWS = 16


def _slcwin_body(m0_ref, qt_ref, bias_ref, kaug_ref, vt_ref, oc_ref, gt_ref, mb_ref, y_ref,
                 q_scr, m_scr, acc_scr, s_scr, p_scr):
    tq = qt_ref.shape[1]
    tk = vt_ref.shape[4]
    assert tq == tk and WINDOW == tk and tk == 8 * SLC_BLOCK
    diag = pl.program_id(1)
    m0 = m0_ref[0]

    sel_slots = (0, 1, 2, 3)
    qs, qw = sel_slots[0], 4
    zeros = jnp.zeros((64, tq), BF16)
    for slot in sel_slots + (qw,):
        for hh in range(NSA_HEADS):
            q_scr[slot, hh, 0:64, :] = qt_ref[hh * 64:(hh + 1) * 64, :]
            q_scr[slot, hh, 64:128, :] = zeros

    def set_selection_bias(kt, slot=qs):
        for g in range(NSA_GROUPS):
            rows = bias_ref[g, pl.ds(pl.multiple_of(kt * 8, 8), 8), :]
            b16 = jnp.concatenate([rows, jnp.zeros_like(rows)], axis=0).astype(BF16)
            for hh in range(NSA_HPG):
                q_scr[slot, g * NSA_HPG + hh, 64:64 + BIAS_ROWS, :] = b16

    def tiles_fixed_reference(tiles):
        chains = [(br, qslot, kt, mask_bias, hh) for br, qslot, kt, mask_bias in tiles for hh in range(NSA_HEADS)]
        for c in range(len(chains) + 1):
            if c < len(chains):
                br, qslot, kt, mask_bias, hh = chains[c]
                s = _dot(kaug_ref[hh // NSA_HPG, br, pl.ds(pl.multiple_of(kt * tk, tk), tk), :], q_scr[qslot, hh])
                if mask_bias is not None:
                    s = s + mask_bias()
                p_scr[c % 4] = jnp.exp2(s - m0).astype(BF16)
            if c >= 1:
                br, _, kt, _, hh = chains[c - 1]
                acc_scr[br, hh] = acc_scr[br, hh] + _dot(vt_ref[hh // NSA_HPG, br, kt], p_scr[(c - 1) % 4])

    def tile_running_max(br, qslot, kt, mask_bias=None):
        for g in range(NSA_GROUPS):
            k = kaug_ref[g, br, pl.ds(pl.multiple_of(kt * tk, tk), tk), :]
            for hh in range(g * NSA_HPG, (g + 1) * NSA_HPG):
                s = _dot(k, q_scr[qslot, hh])
                s_scr[hh % NSA_HPG] = s if mask_bias is None else s + mask_bias()
            for hh in range(g * NSA_HPG, (g + 1) * NSA_HPG):
                _flash_step(s_scr.at[hh % NSA_HPG], vt_ref[g, br, kt], m_scr.at[br, hh], acc_scr.at[br, hh])

    prev = jnp.maximum(diag - 1, 0)
    no_prev = jnp.where(diag == 0, NEG, 0.0)
    band_bias = lambda: mb_ref[1] + no_prev
    causal_bias = lambda: mb_ref[0]

    acc_scr[...] = jnp.zeros(acc_scr.shape, F32)

    def unmasked_tiles(first_tile, count, slots):
        for n in range(count):
            set_selection_bias(first_tile + n, slots[n])
        return [(SEL, slots[n], first_tile + n, None) for n in range(count)]

    def tile_quad(j, carry):
        tiles_fixed_reference(unmasked_tiles(4 * j, 4, sel_slots))
        return carry

    lax.fori_loop(0, diag >> 2, tile_quad, 0)

    def tail(leftover):
        tiles = unmasked_tiles(diag - leftover, leftover, sel_slots[1:])
        set_selection_bias(diag, qs)
        tiles_fixed_reference(tiles + [(WIN, qw, prev, band_bias), (SEL, qs, diag, causal_bias),
                                       (WIN, qw, diag, causal_bias)])

    for leftover in range(4):
        pl.when((diag & 3) == leftover)(functools.partial(tail, leftover))

    denominators = acc_scr[:, :, 64:65, :]
    underflow = jnp.logical_not(jnp.min(denominators) > MIN_DENOMINATOR)

    @pl.when(underflow)
    def _():
        m_scr[...] = jnp.full(m_scr.shape, NEG, F32)
        acc_scr[...] = jnp.zeros(acc_scr.shape, F32)

        def full_tile(kt, carry):
            set_selection_bias(kt, qs)
            tile_running_max(SEL, qs, kt)
            return carry

        lax.fori_loop(0, diag, full_tile, 0)
        set_selection_bias(diag, qs)
        tile_running_max(WIN, qw, prev, band_bias)
        tile_running_max(SEL, qs, diag, causal_bias)
        tile_running_max(WIN, qw, diag, causal_bias)

    for hh in range(NSA_HEADS):
        gt = gt_ref[hh // NSA_HPG]
        h4 = hh % NSA_HPG
        o_s = acc_scr[SEL, hh, 0:64, :] * (1.0 / acc_scr[SEL, hh, 64:65, :])
        o_w = acc_scr[WIN, hh, 0:64, :] * (1.0 / acc_scr[WIN, hh, 64:65, :])
        y_ref[hh * 64:(hh + 1) * 64, :] = (gt[3 * h4:3 * h4 + 1] * oc_ref[hh * 64:(hh + 1) * 64, :]
                                          + gt[3 * h4 + 1:3 * h4 + 2] * o_s
                                          + gt[3 * h4 + 2:3 * h4 + 3] * o_w)


def _score_bound(q_gain, k_gain):
    bound = HEAD_DIM * QK_SCALE_LOG2 * jnp.max(jnp.abs(q_gain)) * jnp.max(jnp.abs(k_gain))
    return (1.02 * bound).reshape(1).astype(F32)


def _slc_win(qt, bias, kaug, vt, oc, gt, score_bound):
    b, _, s = qt.shape
    nblk = bias.shape[2]
    ns, tk = vt.shape[3], vt.shape[5]
    tq = tk
    key_rel, t_rel = np.arange(tk)[:, None], np.arange(tq)[None, :]
    mask_bias = jnp.asarray(np.stack([np.where(key_rel <= t_rel, 0.0, NEG),
                                      np.where(t_rel + tk - key_rel < WINDOW, 0.0, NEG)]), F32)
    qblk = pl.BlockSpec((None, NSA_WIDTH, tq), lambda bi, i: (bi, 0, i))
    return pl.pallas_call(
        _slcwin_body, grid=(b, s // tq),
        in_specs=[pl.BlockSpec(memory_space=pltpu.SMEM), qblk,
                  pl.BlockSpec((None, NSA_GROUPS, nblk, tq), lambda bi, i: (bi, 0, 0, i)),
                  pl.BlockSpec((None, NSA_GROUPS, 2, s, 128), lambda bi, i: (bi, 0, 0, 0, 0)),
                  pl.BlockSpec((None, NSA_GROUPS, 2, ns, V_ROWS, tk), lambda bi, i: (bi, 0, 0, 0, 0, 0)),
                  qblk,
                  pl.BlockSpec((None, NSA_GROUPS, GATE_ROWS, tq), lambda bi, i: (bi, 0, 0, i)),
                  _resident((2, tk, tq))],
        out_specs=qblk,
        out_shape=jax.ShapeDtypeStruct((b, 512, s), F32),
        scratch_shapes=[pltpu.VMEM((5, NSA_HEADS, 128, tq), BF16),
                        pltpu.VMEM((2, NSA_HEADS, 1, tq), F32), pltpu.VMEM((2, NSA_HEADS, V_ROWS, tq), F32),
                        pltpu.VMEM((NSA_HPG, tk, tq), F32), pltpu.VMEM((NSA_HPG, tk, tq), BF16)],
        compiler_params=_cparams(("parallel", "arbitrary")), name="nsa_selected_window",
    )(score_bound, qt, bias, kaug, vt, oc, gt.reshape(b, NSA_GROUPS, GATE_ROWS, s), mask_bias)


def _hgrn_consts():
    c = HGRN_CHUNK
    t = np.arange(c)
    lower = (t[None, :] <= t[:, None]).astype(np.float32)
    rows = [lower]
    masks = []
    for half in HGRN_LEVELS:
        mid = (t // (2 * half)) * (2 * half) + half - 1
        if half < 8:
            rows.append(lower[mid])
        same = (t[:, None] // (2 * half)) == (t[None, :] // (2 * half))
        right = (t[:, None] & half) != 0
        left = (t[None, :] & half) == 0
        masks.append((same & right & left).astype(np.float32))
    masks.append(np.eye(c, dtype=np.float32))
    masks.append(((t[:, None] // HGRN_SUB == t[None, :] // HGRN_SUB) & (t[None, :] <= t[:, None])).astype(np.float32))
    mall = np.concatenate(rows, axis=0)
    lvl = np.stack([np.tile(mk.T, (1, HGRN_HEADS)) for mk in masks])
    bdm = np.kron(np.eye(HGRN_HEADS), np.ones((64, 64), np.float32))
    return jnp.asarray(mall, BF16), jnp.asarray(lvl, F32), jnp.asarray(bdm, F32), jnp.asarray(bdm / 64, BF16)


def _hgrn_body(hg_ref, lb_ref, og_ref, mall_ref, lvl_ref, bdm_ref, bdn_ref, y_ref, st_scr, attn_scr):
    c = HGRN_CHUNK
    w = HGRN_WIDTH

    @pl.when(pl.program_id(1) == 0)
    def _():
        st_scr[...] = jnp.zeros(st_scr.shape, F32)

    chunks = range(hg_ref.shape[0] // c)
    lb = lb_ref[...]
    lane = lax.broadcasted_iota(jnp.int32, (c, w), 1)
    head_masks = [(lane >> 6) == hh for hh in range(HGRN_HEADS)]
    nlev = len(HGRN_LEVELS)

    def stack_heads(x):
        x16 = x.astype(BF16)
        return jnp.concatenate([jnp.where(hm, x16, 0) for hm in head_masks], axis=0)

    def row_bcast(x, half):
        return jnp.concatenate([jnp.broadcast_to(x[p + half - 1:p + half, :], (2 * half, w))
                                for p in range(0, c, 2 * half)], axis=0)

    qa, kk, v16, logf = [], [], [], []
    for ci in chunks:
        rows = slice(ci * c, (ci + 1) * c)
        qa.append(_silu(hg_ref[rows, 0:w]) * (HEAD_DIM ** -0.5))
        fg = lb + (1.0 - lb) * _sigmoid(hg_ref[rows, w:2 * w])
        kk.append(1.0 - fg)
        logf.append(jnp.log2(fg))
        v16.append(hg_ref[rows, 2 * w:3 * w].astype(BF16))

    mall = mall_ref[...]
    parts = _split3(jnp.concatenate(logf, axis=1))
    b_all = sum(_dot(mall[0:c], part) for part in parts)
    bcum = [b_all[:, ci * w:(ci + 1) * w] for ci in chunks]

    def level(ci, li, ref_pt):
        e = jnp.exp2(-jnp.abs(bcum[ci] - ref_pt))
        return lvl_ref[li] * _dot_nt((kk[ci] * e).astype(BF16), stack_heads(qa[ci] * e))

    coarse = [li for li, half in enumerate(HGRN_LEVELS) if half >= HGRN_SUB]
    if not coarse:
        attn_scr[...] = jnp.zeros(attn_scr.shape, F32)
    for li in coarse:
        for ci in chunks:
            contribution = level(ci, li, row_bcast(bcum[ci], HGRN_LEVELS[li]))
            attn_scr[ci] = contribution if li == coarse[0] else attn_scr[ci] + contribution

    def block_start(x):
        firsts = [jnp.zeros((HGRN_SUB, w), F32)]
        firsts += [jnp.broadcast_to(x[p - 1:p, :], (HGRN_SUB, w)) for p in range(HGRN_SUB, c, HGRN_SUB)]
        return jnp.concatenate(firsts, axis=0)

    expo = [block_start(bcum[ci]) - bcum[ci] for ci in chunks]
    largest = expo[0]
    for ci in chunks[1:]:
        largest = jnp.maximum(largest, expo[ci])
    single_reference_ok = jnp.max(largest) < HGRN_MAX_EXPONENT

    @pl.when(single_reference_ok)
    def _():
        for ci in chunks:
            kt = (kk[ci] * jnp.exp2(expo[ci])).astype(BF16)
            attn_scr[ci] = attn_scr[ci] + lvl_ref[nlev + 1] * _dot_nt(kt, stack_heads(qa[ci] * jnp.exp2(-expo[ci])))

    @pl.when(jnp.logical_not(single_reference_ok))
    def _():
        r_fine = sum(_dot(mall[c:], part) for part in parts)
        for ci in chunks:
            attn_scr[ci] = attn_scr[ci] + lvl_ref[nlev] * _dot_nt(kk[ci].astype(BF16), stack_heads(qa[ci]))
        fine = 0
        for li, half in enumerate(HGRN_LEVELS):
            if half >= HGRN_SUB:
                continue
            for ci in chunks:
                if half >= 8:
                    ref_pt = row_bcast(bcum[ci], half)
                else:
                    ref_pt = r_fine[fine * c:(fine + 1) * c, ci * w:(ci + 1) * w]
                attn_scr[ci] = attn_scr[ci] + level(ci, li, ref_pt)
            if half < 8:
                fine += 1

    attn = [attn_scr[ci] for ci in chunks]

    intra, upd, decay, qb = [], [], [], []
    for ci in chunks:
        x = _dot_tn(attn[ci].astype(BF16), v16[ci])
        intra.append(sum(jnp.where(head_masks[hh], x[hh * c:(hh + 1) * c], 0.0) for hh in range(HGRN_HEADS)))
        b_last = bcum[ci][c - 1:c, :]
        kl = (kk[ci] * jnp.exp2(b_last - bcum[ci])).astype(BF16)
        upd.append(bdm_ref[...] * _dot_tn(v16[ci], kl))
        decay.append(jnp.exp2(b_last))
        qb.append((qa[ci] * jnp.exp2(bcum[ci])).astype(BF16))

    st = st_scr[...]
    inter = []
    for ci in chunks:
        inter.append(_dot_nt(qb[ci], st.astype(BF16)))
        st = st * decay[ci] + upd[ci]
    st_scr[...] = st

    for ci in chunks:
        rows = slice(ci * c, (ci + 1) * c)
        o = inter[ci] + intra[ci]
        hi, lo = _split2(o * o)
        ms = _dot(hi, bdn_ref[...]) + _dot(lo, bdn_ref[...])
        y_ref[rows, :] = o * lax.rsqrt(ms + EPS) * og_ref[...] * _silu(hg_ref[rows, 3 * w:4 * w])


def _hgrn(hg, lower_bound, out_gain):
    b, s, _ = hg.shape
    rows = HGRN_CB * HGRN_CHUNK
    mall, lvl, bdm, bdn = _hgrn_consts()
    full = lambda shape: pl.BlockSpec(shape, lambda bi, i: (0,) * len(shape))
    return pl.pallas_call(
        _hgrn_body, grid=(b, s // rows),
        in_specs=[pl.BlockSpec((None, rows, 4 * HGRN_WIDTH), lambda bi, i: (bi, i, 0)),
                  full((1, HGRN_WIDTH)), full((1, HGRN_WIDTH)),
                  full(mall.shape), full(lvl.shape), full(bdm.shape), full(bdn.shape)],
        out_specs=pl.BlockSpec((None, rows, HGRN_WIDTH), lambda bi, i: (bi, i, 0)),
        out_shape=jax.ShapeDtypeStruct((b, s, HGRN_WIDTH), F32),
        scratch_shapes=[pltpu.VMEM((HGRN_WIDTH, HGRN_WIDTH), F32),
                        pltpu.VMEM((HGRN_CB, HGRN_CHUNK, HGRN_HEADS * HGRN_CHUNK), F32)],
        compiler_params=_cparams(("parallel", "arbitrary")), name="hgrn2_chunks",
    )(hg, lower_bound.reshape(1, -1), out_gain.reshape(1, -1), mall, lvl, bdm, bdn)


def _memkv_body(mem_ref, mg_ref, wk_ref, wvt_ref, kg_ref, kh_ref, vht_ref):
    m = mem_ref.shape[0]
    mn = _rms_rows(mem_ref[...], mg_ref[...]).astype(BF16)
    k = _dot(mn, wk_ref[...])
    vt = _dot_nt(wvt_ref[...], mn)
    ones_rows = (lax.broadcasted_iota(jnp.int32, (V_ROWS - 64, m), 0) == 0).astype(BF16)
    for hh in range(MEM_HEADS):
        kh_ref[hh] = _rms_rows(k[:, hh * 64:(hh + 1) * 64], kg_ref[...]).astype(BF16)
        vht_ref[hh, 0:64, :] = vt[hh * 64:(hh + 1) * 64].astype(BF16)
        vht_ref[hh, 64:V_ROWS, :] = ones_rows


def _mem_kv(mem, mem_gain, wk, wvt, k_gain):
    b, m, d = mem.shape
    full = lambda shape: pl.BlockSpec(shape, lambda bi: (0,) * len(shape))
    return pl.pallas_call(
        _memkv_body, grid=(b,),
        in_specs=[pl.BlockSpec((None, m, d), lambda bi: (bi, 0, 0)), full((1, d)),
                  full(wk.shape), full(wvt.shape), full((1, 64))],
        out_specs=(pl.BlockSpec((None, MEM_HEADS, m, 64), lambda bi: (bi, 0, 0, 0)),
                   pl.BlockSpec((None, MEM_HEADS, V_ROWS, m), lambda bi: (bi, 0, 0, 0))),
        out_shape=(jax.ShapeDtypeStruct((b, MEM_HEADS, m, 64), BF16),
                   jax.ShapeDtypeStruct((b, MEM_HEADS, V_ROWS, m), BF16)),
        compiler_params=_cparams(("parallel",)), name="memory_kv",
    )(mem, mem_gain.reshape(1, d), wk, wvt, k_gain.reshape(1, 64))


def _out_body(x_ref, ynt_ref, yh_ref, qmt_ref, kh_ref, vht_ref, ng_ref, mg_ref, wo_ref,
              fg_ref, wg_ref, wu_ref, wd_ref, o_ref, a_scr):
    scores = [_dot(kh_ref[hh], qmt_ref[hh * 64:(hh + 1) * 64, :]) for hh in range(MEM_HEADS)]
    nsa = _rms_cols(ynt_ref[...], ng_ref[...]).astype(BF16)
    acc = _dot_tn(nsa, wo_ref[0:NSA_WIDTH, :])
    acc = acc + _dot(yh_ref[...].astype(BF16), wo_ref[NSA_WIDTH:NSA_WIDTH + HGRN_WIDTH, :])
    y_mem = []
    for s in scores:
        hh = len(y_mem)
        p = jnp.exp2(s - jnp.max(s, axis=0, keepdims=True))
        o = _dot(vht_ref[hh], p.astype(BF16))
        y_mem.append(o[0:64] * (1.0 / o[64:65]))
    mem = _rms_cols(jnp.concatenate(y_mem, axis=0), mg_ref[...]).astype(BF16)
    acc = acc + _dot_tn(mem, wo_ref[NSA_WIDTH + HGRN_WIDTH:, :])
    o_ref[...] = _ffn_half_step(x_ref[...] + acc, fg_ref, wg_ref, wu_ref, wd_ref, a_scr)


def _out_ffn(x3d, ynt, yh, qmt, kh, vht, nsa_gain, mem_gain, wo, ffn_gain, wg, wu, wd, *, tm=TOK_TILE):
    b, s, d = x3d.shape
    m = kh.shape[2]
    return pl.pallas_call(
        _out_body, grid=(b, s // tm),
        in_specs=[pl.BlockSpec((None, tm, d), lambda bi, i: (bi, i, 0)),
                  pl.BlockSpec((None, NSA_WIDTH, tm), lambda bi, i: (bi, 0, i)),
                  pl.BlockSpec((None, tm, HGRN_WIDTH), lambda bi, i: (bi, i, 0)),
                  pl.BlockSpec((None, MEM_WIDTH, tm), lambda bi, i: (bi, 0, i)),
                  pl.BlockSpec((None, MEM_HEADS, m, 64), lambda bi, i: (bi, 0, 0, 0)),
                  pl.BlockSpec((None, MEM_HEADS, V_ROWS, m), lambda bi, i: (bi, 0, 0, 0)),
                  _resident((NSA_WIDTH, 1)), _resident((MEM_WIDTH, 1)), _resident(wo.shape),
                  _resident((1, d)), _resident(wg.shape), _resident(wu.shape), _resident(wd.shape)],
        out_specs=pl.BlockSpec((None, tm, d), lambda bi, i: (bi, i, 0)),
        out_shape=jax.ShapeDtypeStruct((b, s, d), F32),
        scratch_shapes=[pltpu.VMEM((tm, wg.shape[1]), BF16)],
        compiler_params=_cparams(("parallel", "parallel")), name="mix_out_ffn2",
    )(x3d, ynt, yh, qmt, kh, vht, nsa_gain.reshape(-1, 1), mem_gain.reshape(-1, 1), wo,
      ffn_gain.reshape(1, d), wg, wu, wd)


def _layer(x, mem, ffn1, ffn2, mix_norm, w_in, w_out, nsa_q_norm, nsa_k_norm, cmp_pos_k, cmp_w1_k, cmp_w2_k,
           cmp_pos_v, cmp_w1_v, cmp_w2_v, nsa_out_norm, lower_bound, hgrn_out_norm,
           mem_norm, mem_w_k, mem_w_v, mem_q_norm, mem_k_norm, mem_out_norm):
    b, s, d = x.shape
    sizes = (512, 128, 128, 128, 128, 128, 128, 24, 256, 256, 256, 256, 256)
    offs = np.concatenate([[0], np.cumsum(sizes)])
    w16 = w_in.astype(BF16)
    col = lambda i: w16[:, offs[i]:offs[i + 1]]
    (q_a, k_c, v_c, k_s, v_s, k_w, v_w, g_a, q_h, f_h, i_h, g_h, q_m) = [col(i) for i in range(13)]
    gpad = jnp.zeros((d, GATE_ROWS - 3 * NSA_HPG), BF16)
    wt = jnp.concatenate([q_a, v_s, v_w, g_a[:, :3 * NSA_HPG], gpad, g_a[:, 3 * NSA_HPG:], gpad, q_m], axis=1).T
    wn = jnp.concatenate([k_c, v_c, k_s, k_w, q_h, f_h, i_h, g_h], axis=1)

    x1, qt, vt, gt, qmt, kaug, kvc, hg = _ffn_proj(
        x, *ffn1, mix_norm, wt, wn, nsa_q_norm, nsa_k_norm, mem_q_norm, _rope_tables(s))

    cmp_pos, cmp_w1 = _compress_weights(cmp_pos_k, cmp_pos_v, cmp_w1_k, cmp_w1_v)
    kc, vct = _compress(kvc, cmp_pos, cmp_w1, cmp_w2_k.astype(BF16), cmp_w2_v.T.astype(BF16), nsa_k_norm)
    score_bound = _score_bound(nsa_q_norm, nsa_k_norm)
    oc, bias = _cmp_select(qt, kc, vct, score_bound)
    y_nsa = _slc_win(qt, bias, kaug, vt, oc, gt, score_bound)

    y_hgrn = _hgrn(hg, lower_bound, hgrn_out_norm)

    kh, vht = _mem_kv(mem, mem_norm, mem_w_k.astype(BF16), mem_w_v.T.astype(BF16), mem_k_norm)
    return _out_ffn(x1, y_nsa, y_hgrn, qmt, kh, vht, nsa_out_norm, mem_out_norm, w_out.astype(BF16), *ffn2)


def kernel(x, mem, ffn1_norm, ffn1_w_gate, ffn1_w_up, ffn1_w_down, mix_norm, w_in, w_out, nsa_q_norm, nsa_k_norm, cmp_pos_k, cmp_w1_k, cmp_w2_k, cmp_pos_v, cmp_w1_v, cmp_w2_v, nsa_out_norm, hgrn_lb_logits, hgrn_out_norm, mem_norm, mem_w_k, mem_w_v, mem_q_norm, mem_k_norm, mem_out_norm, ffn2_norm, ffn2_w_gate, ffn2_w_up, ffn2_w_down):
    b, s, d = x.shape
    depth = ffn1_norm.shape[0]
    lower_bounds = jnp.cumsum(jax.nn.softmax(hgrn_lb_logits.astype(F32), axis=0), axis=0)
    bf = lambda a: a.astype(BF16)
    for l in range(depth):
        x = _layer(x, mem, (ffn1_norm[l], bf(ffn1_w_gate[l]), bf(ffn1_w_up[l]), bf(ffn1_w_down[l])),
                   (ffn2_norm[l], bf(ffn2_w_gate[l]), bf(ffn2_w_up[l]), bf(ffn2_w_down[l])),
                   mix_norm[l], w_in[l], w_out[l], nsa_q_norm[l], nsa_k_norm[l],
                   cmp_pos_k[l], cmp_w1_k[l], cmp_w2_k[l], cmp_pos_v[l], cmp_w1_v[l], cmp_w2_v[l],
                   nsa_out_norm[l], lower_bounds[l], hgrn_out_norm[l],
                   mem_norm[l], mem_w_k[l], mem_w_v[l], mem_q_norm[l], mem_k_norm[l], mem_out_norm[l])
    return x
```
